```python
import math
import jax, jax.numpy as jnp
from jax import lax
import numpy as np

D_MODEL = 1024
BATCH = 8
SEQ = 2048
DEPTH = 4
DEC_BATCH = 8
DEC_SEQ = 16
PAST_LEN = 2048

CHUNK = 64
N_MIXERS = 4
N_A = (DEPTH + 3) // 4
N_B = (DEPTH + 2) // 4
N_C = (DEPTH + 1) // 4
N_D = DEPTH // 4
EPS = 1e-6
NEG_INF = -1e30
Q_BLOCK = 128

CONV_W = 3
B_HEADS = 8
B_KV = 2
B_REP = B_HEADS // B_KV
B_HD = D_MODEL // B_HEADS
IDX_HEADS = 8
IDX_DIM = 64
TOPK_MAX = 256
B_Q = B_HEADS * B_HD
B_KVW = B_KV * B_HD
B_IDXQ = IDX_HEADS * IDX_DIM
B_IN = B_Q + 2 * B_KVW + B_IDXQ + IDX_DIM + IDX_HEADS
B_SPLITS = (B_Q, B_Q + B_KVW, B_Q + 2 * B_KVW, B_Q + 2 * B_KVW + B_IDXQ, B_Q + 2 * B_KVW + B_IDXQ + IDX_DIM)
IDX_SCALE = (IDX_DIM * IDX_HEADS) ** -0.5
N_BUCKETS = 32
MAX_DIST = 128
POOL_WINDOWS = (2, 4, 8, 16)
POOL_GROUPS = 4
POOL_GW = D_MODEL // POOL_GROUPS
POOL_STATE = max(POOL_WINDOWS) - 1
D_HEADS = 8
D_HD = D_MODEL // D_HEADS
D_IN = 3 * D_MODEL + D_HEADS
D_SPLITS = (D_MODEL, 2 * D_MODEL, 3 * D_MODEL)
FORGET_BIAS = 2.0
N_MEM = 256
MEM_HEADS = 4
MEM_HD = D_MODEL // MEM_HEADS
D_FF = 4 * D_MODEL

kernel_name = 'hybrid_streaming_encoder_step'


def rmsnorm(x, g):
    xf = x.astype(jnp.float32)
    y = xf * lax.rsqrt(jnp.mean(xf * xf, axis=-1, keepdims=True) + EPS)
    return (y * g.astype(jnp.float32)).astype(x.dtype)


def t5_bucket(rel):
    nb = N_BUCKETS // 2
    max_exact = nb // 2
    ret = (rel > 0).astype(jnp.int32) * nb
    n = jnp.abs(rel)
    nf = jnp.maximum(n, 1).astype(jnp.float32)
    large = max_exact + (jnp.log(nf / max_exact) / math.log(MAX_DIST / max_exact) * (nb - max_exact)).astype(jnp.int32)
    large = jnp.minimum(large, nb - 1)
    return ret + jnp.where(n < max_exact, n, large)


def short_conv_mixer(h, conv_state, w_in, w_conv, w_out):
    t_len = h.shape[1]
    b_gate, c_gate, u = jnp.split(h @ w_in, 3, axis=-1)
    z = c_gate * u
    zp = jnp.concatenate([conv_state.astype(z.dtype), z], axis=1)
    conv = sum(zp[:, k:k + t_len] * w_conv[k] for k in range(CONV_W))
    y = (b_gate * conv) @ w_out
    return y, zp[:, -(CONV_W - 1):]


def pool_mixer(h, pool_state, pos0, w_group, scale):
    bsz, t_len, _ = h.shape
    p_len = POOL_STATE
    zp = jnp.concatenate([pool_state.astype(h.dtype), h], axis=1)
    zf = zp.astype(jnp.float32)
    cs = jnp.concatenate([jnp.zeros((bsz, 1, D_MODEL), jnp.float32), jnp.cumsum(zf, axis=1)], axis=1)
    pos = pos0 + jnp.arange(t_len)
    means = []
    for g, w in enumerate(POOL_WINDOWS):
        sl = slice(g * POOL_GW, (g + 1) * POOL_GW)
        win_sum = cs[:, p_len + 1:p_len + t_len + 1, sl] - cs[:, p_len + 1 - w:p_len + t_len + 1 - w, sl]
        count = jnp.minimum(w, pos + 1).astype(jnp.float32)[None, :, None]
        means.append(win_sum / count)
    pooled = jnp.concatenate(means, axis=-1)
    d = (pooled - h.astype(jnp.float32)).astype(h.dtype).reshape(bsz, t_len, POOL_GROUPS, POOL_GW)
    y = jnp.einsum('btgc,gcd->btgd', d, w_group).reshape(bsz, t_len, D_MODEL) * scale
    return y, zp[:, -p_len:]


def dsa_mixer(h, k_past, v_past, ki_past, w_in, w_out, rel_bias):
    bsz, t_len, _ = h.shape
    p_len = k_past.shape[1]
    q, k, v, qi, ki, wi = jnp.split(h @ w_in, B_SPLITS, axis=-1)
    q = q.reshape(bsz, t_len, B_KV, B_REP, B_HD)
    k = k.reshape(bsz, t_len, B_KV, B_HD)
    v = v.reshape(bsz, t_len, B_KV, B_HD)
    qi = qi.reshape(bsz, t_len, IDX_HEADS, IDX_DIM)
    keys = jnp.concatenate([k_past.astype(k.dtype), k], axis=1)
    vals = jnp.concatenate([v_past.astype(v.dtype), v], axis=1)
    kidx = jnp.concatenate([ki_past.astype(ki.dtype), ki], axis=1)
    n_keys = p_len + t_len
    top_k = min(TOPK_MAX, n_keys // 4)
    key_chunk = jnp.arange(n_keys) // CHUNK
    q_blk = min(t_len, Q_BLOCK)
    n_blk = t_len // q_blk
    qpos = (p_len + jnp.arange(t_len)).reshape(n_blk, q_blk)

    def to_blocks(a):
        return jnp.moveaxis(a.reshape((bsz, n_blk, q_blk) + a.shape[2:]), 1, 0)

    def attend(args):
        q_b, qi_b, wi_b, pos_b = args
        s = jnp.einsum('bqhd,bkd->bqhk', qi_b, kidx, preferred_element_type=jnp.float32)
        score = jnp.einsum('bqhk,bqh->bqk', jax.nn.relu(s), wi_b.astype(jnp.float32)) * IDX_SCALE
        q_chunk = pos_b // CHUNK
        admissible = key_chunk[None, :] <= q_chunk[:, None]
        score = jnp.where(admissible[None], score, NEG_INF)
        _, sel = lax.top_k(score, top_k)
        k_sel = jax.vmap(lambda kk, ii: kk[ii])(keys, sel)
        v_sel = jax.vmap(lambda vv, ii: vv[ii])(vals, sel)
        valid = (sel // CHUNK) <= q_chunk[None, :, None]
        logits = jnp.einsum('bqgrd,bqkgd->bqgrk', q_b, k_sel, preferred_element_type=jnp.float32) * (B_HD ** -0.5)
        bias = rel_bias[t5_bucket(sel - pos_b[None, :, None])]
        bias = jnp.moveaxis(bias.reshape(bsz, q_blk, top_k, B_KV, B_REP), 2, -1)
        logits = jnp.where(valid[:, :, None, None, :], logits + bias.astype(jnp.float32), NEG_INF)
        probs = jax.nn.softmax(logits, axis=-1)
        return jnp.einsum('bqgrk,bqkgd->bqgrd', probs.astype(v_sel.dtype), v_sel)

    out = lax.map(attend, (to_blocks(q), to_blocks(qi), to_blocks(wi), qpos))
    out = jnp.moveaxis(out, 0, 1).reshape(bsz, t_len, D_MODEL)
    return out @ w_out, k, v, ki


def fox_mixer(h, k_past, v_past, lf_past, w_in, b_f, w_out):
    bsz, t_len, _ = h.shape
    p_len = k_past.shape[1]
    q, k, v, fg = jnp.split(h @ w_in, D_SPLITS, axis=-1)
    q = q.reshape(bsz, t_len, D_HEADS, D_HD)
    k = k.reshape(bsz, t_len, D_HEADS, D_HD)
    v = v.reshape(bsz, t_len, D_HEADS, D_HD)
    logf = jax.nn.log_sigmoid(fg.astype(jnp.float32) + b_f.astype(jnp.float32))
    keys = jnp.concatenate([k_past.astype(k.dtype), k], axis=1)
    vals = jnp.concatenate([v_past.astype(v.dtype), v], axis=1)
    cum = jnp.cumsum(jnp.concatenate([lf_past.astype(jnp.float32), logf], axis=1), axis=1)
    cum_k = jnp.moveaxis(cum, 2, 1)
    n_keys = p_len + t_len
    key_pos = jnp.arange(n_keys)
    q_blk = min(t_len, Q_BLOCK)
    n_blk = t_len // q_blk
    qpos = (p_len + jnp.arange(t_len)).reshape(n_blk, q_blk)

    def to_blocks(a):
        return jnp.moveaxis(a.reshape((bsz, n_blk, q_blk) + a.shape[2:]), 1, 0)

    def attend(args):
        q_b, cq_b, pos_b = args
        logits = jnp.einsum('bqhd,bkhd->bhqk', q_b, keys, preferred_element_type=jnp.float32) * (D_HD ** -0.5)
        decay = jnp.moveaxis(cq_b, 2, 1)[..., None] - cum_k[:, :, None, :]
        causal = key_pos[None, :] <= pos_b[:, None]
        logits = jnp.where(causal[None, None], logits + decay, NEG_INF)
        probs = jax.nn.softmax(logits, axis=-1)
        return jnp.einsum('bhqk,bkhd->bqhd', probs.astype(vals.dtype), vals)

    out = lax.map(attend, (to_blocks(q), to_blocks(cum[:, p_len:]), qpos))
    out = jnp.moveaxis(out, 0, 1).reshape(bsz, t_len, D_MODEL)
    return out @ w_out, k, v, logf


def memory_kv(mem, g_mem, w_kv):
    mf = mem.astype(jnp.float32)
    mn = mf * lax.rsqrt(jnp.mean(mf * mf, axis=-1, keepdims=True) + EPS)
    m_l = (mn[None] * g_mem[:, None, None, :].astype(jnp.float32)).astype(mem.dtype)
    kv = jnp.einsum('lbmd,lde->lbme', m_l, w_kv)
    k, v = jnp.split(kv, 2, axis=-1)
    shp = k.shape[:3] + (MEM_HEADS, MEM_HD)
    return k.reshape(shp), v.reshape(shp)


def cross_attn(h, mk, mv, w_q, w_o):
    bsz, t_len, _ = h.shape
    q = (h @ w_q).reshape(bsz, t_len, MEM_HEADS, MEM_HD)
    logits = jnp.einsum('bthd,bmhd->bhtm', q, mk, preferred_element_type=jnp.float32) * (MEM_HD ** -0.5)
    probs = jax.nn.softmax(logits, axis=-1)
    o = jnp.einsum('bhtm,bmhd->bthd', probs.astype(mv.dtype), mv)
    return o.reshape(bsz, t_len, D_MODEL) @ w_o


def sq_relu_mlp(h, w1, w2):
    u = jax.nn.relu(h @ w1)
    return (u * u) @ w2


def run_group(x, pos0, a_st, b_k, b_v, b_ki, c_st, d_k, d_v, d_lf, mem_k, mem_v, prm):
    new_a, new_bk, new_bv, new_bki, new_c, new_dk, new_dv, new_dlf = [], [], [], [], [], [], [], []
    for i in range(DEPTH):
        kind, j = i % N_MIXERS, i // N_MIXERS
        h = rmsnorm(x, prm['norm_mix'][i])
        if kind == 0:
            y, st = short_conv_mixer(h, a_st[j], prm['a_w_in'][j], prm['a_conv'][j], prm['a_w_out'][j])
            new_a.append(st)
        elif kind == 1:
            y, kk, vv, ki = dsa_mixer(h, b_k[j], b_v[j], b_ki[j], prm['b_w_in'][j], prm['b_w_out'][j], prm['rel_bias'])
            new_bk.append(kk)
            new_bv.append(vv)
            new_bki.append(ki)
        elif kind == 2:
            y, st = pool_mixer(h, c_st[j], pos0, prm['c_w_group'][j], prm['c_scale'][j])
            new_c.append(st)
        else:
            y, kk, vv, lf = fox_mixer(h, d_k[j], d_v[j], d_lf[j], prm['d_w_in'][j], prm['d_b_f'][j], prm['d_w_out'][j])
            new_dk.append(kk)
            new_dv.append(vv)
            new_dlf.append(lf)
        x = x + y.astype(x.dtype)
        h = rmsnorm(x, prm['norm_xattn'][i])
        x = x + cross_attn(h, mem_k[i], mem_v[i], prm['xa_wq'][i], prm['xa_wo'][i]).astype(x.dtype)
        h = rmsnorm(x, prm['norm_ffn'][i])
        x = x + sq_relu_mlp(h, prm['ffn_w1'][i], prm['ffn_w2'][i]).astype(x.dtype)
    y = rmsnorm(x, prm['final_norm'])
    return (y, jnp.stack(new_a), jnp.stack(new_bk), jnp.stack(new_bv), jnp.stack(new_bki), jnp.stack(new_c),
            jnp.stack(new_dk), jnp.stack(new_dv), jnp.stack(new_dlf))


def setup_inputs(seed: int = 0) -> dict:
    key = jax.random.key(seed)
    keys = iter(jax.random.split(key, 48))
    D = D_MODEL

    def nrm(shape, scale=1.0):
        return jax.random.normal(next(keys), shape, jnp.float32) * scale

    def gain(shape):
        return 1.0 + 0.05 * nrm(shape)

    return {
        'x_prompt': nrm((BATCH, SEQ, D)),
        'x_sample': nrm((DEC_BATCH, DEC_SEQ, D)),
        'state_a_conv': nrm((N_A, DEC_BATCH, CONV_W - 1, D)),
        'cache_b_k': nrm((N_B, DEC_BATCH, PAST_LEN, B_KV, B_HD)),
        'cache_b_v': nrm((N_B, DEC_BATCH, PAST_LEN, B_KV, B_HD)),
        'cache_b_kidx': nrm((N_B, DEC_BATCH, PAST_LEN, IDX_DIM)),
        'state_c_pool': nrm((N_C, DEC_BATCH, POOL_STATE, D)),
        'cache_d_k': nrm((N_D, DEC_BATCH, PAST_LEN, D_HEADS, D_HD)),
        'cache_d_v': nrm((N_D, DEC_BATCH, PAST_LEN, D_HEADS, D_HD)),
        'cache_d_logf': jax.nn.log_sigmoid(FORGET_BIAS + nrm((N_D, DEC_BATCH, PAST_LEN, D_HEADS))),
        'cache_mem_k': nrm((DEPTH, DEC_BATCH, N_MEM, MEM_HEADS, MEM_HD)),
        'cache_mem_v': nrm((DEPTH, DEC_BATCH, N_MEM, MEM_HEADS, MEM_HD)),
        'mem_prompt': nrm((BATCH, N_MEM, D)),
        'norm_mix': gain((DEPTH, D)),
        'norm_xattn': gain((DEPTH, D)),
        'norm_mem': gain((DEPTH, D)),
        'norm_ffn': gain((DEPTH, D)),
        'final_norm': gain((D,)),
        'a_w_in': nrm((N_A, D, 3 * D), D ** -0.5),
        'a_conv': nrm((N_A, CONV_W, D), CONV_W ** -0.5),
        'a_w_out': nrm((N_A, D, D), D ** -0.5),
        'b_w_in': nrm((N_B, D, B_IN), D ** -0.5),
        'b_w_out': nrm((N_B, D, D), D ** -0.5),
        'rel_bias': nrm((N_BUCKETS, B_HEADS), 0.5),
        'c_w_group': nrm((N_C, POOL_GROUPS, POOL_GW, POOL_GW), POOL_GW ** -0.5),
        'c_scale': gain((N_C, D)),
        'd_w_in': nrm((N_D, D, D_IN), D ** -0.5),
        'd_b_f': FORGET_BIAS + 0.5 * nrm((N_D, D_HEADS)),
        'd_w_out': nrm((N_D, D, D), D ** -0.5),
        'xa_wq': nrm((DEPTH, D, D), D ** -0.5),
        'xa_wkv': nrm((DEPTH, D, 2 * D), D ** -0.5),
        'xa_wo': nrm((DEPTH, D, D), D ** -0.5),
        'ffn_w1': nrm((DEPTH, D, D_FF), D ** -0.5),
        'ffn_w2': nrm((DEPTH, D_FF, D), D_FF ** -0.5),
    }


def reference(x_prompt, x_sample, state_a_conv, cache_b_k, cache_b_v, cache_b_kidx, state_c_pool,
              cache_d_k, cache_d_v, cache_d_logf, cache_mem_k, cache_mem_v, mem_prompt,
              norm_mix, norm_xattn, norm_mem, norm_ffn, final_norm,
              a_w_in, a_conv, a_w_out, b_w_in, b_w_out, rel_bias, c_w_group, c_scale,
              d_w_in, d_b_f, d_w_out, xa_wq, xa_wkv, xa_wo, ffn_w1, ffn_w2):
    prm = {'norm_mix': norm_mix, 'norm_xattn': norm_xattn, 'norm_ffn': norm_ffn, 'final_norm': final_norm,
           'a_w_in': a_w_in, 'a_conv': a_conv, 'a_w_out': a_w_out,
           'b_w_in': b_w_in, 'b_w_out': b_w_out, 'rel_bias': rel_bias,
           'c_w_group': c_w_group, 'c_scale': c_scale,
           'd_w_in': d_w_in, 'd_b_f': d_b_f, 'd_w_out': d_w_out,
           'xa_wq': xa_wq, 'xa_wo': xa_wo, 'ffn_w1': ffn_w1, 'ffn_w2': ffn_w2}
    bp, dt = x_prompt.shape[0], x_prompt.dtype

    mem_k_p, mem_v_p = memory_kv(mem_prompt, norm_mem, xa_wkv)
    (y_prompt, a_conv_p, b_k_p, b_v_p, b_kidx_p, c_pool_p, d_k_p, d_v_p, d_logf_p) = run_group(
        x_prompt, 0,
        jnp.zeros((N_A, bp, CONV_W - 1, D_MODEL), dt),
        jnp.zeros((N_B, bp, 0, B_KV, B_HD), dt), jnp.zeros((N_B, bp, 0, B_KV, B_HD), dt),
        jnp.zeros((N_B, bp, 0, IDX_DIM), dt),
        jnp.zeros((N_C, bp, POOL_STATE, D_MODEL), dt),
        jnp.zeros((N_D, bp, 0, D_HEADS, D_HD), dt), jnp.zeros((N_D, bp, 0, D_HEADS, D_HD), dt),
        jnp.zeros((N_D, bp, 0, D_HEADS), jnp.float32),
        mem_k_p, mem_v_p, prm)

    past_len = cache_b_k.shape[2]
    (y_sample, a_conv_s, b_k_s, b_v_s, b_kidx_s, c_pool_s, d_k_s, d_v_s, d_logf_s) = run_group(
        x_sample, past_len, state_a_conv, cache_b_k, cache_b_v, cache_b_kidx, state_c_pool,
        cache_d_k, cache_d_v, cache_d_logf, cache_mem_k, cache_mem_v, prm)

    return (y_prompt, y_sample, a_conv_p, a_conv_s, b_k_p, b_v_p, b_kidx_p, b_k_s, b_v_s, b_kidx_s,
            c_pool_p, c_pool_s, d_k_p, d_v_p, d_logf_p, d_k_s, d_v_s, d_logf_s, mem_k_p, mem_v_p)
```

```python
import functools
import math

import jax
import jax.numpy as jnp
from jax import lax
from jax.experimental import pallas as pl
from jax.experimental.pallas import tpu as pltpu

F32 = jnp.float32
BF16 = jnp.bfloat16
I32 = jnp.int32

EPS = 1e-6
NEG_INF = -1e30
CHUNK = 64
LANE = 128
VMEM_LIMIT = 48 * 1024 * 1024

CONV_W = 3
POOL_WINDOWS = (2, 4, 8, 16)
POOL_STATE = max(POOL_WINDOWS) - 1
B_HEADS, B_KV, B_HD = 8, 2, 128
B_REP = B_HEADS // B_KV
IDX_HEADS, IDX_DIM = 8, 64
TOPK_MAX = 256
N_BUCKETS, MAX_DIST = 32, 128
D_HEADS, D_HD = 8, 128
MEM_HEADS = 4
INT_MIN = -2147483648
BIAS_CENTER = 2 * LANE


def _cparams(*sem):
    return pltpu.CompilerParams(dimension_semantics=sem, vmem_limit_bytes=VMEM_LIMIT)


def _dot(a, b):
    return jnp.dot(a.astype(BF16), b.astype(BF16), preferred_element_type=F32)


def _dot_nt(a, b):
    return lax.dot_general(a.astype(BF16), b.astype(BF16), (((1,), (1,)), ((), ())),
                           preferred_element_type=F32)


def _dot_tn(a, b):
    return lax.dot_general(a.astype(BF16), b.astype(BF16), (((0,), (0,)), ((), ())),
                           preferred_element_type=F32)


def _rms(x, g):
    return x * lax.rsqrt(jnp.mean(x * x, axis=-1, keepdims=True) + EPS) * g


def _row_tile(n, cap):
    t = min(n, cap)
    assert n % t == 0
    return t


def _memkv_kernel(mem_ref, g_ref, w_ref, k_ref, v_ref):
    m = mem_ref[...]
    mn = m * lax.rsqrt(jnp.mean(m * m, axis=-1, keepdims=True) + EPS)
    h = (mn * g_ref[0]).astype(BF16)
    d = k_ref.shape[-1]
    k_ref[0] = jnp.dot(h, w_ref[0, :, :d], preferred_element_type=F32)
    v_ref[0] = jnp.dot(h, w_ref[0, :, d:], preferred_element_type=F32)


def _memory_kv(mem, g_mem, w_kv):
    depth, d = g_mem.shape
    n = mem.shape[0]
    tm = _row_tile(n, 512)
    out = jax.ShapeDtypeStruct((depth, n, d), F32)
    return pl.pallas_call(
        _memkv_kernel,
        grid=(depth, n // tm),
        in_specs=[pl.BlockSpec((tm, d), lambda l, i: (i, 0)),
                  pl.BlockSpec((1, 1, d), lambda l, i: (l, 0, 0)),
                  pl.BlockSpec((1, d, 2 * d), lambda l, i: (l, 0, 0))],
        out_specs=[pl.BlockSpec((1, tm, d), lambda l, i: (l, i, 0)),
                   pl.BlockSpec((1, tm, d), lambda l, i: (l, i, 0))],
        out_shape=[out, out],
        compiler_params=_cparams("parallel", "parallel"),
        name="memory_kv",
    )(mem, g_mem.reshape(depth, 1, d), w_kv)


def _ffn_kernel(x_ref, g_ref, w1_ref, w2_ref, gf_ref, o_ref, h_ref, acc_ref, *, final_norm):
    j = pl.program_id(1)

    @pl.when(j == 0)
    def _():
        h_ref[...] = _rms(x_ref[...], g_ref[...]).astype(BF16)
        acc_ref[...] = jnp.zeros_like(acc_ref)

    u = jnp.maximum(jnp.dot(h_ref[...], w1_ref[...], preferred_element_type=F32), 0.0)
    acc_ref[...] += jnp.dot((u * u).astype(BF16), w2_ref[...], preferred_element_type=F32)

    @pl.when(j == pl.num_programs(1) - 1)
    def _():
        y = x_ref[...] + acc_ref[...]
        o_ref[...] = _rms(y, gf_ref[...]) if final_norm else y


def _ffn(x, g, w1, w2, gf, final_norm):
    n, d = x.shape
    f = w1.shape[1]
    tm = _row_tile(n, 1024)
    tf = 512
    return pl.pallas_call(
        functools.partial(_ffn_kernel, final_norm=final_norm),
        grid=(n // tm, f // tf),
        in_specs=[pl.BlockSpec((tm, d), lambda i, j: (i, 0)),
                  pl.BlockSpec((1, d), lambda i, j: (0, 0)),
                  pl.BlockSpec((d, tf), lambda i, j: (0, j)),
                  pl.BlockSpec((tf, d), lambda i, j: (j, 0)),
                  pl.BlockSpec((1, d), lambda i, j: (0, 0))],
        out_specs=pl.BlockSpec((tm, d), lambda i, j: (i, 0)),
        out_shape=jax.ShapeDtypeStruct((n, d), F32),
        scratch_shapes=[pltpu.VMEM((tm, d), BF16), pltpu.VMEM((tm, d), F32)],
        compiler_params=_cparams("parallel", "arbitrary"),
        name="ffn",
    )(x, g.reshape(1, d), w1, w2, gf.reshape(1, d))


def _xattn_kernel(x_ref, g_ref, wq_ref, mk_ref, mv_ref, wo_ref, o_ref):
    x = x_ref[0]
    d = x.shape[-1]
    hd = d // MEM_HEADS
    h = _rms(x, g_ref[...]).astype(BF16)
    q = jnp.dot(h, wq_ref[...], preferred_element_type=F32)
    outs = []
    for hh in range(MEM_HEADS):
        sl = slice(hh * hd, (hh + 1) * hd)
        s = _dot_nt(q[:, sl], mk_ref[0, :, sl]) * (hd ** -0.5)
        m = jnp.max(s, axis=-1, keepdims=True)
        p = jnp.exp(s - m)
        l = jnp.sum(p, axis=-1, keepdims=True)
        outs.append(_dot(p, mv_ref[0, :, sl]) / l)
    o = jnp.concatenate(outs, axis=-1)
    o_ref[0] = x + _dot(o, wo_ref[...])


def _xattn(x, g, wq, mk, mv, wo):
    b, t, d = x.shape
    nm = mk.shape[1]
    tm = _row_tile(t, 512)
    return pl.pallas_call(
        _xattn_kernel,
        grid=(b, t // tm),
        in_specs=[pl.BlockSpec((1, tm, d), lambda i, j: (i, j, 0)),
                  pl.BlockSpec((1, d), lambda i, j: (0, 0)),
                  pl.BlockSpec((d, d), lambda i, j: (0, 0)),
                  pl.BlockSpec((1, nm, d), lambda i, j: (i, 0, 0)),
                  pl.BlockSpec((1, nm, d), lambda i, j: (i, 0, 0)),
                  pl.BlockSpec((d, d), lambda i, j: (0, 0))],
        out_specs=pl.BlockSpec((1, tm, d), lambda i, j: (i, j, 0)),
        out_shape=jax.ShapeDtypeStruct((b, t, d), F32),
        compiler_params=_cparams("parallel", "parallel"),
        name="xattn",
    )(x, g.reshape(1, d), wq, mk, mv, wo)


def _conv_kernel(x_ref, g_ref, win_ref, wc_ref, st_ref, wout_ref, o_ref, nst_ref, z_ref):
    t = pl.program_id(1)
    x = x_ref[0]
    tm, d = x.shape
    pad = 8

    @pl.when(t == 0)
    def _():
        z_ref[pad - 2:pad, :] = st_ref[0]

    h = _rms(x, g_ref[...]).astype(BF16)
    bg = jnp.dot(h, win_ref[:, 0:d], preferred_element_type=F32)
    cg = jnp.dot(h, win_ref[:, d:2 * d], preferred_element_type=F32)
    u = jnp.dot(h, win_ref[:, 2 * d:3 * d], preferred_element_type=F32)
    z = cg * u
    z_ref[pad:pad + tm, :] = z
    conv = (z_ref[pad - 2:pad - 2 + tm, :] * wc_ref[0:1, :]
            + z_ref[pad - 1:pad - 1 + tm, :] * wc_ref[1:2, :]
            + z * wc_ref[2:3, :])
    o_ref[0] = x + _dot(bg * conv, wout_ref[...])
    last = z_ref[pad + tm - 2:pad + tm, :]
    z_ref[pad - 2:pad, :] = last

    @pl.when(t == pl.num_programs(1) - 1)
    def _():
        nst_ref[0] = last


def _conv_mixer(x, g, w_in, w_conv, state, w_out):
    b, t, d = x.shape
    tm = _row_tile(t, 512)
    return pl.pallas_call(
        _conv_kernel,
        grid=(b, t // tm),
        in_specs=[pl.BlockSpec((1, tm, d), lambda i, j: (i, j, 0)),
                  pl.BlockSpec((1, d), lambda i, j: (0, 0)),
                  pl.BlockSpec((d, 3 * d), lambda i, j: (0, 0)),
                  pl.BlockSpec((CONV_W, d), lambda i, j: (0, 0)),
                  pl.BlockSpec((1, CONV_W - 1, d), lambda i, j: (i, 0, 0)),
                  pl.BlockSpec((d, d), lambda i, j: (0, 0))],
        out_specs=[pl.BlockSpec((1, tm, d), lambda i, j: (i, j, 0)),
                   pl.BlockSpec((1, CONV_W - 1, d), lambda i, j: (i, 0, 0))],
        out_shape=[jax.ShapeDtypeStruct((b, t, d), F32),
                   jax.ShapeDtypeStruct((b, CONV_W - 1, d), F32)],
        scratch_shapes=[pltpu.VMEM((tm + 8, d), F32)],
        compiler_params=_cparams("parallel", "arbitrary"),
        name="conv_mixer",
    )(x, g.reshape(1, d), w_in, w_conv, state, w_out)


def _pool_kernel(x_ref, g_ref, st_ref, wg_ref, sc_ref, o_ref, nst_ref, h_ref, *, pos0):
    t = pl.program_id(1)
    x = x_ref[0]
    tm, d = x.shape
    gw = d // len(POOL_WINDOWS)
    base = POOL_STATE + 1

    @pl.when(t == 0)
    def _():
        h_ref[1:base, :] = st_ref[0]

    h = _rms(x, g_ref[...])
    h_ref[base:base + tm, :] = h
    pos = pos0 + t * tm + lax.broadcasted_iota(I32, (tm, gw), 0)
    ys = []
    for gi, w in enumerate(POOL_WINDOWS):
        sl = slice(gi * gw, (gi + 1) * gw)
        win = h[:, sl]
        for j in range(1, w):
            win = win + h_ref[base - j:base - j + tm, sl]
        count = jnp.minimum(w, pos + 1).astype(F32)
        dlt = win / count - h[:, sl]
        ys.append(_dot(dlt, wg_ref[gi]))
    y = jnp.concatenate(ys, axis=-1) * sc_ref[...]
    o_ref[0] = x + y
    last = h_ref[tm + 1:tm + base, :]
    h_ref[1:base, :] = last

    @pl.when(t == pl.num_programs(1) - 1)
    def _():
        nst_ref[0] = last


def _pool_mixer(x, g, state, w_group, scale, pos0):
    b, t, d = x.shape
    ng, gw, _ = w_group.shape
    tm = _row_tile(t, 512)
    return pl.pallas_call(
        functools.partial(_pool_kernel, pos0=pos0),
        grid=(b, t // tm),
        in_specs=[pl.BlockSpec((1, tm, d), lambda i, j: (i, j, 0)),
                  pl.BlockSpec((1, d), lambda i, j: (0, 0)),
                  pl.BlockSpec((1, POOL_STATE, d), lambda i, j: (i, 0, 0)),
                  pl.BlockSpec((ng, gw, gw), lambda i, j: (0, 0, 0)),
                  pl.BlockSpec((1, d), lambda i, j: (0, 0))],
        out_specs=[pl.BlockSpec((1, tm, d), lambda i, j: (i, j, 0)),
                   pl.BlockSpec((1, POOL_STATE, d), lambda i, j: (i, 0, 0))],
        out_shape=[jax.ShapeDtypeStruct((b, t, d), F32),
                   jax.ShapeDtypeStruct((b, POOL_STATE, d), F32)],
        scratch_shapes=[pltpu.VMEM((tm + POOL_STATE + 1, d), F32)],
        compiler_params=_cparams("parallel", "arbitrary"),
        name="pool_mixer",
    )(x, g.reshape(1, d), state, w_group, scale.reshape(1, d))


def _proj_kernel(*refs, n_w, epilogues):
    x_ref, g_ref = refs[0], refs[1]
    w_refs = refs[2:2 + n_w]
    e_refs = refs[2 + n_w:2 + n_w + 1]
    o_refs = refs[2 + n_w + 1:]
    h = _rms(x_ref[...], g_ref[...]).astype(BF16)
    for w_ref, o_ref, ep in zip(w_refs, o_refs, epilogues):
        y = jnp.dot(h, w_ref[...], preferred_element_type=F32)
        if ep == "log_sigmoid":
            u = -(y + e_refs[0][...])
            y = -(jnp.maximum(u, 0.0) + jnp.log1p(jnp.exp(-jnp.abs(u))))
        o_ref[...] = y.astype(o_ref.dtype)


def _proj(x, g, ws, dtypes, epilogues, extra):
    n, d = x.shape
    tm = _row_tile(n, 512)
    in_specs = [pl.BlockSpec((tm, d), lambda i: (i, 0)), pl.BlockSpec((1, d), lambda i: (0, 0))]
    in_specs += [pl.BlockSpec(w.shape, lambda i: (0, 0)) for w in ws]
    in_specs += [pl.BlockSpec(extra.shape, lambda i: (0, 0))]
    return pl.pallas_call(
        functools.partial(_proj_kernel, n_w=len(ws), epilogues=tuple(epilogues)),
        grid=(n // tm,),
        in_specs=in_specs,
        out_specs=[pl.BlockSpec((tm, w.shape[1]), lambda i: (i, 0)) for w in ws],
        out_shape=[jax.ShapeDtypeStruct((n, w.shape[1]), dt) for w, dt in zip(ws, dtypes)],
        compiler_params=_cparams("parallel"),
        name="norm_proj",
    )(x, g.reshape(1, d), *ws, extra)


def _outproj_kernel(x_ref, a_ref, w_ref, o_ref):
    o_ref[...] = x_ref[...] + _dot(a_ref[...], w_ref[...])


def _out_proj(x, a, w):
    n, d = x.shape
    tm = _row_tile(n, 512)
    return pl.pallas_call(
        _outproj_kernel,
        grid=(n // tm,),
        in_specs=[pl.BlockSpec((tm, d), lambda i: (i, 0)),
                  pl.BlockSpec((tm, d), lambda i: (i, 0)),
                  pl.BlockSpec((d, d), lambda i: (0, 0))],
        out_specs=pl.BlockSpec((tm, d), lambda i: (i, 0)),
        out_shape=jax.ShapeDtypeStruct((n, d), F32),
        compiler_params=_cparams("parallel"),
        name="out_proj",
    )(x, a, w)


def _bias_table_kernel(rbt_ref, o_ref):
    width = o_ref.shape[-1]
    rel = BIAS_CENTER - lax.broadcasted_iota(I32, (1, width), 1)
    nb = N_BUCKETS // 2
    max_exact = nb // 2
    ret = (rel > 0).astype(I32) * nb
    n = jnp.abs(rel)
    nf = jnp.maximum(n, 1).astype(F32)
    large = max_exact + (jnp.log(nf / max_exact) / math.log(MAX_DIST / max_exact)
                         * (nb - max_exact)).astype(I32)
    large = jnp.minimum(large, nb - 1)
    bucket = ret + jnp.where(n < max_exact, n, large)
    acc = jnp.zeros(o_ref.shape, F32)
    for j in range(N_BUCKETS):
        acc = jnp.where(bucket == j, rbt_ref[:, j:j + 1], acc)
    o_ref[...] = acc


def _bias_table(rel_bias, width):
    nh = rel_bias.shape[1]
    tab = pl.pallas_call(
        _bias_table_kernel,
        out_shape=jax.ShapeDtypeStruct((nh, width), F32),
        name="bias_table",
    )(rel_bias.T)
    return tab.reshape(nh, width // LANE, LANE).transpose(1, 0, 2)


def _sortable(x):
    x = jnp.where(x == 0.0, 0.0, x)
    bits = lax.bitcast_convert_type(x, I32)
    return jnp.where(bits < 0, bits ^ 0x7FFFFFFF, bits)


def _neg_inf_key():
    import numpy as np
    b = int(np.float32(NEG_INF).view(np.int32))
    return b ^ 0x7FFFFFFF


def _dsa_kernel(q_ref, qi_ref, kwq_ref, kidx_ref, k_ref, v_ref, tab_ref, o_ref,
                key_ref, sel_ref, m_ref, l_ref, acc_ref, *, past, n_keys, n_valid_q, top_k):
    qi = pl.program_id(1)
    tq = q_ref.shape[1]
    lpad = kidx_ref.shape[1]
    kb_sz = LANE
    q0 = past + qi * tq
    nkb = jnp.minimum(lpad // kb_sz, (q0 + tq) // kb_sz)
    negkey = _neg_inf_key()

    qlane = lax.broadcasted_iota(I32, (1, tq), 1)
    qpos = q0 + qlane
    lim = (qpos // CHUNK + 1) * CHUNK
    krow = lax.broadcasted_iota(I32, (kb_sz, tq), 0)

    wi_t = jnp.transpose(kwq_ref[0])

    def kslice(kb):
        return pl.ds(pl.multiple_of(kb * kb_sz, kb_sz), kb_sz)

    def score_body(kb, c):
        kid = kidx_ref[0, kslice(kb), :].astype(BF16)
        sc = jnp.zeros((kb_sz, tq), F32)
        for h in range(IDX_HEADS):
            s = _dot_nt(kid, qi_ref[0, :, h * IDX_DIM:(h + 1) * IDX_DIM])
            sc = sc + jnp.maximum(s, 0.0) * wi_t[IDX_DIM + h:IDX_DIM + h + 1, :]
        sc = sc * ((IDX_DIM * IDX_HEADS) ** -0.5)
        kpos = kb * kb_sz + krow
        key = jnp.where(kpos < lim, _sortable(sc), negkey)
        key = jnp.where(kpos < n_keys, key, INT_MIN)
        key_ref[kslice(kb), :] = key
        return c

    lax.fori_loop(0, nkb, score_body, 0)

    def count(pred_fn):
        def body(kb, a):
            ind = pred_fn(kb, key_ref[kslice(kb), :])
            return a + jnp.sum(ind.reshape(kb_sz // 8, 8, tq), axis=0)
        a = lax.fori_loop(0, nkb, body, jnp.zeros((8, tq), I32))
        return jnp.sum(a, axis=0, keepdims=True)

    def bit_body(i, t_u):
        cand_u = t_u | jnp.left_shift(jnp.int32(1), 31 - i)
        cand_s = cand_u ^ INT_MIN
        cnt = count(lambda kb, key: jnp.where(key >= cand_s, 1, 0))
        return jnp.where(cnt >= top_k, cand_u, t_u)

    t_s = lax.fori_loop(0, 32, bit_body, jnp.zeros((1, tq), I32)) ^ INT_MIN

    def adm01(kb):
        return jnp.where(kb * kb_sz + krow < lim, 1.0, 0.0)

    def sel_body(kb, a):
        key = key_ref[kslice(kb), :]
        sel = jnp.where(key >= t_s, adm01(kb), 0.0)
        sel_ref[kslice(kb), :] = sel
        return a + jnp.sum(sel.reshape(kb_sz // 8, 8, tq), axis=0)

    n_sel = jnp.sum(lax.fori_loop(0, nkb, sel_body, jnp.zeros((8, tq), F32)), axis=0, keepdims=True)
    over = jnp.max(jnp.where(qlane < n_valid_q, n_sel, 0.0)) > top_k

    @pl.when(over)
    def _():
        n_gt = count(lambda kb, key: jnp.where(key > t_s, 1, 0))
        need = (top_k - n_gt).astype(F32)
        r = lax.broadcasted_iota(I32, (kb_sz, kb_sz), 0)
        c = lax.broadcasted_iota(I32, (kb_sz, kb_sz), 1)
        ltri = jnp.where(c < r, 1.0, 0.0).astype(BF16)

        def tie_body(kb, carry):
            key = key_ref[kslice(kb), :]
            adm = adm01(kb)
            eq = jnp.where(key == t_s, adm, 0.0)
            rank = carry + jnp.dot(ltri, eq.astype(BF16), preferred_element_type=F32)
            keep = jnp.where(rank < need, eq, 0.0)
            sel_ref[kslice(kb), :] = jnp.where(key > t_s, adm, keep)
            return carry + jnp.sum(eq, axis=0, keepdims=True)

        lax.fori_loop(0, nkb, tie_body, jnp.zeros((1, tq), F32))

    m_ref[...] = jnp.full(m_ref.shape, NEG_INF, F32)
    l_ref[...] = jnp.zeros(l_ref.shape, F32)
    acc_ref[...] = jnp.zeros(acc_ref.shape, F32)

    def attend(kb, bias_fn):
        sel = sel_ref[kslice(kb), :]
        for g in range(B_KV):
            kg = k_ref[0, kslice(kb), g * B_HD:(g + 1) * B_HD].astype(BF16)
            vg = v_ref[0, kslice(kb), g * B_HD:(g + 1) * B_HD].astype(BF16)
            for r in range(B_REP):
                h = g * B_REP + r
                s = _dot_nt(kg, q_ref[0, :, h * B_HD:(h + 1) * B_HD]) * (B_HD ** -0.5)
                s = jnp.where(sel != 0.0, s + bias_fn(h), NEG_INF)
                m_old = m_ref[h:h + 1, :]
                m_new = jnp.maximum(m_old, jnp.max(s, axis=0, keepdims=True))
                alpha = jnp.exp(m_old - m_new)
                p = jnp.exp(s - m_new)
                l_ref[h:h + 1, :] = alpha * l_ref[h:h + 1, :] + jnp.sum(p, axis=0, keepdims=True)
                hs = slice(h * B_HD, (h + 1) * B_HD)
                acc_ref[hs, :] = alpha * acc_ref[hs, :] + _dot_tn(vg, p)
                m_ref[h:h + 1, :] = m_new

    kb_near = jnp.maximum(q0 // kb_sz - 1, 0)
    far_blk = (BIAS_CENTER + MAX_DIST) // LANE
    far_bias = tab_ref[far_blk][:, 0:1]

    def far_body(kb, c):
        attend(kb, lambda h: far_bias[h:h + 1, :])
        return c

    lax.fori_loop(0, kb_near, far_body, 0)

    n_win = tq // LANE + 1

    def near_body(kb, c):
        blk0 = 1 - (kb - q0 // kb_sz)
        win = jnp.concatenate([tab_ref[blk0 + w] for w in range(n_win)], axis=1)

        def bias_fn(h):
            rows = jnp.broadcast_to(win[h:h + 1, :], (kb_sz, tq + LANE))
            return pltpu.roll(rows, 0, 1, stride=1, stride_axis=0)[:, LANE:]
        attend(kb, bias_fn)
        return c

    lax.fori_loop(kb_near, nkb, near_body, 0)

    inv_l = 1.0 / l_ref[...]
    outs = [acc_ref[h * B_HD:(h + 1) * B_HD, :] * inv_l[h:h + 1, :] for h in range(B_HEADS)]
    o_ref[0] = jnp.transpose(jnp.concatenate(outs, axis=0))


def _dsa_attention(q, qidx, kwq, kidx, k, v, tab, past, n_keys, n_valid_q, tq):
    b, tqp, d = q.shape
    lpad = kidx.shape[1]
    top_k = min(TOPK_MAX, n_keys // 4)
    kern = functools.partial(_dsa_kernel, past=past, n_keys=n_keys, n_valid_q=n_valid_q, top_k=top_k)
    return pl.pallas_call(
        kern,
        grid=(b, tqp // tq),
        in_specs=[pl.BlockSpec((1, tq, d), lambda i, j: (i, j, 0)),
                  pl.BlockSpec((1, tq, qidx.shape[2]), lambda i, j: (i, j, 0)),
                  pl.BlockSpec((1, tq, LANE), lambda i, j: (i, j, 0)),
                  pl.BlockSpec((1, lpad, IDX_DIM), lambda i, j: (i, 0, 0)),
                  pl.BlockSpec((1, lpad, B_KV * B_HD), lambda i, j: (i, 0, 0)),
                  pl.BlockSpec((1, lpad, B_KV * B_HD), lambda i, j: (i, 0, 0)),
                  pl.BlockSpec(tab.shape, lambda i, j: (0, 0, 0))],
        out_specs=pl.BlockSpec((1, tq, d), lambda i, j: (i, j, 0)),
        out_shape=jax.ShapeDtypeStruct((b, tqp, d), F32),
        scratch_shapes=[pltpu.VMEM((lpad, tq), I32), pltpu.VMEM((lpad, tq), F32),
                        pltpu.VMEM((B_HEADS, tq), F32), pltpu.VMEM((B_HEADS, tq), F32),
                        pltpu.VMEM((d, tq), F32)],
        compiler_params=_cparams("parallel", "parallel"),
        name="dsa_attention",
    )(q, qidx, kwq, kidx, k, v, tab)


def _cumsum_kernel(x_ref, o_ref):
    x = x_ref[0]
    n = x.shape[-1]
    lane = lax.broadcasted_iota(I32, x.shape, 1)
    s = 1
    while s < n:
        x = x + jnp.where(lane >= s, pltpu.roll(x, s, 1), 0.0)
        s *= 2
    o_ref[0] = x


def _cumsum_lanes(x):
    b, h, n = x.shape
    return pl.pallas_call(
        _cumsum_kernel,
        grid=(b,),
        in_specs=[pl.BlockSpec((1, h, n), lambda i: (i, 0, 0))],
        out_specs=pl.BlockSpec((1, h, n), lambda i: (i, 0, 0)),
        out_shape=jax.ShapeDtypeStruct((b, h, n), F32),
        compiler_params=_cparams("parallel"),
        name="logf_cumsum",
    )(x)


def _fox_kernel(q_ref, k_ref, v_ref, cq_ref, ck_ref, o_ref, m_ref, l_ref, acc_ref, *, past):
    qi, ki = pl.program_id(1), pl.program_id(2)
    tq, tk = q_ref.shape[1], k_ref.shape[1]
    q0 = past + qi * tq

    @pl.when(ki == 0)
    def _():
        m_ref[...] = jnp.full(m_ref.shape, NEG_INF, F32)
        l_ref[...] = jnp.zeros(l_ref.shape, F32)
        acc_ref[...] = jnp.zeros(acc_ref.shape, F32)

    @pl.when(ki * tk <= q0 + tq - 1)
    def _():
        kpos = ki * tk + lax.broadcasted_iota(I32, (tk, tq), 0)
        qpos = q0 + lax.broadcasted_iota(I32, (tk, tq), 1)
        causal = kpos <= qpos
        for h in range(D_HEADS):
            hs = slice(h * D_HD, (h + 1) * D_HD)
            s = _dot_nt(k_ref[0, :, hs], q_ref[0, :, hs]) * (D_HD ** -0.5)
            decay = cq_ref[0, h:h + 1, :] - ck_ref[0, :, h:h + 1]
            s = jnp.where(causal, s + decay, NEG_INF)
            m_old = m_ref[h:h + 1, :]
            m_new = jnp.maximum(m_old, jnp.max(s, axis=0, keepdims=True))
            alpha = jnp.exp(m_old - m_new)
            p = jnp.exp(s - m_new)
            l_ref[h:h + 1, :] = alpha * l_ref[h:h + 1, :] + jnp.sum(p, axis=0, keepdims=True)
            acc_ref[hs, :] = alpha * acc_ref[hs, :] + _dot_tn(v_ref[0, :, hs], p)
            m_ref[h:h + 1, :] = m_new

    @pl.when(ki == pl.num_programs(2) - 1)
    def _():
        inv_l = 1.0 / l_ref[...]
        outs = [acc_ref[h * D_HD:(h + 1) * D_HD, :] * inv_l[h:h + 1, :] for h in range(D_HEADS)]
        o_ref[0] = jnp.transpose(jnp.concatenate(outs, axis=0))


def _fox_attention(q, k, v, cum_t, cum, past, tq, tk):
    b, tqp, d = q.shape
    lpad = k.shape[1]
    nq, nk = tqp // tq, lpad // tk

    def kidx(i, j, kk):
        return (i, jnp.minimum(kk, (past + j * tq + tq - 1) // tk), 0)

    return pl.pallas_call(
        functools.partial(_fox_kernel, past=past),
        grid=(b, nq, nk),
        in_specs=[pl.BlockSpec((1, tq, d), lambda i, j, kk: (i, j, 0)),
                  pl.BlockSpec((1, tk, d), kidx),
                  pl.BlockSpec((1, tk, d), kidx),
                  pl.BlockSpec((1, D_HEADS, tq), lambda i, j, kk: (i, 0, past // tq + j)),
                  pl.BlockSpec((1, tk, D_HEADS), kidx)],
        out_specs=pl.BlockSpec((1, tq, d), lambda i, j, kk: (i, j, 0)),
        out_shape=jax.ShapeDtypeStruct((b, tqp, d), F32),
        scratch_shapes=[pltpu.VMEM((D_HEADS, tq), F32), pltpu.VMEM((D_HEADS, tq), F32),
                        pltpu.VMEM((d, tq), F32)],
        compiler_params=_cparams("parallel", "parallel", "arbitrary"),
        name="fox_attention",
    )(q, k, v, cum_t, cum)


def _pad_rows(a, rows):
    if a.shape[1] == rows:
        return a
    return jnp.pad(a, ((0, 0), (0, rows - a.shape[1])) + ((0, 0),) * (a.ndim - 2))


def _round_up(n, m):
    return -(-n // m) * m


def _dsa_mixer(x, g, k_past, v_past, ki_past, w, w_out, tab):
    b, t, d = x.shape
    past = k_past.shape[1]
    n_keys = past + t
    q, k, v, qidx, kw = _proj(x.reshape(b * t, d), g, w, (BF16, F32, F32, BF16, F32),
                              (None,) * 5, jnp.zeros((1, LANE), F32))
    kvw = B_KV * B_HD
    k, v, kw = k.reshape(b, t, kvw), v.reshape(b, t, kvw), kw.reshape(b, t, LANE)
    ki = kw[:, :, :IDX_DIM]
    tq = min(2 * LANE, _round_up(t, LANE))
    tqp = _round_up(t, tq)
    lpad = _round_up(n_keys, LANE)
    cat = lambda p, c: _pad_rows(jnp.concatenate([p.reshape(b, past, c.shape[-1]), c], axis=1), lpad)
    o = _dsa_attention(_pad_rows(q.reshape(b, t, d), tqp), _pad_rows(qidx.reshape(b, t, -1), tqp),
                       _pad_rows(kw, tqp), cat(ki_past, ki), cat(k_past, k), cat(v_past, v),
                       tab, past, n_keys, t, tq)
    y = _out_proj(x.reshape(b * t, d), o[:, :t].reshape(b * t, d), w_out).reshape(b, t, d)
    return (y, k.reshape(b, t, B_KV, B_HD), v.reshape(b, t, B_KV, B_HD), ki)


def _fox_mixer(x, g, k_past, v_past, lf_past, w, b_f, w_out):
    b, t, d = x.shape
    past = k_past.shape[1]
    n_keys = past + t
    q, k, v, lf = _proj(x.reshape(b * t, d), g, w, (BF16, F32, F32, F32),
                        (None, None, None, "log_sigmoid"), b_f)
    k, v = k.reshape(b, t, d), v.reshape(b, t, d)
    logf = lf.reshape(b, t, LANE)[:, :, :D_HEADS]
    tq = min(2 * LANE, _round_up(t, LANE))
    tk = 2 * LANE
    tqp = _round_up(t, tq)
    lpad = _round_up(max(n_keys, past + tqp), tk)
    lf_all = _pad_rows(jnp.concatenate([lf_past, logf], axis=1), lpad)
    cum_t = _cumsum_lanes(jnp.swapaxes(lf_all, 1, 2))
    cum = jnp.swapaxes(cum_t, 1, 2)
    cat = lambda p, c: _pad_rows(jnp.concatenate([p.reshape(b, past, d), c], axis=1), lpad)
    o = _fox_attention(_pad_rows(q.reshape(b, t, d), tqp), cat(k_past, k), cat(v_past, v),
                       cum_t, cum, past, tq, tk)
    y = _out_proj(x.reshape(b * t, d), o[:, :t].reshape(b * t, d), w_out).reshape(b, t, d)
    return (y, k.reshape(b, t, D_HEADS, D_HD), v.reshape(b, t, D_HEADS, D_HD), logf)


def _run_group(x, pos0, a_st, b_k, b_v, b_ki, c_st, d_k, d_v, d_lf, mem_k, mem_v, prm):
    b, t, d = x.shape
    depth = prm["norm_mix"].shape[0]
    new = {n: [] for n in ("a", "bk", "bv", "bki", "c", "dk", "dv", "dlf")}
    for i in range(depth):
        kind, j = i % 4, i // 4
        g = prm["norm_mix"][i]
        if kind == 0:
            x, st = _conv_mixer(x, g, prm["a_w_in"][j], prm["a_conv"][j], a_st[j], prm["a_w_out"][j])
            new["a"].append(st)
        elif kind == 1:
            x, kk, vv, ki = _dsa_mixer(x, g, b_k[j], b_v[j], b_ki[j], prm["b_w"][j], prm["b_w_out"][j],
                                       prm["bias_tab"])
            new["bk"].append(kk); new["bv"].append(vv); new["bki"].append(ki)
        elif kind == 2:
            x, st = _pool_mixer(x, g, c_st[j], prm["c_w_group"][j], prm["c_scale"][j], pos0)
            new["c"].append(st)
        else:
            x, kk, vv, lf = _fox_mixer(x, g, d_k[j], d_v[j], d_lf[j], prm["d_w"][j], prm["d_b_f"][j],
                                       prm["d_w_out"][j])
            new["dk"].append(kk); new["dv"].append(vv); new["dlf"].append(lf)
        x = _xattn(x, prm["norm_xattn"][i], prm["xa_wq"][i], mem_k[i], mem_v[i], prm["xa_wo"][i])
        last = i == depth - 1
        x = _ffn(x.reshape(b * t, d), prm["norm_ffn"][i], prm["ffn_w1"][i], prm["ffn_w2"][i],
                 prm["final_norm"], last).reshape(b, t, d)
    return (x,) + tuple(jnp.stack(new[n]) for n in ("a", "bk", "bv", "bki", "c", "dk", "dv", "dlf"))


def kernel(x_prompt, x_sample, state_a_conv, cache_b_k, cache_b_v, cache_b_kidx, state_c_pool,
           cache_d_k, cache_d_v, cache_d_logf, cache_mem_k, cache_mem_v, mem_prompt,
           norm_mix, norm_xattn, norm_mem, norm_ffn, final_norm,
           a_w_in, a_conv, a_w_out, b_w_in, b_w_out, rel_bias, c_w_group, c_scale,
           d_w_in, d_b_f, d_w_out, xa_wq, xa_wkv, xa_wo, ffn_w1, ffn_w2):
    bp, t, d = x_prompt.shape
    depth = norm_mix.shape[0]
    n_b, n_d = b_w_in.shape[0], d_w_in.shape[0]
    bf = lambda w: w.astype(BF16)

    def split_cols(w, widths):
        out, c = [], 0
        for wd in widths:
            piece = w[:, c:c + wd]
            c += wd
            if wd % LANE:
                piece = jnp.pad(piece, ((0, 0), (0, _round_up(wd, LANE) - wd)))
            out.append(bf(piece))
        assert c == w.shape[1]
        return out

    b_q, b_kvw = B_HEADS * B_HD, B_KV * B_HD
    b_w = [split_cols(b_w_in[j], (b_q, b_kvw, b_kvw, IDX_HEADS * IDX_DIM, IDX_DIM + IDX_HEADS))
           for j in range(n_b)]
    d_w = [split_cols(d_w_in[j], (d, d, d, D_HEADS)) for j in range(n_d)]
    d_bf = [jnp.pad(d_b_f[j], (0, LANE - D_HEADS)).reshape(1, LANE) for j in range(n_d)]
    bias_tab = _bias_table(rel_bias, 5 * LANE)

    prm = {"norm_mix": norm_mix, "norm_xattn": norm_xattn, "norm_ffn": norm_ffn, "final_norm": final_norm,
           "a_w_in": bf(a_w_in), "a_conv": a_conv, "a_w_out": bf(a_w_out),
           "b_w": b_w, "b_w_out": bf(b_w_out), "bias_tab": bias_tab,
           "c_w_group": bf(c_w_group), "c_scale": c_scale,
           "d_w": d_w, "d_b_f": d_bf, "d_w_out": bf(d_w_out),
           "xa_wq": bf(xa_wq), "xa_wo": bf(xa_wo), "ffn_w1": bf(ffn_w1), "ffn_w2": bf(ffn_w2)}

    n_mem = mem_prompt.shape[1]
    mk, mv = _memory_kv(mem_prompt.reshape(bp * n_mem, d), norm_mem, bf(xa_wkv))
    mk, mv = mk.reshape(depth, bp, n_mem, d), mv.reshape(depth, bp, n_mem, d)

    n_a, n_c = a_w_in.shape[0], c_w_group.shape[0]
    z = lambda *s: jnp.zeros(s, F32)
    gp = _run_group(x_prompt, 0,
                    z(n_a, bp, CONV_W - 1, d),
                    z(n_b, bp, 0, B_KV, B_HD), z(n_b, bp, 0, B_KV, B_HD), z(n_b, bp, 0, IDX_DIM),
                    z(n_c, bp, POOL_STATE, d),
                    z(n_d, bp, 0, D_HEADS, D_HD), z(n_d, bp, 0, D_HEADS, D_HD), z(n_d, bp, 0, D_HEADS),
                    mk, mv, prm)

    bs = x_sample.shape[0]
    past_len = cache_b_k.shape[2]
    gs = _run_group(x_sample, past_len, state_a_conv, cache_b_k, cache_b_v, cache_b_kidx, state_c_pool,
                    cache_d_k, cache_d_v, cache_d_logf,
                    cache_mem_k.reshape(depth, bs, n_mem, d), cache_mem_v.reshape(depth, bs, n_mem, d), prm)

    (y_p, a_p, bk_p, bv_p, bki_p, c_p, dk_p, dv_p, dlf_p) = gp
    (y_s, a_s, bk_s, bv_s, bki_s, c_s, dk_s, dv_s, dlf_s) = gs
    mh = (depth, bp, n_mem, MEM_HEADS, d // MEM_HEADS)
    return (y_p, y_s, a_p, a_s, bk_p, bv_p, bki_p, bk_s, bv_s, bki_s, c_p, c_s,
            dk_p, dv_p, dlf_p, dk_s, dv_s, dlf_s, mk.reshape(mh), mv.reshape(mh))
```

```python
import functools
import math

import jax
import jax.numpy as jnp
from jax import lax
from jax.experimental import pallas as pl
from jax.experimental.pallas import tpu as pltpu

F32 = jnp.float32
BF16 = jnp.bfloat16
I32 = jnp.int32

EPS = 1e-6
NEG_INF = -1e30
LOG2E = math.log2(math.e)
CHUNK = 64
LANE = 128
VMEM_LIMIT = 48 * 1024 * 1024

CONV_W = 3
POOL_WINDOWS = (2, 4, 8, 16)
POOL_STATE = max(POOL_WINDOWS) - 1
B_HEADS, B_KV, B_HD = 8, 2, 128
B_REP = B_HEADS // B_KV
IDX_HEADS, IDX_DIM = 8, 64
TOPK_MAX = 256
N_BUCKETS, MAX_DIST = 32, 128
D_HEADS, D_HD = 8, 128
MEM_HEADS = 4
INT_MIN = -2147483648
BIAS_CENTER = 2 * LANE


def _cparams(*sem):
    return pltpu.CompilerParams(dimension_semantics=sem, vmem_limit_bytes=VMEM_LIMIT)


def _dot(a, b):
    return jnp.dot(a.astype(BF16), b.astype(BF16), preferred_element_type=F32)


def _dot_nt(a, b):
    return lax.dot_general(a.astype(BF16), b.astype(BF16), (((1,), (1,)), ((), ())),
                           preferred_element_type=F32)


def _dot_tn(a, b):
    return lax.dot_general(a.astype(BF16), b.astype(BF16), (((0,), (0,)), ((), ())),
                           preferred_element_type=F32)


def _rms(x, g):
    return x * lax.rsqrt(jnp.mean(x * x, axis=-1, keepdims=True) + EPS) * g


def _row_tile(n, cap):
    t = min(n, cap)
    assert n % t == 0
    return t


def _memkv_kernel(mem_ref, g_ref, w_ref, k_ref, v_ref):
    m = mem_ref[0]
    mn = m * lax.rsqrt(jnp.mean(m * m, axis=-1, keepdims=True) + EPS)
    h = (mn * g_ref[0]).astype(BF16)
    d = m.shape[-1]
    hd = d // MEM_HEADS
    k = jnp.dot(h, w_ref[0, :, :d], preferred_element_type=F32)
    v = jnp.dot(h, w_ref[0, :, d:], preferred_element_type=F32)
    for hh in range(MEM_HEADS):
        k_ref[0, 0, :, hh, :] = k[:, hh * hd:(hh + 1) * hd]
        v_ref[0, 0, :, hh, :] = v[:, hh * hd:(hh + 1) * hd]


def _memory_kv(mem, g_mem, w_kv):
    depth, d = g_mem.shape
    b, nm, _ = mem.shape
    hd = d // MEM_HEADS
    out = jax.ShapeDtypeStruct((depth, b, nm, MEM_HEADS, hd), F32)
    return pl.pallas_call(
        _memkv_kernel,
        grid=(depth, b),
        in_specs=[pl.BlockSpec((1, nm, d), lambda l, i: (i, 0, 0)),
                  pl.BlockSpec((1, 1, d), lambda l, i: (l, 0, 0)),
                  pl.BlockSpec((1, d, 2 * d), lambda l, i: (l, 0, 0))],
        out_specs=[pl.BlockSpec((1, 1, nm, MEM_HEADS, hd), lambda l, i: (l, i, 0, 0, 0)),
                   pl.BlockSpec((1, 1, nm, MEM_HEADS, hd), lambda l, i: (l, i, 0, 0, 0))],
        out_shape=[out, out],
        compiler_params=_cparams("parallel", "parallel"),
        name="memory_kv",
    )(mem, g_mem.reshape(depth, 1, d), w_kv)


def _ffn_kernel(x_ref, g_ref, w1_ref, w2_ref, gf_ref, o_ref, h_ref, acc_ref, *, final_norm):
    j = pl.program_id(1)

    @pl.when(j == 0)
    def _():
        h_ref[...] = _rms(x_ref[...], g_ref[...]).astype(BF16)
        acc_ref[...] = jnp.zeros_like(acc_ref)

    u = jnp.maximum(jnp.dot(h_ref[...], w1_ref[...], preferred_element_type=F32), 0.0)
    acc_ref[...] += jnp.dot((u * u).astype(BF16), w2_ref[...], preferred_element_type=F32)

    @pl.when(j == pl.num_programs(1) - 1)
    def _():
        y = x_ref[...] + acc_ref[...]
        o_ref[...] = _rms(y, gf_ref[...]) if final_norm else y


def _ffn(x, g, w1, w2, gf, layer, final_norm):
    n, d = x.shape
    f = w1.shape[2]
    tm = _row_tile(n, 1024)
    tf = 512
    return pl.pallas_call(
        functools.partial(_ffn_kernel, final_norm=final_norm),
        grid=(n // tm, f // tf),
        in_specs=[pl.BlockSpec((tm, d), lambda i, j: (i, 0)),
                  pl.BlockSpec((None, 1, d), lambda i, j: (layer, 0, 0)),
                  pl.BlockSpec((None, d, tf), lambda i, j: (layer, 0, j)),
                  pl.BlockSpec((None, tf, d), lambda i, j: (layer, j, 0)),
                  pl.BlockSpec((1, d), lambda i, j: (0, 0))],
        out_specs=pl.BlockSpec((tm, d), lambda i, j: (i, 0)),
        out_shape=jax.ShapeDtypeStruct((n, d), F32),
        scratch_shapes=[pltpu.VMEM((tm, d), BF16), pltpu.VMEM((tm, d), F32)],
        compiler_params=_cparams("parallel", "arbitrary"),
        name="ffn",
    )(x, g.reshape(-1, 1, d), w1, w2, gf.reshape(1, d))


def _xattn_kernel(x_ref, g_ref, wq_ref, mk_ref, mv_ref, wo_ref, o_ref):
    x = x_ref[0]
    d = x.shape[-1]
    hd = d // MEM_HEADS
    h = _rms(x, g_ref[...]).astype(BF16)
    q = jnp.dot(h, wq_ref[...], preferred_element_type=F32)
    outs = []
    for hh in range(MEM_HEADS):
        sl = slice(hh * hd, (hh + 1) * hd)
        s = _dot_nt(q[:, sl], mk_ref[:, hh, :]) * (hd ** -0.5)
        m = jnp.max(s, axis=-1, keepdims=True)
        p = jnp.exp(s - m)
        l = jnp.sum(p, axis=-1, keepdims=True)
        outs.append(_dot(p, mv_ref[:, hh, :]) / l)
    o = jnp.concatenate(outs, axis=-1)
    o_ref[0] = x + _dot(o, wo_ref[...])


def _xattn(x, g, wq, mk, mv, wo, layer):
    b, t, d = x.shape
    nm, nh, hd = mk.shape[2:]
    tm = _row_tile(t, 512)
    kv_spec = pl.BlockSpec((None, None, nm, nh, hd), lambda i, j: (layer, i, 0, 0, 0))
    return pl.pallas_call(
        _xattn_kernel,
        grid=(b, t // tm),
        in_specs=[pl.BlockSpec((1, tm, d), lambda i, j: (i, j, 0)),
                  pl.BlockSpec((None, 1, d), lambda i, j: (layer, 0, 0)),
                  pl.BlockSpec((None, d, d), lambda i, j: (layer, 0, 0)),
                  kv_spec, kv_spec,
                  pl.BlockSpec((None, d, d), lambda i, j: (layer, 0, 0))],
        out_specs=pl.BlockSpec((1, tm, d), lambda i, j: (i, j, 0)),
        out_shape=jax.ShapeDtypeStruct((b, t, d), F32),
        compiler_params=_cparams("parallel", "parallel"),
        name="xattn",
    )(x, g.reshape(-1, 1, d), wq, mk, mv, wo)


def _conv_kernel(x_ref, g_ref, win_ref, wc_ref, st_ref, wout_ref, o_ref, nst_ref, z_ref):
    t = pl.program_id(1)
    x = x_ref[0]
    tm, d = x.shape
    pad = 8

    @pl.when(t == 0)
    def _():
        z_ref[pad - 2:pad, :] = st_ref[0]

    h = _rms(x, g_ref[...]).astype(BF16)
    bg = jnp.dot(h, win_ref[:, 0:d], preferred_element_type=F32)
    cg = jnp.dot(h, win_ref[:, d:2 * d], preferred_element_type=F32)
    u = jnp.dot(h, win_ref[:, 2 * d:3 * d], preferred_element_type=F32)
    z = cg * u
    z_ref[pad:pad + tm, :] = z
    conv = (z_ref[pad - 2:pad - 2 + tm, :] * wc_ref[0:1, :]
            + z_ref[pad - 1:pad - 1 + tm, :] * wc_ref[1:2, :]
            + z * wc_ref[2:3, :])
    o_ref[0] = x + _dot(bg * conv, wout_ref[...])
    last = z_ref[pad + tm - 2:pad + tm, :]
    z_ref[pad - 2:pad, :] = last

    @pl.when(t == pl.num_programs(1) - 1)
    def _():
        nst_ref[0] = last


def _conv_mixer(x, g, w_in, w_conv, state, w_out):
    b, t, d = x.shape
    tm = _row_tile(t, 512)
    return pl.pallas_call(
        _conv_kernel,
        grid=(b, t // tm),
        in_specs=[pl.BlockSpec((1, tm, d), lambda i, j: (i, j, 0)),
                  pl.BlockSpec((1, d), lambda i, j: (0, 0)),
                  pl.BlockSpec((d, 3 * d), lambda i, j: (0, 0)),
                  pl.BlockSpec((CONV_W, d), lambda i, j: (0, 0)),
                  pl.BlockSpec((1, CONV_W - 1, d), lambda i, j: (i, 0, 0)),
                  pl.BlockSpec((d, d), lambda i, j: (0, 0))],
        out_specs=[pl.BlockSpec((1, tm, d), lambda i, j: (i, j, 0)),
                   pl.BlockSpec((1, CONV_W - 1, d), lambda i, j: (i, 0, 0))],
        out_shape=[jax.ShapeDtypeStruct((b, t, d), F32),
                   jax.ShapeDtypeStruct((b, CONV_W - 1, d), F32)],
        scratch_shapes=[pltpu.VMEM((tm + 8, d), F32)],
        compiler_params=_cparams("parallel", "arbitrary"),
        name="conv_mixer",
    )(x, g.reshape(1, d), w_in, w_conv, state, w_out)


def _pool_kernel(x_ref, g_ref, st_ref, wg_ref, sc_ref, o_ref, nst_ref, h_ref, *, pos0):
    t = pl.program_id(1)
    x = x_ref[0]
    tm, d = x.shape
    gw = d // len(POOL_WINDOWS)
    base = POOL_STATE + 1

    @pl.when(t == 0)
    def _():
        h_ref[1:base, :] = st_ref[0]

    h = _rms(x, g_ref[...])
    h_ref[base:base + tm, :] = h
    pos = pos0 + t * tm + lax.broadcasted_iota(I32, (tm, gw), 0)
    ys = []
    for gi, w in enumerate(POOL_WINDOWS):
        sl = slice(gi * gw, (gi + 1) * gw)
        win = h[:, sl]
        for j in range(1, w):
            win = win + h_ref[base - j:base - j + tm, sl]
        count = jnp.minimum(w, pos + 1).astype(F32)
        dlt = win / count - h[:, sl]
        ys.append(_dot(dlt, wg_ref[gi]))
    y = jnp.concatenate(ys, axis=-1) * sc_ref[...]
    o_ref[0] = x + y
    last = h_ref[tm + 1:tm + base, :]
    h_ref[1:base, :] = last

    @pl.when(t == pl.num_programs(1) - 1)
    def _():
        nst_ref[0] = last


def _pool_mixer(x, g, state, w_group, scale, pos0):
    b, t, d = x.shape
    ng, gw, _ = w_group.shape
    tm = _row_tile(t, 512)
    return pl.pallas_call(
        functools.partial(_pool_kernel, pos0=pos0),
        grid=(b, t // tm),
        in_specs=[pl.BlockSpec((1, tm, d), lambda i, j: (i, j, 0)),
                  pl.BlockSpec((1, d), lambda i, j: (0, 0)),
                  pl.BlockSpec((1, POOL_STATE, d), lambda i, j: (i, 0, 0)),
                  pl.BlockSpec((ng, gw, gw), lambda i, j: (0, 0, 0)),
                  pl.BlockSpec((1, d), lambda i, j: (0, 0))],
        out_specs=[pl.BlockSpec((1, tm, d), lambda i, j: (i, j, 0)),
                   pl.BlockSpec((1, POOL_STATE, d), lambda i, j: (i, 0, 0))],
        out_shape=[jax.ShapeDtypeStruct((b, t, d), F32),
                   jax.ShapeDtypeStruct((b, POOL_STATE, d), F32)],
        scratch_shapes=[pltpu.VMEM((tm + POOL_STATE + 1, d), F32)],
        compiler_params=_cparams("parallel", "arbitrary"),
        name="pool_mixer",
    )(x, g.reshape(1, d), state, w_group, scale.reshape(1, d))


def _proj_kernel(*refs, n_w, outs):
    x_ref, g_ref = refs[0], refs[1]
    w_refs = refs[2:2 + n_w]
    e_ref = refs[2 + n_w]
    o_refs = refs[3 + n_w:]
    h = _rms(x_ref[0], g_ref[...]).astype(BF16)
    ys = {}
    for (wi, mode, _, ep), o_ref in zip(outs, o_refs):
        if wi not in ys:
            ys[wi] = jnp.dot(h, w_refs[wi][...], preferred_element_type=F32)
        y = ys[wi]
        if ep == "log_sigmoid":
            u = -(y + e_ref[...])
            y = -(jnp.maximum(u, 0.0) + jnp.log1p(jnp.exp(-jnp.abs(u))))
        if mode == "rows":
            o_ref[0] = y.astype(o_ref.dtype)
        elif mode == "t":
            o_ref[0] = jnp.transpose(y).astype(o_ref.dtype)
        else:
            for hh in range(y.shape[1] // mode):
                o_ref[0, :, hh, :] = y[:, hh * mode:(hh + 1) * mode].astype(o_ref.dtype)


def _proj(x, g, ws, outs, extra):
    b, t, d = x.shape
    tm = _row_tile(t, 512)
    in_specs = [pl.BlockSpec((1, tm, d), lambda i, j: (i, j, 0)), pl.BlockSpec((1, d), lambda i, j: (0, 0))]
    in_specs += [pl.BlockSpec(w.shape, lambda i, j: (0, 0)) for w in ws]
    in_specs += [pl.BlockSpec(extra.shape, lambda i, j: (0, 0))]
    out_specs, out_shape = [], []
    for wi, mode, dt, _ in outs:
        n = ws[wi].shape[1]
        if mode == "rows":
            out_specs.append(pl.BlockSpec((1, tm, n), lambda i, j: (i, j, 0)))
            out_shape.append(jax.ShapeDtypeStruct((b, t, n), dt))
        elif mode == "t":
            out_specs.append(pl.BlockSpec((1, n, tm), lambda i, j: (i, 0, j)))
            out_shape.append(jax.ShapeDtypeStruct((b, n, t), dt))
        else:
            out_specs.append(pl.BlockSpec((1, tm, n // mode, mode), lambda i, j: (i, j, 0, 0)))
            out_shape.append(jax.ShapeDtypeStruct((b, t, n // mode, mode), dt))
    return pl.pallas_call(
        functools.partial(_proj_kernel, n_w=len(ws), outs=tuple(outs)),
        grid=(b, t // tm),
        in_specs=in_specs,
        out_specs=out_specs,
        out_shape=out_shape,
        compiler_params=_cparams("parallel", "parallel"),
        name="norm_proj",
    )(x, g.reshape(1, d), *ws, extra)


def _outproj_kernel(x_ref, a_ref, w_ref, o_ref):
    o_ref[...] = x_ref[...] + _dot(a_ref[...], w_ref[...])


def _out_proj(x, a, w):
    n, d = x.shape
    tm = _row_tile(n, 512)
    return pl.pallas_call(
        _outproj_kernel,
        grid=(n // tm,),
        in_specs=[pl.BlockSpec((tm, d), lambda i: (i, 0)),
                  pl.BlockSpec((tm, d), lambda i: (i, 0)),
                  pl.BlockSpec((d, d), lambda i: (0, 0))],
        out_specs=pl.BlockSpec((tm, d), lambda i: (i, 0)),
        out_shape=jax.ShapeDtypeStruct((n, d), F32),
        compiler_params=_cparams("parallel"),
        name="out_proj",
    )(x, a, w)


def _bias_table_kernel(rbt_ref, o_ref):
    width = o_ref.shape[-1]
    rel = BIAS_CENTER - lax.broadcasted_iota(I32, (1, width), 1)
    nb = N_BUCKETS // 2
    max_exact = nb // 2
    ret = (rel > 0).astype(I32) * nb
    n = jnp.abs(rel)
    nf = jnp.maximum(n, 1).astype(F32)
    large = max_exact + (jnp.log(nf / max_exact) / math.log(MAX_DIST / max_exact)
                         * (nb - max_exact)).astype(I32)
    large = jnp.minimum(large, nb - 1)
    bucket = ret + jnp.where(n < max_exact, n, large)
    acc = jnp.zeros(o_ref.shape, F32)
    for j in range(N_BUCKETS):
        acc = jnp.where(bucket == j, rbt_ref[:, j:j + 1], acc)
    o_ref[...] = acc


def _bias_table(rel_bias, width):
    nh = rel_bias.shape[1]
    tab = pl.pallas_call(
        _bias_table_kernel,
        out_shape=jax.ShapeDtypeStruct((nh, width), F32),
        name="bias_table",
    )(rel_bias.T)
    return tab.reshape(nh, width // LANE, LANE).transpose(1, 0, 2)


def _sortable(x):
    x = jnp.where(x == 0.0, 0.0, x)
    bits = lax.bitcast_convert_type(x, I32)
    return jnp.where(bits < 0, bits ^ 0x7FFFFFFF, bits)


def _neg_inf_key():
    import numpy as np
    b = int(np.float32(NEG_INF).view(np.int32))
    return b ^ 0x7FFFFFFF


def _dsa_kernel(q_ref, qi_ref, kwq_ref, kidx_ref, k_ref, v_ref, tab_ref, o_ref,
                key_ref, sel_ref, m_ref, l_ref, acc_ref, *, past, n_keys, n_valid_q, top_k):
    qi = pl.program_id(1)
    tq = q_ref.shape[1]
    lpad = kidx_ref.shape[1]
    kb_sz = LANE
    q0 = past + qi * tq
    nkb = jnp.minimum(lpad // kb_sz, (q0 + tq) // kb_sz)
    negkey = _neg_inf_key()

    qlane = lax.broadcasted_iota(I32, (1, tq), 1)
    qpos = q0 + qlane
    lim = (qpos // CHUNK + 1) * CHUNK
    krow = lax.broadcasted_iota(I32, (kb_sz, tq), 0)

    wi_t = jnp.transpose(kwq_ref[0])

    def kslice(kb):
        return pl.ds(pl.multiple_of(kb * kb_sz, kb_sz), kb_sz)

    def score_body(kb, c):
        kid = kidx_ref[0, kslice(kb), :].astype(BF16)
        sc = jnp.zeros((kb_sz, tq), F32)
        for h in range(IDX_HEADS):
            s = _dot_nt(kid, qi_ref[0, :, h * IDX_DIM:(h + 1) * IDX_DIM])
            sc = sc + jnp.maximum(s, 0.0) * wi_t[IDX_DIM + h:IDX_DIM + h + 1, :]
        sc = sc * ((IDX_DIM * IDX_HEADS) ** -0.5)
        kpos = kb * kb_sz + krow
        key = jnp.where(kpos < lim, _sortable(sc), negkey)
        key = jnp.where(kpos < n_keys, key, INT_MIN)
        key_ref[kslice(kb), :] = key
        return c

    lax.fori_loop(0, nkb, score_body, 0)

    def count(pred_fn):
        def body(kb, a):
            ind = pred_fn(kb, key_ref[kslice(kb), :])
            return a + jnp.sum(ind.reshape(kb_sz // 8, 8, tq), axis=0)
        a = lax.fori_loop(0, nkb, body, jnp.zeros((8, tq), I32))
        return jnp.sum(a, axis=0, keepdims=True)

    def bit_body(i, t_u):
        cand_u = t_u | jnp.left_shift(jnp.int32(1), 31 - i)
        cand_s = cand_u ^ INT_MIN
        cnt = count(lambda kb, key: jnp.where(key >= cand_s, 1, 0))
        return jnp.where(cnt >= top_k, cand_u, t_u)

    t_s = lax.fori_loop(0, 32, bit_body, jnp.zeros((1, tq), I32)) ^ INT_MIN

    def adm01(kb):
        return jnp.where(kb * kb_sz + krow < lim, 1.0, 0.0)

    def sel_body(kb, a):
        key = key_ref[kslice(kb), :]
        sel = jnp.where(key >= t_s, adm01(kb), 0.0)
        sel_ref[kslice(kb), :] = sel
        return a + jnp.sum(sel.reshape(kb_sz // 8, 8, tq), axis=0)

    n_sel = jnp.sum(lax.fori_loop(0, nkb, sel_body, jnp.zeros((8, tq), F32)), axis=0, keepdims=True)
    over = jnp.max(jnp.where(qlane < n_valid_q, n_sel, 0.0)) > top_k

    @pl.when(over)
    def _():
        n_gt = count(lambda kb, key: jnp.where(key > t_s, 1, 0))
        need = (top_k - n_gt).astype(F32)
        r = lax.broadcasted_iota(I32, (kb_sz, kb_sz), 0)
        c = lax.broadcasted_iota(I32, (kb_sz, kb_sz), 1)
        ltri = jnp.where(c < r, 1.0, 0.0).astype(BF16)

        def tie_body(kb, carry):
            key = key_ref[kslice(kb), :]
            adm = adm01(kb)
            eq = jnp.where(key == t_s, adm, 0.0)
            rank = carry + jnp.dot(ltri, eq.astype(BF16), preferred_element_type=F32)
            keep = jnp.where(rank < need, eq, 0.0)
            sel_ref[kslice(kb), :] = jnp.where(key > t_s, adm, keep)
            return carry + jnp.sum(eq, axis=0, keepdims=True)

        lax.fori_loop(0, nkb, tie_body, jnp.zeros((1, tq), F32))

    m_ref[...] = jnp.full(m_ref.shape, NEG_INF, F32)
    l_ref[...] = jnp.zeros(l_ref.shape, F32)
    acc_ref[...] = jnp.zeros(acc_ref.shape, F32)

    def attend(kb, bias_fn):
        sel = sel_ref[kslice(kb), :]
        for g in range(B_KV):
            kg = k_ref[0, kslice(kb), g * B_HD:(g + 1) * B_HD].astype(BF16)
            vg = v_ref[0, kslice(kb), g * B_HD:(g + 1) * B_HD].astype(BF16)
            for r in range(B_REP):
                h = g * B_REP + r
                s = _dot_nt(kg, q_ref[0, :, h * B_HD:(h + 1) * B_HD]) * (B_HD ** -0.5)
                s = jnp.where(sel != 0.0, s + bias_fn(h), NEG_INF)
                m_old = m_ref[h:h + 1, :]
                m_new = jnp.maximum(m_old, jnp.max(s, axis=0, keepdims=True))
                alpha = jnp.exp(m_old - m_new)
                p = jnp.exp(s - m_new)
                l_ref[h:h + 1, :] = alpha * l_ref[h:h + 1, :] + jnp.sum(p, axis=0, keepdims=True)
                hs = slice(h * B_HD, (h + 1) * B_HD)
                acc_ref[hs, :] = alpha * acc_ref[hs, :] + _dot_tn(vg, p)
                m_ref[h:h + 1, :] = m_new

    kb_near = jnp.maximum(q0 // kb_sz - 1, 0)
    far_blk = (BIAS_CENTER + MAX_DIST) // LANE
    far_bias = tab_ref[far_blk][:, 0:1]

    def far_body(kb, c):
        attend(kb, lambda h: far_bias[h:h + 1, :])
        return c

    lax.fori_loop(0, kb_near, far_body, 0)

    n_win = tq // LANE + 1

    def near_body(kb, c):
        blk0 = 1 - (kb - q0 // kb_sz)
        win = jnp.concatenate([tab_ref[blk0 + w] for w in range(n_win)], axis=1)

        def bias_fn(h):
            rows = jnp.broadcast_to(win[h:h + 1, :], (kb_sz, tq + LANE))
            return pltpu.roll(rows, 0, 1, stride=1, stride_axis=0)[:, LANE:]
        attend(kb, bias_fn)
        return c

    lax.fori_loop(kb_near, nkb, near_body, 0)

    inv_l = 1.0 / l_ref[...]
    outs = [acc_ref[h * B_HD:(h + 1) * B_HD, :] * inv_l[h:h + 1, :] for h in range(B_HEADS)]
    o_ref[0] = jnp.transpose(jnp.concatenate(outs, axis=0))


def _dsa_attention(q, qidx, kwq, kidx, k, v, tab, past, n_keys, n_valid_q, tq):
    b, tqp, d = q.shape
    lpad = kidx.shape[1]
    top_k = min(TOPK_MAX, n_keys // 4)
    kern = functools.partial(_dsa_kernel, past=past, n_keys=n_keys, n_valid_q=n_valid_q, top_k=top_k)
    return pl.pallas_call(
        kern,
        grid=(b, tqp // tq),
        in_specs=[pl.BlockSpec((1, tq, d), lambda i, j: (i, j, 0)),
                  pl.BlockSpec((1, tq, qidx.shape[2]), lambda i, j: (i, j, 0)),
                  pl.BlockSpec((1, tq, LANE), lambda i, j: (i, j, 0)),
                  pl.BlockSpec((1, lpad, IDX_DIM), lambda i, j: (i, 0, 0)),
                  pl.BlockSpec((1, lpad, B_KV * B_HD), lambda i, j: (i, 0, 0)),
                  pl.BlockSpec((1, lpad, B_KV * B_HD), lambda i, j: (i, 0, 0)),
                  pl.BlockSpec(tab.shape, lambda i, j: (0, 0, 0))],
        out_specs=pl.BlockSpec((1, tq, d), lambda i, j: (i, j, 0)),
        out_shape=jax.ShapeDtypeStruct((b, tqp, d), F32),
        scratch_shapes=[pltpu.VMEM((lpad, tq), I32), pltpu.VMEM((lpad, tq), F32),
                        pltpu.VMEM((B_HEADS, tq), F32), pltpu.VMEM((B_HEADS, tq), F32),
                        pltpu.VMEM((d, tq), F32)],
        compiler_params=_cparams("parallel", "parallel"),
        name="dsa_attention",
    )(q, qidx, kwq, kidx, k, v, tab)


def _cumsum_kernel(x_ref, o_ref):
    x = x_ref[0]
    n = x.shape[-1]
    lane = lax.broadcasted_iota(I32, x.shape, 1)
    s = 1
    while s < n:
        x = x + jnp.where(lane >= s, pltpu.roll(x, s, 1), 0.0)
        s *= 2
    o_ref[0] = x


def _cumsum_lanes(x):
    b, h, n = x.shape
    return pl.pallas_call(
        _cumsum_kernel,
        grid=(b,),
        in_specs=[pl.BlockSpec((1, h, n), lambda i: (i, 0, 0))],
        out_specs=pl.BlockSpec((1, h, n), lambda i: (i, 0, 0)),
        out_shape=jax.ShapeDtypeStruct((b, h, n), F32),
        compiler_params=_cparams("parallel"),
        name="logf_cumsum",
    )(x)


def _fox_init(m_ref, l_ref, acc_ref):
    m_ref[...] = jnp.full(m_ref.shape, NEG_INF, F32)
    l_ref[...] = jnp.zeros(l_ref.shape, F32)
    acc_ref[...] = jnp.zeros(acc_ref.shape, F32)


def _fox_tile(z_fn, pv_fn, cq, ck, mask, m_ref, l_ref, acc_ref):
    c1 = (D_HD ** -0.5) * LOG2E
    cq2, ck2 = cq * LOG2E, ck * LOG2E
    z = z_fn(0)
    for h in range(D_HEADS):
        z_next = z_fn(h + 1) if h + 1 < D_HEADS else None
        a = z * c1 - ck2[:, h:h + 1]
        if mask is not None:
            a = jnp.where(mask, a, NEG_INF)
        cqh = cq2[h:h + 1, :]
        m_old = m_ref[h:h + 1, :]
        m_new = jnp.maximum(m_old, jnp.max(a, axis=0, keepdims=True) + cqh)
        alpha = jnp.exp2(m_old - m_new)
        p = jnp.exp2(a - (m_new - cqh))
        l_ref[h:h + 1, :] = alpha * l_ref[h:h + 1, :] + jnp.sum(p, axis=0, keepdims=True)
        hs = slice(h * D_HD, (h + 1) * D_HD)
        acc_ref[hs, :] = alpha * acc_ref[hs, :] + pv_fn(h, p.astype(BF16))
        m_ref[h:h + 1, :] = m_new
        z = z_next


def _fox_finish(o_ref, l_ref, acc_ref):
    inv_l = 1.0 / l_ref[...]
    outs = [acc_ref[h * D_HD:(h + 1) * D_HD, :] * inv_l[h:h + 1, :] for h in range(D_HEADS)]
    o_ref[0] = jnp.transpose(jnp.concatenate(outs, axis=0))


def _hs(h):
    return slice(h * D_HD, (h + 1) * D_HD)


def _fox_prompt_kernel(qt_ref, k_ref, vt_ref, cq_ref, ck_ref, o_ref, m_ref, l_ref, acc_ref):
    qi, ki = pl.program_id(1), pl.program_id(2)
    tq, tk = qt_ref.shape[2], k_ref.shape[1]
    q0, k0 = qi * tq, ki * tk

    @pl.when(ki == 0)
    def _():
        _fox_init(m_ref, l_ref, acc_ref)

    def run(masked):
        mask = None
        if masked:
            mask = (k0 + lax.broadcasted_iota(I32, (tk, tq), 0)) <= (q0 + lax.broadcasted_iota(I32, (tk, tq), 1))
        _fox_tile(lambda h: jnp.dot(k_ref[0, :, _hs(h)], qt_ref[0, _hs(h), :], preferred_element_type=F32),
                  lambda h, p: jnp.dot(vt_ref[0, _hs(h), :], p, preferred_element_type=F32),
                  cq_ref[0], ck_ref[0], mask, m_ref, l_ref, acc_ref)

    fully_visible = k0 + tk - 1 <= q0
    pl.when(fully_visible)(lambda: run(False))
    pl.when(jnp.logical_and(jnp.logical_not(fully_visible), k0 <= q0 + tq - 1))(lambda: run(True))

    @pl.when(ki == pl.num_programs(2) - 1)
    def _():
        _fox_finish(o_ref, l_ref, acc_ref)


def _fox_attention_prompt(qt, k, vt, cum_t, cum, tq, tk):
    b, d, t = qt.shape
    nq, nk = t // tq, t // tk
    last = lambda j: (j * tq + tq - 1) // tk
    return pl.pallas_call(
        _fox_prompt_kernel,
        grid=(b, nq, nk),
        in_specs=[pl.BlockSpec((1, d, tq), lambda i, j, kk: (i, 0, j)),
                  pl.BlockSpec((1, tk, d), lambda i, j, kk: (i, jnp.minimum(kk, last(j)), 0)),
                  pl.BlockSpec((1, d, tk), lambda i, j, kk: (i, 0, jnp.minimum(kk, last(j)))),
                  pl.BlockSpec((1, D_HEADS, tq), lambda i, j, kk: (i, 0, j)),
                  pl.BlockSpec((1, tk, D_HEADS), lambda i, j, kk: (i, jnp.minimum(kk, last(j)), 0))],
        out_specs=pl.BlockSpec((1, tq, d), lambda i, j, kk: (i, j, 0)),
        out_shape=jax.ShapeDtypeStruct((b, t, d), F32),
        scratch_shapes=[pltpu.VMEM((D_HEADS, tq), F32), pltpu.VMEM((D_HEADS, tq), F32),
                        pltpu.VMEM((d, tq), F32)],
        compiler_params=_cparams("parallel", "parallel", "arbitrary"),
        name="fox_attention",
    )(qt, k, vt, cum_t, cum)


def _fox_cached_kernel(q_ref, kp_ref, vp_ref, kn_ref, vn_ref, cq_ref, ckp_ref, ckn_ref, o_ref,
                       m_ref, l_ref, acc_ref, *, t_new):
    ki = pl.program_id(1)
    n_past = pl.num_programs(1) - 1
    rows = q_ref.shape[1]
    c1 = (D_HD ** -0.5) * LOG2E

    @pl.when(ki == 0)
    def _():
        m_ref[...] = jnp.full(m_ref.shape, NEG_INF, F32)
        l_ref[...] = jnp.zeros(l_ref.shape, F32)
        acc_ref[...] = jnp.zeros(acc_ref.shape, F32)

    def tile(k2d, v2d, ck_row, causal):
        cols = k2d.shape[0]
        a = _dot_nt(q_ref[0], k2d) * c1 - ck_row * LOG2E
        r = lax.broadcasted_iota(I32, (rows, cols), 0)
        c = lax.broadcasted_iota(I32, (rows, cols), 1)
        ok = (c % D_HEADS) == (r // t_new)
        if causal:
            ok = jnp.logical_and(ok, (c // D_HEADS) <= (r % t_new))
        a = jnp.where(ok, a, NEG_INF)
        cq2 = cq_ref[0] * LOG2E
        m_old = m_ref[...]
        m_new = jnp.maximum(m_old, jnp.max(a, axis=1, keepdims=True) + cq2)
        alpha = jnp.exp2(m_old - m_new)
        p = jnp.exp2(a - (m_new - cq2))
        l_ref[...] = alpha * l_ref[...] + jnp.sum(p, axis=1, keepdims=True)
        acc_ref[...] = alpha * acc_ref[...] + _dot(p, v2d)
        m_ref[...] = m_new

    @pl.when(ki < n_past)
    def _():
        tk = kp_ref.shape[1]
        tile(kp_ref[0].reshape(tk * D_HEADS, D_HD), vp_ref[0].reshape(tk * D_HEADS, D_HD), ckp_ref[0], False)

    @pl.when(ki == n_past)
    def _():
        tile(kn_ref[0].reshape(t_new * D_HEADS, D_HD), vn_ref[0].reshape(t_new * D_HEADS, D_HD), ckn_ref[0], True)
        o_ref[0] = acc_ref[...] / l_ref[...]


def _fox_attention_cached(q_rows, k_past, v_past, k_new, v_new, cq_col, ck_past, ck_new, tk):
    b, rows, hd = q_rows.shape
    past, t_new = k_past.shape[1], k_new.shape[1]
    n_past = past // tk
    pidx = lambda i, kk: (i, jnp.minimum(kk, n_past - 1), 0, 0)
    return pl.pallas_call(
        functools.partial(_fox_cached_kernel, t_new=t_new),
        grid=(b, n_past + 1),
        in_specs=[pl.BlockSpec((1, rows, hd), lambda i, kk: (i, 0, 0)),
                  pl.BlockSpec((1, tk, D_HEADS, D_HD), pidx),
                  pl.BlockSpec((1, tk, D_HEADS, D_HD), pidx),
                  pl.BlockSpec((1, t_new, D_HEADS, D_HD), lambda i, kk: (i, 0, 0, 0)),
                  pl.BlockSpec((1, t_new, D_HEADS, D_HD), lambda i, kk: (i, 0, 0, 0)),
                  pl.BlockSpec((1, rows, 1), lambda i, kk: (i, 0, 0)),
                  pl.BlockSpec((1, 1, tk * D_HEADS), lambda i, kk: (i, 0, jnp.minimum(kk, n_past - 1))),
                  pl.BlockSpec((1, 1, t_new * D_HEADS), lambda i, kk: (i, 0, 0))],
        out_specs=pl.BlockSpec((1, rows, hd), lambda i, kk: (i, 0, 0)),
        out_shape=jax.ShapeDtypeStruct((b, rows, hd), F32),
        scratch_shapes=[pltpu.VMEM((rows, 1), F32), pltpu.VMEM((rows, 1), F32), pltpu.VMEM((rows, hd), F32)],
        compiler_params=_cparams("parallel", "arbitrary"),
        name="fox_attention_cached",
    )(q_rows, k_past, v_past, k_new, v_new, cq_col, ck_past, ck_new)


def _pad_rows(a, rows):
    if a.shape[1] == rows:
        return a
    return jnp.pad(a, ((0, 0), (0, rows - a.shape[1])) + ((0, 0),) * (a.ndim - 2))


def _round_up(n, m):
    return -(-n // m) * m


def _dsa_mixer(x, g, k_past, v_past, ki_past, w, w_out, tab):
    b, t, d = x.shape
    past = k_past.shape[1]
    n_keys = past + t
    q, k, v, k4, v4, qidx, kw = _proj(
        x, g, w, ((0, "rows", BF16, None), (1, "rows", F32, None), (2, "rows", F32, None),
                  (1, B_HD, F32, None), (2, B_HD, F32, None), (3, "rows", BF16, None), (4, "rows", F32, None)),
        jnp.zeros((1, LANE), F32))
    ki = kw[:, :, :IDX_DIM]
    tq = min(2 * LANE, _round_up(t, LANE))
    tqp = _round_up(t, tq)
    lpad = _round_up(n_keys, LANE)
    cat = lambda p, c: _pad_rows(jnp.concatenate([p.reshape(b, past, c.shape[-1]), c], axis=1), lpad)
    o = _dsa_attention(_pad_rows(q, tqp), _pad_rows(qidx, tqp),
                       _pad_rows(kw, tqp), cat(ki_past, ki), cat(k_past, k), cat(v_past, v),
                       tab, past, n_keys, t, tq)
    y = _out_proj(x.reshape(b * t, d), o[:, :t].reshape(b * t, d), w_out).reshape(b, t, d)
    return (y, k4, v4, ki)


def _fox_mixer(x, g, k_past, v_past, lf_past, w, b_f, w_out):
    b, t, d = x.shape
    past = k_past.shape[1]
    heads4 = ((1, D_HD, F32, None), (2, D_HD, F32, None), (3, "rows", F32, "log_sigmoid"))
    if past == 0:
        tq = tk = 2 * LANE
        k4, v4, lf, qt, kb, vt = _proj(
            x, g, w, heads4 + ((0, "t", BF16, None), (1, "rows", BF16, None), (2, "t", BF16, None)), b_f)
        logf = lf[:, :, :D_HEADS]
        cum_t = _cumsum_lanes(jnp.swapaxes(logf, 1, 2))
        o = _fox_attention_prompt(qt, kb, vt, cum_t, jnp.swapaxes(cum_t, 1, 2), tq, tk)
    else:
        tk = 2 * LANE
        k4, v4, lf, q = _proj(x, g, w, heads4 + ((0, "rows", BF16, None),), b_f)
        logf = lf[:, :, :D_HEADS]
        lf_all = _pad_rows(jnp.concatenate([lf_past, logf], axis=1), _round_up(past + t, LANE))
        cum = jnp.swapaxes(_cumsum_lanes(jnp.swapaxes(lf_all, 1, 2)), 1, 2)[:, :past + t]
        ck = cum.reshape(b, 1, (past + t) * D_HEADS)
        to_rows = lambda a: jnp.swapaxes(a.reshape(b, t, D_HEADS, -1), 1, 2).reshape(b, D_HEADS * t, -1)
        o = _fox_attention_cached(to_rows(q), k_past, v_past, k4, v4, to_rows(cum[:, past:]),
                                  ck[:, :, :past * D_HEADS], ck[:, :, past * D_HEADS:], tk)
        o = jnp.swapaxes(o.reshape(b, D_HEADS, t, D_HD), 1, 2).reshape(b, t, d)
    y = _out_proj(x.reshape(b * t, d), o.reshape(b * t, d), w_out).reshape(b, t, d)
    return (y, k4, v4, logf)


def _run_group(x, pos0, a_st, b_k, b_v, b_ki, c_st, d_k, d_v, d_lf, mem_k, mem_v, prm):
    b, t, d = x.shape
    depth = prm["norm_mix"].shape[0]
    new = {n: [] for n in ("a", "bk", "bv", "bki", "c", "dk", "dv", "dlf")}
    for i in range(depth):
        kind, j = i % 4, i // 4
        g = prm["norm_mix"][i]
        if kind == 0:
            x, st = _conv_mixer(x, g, prm["a_w_in"][j], prm["a_conv"][j], a_st[j], prm["a_w_out"][j])
            new["a"].append(st)
        elif kind == 1:
            x, kk, vv, ki = _dsa_mixer(x, g, b_k[j], b_v[j], b_ki[j], prm["b_w"][j], prm["b_w_out"][j],
                                       prm["bias_tab"])
            new["bk"].append(kk); new["bv"].append(vv); new["bki"].append(ki)
        elif kind == 2:
            x, st = _pool_mixer(x, g, c_st[j], prm["c_w_group"][j], prm["c_scale"][j], pos0)
            new["c"].append(st)
        else:
            x, kk, vv, lf = _fox_mixer(x, g, d_k[j], d_v[j], d_lf[j], prm["d_w"][j], prm["d_b_f"][j],
                                       prm["d_w_out"][j])
            new["dk"].append(kk); new["dv"].append(vv); new["dlf"].append(lf)
        x = _xattn(x, prm["norm_xattn"], prm["xa_wq"], mem_k, mem_v, prm["xa_wo"], i)
        last = i == depth - 1
        x = _ffn(x.reshape(b * t, d), prm["norm_ffn"], prm["ffn_w1"], prm["ffn_w2"],
                 prm["final_norm"], i, last).reshape(b, t, d)
    return (x,) + tuple(jnp.stack(new[n]) for n in ("a", "bk", "bv", "bki", "c", "dk", "dv", "dlf"))


def kernel(x_prompt, x_sample, state_a_conv, cache_b_k, cache_b_v, cache_b_kidx, state_c_pool,
           cache_d_k, cache_d_v, cache_d_logf, cache_mem_k, cache_mem_v, mem_prompt,
           norm_mix, norm_xattn, norm_mem, norm_ffn, final_norm,
           a_w_in, a_conv, a_w_out, b_w_in, b_w_out, rel_bias, c_w_group, c_scale,
           d_w_in, d_b_f, d_w_out, xa_wq, xa_wkv, xa_wo, ffn_w1, ffn_w2):
    bp, t, d = x_prompt.shape
    depth = norm_mix.shape[0]
    n_b, n_d = b_w_in.shape[0], d_w_in.shape[0]
    bf = lambda w: w.astype(BF16)

    def split_cols(w, widths):
        out, c = [], 0
        for wd in widths:
            piece = w[:, c:c + wd]
            c += wd
            if wd % LANE:
                piece = jnp.pad(piece, ((0, 0), (0, _round_up(wd, LANE) - wd)))
            out.append(bf(piece))
        assert c == w.shape[1]
        return out

    b_q, b_kvw = B_HEADS * B_HD, B_KV * B_HD
    b_w = [split_cols(b_w_in[j], (b_q, b_kvw, b_kvw, IDX_HEADS * IDX_DIM, IDX_DIM + IDX_HEADS))
           for j in range(n_b)]
    d_w = [split_cols(d_w_in[j], (d, d, d, D_HEADS)) for j in range(n_d)]
    d_bf = [jnp.pad(d_b_f[j], (0, LANE - D_HEADS)).reshape(1, LANE) for j in range(n_d)]
    bias_tab = _bias_table(rel_bias, 5 * LANE)

    prm = {"norm_mix": norm_mix, "norm_xattn": norm_xattn, "norm_ffn": norm_ffn, "final_norm": final_norm,
           "a_w_in": bf(a_w_in), "a_conv": a_conv, "a_w_out": bf(a_w_out),
           "b_w": b_w, "b_w_out": bf(b_w_out), "bias_tab": bias_tab,
           "c_w_group": bf(c_w_group), "c_scale": c_scale,
           "d_w": d_w, "d_b_f": d_bf, "d_w_out": bf(d_w_out),
           "xa_wq": bf(xa_wq), "xa_wo": bf(xa_wo), "ffn_w1": bf(ffn_w1), "ffn_w2": bf(ffn_w2)}

    n_mem = mem_prompt.shape[1]
    mk, mv = _memory_kv(mem_prompt, norm_mem, bf(xa_wkv))

    n_a, n_c = a_w_in.shape[0], c_w_group.shape[0]
    z = lambda *s: jnp.zeros(s, F32)
    gp = _run_group(x_prompt, 0,
                    z(n_a, bp, CONV_W - 1, d),
                    z(n_b, bp, 0, B_KV, B_HD), z(n_b, bp, 0, B_KV, B_HD), z(n_b, bp, 0, IDX_DIM),
                    z(n_c, bp, POOL_STATE, d),
                    z(n_d, bp, 0, D_HEADS, D_HD), z(n_d, bp, 0, D_HEADS, D_HD), z(n_d, bp, 0, D_HEADS),
                    mk, mv, prm)

    bs = x_sample.shape[0]
    past_len = cache_b_k.shape[2]
    gs = _run_group(x_sample, past_len, state_a_conv, cache_b_k, cache_b_v, cache_b_kidx, state_c_pool,
                    cache_d_k, cache_d_v, cache_d_logf, cache_mem_k, cache_mem_v, prm)

    (y_p, a_p, bk_p, bv_p, bki_p, c_p, dk_p, dv_p, dlf_p) = gp
    (y_s, a_s, bk_s, bv_s, bki_s, c_s, dk_s, dv_s, dlf_s) = gs
    return (y_p, y_s, a_p, a_s, bk_p, bv_p, bki_p, bk_s, bv_s, bki_s, c_p, c_s,
            dk_p, dv_p, dlf_p, dk_s, dv_s, dlf_s, mk, mv)
```

```python
import functools
import math

import jax
import jax.numpy as jnp
from jax import lax
from jax.experimental import pallas as pl
from jax.experimental.pallas import tpu as pltpu

F32 = jnp.float32
BF16 = jnp.bfloat16
I32 = jnp.int32

EPS = 1e-6
NEG_INF = -1e30
LOG2E = math.log2(math.e)
CHUNK = 64
LANE = 128
VMEM_LIMIT = 48 * 1024 * 1024

CONV_W = 3
POOL_WINDOWS = (2, 4, 8, 16)
POOL_STATE = max(POOL_WINDOWS) - 1
B_HEADS, B_KV, B_HD = 8, 2, 128
B_REP = B_HEADS // B_KV
IDX_HEADS, IDX_DIM = 8, 64
TOPK_MAX = 256
N_BUCKETS, MAX_DIST = 32, 128
D_HEADS, D_HD = 8, 128
MEM_HEADS = 4
INT_MIN = -2147483648
BIAS_CENTER = 2 * LANE


def _cparams(*sem):
    return pltpu.CompilerParams(dimension_semantics=sem, vmem_limit_bytes=VMEM_LIMIT)


def _dot(a, b):
    return jnp.dot(a.astype(BF16), b.astype(BF16), preferred_element_type=F32)


def _dot_nt(a, b):
    return lax.dot_general(a.astype(BF16), b.astype(BF16), (((1,), (1,)), ((), ())),
                           preferred_element_type=F32)


def _dot_tn(a, b):
    return lax.dot_general(a.astype(BF16), b.astype(BF16), (((0,), (0,)), ((), ())),
                           preferred_element_type=F32)


def _rms(x, g):
    return x * lax.rsqrt(jnp.mean(x * x, axis=-1, keepdims=True) + EPS) * g


def _row_tile(n, cap):
    t = min(n, cap)
    assert n % t == 0
    return t


def _memkv_kernel(mem_ref, g_ref, w_ref, k_ref, v_ref, kb_ref, vb_ref):
    m = mem_ref[0]
    mn = m * lax.rsqrt(jnp.mean(m * m, axis=-1, keepdims=True) + EPS)
    h = (mn * g_ref[0]).astype(BF16)
    d = m.shape[-1]
    hd = d // MEM_HEADS
    k = jnp.dot(h, w_ref[0, :, :d], preferred_element_type=F32)
    v = jnp.dot(h, w_ref[0, :, d:], preferred_element_type=F32)
    kb_ref[0, 0] = k.astype(BF16)
    vb_ref[0, 0] = v.astype(BF16)
    for hh in range(MEM_HEADS):
        k_ref[0, 0, :, hh, :] = k[:, hh * hd:(hh + 1) * hd]
        v_ref[0, 0, :, hh, :] = v[:, hh * hd:(hh + 1) * hd]


def _memory_kv(mem, g_mem, w_kv):
    depth, d = g_mem.shape
    b, nm, _ = mem.shape
    hd = d // MEM_HEADS
    out = jax.ShapeDtypeStruct((depth, b, nm, MEM_HEADS, hd), F32)
    out_b = jax.ShapeDtypeStruct((depth, b, nm, d), BF16)
    heads_spec = pl.BlockSpec((1, 1, nm, MEM_HEADS, hd), lambda l, i: (l, i, 0, 0, 0))
    rows_spec = pl.BlockSpec((1, 1, nm, d), lambda l, i: (l, i, 0, 0))
    return pl.pallas_call(
        _memkv_kernel,
        grid=(depth, b),
        in_specs=[pl.BlockSpec((1, nm, d), lambda l, i: (i, 0, 0)),
                  pl.BlockSpec((1, 1, d), lambda l, i: (l, 0, 0)),
                  pl.BlockSpec((1, d, 2 * d), lambda l, i: (l, 0, 0))],
        out_specs=[heads_spec, heads_spec, rows_spec, rows_spec],
        out_shape=[out, out, out_b, out_b],
        compiler_params=_cparams("parallel", "parallel"),
        name="memory_kv",
    )(mem, g_mem.reshape(depth, 1, d), w_kv)


def _ffn_kernel(x_ref, g_ref, w1_ref, w2_ref, gf_ref, o_ref, h_ref, acc_ref, *, final_norm):
    j = pl.program_id(1)

    @pl.when(j == 0)
    def _():
        h_ref[...] = _rms(x_ref[...], g_ref[...]).astype(BF16)
        acc_ref[...] = jnp.zeros_like(acc_ref)

    u = jnp.maximum(jnp.dot(h_ref[...], w1_ref[...], preferred_element_type=F32), 0.0)
    acc_ref[...] += jnp.dot((u * u).astype(BF16), w2_ref[...], preferred_element_type=F32)

    @pl.when(j == pl.num_programs(1) - 1)
    def _():
        y = x_ref[...] + acc_ref[...]
        o_ref[...] = _rms(y, gf_ref[...]) if final_norm else y


def _ffn(x, g, w1, w2, gf, layer, final_norm):
    n, d = x.shape
    f = w1.shape[2]
    tm = _row_tile(n, 1024)
    tf = 1024
    return pl.pallas_call(
        functools.partial(_ffn_kernel, final_norm=final_norm),
        grid=(n // tm, f // tf),
        in_specs=[pl.BlockSpec((tm, d), lambda i, j: (i, 0)),
                  pl.BlockSpec((None, 1, d), lambda i, j: (layer, 0, 0)),
                  pl.BlockSpec((None, d, tf), lambda i, j: (layer, 0, j)),
                  pl.BlockSpec((None, tf, d), lambda i, j: (layer, j, 0)),
                  pl.BlockSpec((1, d), lambda i, j: (0, 0))],
        out_specs=pl.BlockSpec((tm, d), lambda i, j: (i, 0)),
        out_shape=jax.ShapeDtypeStruct((n, d), F32),
        scratch_shapes=[pltpu.VMEM((tm, d), BF16), pltpu.VMEM((tm, d), F32)],
        compiler_params=_cparams("parallel", "arbitrary"),
        name="ffn",
    )(x, g.reshape(-1, 1, d), w1, w2, gf.reshape(1, d))


def _xattn_kernel(x_ref, g_ref, wq_ref, mk_ref, mv_ref, wo_ref, o_ref):
    x = x_ref[0]
    d = x.shape[-1]
    hd = d // MEM_HEADS
    h = _rms(x, g_ref[...]).astype(BF16)
    q = jnp.dot(h, wq_ref[...], preferred_element_type=F32)
    outs = []
    by_lanes = mk_ref.shape[1] == LANE
    pieces = hd // LANE

    def head(ref, hh):
        if not by_lanes:
            return ref[:, hh * hd:(hh + 1) * hd]
        n_rows = ref.shape[0] * LANE // d
        return jnp.concatenate([ref[pl.ds(hh * pieces + c, n_rows, stride=d // LANE), :]
                                for c in range(pieces)], axis=1)

    for hh in range(MEM_HEADS):
        sl = slice(hh * hd, (hh + 1) * hd)
        kh, vh = head(mk_ref, hh), head(mv_ref, hh)
        s = _dot_nt(q[:, sl], kh) * (hd ** -0.5)
        m = jnp.max(s, axis=-1, keepdims=True)
        p = jnp.exp(s - m)
        l = jnp.sum(p, axis=-1, keepdims=True)
        outs.append(_dot(p, vh) / l)
    o = jnp.concatenate(outs, axis=-1)
    o_ref[0] = x + _dot(o, wo_ref[...])


def _xattn(x, g, wq, mk, mv, wo, layer):
    b, t, d = x.shape
    tm = _row_tile(t, 512)
    if mk.ndim == 5:
        mk, mv = (a.reshape(a.shape[:2] + (a.shape[2] * d // LANE, LANE)) for a in (mk, mv))
    kv_spec = pl.BlockSpec((None, None) + mk.shape[2:], lambda i, j: (layer, i, 0, 0))
    return pl.pallas_call(
        _xattn_kernel,
        grid=(b, t // tm),
        in_specs=[pl.BlockSpec((1, tm, d), lambda i, j: (i, j, 0)),
                  pl.BlockSpec((None, 1, d), lambda i, j: (layer, 0, 0)),
                  pl.BlockSpec((None, d, d), lambda i, j: (layer, 0, 0)),
                  kv_spec, kv_spec,
                  pl.BlockSpec((None, d, d), lambda i, j: (layer, 0, 0))],
        out_specs=pl.BlockSpec((1, tm, d), lambda i, j: (i, j, 0)),
        out_shape=jax.ShapeDtypeStruct((b, t, d), F32),
        compiler_params=_cparams("parallel", "parallel"),
        name="xattn",
    )(x, g.reshape(-1, 1, d), wq, mk, mv, wo)


def _conv_kernel(x_ref, g_ref, win_ref, wc_ref, st_ref, wout_ref, o_ref, nst_ref, z_ref):
    t = pl.program_id(1)
    x = x_ref[0]
    tm, d = x.shape
    pad = 8

    @pl.when(t == 0)
    def _():
        z_ref[pad - 2:pad, :] = st_ref[0]

    h = _rms(x, g_ref[...]).astype(BF16)
    bg = jnp.dot(h, win_ref[:, 0:d], preferred_element_type=F32)
    cg = jnp.dot(h, win_ref[:, d:2 * d], preferred_element_type=F32)
    u = jnp.dot(h, win_ref[:, 2 * d:3 * d], preferred_element_type=F32)
    z = cg * u
    z_ref[pad:pad + tm, :] = z
    conv = (z_ref[pad - 2:pad - 2 + tm, :] * wc_ref[0:1, :]
            + z_ref[pad - 1:pad - 1 + tm, :] * wc_ref[1:2, :]
            + z * wc_ref[2:3, :])
    o_ref[0] = x + _dot(bg * conv, wout_ref[...])
    last = z_ref[pad + tm - 2:pad + tm, :]
    z_ref[pad - 2:pad, :] = last

    @pl.when(t == pl.num_programs(1) - 1)
    def _():
        nst_ref[0] = last


def _conv_mixer(x, g, w_in, w_conv, state, w_out):
    b, t, d = x.shape
    tm = _row_tile(t, 512)
    return pl.pallas_call(
        _conv_kernel,
        grid=(b, t // tm),
        in_specs=[pl.BlockSpec((1, tm, d), lambda i, j: (i, j, 0)),
                  pl.BlockSpec((1, d), lambda i, j: (0, 0)),
                  pl.BlockSpec((d, 3 * d), lambda i, j: (0, 0)),
                  pl.BlockSpec((CONV_W, d), lambda i, j: (0, 0)),
                  pl.BlockSpec((1, CONV_W - 1, d), lambda i, j: (i, 0, 0)),
                  pl.BlockSpec((d, d), lambda i, j: (0, 0))],
        out_specs=[pl.BlockSpec((1, tm, d), lambda i, j: (i, j, 0)),
                   pl.BlockSpec((1, CONV_W - 1, d), lambda i, j: (i, 0, 0))],
        out_shape=[jax.ShapeDtypeStruct((b, t, d), F32),
                   jax.ShapeDtypeStruct((b, CONV_W - 1, d), F32)],
        scratch_shapes=[pltpu.VMEM((tm + 8, d), F32)],
        compiler_params=_cparams("parallel", "arbitrary"),
        name="conv_mixer",
    )(x, g.reshape(1, d), w_in, w_conv, state, w_out)


def _pool_kernel(x_ref, g_ref, st_ref, wg_ref, sc_ref, o_ref, nst_ref, h_ref, *, pos0):
    t = pl.program_id(1)
    x = x_ref[0]
    tm, d = x.shape
    gw = d // len(POOL_WINDOWS)
    base = POOL_STATE + 1

    @pl.when(t == 0)
    def _():
        h_ref[1:base, :] = st_ref[0]

    h = _rms(x, g_ref[...])
    h_ref[base:base + tm, :] = h
    pos = pos0 + t * tm + lax.broadcasted_iota(I32, (tm, gw), 0)
    ys = []
    for gi, w in enumerate(POOL_WINDOWS):
        sl = slice(gi * gw, (gi + 1) * gw)
        win = h[:, sl]
        for j in range(1, w):
            win = win + h_ref[base - j:base - j + tm, sl]
        count = jnp.minimum(w, pos + 1).astype(F32)
        dlt = win / count - h[:, sl]
        ys.append(_dot(dlt, wg_ref[gi]))
    y = jnp.concatenate(ys, axis=-1) * sc_ref[...]
    o_ref[0] = x + y
    last = h_ref[tm + 1:tm + base, :]
    h_ref[1:base, :] = last

    @pl.when(t == pl.num_programs(1) - 1)
    def _():
        nst_ref[0] = last


def _pool_mixer(x, g, state, w_group, scale, pos0):
    b, t, d = x.shape
    ng, gw, _ = w_group.shape
    tm = _row_tile(t, 512)
    return pl.pallas_call(
        functools.partial(_pool_kernel, pos0=pos0),
        grid=(b, t // tm),
        in_specs=[pl.BlockSpec((1, tm, d), lambda i, j: (i, j, 0)),
                  pl.BlockSpec((1, d), lambda i, j: (0, 0)),
                  pl.BlockSpec((1, POOL_STATE, d), lambda i, j: (i, 0, 0)),
                  pl.BlockSpec((ng, gw, gw), lambda i, j: (0, 0, 0)),
                  pl.BlockSpec((1, d), lambda i, j: (0, 0))],
        out_specs=[pl.BlockSpec((1, tm, d), lambda i, j: (i, j, 0)),
                   pl.BlockSpec((1, POOL_STATE, d), lambda i, j: (i, 0, 0))],
        out_shape=[jax.ShapeDtypeStruct((b, t, d), F32),
                   jax.ShapeDtypeStruct((b, POOL_STATE, d), F32)],
        scratch_shapes=[pltpu.VMEM((tm + POOL_STATE + 1, d), F32)],
        compiler_params=_cparams("parallel", "arbitrary"),
        name="pool_mixer",
    )(x, g.reshape(1, d), state, w_group, scale.reshape(1, d))


def _proj_kernel(*refs, n_w, outs):
    x_ref, g_ref = refs[0], refs[1]
    w_refs = refs[2:2 + n_w]
    e_ref = refs[2 + n_w]
    o_refs = refs[3 + n_w:]
    h = _rms(x_ref[0], g_ref[...]).astype(BF16)
    ys = {}
    for (wi, mode, _, ep), o_ref in zip(outs, o_refs):
        if wi not in ys:
            ys[wi] = jnp.dot(h, w_refs[wi][...], preferred_element_type=F32)
        y = ys[wi]
        if ep == "log_sigmoid":
            u = -(y + e_ref[...])
            y = -(jnp.maximum(u, 0.0) + jnp.log1p(jnp.exp(-jnp.abs(u))))
        if mode == "rows":
            o_ref[0] = y.astype(o_ref.dtype)
        elif mode == "t":
            o_ref[0] = jnp.transpose(y).astype(o_ref.dtype)
        else:
            for hh in range(y.shape[1] // mode):
                o_ref[0, :, hh, :] = y[:, hh * mode:(hh + 1) * mode].astype(o_ref.dtype)


def _proj(x, g, ws, outs, extra):
    b, t, d = x.shape
    tm = _row_tile(t, 512)
    in_specs = [pl.BlockSpec((1, tm, d), lambda i, j: (i, j, 0)), pl.BlockSpec((1, d), lambda i, j: (0, 0))]
    in_specs += [pl.BlockSpec(w.shape, lambda i, j: (0, 0)) for w in ws]
    in_specs += [pl.BlockSpec(extra.shape, lambda i, j: (0, 0))]
    out_specs, out_shape = [], []
    for wi, mode, dt, _ in outs:
        n = ws[wi].shape[1]
        if mode == "rows":
            out_specs.append(pl.BlockSpec((1, tm, n), lambda i, j: (i, j, 0)))
            out_shape.append(jax.ShapeDtypeStruct((b, t, n), dt))
        elif mode == "t":
            out_specs.append(pl.BlockSpec((1, n, tm), lambda i, j: (i, 0, j)))
            out_shape.append(jax.ShapeDtypeStruct((b, n, t), dt))
        else:
            out_specs.append(pl.BlockSpec((1, tm, n // mode, mode), lambda i, j: (i, j, 0, 0)))
            out_shape.append(jax.ShapeDtypeStruct((b, t, n // mode, mode), dt))
    return pl.pallas_call(
        functools.partial(_proj_kernel, n_w=len(ws), outs=tuple(outs)),
        grid=(b, t // tm),
        in_specs=in_specs,
        out_specs=out_specs,
        out_shape=out_shape,
        compiler_params=_cparams("parallel", "parallel"),
        name="norm_proj",
    )(x, g.reshape(1, d), *ws, extra)


def _outproj_kernel(x_ref, a_ref, w_ref, o_ref):
    o_ref[...] = x_ref[...] + _dot(a_ref[...], w_ref[...])


def _out_proj(x, a, w):
    n, d = x.shape
    tm = _row_tile(n, 512)
    return pl.pallas_call(
        _outproj_kernel,
        grid=(n // tm,),
        in_specs=[pl.BlockSpec((tm, d), lambda i: (i, 0)),
                  pl.BlockSpec((tm, d), lambda i: (i, 0)),
                  pl.BlockSpec((d, d), lambda i: (0, 0))],
        out_specs=pl.BlockSpec((tm, d), lambda i: (i, 0)),
        out_shape=jax.ShapeDtypeStruct((n, d), F32),
        compiler_params=_cparams("parallel"),
        name="out_proj",
    )(x, a, w)


def _bias_table_kernel(rbt_ref, o_ref):
    width = o_ref.shape[-1]
    rel = BIAS_CENTER - lax.broadcasted_iota(I32, (1, width), 1)
    nb = N_BUCKETS // 2
    max_exact = nb // 2
    ret = (rel > 0).astype(I32) * nb
    n = jnp.abs(rel)
    nf = jnp.maximum(n, 1).astype(F32)
    large = max_exact + (jnp.log(nf / max_exact) / math.log(MAX_DIST / max_exact)
                         * (nb - max_exact)).astype(I32)
    large = jnp.minimum(large, nb - 1)
    bucket = ret + jnp.where(n < max_exact, n, large)
    acc = jnp.zeros(o_ref.shape, F32)
    for j in range(N_BUCKETS):
        acc = jnp.where(bucket == j, rbt_ref[:, j:j + 1], acc)
    o_ref[...] = acc


def _bias_table(rel_bias, width):
    nh = rel_bias.shape[1]
    tab = pl.pallas_call(
        _bias_table_kernel,
        out_shape=jax.ShapeDtypeStruct((nh, width), F32),
        name="bias_table",
    )(rel_bias.T)
    return tab.reshape(nh, width // LANE, LANE).transpose(1, 0, 2)


def _sortable(x):
    x = jnp.where(x == 0.0, 0.0, x)
    bits = lax.bitcast_convert_type(x, I32)
    return jnp.where(bits < 0, bits ^ 0x7FFFFFFF, bits)


def _neg_inf_key():
    import numpy as np
    b = int(np.float32(NEG_INF).view(np.int32))
    return b ^ 0x7FFFFFFF


def _dsa_prompt_kernel(qt_ref, qit_ref, kwt_ref, kw_ref, k_ref, vt_ref, tab_ref, o_ref,
                       key_ref, sel_ref, m_ref, l_ref, acc_ref, *, top_k):
    qi = pl.program_id(1)
    tq = qt_ref.shape[2]
    n_keys = kw_ref.shape[1]
    kb_sz = LANE
    q0 = qi * tq
    nkb = jnp.minimum(n_keys, q0 + tq) // kb_sz
    negkey = _neg_inf_key()

    qlane = lax.broadcasted_iota(I32, (1, tq), 1)
    lim = ((q0 + qlane) // CHUNK + 1) * CHUNK
    krow = lax.broadcasted_iota(I32, (kb_sz, tq), 0)

    def kslice(kb):
        return pl.ds(pl.multiple_of(kb * kb_sz, kb_sz), kb_sz)

    def score_body(kb, c):
        kid = kw_ref[0, kslice(kb), :][:, :IDX_DIM].astype(BF16)
        sc = jnp.zeros((kb_sz, tq), F32)
        for h in range(IDX_HEADS):
            s = jnp.dot(kid, qit_ref[0, h * IDX_DIM:(h + 1) * IDX_DIM, :], preferred_element_type=F32)
            sc = sc + jnp.maximum(s, 0.0) * kwt_ref[0, IDX_DIM + h:IDX_DIM + h + 1, :]
        sc = sc * ((IDX_DIM * IDX_HEADS) ** -0.5)
        key_ref[kslice(kb), :] = jnp.where(kb * kb_sz + krow < lim, _sortable(sc), negkey)
        return c

    lax.fori_loop(0, nkb, score_body, 0)

    def count(pred_fn):
        def body(kb, a):
            ind = pred_fn(kb, key_ref[kslice(kb), :])
            return a + jnp.sum(ind.reshape(kb_sz // 8, 8, tq), axis=0)
        a = lax.fori_loop(0, nkb, body, jnp.zeros((8, tq), I32))
        return jnp.sum(a, axis=0, keepdims=True)

    def bit_body(i, t_u):
        cand_u = t_u | jnp.left_shift(jnp.int32(1), 31 - i)
        cand_s = cand_u ^ INT_MIN
        cnt = count(lambda kb, key: jnp.where(key >= cand_s, 1, 0))
        return jnp.where(cnt >= top_k, cand_u, t_u)

    t_s = lax.fori_loop(0, 32, bit_body, jnp.zeros((1, tq), I32)) ^ INT_MIN

    def adm01(kb):
        return jnp.where(kb * kb_sz + krow < lim, 1.0, 0.0)

    def sel_body(kb, a):
        sel = jnp.where(key_ref[kslice(kb), :] >= t_s, adm01(kb), 0.0)
        sel_ref[kslice(kb), :] = sel
        return a + jnp.sum(sel.reshape(kb_sz // 8, 8, tq), axis=0)

    n_sel = jnp.sum(lax.fori_loop(0, nkb, sel_body, jnp.zeros((8, tq), F32)), axis=0, keepdims=True)

    @pl.when(jnp.max(n_sel) > top_k)
    def _():
        n_gt = count(lambda kb, key: jnp.where(key > t_s, 1, 0))
        need = (top_k - n_gt).astype(F32)
        r = lax.broadcasted_iota(I32, (kb_sz, kb_sz), 0)
        c = lax.broadcasted_iota(I32, (kb_sz, kb_sz), 1)
        ltri = jnp.where(c < r, 1.0, 0.0).astype(BF16)

        def tie_body(kb, carry):
            key = key_ref[kslice(kb), :]
            adm = adm01(kb)
            eq = jnp.where(key == t_s, adm, 0.0)
            rank = carry + jnp.dot(ltri, eq.astype(BF16), preferred_element_type=F32)
            keep = jnp.where(rank < need, eq, 0.0)
            sel_ref[kslice(kb), :] = jnp.where(key > t_s, adm, keep)
            return carry + jnp.sum(eq, axis=0, keepdims=True)

        lax.fori_loop(0, nkb, tie_body, jnp.zeros((1, tq), F32))

    m_ref[...] = jnp.full(m_ref.shape, NEG_INF, F32)
    l_ref[...] = jnp.zeros(l_ref.shape, F32)
    acc_ref[...] = jnp.zeros(acc_ref.shape, F32)
    c1 = (B_HD ** -0.5) * LOG2E

    def attend(kb, bias2_fn):
        sel = sel_ref[kslice(kb), :] != 0.0
        ks = k_ref[0, kslice(kb), :]

        def z_fn(h):
            g = h // B_REP
            return jnp.dot(ks[:, g * B_HD:(g + 1) * B_HD], qt_ref[0, h * B_HD:(h + 1) * B_HD, :],
                           preferred_element_type=F32)

        z = z_fn(0)
        for h in range(B_HEADS):
            z_next = z_fn(h + 1) if h + 1 < B_HEADS else None
            g = h // B_REP
            a = jnp.where(sel, z * c1 + bias2_fn(h), NEG_INF)
            m_old = m_ref[h:h + 1, :]
            m_new = jnp.maximum(m_old, jnp.max(a, axis=0, keepdims=True))
            alpha = jnp.exp2(m_old - m_new)
            p = jnp.exp2(a - m_new)
            l_ref[h:h + 1, :] = alpha * l_ref[h:h + 1, :] + jnp.sum(p, axis=0, keepdims=True)
            hs = slice(h * B_HD, (h + 1) * B_HD)
            pv = jnp.dot(vt_ref[0, g * B_HD:(g + 1) * B_HD, kslice(kb)], p.astype(BF16),
                         preferred_element_type=F32)
            acc_ref[hs, :] = alpha * acc_ref[hs, :] + pv
            m_ref[h:h + 1, :] = m_new
            z = z_next

    kb_near = jnp.maximum(q0 // kb_sz - 1, 0)
    far_blk = (BIAS_CENTER + MAX_DIST) // LANE
    far_bias2 = tab_ref[far_blk][:, 0:1] * LOG2E

    def far_body(kb, c):
        attend(kb, lambda h: far_bias2[h:h + 1, :])
        return c

    lax.fori_loop(0, kb_near, far_body, 0)

    n_win = tq // LANE + 1

    def near_body(kb, c):
        blk0 = 1 - (kb - q0 // kb_sz)
        win = jnp.concatenate([tab_ref[blk0 + w] for w in range(n_win)], axis=1) * LOG2E

        def bias2_fn(h):
            rows = jnp.broadcast_to(win[h:h + 1, :], (kb_sz, tq + LANE))
            return pltpu.roll(rows, 0, 1, stride=1, stride_axis=0)[:, LANE:]
        attend(kb, bias2_fn)
        return c

    lax.fori_loop(kb_near, nkb, near_body, 0)

    inv_l = 1.0 / l_ref[...]
    outs = [acc_ref[h * B_HD:(h + 1) * B_HD, :] * inv_l[h:h + 1, :] for h in range(B_HEADS)]
    o_ref[0] = jnp.transpose(jnp.concatenate(outs, axis=0))


def _dsa_attention_prompt(qt, qit, kwt, kw, k, vt, tab, tq):
    b, d, t = qt.shape
    top_k = min(TOPK_MAX, t // 4)
    assert t % tq == 0 and tq % CHUNK == 0 and tq % LANE == 0
    return pl.pallas_call(
        functools.partial(_dsa_prompt_kernel, top_k=top_k),
        grid=(b, t // tq),
        in_specs=[pl.BlockSpec((1, d, tq), lambda i, j: (i, 0, j)),
                  pl.BlockSpec((1, qit.shape[1], tq), lambda i, j: (i, 0, j)),
                  pl.BlockSpec((1, LANE, tq), lambda i, j: (i, 0, j)),
                  pl.BlockSpec((1, t, LANE), lambda i, j: (i, 0, 0)),
                  pl.BlockSpec((1, t, B_KV * B_HD), lambda i, j: (i, 0, 0)),
                  pl.BlockSpec((1, B_KV * B_HD, t), lambda i, j: (i, 0, 0)),
                  pl.BlockSpec(tab.shape, lambda i, j: (0, 0, 0))],
        out_specs=pl.BlockSpec((1, tq, d), lambda i, j: (i, j, 0)),
        out_shape=jax.ShapeDtypeStruct((b, t, d), F32),
        scratch_shapes=[pltpu.VMEM((t, tq), I32), pltpu.VMEM((t, tq), F32),
                        pltpu.VMEM((B_HEADS, tq), F32), pltpu.VMEM((B_HEADS, tq), F32),
                        pltpu.VMEM((d, tq), F32)],
        compiler_params=_cparams("parallel", "parallel"),
        name="dsa_attention",
    )(qt, qit, kwt, kw, k, vt, tab)


def _dsa_cached_kernel(q_ref, qi_ref, wi_ref, rb_ref, kidx_ref, kp_ref, vp_ref, kn_ref, vn_ref, o_ref,
                       *, past, t_new, top_k):
    n_keys = past + t_new
    lp = kidx_ref.shape[1]
    negkey = _neg_inf_key()
    kpos = lax.broadcasted_iota(I32, (t_new, lp), 1)
    qpos = past + lax.broadcasted_iota(I32, (t_new, lp), 0)
    adm = kpos < (qpos // CHUNK + 1) * CHUNK

    s = _dot_nt(qi_ref[0], kidx_ref[0])
    w = jnp.maximum(s, 0.0) * wi_ref[0]
    sc = w[0:t_new]
    for h in range(1, IDX_HEADS):
        sc = sc + w[h * t_new:(h + 1) * t_new]
    sc = sc * ((IDX_DIM * IDX_HEADS) ** -0.5)
    key = jnp.where(adm, _sortable(sc), negkey)
    key = jnp.where(kpos < n_keys, key, INT_MIN)

    def bit_body(i, t_u):
        cand_u = t_u | jnp.left_shift(jnp.int32(1), 31 - i)
        cnt = jnp.sum(jnp.where(key >= (cand_u ^ INT_MIN), 1.0, 0.0), axis=1, keepdims=True)
        return jnp.where(cnt >= top_k, cand_u, t_u)

    t_s = lax.fori_loop(0, 32, bit_body, jnp.zeros((t_new, 1), I32)) ^ INT_MIN

    adm01 = jnp.where(adm, 1.0, 0.0)
    gt = jnp.where(key > t_s, adm01, 0.0)
    eq = jnp.where(key == t_s, adm01, 0.0)
    need = top_k - jnp.sum(jnp.where(key > t_s, 1.0, 0.0), axis=1, keepdims=True)
    r = lax.broadcasted_iota(I32, (LANE, LANE), 0)
    c = lax.broadcasted_iota(I32, (LANE, LANE), 1)
    utri = jnp.where(r < c, 1.0, 0.0).astype(BF16)
    carry = jnp.zeros((t_new, 1), F32)
    keeps = []
    for blk in range(lp // LANE):
        e = eq[:, blk * LANE:(blk + 1) * LANE]
        rank = carry + jnp.dot(e.astype(BF16), utri, preferred_element_type=F32)
        keeps.append(jnp.where(rank < need, e, 0.0))
        carry = carry + jnp.sum(e, axis=1, keepdims=True)
    sel = gt + jnp.concatenate(keeps, axis=1)

    rel = kpos - qpos
    nb = N_BUCKETS // 2
    max_exact = nb // 2
    n = jnp.abs(rel)
    nf = jnp.maximum(n, 1).astype(F32)
    large = max_exact + (jnp.log(nf / max_exact) / math.log(MAX_DIST / max_exact)
                         * (nb - max_exact)).astype(I32)
    bucket = (rel > 0).astype(I32) * nb + jnp.where(n < max_exact, n, jnp.minimum(large, nb - 1))

    rows = B_REP * t_new
    sel_g = jnp.concatenate([sel] * B_REP, axis=0) != 0.0
    bucket_g = jnp.concatenate([bucket] * B_REP, axis=0)
    for g in range(B_KV):
        grp = lambda ref, n: ref[0, pl.ds(g, n, stride=B_KV), :]
        qg = q_ref[0, g * rows:(g + 1) * rows, :]
        rb = rb_ref[g * rows:(g + 1) * rows, :]
        bias = jnp.zeros((rows, lp), F32)
        for j in range(N_BUCKETS):
            bias = jnp.where(bucket_g == j, rb[:, j:j + 1], bias)
        zp = _dot_nt(qg, grp(kp_ref, past)) * (B_HD ** -0.5)
        zn = _dot_nt(qg, grp(kn_ref, t_new)) * (B_HD ** -0.5)
        ap = jnp.where(sel_g[:, :past], zp + bias[:, :past], NEG_INF)
        an = jnp.where(sel_g[:, past:n_keys], zn + bias[:, past:n_keys], NEG_INF)
        m = jnp.maximum(jnp.max(ap, axis=1, keepdims=True), jnp.max(an, axis=1, keepdims=True))
        pp, pn = jnp.exp(ap - m), jnp.exp(an - m)
        l = jnp.sum(pp, axis=1, keepdims=True) + jnp.sum(pn, axis=1, keepdims=True)
        o_ref[0, g * rows:(g + 1) * rows, :] = (_dot(pp, grp(vp_ref, past)) + _dot(pn, grp(vn_ref, t_new))) / l


def _dsa_attention_cached(q_rows, qi_rows, wi_col, rb_rows, kidx_all, k_past, v_past, k_new, v_new):
    b, rows, hd = q_rows.shape
    past, t_new = k_past.shape[1], k_new.shape[1]
    lp = kidx_all.shape[1]
    top_k = min(TOPK_MAX, (past + t_new) // 4)
    assert past % LANE == 0
    flat = lambda a: a.reshape(b, a.shape[1] * B_KV, B_HD)
    kv_spec = lambda n: pl.BlockSpec((1, n * B_KV, B_HD), lambda i: (i, 0, 0))
    return pl.pallas_call(
        functools.partial(_dsa_cached_kernel, past=past, t_new=t_new, top_k=top_k),
        grid=(b,),
        in_specs=[pl.BlockSpec((1, rows, hd), lambda i: (i, 0, 0)),
                  pl.BlockSpec((1,) + qi_rows.shape[1:], lambda i: (i, 0, 0)),
                  pl.BlockSpec((1,) + wi_col.shape[1:], lambda i: (i, 0, 0)),
                  pl.BlockSpec(rb_rows.shape, lambda i: (0, 0)),
                  pl.BlockSpec((1, lp, IDX_DIM), lambda i: (i, 0, 0)),
                  kv_spec(past), kv_spec(past), kv_spec(t_new), kv_spec(t_new)],
        out_specs=pl.BlockSpec((1, rows, hd), lambda i: (i, 0, 0)),
        out_shape=jax.ShapeDtypeStruct((b, rows, hd), F32),
        compiler_params=_cparams("parallel"),
        name="dsa_attention_cached",
    )(q_rows, qi_rows, wi_col, rb_rows, kidx_all, flat(k_past), flat(v_past), flat(k_new), flat(v_new))


def _cumsum_kernel(x_ref, o_ref):
    x = x_ref[0]
    n = x.shape[-1]
    lane = lax.broadcasted_iota(I32, x.shape, 1)
    s = 1
    while s < n:
        x = x + jnp.where(lane >= s, pltpu.roll(x, s, 1), 0.0)
        s *= 2
    o_ref[0] = x


def _cumsum_lanes(x):
    b, h, n = x.shape
    return pl.pallas_call(
        _cumsum_kernel,
        grid=(b,),
        in_specs=[pl.BlockSpec((1, h, n), lambda i: (i, 0, 0))],
        out_specs=pl.BlockSpec((1, h, n), lambda i: (i, 0, 0)),
        out_shape=jax.ShapeDtypeStruct((b, h, n), F32),
        compiler_params=_cparams("parallel"),
        name="logf_cumsum",
    )(x)


def _fox_init(m_ref, l_ref, acc_ref):
    m_ref[...] = jnp.full(m_ref.shape, NEG_INF, F32)
    l_ref[...] = jnp.zeros(l_ref.shape, F32)
    acc_ref[...] = jnp.zeros(acc_ref.shape, F32)


def _fox_tile(z_fn, pv_fn, cq, ck, mask, m_ref, l_ref, acc_ref):
    c1 = (D_HD ** -0.5) * LOG2E
    cq2, ck2 = cq * LOG2E, ck * LOG2E
    z = z_fn(0)
    for h in range(D_HEADS):
        z_next = z_fn(h + 1) if h + 1 < D_HEADS else None
        a = z * c1 - ck2[:, h:h + 1]
        if mask is not None:
            a = jnp.where(mask, a, NEG_INF)
        cqh = cq2[h:h + 1, :]
        m_old = m_ref[h:h + 1, :]
        m_new = jnp.maximum(m_old, jnp.max(a, axis=0, keepdims=True) + cqh)
        alpha = jnp.exp2(m_old - m_new)
        p = jnp.exp2(a - (m_new - cqh))
        l_ref[h:h + 1, :] = alpha * l_ref[h:h + 1, :] + jnp.sum(p, axis=0, keepdims=True)
        hs = slice(h * D_HD, (h + 1) * D_HD)
        acc_ref[hs, :] = alpha * acc_ref[hs, :] + pv_fn(h, p.astype(BF16))
        m_ref[h:h + 1, :] = m_new
        z = z_next


def _fox_finish(o_ref, l_ref, acc_ref):
    inv_l = 1.0 / l_ref[...]
    outs = [acc_ref[h * D_HD:(h + 1) * D_HD, :] * inv_l[h:h + 1, :] for h in range(D_HEADS)]
    o_ref[0] = jnp.transpose(jnp.concatenate(outs, axis=0))


def _hs(h):
    return slice(h * D_HD, (h + 1) * D_HD)


def _fox_prompt_kernel(qt_ref, k_ref, vt_ref, cq_ref, ck_ref, o_ref, m_ref, l_ref, acc_ref):
    qi, ki = pl.program_id(1), pl.program_id(2)
    tq, tk = qt_ref.shape[2], k_ref.shape[1]
    q0, k0 = qi * tq, ki * tk

    @pl.when(ki == 0)
    def _():
        _fox_init(m_ref, l_ref, acc_ref)

    def run(masked):
        mask = None
        if masked:
            mask = (k0 + lax.broadcasted_iota(I32, (tk, tq), 0)) <= (q0 + lax.broadcasted_iota(I32, (tk, tq), 1))
        _fox_tile(lambda h: jnp.dot(k_ref[0, :, _hs(h)], qt_ref[0, _hs(h), :], preferred_element_type=F32),
                  lambda h, p: jnp.dot(vt_ref[0, _hs(h), :], p, preferred_element_type=F32),
                  cq_ref[0], ck_ref[0], mask, m_ref, l_ref, acc_ref)

    fully_visible = k0 + tk - 1 <= q0
    pl.when(fully_visible)(lambda: run(False))
    pl.when(jnp.logical_and(jnp.logical_not(fully_visible), k0 <= q0 + tq - 1))(lambda: run(True))

    @pl.when(ki == pl.num_programs(2) - 1)
    def _():
        _fox_finish(o_ref, l_ref, acc_ref)


def _fox_attention_prompt(qt, k, vt, cum_t, cum, tq, tk):
    b, d, t = qt.shape
    nq, nk = t // tq, t // tk
    last = lambda j: (j * tq + tq - 1) // tk
    return pl.pallas_call(
        _fox_prompt_kernel,
        grid=(b, nq, nk),
        in_specs=[pl.BlockSpec((1, d, tq), lambda i, j, kk: (i, 0, j)),
                  pl.BlockSpec((1, tk, d), lambda i, j, kk: (i, jnp.minimum(kk, last(j)), 0)),
                  pl.BlockSpec((1, d, tk), lambda i, j, kk: (i, 0, jnp.minimum(kk, last(j)))),
                  pl.BlockSpec((1, D_HEADS, tq), lambda i, j, kk: (i, 0, j)),
                  pl.BlockSpec((1, tk, D_HEADS), lambda i, j, kk: (i, jnp.minimum(kk, last(j)), 0))],
        out_specs=pl.BlockSpec((1, tq, d), lambda i, j, kk: (i, j, 0)),
        out_shape=jax.ShapeDtypeStruct((b, t, d), F32),
        scratch_shapes=[pltpu.VMEM((D_HEADS, tq), F32), pltpu.VMEM((D_HEADS, tq), F32),
                        pltpu.VMEM((d, tq), F32)],
        compiler_params=_cparams("parallel", "parallel", "arbitrary"),
        name="fox_attention",
    )(qt, k, vt, cum_t, cum)


def _fox_cached_kernel(q_ref, kp_ref, vp_ref, kn_ref, vn_ref, cq_ref, ckp_ref, ckn_ref, o_ref,
                       m_ref, l_ref, acc_ref, *, t_new):
    ki = pl.program_id(1)
    n_past = pl.num_programs(1) - 1
    rows = q_ref.shape[1]
    c1 = (D_HD ** -0.5) * LOG2E

    @pl.when(ki == 0)
    def _():
        m_ref[...] = jnp.full(m_ref.shape, NEG_INF, F32)
        l_ref[...] = jnp.zeros(l_ref.shape, F32)
        acc_ref[...] = jnp.zeros(acc_ref.shape, F32)

    def tile(k2d, v2d, ck_row, causal):
        cols = k2d.shape[0]
        a = _dot_nt(q_ref[0], k2d) * c1 - ck_row * LOG2E
        r = lax.broadcasted_iota(I32, (rows, cols), 0)
        c = lax.broadcasted_iota(I32, (rows, cols), 1)
        ok = (c % D_HEADS) == (r // t_new)
        if causal:
            ok = jnp.logical_and(ok, (c // D_HEADS) <= (r % t_new))
        a = jnp.where(ok, a, NEG_INF)
        cq2 = cq_ref[0] * LOG2E
        m_old = m_ref[...]
        m_new = jnp.maximum(m_old, jnp.max(a, axis=1, keepdims=True) + cq2)
        alpha = jnp.exp2(m_old - m_new)
        p = jnp.exp2(a - (m_new - cq2))
        l_ref[...] = alpha * l_ref[...] + jnp.sum(p, axis=1, keepdims=True)
        acc_ref[...] = alpha * acc_ref[...] + _dot(p, v2d)
        m_ref[...] = m_new

    @pl.when(ki < n_past)
    def _():
        tk = kp_ref.shape[1]
        tile(kp_ref[0].reshape(tk * D_HEADS, D_HD), vp_ref[0].reshape(tk * D_HEADS, D_HD), ckp_ref[0], False)

    @pl.when(ki == n_past)
    def _():
        tile(kn_ref[0].reshape(t_new * D_HEADS, D_HD), vn_ref[0].reshape(t_new * D_HEADS, D_HD), ckn_ref[0], True)
        o_ref[0] = acc_ref[...] / l_ref[...]


def _fox_attention_cached(q_rows, k_past, v_past, k_new, v_new, cq_col, ck_past, ck_new, tk):
    b, rows, hd = q_rows.shape
    past, t_new = k_past.shape[1], k_new.shape[1]
    n_past = past // tk
    pidx = lambda i, kk: (i, jnp.minimum(kk, n_past - 1), 0, 0)
    return pl.pallas_call(
        functools.partial(_fox_cached_kernel, t_new=t_new),
        grid=(b, n_past + 1),
        in_specs=[pl.BlockSpec((1, rows, hd), lambda i, kk: (i, 0, 0)),
                  pl.BlockSpec((1, tk, D_HEADS, D_HD), pidx),
                  pl.BlockSpec((1, tk, D_HEADS, D_HD), pidx),
                  pl.BlockSpec((1, t_new, D_HEADS, D_HD), lambda i, kk: (i, 0, 0, 0)),
                  pl.BlockSpec((1, t_new, D_HEADS, D_HD), lambda i, kk: (i, 0, 0, 0)),
                  pl.BlockSpec((1, rows, 1), lambda i, kk: (i, 0, 0)),
                  pl.BlockSpec((1, 1, tk * D_HEADS), lambda i, kk: (i, 0, jnp.minimum(kk, n_past - 1))),
                  pl.BlockSpec((1, 1, t_new * D_HEADS), lambda i, kk: (i, 0, 0))],
        out_specs=pl.BlockSpec((1, rows, hd), lambda i, kk: (i, 0, 0)),
        out_shape=jax.ShapeDtypeStruct((b, rows, hd), F32),
        scratch_shapes=[pltpu.VMEM((rows, 1), F32), pltpu.VMEM((rows, 1), F32), pltpu.VMEM((rows, hd), F32)],
        compiler_params=_cparams("parallel", "arbitrary"),
        name="fox_attention_cached",
    )(q_rows, k_past, v_past, k_new, v_new, cq_col, ck_past, ck_new)


def _pad_rows(a, rows):
    if a.shape[1] == rows:
        return a
    return jnp.pad(a, ((0, 0), (0, rows - a.shape[1])) + ((0, 0),) * (a.ndim - 2))


def _round_up(n, m):
    return -(-n // m) * m


def _dsa_mixer(x, g, k_past, v_past, ki_past, w, w_out, tab, rel_bias):
    b, t, d = x.shape
    past = k_past.shape[1]
    n_keys = past + t
    if past == 0:
        qt, k4, kb, v4, vt, qit, kw, kwt = _proj(
            x, g, w, ((0, "t", BF16, None), (1, B_HD, F32, None), (1, "rows", BF16, None),
                      (2, B_HD, F32, None), (2, "t", BF16, None), (3, "t", BF16, None),
                      (4, "rows", F32, None), (4, "t", F32, None)),
            jnp.zeros((1, LANE), F32))
        o = _dsa_attention_prompt(qt, qit, kwt, kw, kb, vt, tab, 2 * LANE)
        y = _out_proj(x.reshape(b * t, d), o.reshape(b * t, d), w_out).reshape(b, t, d)
        return (y, k4, v4, kw[:, :, :IDX_DIM])
    q, k4, v4, qidx, kw = _proj(
        x, g, w, ((0, "rows", BF16, None), (1, B_HD, F32, None), (2, B_HD, F32, None),
                  (3, "rows", BF16, None), (4, "rows", F32, None)),
        jnp.zeros((1, LANE), F32))
    ki = kw[:, :, :IDX_DIM]
    to_rows = lambda a, nh: jnp.swapaxes(a.reshape(b, t, nh, -1), 1, 2).reshape(b, nh * t, -1)
    kidx_all = _pad_rows(jnp.concatenate([ki_past, ki], axis=1), _round_up(n_keys, LANE))
    o = _dsa_attention_cached(to_rows(q, B_HEADS), to_rows(qidx, IDX_HEADS),
                              to_rows(kw[:, :, IDX_DIM:IDX_DIM + IDX_HEADS], IDX_HEADS),
                              jnp.repeat(rel_bias.T, t, axis=0), kidx_all, k_past, v_past, k4, v4)
    o = jnp.swapaxes(o.reshape(b, B_HEADS, t, B_HD), 1, 2).reshape(b * t, d)
    y = _out_proj(x.reshape(b * t, d), o, w_out).reshape(b, t, d)
    return (y, k4, v4, ki)


def _fox_mixer(x, g, k_past, v_past, lf_past, w, b_f, w_out):
    b, t, d = x.shape
    past = k_past.shape[1]
    heads4 = ((1, D_HD, F32, None), (2, D_HD, F32, None), (3, "rows", F32, "log_sigmoid"))
    if past == 0:
        tq = tk = 2 * LANE
        k4, v4, lf, qt, kb, vt = _proj(
            x, g, w, heads4 + ((0, "t", BF16, None), (1, "rows", BF16, None), (2, "t", BF16, None)), b_f)
        logf = lf[:, :, :D_HEADS]
        cum_t = _cumsum_lanes(jnp.swapaxes(logf, 1, 2))
        o = _fox_attention_prompt(qt, kb, vt, cum_t, jnp.swapaxes(cum_t, 1, 2), tq, tk)
    else:
        tk = 2 * LANE
        k4, v4, lf, q = _proj(x, g, w, heads4 + ((0, "rows", BF16, None),), b_f)
        logf = lf[:, :, :D_HEADS]
        lf_all = _pad_rows(jnp.concatenate([lf_past, logf], axis=1), _round_up(past + t, LANE))
        cum = jnp.swapaxes(_cumsum_lanes(jnp.swapaxes(lf_all, 1, 2)), 1, 2)[:, :past + t]
        ck = cum.reshape(b, 1, (past + t) * D_HEADS)
        to_rows = lambda a: jnp.swapaxes(a.reshape(b, t, D_HEADS, -1), 1, 2).reshape(b, D_HEADS * t, -1)
        o = _fox_attention_cached(to_rows(q), k_past, v_past, k4, v4, to_rows(cum[:, past:]),
                                  ck[:, :, :past * D_HEADS], ck[:, :, past * D_HEADS:], tk)
        o = jnp.swapaxes(o.reshape(b, D_HEADS, t, D_HD), 1, 2).reshape(b, t, d)
    y = _out_proj(x.reshape(b * t, d), o.reshape(b * t, d), w_out).reshape(b, t, d)
    return (y, k4, v4, logf)


def _run_group(x, pos0, a_st, b_k, b_v, b_ki, c_st, d_k, d_v, d_lf, mem_k, mem_v, prm):
    b, t, d = x.shape
    depth = prm["norm_mix"].shape[0]
    new = {n: [] for n in ("a", "bk", "bv", "bki", "c", "dk", "dv", "dlf")}
    for i in range(depth):
        kind, j = i % 4, i // 4
        g = prm["norm_mix"][i]
        if kind == 0:
            x, st = _conv_mixer(x, g, prm["a_w_in"][j], prm["a_conv"][j], a_st[j], prm["a_w_out"][j])
            new["a"].append(st)
        elif kind == 1:
            x, kk, vv, ki = _dsa_mixer(x, g, b_k[j], b_v[j], b_ki[j], prm["b_w"][j], prm["b_w_out"][j],
                                       prm["bias_tab"], prm["rel_bias"])
            new["bk"].append(kk); new["bv"].append(vv); new["bki"].append(ki)
        elif kind == 2:
            x, st = _pool_mixer(x, g, c_st[j], prm["c_w_group"][j], prm["c_scale"][j], pos0)
            new["c"].append(st)
        else:
            x, kk, vv, lf = _fox_mixer(x, g, d_k[j], d_v[j], d_lf[j], prm["d_w"][j], prm["d_b_f"][j],
                                       prm["d_w_out"][j])
            new["dk"].append(kk); new["dv"].append(vv); new["dlf"].append(lf)
        x = _xattn(x, prm["norm_xattn"], prm["xa_wq"], mem_k, mem_v, prm["xa_wo"], i)
        last = i == depth - 1
        x = _ffn(x.reshape(b * t, d), prm["norm_ffn"], prm["ffn_w1"], prm["ffn_w2"],
                 prm["final_norm"], i, last).reshape(b, t, d)
    return (x,) + tuple(jnp.stack(new[n]) for n in ("a", "bk", "bv", "bki", "c", "dk", "dv", "dlf"))


def kernel(x_prompt, x_sample, state_a_conv, cache_b_k, cache_b_v, cache_b_kidx, state_c_pool,
           cache_d_k, cache_d_v, cache_d_logf, cache_mem_k, cache_mem_v, mem_prompt,
           norm_mix, norm_xattn, norm_mem, norm_ffn, final_norm,
           a_w_in, a_conv, a_w_out, b_w_in, b_w_out, rel_bias, c_w_group, c_scale,
           d_w_in, d_b_f, d_w_out, xa_wq, xa_wkv, xa_wo, ffn_w1, ffn_w2):
    bp, t, d = x_prompt.shape
    depth = norm_mix.shape[0]
    n_b, n_d = b_w_in.shape[0], d_w_in.shape[0]
    bf = lambda w: w.astype(BF16)

    def split_cols(w, widths):
        out, c = [], 0
        for wd in widths:
            piece = w[:, c:c + wd]
            c += wd
            if wd % LANE:
                piece = jnp.pad(piece, ((0, 0), (0, _round_up(wd, LANE) - wd)))
            out.append(bf(piece))
        assert c == w.shape[1]
        return out

    b_q, b_kvw = B_HEADS * B_HD, B_KV * B_HD
    b_w = [split_cols(b_w_in[j], (b_q, b_kvw, b_kvw, IDX_HEADS * IDX_DIM, IDX_DIM + IDX_HEADS))
           for j in range(n_b)]
    d_w = [split_cols(d_w_in[j], (d, d, d, D_HEADS)) for j in range(n_d)]
    d_bf = [jnp.pad(d_b_f[j], (0, LANE - D_HEADS)).reshape(1, LANE) for j in range(n_d)]
    bias_tab = _bias_table(rel_bias, 5 * LANE)

    prm = {"norm_mix": norm_mix, "norm_xattn": norm_xattn, "norm_ffn": norm_ffn, "final_norm": final_norm,
           "a_w_in": bf(a_w_in), "a_conv": a_conv, "a_w_out": bf(a_w_out),
           "b_w": b_w, "b_w_out": bf(b_w_out), "bias_tab": bias_tab, "rel_bias": rel_bias,
           "c_w_group": bf(c_w_group), "c_scale": c_scale,
           "d_w": d_w, "d_b_f": d_bf, "d_w_out": bf(d_w_out),
           "xa_wq": bf(xa_wq), "xa_wo": bf(xa_wo), "ffn_w1": bf(ffn_w1), "ffn_w2": bf(ffn_w2)}

    n_mem = mem_prompt.shape[1]
    mk, mv, mk_rows, mv_rows = _memory_kv(mem_prompt, norm_mem, bf(xa_wkv))

    n_a, n_c = a_w_in.shape[0], c_w_group.shape[0]
    z = lambda *s: jnp.zeros(s, F32)
    gp = _run_group(x_prompt, 0,
                    z(n_a, bp, CONV_W - 1, d),
                    z(n_b, bp, 0, B_KV, B_HD), z(n_b, bp, 0, B_KV, B_HD), z(n_b, bp, 0, IDX_DIM),
                    z(n_c, bp, POOL_STATE, d),
                    z(n_d, bp, 0, D_HEADS, D_HD), z(n_d, bp, 0, D_HEADS, D_HD), z(n_d, bp, 0, D_HEADS),
                    mk_rows, mv_rows, prm)

    bs = x_sample.shape[0]
    past_len = cache_b_k.shape[2]
    gs = _run_group(x_sample, past_len, state_a_conv, cache_b_k, cache_b_v, cache_b_kidx, state_c_pool,
                    cache_d_k, cache_d_v, cache_d_logf, cache_mem_k, cache_mem_v, prm)

    (y_p, a_p, bk_p, bv_p, bki_p, c_p, dk_p, dv_p, dlf_p) = gp
    (y_s, a_s, bk_s, bv_s, bki_s, c_s, dk_s, dv_s, dlf_s) = gs
    return (y_p, y_s, a_p, a_s, bk_p, bv_p, bki_p, bk_s, bv_s, bki_s, c_p, c_s,
            dk_p, dv_p, dlf_p, dk_s, dv_s, dlf_s, mk, mv)
```

```python
import functools
import math

import jax
import jax.numpy as jnp
from jax import lax
from jax.experimental import pallas as pl
from jax.experimental.pallas import tpu as pltpu

F32 = jnp.float32
BF16 = jnp.bfloat16
I32 = jnp.int32

EPS = 1e-6
NEG_INF = -1e30
LOG2E = math.log2(math.e)
CHUNK = 64
LANE = 128
VMEM_LIMIT = 48 * 1024 * 1024

CONV_W = 3
POOL_WINDOWS = (2, 4, 8, 16)
POOL_STATE = max(POOL_WINDOWS) - 1
B_HEADS, B_KV, B_HD = 8, 2, 128
B_REP = B_HEADS // B_KV
IDX_HEADS, IDX_DIM = 8, 64
TOPK_MAX = 256
N_BUCKETS, MAX_DIST = 32, 128
D_HEADS, D_HD = 8, 128
MEM_HEADS = 4
INT_MIN = -2147483648
BIAS_CENTER = 2 * LANE


def _cparams(*sem):
    return pltpu.CompilerParams(dimension_semantics=sem, vmem_limit_bytes=VMEM_LIMIT)


def _dot(a, b):
    return jnp.dot(a.astype(BF16), b.astype(BF16), preferred_element_type=F32)


def _dot_nt(a, b):
    return lax.dot_general(a.astype(BF16), b.astype(BF16), (((1,), (1,)), ((), ())),
                           preferred_element_type=F32)


def _dot_tn(a, b):
    return lax.dot_general(a.astype(BF16), b.astype(BF16), (((0,), (0,)), ((), ())),
                           preferred_element_type=F32)


def _rms(x, g):
    return x * lax.rsqrt(jnp.mean(x * x, axis=-1, keepdims=True) + EPS) * g


def _finish_heads(o_ref, x_ref, w_ref, l_ref, acc_ref, n_heads):
    hd = acc_ref.shape[0] // n_heads
    inv_l = 1.0 / l_ref[...]
    heads_t = jnp.concatenate([acc_ref[h * hd:(h + 1) * hd, :] * inv_l[h:h + 1, :] for h in range(n_heads)],
                              axis=0)
    o_ref[0] = x_ref[0] + _dot_tn(heads_t, w_ref[...])


def _row_tile(n, cap):
    t = min(n, cap)
    assert n % t == 0
    return t


def _memkv_kernel(mem_ref, g_ref, w_ref, k_ref, v_ref, kb_ref, vb_ref):
    m = mem_ref[0]
    mn = m * lax.rsqrt(jnp.mean(m * m, axis=-1, keepdims=True) + EPS)
    h = (mn * g_ref[0]).astype(BF16)
    d = m.shape[-1]
    hd = d // MEM_HEADS
    k = jnp.dot(h, w_ref[0, :, :d], preferred_element_type=F32)
    v = jnp.dot(h, w_ref[0, :, d:], preferred_element_type=F32)
    kb_ref[0, 0] = k.astype(BF16)
    vb_ref[0, 0] = v.astype(BF16)
    for hh in range(MEM_HEADS):
        k_ref[0, 0, :, hh, :] = k[:, hh * hd:(hh + 1) * hd]
        v_ref[0, 0, :, hh, :] = v[:, hh * hd:(hh + 1) * hd]


def _memory_kv(mem, g_mem, w_kv):
    depth, d = g_mem.shape
    b, nm, _ = mem.shape
    hd = d // MEM_HEADS
    out = jax.ShapeDtypeStruct((depth, b, nm, MEM_HEADS, hd), F32)
    out_b = jax.ShapeDtypeStruct((depth, b, nm, d), BF16)
    heads_spec = pl.BlockSpec((1, 1, nm, MEM_HEADS, hd), lambda l, i: (l, i, 0, 0, 0))
    rows_spec = pl.BlockSpec((1, 1, nm, d), lambda l, i: (l, i, 0, 0))
    return pl.pallas_call(
        _memkv_kernel,
        grid=(depth, b),
        in_specs=[pl.BlockSpec((1, nm, d), lambda l, i: (i, 0, 0)),
                  pl.BlockSpec((1, 1, d), lambda l, i: (l, 0, 0)),
                  pl.BlockSpec((1, d, 2 * d), lambda l, i: (l, 0, 0))],
        out_specs=[heads_spec, heads_spec, rows_spec, rows_spec],
        out_shape=[out, out, out_b, out_b],
        compiler_params=_cparams("parallel", "parallel"),
        name="memory_kv",
    )(mem, g_mem.reshape(depth, 1, d), w_kv)


def _ffn_kernel(x_ref, g_ref, w1_ref, w2_ref, gf_ref, o_ref, h_ref, acc_ref, *, final_norm):
    j = pl.program_id(1)

    @pl.when(j == 0)
    def _():
        h_ref[...] = _rms(x_ref[...], g_ref[...]).astype(BF16)
        acc_ref[...] = jnp.zeros_like(acc_ref)

    u = jnp.maximum(jnp.dot(h_ref[...], w1_ref[...], preferred_element_type=F32), 0.0)
    acc_ref[...] += jnp.dot((u * u).astype(BF16), w2_ref[...], preferred_element_type=F32)

    @pl.when(j == pl.num_programs(1) - 1)
    def _():
        y = x_ref[...] + acc_ref[...]
        o_ref[...] = _rms(y, gf_ref[...]) if final_norm else y


def _ffn(x, g, w1, w2, gf, layer, final_norm):
    n, d = x.shape
    f = w1.shape[2]
    tm = _row_tile(n, 1024)
    tf = 1024
    return pl.pallas_call(
        functools.partial(_ffn_kernel, final_norm=final_norm),
        grid=(n // tm, f // tf),
        in_specs=[pl.BlockSpec((tm, d), lambda i, j: (i, 0)),
                  pl.BlockSpec((None, 1, d), lambda i, j: (layer, 0, 0)),
                  pl.BlockSpec((None, d, tf), lambda i, j: (layer, 0, j)),
                  pl.BlockSpec((None, tf, d), lambda i, j: (layer, j, 0)),
                  pl.BlockSpec((1, d), lambda i, j: (0, 0))],
        out_specs=pl.BlockSpec((tm, d), lambda i, j: (i, 0)),
        out_shape=jax.ShapeDtypeStruct((n, d), F32),
        scratch_shapes=[pltpu.VMEM((tm, d), BF16), pltpu.VMEM((tm, d), F32)],
        compiler_params=_cparams("parallel", "arbitrary"),
        name="ffn",
    )(x, g.reshape(-1, 1, d), w1, w2, gf.reshape(1, d))


def _xattn_kernel(x_ref, g_ref, wq_ref, mk_ref, mv_ref, wo_ref, o_ref):
    x = x_ref[0]
    d = x.shape[-1]
    hd = d // MEM_HEADS
    h = _rms(x, g_ref[...]).astype(BF16)
    q = jnp.dot(h, wq_ref[...], preferred_element_type=F32)
    outs = []
    by_lanes = mk_ref.shape[1] == LANE
    pieces = hd // LANE

    def head(ref, hh):
        if not by_lanes:
            return ref[:, hh * hd:(hh + 1) * hd]
        n_rows = ref.shape[0] * LANE // d
        return jnp.concatenate([ref[pl.ds(hh * pieces + c, n_rows, stride=d // LANE), :]
                                for c in range(pieces)], axis=1)

    for hh in range(MEM_HEADS):
        sl = slice(hh * hd, (hh + 1) * hd)
        kh, vh = head(mk_ref, hh), head(mv_ref, hh)
        s = _dot_nt(q[:, sl], kh) * (hd ** -0.5)
        m = jnp.max(s, axis=-1, keepdims=True)
        p = jnp.exp(s - m)
        l = jnp.sum(p, axis=-1, keepdims=True)
        outs.append(_dot(p, vh) / l)
    o = jnp.concatenate(outs, axis=-1)
    o_ref[0] = x + _dot(o, wo_ref[...])


def _xattn(x, g, wq, mk, mv, wo, layer):
    b, t, d = x.shape
    tm = _row_tile(t, 512)
    if mk.ndim == 5:
        mk, mv = (a.reshape(a.shape[:2] + (a.shape[2] * d // LANE, LANE)) for a in (mk, mv))
    kv_spec = pl.BlockSpec((None, None) + mk.shape[2:], lambda i, j: (layer, i, 0, 0))
    return pl.pallas_call(
        _xattn_kernel,
        grid=(b, t // tm),
        in_specs=[pl.BlockSpec((1, tm, d), lambda i, j: (i, j, 0)),
                  pl.BlockSpec((None, 1, d), lambda i, j: (layer, 0, 0)),
                  pl.BlockSpec((None, d, d), lambda i, j: (layer, 0, 0)),
                  kv_spec, kv_spec,
                  pl.BlockSpec((None, d, d), lambda i, j: (layer, 0, 0))],
        out_specs=pl.BlockSpec((1, tm, d), lambda i, j: (i, j, 0)),
        out_shape=jax.ShapeDtypeStruct((b, t, d), F32),
        compiler_params=_cparams("parallel", "parallel"),
        name="xattn",
    )(x, g.reshape(-1, 1, d), wq, mk, mv, wo)


def _conv_kernel(x_ref, g_ref, win_ref, wc_ref, st_ref, wout_ref, o_ref, nst_ref, z_ref):
    t = pl.program_id(1)
    x = x_ref[0]
    tm, d = x.shape
    pad = 8

    @pl.when(t == 0)
    def _():
        z_ref[pad - 2:pad, :] = st_ref[0]

    h = _rms(x, g_ref[...]).astype(BF16)
    bg = jnp.dot(h, win_ref[:, 0:d], preferred_element_type=F32)
    cg = jnp.dot(h, win_ref[:, d:2 * d], preferred_element_type=F32)
    u = jnp.dot(h, win_ref[:, 2 * d:3 * d], preferred_element_type=F32)
    z = cg * u
    z_ref[pad:pad + tm, :] = z
    conv = (z_ref[pad - 2:pad - 2 + tm, :] * wc_ref[0:1, :]
            + z_ref[pad - 1:pad - 1 + tm, :] * wc_ref[1:2, :]
            + z * wc_ref[2:3, :])
    o_ref[0] = x + _dot(bg * conv, wout_ref[...])
    last = z_ref[pad + tm - 2:pad + tm, :]
    z_ref[pad - 2:pad, :] = last

    @pl.when(t == pl.num_programs(1) - 1)
    def _():
        nst_ref[0] = last


def _conv_mixer(x, g, w_in, w_conv, state, w_out):
    b, t, d = x.shape
    tm = _row_tile(t, 512)
    return pl.pallas_call(
        _conv_kernel,
        grid=(b, t // tm),
        in_specs=[pl.BlockSpec((1, tm, d), lambda i, j: (i, j, 0)),
                  pl.BlockSpec((1, d), lambda i, j: (0, 0)),
                  pl.BlockSpec((d, 3 * d), lambda i, j: (0, 0)),
                  pl.BlockSpec((CONV_W, d), lambda i, j: (0, 0)),
                  pl.BlockSpec((1, CONV_W - 1, d), lambda i, j: (i, 0, 0)),
                  pl.BlockSpec((d, d), lambda i, j: (0, 0))],
        out_specs=[pl.BlockSpec((1, tm, d), lambda i, j: (i, j, 0)),
                   pl.BlockSpec((1, CONV_W - 1, d), lambda i, j: (i, 0, 0))],
        out_shape=[jax.ShapeDtypeStruct((b, t, d), F32),
                   jax.ShapeDtypeStruct((b, CONV_W - 1, d), F32)],
        scratch_shapes=[pltpu.VMEM((tm + 8, d), F32)],
        compiler_params=_cparams("parallel", "arbitrary"),
        name="conv_mixer",
    )(x, g.reshape(1, d), w_in, w_conv, state, w_out)


def _pool_kernel(x_ref, g_ref, st_ref, wg_ref, sc_ref, o_ref, nst_ref, h_ref, *, pos0):
    t = pl.program_id(1)
    x = x_ref[0]
    tm, d = x.shape
    gw = d // len(POOL_WINDOWS)
    base = POOL_STATE + 1

    @pl.when(t == 0)
    def _():
        h_ref[1:base, :] = st_ref[0]

    h = _rms(x, g_ref[...])
    h_ref[base:base + tm, :] = h
    pos = pos0 + t * tm + lax.broadcasted_iota(I32, (tm, gw), 0)
    ys = []
    for gi, w in enumerate(POOL_WINDOWS):
        sl = slice(gi * gw, (gi + 1) * gw)
        win = h[:, sl]
        for j in range(1, w):
            win = win + h_ref[base - j:base - j + tm, sl]
        count = jnp.minimum(w, pos + 1).astype(F32)
        dlt = win / count - h[:, sl]
        ys.append(_dot(dlt, wg_ref[gi]))
    y = jnp.concatenate(ys, axis=-1) * sc_ref[...]
    o_ref[0] = x + y
    last = h_ref[tm + 1:tm + base, :]
    h_ref[1:base, :] = last

    @pl.when(t == pl.num_programs(1) - 1)
    def _():
        nst_ref[0] = last


def _pool_mixer(x, g, state, w_group, scale, pos0):
    b, t, d = x.shape
    ng, gw, _ = w_group.shape
    tm = _row_tile(t, 512)
    return pl.pallas_call(
        functools.partial(_pool_kernel, pos0=pos0),
        grid=(b, t // tm),
        in_specs=[pl.BlockSpec((1, tm, d), lambda i, j: (i, j, 0)),
                  pl.BlockSpec((1, d), lambda i, j: (0, 0)),
                  pl.BlockSpec((1, POOL_STATE, d), lambda i, j: (i, 0, 0)),
                  pl.BlockSpec((ng, gw, gw), lambda i, j: (0, 0, 0)),
                  pl.BlockSpec((1, d), lambda i, j: (0, 0))],
        out_specs=[pl.BlockSpec((1, tm, d), lambda i, j: (i, j, 0)),
                   pl.BlockSpec((1, POOL_STATE, d), lambda i, j: (i, 0, 0))],
        out_shape=[jax.ShapeDtypeStruct((b, t, d), F32),
                   jax.ShapeDtypeStruct((b, POOL_STATE, d), F32)],
        scratch_shapes=[pltpu.VMEM((tm + POOL_STATE + 1, d), F32)],
        compiler_params=_cparams("parallel", "arbitrary"),
        name="pool_mixer",
    )(x, g.reshape(1, d), state, w_group, scale.reshape(1, d))


def _proj_kernel(*refs, n_w, outs):
    x_ref, g_ref = refs[0], refs[1]
    w_refs = refs[2:2 + n_w]
    e_ref = refs[2 + n_w]
    o_refs = refs[3 + n_w:]
    h = _rms(x_ref[0], g_ref[...]).astype(BF16)
    ys = {}
    for (wi, mode, _, ep), o_ref in zip(outs, o_refs):
        if wi not in ys:
            ys[wi] = jnp.dot(h, w_refs[wi][...], preferred_element_type=F32)
        y = ys[wi]
        if ep == "log_sigmoid":
            u = -(y + e_ref[...])
            y = -(jnp.maximum(u, 0.0) + jnp.log1p(jnp.exp(-jnp.abs(u))))
        if mode == "rows":
            o_ref[0] = y.astype(o_ref.dtype)
        elif mode == "t":
            o_ref[0] = jnp.transpose(y).astype(o_ref.dtype)
        else:
            for hh in range(y.shape[1] // mode):
                o_ref[0, :, hh, :] = y[:, hh * mode:(hh + 1) * mode].astype(o_ref.dtype)


def _proj(x, g, ws, outs, extra):
    b, t, d = x.shape
    tm = _row_tile(t, 512)
    in_specs = [pl.BlockSpec((1, tm, d), lambda i, j: (i, j, 0)), pl.BlockSpec((1, d), lambda i, j: (0, 0))]
    in_specs += [pl.BlockSpec(w.shape, lambda i, j: (0, 0)) for w in ws]
    in_specs += [pl.BlockSpec(extra.shape, lambda i, j: (0, 0))]
    out_specs, out_shape = [], []
    for wi, mode, dt, _ in outs:
        n = ws[wi].shape[1]
        if mode == "rows":
            out_specs.append(pl.BlockSpec((1, tm, n), lambda i, j: (i, j, 0)))
            out_shape.append(jax.ShapeDtypeStruct((b, t, n), dt))
        elif mode == "t":
            out_specs.append(pl.BlockSpec((1, n, tm), lambda i, j: (i, 0, j)))
            out_shape.append(jax.ShapeDtypeStruct((b, n, t), dt))
        else:
            out_specs.append(pl.BlockSpec((1, tm, n // mode, mode), lambda i, j: (i, j, 0, 0)))
            out_shape.append(jax.ShapeDtypeStruct((b, t, n // mode, mode), dt))
    return pl.pallas_call(
        functools.partial(_proj_kernel, n_w=len(ws), outs=tuple(outs)),
        grid=(b, t // tm),
        in_specs=in_specs,
        out_specs=out_specs,
        out_shape=out_shape,
        compiler_params=_cparams("parallel", "parallel"),
        name="norm_proj",
    )(x, g.reshape(1, d), *ws, extra)


def _outproj_kernel(x_ref, a_ref, w_ref, o_ref):
    o_ref[...] = x_ref[...] + _dot(a_ref[...], w_ref[...])


def _out_proj(x, a, w):
    n, d = x.shape
    tm = _row_tile(n, 512)
    return pl.pallas_call(
        _outproj_kernel,
        grid=(n // tm,),
        in_specs=[pl.BlockSpec((tm, d), lambda i: (i, 0)),
                  pl.BlockSpec((tm, d), lambda i: (i, 0)),
                  pl.BlockSpec((d, d), lambda i: (0, 0))],
        out_specs=pl.BlockSpec((tm, d), lambda i: (i, 0)),
        out_shape=jax.ShapeDtypeStruct((n, d), F32),
        compiler_params=_cparams("parallel"),
        name="out_proj",
    )(x, a, w)


def _bias_table_kernel(rbt_ref, o_ref):
    width = o_ref.shape[-1]
    rel = BIAS_CENTER - lax.broadcasted_iota(I32, (1, width), 1)
    nb = N_BUCKETS // 2
    max_exact = nb // 2
    ret = (rel > 0).astype(I32) * nb
    n = jnp.abs(rel)
    nf = jnp.maximum(n, 1).astype(F32)
    large = max_exact + (jnp.log(nf / max_exact) / math.log(MAX_DIST / max_exact)
                         * (nb - max_exact)).astype(I32)
    large = jnp.minimum(large, nb - 1)
    bucket = ret + jnp.where(n < max_exact, n, large)
    acc = jnp.zeros(o_ref.shape, F32)
    for j in range(N_BUCKETS):
        acc = jnp.where(bucket == j, rbt_ref[:, j:j + 1], acc)
    o_ref[...] = acc * LOG2E


def _bias_tiles_kernel(rbt_ref, far_ref, near_ref, tab_ref):
    n_d, nh, kb, tq = near_ref.shape
    _bias_table_kernel(rbt_ref, tab_ref)
    far_ref[...] = jnp.broadcast_to(tab_ref[:, BIAS_CENTER + MAX_DIST:BIAS_CENTER + MAX_DIST + 1], far_ref.shape)
    for dd in range(n_d):
        s0 = BIAS_CENTER - (dd - 1) * kb - kb
        for h in range(nh):
            rows = jnp.broadcast_to(tab_ref[h:h + 1, s0:s0 + tq + kb], (kb, tq + kb))
            near_ref[dd, h] = pltpu.roll(rows, 0, 1, stride=1, stride_axis=0)[:, kb:]


def _bias_tiles(rel_bias, kb, tq):
    nh = rel_bias.shape[1]
    n_d = tq // kb + 1
    width = BIAS_CENTER + kb + tq + kb
    assert kb % LANE == 0 and tq % kb == 0 and kb >= MAX_DIST and BIAS_CENTER >= tq
    return pl.pallas_call(
        _bias_tiles_kernel,
        out_shape=[jax.ShapeDtypeStruct((nh, LANE), F32), jax.ShapeDtypeStruct((n_d, nh, kb, tq), F32)],
        scratch_shapes=[pltpu.VMEM((nh, width), F32)],
        name="bias_tiles",
    )(rel_bias.T)


def _sortable(x):
    x = jnp.where(x == 0.0, 0.0, x)
    bits = lax.bitcast_convert_type(x, I32)
    return jnp.where(bits < 0, bits ^ 0x7FFFFFFF, bits)


def _neg_inf_key():
    import numpy as np
    b = int(np.float32(NEG_INF).view(np.int32))
    return b ^ 0x7FFFFFFF


def _dsa_prompt_kernel(qt_ref, qit_ref, kwt_ref, kw_ref, k_ref, vt_ref, far_ref, near_ref, x_ref, wout_ref, o_ref,
                       key_ref, sel_ref, m_ref, l_ref, acc_ref, a_ref, *, top_k):
    qi = pl.program_id(1)
    tq = qt_ref.shape[2]
    n_keys = kw_ref.shape[1]
    kb_sz = LANE
    q0 = qi * tq
    nkb = jnp.minimum(n_keys, q0 + tq) // kb_sz
    negkey = _neg_inf_key()

    qlane = lax.broadcasted_iota(I32, (1, tq), 1)
    lim = ((q0 + qlane) // CHUNK + 1) * CHUNK
    krow = lax.broadcasted_iota(I32, (kb_sz, tq), 0)

    def kslice(kb):
        return pl.ds(pl.multiple_of(kb * kb_sz, kb_sz), kb_sz)

    sb = 2 * kb_sz
    srow = lax.broadcasted_iota(I32, (sb, tq), 0)

    def score_body(i, c):
        rows = pl.ds(pl.multiple_of(i * sb, sb), sb)
        kid = kw_ref[0, rows, :][:, :IDX_DIM].astype(BF16)
        sc = jnp.zeros((sb, tq), F32)
        for h in range(IDX_HEADS):
            s = jnp.dot(kid, qit_ref[0, h * IDX_DIM:(h + 1) * IDX_DIM, :], preferred_element_type=F32)
            sc = sc + jnp.maximum(s, 0.0) * kwt_ref[0, IDX_DIM + h:IDX_DIM + h + 1, :]
        sc = sc * ((IDX_DIM * IDX_HEADS) ** -0.5)
        key_ref[rows, :] = jnp.where(i * sb + srow < lim, _sortable(sc), negkey)
        return c

    lax.fori_loop(0, nkb // 2, score_body, 0)

    def count(pred_fn):
        def body(i, a):
            for u in range(2):
                kb = 2 * i + u
                ind = pred_fn(kb, key_ref[kslice(kb), :])
                a = a + jnp.sum(ind.reshape(kb_sz // 8, 8, tq), axis=0)
            return a
        a = lax.fori_loop(0, nkb // 2, body, jnp.zeros((8, tq), I32))
        return jnp.sum(a, axis=0, keepdims=True)

    def bit_body(i, t_u):
        cand_u = t_u | jnp.left_shift(jnp.int32(1), 31 - i)
        cand_s = cand_u ^ INT_MIN
        cnt = count(lambda kb, key: jnp.where(key >= cand_s, 1, 0))
        return jnp.where(cnt >= top_k, cand_u, t_u)

    t_s = lax.fori_loop(0, 32, bit_body, jnp.zeros((1, tq), I32)) ^ INT_MIN

    def adm01(kb):
        return jnp.where(kb * kb_sz + krow < lim, 1.0, 0.0)

    def sel_body(kb, a):
        sel = jnp.where(key_ref[kslice(kb), :] >= t_s, adm01(kb), 0.0)
        sel_ref[kslice(kb), :] = sel
        return a + jnp.sum(sel.reshape(kb_sz // 8, 8, tq), axis=0)

    n_sel = jnp.sum(lax.fori_loop(0, nkb, sel_body, jnp.zeros((8, tq), F32)), axis=0, keepdims=True)

    @pl.when(jnp.max(n_sel) > top_k)
    def _():
        n_gt = count(lambda kb, key: jnp.where(key > t_s, 1, 0))
        need = (top_k - n_gt).astype(F32)
        r = lax.broadcasted_iota(I32, (kb_sz, kb_sz), 0)
        c = lax.broadcasted_iota(I32, (kb_sz, kb_sz), 1)
        ltri = jnp.where(c < r, 1.0, 0.0).astype(BF16)

        def tie_body(kb, carry):
            key = key_ref[kslice(kb), :]
            adm = adm01(kb)
            eq = jnp.where(key == t_s, adm, 0.0)
            rank = carry + jnp.dot(ltri, eq.astype(BF16), preferred_element_type=F32)
            keep = jnp.where(rank < need, eq, 0.0)
            sel_ref[kslice(kb), :] = jnp.where(key > t_s, adm, keep)
            return carry + jnp.sum(eq, axis=0, keepdims=True)

        lax.fori_loop(0, nkb, tie_body, jnp.zeros((1, tq), F32))

    m_ref[...] = jnp.full(m_ref.shape, NEG_INF, F32)
    l_ref[...] = jnp.zeros(l_ref.shape, F32)
    acc_ref[...] = jnp.zeros(acc_ref.shape, F32)
    c1 = (B_HD ** -0.5) * LOG2E

    def attend(k0, nk, bias2_fn):
        rows = pl.ds(pl.multiple_of(k0, LANE), nk)
        sel = sel_ref[rows, :] != 0.0
        ks = k_ref[0, rows, :]
        cols = []
        for h in range(B_HEADS):
            g = h // B_REP
            z = jnp.dot(ks[:, g * B_HD:(g + 1) * B_HD], qt_ref[0, h * B_HD:(h + 1) * B_HD, :],
                        preferred_element_type=F32)
            a = jnp.where(sel, z * c1 + bias2_fn(h), NEG_INF)
            a_ref[h, 0:nk, :] = a
            cols.append(jnp.max(a, axis=0, keepdims=True))
        m_old = m_ref[...]
        m_new = jnp.maximum(m_old, jnp.concatenate(cols, axis=0))
        alpha = jnp.exp2(m_old - m_new)
        m_ref[...] = m_new
        sums = []
        for h in range(B_HEADS):
            g = h // B_REP
            p = jnp.exp2(a_ref[h, 0:nk, :] - m_new[h:h + 1, :])
            sums.append(jnp.sum(p, axis=0, keepdims=True))
            hs = slice(h * B_HD, (h + 1) * B_HD)
            pv = jnp.dot(vt_ref[0, g * B_HD:(g + 1) * B_HD, rows], p.astype(BF16),
                         preferred_element_type=F32)
            acc_ref[hs, :] = alpha[h:h + 1, :] * acc_ref[hs, :] + pv
        l_ref[...] = alpha * l_ref[...] + jnp.concatenate(sums, axis=0)

    ab = near_ref.shape[2]
    n_far = jnp.maximum(q0 // ab - 1, 0)
    far_bias2 = far_ref[:, 0:1]

    def far_body(i, c):
        attend(i * ab, ab, lambda h: far_bias2[h:h + 1, :])
        return c

    lax.fori_loop(0, n_far, far_body, 0)

    def near_body(i, c):
        dd = i - q0 // ab + 1
        attend(i * ab, ab, lambda h: near_ref[dd, h])
        return c

    lax.fori_loop(n_far, nkb * kb_sz // ab, near_body, 0)

    _finish_heads(o_ref, x_ref, wout_ref, l_ref, acc_ref, B_HEADS)


def _dsa_attention_prompt(qt, qit, kwt, kw, k, vt, far, near, x, w_out):
    b, d, t = qt.shape
    tq = near.shape[3]
    top_k = min(TOPK_MAX, t // 4)
    ab = near.shape[2]
    assert t % tq == 0 and tq % CHUNK == 0 and tq % ab == 0 and ab % LANE == 0 and tq % (2 * LANE) == 0
    return pl.pallas_call(
        functools.partial(_dsa_prompt_kernel, top_k=top_k),
        grid=(b, t // tq),
        in_specs=[pl.BlockSpec((1, d, tq), lambda i, j: (i, 0, j)),
                  pl.BlockSpec((1, qit.shape[1], tq), lambda i, j: (i, 0, j)),
                  pl.BlockSpec((1, LANE, tq), lambda i, j: (i, 0, j)),
                  pl.BlockSpec((1, t, LANE), lambda i, j: (i, 0, 0)),
                  pl.BlockSpec((1, t, B_KV * B_HD), lambda i, j: (i, 0, 0)),
                  pl.BlockSpec((1, B_KV * B_HD, t), lambda i, j: (i, 0, 0)),
                  pl.BlockSpec(far.shape, lambda i, j: (0, 0)),
                  pl.BlockSpec(near.shape, lambda i, j: (0, 0, 0, 0)),
                  pl.BlockSpec((1, tq, d), lambda i, j: (i, j, 0)),
                  pl.BlockSpec((d, d), lambda i, j: (0, 0))],
        out_specs=pl.BlockSpec((1, tq, d), lambda i, j: (i, j, 0)),
        out_shape=jax.ShapeDtypeStruct((b, t, d), F32),
        scratch_shapes=[pltpu.VMEM((t, tq), I32), pltpu.VMEM((t, tq), F32),
                        pltpu.VMEM((B_HEADS, tq), F32), pltpu.VMEM((B_HEADS, tq), F32),
                        pltpu.VMEM((d, tq), F32), pltpu.VMEM((B_HEADS, ab, tq), F32)],
        compiler_params=_cparams("parallel", "parallel"),
        name="dsa_attention",
    )(qt, qit, kwt, kw, k, vt, far, near, x, w_out)


def _dsa_cached_kernel(q_ref, qi_ref, wi_ref, rb_ref, kidx_ref, kp_ref, vp_ref, kn_ref, vn_ref, o_ref,
                       *, past, t_new, top_k):
    n_keys = past + t_new
    lp = kidx_ref.shape[1]
    negkey = _neg_inf_key()
    kpos = lax.broadcasted_iota(I32, (t_new, lp), 1)
    qpos = past + lax.broadcasted_iota(I32, (t_new, lp), 0)
    adm = kpos < (qpos // CHUNK + 1) * CHUNK

    s = _dot_nt(qi_ref[0], kidx_ref[0])
    w = jnp.maximum(s, 0.0) * wi_ref[0]
    sc = w[0:t_new]
    for h in range(1, IDX_HEADS):
        sc = sc + w[h * t_new:(h + 1) * t_new]
    sc = sc * ((IDX_DIM * IDX_HEADS) ** -0.5)
    key = jnp.where(adm, _sortable(sc), negkey)
    key = jnp.where(kpos < n_keys, key, INT_MIN)

    def bit_body(i, t_u):
        cand_u = t_u | jnp.left_shift(jnp.int32(1), 31 - i)
        cnt = jnp.sum(jnp.where(key >= (cand_u ^ INT_MIN), 1.0, 0.0), axis=1, keepdims=True)
        return jnp.where(cnt >= top_k, cand_u, t_u)

    t_s = lax.fori_loop(0, 32, bit_body, jnp.zeros((t_new, 1), I32)) ^ INT_MIN

    adm01 = jnp.where(adm, 1.0, 0.0)
    gt = jnp.where(key > t_s, adm01, 0.0)
    eq = jnp.where(key == t_s, adm01, 0.0)
    need = top_k - jnp.sum(jnp.where(key > t_s, 1.0, 0.0), axis=1, keepdims=True)
    r = lax.broadcasted_iota(I32, (LANE, LANE), 0)
    c = lax.broadcasted_iota(I32, (LANE, LANE), 1)
    utri = jnp.where(r < c, 1.0, 0.0).astype(BF16)
    carry = jnp.zeros((t_new, 1), F32)
    keeps = []
    for blk in range(lp // LANE):
        e = eq[:, blk * LANE:(blk + 1) * LANE]
        rank = carry + jnp.dot(e.astype(BF16), utri, preferred_element_type=F32)
        keeps.append(jnp.where(rank < need, e, 0.0))
        carry = carry + jnp.sum(e, axis=1, keepdims=True)
    sel = gt + jnp.concatenate(keeps, axis=1)

    rel = kpos - qpos
    nb = N_BUCKETS // 2
    max_exact = nb // 2
    n = jnp.abs(rel)
    nf = jnp.maximum(n, 1).astype(F32)
    large = max_exact + (jnp.log(nf / max_exact) / math.log(MAX_DIST / max_exact)
                         * (nb - max_exact)).astype(I32)
    bucket = (rel > 0).astype(I32) * nb + jnp.where(n < max_exact, n, jnp.minimum(large, nb - 1))

    rows = B_REP * t_new
    sel_g = jnp.concatenate([sel] * B_REP, axis=0) != 0.0
    bucket_g = jnp.concatenate([bucket] * B_REP, axis=0)
    for g in range(B_KV):
        grp = lambda ref, n: ref[0, pl.ds(g, n, stride=B_KV), :]
        qg = q_ref[0, g * rows:(g + 1) * rows, :]
        rb = rb_ref[g * rows:(g + 1) * rows, :]
        bias = jnp.zeros((rows, lp), F32)
        for j in range(N_BUCKETS):
            bias = jnp.where(bucket_g == j, rb[:, j:j + 1], bias)
        zp = _dot_nt(qg, grp(kp_ref, past)) * (B_HD ** -0.5)
        zn = _dot_nt(qg, grp(kn_ref, t_new)) * (B_HD ** -0.5)
        ap = jnp.where(sel_g[:, :past], zp + bias[:, :past], NEG_INF)
        an = jnp.where(sel_g[:, past:n_keys], zn + bias[:, past:n_keys], NEG_INF)
        m = jnp.maximum(jnp.max(ap, axis=1, keepdims=True), jnp.max(an, axis=1, keepdims=True))
        pp, pn = jnp.exp(ap - m), jnp.exp(an - m)
        l = jnp.sum(pp, axis=1, keepdims=True) + jnp.sum(pn, axis=1, keepdims=True)
        o_ref[0, g * rows:(g + 1) * rows, :] = (_dot(pp, grp(vp_ref, past)) + _dot(pn, grp(vn_ref, t_new))) / l


def _dsa_attention_cached(q_rows, qi_rows, wi_col, rb_rows, kidx_all, k_past, v_past, k_new, v_new):
    b, rows, hd = q_rows.shape
    past, t_new = k_past.shape[1], k_new.shape[1]
    lp = kidx_all.shape[1]
    top_k = min(TOPK_MAX, (past + t_new) // 4)
    assert past % LANE == 0
    flat = lambda a: a.reshape(b, a.shape[1] * B_KV, B_HD)
    kv_spec = lambda n: pl.BlockSpec((1, n * B_KV, B_HD), lambda i: (i, 0, 0))
    return pl.pallas_call(
        functools.partial(_dsa_cached_kernel, past=past, t_new=t_new, top_k=top_k),
        grid=(b,),
        in_specs=[pl.BlockSpec((1, rows, hd), lambda i: (i, 0, 0)),
                  pl.BlockSpec((1,) + qi_rows.shape[1:], lambda i: (i, 0, 0)),
                  pl.BlockSpec((1,) + wi_col.shape[1:], lambda i: (i, 0, 0)),
                  pl.BlockSpec(rb_rows.shape, lambda i: (0, 0)),
                  pl.BlockSpec((1, lp, IDX_DIM), lambda i: (i, 0, 0)),
                  kv_spec(past), kv_spec(past), kv_spec(t_new), kv_spec(t_new)],
        out_specs=pl.BlockSpec((1, rows, hd), lambda i: (i, 0, 0)),
        out_shape=jax.ShapeDtypeStruct((b, rows, hd), F32),
        compiler_params=_cparams("parallel"),
        name="dsa_attention_cached",
    )(q_rows, qi_rows, wi_col, rb_rows, kidx_all, flat(k_past), flat(v_past), flat(k_new), flat(v_new))


def _cumsum_kernel(x_ref, o_ref):
    x = x_ref[0]
    n = x.shape[-1]
    lane = lax.broadcasted_iota(I32, x.shape, 1)
    s = 1
    while s < n:
        x = x + jnp.where(lane >= s, pltpu.roll(x, s, 1), 0.0)
        s *= 2
    o_ref[0] = x


def _cumsum_lanes(x):
    b, h, n = x.shape
    return pl.pallas_call(
        _cumsum_kernel,
        grid=(b,),
        in_specs=[pl.BlockSpec((1, h, n), lambda i: (i, 0, 0))],
        out_specs=pl.BlockSpec((1, h, n), lambda i: (i, 0, 0)),
        out_shape=jax.ShapeDtypeStruct((b, h, n), F32),
        compiler_params=_cparams("parallel"),
        name="logf_cumsum",
    )(x)


def _fox_init(m_ref, l_ref, acc_ref):
    m_ref[...] = jnp.full(m_ref.shape, NEG_INF, F32)
    l_ref[...] = jnp.zeros(l_ref.shape, F32)
    acc_ref[...] = jnp.zeros(acc_ref.shape, F32)


def _fox_tile(z_fn, pv_fn, cq, ck, mask, m_ref, l_ref, acc_ref, a_ref):
    c1 = (D_HD ** -0.5) * LOG2E
    cq2, ck2 = cq * LOG2E, ck * LOG2E
    cols = []
    for h in range(D_HEADS):
        a = z_fn(h) * c1 - ck2[:, h:h + 1]
        if mask is not None:
            a = jnp.where(mask, a, NEG_INF)
        a_ref[h] = a
        cols.append(jnp.max(a, axis=0, keepdims=True))
    m_old = m_ref[...]
    m_new = jnp.maximum(m_old, jnp.concatenate(cols, axis=0) + cq2)
    alpha = jnp.exp2(m_old - m_new)
    shift = m_new - cq2
    m_ref[...] = m_new
    sums = []
    for h in range(D_HEADS):
        p = jnp.exp2(a_ref[h] - shift[h:h + 1, :])
        sums.append(jnp.sum(p, axis=0, keepdims=True))
        hs = slice(h * D_HD, (h + 1) * D_HD)
        acc_ref[hs, :] = alpha[h:h + 1, :] * acc_ref[hs, :] + pv_fn(h, p.astype(BF16))
    l_ref[...] = alpha * l_ref[...] + jnp.concatenate(sums, axis=0)


def _hs(h):
    return slice(h * D_HD, (h + 1) * D_HD)


def _fox_prompt_kernel(qt_ref, k_ref, vt_ref, cq_ref, ck_ref, x_ref, wout_ref, o_ref,
                       m_ref, l_ref, acc_ref, a_ref):
    qi, ki = pl.program_id(1), pl.program_id(2)
    tq, tk = qt_ref.shape[2], k_ref.shape[1]
    q0, k0 = qi * tq, ki * tk

    @pl.when(ki == 0)
    def _():
        _fox_init(m_ref, l_ref, acc_ref)

    def run(masked):
        mask = None
        if masked:
            mask = (k0 + lax.broadcasted_iota(I32, (tk, tq), 0)) <= (q0 + lax.broadcasted_iota(I32, (tk, tq), 1))
        _fox_tile(lambda h: jnp.dot(k_ref[0, :, _hs(h)], qt_ref[0, _hs(h), :], preferred_element_type=F32),
                  lambda h, p: jnp.dot(vt_ref[0, _hs(h), :], p, preferred_element_type=F32),
                  cq_ref[0], ck_ref[0], mask, m_ref, l_ref, acc_ref, a_ref)

    fully_visible = k0 + tk - 1 <= q0
    pl.when(fully_visible)(lambda: run(False))
    pl.when(jnp.logical_and(jnp.logical_not(fully_visible), k0 <= q0 + tq - 1))(lambda: run(True))

    @pl.when(ki == pl.num_programs(2) - 1)
    def _():
        _finish_heads(o_ref, x_ref, wout_ref, l_ref, acc_ref, D_HEADS)


def _fox_attention_prompt(qt, k, vt, cum_t, cum, x, w_out, tq, tk):
    b, d, t = qt.shape
    nq, nk = t // tq, t // tk
    last = lambda j: (j * tq + tq - 1) // tk
    return pl.pallas_call(
        _fox_prompt_kernel,
        grid=(b, nq, nk),
        in_specs=[pl.BlockSpec((1, d, tq), lambda i, j, kk: (i, 0, j)),
                  pl.BlockSpec((1, tk, d), lambda i, j, kk: (i, jnp.minimum(kk, last(j)), 0)),
                  pl.BlockSpec((1, d, tk), lambda i, j, kk: (i, 0, jnp.minimum(kk, last(j)))),
                  pl.BlockSpec((1, D_HEADS, tq), lambda i, j, kk: (i, 0, j)),
                  pl.BlockSpec((1, tk, D_HEADS), lambda i, j, kk: (i, jnp.minimum(kk, last(j)), 0)),
                  pl.BlockSpec((1, tq, d), lambda i, j, kk: (i, j, 0)),
                  pl.BlockSpec((d, d), lambda i, j, kk: (0, 0))],
        out_specs=pl.BlockSpec((1, tq, d), lambda i, j, kk: (i, j, 0)),
        out_shape=jax.ShapeDtypeStruct((b, t, d), F32),
        scratch_shapes=[pltpu.VMEM((D_HEADS, tq), F32), pltpu.VMEM((D_HEADS, tq), F32),
                        pltpu.VMEM((d, tq), F32), pltpu.VMEM((D_HEADS, tk, tq), F32)],
        compiler_params=_cparams("parallel", "parallel", "arbitrary"),
        name="fox_attention",
    )(qt, k, vt, cum_t, cum, x, w_out)


def _fox_cached_kernel(q_ref, kp_ref, vp_ref, kn_ref, vn_ref, cq_ref, ckp_ref, ckn_ref, o_ref,
                       m_ref, l_ref, acc_ref, *, t_new):
    ki = pl.program_id(1)
    n_past = pl.num_programs(1) - 1
    rows = q_ref.shape[1]
    c1 = (D_HD ** -0.5) * LOG2E

    @pl.when(ki == 0)
    def _():
        m_ref[...] = jnp.full(m_ref.shape, NEG_INF, F32)
        l_ref[...] = jnp.zeros(l_ref.shape, F32)
        acc_ref[...] = jnp.zeros(acc_ref.shape, F32)

    def tile(k2d, v2d, ck_row, causal):
        cols = k2d.shape[0]
        a = _dot_nt(q_ref[0], k2d) * c1 - ck_row * LOG2E
        r = lax.broadcasted_iota(I32, (rows, cols), 0)
        c = lax.broadcasted_iota(I32, (rows, cols), 1)
        ok = (c % D_HEADS) == (r // t_new)
        if causal:
            ok = jnp.logical_and(ok, (c // D_HEADS) <= (r % t_new))
        a = jnp.where(ok, a, NEG_INF)
        cq2 = cq_ref[0] * LOG2E
        m_old = m_ref[...]
        m_new = jnp.maximum(m_old, jnp.max(a, axis=1, keepdims=True) + cq2)
        alpha = jnp.exp2(m_old - m_new)
        p = jnp.exp2(a - (m_new - cq2))
        l_ref[...] = alpha * l_ref[...] + jnp.sum(p, axis=1, keepdims=True)
        acc_ref[...] = alpha * acc_ref[...] + _dot(p, v2d)
        m_ref[...] = m_new

    @pl.when(ki < n_past)
    def _():
        tk = kp_ref.shape[1]
        tile(kp_ref[0].reshape(tk * D_HEADS, D_HD), vp_ref[0].reshape(tk * D_HEADS, D_HD), ckp_ref[0], False)

    @pl.when(ki == n_past)
    def _():
        tile(kn_ref[0].reshape(t_new * D_HEADS, D_HD), vn_ref[0].reshape(t_new * D_HEADS, D_HD), ckn_ref[0], True)
        o_ref[0] = acc_ref[...] / l_ref[...]


def _fox_attention_cached(q_rows, k_past, v_past, k_new, v_new, cq_col, ck_past, ck_new, tk):
    b, rows, hd = q_rows.shape
    past, t_new = k_past.shape[1], k_new.shape[1]
    n_past = past // tk
    pidx = lambda i, kk: (i, jnp.minimum(kk, n_past - 1), 0, 0)
    return pl.pallas_call(
        functools.partial(_fox_cached_kernel, t_new=t_new),
        grid=(b, n_past + 1),
        in_specs=[pl.BlockSpec((1, rows, hd), lambda i, kk: (i, 0, 0)),
                  pl.BlockSpec((1, tk, D_HEADS, D_HD), pidx),
                  pl.BlockSpec((1, tk, D_HEADS, D_HD), pidx),
                  pl.BlockSpec((1, t_new, D_HEADS, D_HD), lambda i, kk: (i, 0, 0, 0)),
                  pl.BlockSpec((1, t_new, D_HEADS, D_HD), lambda i, kk: (i, 0, 0, 0)),
                  pl.BlockSpec((1, rows, 1), lambda i, kk: (i, 0, 0)),
                  pl.BlockSpec((1, 1, tk * D_HEADS), lambda i, kk: (i, 0, jnp.minimum(kk, n_past - 1))),
                  pl.BlockSpec((1, 1, t_new * D_HEADS), lambda i, kk: (i, 0, 0))],
        out_specs=pl.BlockSpec((1, rows, hd), lambda i, kk: (i, 0, 0)),
        out_shape=jax.ShapeDtypeStruct((b, rows, hd), F32),
        scratch_shapes=[pltpu.VMEM((rows, 1), F32), pltpu.VMEM((rows, 1), F32), pltpu.VMEM((rows, hd), F32)],
        compiler_params=_cparams("parallel", "arbitrary"),
        name="fox_attention_cached",
    )(q_rows, k_past, v_past, k_new, v_new, cq_col, ck_past, ck_new)


def _pad_rows(a, rows):
    if a.shape[1] == rows:
        return a
    return jnp.pad(a, ((0, 0), (0, rows - a.shape[1])) + ((0, 0),) * (a.ndim - 2))


def _round_up(n, m):
    return -(-n // m) * m


def _dsa_mixer(x, g, k_past, v_past, ki_past, w, w_out, rel_bias):
    b, t, d = x.shape
    past = k_past.shape[1]
    n_keys = past + t
    if past == 0:
        qt, k4, kb, v4, vt, qit, kw, kwt = _proj(
            x, g, w, ((0, "t", BF16, None), (1, B_HD, F32, None), (1, "rows", BF16, None),
                      (2, B_HD, F32, None), (2, "t", BF16, None), (3, "t", BF16, None),
                      (4, "rows", F32, None), (4, "t", F32, None)),
            jnp.zeros((1, LANE), F32))
        far, near = _bias_tiles(rel_bias, 2 * LANE, 2 * LANE)
        y = _dsa_attention_prompt(qt, qit, kwt, kw, kb, vt, far, near, x, w_out)
        return (y, k4, v4, kw[:, :, :IDX_DIM])
    q, k4, v4, qidx, kw = _proj(
        x, g, w, ((0, "rows", BF16, None), (1, B_HD, F32, None), (2, B_HD, F32, None),
                  (3, "rows", BF16, None), (4, "rows", F32, None)),
        jnp.zeros((1, LANE), F32))
    ki = kw[:, :, :IDX_DIM]
    to_rows = lambda a, nh: jnp.swapaxes(a.reshape(b, t, nh, -1), 1, 2).reshape(b, nh * t, -1)
    kidx_all = _pad_rows(jnp.concatenate([ki_past, ki], axis=1), _round_up(n_keys, LANE))
    o = _dsa_attention_cached(to_rows(q, B_HEADS), to_rows(qidx, IDX_HEADS),
                              to_rows(kw[:, :, IDX_DIM:IDX_DIM + IDX_HEADS], IDX_HEADS),
                              jnp.repeat(rel_bias.T, t, axis=0), kidx_all, k_past, v_past, k4, v4)
    o = jnp.swapaxes(o.reshape(b, B_HEADS, t, B_HD), 1, 2).reshape(b * t, d)
    y = _out_proj(x.reshape(b * t, d), o, w_out).reshape(b, t, d)
    return (y, k4, v4, ki)


def _fox_mixer(x, g, k_past, v_past, lf_past, w, b_f, w_out):
    b, t, d = x.shape
    past = k_past.shape[1]
    heads4 = ((1, D_HD, F32, None), (2, D_HD, F32, None), (3, "rows", F32, "log_sigmoid"))
    if past == 0:
        tq = tk = min(4 * LANE, t)
        k4, v4, lf, qt, kb, vt = _proj(
            x, g, w, heads4 + ((0, "t", BF16, None), (1, "rows", BF16, None), (2, "t", BF16, None)), b_f)
        logf = lf[:, :, :D_HEADS]
        cum_t = _cumsum_lanes(jnp.swapaxes(logf, 1, 2))
        y = _fox_attention_prompt(qt, kb, vt, cum_t, jnp.swapaxes(cum_t, 1, 2), x, w_out, tq, tk)
        return (y, k4, v4, logf)
    else:
        tk = 2 * LANE
        k4, v4, lf, q = _proj(x, g, w, heads4 + ((0, "rows", BF16, None),), b_f)
        logf = lf[:, :, :D_HEADS]
        lf_all = _pad_rows(jnp.concatenate([lf_past, logf], axis=1), _round_up(past + t, LANE))
        cum = jnp.swapaxes(_cumsum_lanes(jnp.swapaxes(lf_all, 1, 2)), 1, 2)[:, :past + t]
        ck = cum.reshape(b, 1, (past + t) * D_HEADS)
        to_rows = lambda a: jnp.swapaxes(a.reshape(b, t, D_HEADS, -1), 1, 2).reshape(b, D_HEADS * t, -1)
        o = _fox_attention_cached(to_rows(q), k_past, v_past, k4, v4, to_rows(cum[:, past:]),
                                  ck[:, :, :past * D_HEADS], ck[:, :, past * D_HEADS:], tk)
        o = jnp.swapaxes(o.reshape(b, D_HEADS, t, D_HD), 1, 2).reshape(b, t, d)
    y = _out_proj(x.reshape(b * t, d), o.reshape(b * t, d), w_out).reshape(b, t, d)
    return (y, k4, v4, logf)


def _run_group(x, pos0, a_st, b_k, b_v, b_ki, c_st, d_k, d_v, d_lf, mem_k, mem_v, prm):
    b, t, d = x.shape
    depth = prm["norm_mix"].shape[0]
    new = {n: [] for n in ("a", "bk", "bv", "bki", "c", "dk", "dv", "dlf")}
    for i in range(depth):
        kind, j = i % 4, i // 4
        g = prm["norm_mix"][i]
        if kind == 0:
            x, st = _conv_mixer(x, g, prm["a_w_in"][j], prm["a_conv"][j], a_st[j], prm["a_w_out"][j])
            new["a"].append(st)
        elif kind == 1:
            x, kk, vv, ki = _dsa_mixer(x, g, b_k[j], b_v[j], b_ki[j], prm["b_w"][j], prm["b_w_out"][j],
                                       prm["rel_bias"])
            new["bk"].append(kk); new["bv"].append(vv); new["bki"].append(ki)
        elif kind == 2:
            x, st = _pool_mixer(x, g, c_st[j], prm["c_w_group"][j], prm["c_scale"][j], pos0)
            new["c"].append(st)
        else:
            x, kk, vv, lf = _fox_mixer(x, g, d_k[j], d_v[j], d_lf[j], prm["d_w"][j], prm["d_b_f"][j],
                                       prm["d_w_out"][j])
            new["dk"].append(kk); new["dv"].append(vv); new["dlf"].append(lf)
        x = _xattn(x, prm["norm_xattn"], prm["xa_wq"], mem_k, mem_v, prm["xa_wo"], i)
        last = i == depth - 1
        x = _ffn(x.reshape(b * t, d), prm["norm_ffn"], prm["ffn_w1"], prm["ffn_w2"],
                 prm["final_norm"], i, last).reshape(b, t, d)
    return (x,) + tuple(jnp.stack(new[n]) for n in ("a", "bk", "bv", "bki", "c", "dk", "dv", "dlf"))


def kernel(x_prompt, x_sample, state_a_conv, cache_b_k, cache_b_v, cache_b_kidx, state_c_pool,
           cache_d_k, cache_d_v, cache_d_logf, cache_mem_k, cache_mem_v, mem_prompt,
           norm_mix, norm_xattn, norm_mem, norm_ffn, final_norm,
           a_w_in, a_conv, a_w_out, b_w_in, b_w_out, rel_bias, c_w_group, c_scale,
           d_w_in, d_b_f, d_w_out, xa_wq, xa_wkv, xa_wo, ffn_w1, ffn_w2):
    bp, t, d = x_prompt.shape
    depth = norm_mix.shape[0]
    n_b, n_d = b_w_in.shape[0], d_w_in.shape[0]
    bf = lambda w: w.astype(BF16)

    def split_cols(w, widths):
        out, c = [], 0
        for wd in widths:
            piece = w[:, c:c + wd]
            c += wd
            if wd % LANE:
                piece = jnp.pad(piece, ((0, 0), (0, _round_up(wd, LANE) - wd)))
            out.append(bf(piece))
        assert c == w.shape[1]
        return out

    b_q, b_kvw = B_HEADS * B_HD, B_KV * B_HD
    b_w = [split_cols(b_w_in[j], (b_q, b_kvw, b_kvw, IDX_HEADS * IDX_DIM, IDX_DIM + IDX_HEADS))
           for j in range(n_b)]
    d_w = [split_cols(d_w_in[j], (d, d, d, D_HEADS)) for j in range(n_d)]
    d_bf = [jnp.pad(d_b_f[j], (0, LANE - D_HEADS)).reshape(1, LANE) for j in range(n_d)]

    prm = {"norm_mix": norm_mix, "norm_xattn": norm_xattn, "norm_ffn": norm_ffn, "final_norm": final_norm,
           "a_w_in": bf(a_w_in), "a_conv": a_conv, "a_w_out": bf(a_w_out),
           "b_w": b_w, "b_w_out": bf(b_w_out), "rel_bias": rel_bias,
           "c_w_group": bf(c_w_group), "c_scale": c_scale,
           "d_w": d_w, "d_b_f": d_bf, "d_w_out": bf(d_w_out),
           "xa_wq": bf(xa_wq), "xa_wo": bf(xa_wo), "ffn_w1": bf(ffn_w1), "ffn_w2": bf(ffn_w2)}

    n_mem = mem_prompt.shape[1]
    mk, mv, mk_rows, mv_rows = _memory_kv(mem_prompt, norm_mem, bf(xa_wkv))

    n_a, n_c = a_w_in.shape[0], c_w_group.shape[0]
    z = lambda *s: jnp.zeros(s, F32)
    gp = _run_group(x_prompt, 0,
                    z(n_a, bp, CONV_W - 1, d),
                    z(n_b, bp, 0, B_KV, B_HD), z(n_b, bp, 0, B_KV, B_HD), z(n_b, bp, 0, IDX_DIM),
                    z(n_c, bp, POOL_STATE, d),
                    z(n_d, bp, 0, D_HEADS, D_HD), z(n_d, bp, 0, D_HEADS, D_HD), z(n_d, bp, 0, D_HEADS),
                    mk_rows, mv_rows, prm)

    bs = x_sample.shape[0]
    past_len = cache_b_k.shape[2]
    gs = _run_group(x_sample, past_len, state_a_conv, cache_b_k, cache_b_v, cache_b_kidx, state_c_pool,
                    cache_d_k, cache_d_v, cache_d_logf, cache_mem_k, cache_mem_v, prm)

    (y_p, a_p, bk_p, bv_p, bki_p, c_p, dk_p, dv_p, dlf_p) = gp
    (y_s, a_s, bk_s, bv_s, bki_s, c_s, dk_s, dv_s, dlf_s) = gs
    return (y_p, y_s, a_p, a_s, bk_p, bv_p, bki_p, bk_s, bv_s, bki_s, c_p, c_s,
            dk_p, dv_p, dlf_p, dk_s, dv_s, dlf_s, mk, mv)
```

```python
import functools
import math

import jax
import jax.numpy as jnp
from jax import lax
from jax.experimental import pallas as pl
from jax.experimental.pallas import tpu as pltpu

F32 = jnp.float32
BF16 = jnp.bfloat16
I32 = jnp.int32

EPS = 1e-6
NEG_INF = -1e30
LOG2E = math.log2(math.e)
CHUNK = 64
LANE = 128
VMEM_LIMIT = 48 * 1024 * 1024

CONV_W = 3
POOL_WINDOWS = (2, 4, 8, 16)
POOL_STATE = max(POOL_WINDOWS) - 1
B_HEADS, B_KV, B_HD = 8, 2, 128
B_REP = B_HEADS // B_KV
IDX_HEADS, IDX_DIM = 8, 64
TOPK_MAX = 256
N_BUCKETS, MAX_DIST = 32, 128
D_HEADS, D_HD = 8, 128
MEM_HEADS = 4
INT_MIN = -2147483648
BIAS_CENTER = 2 * LANE


def _cparams(*sem):
    return pltpu.CompilerParams(dimension_semantics=sem, vmem_limit_bytes=VMEM_LIMIT)


def _dot(a, b):
    return jnp.dot(a.astype(BF16), b.astype(BF16), preferred_element_type=F32)


def _dot_nt(a, b):
    return lax.dot_general(a.astype(BF16), b.astype(BF16), (((1,), (1,)), ((), ())),
                           preferred_element_type=F32)


def _dot_tn(a, b):
    return lax.dot_general(a.astype(BF16), b.astype(BF16), (((0,), (0,)), ((), ())),
                           preferred_element_type=F32)


def _rms(x, g):
    return x * lax.rsqrt(jnp.mean(x * x, axis=-1, keepdims=True) + EPS) * g


def _finish_heads(o_ref, x_ref, w_ref, l_ref, acc_ref, n_heads):
    hd = acc_ref.shape[0] // n_heads
    inv_l = 1.0 / l_ref[...]
    heads_t = jnp.concatenate([acc_ref[h * hd:(h + 1) * hd, :] * inv_l[h:h + 1, :] for h in range(n_heads)],
                              axis=0)
    o_ref[0] = x_ref[0] + _dot_tn(heads_t, w_ref[...])


def _row_tile(n, cap):
    t = min(n, cap)
    assert n % t == 0
    return t


def _memkv_kernel(mem_ref, g_ref, w_ref, k_ref, v_ref, kb_ref, vb_ref):
    m = mem_ref[0]
    mn = m * lax.rsqrt(jnp.mean(m * m, axis=-1, keepdims=True) + EPS)
    h = (mn * g_ref[0]).astype(BF16)
    d = m.shape[-1]
    hd = d // MEM_HEADS
    k = jnp.dot(h, w_ref[0, :, :d], preferred_element_type=F32)
    v = jnp.dot(h, w_ref[0, :, d:], preferred_element_type=F32)
    kb_ref[0, 0] = k.astype(BF16)
    vb_ref[0, 0] = v.astype(BF16)
    k_ref[0, 0] = pltpu.einshape("m(hd)->mhd", k, d=hd)
    v_ref[0, 0] = pltpu.einshape("m(hd)->mhd", v, d=hd)


def _memory_kv(mem, g_mem, w_kv):
    depth, d = g_mem.shape
    b, nm, _ = mem.shape
    hd = d // MEM_HEADS
    out = jax.ShapeDtypeStruct((depth, b, nm, MEM_HEADS, hd), F32)
    out_b = jax.ShapeDtypeStruct((depth, b, nm, d), BF16)
    heads_spec = pl.BlockSpec((1, 1, nm, MEM_HEADS, hd), lambda l, i: (l, i, 0, 0, 0))
    rows_spec = pl.BlockSpec((1, 1, nm, d), lambda l, i: (l, i, 0, 0))
    return pl.pallas_call(
        _memkv_kernel,
        grid=(depth, b),
        in_specs=[pl.BlockSpec((1, nm, d), lambda l, i: (i, 0, 0)),
                  pl.BlockSpec((1, 1, d), lambda l, i: (l, 0, 0)),
                  pl.BlockSpec((1, d, 2 * d), lambda l, i: (l, 0, 0))],
        out_specs=[heads_spec, heads_spec, rows_spec, rows_spec],
        out_shape=[out, out, out_b, out_b],
        compiler_params=_cparams("parallel", "parallel"),
        name="memory_kv",
    )(mem, g_mem.reshape(depth, 1, d), w_kv)


def _ffn_kernel(x_ref, g_ref, w1_ref, w2_ref, gf_ref, o_ref, h_ref, acc_ref, *, final_norm):
    j = pl.program_id(1)

    @pl.when(j == 0)
    def _():
        h_ref[...] = _rms(x_ref[...], g_ref[...]).astype(BF16)
        acc_ref[...] = jnp.zeros_like(acc_ref)

    u = jnp.maximum(jnp.dot(h_ref[...], w1_ref[...], preferred_element_type=F32), 0.0)
    acc_ref[...] += jnp.dot((u * u).astype(BF16), w2_ref[...], preferred_element_type=F32)

    @pl.when(j == pl.num_programs(1) - 1)
    def _():
        y = x_ref[...] + acc_ref[...]
        o_ref[...] = _rms(y, gf_ref[...]) if final_norm else y


def _ffn(x, g, w1, w2, gf, layer, final_norm):
    n, d = x.shape
    f = w1.shape[2]
    tm = _row_tile(n, 1024)
    tf = 1024
    return pl.pallas_call(
        functools.partial(_ffn_kernel, final_norm=final_norm),
        grid=(n // tm, f // tf),
        in_specs=[pl.BlockSpec((tm, d), lambda i, j: (i, 0)),
                  pl.BlockSpec((None, 1, d), lambda i, j: (layer, 0, 0)),
                  pl.BlockSpec((None, d, tf), lambda i, j: (layer, 0, j)),
                  pl.BlockSpec((None, tf, d), lambda i, j: (layer, j, 0)),
                  pl.BlockSpec((1, d), lambda i, j: (0, 0))],
        out_specs=pl.BlockSpec((tm, d), lambda i, j: (i, 0)),
        out_shape=jax.ShapeDtypeStruct((n, d), F32),
        scratch_shapes=[pltpu.VMEM((tm, d), BF16), pltpu.VMEM((tm, d), F32)],
        compiler_params=_cparams("parallel", "arbitrary"),
        name="ffn",
    )(x, g.reshape(-1, 1, d), w1, w2, gf.reshape(1, d))


def _xattn_kernel(x_ref, g_ref, wq_ref, mk_ref, mv_ref, wo_ref, o_ref):
    x = x_ref[0]
    d = x.shape[-1]
    hd = d // MEM_HEADS
    h = _rms(x, g_ref[...]).astype(BF16)
    q = jnp.dot(h, wq_ref[...], preferred_element_type=F32)
    outs = []
    if len(mk_ref.shape) == 3:
        mk = pltpu.einshape("mhd->m(hd)", mk_ref[...]).astype(BF16)
        mv = pltpu.einshape("mhd->m(hd)", mv_ref[...]).astype(BF16)
    else:
        mk, mv = mk_ref[...], mv_ref[...]
    for hh in range(MEM_HEADS):
        sl = slice(hh * hd, (hh + 1) * hd)
        kh, vh = mk[:, sl], mv[:, sl]
        s = _dot_nt(q[:, sl], kh) * (hd ** -0.5)
        m = jnp.max(s, axis=-1, keepdims=True)
        p = jnp.exp(s - m)
        l = jnp.sum(p, axis=-1, keepdims=True)
        outs.append(_dot(p, vh) / l)
    o = jnp.concatenate(outs, axis=-1)
    o_ref[0] = x + _dot(o, wo_ref[...])


def _xattn(x, g, wq, mk, mv, wo, layer):
    b, t, d = x.shape
    tm = _row_tile(t, 512)
    kv_spec = pl.BlockSpec((None, None) + mk.shape[2:], lambda i, j: (layer, i) + (0,) * (mk.ndim - 2))
    return pl.pallas_call(
        _xattn_kernel,
        grid=(b, t // tm),
        in_specs=[pl.BlockSpec((1, tm, d), lambda i, j: (i, j, 0)),
                  pl.BlockSpec((None, 1, d), lambda i, j: (layer, 0, 0)),
                  pl.BlockSpec((None, d, d), lambda i, j: (layer, 0, 0)),
                  kv_spec, kv_spec,
                  pl.BlockSpec((None, d, d), lambda i, j: (layer, 0, 0))],
        out_specs=pl.BlockSpec((1, tm, d), lambda i, j: (i, j, 0)),
        out_shape=jax.ShapeDtypeStruct((b, t, d), F32),
        compiler_params=_cparams("parallel", "parallel"),
        name="xattn",
    )(x, g.reshape(-1, 1, d), wq, mk, mv, wo)


def _conv_kernel(x_ref, g_ref, win_ref, wc_ref, st_ref, wout_ref, o_ref, nst_ref, z_ref):
    t = pl.program_id(1)
    x = x_ref[0]
    tm, d = x.shape
    pad = 8

    @pl.when(t == 0)
    def _():
        z_ref[pad - 2:pad, :] = st_ref[0]

    h = _rms(x, g_ref[...]).astype(BF16)
    bg = jnp.dot(h, win_ref[:, 0:d], preferred_element_type=F32)
    cg = jnp.dot(h, win_ref[:, d:2 * d], preferred_element_type=F32)
    u = jnp.dot(h, win_ref[:, 2 * d:3 * d], preferred_element_type=F32)
    z = cg * u
    z_ref[pad:pad + tm, :] = z
    conv = (z_ref[pad - 2:pad - 2 + tm, :] * wc_ref[0:1, :]
            + z_ref[pad - 1:pad - 1 + tm, :] * wc_ref[1:2, :]
            + z * wc_ref[2:3, :])
    o_ref[0] = x + _dot(bg * conv, wout_ref[...])
    last = z_ref[pad + tm - 2:pad + tm, :]
    z_ref[pad - 2:pad, :] = last

    @pl.when(t == pl.num_programs(1) - 1)
    def _():
        nst_ref[0] = last


def _conv_mixer(x, g, w_in, w_conv, state, w_out):
    b, t, d = x.shape
    tm = _row_tile(t, 512)
    return pl.pallas_call(
        _conv_kernel,
        grid=(b, t // tm),
        in_specs=[pl.BlockSpec((1, tm, d), lambda i, j: (i, j, 0)),
                  pl.BlockSpec((1, d), lambda i, j: (0, 0)),
                  pl.BlockSpec((d, 3 * d), lambda i, j: (0, 0)),
                  pl.BlockSpec((CONV_W, d), lambda i, j: (0, 0)),
                  pl.BlockSpec((1, CONV_W - 1, d), lambda i, j: (i, 0, 0)),
                  pl.BlockSpec((d, d), lambda i, j: (0, 0))],
        out_specs=[pl.BlockSpec((1, tm, d), lambda i, j: (i, j, 0)),
                   pl.BlockSpec((1, CONV_W - 1, d), lambda i, j: (i, 0, 0))],
        out_shape=[jax.ShapeDtypeStruct((b, t, d), F32),
                   jax.ShapeDtypeStruct((b, CONV_W - 1, d), F32)],
        scratch_shapes=[pltpu.VMEM((tm + 8, d), F32)],
        compiler_params=_cparams("parallel", "arbitrary"),
        name="conv_mixer",
    )(x, g.reshape(1, d), w_in, w_conv, state, w_out)


def _pool_kernel(x_ref, g_ref, st_ref, wg_ref, sc_ref, o_ref, nst_ref, h_ref, *, pos0):
    t = pl.program_id(1)
    x = x_ref[0]
    tm, d = x.shape
    gw = d // len(POOL_WINDOWS)
    base = POOL_STATE + 1

    @pl.when(t == 0)
    def _():
        h_ref[1:base, :] = st_ref[0]

    h = _rms(x, g_ref[...])
    h_ref[base:base + tm, :] = h
    pos = pos0 + t * tm + lax.broadcasted_iota(I32, (tm, gw), 0)
    ys = []
    for gi, w in enumerate(POOL_WINDOWS):
        sl = slice(gi * gw, (gi + 1) * gw)
        win = h[:, sl]
        for j in range(1, w):
            win = win + h_ref[base - j:base - j + tm, sl]
        count = jnp.minimum(w, pos + 1).astype(F32)
        dlt = win / count - h[:, sl]
        ys.append(_dot(dlt, wg_ref[gi]))
    y = jnp.concatenate(ys, axis=-1) * sc_ref[...]
    o_ref[0] = x + y
    last = h_ref[tm + 1:tm + base, :]
    h_ref[1:base, :] = last

    @pl.when(t == pl.num_programs(1) - 1)
    def _():
        nst_ref[0] = last


def _pool_mixer(x, g, state, w_group, scale, pos0):
    b, t, d = x.shape
    ng, gw, _ = w_group.shape
    tm = _row_tile(t, 512)
    return pl.pallas_call(
        functools.partial(_pool_kernel, pos0=pos0),
        grid=(b, t // tm),
        in_specs=[pl.BlockSpec((1, tm, d), lambda i, j: (i, j, 0)),
                  pl.BlockSpec((1, d), lambda i, j: (0, 0)),
                  pl.BlockSpec((1, POOL_STATE, d), lambda i, j: (i, 0, 0)),
                  pl.BlockSpec((ng, gw, gw), lambda i, j: (0, 0, 0)),
                  pl.BlockSpec((1, d), lambda i, j: (0, 0))],
        out_specs=[pl.BlockSpec((1, tm, d), lambda i, j: (i, j, 0)),
                   pl.BlockSpec((1, POOL_STATE, d), lambda i, j: (i, 0, 0))],
        out_shape=[jax.ShapeDtypeStruct((b, t, d), F32),
                   jax.ShapeDtypeStruct((b, POOL_STATE, d), F32)],
        scratch_shapes=[pltpu.VMEM((tm + POOL_STATE + 1, d), F32)],
        compiler_params=_cparams("parallel", "arbitrary"),
        name="pool_mixer",
    )(x, g.reshape(1, d), state, w_group, scale.reshape(1, d))


def _proj_kernel(*refs, n_w, outs):
    x_ref, g_ref = refs[0], refs[1]
    w_refs = refs[2:2 + n_w]
    e_ref = refs[2 + n_w]
    o_refs = refs[3 + n_w:]
    h = _rms(x_ref[0], g_ref[...]).astype(BF16)
    ys = {}
    for (wi, mode, _, ep), o_ref in zip(outs, o_refs):
        if wi not in ys:
            ys[wi] = jnp.dot(h, w_refs[wi][...], preferred_element_type=F32)
        y = ys[wi]
        if ep == "log_sigmoid":
            u = -(y + e_ref[...])
            y = -(jnp.maximum(u, 0.0) + jnp.log1p(jnp.exp(-jnp.abs(u))))
        if mode == "rows":
            o_ref[0] = y.astype(o_ref.dtype)
        elif mode == "t":
            o_ref[0] = jnp.transpose(y).astype(o_ref.dtype)
        else:
            o_ref[0] = pltpu.einshape("m(hd)->mhd", y.astype(o_ref.dtype), d=mode)


def _proj(x, g, ws, outs, extra):
    b, t, d = x.shape
    tm = _row_tile(t, 512)
    in_specs = [pl.BlockSpec((1, tm, d), lambda i, j: (i, j, 0)), pl.BlockSpec((1, d), lambda i, j: (0, 0))]
    in_specs += [pl.BlockSpec(w.shape, lambda i, j: (0, 0)) for w in ws]
    in_specs += [pl.BlockSpec(extra.shape, lambda i, j: (0, 0))]
    out_specs, out_shape = [], []
    for wi, mode, dt, _ in outs:
        n = ws[wi].shape[1]
        if mode == "rows":
            out_specs.append(pl.BlockSpec((1, tm, n), lambda i, j: (i, j, 0)))
            out_shape.append(jax.ShapeDtypeStruct((b, t, n), dt))
        elif mode == "t":
            out_specs.append(pl.BlockSpec((1, n, tm), lambda i, j: (i, 0, j)))
            out_shape.append(jax.ShapeDtypeStruct((b, n, t), dt))
        else:
            out_specs.append(pl.BlockSpec((1, tm, n // mode, mode), lambda i, j: (i, j, 0, 0)))
            out_shape.append(jax.ShapeDtypeStruct((b, t, n // mode, mode), dt))
    return pl.pallas_call(
        functools.partial(_proj_kernel, n_w=len(ws), outs=tuple(outs)),
        grid=(b, t // tm),
        in_specs=in_specs,
        out_specs=out_specs,
        out_shape=out_shape,
        compiler_params=_cparams("parallel", "parallel"),
        name="norm_proj",
    )(x, g.reshape(1, d), *ws, extra)


def _outproj_kernel(x_ref, a_ref, w_ref, o_ref):
    o_ref[...] = x_ref[...] + _dot(a_ref[...], w_ref[...])


def _out_proj(x, a, w):
    n, d = x.shape
    tm = _row_tile(n, 512)
    return pl.pallas_call(
        _outproj_kernel,
        grid=(n // tm,),
        in_specs=[pl.BlockSpec((tm, d), lambda i: (i, 0)),
                  pl.BlockSpec((tm, d), lambda i: (i, 0)),
                  pl.BlockSpec((d, d), lambda i: (0, 0))],
        out_specs=pl.BlockSpec((tm, d), lambda i: (i, 0)),
        out_shape=jax.ShapeDtypeStruct((n, d), F32),
        compiler_params=_cparams("parallel"),
        name="out_proj",
    )(x, a, w)


def _bias_table_kernel(rbt_ref, o_ref):
    width = o_ref.shape[-1]
    rel = BIAS_CENTER - lax.broadcasted_iota(I32, (1, width), 1)
    nb = N_BUCKETS // 2
    max_exact = nb // 2
    ret = (rel > 0).astype(I32) * nb
    n = jnp.abs(rel)
    nf = jnp.maximum(n, 1).astype(F32)
    large = max_exact + (jnp.log(nf / max_exact) / math.log(MAX_DIST / max_exact)
                         * (nb - max_exact)).astype(I32)
    large = jnp.minimum(large, nb - 1)
    bucket = ret + jnp.where(n < max_exact, n, large)
    acc = jnp.zeros(o_ref.shape, F32)
    for j in range(N_BUCKETS):
        acc = jnp.where(bucket == j, rbt_ref[:, j:j + 1], acc)
    o_ref[...] = acc * LOG2E


def _bias_tiles_kernel(rbt_ref, far_ref, near_ref, tab_ref):
    n_d, nh, kb, tq = near_ref.shape
    _bias_table_kernel(rbt_ref, tab_ref)
    far_ref[...] = jnp.broadcast_to(tab_ref[:, BIAS_CENTER + MAX_DIST:BIAS_CENTER + MAX_DIST + 1], far_ref.shape)
    for dd in range(n_d):
        s0 = BIAS_CENTER - (dd - 1) * kb - kb
        for h in range(nh):
            rows = jnp.broadcast_to(tab_ref[h:h + 1, s0:s0 + tq + kb], (kb, tq + kb))
            near_ref[dd, h] = pltpu.roll(rows, 0, 1, stride=1, stride_axis=0)[:, kb:]


def _bias_tiles(rel_bias, kb, tq):
    nh = rel_bias.shape[1]
    n_d = tq // kb + 1
    width = BIAS_CENTER + kb + tq + kb
    assert kb % LANE == 0 and tq % kb == 0 and kb >= MAX_DIST and BIAS_CENTER >= tq
    return pl.pallas_call(
        _bias_tiles_kernel,
        out_shape=[jax.ShapeDtypeStruct((nh, LANE), F32), jax.ShapeDtypeStruct((n_d, nh, kb, tq), F32)],
        scratch_shapes=[pltpu.VMEM((nh, width), F32)],
        name="bias_tiles",
    )(rel_bias.T)


def _sortable(x):
    x = jnp.where(x == 0.0, 0.0, x)
    bits = lax.bitcast_convert_type(x, I32)
    return jnp.where(bits < 0, bits ^ 0x7FFFFFFF, bits)


def _neg_inf_key():
    import numpy as np
    b = int(np.float32(NEG_INF).view(np.int32))
    return b ^ 0x7FFFFFFF


def _dsa_prompt_kernel(qt_ref, qit_ref, kwt_ref, kw_ref, k_ref, vt_ref, far_ref, near_ref, x_ref, wout_ref, o_ref,
                       key_ref, sel_ref, m_ref, l_ref, acc_ref, a_ref, *, top_k):
    qi = pl.program_id(1)
    tq = qt_ref.shape[2]
    n_keys = kw_ref.shape[1]
    kb_sz = LANE
    q0 = qi * tq
    nkb = jnp.minimum(n_keys, q0 + tq) // kb_sz
    negkey = _neg_inf_key()

    qlane = lax.broadcasted_iota(I32, (1, tq), 1)
    lim = ((q0 + qlane) // CHUNK + 1) * CHUNK
    krow = lax.broadcasted_iota(I32, (kb_sz, tq), 0)

    def kslice(kb):
        return pl.ds(pl.multiple_of(kb * kb_sz, kb_sz), kb_sz)

    sb = 2 * kb_sz
    srow = lax.broadcasted_iota(I32, (sb, tq), 0)

    def score_body(i, c):
        rows = pl.ds(pl.multiple_of(i * sb, sb), sb)
        kid = kw_ref[0, rows, :][:, :IDX_DIM].astype(BF16)
        sc = jnp.zeros((sb, tq), F32)
        for h in range(IDX_HEADS):
            s = jnp.dot(kid, qit_ref[0, h * IDX_DIM:(h + 1) * IDX_DIM, :], preferred_element_type=F32)
            sc = sc + jnp.maximum(s, 0.0) * kwt_ref[0, IDX_DIM + h:IDX_DIM + h + 1, :]
        sc = sc * ((IDX_DIM * IDX_HEADS) ** -0.5)
        key_ref[rows, :] = jnp.where(i * sb + srow < lim, _sortable(sc), negkey)
        return c

    lax.fori_loop(0, nkb // 2, score_body, 0)

    def count(pred_fn):
        def body(i, a):
            for u in range(2):
                kb = 2 * i + u
                ind = pred_fn(kb, key_ref[kslice(kb), :])
                a = a + jnp.sum(ind.reshape(kb_sz // 8, 8, tq), axis=0)
            return a
        a = lax.fori_loop(0, nkb // 2, body, jnp.zeros((8, tq), I32))
        return jnp.sum(a, axis=0, keepdims=True)

    def bit_body(i, t_u):
        cand_u = t_u | jnp.left_shift(jnp.int32(1), 31 - i)
        cand_s = cand_u ^ INT_MIN
        cnt = count(lambda kb, key: jnp.where(key >= cand_s, 1, 0))
        return jnp.where(cnt >= top_k, cand_u, t_u)

    t_s = lax.fori_loop(0, 32, bit_body, jnp.zeros((1, tq), I32)) ^ INT_MIN

    def adm01(kb):
        return jnp.where(kb * kb_sz + krow < lim, 1.0, 0.0)

    def sel_body(kb, a):
        sel = jnp.where(key_ref[kslice(kb), :] >= t_s, adm01(kb), 0.0)
        sel_ref[kslice(kb), :] = sel
        return a + jnp.sum(sel.reshape(kb_sz // 8, 8, tq), axis=0)

    n_sel = jnp.sum(lax.fori_loop(0, nkb, sel_body, jnp.zeros((8, tq), F32)), axis=0, keepdims=True)

    @pl.when(jnp.max(n_sel) > top_k)
    def _():
        n_gt = count(lambda kb, key: jnp.where(key > t_s, 1, 0))
        need = (top_k - n_gt).astype(F32)
        r = lax.broadcasted_iota(I32, (kb_sz, kb_sz), 0)
        c = lax.broadcasted_iota(I32, (kb_sz, kb_sz), 1)
        ltri = jnp.where(c < r, 1.0, 0.0).astype(BF16)

        def tie_body(kb, carry):
            key = key_ref[kslice(kb), :]
            adm = adm01(kb)
            eq = jnp.where(key == t_s, adm, 0.0)
            rank = carry + jnp.dot(ltri, eq.astype(BF16), preferred_element_type=F32)
            keep = jnp.where(rank < need, eq, 0.0)
            sel_ref[kslice(kb), :] = jnp.where(key > t_s, adm, keep)
            return carry + jnp.sum(eq, axis=0, keepdims=True)

        lax.fori_loop(0, nkb, tie_body, jnp.zeros((1, tq), F32))

    m_ref[...] = jnp.full(m_ref.shape, NEG_INF, F32)
    l_ref[...] = jnp.zeros(l_ref.shape, F32)
    acc_ref[...] = jnp.zeros(acc_ref.shape, F32)
    c1 = (B_HD ** -0.5) * LOG2E

    def attend(k0, nk, bias2_fn):
        rows = pl.ds(pl.multiple_of(k0, LANE), nk)
        sel = sel_ref[rows, :] != 0.0
        ks = k_ref[0, rows, :]
        cols = []
        for h in range(B_HEADS):
            g = h // B_REP
            z = jnp.dot(ks[:, g * B_HD:(g + 1) * B_HD], qt_ref[0, h * B_HD:(h + 1) * B_HD, :],
                        preferred_element_type=F32)
            a = jnp.where(sel, z * c1 + bias2_fn(h), NEG_INF)
            a_ref[h, 0:nk, :] = a
            cols.append(jnp.max(a, axis=0, keepdims=True))
        m_old = m_ref[...]
        m_new = jnp.maximum(m_old, jnp.concatenate(cols, axis=0))
        alpha = jnp.exp2(m_old - m_new)
        m_ref[...] = m_new
        sums = []
        for h in range(B_HEADS):
            g = h // B_REP
            p = jnp.exp2(a_ref[h, 0:nk, :] - m_new[h:h + 1, :])
            sums.append(jnp.sum(p, axis=0, keepdims=True))
            hs = slice(h * B_HD, (h + 1) * B_HD)
            pv = jnp.dot(vt_ref[0, g * B_HD:(g + 1) * B_HD, rows], p.astype(BF16),
                         preferred_element_type=F32)
            acc_ref[hs, :] = alpha[h:h + 1, :] * acc_ref[hs, :] + pv
        l_ref[...] = alpha * l_ref[...] + jnp.concatenate(sums, axis=0)

    ab = near_ref.shape[2]
    n_far = jnp.maximum(q0 // ab - 1, 0)
    far_bias2 = far_ref[:, 0:1]

    def far_body(i, c):
        attend(i * ab, ab, lambda h: far_bias2[h:h + 1, :])
        return c

    lax.fori_loop(0, n_far, far_body, 0)

    def near_body(i, c):
        dd = i - q0 // ab + 1
        attend(i * ab, ab, lambda h: near_ref[dd, h])
        return c

    lax.fori_loop(n_far, nkb * kb_sz // ab, near_body, 0)

    _finish_heads(o_ref, x_ref, wout_ref, l_ref, acc_ref, B_HEADS)


def _dsa_attention_prompt(qt, qit, kwt, kw, k, vt, far, near, x, w_out):
    b, d, t = qt.shape
    tq = near.shape[3]
    top_k = min(TOPK_MAX, t // 4)
    ab = near.shape[2]
    assert t % tq == 0 and tq % CHUNK == 0 and tq % ab == 0 and ab % LANE == 0 and tq % (2 * LANE) == 0
    return pl.pallas_call(
        functools.partial(_dsa_prompt_kernel, top_k=top_k),
        grid=(b, t // tq),
        in_specs=[pl.BlockSpec((1, d, tq), lambda i, j: (i, 0, j)),
                  pl.BlockSpec((1, qit.shape[1], tq), lambda i, j: (i, 0, j)),
                  pl.BlockSpec((1, LANE, tq), lambda i, j: (i, 0, j)),
                  pl.BlockSpec((1, t, LANE), lambda i, j: (i, 0, 0)),
                  pl.BlockSpec((1, t, B_KV * B_HD), lambda i, j: (i, 0, 0)),
                  pl.BlockSpec((1, B_KV * B_HD, t), lambda i, j: (i, 0, 0)),
                  pl.BlockSpec(far.shape, lambda i, j: (0, 0)),
                  pl.BlockSpec(near.shape, lambda i, j: (0, 0, 0, 0)),
                  pl.BlockSpec((1, tq, d), lambda i, j: (i, j, 0)),
                  pl.BlockSpec((d, d), lambda i, j: (0, 0))],
        out_specs=pl.BlockSpec((1, tq, d), lambda i, j: (i, j, 0)),
        out_shape=jax.ShapeDtypeStruct((b, t, d), F32),
        scratch_shapes=[pltpu.VMEM((t, tq), I32), pltpu.VMEM((t, tq), F32),
                        pltpu.VMEM((B_HEADS, tq), F32), pltpu.VMEM((B_HEADS, tq), F32),
                        pltpu.VMEM((d, tq), F32), pltpu.VMEM((B_HEADS, ab, tq), F32)],
        compiler_params=_cparams("parallel", "parallel"),
        name="dsa_attention",
    )(qt, qit, kwt, kw, k, vt, far, near, x, w_out)


def _dsa_cached_kernel(q_ref, qi_ref, wi_ref, rb_ref, kidx_ref, kp_ref, vp_ref, kn_ref, vn_ref, o_ref,
                       *, past, t_new, top_k):
    n_keys = past + t_new
    lp = kidx_ref.shape[1]
    negkey = _neg_inf_key()
    kpos = lax.broadcasted_iota(I32, (t_new, lp), 1)
    qpos = past + lax.broadcasted_iota(I32, (t_new, lp), 0)
    adm = kpos < (qpos // CHUNK + 1) * CHUNK

    s = _dot_nt(qi_ref[0], kidx_ref[0])
    w = jnp.maximum(s, 0.0) * wi_ref[0]
    sc = w[0:t_new]
    for h in range(1, IDX_HEADS):
        sc = sc + w[h * t_new:(h + 1) * t_new]
    sc = sc * ((IDX_DIM * IDX_HEADS) ** -0.5)
    key = jnp.where(adm, _sortable(sc), negkey)
    key = jnp.where(kpos < n_keys, key, INT_MIN)

    def bit_body(i, t_u):
        cand_u = t_u | jnp.left_shift(jnp.int32(1), 31 - i)
        cnt = jnp.sum(jnp.where(key >= (cand_u ^ INT_MIN), 1.0, 0.0), axis=1, keepdims=True)
        return jnp.where(cnt >= top_k, cand_u, t_u)

    t_s = lax.fori_loop(0, 32, bit_body, jnp.zeros((t_new, 1), I32)) ^ INT_MIN

    adm01 = jnp.where(adm, 1.0, 0.0)
    gt = jnp.where(key > t_s, adm01, 0.0)
    eq = jnp.where(key == t_s, adm01, 0.0)
    need = top_k - jnp.sum(jnp.where(key > t_s, 1.0, 0.0), axis=1, keepdims=True)
    r = lax.broadcasted_iota(I32, (LANE, LANE), 0)
    c = lax.broadcasted_iota(I32, (LANE, LANE), 1)
    utri = jnp.where(r < c, 1.0, 0.0).astype(BF16)
    carry = jnp.zeros((t_new, 1), F32)
    keeps = []
    for blk in range(lp // LANE):
        e = eq[:, blk * LANE:(blk + 1) * LANE]
        rank = carry + jnp.dot(e.astype(BF16), utri, preferred_element_type=F32)
        keeps.append(jnp.where(rank < need, e, 0.0))
        carry = carry + jnp.sum(e, axis=1, keepdims=True)
    sel = gt + jnp.concatenate(keeps, axis=1)

    rel = kpos - qpos
    nb = N_BUCKETS // 2
    max_exact = nb // 2
    n = jnp.abs(rel)
    nf = jnp.maximum(n, 1).astype(F32)
    large = max_exact + (jnp.log(nf / max_exact) / math.log(MAX_DIST / max_exact)
                         * (nb - max_exact)).astype(I32)
    bucket = (rel > 0).astype(I32) * nb + jnp.where(n < max_exact, n, jnp.minimum(large, nb - 1))

    rows = B_REP * t_new
    sel_g = jnp.concatenate([sel] * B_REP, axis=0) != 0.0
    bucket_g = jnp.concatenate([bucket] * B_REP, axis=0)
    for g in range(B_KV):
        grp = lambda ref, n: ref[0, pl.ds(g, n, stride=B_KV), :]
        qg = q_ref[0, g * rows:(g + 1) * rows, :]
        rb = rb_ref[g * rows:(g + 1) * rows, :]
        bias = jnp.zeros((rows, lp), F32)
        for j in range(N_BUCKETS):
            bias = jnp.where(bucket_g == j, rb[:, j:j + 1], bias)
        zp = _dot_nt(qg, grp(kp_ref, past)) * (B_HD ** -0.5)
        zn = _dot_nt(qg, grp(kn_ref, t_new)) * (B_HD ** -0.5)
        ap = jnp.where(sel_g[:, :past], zp + bias[:, :past], NEG_INF)
        an = jnp.where(sel_g[:, past:n_keys], zn + bias[:, past:n_keys], NEG_INF)
        m = jnp.maximum(jnp.max(ap, axis=1, keepdims=True), jnp.max(an, axis=1, keepdims=True))
        pp, pn = jnp.exp(ap - m), jnp.exp(an - m)
        l = jnp.sum(pp, axis=1, keepdims=True) + jnp.sum(pn, axis=1, keepdims=True)
        o_ref[0, g * rows:(g + 1) * rows, :] = (_dot(pp, grp(vp_ref, past)) + _dot(pn, grp(vn_ref, t_new))) / l


def _dsa_attention_cached(q_rows, qi_rows, wi_col, rb_rows, kidx_all, k_past, v_past, k_new, v_new):
    b, rows, hd = q_rows.shape
    past, t_new = k_past.shape[1], k_new.shape[1]
    lp = kidx_all.shape[1]
    top_k = min(TOPK_MAX, (past + t_new) // 4)
    assert past % LANE == 0
    flat = lambda a: a.reshape(b, a.shape[1] * B_KV, B_HD)
    kv_spec = lambda n: pl.BlockSpec((1, n * B_KV, B_HD), lambda i: (i, 0, 0))
    return pl.pallas_call(
        functools.partial(_dsa_cached_kernel, past=past, t_new=t_new, top_k=top_k),
        grid=(b,),
        in_specs=[pl.BlockSpec((1, rows, hd), lambda i: (i, 0, 0)),
                  pl.BlockSpec((1,) + qi_rows.shape[1:], lambda i: (i, 0, 0)),
                  pl.BlockSpec((1,) + wi_col.shape[1:], lambda i: (i, 0, 0)),
                  pl.BlockSpec(rb_rows.shape, lambda i: (0, 0)),
                  pl.BlockSpec((1, lp, IDX_DIM), lambda i: (i, 0, 0)),
                  kv_spec(past), kv_spec(past), kv_spec(t_new), kv_spec(t_new)],
        out_specs=pl.BlockSpec((1, rows, hd), lambda i: (i, 0, 0)),
        out_shape=jax.ShapeDtypeStruct((b, rows, hd), F32),
        compiler_params=_cparams("parallel"),
        name="dsa_attention_cached",
    )(q_rows, qi_rows, wi_col, rb_rows, kidx_all, flat(k_past), flat(v_past), flat(k_new), flat(v_new))


def _cumsum_kernel(x_ref, o_ref):
    x = x_ref[0]
    n = x.shape[-1]
    lane = lax.broadcasted_iota(I32, x.shape, 1)
    s = 1
    while s < n:
        x = x + jnp.where(lane >= s, pltpu.roll(x, s, 1), 0.0)
        s *= 2
    o_ref[0] = x


def _cumsum_lanes(x):
    b, h, n = x.shape
    return pl.pallas_call(
        _cumsum_kernel,
        grid=(b,),
        in_specs=[pl.BlockSpec((1, h, n), lambda i: (i, 0, 0))],
        out_specs=pl.BlockSpec((1, h, n), lambda i: (i, 0, 0)),
        out_shape=jax.ShapeDtypeStruct((b, h, n), F32),
        compiler_params=_cparams("parallel"),
        name="logf_cumsum",
    )(x)


def _fox_init(m_ref, l_ref, acc_ref):
    m_ref[...] = jnp.full(m_ref.shape, NEG_INF, F32)
    l_ref[...] = jnp.zeros(l_ref.shape, F32)
    acc_ref[...] = jnp.zeros(acc_ref.shape, F32)


def _fox_tile(z_fn, pv_fn, cq, ck, mask, m_ref, l_ref, acc_ref, a_ref):
    c1 = (D_HD ** -0.5) * LOG2E
    cq2, ck2 = cq * LOG2E, ck * LOG2E
    cols = []
    for h in range(D_HEADS):
        a = z_fn(h) * c1 - ck2[:, h:h + 1]
        if mask is not None:
            a = jnp.where(mask, a, NEG_INF)
        a_ref[h] = a
        cols.append(jnp.max(a, axis=0, keepdims=True))
    m_old = m_ref[...]
    m_new = jnp.maximum(m_old, jnp.concatenate(cols, axis=0) + cq2)
    alpha = jnp.exp2(m_old - m_new)
    shift = m_new - cq2
    m_ref[...] = m_new
    sums = []
    for h in range(D_HEADS):
        p = jnp.exp2(a_ref[h] - shift[h:h + 1, :])
        sums.append(jnp.sum(p, axis=0, keepdims=True))
        hs = slice(h * D_HD, (h + 1) * D_HD)
        acc_ref[hs, :] = alpha[h:h + 1, :] * acc_ref[hs, :] + pv_fn(h, p.astype(BF16))
    l_ref[...] = alpha * l_ref[...] + jnp.concatenate(sums, axis=0)


def _hs(h):
    return slice(h * D_HD, (h + 1) * D_HD)


def _fox_prompt_kernel(qt_ref, k_ref, vt_ref, cq_ref, ck_ref, x_ref, wout_ref, o_ref,
                       m_ref, l_ref, acc_ref, a_ref):
    qi, ki = pl.program_id(1), pl.program_id(2)
    tq, tk = qt_ref.shape[2], k_ref.shape[1]
    q0, k0 = qi * tq, ki * tk

    @pl.when(ki == 0)
    def _():
        _fox_init(m_ref, l_ref, acc_ref)

    def run(masked):
        mask = None
        if masked:
            mask = (k0 + lax.broadcasted_iota(I32, (tk, tq), 0)) <= (q0 + lax.broadcasted_iota(I32, (tk, tq), 1))
        _fox_tile(lambda h: jnp.dot(k_ref[0, :, _hs(h)], qt_ref[0, _hs(h), :], preferred_element_type=F32),
                  lambda h, p: jnp.dot(vt_ref[0, _hs(h), :], p, preferred_element_type=F32),
                  cq_ref[0], ck_ref[0], mask, m_ref, l_ref, acc_ref, a_ref)

    fully_visible = k0 + tk - 1 <= q0
    pl.when(fully_visible)(lambda: run(False))
    pl.when(jnp.logical_and(jnp.logical_not(fully_visible), k0 <= q0 + tq - 1))(lambda: run(True))

    @pl.when(ki == pl.num_programs(2) - 1)
    def _():
        _finish_heads(o_ref, x_ref, wout_ref, l_ref, acc_ref, D_HEADS)


def _fox_attention_prompt(qt, k, vt, cum_t, cum, x, w_out, tq, tk):
    b, d, t = qt.shape
    nq, nk = t // tq, t // tk
    last = lambda j: (j * tq + tq - 1) // tk
    return pl.pallas_call(
        _fox_prompt_kernel,
        grid=(b, nq, nk),
        in_specs=[pl.BlockSpec((1, d, tq), lambda i, j, kk: (i, 0, j)),
                  pl.BlockSpec((1, tk, d), lambda i, j, kk: (i, jnp.minimum(kk, last(j)), 0)),
                  pl.BlockSpec((1, d, tk), lambda i, j, kk: (i, 0, jnp.minimum(kk, last(j)))),
                  pl.BlockSpec((1, D_HEADS, tq), lambda i, j, kk: (i, 0, j)),
                  pl.BlockSpec((1, tk, D_HEADS), lambda i, j, kk: (i, jnp.minimum(kk, last(j)), 0)),
                  pl.BlockSpec((1, tq, d), lambda i, j, kk: (i, j, 0)),
                  pl.BlockSpec((d, d), lambda i, j, kk: (0, 0))],
        out_specs=pl.BlockSpec((1, tq, d), lambda i, j, kk: (i, j, 0)),
        out_shape=jax.ShapeDtypeStruct((b, t, d), F32),
        scratch_shapes=[pltpu.VMEM((D_HEADS, tq), F32), pltpu.VMEM((D_HEADS, tq), F32),
                        pltpu.VMEM((d, tq), F32), pltpu.VMEM((D_HEADS, tk, tq), F32)],
        compiler_params=_cparams("parallel", "parallel", "arbitrary"),
        name="fox_attention",
    )(qt, k, vt, cum_t, cum, x, w_out)


def _fox_cached_kernel(q_ref, kp_ref, vp_ref, kn_ref, vn_ref, cq_ref, ckp_ref, ckn_ref, o_ref,
                       m_ref, l_ref, acc_ref, *, t_new):
    ki = pl.program_id(1)
    n_past = pl.num_programs(1) - 1
    rows = q_ref.shape[1]
    c1 = (D_HD ** -0.5) * LOG2E

    @pl.when(ki == 0)
    def _():
        m_ref[...] = jnp.full(m_ref.shape, NEG_INF, F32)
        l_ref[...] = jnp.zeros(l_ref.shape, F32)
        acc_ref[...] = jnp.zeros(acc_ref.shape, F32)

    def tile(k2d, v2d, ck_row, causal):
        cols = k2d.shape[0]
        a = _dot_nt(q_ref[0], k2d) * c1 - ck_row * LOG2E
        r = lax.broadcasted_iota(I32, (rows, cols), 0)
        c = lax.broadcasted_iota(I32, (rows, cols), 1)
        ok = (c % D_HEADS) == (r // t_new)
        if causal:
            ok = jnp.logical_and(ok, (c // D_HEADS) <= (r % t_new))
        a = jnp.where(ok, a, NEG_INF)
        cq2 = cq_ref[0] * LOG2E
        m_old = m_ref[...]
        m_new = jnp.maximum(m_old, jnp.max(a, axis=1, keepdims=True) + cq2)
        alpha = jnp.exp2(m_old - m_new)
        p = jnp.exp2(a - (m_new - cq2))
        l_ref[...] = alpha * l_ref[...] + jnp.sum(p, axis=1, keepdims=True)
        acc_ref[...] = alpha * acc_ref[...] + _dot(p, v2d)
        m_ref[...] = m_new

    @pl.when(ki < n_past)
    def _():
        tk = kp_ref.shape[1]
        tile(kp_ref[0].reshape(tk * D_HEADS, D_HD), vp_ref[0].reshape(tk * D_HEADS, D_HD), ckp_ref[0], False)

    @pl.when(ki == n_past)
    def _():
        tile(kn_ref[0].reshape(t_new * D_HEADS, D_HD), vn_ref[0].reshape(t_new * D_HEADS, D_HD), ckn_ref[0], True)
        o_ref[0] = acc_ref[...] / l_ref[...]


def _fox_attention_cached(q_rows, k_past, v_past, k_new, v_new, cq_col, ck_past, ck_new, tk):
    b, rows, hd = q_rows.shape
    past, t_new = k_past.shape[1], k_new.shape[1]
    n_past = past // tk
    pidx = lambda i, kk: (i, jnp.minimum(kk, n_past - 1), 0, 0)
    return pl.pallas_call(
        functools.partial(_fox_cached_kernel, t_new=t_new),
        grid=(b, n_past + 1),
        in_specs=[pl.BlockSpec((1, rows, hd), lambda i, kk: (i, 0, 0)),
                  pl.BlockSpec((1, tk, D_HEADS, D_HD), pidx),
                  pl.BlockSpec((1, tk, D_HEADS, D_HD), pidx),
                  pl.BlockSpec((1, t_new, D_HEADS, D_HD), lambda i, kk: (i, 0, 0, 0)),
                  pl.BlockSpec((1, t_new, D_HEADS, D_HD), lambda i, kk: (i, 0, 0, 0)),
                  pl.BlockSpec((1, rows, 1), lambda i, kk: (i, 0, 0)),
                  pl.BlockSpec((1, 1, tk * D_HEADS), lambda i, kk: (i, 0, jnp.minimum(kk, n_past - 1))),
                  pl.BlockSpec((1, 1, t_new * D_HEADS), lambda i, kk: (i, 0, 0))],
        out_specs=pl.BlockSpec((1, rows, hd), lambda i, kk: (i, 0, 0)),
        out_shape=jax.ShapeDtypeStruct((b, rows, hd), F32),
        scratch_shapes=[pltpu.VMEM((rows, 1), F32), pltpu.VMEM((rows, 1), F32), pltpu.VMEM((rows, hd), F32)],
        compiler_params=_cparams("parallel", "arbitrary"),
        name="fox_attention_cached",
    )(q_rows, k_past, v_past, k_new, v_new, cq_col, ck_past, ck_new)


def _pad_rows(a, rows):
    if a.shape[1] == rows:
        return a
    return jnp.pad(a, ((0, 0), (0, rows - a.shape[1])) + ((0, 0),) * (a.ndim - 2))


def _round_up(n, m):
    return -(-n // m) * m


def _dsa_mixer(x, g, k_past, v_past, ki_past, w, w_out, rel_bias):
    b, t, d = x.shape
    past = k_past.shape[1]
    n_keys = past + t
    if past == 0:
        qt, k4, kb, v4, vt, qit, kw, kwt = _proj(
            x, g, w, ((0, "t", BF16, None), (1, B_HD, F32, None), (1, "rows", BF16, None),
                      (2, B_HD, F32, None), (2, "t", BF16, None), (3, "t", BF16, None),
                      (4, "rows", F32, None), (4, "t", F32, None)),
            jnp.zeros((1, LANE), F32))
        far, near = _bias_tiles(rel_bias, 2 * LANE, 2 * LANE)
        y = _dsa_attention_prompt(qt, qit, kwt, kw, kb, vt, far, near, x, w_out)
        return (y, k4, v4, kw[:, :, :IDX_DIM])
    q, k4, v4, qidx, kw = _proj(
        x, g, w, ((0, "rows", BF16, None), (1, B_HD, F32, None), (2, B_HD, F32, None),
                  (3, "rows", BF16, None), (4, "rows", F32, None)),
        jnp.zeros((1, LANE), F32))
    ki = kw[:, :, :IDX_DIM]
    to_rows = lambda a, nh: jnp.swapaxes(a.reshape(b, t, nh, -1), 1, 2).reshape(b, nh * t, -1)
    kidx_all = _pad_rows(jnp.concatenate([ki_past, ki], axis=1), _round_up(n_keys, LANE))
    o = _dsa_attention_cached(to_rows(q, B_HEADS), to_rows(qidx, IDX_HEADS),
                              to_rows(kw[:, :, IDX_DIM:IDX_DIM + IDX_HEADS], IDX_HEADS),
                              jnp.repeat(rel_bias.T, t, axis=0), kidx_all, k_past, v_past, k4, v4)
    o = jnp.swapaxes(o.reshape(b, B_HEADS, t, B_HD), 1, 2).reshape(b * t, d)
    y = _out_proj(x.reshape(b * t, d), o, w_out).reshape(b, t, d)
    return (y, k4, v4, ki)


def _fox_mixer(x, g, k_past, v_past, lf_past, w, b_f, w_out):
    b, t, d = x.shape
    past = k_past.shape[1]
    heads4 = ((1, D_HD, F32, None), (2, D_HD, F32, None), (3, "rows", F32, "log_sigmoid"))
    if past == 0:
        tq = tk = min(4 * LANE, t)
        k4, v4, lf, qt, kb, vt = _proj(
            x, g, w, heads4 + ((0, "t", BF16, None), (1, "rows", BF16, None), (2, "t", BF16, None)), b_f)
        logf = lf[:, :, :D_HEADS]
        cum_t = _cumsum_lanes(jnp.swapaxes(logf, 1, 2))
        y = _fox_attention_prompt(qt, kb, vt, cum_t, jnp.swapaxes(cum_t, 1, 2), x, w_out, tq, tk)
        return (y, k4, v4, logf)
    else:
        tk = 2 * LANE
        k4, v4, lf, q = _proj(x, g, w, heads4 + ((0, "rows", BF16, None),), b_f)
        logf = lf[:, :, :D_HEADS]
        lf_all = _pad_rows(jnp.concatenate([lf_past, logf], axis=1), _round_up(past + t, LANE))
        cum = jnp.swapaxes(_cumsum_lanes(jnp.swapaxes(lf_all, 1, 2)), 1, 2)[:, :past + t]
        ck = cum.reshape(b, 1, (past + t) * D_HEADS)
        to_rows = lambda a: jnp.swapaxes(a.reshape(b, t, D_HEADS, -1), 1, 2).reshape(b, D_HEADS * t, -1)
        o = _fox_attention_cached(to_rows(q), k_past, v_past, k4, v4, to_rows(cum[:, past:]),
                                  ck[:, :, :past * D_HEADS], ck[:, :, past * D_HEADS:], tk)
        o = jnp.swapaxes(o.reshape(b, D_HEADS, t, D_HD), 1, 2).reshape(b, t, d)
    y = _out_proj(x.reshape(b * t, d), o.reshape(b * t, d), w_out).reshape(b, t, d)
    return (y, k4, v4, logf)


def _run_group(x, pos0, a_st, b_k, b_v, b_ki, c_st, d_k, d_v, d_lf, mem_k, mem_v, prm):
    b, t, d = x.shape
    depth = prm["norm_mix"].shape[0]
    new = {n: [] for n in ("a", "bk", "bv", "bki", "c", "dk", "dv", "dlf")}
    for i in range(depth):
        kind, j = i % 4, i // 4
        g = prm["norm_mix"][i]
        if kind == 0:
            x, st = _conv_mixer(x, g, prm["a_w_in"][j], prm["a_conv"][j], a_st[j], prm["a_w_out"][j])
            new["a"].append(st)
        elif kind == 1:
            x, kk, vv, ki = _dsa_mixer(x, g, b_k[j], b_v[j], b_ki[j], prm["b_w"][j], prm["b_w_out"][j],
                                       prm["rel_bias"])
            new["bk"].append(kk); new["bv"].append(vv); new["bki"].append(ki)
        elif kind == 2:
            x, st = _pool_mixer(x, g, c_st[j], prm["c_w_group"][j], prm["c_scale"][j], pos0)
            new["c"].append(st)
        else:
            x, kk, vv, lf = _fox_mixer(x, g, d_k[j], d_v[j], d_lf[j], prm["d_w"][j], prm["d_b_f"][j],
                                       prm["d_w_out"][j])
            new["dk"].append(kk); new["dv"].append(vv); new["dlf"].append(lf)
        x = _xattn(x, prm["norm_xattn"], prm["xa_wq"], mem_k, mem_v, prm["xa_wo"], i)
        last = i == depth - 1
        x = _ffn(x.reshape(b * t, d), prm["norm_ffn"], prm["ffn_w1"], prm["ffn_w2"],
                 prm["final_norm"], i, last).reshape(b, t, d)
    return (x,) + tuple(jnp.stack(new[n]) for n in ("a", "bk", "bv", "bki", "c", "dk", "dv", "dlf"))


def kernel(x_prompt, x_sample, state_a_conv, cache_b_k, cache_b_v, cache_b_kidx, state_c_pool,
           cache_d_k, cache_d_v, cache_d_logf, cache_mem_k, cache_mem_v, mem_prompt,
           norm_mix, norm_xattn, norm_mem, norm_ffn, final_norm,
           a_w_in, a_conv, a_w_out, b_w_in, b_w_out, rel_bias, c_w_group, c_scale,
           d_w_in, d_b_f, d_w_out, xa_wq, xa_wkv, xa_wo, ffn_w1, ffn_w2):
    bp, t, d = x_prompt.shape
    depth = norm_mix.shape[0]
    n_b, n_d = b_w_in.shape[0], d_w_in.shape[0]
    bf = lambda w: w.astype(BF16)

    def split_cols(w, widths):
        out, c = [], 0
        for wd in widths:
            piece = w[:, c:c + wd]
            c += wd
            if wd % LANE:
                piece = jnp.pad(piece, ((0, 0), (0, _round_up(wd, LANE) - wd)))
            out.append(bf(piece))
        assert c == w.shape[1]
        return out

    b_q, b_kvw = B_HEADS * B_HD, B_KV * B_HD
    b_w = [split_cols(b_w_in[j], (b_q, b_kvw, b_kvw, IDX_HEADS * IDX_DIM, IDX_DIM + IDX_HEADS))
           for j in range(n_b)]
    d_w = [split_cols(d_w_in[j], (d, d, d, D_HEADS)) for j in range(n_d)]
    d_bf = [jnp.pad(d_b_f[j], (0, LANE - D_HEADS)).reshape(1, LANE) for j in range(n_d)]

    prm = {"norm_mix": norm_mix, "norm_xattn": norm_xattn, "norm_ffn": norm_ffn, "final_norm": final_norm,
           "a_w_in": bf(a_w_in), "a_conv": a_conv, "a_w_out": bf(a_w_out),
           "b_w": b_w, "b_w_out": bf(b_w_out), "rel_bias": rel_bias,
           "c_w_group": bf(c_w_group), "c_scale": c_scale,
           "d_w": d_w, "d_b_f": d_bf, "d_w_out": bf(d_w_out),
           "xa_wq": bf(xa_wq), "xa_wo": bf(xa_wo), "ffn_w1": bf(ffn_w1), "ffn_w2": bf(ffn_w2)}

    n_mem = mem_prompt.shape[1]
    mk, mv, mk_rows, mv_rows = _memory_kv(mem_prompt, norm_mem, bf(xa_wkv))

    n_a, n_c = a_w_in.shape[0], c_w_group.shape[0]
    z = lambda *s: jnp.zeros(s, F32)
    gp = _run_group(x_prompt, 0,
                    z(n_a, bp, CONV_W - 1, d),
                    z(n_b, bp, 0, B_KV, B_HD), z(n_b, bp, 0, B_KV, B_HD), z(n_b, bp, 0, IDX_DIM),
                    z(n_c, bp, POOL_STATE, d),
                    z(n_d, bp, 0, D_HEADS, D_HD), z(n_d, bp, 0, D_HEADS, D_HD), z(n_d, bp, 0, D_HEADS),
                    mk_rows, mv_rows, prm)

    bs = x_sample.shape[0]
    past_len = cache_b_k.shape[2]
    gs = _run_group(x_sample, past_len, state_a_conv, cache_b_k, cache_b_v, cache_b_kidx, state_c_pool,
                    cache_d_k, cache_d_v, cache_d_logf, cache_mem_k, cache_mem_v, prm)

    (y_p, a_p, bk_p, bv_p, bki_p, c_p, dk_p, dv_p, dlf_p) = gp
    (y_s, a_s, bk_s, bv_s, bki_s, c_s, dk_s, dv_s, dlf_s) = gs
    return (y_p, y_s, a_p, a_s, bk_p, bv_p, bki_p, bk_s, bv_s, bki_s, c_p, c_s,
            dk_p, dv_p, dlf_p, dk_s, dv_s, dlf_s, mk, mv)
```

```python
import functools
import math

import jax
import jax.numpy as jnp
from jax import lax
from jax.experimental import pallas as pl
from jax.experimental.pallas import tpu as pltpu

F32 = jnp.float32
BF16 = jnp.bfloat16
I32 = jnp.int32

EPS = 1e-6
NEG_INF = -1e30
LOG2E = math.log2(math.e)
CHUNK = 64
LANE = 128
VMEM_LIMIT = 48 * 1024 * 1024

CONV_W = 3
POOL_WINDOWS = (2, 4, 8, 16)
POOL_STATE = max(POOL_WINDOWS) - 1
B_HEADS, B_KV, B_HD = 8, 2, 128
B_REP = B_HEADS // B_KV
IDX_HEADS, IDX_DIM = 8, 64
TOPK_MAX = 256
N_BUCKETS, MAX_DIST = 32, 128
D_HEADS, D_HD = 8, 128
MEM_HEADS = 4
INT_MIN = -2147483648
BIAS_CENTER = 2 * LANE


def _cparams(*sem):
    return pltpu.CompilerParams(dimension_semantics=sem, vmem_limit_bytes=VMEM_LIMIT)


def _dot(a, b):
    return jnp.dot(a.astype(BF16), b.astype(BF16), preferred_element_type=F32)


def _dot_nt(a, b):
    return lax.dot_general(a.astype(BF16), b.astype(BF16), (((1,), (1,)), ((), ())),
                           preferred_element_type=F32)


def _dot_tn(a, b):
    return lax.dot_general(a.astype(BF16), b.astype(BF16), (((0,), (0,)), ((), ())),
                           preferred_element_type=F32)


def _rms(x, g):
    return x * lax.rsqrt(jnp.mean(x * x, axis=-1, keepdims=True) + EPS) * g


def _finish_heads(o_ref, x_ref, w_ref, l_ref, acc_ref, n_heads):
    hd = acc_ref.shape[0] // n_heads
    inv_l = 1.0 / l_ref[...]
    heads_t = jnp.concatenate([acc_ref[h * hd:(h + 1) * hd, :] * inv_l[h:h + 1, :] for h in range(n_heads)],
                              axis=0)
    o_ref[0] = x_ref[0] + _dot_tn(heads_t, w_ref[...])


def _row_tile(n, cap):
    t = min(n, cap)
    assert n % t == 0
    return t


def _memkv_kernel(mem_ref, g_ref, w_ref, k_ref, v_ref, kb_ref, vb_ref):
    m = mem_ref[0]
    mn = m * lax.rsqrt(jnp.mean(m * m, axis=-1, keepdims=True) + EPS)
    h = (mn * g_ref[0]).astype(BF16)
    d = m.shape[-1]
    hd = d // MEM_HEADS
    k = jnp.dot(h, w_ref[0, :, :d], preferred_element_type=F32)
    v = jnp.dot(h, w_ref[0, :, d:], preferred_element_type=F32)
    kb_ref[0, 0] = k.astype(BF16)
    vb_ref[0, 0] = v.astype(BF16)
    k_ref[0, 0] = pltpu.einshape("m(hd)->mhd", k, d=hd)
    v_ref[0, 0] = pltpu.einshape("m(hd)->mhd", v, d=hd)


def _memory_kv(mem, g_mem, w_kv):
    depth, d = g_mem.shape
    b, nm, _ = mem.shape
    hd = d // MEM_HEADS
    out = jax.ShapeDtypeStruct((depth, b, nm, MEM_HEADS, hd), F32)
    out_b = jax.ShapeDtypeStruct((depth, b, nm, d), BF16)
    heads_spec = pl.BlockSpec((1, 1, nm, MEM_HEADS, hd), lambda l, i: (l, i, 0, 0, 0))
    rows_spec = pl.BlockSpec((1, 1, nm, d), lambda l, i: (l, i, 0, 0))
    return pl.pallas_call(
        _memkv_kernel,
        grid=(depth, b),
        in_specs=[pl.BlockSpec((1, nm, d), lambda l, i: (i, 0, 0)),
                  pl.BlockSpec((1, 1, d), lambda l, i: (l, 0, 0)),
                  pl.BlockSpec((1, d, 2 * d), lambda l, i: (l, 0, 0))],
        out_specs=[heads_spec, heads_spec, rows_spec, rows_spec],
        out_shape=[out, out, out_b, out_b],
        compiler_params=_cparams("parallel", "parallel"),
        name="memory_kv",
    )(mem, g_mem.reshape(depth, 1, d), w_kv)


def _ffn_kernel(x_ref, g_ref, w1_ref, w2_ref, gf_ref, o_ref, h_ref, acc_ref, *, final_norm):
    j = pl.program_id(1)

    @pl.when(j == 0)
    def _():
        h_ref[...] = _rms(x_ref[...], g_ref[...]).astype(BF16)
        acc_ref[...] = jnp.zeros_like(acc_ref)

    u = jnp.maximum(jnp.dot(h_ref[...], w1_ref[...], preferred_element_type=F32), 0.0)
    acc_ref[...] += jnp.dot((u * u).astype(BF16), w2_ref[...], preferred_element_type=F32)

    @pl.when(j == pl.num_programs(1) - 1)
    def _():
        y = x_ref[...] + acc_ref[...]
        o_ref[...] = _rms(y, gf_ref[...]) if final_norm else y


def _ffn(x, g, w1, w2, gf, layer, final_norm):
    n, d = x.shape
    f = w1.shape[2]
    tm = _row_tile(n, 1024)
    tf = 1024
    return pl.pallas_call(
        functools.partial(_ffn_kernel, final_norm=final_norm),
        grid=(n // tm, f // tf),
        in_specs=[pl.BlockSpec((tm, d), lambda i, j: (i, 0)),
                  pl.BlockSpec((None, 1, d), lambda i, j: (layer, 0, 0)),
                  pl.BlockSpec((None, d, tf), lambda i, j: (layer, 0, j)),
                  pl.BlockSpec((None, tf, d), lambda i, j: (layer, j, 0)),
                  pl.BlockSpec((1, d), lambda i, j: (0, 0))],
        out_specs=pl.BlockSpec((tm, d), lambda i, j: (i, 0)),
        out_shape=jax.ShapeDtypeStruct((n, d), F32),
        scratch_shapes=[pltpu.VMEM((tm, d), BF16), pltpu.VMEM((tm, d), F32)],
        compiler_params=_cparams("parallel", "arbitrary"),
        name="ffn",
    )(x, g.reshape(-1, 1, d), w1, w2, gf.reshape(1, d))


def _xattn_kernel(x_ref, g_ref, wq_ref, mk_ref, mv_ref, wo_ref, o_ref):
    x = x_ref[0]
    d = x.shape[-1]
    hd = d // MEM_HEADS
    h = _rms(x, g_ref[...]).astype(BF16)
    q = jnp.dot(h, wq_ref[...], preferred_element_type=F32)
    outs = []
    if len(mk_ref.shape) == 3:
        mk = pltpu.einshape("mhd->m(hd)", mk_ref[...]).astype(BF16)
        mv = pltpu.einshape("mhd->m(hd)", mv_ref[...]).astype(BF16)
    else:
        mk, mv = mk_ref[...], mv_ref[...]
    for hh in range(MEM_HEADS):
        sl = slice(hh * hd, (hh + 1) * hd)
        kh, vh = mk[:, sl], mv[:, sl]
        s = _dot_nt(q[:, sl], kh) * (hd ** -0.5)
        m = jnp.max(s, axis=-1, keepdims=True)
        p = jnp.exp(s - m)
        l = jnp.sum(p, axis=-1, keepdims=True)
        outs.append(_dot(p, vh) / l)
    o = jnp.concatenate(outs, axis=-1)
    o_ref[0] = x + _dot(o, wo_ref[...])


def _xattn(x, g, wq, mk, mv, wo, layer):
    b, t, d = x.shape
    tm = _row_tile(t, 512)
    kv_spec = pl.BlockSpec((None, None) + mk.shape[2:], lambda i, j: (layer, i) + (0,) * (mk.ndim - 2))
    return pl.pallas_call(
        _xattn_kernel,
        grid=(b, t // tm),
        in_specs=[pl.BlockSpec((1, tm, d), lambda i, j: (i, j, 0)),
                  pl.BlockSpec((None, 1, d), lambda i, j: (layer, 0, 0)),
                  pl.BlockSpec((None, d, d), lambda i, j: (layer, 0, 0)),
                  kv_spec, kv_spec,
                  pl.BlockSpec((None, d, d), lambda i, j: (layer, 0, 0))],
        out_specs=pl.BlockSpec((1, tm, d), lambda i, j: (i, j, 0)),
        out_shape=jax.ShapeDtypeStruct((b, t, d), F32),
        compiler_params=_cparams("parallel", "parallel"),
        name="xattn",
    )(x, g.reshape(-1, 1, d), wq, mk, mv, wo)


def _conv_kernel(x_ref, g_ref, win_ref, wc_ref, st_ref, wout_ref, o_ref, nst_ref, z_ref):
    t = pl.program_id(1)
    x = x_ref[0]
    tm, d = x.shape
    pad = 8

    @pl.when(t == 0)
    def _():
        z_ref[pad - 2:pad, :] = st_ref[0]

    h = _rms(x, g_ref[...]).astype(BF16)
    bg = jnp.dot(h, win_ref[:, 0:d], preferred_element_type=F32)
    cg = jnp.dot(h, win_ref[:, d:2 * d], preferred_element_type=F32)
    u = jnp.dot(h, win_ref[:, 2 * d:3 * d], preferred_element_type=F32)
    z = cg * u
    z_ref[pad:pad + tm, :] = z
    conv = (z_ref[pad - 2:pad - 2 + tm, :] * wc_ref[0:1, :]
            + z_ref[pad - 1:pad - 1 + tm, :] * wc_ref[1:2, :]
            + z * wc_ref[2:3, :])
    o_ref[0] = x + _dot(bg * conv, wout_ref[...])
    last = z_ref[pad + tm - 2:pad + tm, :]
    z_ref[pad - 2:pad, :] = last

    @pl.when(t == pl.num_programs(1) - 1)
    def _():
        nst_ref[0] = last


def _conv_mixer(x, g, w_in, w_conv, state, w_out):
    b, t, d = x.shape
    tm = _row_tile(t, 512)
    return pl.pallas_call(
        _conv_kernel,
        grid=(b, t // tm),
        in_specs=[pl.BlockSpec((1, tm, d), lambda i, j: (i, j, 0)),
                  pl.BlockSpec((1, d), lambda i, j: (0, 0)),
                  pl.BlockSpec((d, 3 * d), lambda i, j: (0, 0)),
                  pl.BlockSpec((CONV_W, d), lambda i, j: (0, 0)),
                  pl.BlockSpec((1, CONV_W - 1, d), lambda i, j: (i, 0, 0)),
                  pl.BlockSpec((d, d), lambda i, j: (0, 0))],
        out_specs=[pl.BlockSpec((1, tm, d), lambda i, j: (i, j, 0)),
                   pl.BlockSpec((1, CONV_W - 1, d), lambda i, j: (i, 0, 0))],
        out_shape=[jax.ShapeDtypeStruct((b, t, d), F32),
                   jax.ShapeDtypeStruct((b, CONV_W - 1, d), F32)],
        scratch_shapes=[pltpu.VMEM((tm + 8, d), F32)],
        compiler_params=_cparams("parallel", "arbitrary"),
        name="conv_mixer",
    )(x, g.reshape(1, d), w_in, w_conv, state, w_out)


def _pool_kernel(x_ref, g_ref, st_ref, wg_ref, sc_ref, o_ref, nst_ref, h_ref, *, pos0):
    t = pl.program_id(1)
    x = x_ref[0]
    tm, d = x.shape
    gw = d // len(POOL_WINDOWS)
    base = POOL_STATE + 1

    @pl.when(t == 0)
    def _():
        h_ref[1:base, :] = st_ref[0]

    h = _rms(x, g_ref[...])
    h_ref[base:base + tm, :] = h
    pos = pos0 + t * tm + lax.broadcasted_iota(I32, (tm, gw), 0)
    ys = []
    for gi, w in enumerate(POOL_WINDOWS):
        sl = slice(gi * gw, (gi + 1) * gw)
        win = h[:, sl]
        for j in range(1, w):
            win = win + h_ref[base - j:base - j + tm, sl]
        count = jnp.minimum(w, pos + 1).astype(F32)
        dlt = win / count - h[:, sl]
        ys.append(_dot(dlt, wg_ref[gi]))
    y = jnp.concatenate(ys, axis=-1) * sc_ref[...]
    o_ref[0] = x + y
    last = h_ref[tm + 1:tm + base, :]
    h_ref[1:base, :] = last

    @pl.when(t == pl.num_programs(1) - 1)
    def _():
        nst_ref[0] = last


def _pool_mixer(x, g, state, w_group, scale, pos0):
    b, t, d = x.shape
    ng, gw, _ = w_group.shape
    tm = _row_tile(t, 512)
    return pl.pallas_call(
        functools.partial(_pool_kernel, pos0=pos0),
        grid=(b, t // tm),
        in_specs=[pl.BlockSpec((1, tm, d), lambda i, j: (i, j, 0)),
                  pl.BlockSpec((1, d), lambda i, j: (0, 0)),
                  pl.BlockSpec((1, POOL_STATE, d), lambda i, j: (i, 0, 0)),
                  pl.BlockSpec((ng, gw, gw), lambda i, j: (0, 0, 0)),
                  pl.BlockSpec((1, d), lambda i, j: (0, 0))],
        out_specs=[pl.BlockSpec((1, tm, d), lambda i, j: (i, j, 0)),
                   pl.BlockSpec((1, POOL_STATE, d), lambda i, j: (i, 0, 0))],
        out_shape=[jax.ShapeDtypeStruct((b, t, d), F32),
                   jax.ShapeDtypeStruct((b, POOL_STATE, d), F32)],
        scratch_shapes=[pltpu.VMEM((tm + POOL_STATE + 1, d), F32)],
        compiler_params=_cparams("parallel", "arbitrary"),
        name="pool_mixer",
    )(x, g.reshape(1, d), state, w_group, scale.reshape(1, d))


def _proj_kernel(*refs, n_w, outs):
    x_ref, g_ref = refs[0], refs[1]
    w_refs = refs[2:2 + n_w]
    e_ref = refs[2 + n_w]
    o_refs = refs[3 + n_w:]
    h = _rms(x_ref[0], g_ref[...]).astype(BF16)
    ys = {}
    for (wi, mode, _, ep), o_ref in zip(outs, o_refs):
        if wi not in ys:
            ys[wi] = jnp.dot(h, w_refs[wi][...], preferred_element_type=F32)
        y = ys[wi]
        if ep == "log_sigmoid":
            u = -(y + e_ref[...])
            y = -(jnp.maximum(u, 0.0) + jnp.log1p(jnp.exp(-jnp.abs(u))))
        if mode == "rows":
            o_ref[0] = y.astype(o_ref.dtype)
        elif isinstance(mode, tuple):
            o_ref[0] = y[:, :mode[1]].astype(o_ref.dtype)
        elif mode == "t":
            o_ref[0] = jnp.transpose(y).astype(o_ref.dtype)
        else:
            o_ref[0] = pltpu.einshape("m(hd)->mhd", y.astype(o_ref.dtype), d=mode)


def _proj(x, g, ws, outs, extra):
    b, t, d = x.shape
    tm = _row_tile(t, 512)
    in_specs = [pl.BlockSpec((1, tm, d), lambda i, j: (i, j, 0)), pl.BlockSpec((1, d), lambda i, j: (0, 0))]
    in_specs += [pl.BlockSpec(w.shape, lambda i, j: (0, 0)) for w in ws]
    in_specs += [pl.BlockSpec(extra.shape, lambda i, j: (0, 0))]
    out_specs, out_shape = [], []
    for wi, mode, dt, _ in outs:
        n = ws[wi].shape[1]
        if mode == "rows" or isinstance(mode, tuple):
            n = n if mode == "rows" else mode[1]
            out_specs.append(pl.BlockSpec((1, tm, n), lambda i, j: (i, j, 0)))
            out_shape.append(jax.ShapeDtypeStruct((b, t, n), dt))
        elif mode == "t":
            out_specs.append(pl.BlockSpec((1, n, tm), lambda i, j: (i, 0, j)))
            out_shape.append(jax.ShapeDtypeStruct((b, n, t), dt))
        else:
            out_specs.append(pl.BlockSpec((1, tm, n // mode, mode), lambda i, j: (i, j, 0, 0)))
            out_shape.append(jax.ShapeDtypeStruct((b, t, n // mode, mode), dt))
    return pl.pallas_call(
        functools.partial(_proj_kernel, n_w=len(ws), outs=tuple(outs)),
        grid=(b, t // tm),
        in_specs=in_specs,
        out_specs=out_specs,
        out_shape=out_shape,
        compiler_params=_cparams("parallel", "parallel"),
        name="norm_proj",
    )(x, g.reshape(1, d), *ws, extra)


def _outproj_kernel(x_ref, a_ref, w_ref, o_ref):
    o_ref[...] = x_ref[...] + _dot(a_ref[...], w_ref[...])


def _out_proj(x, a, w):
    n, d = x.shape
    tm = _row_tile(n, 512)
    return pl.pallas_call(
        _outproj_kernel,
        grid=(n // tm,),
        in_specs=[pl.BlockSpec((tm, d), lambda i: (i, 0)),
                  pl.BlockSpec((tm, d), lambda i: (i, 0)),
                  pl.BlockSpec((d, d), lambda i: (0, 0))],
        out_specs=pl.BlockSpec((tm, d), lambda i: (i, 0)),
        out_shape=jax.ShapeDtypeStruct((n, d), F32),
        compiler_params=_cparams("parallel"),
        name="out_proj",
    )(x, a, w)


def _bias_table_kernel(rbt_ref, o_ref):
    width = o_ref.shape[-1]
    rel = BIAS_CENTER - lax.broadcasted_iota(I32, (1, width), 1)
    nb = N_BUCKETS // 2
    max_exact = nb // 2
    ret = (rel > 0).astype(I32) * nb
    n = jnp.abs(rel)
    nf = jnp.maximum(n, 1).astype(F32)
    large = max_exact + (jnp.log(nf / max_exact) / math.log(MAX_DIST / max_exact)
                         * (nb - max_exact)).astype(I32)
    large = jnp.minimum(large, nb - 1)
    bucket = ret + jnp.where(n < max_exact, n, large)
    acc = jnp.zeros(o_ref.shape, F32)
    for j in range(N_BUCKETS):
        acc = jnp.where(bucket == j, rbt_ref[:, j:j + 1], acc)
    o_ref[...] = acc * LOG2E


def _bias_tiles_kernel(rbt_ref, far_ref, near_ref, tab_ref):
    n_d, nh, kb, tq = near_ref.shape
    _bias_table_kernel(rbt_ref, tab_ref)
    far_ref[...] = jnp.broadcast_to(tab_ref[:, BIAS_CENTER + MAX_DIST:BIAS_CENTER + MAX_DIST + 1], far_ref.shape)
    for dd in range(n_d):
        s0 = BIAS_CENTER - (dd - 1) * kb - kb
        for h in range(nh):
            rows = jnp.broadcast_to(tab_ref[h:h + 1, s0:s0 + tq + kb], (kb, tq + kb))
            near_ref[dd, h] = pltpu.roll(rows, 0, 1, stride=1, stride_axis=0)[:, kb:]


def _bias_tiles(rel_bias, kb, tq):
    nh = rel_bias.shape[1]
    n_d = tq // kb + 1
    width = BIAS_CENTER + kb + tq + kb
    assert kb % LANE == 0 and tq % kb == 0 and kb >= MAX_DIST and BIAS_CENTER >= tq
    return pl.pallas_call(
        _bias_tiles_kernel,
        out_shape=[jax.ShapeDtypeStruct((nh, LANE), F32), jax.ShapeDtypeStruct((n_d, nh, kb, tq), F32)],
        scratch_shapes=[pltpu.VMEM((nh, width), F32)],
        name="bias_tiles",
    )(rel_bias.T)


def _sortable(x):
    x = jnp.where(x == 0.0, 0.0, x)
    bits = lax.bitcast_convert_type(x, I32)
    return jnp.where(bits < 0, bits ^ 0x7FFFFFFF, bits)


def _neg_inf_key():
    import numpy as np
    b = int(np.float32(NEG_INF).view(np.int32))
    return b ^ 0x7FFFFFFF


def _dsa_prompt_kernel(qt_ref, qit_ref, kwt_ref, kw_ref, k_ref, vt_ref, far_ref, near_ref, x_ref, wout_ref, o_ref,
                       key_ref, sel_ref, m_ref, l_ref, acc_ref, a_ref, *, top_k):
    qi = pl.program_id(1)
    tq = qt_ref.shape[2]
    n_keys = kw_ref.shape[1]
    kb_sz = LANE
    q0 = qi * tq
    nkb = jnp.minimum(n_keys, q0 + tq) // kb_sz
    negkey = _neg_inf_key()

    qlane = lax.broadcasted_iota(I32, (1, tq), 1)
    lim = ((q0 + qlane) // CHUNK + 1) * CHUNK
    krow = lax.broadcasted_iota(I32, (kb_sz, tq), 0)

    def kslice(kb):
        return pl.ds(pl.multiple_of(kb * kb_sz, kb_sz), kb_sz)

    sb = 2 * kb_sz
    srow = lax.broadcasted_iota(I32, (sb, tq), 0)

    def score_body(i, c):
        rows = pl.ds(pl.multiple_of(i * sb, sb), sb)
        kid = kw_ref[0, rows, :][:, :IDX_DIM].astype(BF16)
        sc = jnp.zeros((sb, tq), F32)
        for h in range(IDX_HEADS):
            s = jnp.dot(kid, qit_ref[0, h * IDX_DIM:(h + 1) * IDX_DIM, :], preferred_element_type=F32)
            sc = sc + jnp.maximum(s, 0.0) * kwt_ref[0, IDX_DIM + h:IDX_DIM + h + 1, :]
        sc = sc * ((IDX_DIM * IDX_HEADS) ** -0.5)
        key_ref[rows, :] = jnp.where(i * sb + srow < lim, _sortable(sc), negkey)
        return c

    lax.fori_loop(0, nkb // 2, score_body, 0)

    def count(pred_fn):
        def body(i, a):
            for u in range(2):
                kb = 2 * i + u
                ind = pred_fn(kb, key_ref[kslice(kb), :])
                a = a + jnp.sum(ind.reshape(kb_sz // 8, 8, tq), axis=0)
            return a
        a = lax.fori_loop(0, nkb // 2, body, jnp.zeros((8, tq), I32))
        return jnp.sum(a, axis=0, keepdims=True)

    def bit_body(i, t_u):
        cand_u = t_u | jnp.left_shift(jnp.int32(1), 31 - i)
        cand_s = cand_u ^ INT_MIN
        cnt = count(lambda kb, key: jnp.where(key >= cand_s, 1, 0))
        return jnp.where(cnt >= top_k, cand_u, t_u)

    t_s = lax.fori_loop(0, 32, bit_body, jnp.zeros((1, tq), I32)) ^ INT_MIN

    def adm01(kb):
        return jnp.where(kb * kb_sz + krow < lim, 1.0, 0.0)

    def sel_body(kb, a):
        sel = jnp.where(key_ref[kslice(kb), :] >= t_s, adm01(kb), 0.0)
        sel_ref[kslice(kb), :] = sel
        return a + jnp.sum(sel.reshape(kb_sz // 8, 8, tq), axis=0)

    n_sel = jnp.sum(lax.fori_loop(0, nkb, sel_body, jnp.zeros((8, tq), F32)), axis=0, keepdims=True)

    @pl.when(jnp.max(n_sel) > top_k)
    def _():
        n_gt = count(lambda kb, key: jnp.where(key > t_s, 1, 0))
        need = (top_k - n_gt).astype(F32)
        r = lax.broadcasted_iota(I32, (kb_sz, kb_sz), 0)
        c = lax.broadcasted_iota(I32, (kb_sz, kb_sz), 1)
        ltri = jnp.where(c < r, 1.0, 0.0).astype(BF16)

        def tie_body(kb, carry):
            key = key_ref[kslice(kb), :]
            adm = adm01(kb)
            eq = jnp.where(key == t_s, adm, 0.0)
            rank = carry + jnp.dot(ltri, eq.astype(BF16), preferred_element_type=F32)
            keep = jnp.where(rank < need, eq, 0.0)
            sel_ref[kslice(kb), :] = jnp.where(key > t_s, adm, keep)
            return carry + jnp.sum(eq, axis=0, keepdims=True)

        lax.fori_loop(0, nkb, tie_body, jnp.zeros((1, tq), F32))

    m_ref[...] = jnp.full(m_ref.shape, NEG_INF, F32)
    l_ref[...] = jnp.zeros(l_ref.shape, F32)
    acc_ref[...] = jnp.zeros(acc_ref.shape, F32)
    c1 = (B_HD ** -0.5) * LOG2E

    def attend(k0, nk, bias2_fn):
        rows = pl.ds(pl.multiple_of(k0, LANE), nk)
        sel = sel_ref[rows, :] != 0.0
        ks = k_ref[0, rows, :]
        cols = []
        for h in range(B_HEADS):
            g = h // B_REP
            z = jnp.dot(ks[:, g * B_HD:(g + 1) * B_HD], qt_ref[0, h * B_HD:(h + 1) * B_HD, :],
                        preferred_element_type=F32)
            a = jnp.where(sel, z * c1 + bias2_fn(h), NEG_INF)
            a_ref[h, 0:nk, :] = a
            cols.append(jnp.max(a, axis=0, keepdims=True))
        m_old = m_ref[...]
        m_new = jnp.maximum(m_old, jnp.concatenate(cols, axis=0))
        alpha = jnp.exp2(m_old - m_new)
        m_ref[...] = m_new
        sums = []
        for h in range(B_HEADS):
            g = h // B_REP
            p = jnp.exp2(a_ref[h, 0:nk, :] - m_new[h:h + 1, :])
            sums.append(jnp.sum(p, axis=0, keepdims=True))
            hs = slice(h * B_HD, (h + 1) * B_HD)
            pv = jnp.dot(vt_ref[0, g * B_HD:(g + 1) * B_HD, rows], p.astype(BF16),
                         preferred_element_type=F32)
            acc_ref[hs, :] = alpha[h:h + 1, :] * acc_ref[hs, :] + pv
        l_ref[...] = alpha * l_ref[...] + jnp.concatenate(sums, axis=0)

    ab = near_ref.shape[2]
    n_far = jnp.maximum(q0 // ab - 1, 0)
    far_bias2 = far_ref[:, 0:1]

    def far_body(i, c):
        attend(i * ab, ab, lambda h: far_bias2[h:h + 1, :])
        return c

    lax.fori_loop(0, n_far, far_body, 0)

    def near_body(i, c):
        dd = i - q0 // ab + 1
        attend(i * ab, ab, lambda h: near_ref[dd, h])
        return c

    lax.fori_loop(n_far, nkb * kb_sz // ab, near_body, 0)

    _finish_heads(o_ref, x_ref, wout_ref, l_ref, acc_ref, B_HEADS)


def _dsa_attention_prompt(qt, qit, kwt, kw, k, vt, far, near, x, w_out):
    b, d, t = qt.shape
    tq = near.shape[3]
    top_k = min(TOPK_MAX, t // 4)
    ab = near.shape[2]
    assert t % tq == 0 and tq % CHUNK == 0 and tq % ab == 0 and ab % LANE == 0 and tq % (2 * LANE) == 0
    return pl.pallas_call(
        functools.partial(_dsa_prompt_kernel, top_k=top_k),
        grid=(b, t // tq),
        in_specs=[pl.BlockSpec((1, d, tq), lambda i, j: (i, 0, j)),
                  pl.BlockSpec((1, qit.shape[1], tq), lambda i, j: (i, 0, j)),
                  pl.BlockSpec((1, LANE, tq), lambda i, j: (i, 0, j)),
                  pl.BlockSpec((1, t, LANE), lambda i, j: (i, 0, 0)),
                  pl.BlockSpec((1, t, B_KV * B_HD), lambda i, j: (i, 0, 0)),
                  pl.BlockSpec((1, B_KV * B_HD, t), lambda i, j: (i, 0, 0)),
                  pl.BlockSpec(far.shape, lambda i, j: (0, 0)),
                  pl.BlockSpec(near.shape, lambda i, j: (0, 0, 0, 0)),
                  pl.BlockSpec((1, tq, d), lambda i, j: (i, j, 0)),
                  pl.BlockSpec((d, d), lambda i, j: (0, 0))],
        out_specs=pl.BlockSpec((1, tq, d), lambda i, j: (i, j, 0)),
        out_shape=jax.ShapeDtypeStruct((b, t, d), F32),
        scratch_shapes=[pltpu.VMEM((t, tq), I32), pltpu.VMEM((t, tq), F32),
                        pltpu.VMEM((B_HEADS, tq), F32), pltpu.VMEM((B_HEADS, tq), F32),
                        pltpu.VMEM((d, tq), F32), pltpu.VMEM((B_HEADS, ab, tq), F32)],
        compiler_params=_cparams("parallel", "parallel"),
        name="dsa_attention",
    )(qt, qit, kwt, kw, k, vt, far, near, x, w_out)


def _dsa_cached_kernel(q_ref, qi_ref, wi_ref, rb_ref, kidx_ref, kp_ref, vp_ref, kn_ref, vn_ref, o_ref,
                       *, past, t_new, top_k):
    n_keys = past + t_new
    lp = kidx_ref.shape[1]
    negkey = _neg_inf_key()
    kpos = lax.broadcasted_iota(I32, (t_new, lp), 1)
    qpos = past + lax.broadcasted_iota(I32, (t_new, lp), 0)
    adm = kpos < (qpos // CHUNK + 1) * CHUNK

    s = _dot_nt(qi_ref[0], kidx_ref[0])
    w = jnp.maximum(s, 0.0) * wi_ref[0]
    sc = w[0:t_new]
    for h in range(1, IDX_HEADS):
        sc = sc + w[h * t_new:(h + 1) * t_new]
    sc = sc * ((IDX_DIM * IDX_HEADS) ** -0.5)
    key = jnp.where(adm, _sortable(sc), negkey)
    key = jnp.where(kpos < n_keys, key, INT_MIN)

    def bit_body(i, t_u):
        cand_u = t_u | jnp.left_shift(jnp.int32(1), 31 - i)
        cnt = jnp.sum(jnp.where(key >= (cand_u ^ INT_MIN), 1.0, 0.0), axis=1, keepdims=True)
        return jnp.where(cnt >= top_k, cand_u, t_u)

    t_s = lax.fori_loop(0, 32, bit_body, jnp.zeros((t_new, 1), I32)) ^ INT_MIN

    adm01 = jnp.where(adm, 1.0, 0.0)
    gt = jnp.where(key > t_s, adm01, 0.0)
    eq = jnp.where(key == t_s, adm01, 0.0)
    need = top_k - jnp.sum(jnp.where(key > t_s, 1.0, 0.0), axis=1, keepdims=True)
    r = lax.broadcasted_iota(I32, (LANE, LANE), 0)
    c = lax.broadcasted_iota(I32, (LANE, LANE), 1)
    utri = jnp.where(r < c, 1.0, 0.0).astype(BF16)
    carry = jnp.zeros((t_new, 1), F32)
    keeps = []
    for blk in range(lp // LANE):
        e = eq[:, blk * LANE:(blk + 1) * LANE]
        rank = carry + jnp.dot(e.astype(BF16), utri, preferred_element_type=F32)
        keeps.append(jnp.where(rank < need, e, 0.0))
        carry = carry + jnp.sum(e, axis=1, keepdims=True)
    sel = gt + jnp.concatenate(keeps, axis=1)

    near = max(past - MAX_DIST, 0) // LANE * LANE
    rel = (kpos - qpos)[:, near:]
    nb = N_BUCKETS // 2
    max_exact = nb // 2
    n = jnp.abs(rel)
    nf = jnp.maximum(n, 1).astype(F32)
    large = max_exact + (jnp.log(nf / max_exact) / math.log(MAX_DIST / max_exact)
                         * (nb - max_exact)).astype(I32)
    bucket = (rel > 0).astype(I32) * nb + jnp.where(n < max_exact, n, jnp.minimum(large, nb - 1))

    rows = B_REP * t_new
    sel_g = jnp.concatenate([sel] * B_REP, axis=0) != 0.0
    bucket_g = jnp.concatenate([bucket] * B_REP, axis=0)
    for g in range(B_KV):
        grp = lambda ref, n: ref[0, pl.ds(g, n, stride=B_KV), :]
        qg = q_ref[0, g * rows:(g + 1) * rows, :]
        rb = rb_ref[g * rows:(g + 1) * rows, :]
        bias_near = jnp.zeros((rows, lp - near), F32)
        for j in range(N_BUCKETS):
            bias_near = jnp.where(bucket_g == j, rb[:, j:j + 1], bias_near)
        bias = jnp.concatenate([jnp.broadcast_to(rb[:, nb - 1:nb], (rows, near)), bias_near], axis=1)
        zp = _dot_nt(qg, grp(kp_ref, past)) * (B_HD ** -0.5)
        zn = _dot_nt(qg, grp(kn_ref, t_new)) * (B_HD ** -0.5)
        ap = jnp.where(sel_g[:, :past], zp + bias[:, :past], NEG_INF)
        an = jnp.where(sel_g[:, past:n_keys], zn + bias[:, past:n_keys], NEG_INF)
        m = jnp.maximum(jnp.max(ap, axis=1, keepdims=True), jnp.max(an, axis=1, keepdims=True))
        pp, pn = jnp.exp(ap - m), jnp.exp(an - m)
        l = jnp.sum(pp, axis=1, keepdims=True) + jnp.sum(pn, axis=1, keepdims=True)
        o_ref[0, g * rows:(g + 1) * rows, :] = (_dot(pp, grp(vp_ref, past)) + _dot(pn, grp(vn_ref, t_new))) / l


def _dsa_attention_cached(q_rows, qi_rows, wi_col, rb_rows, kidx_all, k_past, v_past, k_new, v_new):
    b, rows, hd = q_rows.shape
    past, t_new = k_past.shape[1], k_new.shape[1]
    lp = kidx_all.shape[1]
    top_k = min(TOPK_MAX, (past + t_new) // 4)
    assert past % LANE == 0
    flat = lambda a: a.reshape(b, a.shape[1] * B_KV, B_HD)
    kv_spec = lambda n: pl.BlockSpec((1, n * B_KV, B_HD), lambda i: (i, 0, 0))
    return pl.pallas_call(
        functools.partial(_dsa_cached_kernel, past=past, t_new=t_new, top_k=top_k),
        grid=(b,),
        in_specs=[pl.BlockSpec((1, rows, hd), lambda i: (i, 0, 0)),
                  pl.BlockSpec((1,) + qi_rows.shape[1:], lambda i: (i, 0, 0)),
                  pl.BlockSpec((1,) + wi_col.shape[1:], lambda i: (i, 0, 0)),
                  pl.BlockSpec(rb_rows.shape, lambda i: (0, 0)),
                  pl.BlockSpec((1, lp, IDX_DIM), lambda i: (i, 0, 0)),
                  kv_spec(past), kv_spec(past), kv_spec(t_new), kv_spec(t_new)],
        out_specs=pl.BlockSpec((1, rows, hd), lambda i: (i, 0, 0)),
        out_shape=jax.ShapeDtypeStruct((b, rows, hd), F32),
        compiler_params=_cparams("parallel"),
        name="dsa_attention_cached",
    )(q_rows, qi_rows, wi_col, rb_rows, kidx_all, flat(k_past), flat(v_past), flat(k_new), flat(v_new))


def _cumsum_kernel(x_ref, o_ref):
    x = x_ref[0]
    n = x.shape[-1]
    lane = lax.broadcasted_iota(I32, x.shape, 1)
    s = 1
    while s < n:
        x = x + jnp.where(lane >= s, pltpu.roll(x, s, 1), 0.0)
        s *= 2
    o_ref[0] = x


def _cumsum_lanes(x):
    b, h, n = x.shape
    return pl.pallas_call(
        _cumsum_kernel,
        grid=(b,),
        in_specs=[pl.BlockSpec((1, h, n), lambda i: (i, 0, 0))],
        out_specs=pl.BlockSpec((1, h, n), lambda i: (i, 0, 0)),
        out_shape=jax.ShapeDtypeStruct((b, h, n), F32),
        compiler_params=_cparams("parallel"),
        name="logf_cumsum",
    )(x)


def _fox_init(m_ref, l_ref, acc_ref):
    m_ref[...] = jnp.full(m_ref.shape, NEG_INF, F32)
    l_ref[...] = jnp.zeros(l_ref.shape, F32)
    acc_ref[...] = jnp.zeros(acc_ref.shape, F32)


def _fox_tile(z_fn, pv_fn, cq, ck, mask, m_ref, l_ref, acc_ref, a_ref):
    c1 = (D_HD ** -0.5) * LOG2E
    cq2, ck2 = cq * LOG2E, ck * LOG2E
    cols = []
    for h in range(D_HEADS):
        a = z_fn(h) * c1 - ck2[:, h:h + 1]
        if mask is not None:
            a = jnp.where(mask, a, NEG_INF)
        a_ref[h] = a
        cols.append(jnp.max(a, axis=0, keepdims=True))
    m_old = m_ref[...]
    m_new = jnp.maximum(m_old, jnp.concatenate(cols, axis=0) + cq2)
    alpha = jnp.exp2(m_old - m_new)
    shift = m_new - cq2
    m_ref[...] = m_new
    sums = []
    for h in range(D_HEADS):
        p = jnp.exp2(a_ref[h] - shift[h:h + 1, :])
        sums.append(jnp.sum(p, axis=0, keepdims=True))
        hs = slice(h * D_HD, (h + 1) * D_HD)
        acc_ref[hs, :] = alpha[h:h + 1, :] * acc_ref[hs, :] + pv_fn(h, p.astype(BF16))
    l_ref[...] = alpha * l_ref[...] + jnp.concatenate(sums, axis=0)


def _hs(h):
    return slice(h * D_HD, (h + 1) * D_HD)


def _fox_prompt_kernel(qt_ref, k_ref, vt_ref, cq_ref, ck_ref, x_ref, wout_ref, o_ref,
                       m_ref, l_ref, acc_ref, a_ref):
    qi, step = pl.program_id(1), pl.program_id(2)
    tq, tk = qt_ref.shape[2], k_ref.shape[1]
    q0 = qi * tq
    ki = step - (pl.num_programs(2) - 1 - (q0 + tq - 1) // tk)
    k0 = ki * tk

    @pl.when(step == 0)
    def _():
        _fox_init(m_ref, l_ref, acc_ref)

    def run(masked):
        mask = None
        if masked:
            mask = (k0 + lax.broadcasted_iota(I32, (tk, tq), 0)) <= (q0 + lax.broadcasted_iota(I32, (tk, tq), 1))
        _fox_tile(lambda h: jnp.dot(k_ref[0, :, _hs(h)], qt_ref[0, _hs(h), :], preferred_element_type=F32),
                  lambda h, p: jnp.dot(vt_ref[0, _hs(h), :], p, preferred_element_type=F32),
                  cq_ref[0], ck_ref[0], mask, m_ref, l_ref, acc_ref, a_ref)

    fully_visible = k0 + tk - 1 <= q0
    pl.when(jnp.logical_and(ki >= 0, fully_visible))(lambda: run(False))
    pl.when(jnp.logical_and(ki >= 0, jnp.logical_not(fully_visible)))(lambda: run(True))

    @pl.when(step == pl.num_programs(2) - 1)
    def _():
        _finish_heads(o_ref, x_ref, wout_ref, l_ref, acc_ref, D_HEADS)


def _fox_attention_prompt(qt, k, vt, cum_t, cum, x, w_out, tq, tk):
    b, d, t = qt.shape
    nq, nk = t // tq, t // tk
    ktile = lambda j, s: jnp.maximum(s - (nk - 1 - (j * tq + tq - 1) // tk), 0)
    return pl.pallas_call(
        _fox_prompt_kernel,
        grid=(b, nq, nk),
        in_specs=[pl.BlockSpec((1, d, tq), lambda i, j, kk: (i, 0, j)),
                  pl.BlockSpec((1, tk, d), lambda i, j, kk: (i, ktile(j, kk), 0)),
                  pl.BlockSpec((1, d, tk), lambda i, j, kk: (i, 0, ktile(j, kk))),
                  pl.BlockSpec((1, D_HEADS, tq), lambda i, j, kk: (i, 0, j)),
                  pl.BlockSpec((1, tk, D_HEADS), lambda i, j, kk: (i, ktile(j, kk), 0)),
                  pl.BlockSpec((1, tq, d), lambda i, j, kk: (i, j, 0)),
                  pl.BlockSpec((d, d), lambda i, j, kk: (0, 0))],
        out_specs=pl.BlockSpec((1, tq, d), lambda i, j, kk: (i, j, 0)),
        out_shape=jax.ShapeDtypeStruct((b, t, d), F32),
        scratch_shapes=[pltpu.VMEM((D_HEADS, tq), F32), pltpu.VMEM((D_HEADS, tq), F32),
                        pltpu.VMEM((d, tq), F32), pltpu.VMEM((D_HEADS, tk, tq), F32)],
        compiler_params=_cparams("parallel", "parallel", "arbitrary"),
        name="fox_attention",
    )(qt, k, vt, cum_t, cum, x, w_out)


def _fox_cached_kernel(q_ref, kp_ref, vp_ref, kn_ref, vn_ref, cq_ref, ckp_ref, ckn_ref, o_ref,
                       m_ref, l_ref, acc_ref, *, t_new):
    ki = pl.program_id(1)
    n_past = pl.num_programs(1) - 1
    rows = q_ref.shape[1]
    c1 = (D_HD ** -0.5) * LOG2E

    @pl.when(ki == 0)
    def _():
        m_ref[...] = jnp.full(m_ref.shape, NEG_INF, F32)
        l_ref[...] = jnp.zeros(l_ref.shape, F32)
        acc_ref[...] = jnp.zeros(acc_ref.shape, F32)

    def tile(k2d, v2d, ck_row, causal):
        cols = k2d.shape[0]
        a = _dot_nt(q_ref[0], k2d) * c1 - ck_row * LOG2E
        r = lax.broadcasted_iota(I32, (rows, cols), 0)
        c = lax.broadcasted_iota(I32, (rows, cols), 1)
        ok = (c % D_HEADS) == (r // t_new)
        if causal:
            ok = jnp.logical_and(ok, (c // D_HEADS) <= (r % t_new))
        a = jnp.where(ok, a, NEG_INF)
        cq2 = cq_ref[0] * LOG2E
        m_old = m_ref[...]
        m_new = jnp.maximum(m_old, jnp.max(a, axis=1, keepdims=True) + cq2)
        alpha = jnp.exp2(m_old - m_new)
        p = jnp.exp2(a - (m_new - cq2))
        l_ref[...] = alpha * l_ref[...] + jnp.sum(p, axis=1, keepdims=True)
        acc_ref[...] = alpha * acc_ref[...] + _dot(p, v2d)
        m_ref[...] = m_new

    @pl.when(ki < n_past)
    def _():
        tk = kp_ref.shape[1]
        tile(kp_ref[0].reshape(tk * D_HEADS, D_HD), vp_ref[0].reshape(tk * D_HEADS, D_HD), ckp_ref[0], False)

    @pl.when(ki == n_past)
    def _():
        tile(kn_ref[0].reshape(t_new * D_HEADS, D_HD), vn_ref[0].reshape(t_new * D_HEADS, D_HD), ckn_ref[0], True)
        o_ref[0] = acc_ref[...] / l_ref[...]


def _fox_attention_cached(q_rows, k_past, v_past, k_new, v_new, cq_col, ck_past, ck_new, tk):
    b, rows, hd = q_rows.shape
    past, t_new = k_past.shape[1], k_new.shape[1]
    n_past = past // tk
    pidx = lambda i, kk: (i, jnp.minimum(kk, n_past - 1), 0, 0)
    return pl.pallas_call(
        functools.partial(_fox_cached_kernel, t_new=t_new),
        grid=(b, n_past + 1),
        in_specs=[pl.BlockSpec((1, rows, hd), lambda i, kk: (i, 0, 0)),
                  pl.BlockSpec((1, tk, D_HEADS, D_HD), pidx),
                  pl.BlockSpec((1, tk, D_HEADS, D_HD), pidx),
                  pl.BlockSpec((1, t_new, D_HEADS, D_HD), lambda i, kk: (i, 0, 0, 0)),
                  pl.BlockSpec((1, t_new, D_HEADS, D_HD), lambda i, kk: (i, 0, 0, 0)),
                  pl.BlockSpec((1, rows, 1), lambda i, kk: (i, 0, 0)),
                  pl.BlockSpec((1, 1, tk * D_HEADS), lambda i, kk: (i, 0, jnp.minimum(kk, n_past - 1))),
                  pl.BlockSpec((1, 1, t_new * D_HEADS), lambda i, kk: (i, 0, 0))],
        out_specs=pl.BlockSpec((1, rows, hd), lambda i, kk: (i, 0, 0)),
        out_shape=jax.ShapeDtypeStruct((b, rows, hd), F32),
        scratch_shapes=[pltpu.VMEM((rows, 1), F32), pltpu.VMEM((rows, 1), F32), pltpu.VMEM((rows, hd), F32)],
        compiler_params=_cparams("parallel", "arbitrary"),
        name="fox_attention_cached",
    )(q_rows, k_past, v_past, k_new, v_new, cq_col, ck_past, ck_new)


def _pad_rows(a, rows):
    if a.shape[1] == rows:
        return a
    return jnp.pad(a, ((0, 0), (0, rows - a.shape[1])) + ((0, 0),) * (a.ndim - 2))


def _round_up(n, m):
    return -(-n // m) * m


def _dsa_mixer(x, g, k_past, v_past, ki_past, w, w_out, rel_bias):
    b, t, d = x.shape
    past = k_past.shape[1]
    n_keys = past + t
    if past == 0:
        qt, k4, kb, v4, vt, qit, kw, kwt, ki = _proj(
            x, g, w, ((0, "t", BF16, None), (1, B_HD, F32, None), (1, "rows", BF16, None),
                      (2, B_HD, F32, None), (2, "t", BF16, None), (3, "t", BF16, None),
                      (4, "rows", F32, None), (4, "t", F32, None), (4, ("first", IDX_DIM), F32, None)),
            jnp.zeros((1, LANE), F32))
        far, near = _bias_tiles(rel_bias, 2 * LANE, 2 * LANE)
        y = _dsa_attention_prompt(qt, qit, kwt, kw, kb, vt, far, near, x, w_out)
        return (y, k4, v4, ki)
    q, k4, v4, qidx, kw = _proj(
        x, g, w, ((0, "rows", BF16, None), (1, B_HD, F32, None), (2, B_HD, F32, None),
                  (3, "rows", BF16, None), (4, "rows", F32, None)),
        jnp.zeros((1, LANE), F32))
    ki = kw[:, :, :IDX_DIM]
    to_rows = lambda a, nh: jnp.swapaxes(a.reshape(b, t, nh, -1), 1, 2).reshape(b, nh * t, -1)
    kidx_all = _pad_rows(jnp.concatenate([ki_past, ki], axis=1), _round_up(n_keys, LANE))
    o = _dsa_attention_cached(to_rows(q, B_HEADS), to_rows(qidx, IDX_HEADS),
                              to_rows(kw[:, :, IDX_DIM:IDX_DIM + IDX_HEADS], IDX_HEADS),
                              jnp.repeat(rel_bias.T, t, axis=0), kidx_all, k_past, v_past, k4, v4)
    o = jnp.swapaxes(o.reshape(b, B_HEADS, t, B_HD), 1, 2).reshape(b * t, d)
    y = _out_proj(x.reshape(b * t, d), o, w_out).reshape(b, t, d)
    return (y, k4, v4, ki)


def _fox_mixer(x, g, k_past, v_past, lf_past, w, b_f, w_out):
    b, t, d = x.shape
    past = k_past.shape[1]
    heads4 = ((1, D_HD, F32, None), (2, D_HD, F32, None), (3, ("first", D_HEADS), F32, "log_sigmoid"))
    if past == 0:
        tq = tk = min(4 * LANE, t)
        k4, v4, logf, qt, kb, vt = _proj(
            x, g, w, heads4 + ((0, "t", BF16, None), (1, "rows", BF16, None), (2, "t", BF16, None)), b_f)
        cum_t = _cumsum_lanes(jnp.swapaxes(logf, 1, 2))
        y = _fox_attention_prompt(qt, kb, vt, cum_t, jnp.swapaxes(cum_t, 1, 2), x, w_out, tq, tk)
        return (y, k4, v4, logf)
    else:
        tk = math.gcd(past, 4 * LANE)
        k4, v4, logf, q = _proj(x, g, w, heads4 + ((0, "rows", BF16, None),), b_f)
        lf_all = _pad_rows(jnp.concatenate([lf_past, logf], axis=1), _round_up(past + t, LANE))
        cum = jnp.swapaxes(_cumsum_lanes(jnp.swapaxes(lf_all, 1, 2)), 1, 2)[:, :past + t]
        ck = cum.reshape(b, 1, (past + t) * D_HEADS)
        to_rows = lambda a: jnp.swapaxes(a.reshape(b, t, D_HEADS, -1), 1, 2).reshape(b, D_HEADS * t, -1)
        o = _fox_attention_cached(to_rows(q), k_past, v_past, k4, v4, to_rows(cum[:, past:]),
                                  ck[:, :, :past * D_HEADS], ck[:, :, past * D_HEADS:], tk)
        o = jnp.swapaxes(o.reshape(b, D_HEADS, t, D_HD), 1, 2).reshape(b, t, d)
    y = _out_proj(x.reshape(b * t, d), o.reshape(b * t, d), w_out).reshape(b, t, d)
    return (y, k4, v4, logf)


def _run_group(x, pos0, a_st, b_k, b_v, b_ki, c_st, d_k, d_v, d_lf, mem_k, mem_v, prm):
    b, t, d = x.shape
    depth = prm["norm_mix"].shape[0]
    new = {n: [] for n in ("a", "bk", "bv", "bki", "c", "dk", "dv", "dlf")}
    for i in range(depth):
        kind, j = i % 4, i // 4
        g = prm["norm_mix"][i]
        if kind == 0:
            x, st = _conv_mixer(x, g, prm["a_w_in"][j], prm["a_conv"][j], a_st[j], prm["a_w_out"][j])
            new["a"].append(st)
        elif kind == 1:
            x, kk, vv, ki = _dsa_mixer(x, g, b_k[j], b_v[j], b_ki[j], prm["b_w"][j], prm["b_w_out"][j],
                                       prm["rel_bias"])
            new["bk"].append(kk); new["bv"].append(vv); new["bki"].append(ki)
        elif kind == 2:
            x, st = _pool_mixer(x, g, c_st[j], prm["c_w_group"][j], prm["c_scale"][j], pos0)
            new["c"].append(st)
        else:
            x, kk, vv, lf = _fox_mixer(x, g, d_k[j], d_v[j], d_lf[j], prm["d_w"][j], prm["d_b_f"][j],
                                       prm["d_w_out"][j])
            new["dk"].append(kk); new["dv"].append(vv); new["dlf"].append(lf)
        x = _xattn(x, prm["norm_xattn"], prm["xa_wq"], mem_k, mem_v, prm["xa_wo"], i)
        last = i == depth - 1
        x = _ffn(x.reshape(b * t, d), prm["norm_ffn"], prm["ffn_w1"], prm["ffn_w2"],
                 prm["final_norm"], i, last).reshape(b, t, d)
    return (x,) + tuple(jnp.stack(new[n]) for n in ("a", "bk", "bv", "bki", "c", "dk", "dv", "dlf"))


def kernel(x_prompt, x_sample, state_a_conv, cache_b_k, cache_b_v, cache_b_kidx, state_c_pool,
           cache_d_k, cache_d_v, cache_d_logf, cache_mem_k, cache_mem_v, mem_prompt,
           norm_mix, norm_xattn, norm_mem, norm_ffn, final_norm,
           a_w_in, a_conv, a_w_out, b_w_in, b_w_out, rel_bias, c_w_group, c_scale,
           d_w_in, d_b_f, d_w_out, xa_wq, xa_wkv, xa_wo, ffn_w1, ffn_w2):
    bp, t, d = x_prompt.shape
    depth = norm_mix.shape[0]
    n_b, n_d = b_w_in.shape[0], d_w_in.shape[0]
    bf = lambda w: w.astype(BF16)

    def split_cols(w, widths):
        out, c = [], 0
        for wd in widths:
            piece = w[:, c:c + wd]
            c += wd
            if wd % LANE:
                piece = jnp.pad(piece, ((0, 0), (0, _round_up(wd, LANE) - wd)))
            out.append(bf(piece))
        assert c == w.shape[1]
        return out

    b_q, b_kvw = B_HEADS * B_HD, B_KV * B_HD
    b_w = [split_cols(b_w_in[j], (b_q, b_kvw, b_kvw, IDX_HEADS * IDX_DIM, IDX_DIM + IDX_HEADS))
           for j in range(n_b)]
    d_w = [split_cols(d_w_in[j], (d, d, d, D_HEADS)) for j in range(n_d)]
    d_bf = [jnp.pad(d_b_f[j], (0, LANE - D_HEADS)).reshape(1, LANE) for j in range(n_d)]

    prm = {"norm_mix": norm_mix, "norm_xattn": norm_xattn, "norm_ffn": norm_ffn, "final_norm": final_norm,
           "a_w_in": bf(a_w_in), "a_conv": a_conv, "a_w_out": bf(a_w_out),
           "b_w": b_w, "b_w_out": bf(b_w_out), "rel_bias": rel_bias,
           "c_w_group": bf(c_w_group), "c_scale": c_scale,
           "d_w": d_w, "d_b_f": d_bf, "d_w_out": bf(d_w_out),
           "xa_wq": bf(xa_wq), "xa_wo": bf(xa_wo), "ffn_w1": bf(ffn_w1), "ffn_w2": bf(ffn_w2)}

    n_mem = mem_prompt.shape[1]
    mk, mv, mk_rows, mv_rows = _memory_kv(mem_prompt, norm_mem, bf(xa_wkv))

    n_a, n_c = a_w_in.shape[0], c_w_group.shape[0]
    z = lambda *s: jnp.zeros(s, F32)
    gp = _run_group(x_prompt, 0,
                    z(n_a, bp, CONV_W - 1, d),
                    z(n_b, bp, 0, B_KV, B_HD), z(n_b, bp, 0, B_KV, B_HD), z(n_b, bp, 0, IDX_DIM),
                    z(n_c, bp, POOL_STATE, d),
                    z(n_d, bp, 0, D_HEADS, D_HD), z(n_d, bp, 0, D_HEADS, D_HD), z(n_d, bp, 0, D_HEADS),
                    mk_rows, mv_rows, prm)

    bs = x_sample.shape[0]
    past_len = cache_b_k.shape[2]
    gs = _run_group(x_sample, past_len, state_a_conv, cache_b_k, cache_b_v, cache_b_kidx, state_c_pool,
                    cache_d_k, cache_d_v, cache_d_logf, cache_mem_k, cache_mem_v, prm)

    (y_p, a_p, bk_p, bv_p, bki_p, c_p, dk_p, dv_p, dlf_p) = gp
    (y_s, a_s, bk_s, bv_s, bki_s, c_s, dk_s, dv_s, dlf_s) = gs
    return (y_p, y_s, a_p, a_s, bk_p, bv_p, bki_p, bk_s, bv_s, bki_s, c_p, c_s,
            dk_p, dv_p, dlf_p, dk_s, dv_s, dlf_s, mk, mv)
```

```python
import functools
import math

import jax
import jax.numpy as jnp
from jax import lax
from jax.experimental import pallas as pl
from jax.experimental.pallas import tpu as pltpu

F32 = jnp.float32
BF16 = jnp.bfloat16
I32 = jnp.int32
I16 = jnp.int16

EPS = 1e-6
NEG_INF = -1e30
LOG2E = math.log2(math.e)
CHUNK = 64
LANE = 128
VMEM_LIMIT = 48 * 1024 * 1024

CONV_W = 3
POOL_WINDOWS = (2, 4, 8, 16)
POOL_STATE = max(POOL_WINDOWS) - 1
B_HEADS, B_KV, B_HD = 8, 2, 128
B_REP = B_HEADS // B_KV
IDX_HEADS, IDX_DIM = 8, 64
TOPK_MAX = 256
N_BUCKETS, MAX_DIST = 32, 128
D_HEADS, D_HD = 8, 128
MEM_HEADS = 4
INT_MIN = -2147483648
BIAS_CENTER = 2 * LANE


def _cparams(*sem):
    return pltpu.CompilerParams(dimension_semantics=sem, vmem_limit_bytes=VMEM_LIMIT)


def _dot(a, b):
    return jnp.dot(a.astype(BF16), b.astype(BF16), preferred_element_type=F32)


def _dot_nt(a, b):
    return lax.dot_general(a.astype(BF16), b.astype(BF16), (((1,), (1,)), ((), ())),
                           preferred_element_type=F32)


def _dot_tn(a, b):
    return lax.dot_general(a.astype(BF16), b.astype(BF16), (((0,), (0,)), ((), ())),
                           preferred_element_type=F32)


def _rms(x, g):
    return x * lax.rsqrt(jnp.mean(x * x, axis=-1, keepdims=True) + EPS) * g


def _finish_heads(o_ref, x_ref, w_ref, l_ref, acc_ref, n_heads):
    hd = acc_ref.shape[0] // n_heads
    inv_l = 1.0 / l_ref[...]
    heads_t = jnp.concatenate([acc_ref[h * hd:(h + 1) * hd, :] * inv_l[h:h + 1, :] for h in range(n_heads)],
                              axis=0)
    o_ref[0] = x_ref[0] + _dot_tn(heads_t, w_ref[...])


def _row_tile(n, cap):
    t = min(n, cap)
    assert n % t == 0
    return t


def _memkv_kernel(mem_ref, g_ref, w_ref, k_ref, v_ref, kb_ref, vb_ref):
    m = mem_ref[0]
    mn = m * lax.rsqrt(jnp.mean(m * m, axis=-1, keepdims=True) + EPS)
    h = (mn * g_ref[0]).astype(BF16)
    d = m.shape[-1]
    hd = d // MEM_HEADS
    k = jnp.dot(h, w_ref[0, :, :d], preferred_element_type=F32)
    v = jnp.dot(h, w_ref[0, :, d:], preferred_element_type=F32)
    kb_ref[0, 0] = k.astype(BF16)
    vb_ref[0, 0] = v.astype(BF16)
    k_ref[0, 0] = pltpu.einshape("m(hd)->mhd", k, d=hd)
    v_ref[0, 0] = pltpu.einshape("m(hd)->mhd", v, d=hd)


def _memory_kv(mem, g_mem, w_kv):
    depth, d = g_mem.shape
    b, nm, _ = mem.shape
    hd = d // MEM_HEADS
    out = jax.ShapeDtypeStruct((depth, b, nm, MEM_HEADS, hd), F32)
    out_b = jax.ShapeDtypeStruct((depth, b, nm, d), BF16)
    heads_spec = pl.BlockSpec((1, 1, nm, MEM_HEADS, hd), lambda l, i: (l, i, 0, 0, 0))
    rows_spec = pl.BlockSpec((1, 1, nm, d), lambda l, i: (l, i, 0, 0))
    return pl.pallas_call(
        _memkv_kernel,
        grid=(depth, b),
        in_specs=[pl.BlockSpec((1, nm, d), lambda l, i: (i, 0, 0)),
                  pl.BlockSpec((1, 1, d), lambda l, i: (l, 0, 0)),
                  pl.BlockSpec((1, d, 2 * d), lambda l, i: (l, 0, 0))],
        out_specs=[heads_spec, heads_spec, rows_spec, rows_spec],
        out_shape=[out, out, out_b, out_b],
        compiler_params=_cparams("parallel", "parallel"),
        name="memory_kv",
    )(mem, g_mem.reshape(depth, 1, d), w_kv)


def _ffn_kernel(x_ref, g_ref, w1_ref, w2_ref, gf_ref, o_ref, h_ref, acc_ref, *, final_norm):
    j = pl.program_id(1)

    @pl.when(j == 0)
    def _():
        h_ref[...] = _rms(x_ref[...], g_ref[...]).astype(BF16)
        acc_ref[...] = jnp.zeros_like(acc_ref)

    u = jnp.maximum(jnp.dot(h_ref[...], w1_ref[...], preferred_element_type=F32), 0.0)
    acc_ref[...] += jnp.dot((u * u).astype(BF16), w2_ref[...], preferred_element_type=F32)

    @pl.when(j == pl.num_programs(1) - 1)
    def _():
        y = x_ref[...] + acc_ref[...]
        o_ref[...] = _rms(y, gf_ref[...]) if final_norm else y


def _ffn(x, g, w1, w2, gf, layer, final_norm):
    n, d = x.shape
    f = w1.shape[2]
    tm = _row_tile(n, 1024)
    tf = 1024
    return pl.pallas_call(
        functools.partial(_ffn_kernel, final_norm=final_norm),
        grid=(n // tm, f // tf),
        in_specs=[pl.BlockSpec((tm, d), lambda i, j: (i, 0)),
                  pl.BlockSpec((None, 1, d), lambda i, j: (layer, 0, 0)),
                  pl.BlockSpec((None, d, tf), lambda i, j: (layer, 0, j)),
                  pl.BlockSpec((None, tf, d), lambda i, j: (layer, j, 0)),
                  pl.BlockSpec((1, d), lambda i, j: (0, 0))],
        out_specs=pl.BlockSpec((tm, d), lambda i, j: (i, 0)),
        out_shape=jax.ShapeDtypeStruct((n, d), F32),
        scratch_shapes=[pltpu.VMEM((tm, d), BF16), pltpu.VMEM((tm, d), F32)],
        compiler_params=_cparams("parallel", "arbitrary"),
        name="ffn",
    )(x, g.reshape(-1, 1, d), w1, w2, gf.reshape(1, d))


def _xattn_kernel(x_ref, g_ref, wq_ref, mk_ref, mv_ref, wo_ref, o_ref):
    x = x_ref[0]
    d = x.shape[-1]
    hd = d // MEM_HEADS
    h = _rms(x, g_ref[...]).astype(BF16)
    q = jnp.dot(h, wq_ref[...], preferred_element_type=F32)
    outs = []
    if len(mk_ref.shape) == 3:
        mk = pltpu.einshape("mhd->m(hd)", mk_ref[...]).astype(BF16)
        mv = pltpu.einshape("mhd->m(hd)", mv_ref[...]).astype(BF16)
    else:
        mk, mv = mk_ref[...], mv_ref[...]
    for hh in range(MEM_HEADS):
        sl = slice(hh * hd, (hh + 1) * hd)
        kh, vh = mk[:, sl], mv[:, sl]
        s = _dot_nt(q[:, sl], kh) * (hd ** -0.5)
        m = jnp.max(s, axis=-1, keepdims=True)
        p = jnp.exp(s - m)
        l = jnp.sum(p, axis=-1, keepdims=True)
        outs.append(_dot(p, vh) / l)
    o = jnp.concatenate(outs, axis=-1)
    o_ref[0] = x + _dot(o, wo_ref[...])


def _xattn(x, g, wq, mk, mv, wo, layer):
    b, t, d = x.shape
    tm = _row_tile(t, 512)
    kv_spec = pl.BlockSpec((None, None) + mk.shape[2:], lambda i, j: (layer, i) + (0,) * (mk.ndim - 2))
    return pl.pallas_call(
        _xattn_kernel,
        grid=(b, t // tm),
        in_specs=[pl.BlockSpec((1, tm, d), lambda i, j: (i, j, 0)),
                  pl.BlockSpec((None, 1, d), lambda i, j: (layer, 0, 0)),
                  pl.BlockSpec((None, d, d), lambda i, j: (layer, 0, 0)),
                  kv_spec, kv_spec,
                  pl.BlockSpec((None, d, d), lambda i, j: (layer, 0, 0))],
        out_specs=pl.BlockSpec((1, tm, d), lambda i, j: (i, j, 0)),
        out_shape=jax.ShapeDtypeStruct((b, t, d), F32),
        compiler_params=_cparams("parallel", "parallel"),
        name="xattn",
    )(x, g.reshape(-1, 1, d), wq, mk, mv, wo)


def _conv_kernel(x_ref, g_ref, win_ref, wc_ref, st_ref, wout_ref, o_ref, nst_ref, z_ref):
    t = pl.program_id(1)
    x = x_ref[0]
    tm, d = x.shape
    pad = 8

    @pl.when(t == 0)
    def _():
        z_ref[pad - 2:pad, :] = st_ref[0]

    h = _rms(x, g_ref[...]).astype(BF16)
    bg = jnp.dot(h, win_ref[:, 0:d], preferred_element_type=F32)
    cg = jnp.dot(h, win_ref[:, d:2 * d], preferred_element_type=F32)
    u = jnp.dot(h, win_ref[:, 2 * d:3 * d], preferred_element_type=F32)
    z = cg * u
    z_ref[pad:pad + tm, :] = z
    conv = (z_ref[pad - 2:pad - 2 + tm, :] * wc_ref[0:1, :]
            + z_ref[pad - 1:pad - 1 + tm, :] * wc_ref[1:2, :]
            + z * wc_ref[2:3, :])
    o_ref[0] = x + _dot(bg * conv, wout_ref[...])
    last = z_ref[pad + tm - 2:pad + tm, :]
    z_ref[pad - 2:pad, :] = last

    @pl.when(t == pl.num_programs(1) - 1)
    def _():
        nst_ref[0] = last


def _conv_mixer(x, g, w_in, w_conv, state, w_out):
    b, t, d = x.shape
    tm = _row_tile(t, 512)
    return pl.pallas_call(
        _conv_kernel,
        grid=(b, t // tm),
        in_specs=[pl.BlockSpec((1, tm, d), lambda i, j: (i, j, 0)),
                  pl.BlockSpec((1, d), lambda i, j: (0, 0)),
                  pl.BlockSpec((d, 3 * d), lambda i, j: (0, 0)),
                  pl.BlockSpec((CONV_W, d), lambda i, j: (0, 0)),
                  pl.BlockSpec((1, CONV_W - 1, d), lambda i, j: (i, 0, 0)),
                  pl.BlockSpec((d, d), lambda i, j: (0, 0))],
        out_specs=[pl.BlockSpec((1, tm, d), lambda i, j: (i, j, 0)),
                   pl.BlockSpec((1, CONV_W - 1, d), lambda i, j: (i, 0, 0))],
        out_shape=[jax.ShapeDtypeStruct((b, t, d), F32),
                   jax.ShapeDtypeStruct((b, CONV_W - 1, d), F32)],
        scratch_shapes=[pltpu.VMEM((tm + 8, d), F32)],
        compiler_params=_cparams("parallel", "arbitrary"),
        name="conv_mixer",
    )(x, g.reshape(1, d), w_in, w_conv, state, w_out)


def _pool_kernel(x_ref, g_ref, st_ref, wg_ref, sc_ref, o_ref, nst_ref, h_ref, *, pos0):
    t = pl.program_id(1)
    x = x_ref[0]
    tm, d = x.shape
    gw = d // len(POOL_WINDOWS)
    base = POOL_STATE + 1

    @pl.when(t == 0)
    def _():
        h_ref[1:base, :] = st_ref[0]

    h = _rms(x, g_ref[...])
    h_ref[base:base + tm, :] = h
    pos = pos0 + t * tm + lax.broadcasted_iota(I32, (tm, gw), 0)
    ys = []
    for gi, w in enumerate(POOL_WINDOWS):
        sl = slice(gi * gw, (gi + 1) * gw)
        win = h[:, sl]
        for j in range(1, w):
            win = win + h_ref[base - j:base - j + tm, sl]
        count = jnp.minimum(w, pos + 1).astype(F32)
        dlt = win / count - h[:, sl]
        ys.append(_dot(dlt, wg_ref[gi]))
    y = jnp.concatenate(ys, axis=-1) * sc_ref[...]
    o_ref[0] = x + y
    last = h_ref[tm + 1:tm + base, :]
    h_ref[1:base, :] = last

    @pl.when(t == pl.num_programs(1) - 1)
    def _():
        nst_ref[0] = last


def _pool_mixer(x, g, state, w_group, scale, pos0):
    b, t, d = x.shape
    ng, gw, _ = w_group.shape
    tm = _row_tile(t, 512)
    return pl.pallas_call(
        functools.partial(_pool_kernel, pos0=pos0),
        grid=(b, t // tm),
        in_specs=[pl.BlockSpec((1, tm, d), lambda i, j: (i, j, 0)),
                  pl.BlockSpec((1, d), lambda i, j: (0, 0)),
                  pl.BlockSpec((1, POOL_STATE, d), lambda i, j: (i, 0, 0)),
                  pl.BlockSpec((ng, gw, gw), lambda i, j: (0, 0, 0)),
                  pl.BlockSpec((1, d), lambda i, j: (0, 0))],
        out_specs=[pl.BlockSpec((1, tm, d), lambda i, j: (i, j, 0)),
                   pl.BlockSpec((1, POOL_STATE, d), lambda i, j: (i, 0, 0))],
        out_shape=[jax.ShapeDtypeStruct((b, t, d), F32),
                   jax.ShapeDtypeStruct((b, POOL_STATE, d), F32)],
        scratch_shapes=[pltpu.VMEM((tm + POOL_STATE + 1, d), F32)],
        compiler_params=_cparams("parallel", "arbitrary"),
        name="pool_mixer",
    )(x, g.reshape(1, d), state, w_group, scale.reshape(1, d))


def _proj_kernel(*refs, n_w, outs):
    x_ref, g_ref = refs[0], refs[1]
    w_refs = refs[2:2 + n_w]
    e_ref = refs[2 + n_w]
    o_refs = refs[3 + n_w:]
    h = _rms(x_ref[0], g_ref[...]).astype(BF16)
    ys = {}
    for (wi, mode, _, ep), o_ref in zip(outs, o_refs):
        if wi not in ys:
            ys[wi] = jnp.dot(h, w_refs[wi][...], preferred_element_type=F32)
        y = ys[wi]
        if ep == "log_sigmoid":
            u = -(y + e_ref[...])
            y = -(jnp.maximum(u, 0.0) + jnp.log1p(jnp.exp(-jnp.abs(u))))
        if mode == "rows":
            o_ref[0] = y.astype(o_ref.dtype)
        elif isinstance(mode, tuple):
            o_ref[0] = y[:, :mode[1]].astype(o_ref.dtype)
        elif mode == "t":
            o_ref[0] = jnp.transpose(y).astype(o_ref.dtype)
        else:
            o_ref[0] = pltpu.einshape("m(hd)->mhd", y.astype(o_ref.dtype), d=mode)


def _proj(x, g, ws, outs, extra):
    b, t, d = x.shape
    tm = _row_tile(t, 512)
    in_specs = [pl.BlockSpec((1, tm, d), lambda i, j: (i, j, 0)), pl.BlockSpec((1, d), lambda i, j: (0, 0))]
    in_specs += [pl.BlockSpec(w.shape, lambda i, j: (0, 0)) for w in ws]
    in_specs += [pl.BlockSpec(extra.shape, lambda i, j: (0, 0))]
    out_specs, out_shape = [], []
    for wi, mode, dt, _ in outs:
        n = ws[wi].shape[1]
        if mode == "rows" or isinstance(mode, tuple):
            n = n if mode == "rows" else mode[1]
            out_specs.append(pl.BlockSpec((1, tm, n), lambda i, j: (i, j, 0)))
            out_shape.append(jax.ShapeDtypeStruct((b, t, n), dt))
        elif mode == "t":
            out_specs.append(pl.BlockSpec((1, n, tm), lambda i, j: (i, 0, j)))
            out_shape.append(jax.ShapeDtypeStruct((b, n, t), dt))
        else:
            out_specs.append(pl.BlockSpec((1, tm, n // mode, mode), lambda i, j: (i, j, 0, 0)))
            out_shape.append(jax.ShapeDtypeStruct((b, t, n // mode, mode), dt))
    return pl.pallas_call(
        functools.partial(_proj_kernel, n_w=len(ws), outs=tuple(outs)),
        grid=(b, t // tm),
        in_specs=in_specs,
        out_specs=out_specs,
        out_shape=out_shape,
        compiler_params=_cparams("parallel", "parallel"),
        name="norm_proj",
    )(x, g.reshape(1, d), *ws, extra)


def _outproj_kernel(x_ref, a_ref, w_ref, o_ref):
    o_ref[...] = x_ref[...] + _dot(a_ref[...], w_ref[...])


def _out_proj(x, a, w):
    n, d = x.shape
    tm = _row_tile(n, 512)
    return pl.pallas_call(
        _outproj_kernel,
        grid=(n // tm,),
        in_specs=[pl.BlockSpec((tm, d), lambda i: (i, 0)),
                  pl.BlockSpec((tm, d), lambda i: (i, 0)),
                  pl.BlockSpec((d, d), lambda i: (0, 0))],
        out_specs=pl.BlockSpec((tm, d), lambda i: (i, 0)),
        out_shape=jax.ShapeDtypeStruct((n, d), F32),
        compiler_params=_cparams("parallel"),
        name="out_proj",
    )(x, a, w)


def _bias_table_kernel(rbt_ref, o_ref):
    width = o_ref.shape[-1]
    rel = BIAS_CENTER - lax.broadcasted_iota(I32, (1, width), 1)
    nb = N_BUCKETS // 2
    max_exact = nb // 2
    ret = (rel > 0).astype(I32) * nb
    n = jnp.abs(rel)
    nf = jnp.maximum(n, 1).astype(F32)
    large = max_exact + (jnp.log(nf / max_exact) / math.log(MAX_DIST / max_exact)
                         * (nb - max_exact)).astype(I32)
    large = jnp.minimum(large, nb - 1)
    bucket = ret + jnp.where(n < max_exact, n, large)
    acc = jnp.zeros(o_ref.shape, F32)
    for j in range(N_BUCKETS):
        acc = jnp.where(bucket == j, rbt_ref[:, j:j + 1], acc)
    o_ref[...] = acc * LOG2E


def _bias_tiles_kernel(rbt_ref, far_ref, near_ref, tab_ref):
    n_d, nh, kb, tq = near_ref.shape
    _bias_table_kernel(rbt_ref, tab_ref)
    far_ref[...] = jnp.broadcast_to(tab_ref[:, BIAS_CENTER + MAX_DIST:BIAS_CENTER + MAX_DIST + 1], far_ref.shape)
    for dd in range(n_d):
        s0 = BIAS_CENTER - (dd - 1) * kb - kb
        for h in range(nh):
            rows = jnp.broadcast_to(tab_ref[h:h + 1, s0:s0 + tq + kb], (kb, tq + kb))
            near_ref[dd, h] = pltpu.roll(rows, 0, 1, stride=1, stride_axis=0)[:, kb:]


def _bias_tiles(rel_bias, kb, tq):
    nh = rel_bias.shape[1]
    n_d = tq // kb + 1
    width = BIAS_CENTER + kb + tq + kb
    assert kb % LANE == 0 and tq % kb == 0 and kb >= MAX_DIST and BIAS_CENTER >= tq
    return pl.pallas_call(
        _bias_tiles_kernel,
        out_shape=[jax.ShapeDtypeStruct((nh, LANE), F32), jax.ShapeDtypeStruct((n_d, nh, kb, tq), F32)],
        scratch_shapes=[pltpu.VMEM((nh, width), F32)],
        name="bias_tiles",
    )(rel_bias.T)


def _sortable(x):
    x = jnp.where(x == 0.0, 0.0, x)
    bits = lax.bitcast_convert_type(x, I32)
    return jnp.where(bits < 0, bits ^ 0x7FFFFFFF, bits)


def _neg_inf_key():
    import numpy as np
    b = int(np.float32(NEG_INF).view(np.int32))
    return b ^ 0x7FFFFFFF


def _dsa_prompt_kernel(qt_ref, qit_ref, kwt_ref, kw_ref, k_ref, vt_ref, far_ref, near_ref, x_ref, wout_ref, o_ref,
                       key_ref, hi_ref, lo_ref, sel_ref, m_ref, l_ref, acc_ref, a_ref, *, top_k):
    qi = pl.program_id(1)
    tq = qt_ref.shape[2]
    n_keys = kw_ref.shape[1]
    kb_sz = LANE
    q0 = qi * tq
    nkb = jnp.minimum(n_keys, q0 + tq) // kb_sz
    negkey = _neg_inf_key()

    qlane = lax.broadcasted_iota(I32, (1, tq), 1)
    lim = ((q0 + qlane) // CHUNK + 1) * CHUNK
    krow = lax.broadcasted_iota(I32, (kb_sz, tq), 0)

    def kslice(kb):
        return pl.ds(pl.multiple_of(kb * kb_sz, kb_sz), kb_sz)

    sb = 2 * kb_sz
    srow = lax.broadcasted_iota(I32, (sb, tq), 0)

    def score_body(i, c):
        rows = pl.ds(pl.multiple_of(i * sb, sb), sb)
        kid = kw_ref[0, rows, :][:, :IDX_DIM].astype(BF16)
        sc = jnp.zeros((sb, tq), F32)
        for h in range(IDX_HEADS):
            s = jnp.dot(kid, qit_ref[0, h * IDX_DIM:(h + 1) * IDX_DIM, :], preferred_element_type=F32)
            sc = sc + jnp.maximum(s, 0.0) * kwt_ref[0, IDX_DIM + h:IDX_DIM + h + 1, :]
        sc = sc * ((IDX_DIM * IDX_HEADS) ** -0.5)
        key = jnp.where(i * sb + srow < lim, _sortable(sc), negkey)
        key_ref[rows, :] = key
        hi_ref[rows, :] = (key >> 16).astype(I16)
        lo_ref[rows, :] = ((key & 0xFFFF) - 0x8000).astype(I16)
        return c

    lax.fori_loop(0, nkb // 2, score_body, 0)

    def search16(ref):
        def bit_body(i, t_u):
            cand_u = t_u | jnp.left_shift(jnp.int32(1), 15 - i)
            cand = (cand_u - 0x8000).astype(I16)

            def body(j, a):
                ind = jnp.where(ref[pl.ds(pl.multiple_of(j * sb, sb), sb), :] >= cand,
                                jnp.ones((), I16), jnp.zeros((), I16))
                parts = [ind[16 * r:16 * (r + 1), :] for r in range(sb // 16)]
                while len(parts) > 1:
                    parts = [parts[r] + parts[r + 1] for r in range(0, len(parts), 2)]
                return a + parts[0]
            a = lax.fori_loop(0, nkb // 2, body, jnp.zeros((16, tq), I16))
            cnt = jnp.sum(a.astype(I32), axis=0, keepdims=True)
            return jnp.where(cnt >= top_k, cand_u, t_u)
        return lax.fori_loop(0, 16, bit_body, jnp.zeros((1, tq), I32))

    def count(pred_fn):
        def body(i, a):
            for u in range(2):
                kb = 2 * i + u
                ind = pred_fn(kb, key_ref[kslice(kb), :])
                a = a + jnp.sum(ind.reshape(kb_sz // 8, 8, tq), axis=0)
            return a
        a = lax.fori_loop(0, nkb // 2, body, jnp.zeros((8, tq), I32))
        return jnp.sum(a, axis=0, keepdims=True)

    t_hi = search16(hi_ref)
    t_hi16 = (t_hi - 0x8000).astype(I16)

    def lo_body(j, c):
        rows = pl.ds(pl.multiple_of(j * sb, sb), sb)
        hi = hi_ref[rows, :]
        lo_ref[rows, :] = jnp.where(hi == t_hi16, lo_ref[rows, :],
                                    jnp.where(hi > t_hi16, jnp.full((), 0x7FFF, I16), jnp.full((), -0x8000, I16)))
        return c

    lax.fori_loop(0, nkb // 2, lo_body, 0)
    t_s = (jnp.left_shift(t_hi, 16) | search16(lo_ref)) ^ INT_MIN

    def adm01(kb):
        return jnp.where(kb * kb_sz + krow < lim, 1.0, 0.0)

    def sel_body(kb, a):
        sel = jnp.where(key_ref[kslice(kb), :] >= t_s, adm01(kb), 0.0)
        sel_ref[kslice(kb), :] = sel
        return a + jnp.sum(sel.reshape(kb_sz // 8, 8, tq), axis=0)

    n_sel = jnp.sum(lax.fori_loop(0, nkb, sel_body, jnp.zeros((8, tq), F32)), axis=0, keepdims=True)

    @pl.when(jnp.max(n_sel) > top_k)
    def _():
        n_gt = count(lambda kb, key: jnp.where(key > t_s, 1, 0))
        need = (top_k - n_gt).astype(F32)
        r = lax.broadcasted_iota(I32, (kb_sz, kb_sz), 0)
        c = lax.broadcasted_iota(I32, (kb_sz, kb_sz), 1)
        ltri = jnp.where(c < r, 1.0, 0.0).astype(BF16)

        def tie_body(kb, carry):
            key = key_ref[kslice(kb), :]
            adm = adm01(kb)
            eq = jnp.where(key == t_s, adm, 0.0)
            rank = carry + jnp.dot(ltri, eq.astype(BF16), preferred_element_type=F32)
            keep = jnp.where(rank < need, eq, 0.0)
            sel_ref[kslice(kb), :] = jnp.where(key > t_s, adm, keep)
            return carry + jnp.sum(eq, axis=0, keepdims=True)

        lax.fori_loop(0, nkb, tie_body, jnp.zeros((1, tq), F32))

    m_ref[...] = jnp.full(m_ref.shape, NEG_INF, F32)
    l_ref[...] = jnp.zeros(l_ref.shape, F32)
    acc_ref[...] = jnp.zeros(acc_ref.shape, F32)
    c1 = (B_HD ** -0.5) * LOG2E

    def attend(k0, nk, bias2_fn):
        rows = pl.ds(pl.multiple_of(k0, LANE), nk)
        sel = sel_ref[rows, :] != 0.0
        ks = k_ref[0, rows, :]
        cols = []
        for h in range(B_HEADS):
            g = h // B_REP
            z = jnp.dot(ks[:, g * B_HD:(g + 1) * B_HD], qt_ref[0, h * B_HD:(h + 1) * B_HD, :],
                        preferred_element_type=F32)
            a = jnp.where(sel, z * c1 + bias2_fn(h), NEG_INF)
            a_ref[h, 0:nk, :] = a
            cols.append(jnp.max(a, axis=0, keepdims=True))
        m_old = m_ref[...]
        m_new = jnp.maximum(m_old, jnp.concatenate(cols, axis=0))
        alpha = jnp.exp2(m_old - m_new)
        m_ref[...] = m_new
        sums = []
        for h in range(B_HEADS):
            g = h // B_REP
            p = jnp.exp2(a_ref[h, 0:nk, :] - m_new[h:h + 1, :])
            sums.append(jnp.sum(p, axis=0, keepdims=True))
            hs = slice(h * B_HD, (h + 1) * B_HD)
            pv = jnp.dot(vt_ref[0, g * B_HD:(g + 1) * B_HD, rows], p.astype(BF16),
                         preferred_element_type=F32)
            acc_ref[hs, :] = alpha[h:h + 1, :] * acc_ref[hs, :] + pv
        l_ref[...] = alpha * l_ref[...] + jnp.concatenate(sums, axis=0)

    ab = near_ref.shape[2]
    n_far = jnp.maximum(q0 // ab - 1, 0)
    far_bias2 = far_ref[:, 0:1]

    def far_body(i, c):
        attend(i * ab, ab, lambda h: far_bias2[h:h + 1, :])
        return c

    lax.fori_loop(0, n_far, far_body, 0)

    def near_body(i, c):
        dd = i - q0 // ab + 1
        attend(i * ab, ab, lambda h: near_ref[dd, h])
        return c

    lax.fori_loop(n_far, nkb * kb_sz // ab, near_body, 0)

    _finish_heads(o_ref, x_ref, wout_ref, l_ref, acc_ref, B_HEADS)


def _dsa_attention_prompt(qt, qit, kwt, kw, k, vt, far, near, x, w_out):
    b, d, t = qt.shape
    tq = near.shape[3]
    top_k = min(TOPK_MAX, t // 4)
    ab = near.shape[2]
    assert t % tq == 0 and tq % CHUNK == 0 and tq % ab == 0 and ab % LANE == 0 and tq % (2 * LANE) == 0
    return pl.pallas_call(
        functools.partial(_dsa_prompt_kernel, top_k=top_k),
        grid=(b, t // tq),
        in_specs=[pl.BlockSpec((1, d, tq), lambda i, j: (i, 0, j)),
                  pl.BlockSpec((1, qit.shape[1], tq), lambda i, j: (i, 0, j)),
                  pl.BlockSpec((1, LANE, tq), lambda i, j: (i, 0, j)),
                  pl.BlockSpec((1, t, LANE), lambda i, j: (i, 0, 0)),
                  pl.BlockSpec((1, t, B_KV * B_HD), lambda i, j: (i, 0, 0)),
                  pl.BlockSpec((1, B_KV * B_HD, t), lambda i, j: (i, 0, 0)),
                  pl.BlockSpec(far.shape, lambda i, j: (0, 0)),
                  pl.BlockSpec(near.shape, lambda i, j: (0, 0, 0, 0)),
                  pl.BlockSpec((1, tq, d), lambda i, j: (i, j, 0)),
                  pl.BlockSpec((d, d), lambda i, j: (0, 0))],
        out_specs=pl.BlockSpec((1, tq, d), lambda i, j: (i, j, 0)),
        out_shape=jax.ShapeDtypeStruct((b, t, d), F32),
        scratch_shapes=[pltpu.VMEM((t, tq), I32), pltpu.VMEM((t, tq), I16), pltpu.VMEM((t, tq), I16),
                        pltpu.VMEM((t, tq), F32),
                        pltpu.VMEM((B_HEADS, tq), F32), pltpu.VMEM((B_HEADS, tq), F32),
                        pltpu.VMEM((d, tq), F32), pltpu.VMEM((B_HEADS, ab, tq), F32)],
        compiler_params=_cparams("parallel", "parallel"),
        name="dsa_attention",
    )(qt, qit, kwt, kw, k, vt, far, near, x, w_out)


def _dsa_cached_kernel(q_ref, qi_ref, wi_ref, rb_ref, kidx_ref, kp_ref, vp_ref, kn_ref, vn_ref, o_ref,
                       *, past, t_new, top_k):
    n_keys = past + t_new
    lp = kidx_ref.shape[1]
    negkey = _neg_inf_key()
    kpos = lax.broadcasted_iota(I32, (t_new, lp), 1)
    qpos = past + lax.broadcasted_iota(I32, (t_new, lp), 0)
    adm = kpos < (qpos // CHUNK + 1) * CHUNK

    s = _dot_nt(qi_ref[0], kidx_ref[0])
    w = jnp.maximum(s, 0.0) * wi_ref[0]
    sc = w[0:t_new]
    for h in range(1, IDX_HEADS):
        sc = sc + w[h * t_new:(h + 1) * t_new]
    sc = sc * ((IDX_DIM * IDX_HEADS) ** -0.5)
    key = jnp.where(adm, _sortable(sc), negkey)
    key = jnp.where(kpos < n_keys, key, INT_MIN)

    def bit_body(i, t_u):
        cand_u = t_u | jnp.left_shift(jnp.int32(1), 31 - i)
        cnt = jnp.sum(jnp.where(key >= (cand_u ^ INT_MIN), 1.0, 0.0), axis=1, keepdims=True)
        return jnp.where(cnt >= top_k, cand_u, t_u)

    t_s = lax.fori_loop(0, 32, bit_body, jnp.zeros((t_new, 1), I32)) ^ INT_MIN

    adm01 = jnp.where(adm, 1.0, 0.0)
    gt = jnp.where(key > t_s, adm01, 0.0)
    eq = jnp.where(key == t_s, adm01, 0.0)
    need = top_k - jnp.sum(jnp.where(key > t_s, 1.0, 0.0), axis=1, keepdims=True)
    r = lax.broadcasted_iota(I32, (LANE, LANE), 0)
    c = lax.broadcasted_iota(I32, (LANE, LANE), 1)
    utri = jnp.where(r < c, 1.0, 0.0).astype(BF16)
    carry = jnp.zeros((t_new, 1), F32)
    keeps = []
    for blk in range(lp // LANE):
        e = eq[:, blk * LANE:(blk + 1) * LANE]
        rank = carry + jnp.dot(e.astype(BF16), utri, preferred_element_type=F32)
        keeps.append(jnp.where(rank < need, e, 0.0))
        carry = carry + jnp.sum(e, axis=1, keepdims=True)
    sel = gt + jnp.concatenate(keeps, axis=1)

    near = max(past - MAX_DIST, 0) // LANE * LANE
    rel = (kpos - qpos)[:, near:]
    nb = N_BUCKETS // 2
    max_exact = nb // 2
    n = jnp.abs(rel)
    nf = jnp.maximum(n, 1).astype(F32)
    large = max_exact + (jnp.log(nf / max_exact) / math.log(MAX_DIST / max_exact)
                         * (nb - max_exact)).astype(I32)
    bucket = (rel > 0).astype(I32) * nb + jnp.where(n < max_exact, n, jnp.minimum(large, nb - 1))

    rows = B_REP * t_new
    sel_g = jnp.concatenate([sel] * B_REP, axis=0) != 0.0
    bucket_g = jnp.concatenate([bucket] * B_REP, axis=0)
    for g in range(B_KV):
        grp = lambda ref, n: ref[0, pl.ds(g, n, stride=B_KV), :]
        qg = q_ref[0, g * rows:(g + 1) * rows, :]
        rb = rb_ref[g * rows:(g + 1) * rows, :]
        bias_near = jnp.zeros((rows, lp - near), F32)
        for j in range(N_BUCKETS):
            bias_near = jnp.where(bucket_g == j, rb[:, j:j + 1], bias_near)
        bias = jnp.concatenate([jnp.broadcast_to(rb[:, nb - 1:nb], (rows, near)), bias_near], axis=1)
        zp = _dot_nt(qg, grp(kp_ref, past)) * (B_HD ** -0.5)
        zn = _dot_nt(qg, grp(kn_ref, t_new)) * (B_HD ** -0.5)
        ap = jnp.where(sel_g[:, :past], zp + bias[:, :past], NEG_INF)
        an = jnp.where(sel_g[:, past:n_keys], zn + bias[:, past:n_keys], NEG_INF)
        m = jnp.maximum(jnp.max(ap, axis=1, keepdims=True), jnp.max(an, axis=1, keepdims=True))
        pp, pn = jnp.exp(ap - m), jnp.exp(an - m)
        l = jnp.sum(pp, axis=1, keepdims=True) + jnp.sum(pn, axis=1, keepdims=True)
        o_ref[0, g * rows:(g + 1) * rows, :] = (_dot(pp, grp(vp_ref, past)) + _dot(pn, grp(vn_ref, t_new))) / l


def _dsa_attention_cached(q_rows, qi_rows, wi_col, rb_rows, kidx_all, k_past, v_past, k_new, v_new):
    b, rows, hd = q_rows.shape
    past, t_new = k_past.shape[1], k_new.shape[1]
    lp = kidx_all.shape[1]
    top_k = min(TOPK_MAX, (past + t_new) // 4)
    assert past % LANE == 0
    flat = lambda a: a.reshape(b, a.shape[1] * B_KV, B_HD)
    kv_spec = lambda n: pl.BlockSpec((1, n * B_KV, B_HD), lambda i: (i, 0, 0))
    return pl.pallas_call(
        functools.partial(_dsa_cached_kernel, past=past, t_new=t_new, top_k=top_k),
        grid=(b,),
        in_specs=[pl.BlockSpec((1, rows, hd), lambda i: (i, 0, 0)),
                  pl.BlockSpec((1,) + qi_rows.shape[1:], lambda i: (i, 0, 0)),
                  pl.BlockSpec((1,) + wi_col.shape[1:], lambda i: (i, 0, 0)),
                  pl.BlockSpec(rb_rows.shape, lambda i: (0, 0)),
                  pl.BlockSpec((1, lp, IDX_DIM), lambda i: (i, 0, 0)),
                  kv_spec(past), kv_spec(past), kv_spec(t_new), kv_spec(t_new)],
        out_specs=pl.BlockSpec((1, rows, hd), lambda i: (i, 0, 0)),
        out_shape=jax.ShapeDtypeStruct((b, rows, hd), F32),
        compiler_params=_cparams("parallel"),
        name="dsa_attention_cached",
    )(q_rows, qi_rows, wi_col, rb_rows, kidx_all, flat(k_past), flat(v_past), flat(k_new), flat(v_new))


def _cumsum_kernel(x_ref, o_ref):
    x = x_ref[0]
    n = x.shape[-1]
    lane = lax.broadcasted_iota(I32, x.shape, 1)
    s = 1
    while s < n:
        x = x + jnp.where(lane >= s, pltpu.roll(x, s, 1), 0.0)
        s *= 2
    o_ref[0] = x


def _cumsum_lanes(x):
    b, h, n = x.shape
    return pl.pallas_call(
        _cumsum_kernel,
        grid=(b,),
        in_specs=[pl.BlockSpec((1, h, n), lambda i: (i, 0, 0))],
        out_specs=pl.BlockSpec((1, h, n), lambda i: (i, 0, 0)),
        out_shape=jax.ShapeDtypeStruct((b, h, n), F32),
        compiler_params=_cparams("parallel"),
        name="logf_cumsum",
    )(x)


def _fox_init(m_ref, l_ref, acc_ref):
    m_ref[...] = jnp.full(m_ref.shape, NEG_INF, F32)
    l_ref[...] = jnp.zeros(l_ref.shape, F32)
    acc_ref[...] = jnp.zeros(acc_ref.shape, F32)


def _fox_tile(z_fn, pv_fn, cq, ck, mask, m_ref, l_ref, acc_ref, a_ref):
    c1 = (D_HD ** -0.5) * LOG2E
    cq2, ck2 = cq * LOG2E, ck * LOG2E
    cols = []
    for h in range(D_HEADS):
        a = z_fn(h) * c1 - ck2[:, h:h + 1]
        if mask is not None:
            a = jnp.where(mask, a, NEG_INF)
        a_ref[h] = a
        cols.append(jnp.max(a, axis=0, keepdims=True))
    m_old = m_ref[...]
    m_new = jnp.maximum(m_old, jnp.concatenate(cols, axis=0) + cq2)
    alpha = jnp.exp2(m_old - m_new)
    shift = m_new - cq2
    m_ref[...] = m_new
    sums = []
    for h in range(D_HEADS):
        p = jnp.exp2(a_ref[h] - shift[h:h + 1, :])
        sums.append(jnp.sum(p, axis=0, keepdims=True))
        hs = slice(h * D_HD, (h + 1) * D_HD)
        acc_ref[hs, :] = alpha[h:h + 1, :] * acc_ref[hs, :] + pv_fn(h, p.astype(BF16))
    l_ref[...] = alpha * l_ref[...] + jnp.concatenate(sums, axis=0)


def _hs(h):
    return slice(h * D_HD, (h + 1) * D_HD)


def _fox_prompt_kernel(qt_ref, k_ref, vt_ref, cq_ref, ck_ref, x_ref, wout_ref, o_ref,
                       m_ref, l_ref, acc_ref, a_ref):
    qi, step = pl.program_id(1), pl.program_id(2)
    tq, tk = qt_ref.shape[2], k_ref.shape[1]
    q0 = qi * tq
    ki = step - (pl.num_programs(2) - 1 - (q0 + tq - 1) // tk)
    k0 = ki * tk

    @pl.when(step == 0)
    def _():
        _fox_init(m_ref, l_ref, acc_ref)

    def run(masked):
        mask = None
        if masked:
            mask = (k0 + lax.broadcasted_iota(I32, (tk, tq), 0)) <= (q0 + lax.broadcasted_iota(I32, (tk, tq), 1))
        _fox_tile(lambda h: jnp.dot(k_ref[0, :, _hs(h)], qt_ref[0, _hs(h), :], preferred_element_type=F32),
                  lambda h, p: jnp.dot(vt_ref[0, _hs(h), :], p, preferred_element_type=F32),
                  cq_ref[0], ck_ref[0], mask, m_ref, l_ref, acc_ref, a_ref)

    fully_visible = k0 + tk - 1 <= q0
    pl.when(jnp.logical_and(ki >= 0, fully_visible))(lambda: run(False))
    pl.when(jnp.logical_and(ki >= 0, jnp.logical_not(fully_visible)))(lambda: run(True))

    @pl.when(step == pl.num_programs(2) - 1)
    def _():
        _finish_heads(o_ref, x_ref, wout_ref, l_ref, acc_ref, D_HEADS)


def _fox_attention_prompt(qt, k, vt, cum_t, cum, x, w_out, tq, tk):
    b, d, t = qt.shape
    nq, nk = t // tq, t // tk
    ktile = lambda j, s: jnp.maximum(s - (nk - 1 - (j * tq + tq - 1) // tk), 0)
    return pl.pallas_call(
        _fox_prompt_kernel,
        grid=(b, nq, nk),
        in_specs=[pl.BlockSpec((1, d, tq), lambda i, j, kk: (i, 0, j)),
                  pl.BlockSpec((1, tk, d), lambda i, j, kk: (i, ktile(j, kk), 0)),
                  pl.BlockSpec((1, d, tk), lambda i, j, kk: (i, 0, ktile(j, kk))),
                  pl.BlockSpec((1, D_HEADS, tq), lambda i, j, kk: (i, 0, j)),
                  pl.BlockSpec((1, tk, D_HEADS), lambda i, j, kk: (i, ktile(j, kk), 0)),
                  pl.BlockSpec((1, tq, d), lambda i, j, kk: (i, j, 0)),
                  pl.BlockSpec((d, d), lambda i, j, kk: (0, 0))],
        out_specs=pl.BlockSpec((1, tq, d), lambda i, j, kk: (i, j, 0)),
        out_shape=jax.ShapeDtypeStruct((b, t, d), F32),
        scratch_shapes=[pltpu.VMEM((D_HEADS, tq), F32), pltpu.VMEM((D_HEADS, tq), F32),
                        pltpu.VMEM((d, tq), F32), pltpu.VMEM((D_HEADS, tk, tq), F32)],
        compiler_params=_cparams("parallel", "parallel", "arbitrary"),
        name="fox_attention",
    )(qt, k, vt, cum_t, cum, x, w_out)


def _fox_cached_kernel(q_ref, kp_ref, vp_ref, kn_ref, vn_ref, cq_ref, ckp_ref, ckn_ref, o_ref,
                       m_ref, l_ref, acc_ref, *, t_new):
    ki = pl.program_id(1)
    n_past = pl.num_programs(1) - 1
    rows = q_ref.shape[1]
    c1 = (D_HD ** -0.5) * LOG2E

    @pl.when(ki == 0)
    def _():
        m_ref[...] = jnp.full(m_ref.shape, NEG_INF, F32)
        l_ref[...] = jnp.zeros(l_ref.shape, F32)
        acc_ref[...] = jnp.zeros(acc_ref.shape, F32)

    def tile(k2d, v2d, ck_row, causal):
        cols = k2d.shape[0]
        a = _dot_nt(q_ref[0], k2d) * c1 - ck_row * LOG2E
        r = lax.broadcasted_iota(I32, (rows, cols), 0)
        c = lax.broadcasted_iota(I32, (rows, cols), 1)
        ok = (c % D_HEADS) == (r // t_new)
        if causal:
            ok = jnp.logical_and(ok, (c // D_HEADS) <= (r % t_new))
        a = jnp.where(ok, a, NEG_INF)
        cq2 = cq_ref[0] * LOG2E
        m_old = m_ref[...]
        m_new = jnp.maximum(m_old, jnp.max(a, axis=1, keepdims=True) + cq2)
        alpha = jnp.exp2(m_old - m_new)
        p = jnp.exp2(a - (m_new - cq2))
        l_ref[...] = alpha * l_ref[...] + jnp.sum(p, axis=1, keepdims=True)
        acc_ref[...] = alpha * acc_ref[...] + _dot(p, v2d)
        m_ref[...] = m_new

    @pl.when(ki < n_past)
    def _():
        tk = kp_ref.shape[1]
        tile(kp_ref[0].reshape(tk * D_HEADS, D_HD), vp_ref[0].reshape(tk * D_HEADS, D_HD), ckp_ref[0], False)

    @pl.when(ki == n_past)
    def _():
        tile(kn_ref[0].reshape(t_new * D_HEADS, D_HD), vn_ref[0].reshape(t_new * D_HEADS, D_HD), ckn_ref[0], True)
        o_ref[0] = acc_ref[...] / l_ref[...]


def _fox_attention_cached(q_rows, k_past, v_past, k_new, v_new, cq_col, ck_past, ck_new, tk):
    b, rows, hd = q_rows.shape
    past, t_new = k_past.shape[1], k_new.shape[1]
    n_past = past // tk
    pidx = lambda i, kk: (i, jnp.minimum(kk, n_past - 1), 0, 0)
    return pl.pallas_call(
        functools.partial(_fox_cached_kernel, t_new=t_new),
        grid=(b, n_past + 1),
        in_specs=[pl.BlockSpec((1, rows, hd), lambda i, kk: (i, 0, 0)),
                  pl.BlockSpec((1, tk, D_HEADS, D_HD), pidx),
                  pl.BlockSpec((1, tk, D_HEADS, D_HD), pidx),
                  pl.BlockSpec((1, t_new, D_HEADS, D_HD), lambda i, kk: (i, 0, 0, 0)),
                  pl.BlockSpec((1, t_new, D_HEADS, D_HD), lambda i, kk: (i, 0, 0, 0)),
                  pl.BlockSpec((1, rows, 1), lambda i, kk: (i, 0, 0)),
                  pl.BlockSpec((1, 1, tk * D_HEADS), lambda i, kk: (i, 0, jnp.minimum(kk, n_past - 1))),
                  pl.BlockSpec((1, 1, t_new * D_HEADS), lambda i, kk: (i, 0, 0))],
        out_specs=pl.BlockSpec((1, rows, hd), lambda i, kk: (i, 0, 0)),
        out_shape=jax.ShapeDtypeStruct((b, rows, hd), F32),
        scratch_shapes=[pltpu.VMEM((rows, 1), F32), pltpu.VMEM((rows, 1), F32), pltpu.VMEM((rows, hd), F32)],
        compiler_params=_cparams("parallel", "arbitrary"),
        name="fox_attention_cached",
    )(q_rows, k_past, v_past, k_new, v_new, cq_col, ck_past, ck_new)


def _pad_rows(a, rows):
    if a.shape[1] == rows:
        return a
    return jnp.pad(a, ((0, 0), (0, rows - a.shape[1])) + ((0, 0),) * (a.ndim - 2))


def _round_up(n, m):
    return -(-n // m) * m


def _dsa_mixer(x, g, k_past, v_past, ki_past, w, w_out, rel_bias):
    b, t, d = x.shape
    past = k_past.shape[1]
    n_keys = past + t
    if past == 0:
        qt, k4, kb, v4, vt, qit, kw, kwt, ki = _proj(
            x, g, w, ((0, "t", BF16, None), (1, B_HD, F32, None), (1, "rows", BF16, None),
                      (2, B_HD, F32, None), (2, "t", BF16, None), (3, "t", BF16, None),
                      (4, "rows", F32, None), (4, "t", F32, None), (4, ("first", IDX_DIM), F32, None)),
            jnp.zeros((1, LANE), F32))
        far, near = _bias_tiles(rel_bias, 2 * LANE, 2 * LANE)
        y = _dsa_attention_prompt(qt, qit, kwt, kw, kb, vt, far, near, x, w_out)
        return (y, k4, v4, ki)
    q, k4, v4, qidx, kw = _proj(
        x, g, w, ((0, "rows", BF16, None), (1, B_HD, F32, None), (2, B_HD, F32, None),
                  (3, "rows", BF16, None), (4, "rows", F32, None)),
        jnp.zeros((1, LANE), F32))
    ki = kw[:, :, :IDX_DIM]
    to_rows = lambda a, nh: jnp.swapaxes(a.reshape(b, t, nh, -1), 1, 2).reshape(b, nh * t, -1)
    kidx_all = _pad_rows(jnp.concatenate([ki_past, ki], axis=1), _round_up(n_keys, LANE))
    o = _dsa_attention_cached(to_rows(q, B_HEADS), to_rows(qidx, IDX_HEADS),
                              to_rows(kw[:, :, IDX_DIM:IDX_DIM + IDX_HEADS], IDX_HEADS),
                              jnp.repeat(rel_bias.T, t, axis=0), kidx_all, k_past, v_past, k4, v4)
    o = jnp.swapaxes(o.reshape(b, B_HEADS, t, B_HD), 1, 2).reshape(b * t, d)
    y = _out_proj(x.reshape(b * t, d), o, w_out).reshape(b, t, d)
    return (y, k4, v4, ki)


def _fox_mixer(x, g, k_past, v_past, lf_past, w, b_f, w_out):
    b, t, d = x.shape
    past = k_past.shape[1]
    heads4 = ((1, D_HD, F32, None), (2, D_HD, F32, None), (3, ("first", D_HEADS), F32, "log_sigmoid"))
    if past == 0:
        tq = tk = min(4 * LANE, t)
        k4, v4, logf, qt, kb, vt = _proj(
            x, g, w, heads4 + ((0, "t", BF16, None), (1, "rows", BF16, None), (2, "t", BF16, None)), b_f)
        cum_t = _cumsum_lanes(jnp.swapaxes(logf, 1, 2))
        y = _fox_attention_prompt(qt, kb, vt, cum_t, jnp.swapaxes(cum_t, 1, 2), x, w_out, tq, tk)
        return (y, k4, v4, logf)
    else:
        tk = math.gcd(past, 4 * LANE)
        k4, v4, logf, q = _proj(x, g, w, heads4 + ((0, "rows", BF16, None),), b_f)
        lf_all = _pad_rows(jnp.concatenate([lf_past, logf], axis=1), _round_up(past + t, LANE))
        cum = jnp.swapaxes(_cumsum_lanes(jnp.swapaxes(lf_all, 1, 2)), 1, 2)[:, :past + t]
        ck = cum.reshape(b, 1, (past + t) * D_HEADS)
        to_rows = lambda a: jnp.swapaxes(a.reshape(b, t, D_HEADS, -1), 1, 2).reshape(b, D_HEADS * t, -1)
        o = _fox_attention_cached(to_rows(q), k_past, v_past, k4, v4, to_rows(cum[:, past:]),
                                  ck[:, :, :past * D_HEADS], ck[:, :, past * D_HEADS:], tk)
        o = jnp.swapaxes(o.reshape(b, D_HEADS, t, D_HD), 1, 2).reshape(b, t, d)
    y = _out_proj(x.reshape(b * t, d), o.reshape(b * t, d), w_out).reshape(b, t, d)
    return (y, k4, v4, logf)


def _run_group(x, pos0, a_st, b_k, b_v, b_ki, c_st, d_k, d_v, d_lf, mem_k, mem_v, prm):
    b, t, d = x.shape
    depth = prm["norm_mix"].shape[0]
    new = {n: [] for n in ("a", "bk", "bv", "bki", "c", "dk", "dv", "dlf")}
    for i in range(depth):
        kind, j = i % 4, i // 4
        g = prm["norm_mix"][i]
        if kind == 0:
            x, st = _conv_mixer(x, g, prm["a_w_in"][j], prm["a_conv"][j], a_st[j], prm["a_w_out"][j])
            new["a"].append(st)
        elif kind == 1:
            x, kk, vv, ki = _dsa_mixer(x, g, b_k[j], b_v[j], b_ki[j], prm["b_w"][j], prm["b_w_out"][j],
                                       prm["rel_bias"])
            new["bk"].append(kk); new["bv"].append(vv); new["bki"].append(ki)
        elif kind == 2:
            x, st = _pool_mixer(x, g, c_st[j], prm["c_w_group"][j], prm["c_scale"][j], pos0)
            new["c"].append(st)
        else:
            x, kk, vv, lf = _fox_mixer(x, g, d_k[j], d_v[j], d_lf[j], prm["d_w"][j], prm["d_b_f"][j],
                                       prm["d_w_out"][j])
            new["dk"].append(kk); new["dv"].append(vv); new["dlf"].append(lf)
        x = _xattn(x, prm["norm_xattn"], prm["xa_wq"], mem_k, mem_v, prm["xa_wo"], i)
        last = i == depth - 1
        x = _ffn(x.reshape(b * t, d), prm["norm_ffn"], prm["ffn_w1"], prm["ffn_w2"],
                 prm["final_norm"], i, last).reshape(b, t, d)
    return (x,) + tuple(jnp.stack(new[n]) for n in ("a", "bk", "bv", "bki", "c", "dk", "dv", "dlf"))


def kernel(x_prompt, x_sample, state_a_conv, cache_b_k, cache_b_v, cache_b_kidx, state_c_pool,
           cache_d_k, cache_d_v, cache_d_logf, cache_mem_k, cache_mem_v, mem_prompt,
           norm_mix, norm_xattn, norm_mem, norm_ffn, final_norm,
           a_w_in, a_conv, a_w_out, b_w_in, b_w_out, rel_bias, c_w_group, c_scale,
           d_w_in, d_b_f, d_w_out, xa_wq, xa_wkv, xa_wo, ffn_w1, ffn_w2):
    bp, t, d = x_prompt.shape
    depth = norm_mix.shape[0]
    n_b, n_d = b_w_in.shape[0], d_w_in.shape[0]
    bf = lambda w: w.astype(BF16)

    def split_cols(w, widths):
        out, c = [], 0
        for wd in widths:
            piece = w[:, c:c + wd]
            c += wd
            if wd % LANE:
                piece = jnp.pad(piece, ((0, 0), (0, _round_up(wd, LANE) - wd)))
            out.append(bf(piece))
        assert c == w.shape[1]
        return out

    b_q, b_kvw = B_HEADS * B_HD, B_KV * B_HD
    b_w = [split_cols(b_w_in[j], (b_q, b_kvw, b_kvw, IDX_HEADS * IDX_DIM, IDX_DIM + IDX_HEADS))
           for j in range(n_b)]
    d_w = [split_cols(d_w_in[j], (d, d, d, D_HEADS)) for j in range(n_d)]
    d_bf = [jnp.pad(d_b_f[j], (0, LANE - D_HEADS)).reshape(1, LANE) for j in range(n_d)]

    prm = {"norm_mix": norm_mix, "norm_xattn": norm_xattn, "norm_ffn": norm_ffn, "final_norm": final_norm,
           "a_w_in": bf(a_w_in), "a_conv": a_conv, "a_w_out": bf(a_w_out),
           "b_w": b_w, "b_w_out": bf(b_w_out), "rel_bias": rel_bias,
           "c_w_group": bf(c_w_group), "c_scale": c_scale,
           "d_w": d_w, "d_b_f": d_bf, "d_w_out": bf(d_w_out),
           "xa_wq": bf(xa_wq), "xa_wo": bf(xa_wo), "ffn_w1": bf(ffn_w1), "ffn_w2": bf(ffn_w2)}

    n_mem = mem_prompt.shape[1]
    mk, mv, mk_rows, mv_rows = _memory_kv(mem_prompt, norm_mem, bf(xa_wkv))

    n_a, n_c = a_w_in.shape[0], c_w_group.shape[0]
    z = lambda *s: jnp.zeros(s, F32)
    gp = _run_group(x_prompt, 0,
                    z(n_a, bp, CONV_W - 1, d),
                    z(n_b, bp, 0, B_KV, B_HD), z(n_b, bp, 0, B_KV, B_HD), z(n_b, bp, 0, IDX_DIM),
                    z(n_c, bp, POOL_STATE, d),
                    z(n_d, bp, 0, D_HEADS, D_HD), z(n_d, bp, 0, D_HEADS, D_HD), z(n_d, bp, 0, D_HEADS),
                    mk_rows, mv_rows, prm)

    bs = x_sample.shape[0]
    past_len = cache_b_k.shape[2]
    gs = _run_group(x_sample, past_len, state_a_conv, cache_b_k, cache_b_v, cache_b_kidx, state_c_pool,
                    cache_d_k, cache_d_v, cache_d_logf, cache_mem_k, cache_mem_v, prm)

    (y_p, a_p, bk_p, bv_p, bki_p, c_p, dk_p, dv_p, dlf_p) = gp
    (y_s, a_s, bk_s, bv_s, bki_s, c_s, dk_s, dv_s, dlf_s) = gs
    return (y_p, y_s, a_p, a_s, bk_p, bv_p, bki_p, bk_s, bv_s, bki_s, c_p, c_s,
            dk_p, dv_p, dlf_p, dk_s, dv_s, dlf_s, mk, mv)
```

```python
import functools
import math

import jax
import jax.numpy as jnp
from jax import lax
from jax.experimental import pallas as pl
from jax.experimental.pallas import tpu as pltpu

F32 = jnp.float32
BF16 = jnp.bfloat16
I32 = jnp.int32
I16 = jnp.int16

EPS = 1e-6
NEG_INF = -1e30
LOG2E = math.log2(math.e)
CHUNK = 64
LANE = 128
VMEM_LIMIT = 48 * 1024 * 1024

CONV_W = 3
POOL_WINDOWS = (2, 4, 8, 16)
POOL_STATE = max(POOL_WINDOWS) - 1
B_HEADS, B_KV, B_HD = 8, 2, 128
B_REP = B_HEADS // B_KV
IDX_HEADS, IDX_DIM = 8, 64
TOPK_MAX = 256
N_BUCKETS, MAX_DIST = 32, 128
D_HEADS, D_HD = 8, 128
MEM_HEADS = 4
INT_MIN = -2147483648
BIAS_CENTER = 2 * LANE


def _cparams(*sem):
    return pltpu.CompilerParams(dimension_semantics=sem, vmem_limit_bytes=VMEM_LIMIT)


def _dot(a, b):
    return jnp.dot(a.astype(BF16), b.astype(BF16), preferred_element_type=F32)


def _dot_nt(a, b):
    return lax.dot_general(a.astype(BF16), b.astype(BF16), (((1,), (1,)), ((), ())),
                           preferred_element_type=F32)


def _dot_tn(a, b):
    return lax.dot_general(a.astype(BF16), b.astype(BF16), (((0,), (0,)), ((), ())),
                           preferred_element_type=F32)


def _rms(x, g):
    return x * lax.rsqrt(jnp.mean(x * x, axis=-1, keepdims=True) + EPS) * g


def _finish_heads(o_ref, x_ref, w_ref, l_ref, acc_ref, n_heads):
    hd = acc_ref.shape[0] // n_heads
    inv_l = 1.0 / l_ref[...]
    heads_t = jnp.concatenate([acc_ref[h * hd:(h + 1) * hd, :] * inv_l[h:h + 1, :] for h in range(n_heads)],
                              axis=0)
    o_ref[0] = x_ref[0] + _dot_tn(heads_t, w_ref[...])


def _row_tile(n, cap):
    t = min(n, cap)
    assert n % t == 0
    return t


def _memkv_kernel(mem_ref, g_ref, w_ref, k_ref, v_ref, kb_ref, vb_ref):
    m = mem_ref[0]
    mn = m * lax.rsqrt(jnp.mean(m * m, axis=-1, keepdims=True) + EPS)
    h = (mn * g_ref[0]).astype(BF16)
    d = m.shape[-1]
    hd = d // MEM_HEADS
    k = jnp.dot(h, w_ref[0, :, :d], preferred_element_type=F32)
    v = jnp.dot(h, w_ref[0, :, d:], preferred_element_type=F32)
    kb_ref[0, 0] = k.astype(BF16)
    vb_ref[0, 0] = v.astype(BF16)
    k_ref[0, 0] = pltpu.einshape("m(hd)->mhd", k, d=hd)
    v_ref[0, 0] = pltpu.einshape("m(hd)->mhd", v, d=hd)


def _memory_kv(mem, g_mem, w_kv):
    depth, d = g_mem.shape
    b, nm, _ = mem.shape
    hd = d // MEM_HEADS
    out = jax.ShapeDtypeStruct((depth, b, nm, MEM_HEADS, hd), F32)
    out_b = jax.ShapeDtypeStruct((depth, b, nm, d), BF16)
    heads_spec = pl.BlockSpec((1, 1, nm, MEM_HEADS, hd), lambda l, i: (l, i, 0, 0, 0))
    rows_spec = pl.BlockSpec((1, 1, nm, d), lambda l, i: (l, i, 0, 0))
    return pl.pallas_call(
        _memkv_kernel,
        grid=(depth, b),
        in_specs=[pl.BlockSpec((1, nm, d), lambda l, i: (i, 0, 0)),
                  pl.BlockSpec((1, 1, d), lambda l, i: (l, 0, 0)),
                  pl.BlockSpec((1, d, 2 * d), lambda l, i: (l, 0, 0))],
        out_specs=[heads_spec, heads_spec, rows_spec, rows_spec],
        out_shape=[out, out, out_b, out_b],
        compiler_params=_cparams("parallel", "parallel"),
        name="memory_kv",
    )(mem, g_mem.reshape(depth, 1, d), w_kv)


def _ffn_kernel(x_ref, g_ref, w1_ref, w2_ref, gf_ref, o_ref, h_ref, acc_ref, *, final_norm):
    j = pl.program_id(1)

    @pl.when(j == 0)
    def _():
        h_ref[...] = _rms(x_ref[...], g_ref[...]).astype(BF16)
        acc_ref[...] = jnp.zeros_like(acc_ref)

    u = jnp.maximum(jnp.dot(h_ref[...], w1_ref[...], preferred_element_type=F32), 0.0)
    acc_ref[...] += jnp.dot((u * u).astype(BF16), w2_ref[...], preferred_element_type=F32)

    @pl.when(j == pl.num_programs(1) - 1)
    def _():
        y = x_ref[...] + acc_ref[...]
        o_ref[...] = _rms(y, gf_ref[...]) if final_norm else y


def _ffn(x, g, w1, w2, gf, layer, final_norm):
    n, d = x.shape
    f = w1.shape[2]
    tm = _row_tile(n, 1024)
    tf = 1024
    return pl.pallas_call(
        functools.partial(_ffn_kernel, final_norm=final_norm),
        grid=(n // tm, f // tf),
        in_specs=[pl.BlockSpec((tm, d), lambda i, j: (i, 0)),
                  pl.BlockSpec((None, 1, d), lambda i, j: (layer, 0, 0)),
                  pl.BlockSpec((None, d, tf), lambda i, j: (layer, 0, j)),
                  pl.BlockSpec((None, tf, d), lambda i, j: (layer, j, 0)),
                  pl.BlockSpec((1, d), lambda i, j: (0, 0))],
        out_specs=pl.BlockSpec((tm, d), lambda i, j: (i, 0)),
        out_shape=jax.ShapeDtypeStruct((n, d), F32),
        scratch_shapes=[pltpu.VMEM((tm, d), BF16), pltpu.VMEM((tm, d), F32)],
        compiler_params=_cparams("parallel", "arbitrary"),
        name="ffn",
    )(x, g.reshape(-1, 1, d), w1, w2, gf.reshape(1, d))


def _xattn_kernel(x_ref, g_ref, wq_ref, mk_ref, mv_ref, wo_ref, o_ref):
    x = x_ref[0]
    d = x.shape[-1]
    hd = d // MEM_HEADS
    h = _rms(x, g_ref[...]).astype(BF16)
    q = jnp.dot(h, wq_ref[...], preferred_element_type=F32)
    outs = []
    if len(mk_ref.shape) == 3:
        mk = pltpu.einshape("mhd->m(hd)", mk_ref[...]).astype(BF16)
        mv = pltpu.einshape("mhd->m(hd)", mv_ref[...]).astype(BF16)
    else:
        mk, mv = mk_ref[...], mv_ref[...]
    for hh in range(MEM_HEADS):
        sl = slice(hh * hd, (hh + 1) * hd)
        kh, vh = mk[:, sl], mv[:, sl]
        s = _dot_nt(q[:, sl], kh) * (hd ** -0.5)
        m = jnp.max(s, axis=-1, keepdims=True)
        p = jnp.exp(s - m)
        l = jnp.sum(p, axis=-1, keepdims=True)
        outs.append(_dot(p, vh) / l)
    o = jnp.concatenate(outs, axis=-1)
    o_ref[0] = x + _dot(o, wo_ref[...])


def _xattn(x, g, wq, mk, mv, wo, layer):
    b, t, d = x.shape
    tm = _row_tile(t, 512)
    kv_spec = pl.BlockSpec((None, None) + mk.shape[2:], lambda i, j: (layer, i) + (0,) * (mk.ndim - 2))
    return pl.pallas_call(
        _xattn_kernel,
        grid=(b, t // tm),
        in_specs=[pl.BlockSpec((1, tm, d), lambda i, j: (i, j, 0)),
                  pl.BlockSpec((None, 1, d), lambda i, j: (layer, 0, 0)),
                  pl.BlockSpec((None, d, d), lambda i, j: (layer, 0, 0)),
                  kv_spec, kv_spec,
                  pl.BlockSpec((None, d, d), lambda i, j: (layer, 0, 0))],
        out_specs=pl.BlockSpec((1, tm, d), lambda i, j: (i, j, 0)),
        out_shape=jax.ShapeDtypeStruct((b, t, d), F32),
        compiler_params=_cparams("parallel", "parallel"),
        name="xattn",
    )(x, g.reshape(-1, 1, d), wq, mk, mv, wo)


def _conv_kernel(x_ref, g_ref, win_ref, wc_ref, st_ref, wout_ref, o_ref, nst_ref, z_ref):
    t = pl.program_id(1)
    x = x_ref[0]
    tm, d = x.shape
    pad = 8

    @pl.when(t == 0)
    def _():
        z_ref[pad - 2:pad, :] = st_ref[0]

    h = _rms(x, g_ref[...]).astype(BF16)
    bg = jnp.dot(h, win_ref[:, 0:d], preferred_element_type=F32)
    cg = jnp.dot(h, win_ref[:, d:2 * d], preferred_element_type=F32)
    u = jnp.dot(h, win_ref[:, 2 * d:3 * d], preferred_element_type=F32)
    z = cg * u
    z_ref[pad:pad + tm, :] = z
    conv = (z_ref[pad - 2:pad - 2 + tm, :] * wc_ref[0:1, :]
            + z_ref[pad - 1:pad - 1 + tm, :] * wc_ref[1:2, :]
            + z * wc_ref[2:3, :])
    o_ref[0] = x + _dot(bg * conv, wout_ref[...])
    last = z_ref[pad + tm - 2:pad + tm, :]
    z_ref[pad - 2:pad, :] = last

    @pl.when(t == pl.num_programs(1) - 1)
    def _():
        nst_ref[0] = last


def _conv_mixer(x, g, w_in, w_conv, state, w_out):
    b, t, d = x.shape
    tm = _row_tile(t, 512)
    return pl.pallas_call(
        _conv_kernel,
        grid=(b, t // tm),
        in_specs=[pl.BlockSpec((1, tm, d), lambda i, j: (i, j, 0)),
                  pl.BlockSpec((1, d), lambda i, j: (0, 0)),
                  pl.BlockSpec((d, 3 * d), lambda i, j: (0, 0)),
                  pl.BlockSpec((CONV_W, d), lambda i, j: (0, 0)),
                  pl.BlockSpec((1, CONV_W - 1, d), lambda i, j: (i, 0, 0)),
                  pl.BlockSpec((d, d), lambda i, j: (0, 0))],
        out_specs=[pl.BlockSpec((1, tm, d), lambda i, j: (i, j, 0)),
                   pl.BlockSpec((1, CONV_W - 1, d), lambda i, j: (i, 0, 0))],
        out_shape=[jax.ShapeDtypeStruct((b, t, d), F32),
                   jax.ShapeDtypeStruct((b, CONV_W - 1, d), F32)],
        scratch_shapes=[pltpu.VMEM((tm + 8, d), F32)],
        compiler_params=_cparams("parallel", "arbitrary"),
        name="conv_mixer",
    )(x, g.reshape(1, d), w_in, w_conv, state, w_out)


def _pool_kernel(x_ref, g_ref, st_ref, wg_ref, sc_ref, o_ref, nst_ref, h_ref, *, pos0):
    t = pl.program_id(1)
    x = x_ref[0]
    tm, d = x.shape
    gw = d // len(POOL_WINDOWS)
    base = POOL_STATE + 1

    @pl.when(t == 0)
    def _():
        h_ref[1:base, :] = st_ref[0]

    h = _rms(x, g_ref[...])
    h_ref[base:base + tm, :] = h
    pos = pos0 + t * tm + lax.broadcasted_iota(I32, (tm, gw), 0)
    ys = []
    for gi, w in enumerate(POOL_WINDOWS):
        sl = slice(gi * gw, (gi + 1) * gw)
        win = h[:, sl]
        for j in range(1, w):
            win = win + h_ref[base - j:base - j + tm, sl]
        count = jnp.minimum(w, pos + 1).astype(F32)
        dlt = win / count - h[:, sl]
        ys.append(_dot(dlt, wg_ref[gi]))
    y = jnp.concatenate(ys, axis=-1) * sc_ref[...]
    o_ref[0] = x + y
    last = h_ref[tm + 1:tm + base, :]
    h_ref[1:base, :] = last

    @pl.when(t == pl.num_programs(1) - 1)
    def _():
        nst_ref[0] = last


def _pool_mixer(x, g, state, w_group, scale, pos0):
    b, t, d = x.shape
    ng, gw, _ = w_group.shape
    tm = _row_tile(t, 512)
    return pl.pallas_call(
        functools.partial(_pool_kernel, pos0=pos0),
        grid=(b, t // tm),
        in_specs=[pl.BlockSpec((1, tm, d), lambda i, j: (i, j, 0)),
                  pl.BlockSpec((1, d), lambda i, j: (0, 0)),
                  pl.BlockSpec((1, POOL_STATE, d), lambda i, j: (i, 0, 0)),
                  pl.BlockSpec((ng, gw, gw), lambda i, j: (0, 0, 0)),
                  pl.BlockSpec((1, d), lambda i, j: (0, 0))],
        out_specs=[pl.BlockSpec((1, tm, d), lambda i, j: (i, j, 0)),
                   pl.BlockSpec((1, POOL_STATE, d), lambda i, j: (i, 0, 0))],
        out_shape=[jax.ShapeDtypeStruct((b, t, d), F32),
                   jax.ShapeDtypeStruct((b, POOL_STATE, d), F32)],
        scratch_shapes=[pltpu.VMEM((tm + POOL_STATE + 1, d), F32)],
        compiler_params=_cparams("parallel", "arbitrary"),
        name="pool_mixer",
    )(x, g.reshape(1, d), state, w_group, scale.reshape(1, d))


def _proj_kernel(*refs, n_w, outs):
    x_ref, g_ref = refs[0], refs[1]
    w_refs = refs[2:2 + n_w]
    e_ref = refs[2 + n_w]
    o_refs = refs[3 + n_w:]
    h = _rms(x_ref[0], g_ref[...]).astype(BF16)
    ys = {}
    for (wi, mode, _, ep), o_ref in zip(outs, o_refs):
        if wi not in ys:
            ys[wi] = jnp.dot(h, w_refs[wi][...], preferred_element_type=F32)
        y = ys[wi]
        if ep == "log_sigmoid":
            u = -(y + e_ref[...])
            y = -(jnp.maximum(u, 0.0) + jnp.log1p(jnp.exp(-jnp.abs(u))))
        if mode == "rows":
            o_ref[0] = y.astype(o_ref.dtype)
        elif isinstance(mode, tuple):
            o_ref[0] = y[:, :mode[1]].astype(o_ref.dtype)
        elif mode == "t":
            o_ref[0] = jnp.transpose(y).astype(o_ref.dtype)
        else:
            o_ref[0] = pltpu.einshape("m(hd)->mhd", y.astype(o_ref.dtype), d=mode)


def _proj(x, g, ws, outs, extra):
    b, t, d = x.shape
    tm = _row_tile(t, 512)
    in_specs = [pl.BlockSpec((1, tm, d), lambda i, j: (i, j, 0)), pl.BlockSpec((1, d), lambda i, j: (0, 0))]
    in_specs += [pl.BlockSpec(w.shape, lambda i, j: (0, 0)) for w in ws]
    in_specs += [pl.BlockSpec(extra.shape, lambda i, j: (0, 0))]
    out_specs, out_shape = [], []
    for wi, mode, dt, _ in outs:
        n = ws[wi].shape[1]
        if mode == "rows" or isinstance(mode, tuple):
            n = n if mode == "rows" else mode[1]
            out_specs.append(pl.BlockSpec((1, tm, n), lambda i, j: (i, j, 0)))
            out_shape.append(jax.ShapeDtypeStruct((b, t, n), dt))
        elif mode == "t":
            out_specs.append(pl.BlockSpec((1, n, tm), lambda i, j: (i, 0, j)))
            out_shape.append(jax.ShapeDtypeStruct((b, n, t), dt))
        else:
            out_specs.append(pl.BlockSpec((1, tm, n // mode, mode), lambda i, j: (i, j, 0, 0)))
            out_shape.append(jax.ShapeDtypeStruct((b, t, n // mode, mode), dt))
    return pl.pallas_call(
        functools.partial(_proj_kernel, n_w=len(ws), outs=tuple(outs)),
        grid=(b, t // tm),
        in_specs=in_specs,
        out_specs=out_specs,
        out_shape=out_shape,
        compiler_params=_cparams("parallel", "parallel"),
        name="norm_proj",
    )(x, g.reshape(1, d), *ws, extra)


def _outproj_kernel(x_ref, a_ref, w_ref, o_ref):
    o_ref[...] = x_ref[...] + _dot(a_ref[...], w_ref[...])


def _out_proj(x, a, w):
    n, d = x.shape
    tm = _row_tile(n, 512)
    return pl.pallas_call(
        _outproj_kernel,
        grid=(n // tm,),
        in_specs=[pl.BlockSpec((tm, d), lambda i: (i, 0)),
                  pl.BlockSpec((tm, d), lambda i: (i, 0)),
                  pl.BlockSpec((d, d), lambda i: (0, 0))],
        out_specs=pl.BlockSpec((tm, d), lambda i: (i, 0)),
        out_shape=jax.ShapeDtypeStruct((n, d), F32),
        compiler_params=_cparams("parallel"),
        name="out_proj",
    )(x, a, w)


def _bias_table_kernel(rbt_ref, o_ref):
    width = o_ref.shape[-1]
    rel = BIAS_CENTER - lax.broadcasted_iota(I32, (1, width), 1)
    nb = N_BUCKETS // 2
    max_exact = nb // 2
    ret = (rel > 0).astype(I32) * nb
    n = jnp.abs(rel)
    nf = jnp.maximum(n, 1).astype(F32)
    large = max_exact + (jnp.log(nf / max_exact) / math.log(MAX_DIST / max_exact)
                         * (nb - max_exact)).astype(I32)
    large = jnp.minimum(large, nb - 1)
    bucket = ret + jnp.where(n < max_exact, n, large)
    acc = jnp.zeros(o_ref.shape, F32)
    for j in range(N_BUCKETS):
        acc = jnp.where(bucket == j, rbt_ref[:, j:j + 1], acc)
    o_ref[...] = acc * LOG2E


def _bias_tiles_kernel(rbt_ref, far_ref, near_ref, tab_ref):
    n_d, nh, kb, tq = near_ref.shape
    _bias_table_kernel(rbt_ref, tab_ref)
    far_ref[...] = jnp.broadcast_to(tab_ref[:, BIAS_CENTER + MAX_DIST:BIAS_CENTER + MAX_DIST + 1], far_ref.shape)
    for dd in range(n_d):
        s0 = BIAS_CENTER - (dd - 1) * kb - kb
        for h in range(nh):
            rows = jnp.broadcast_to(tab_ref[h:h + 1, s0:s0 + tq + kb], (kb, tq + kb))
            near_ref[dd, h] = pltpu.roll(rows, 0, 1, stride=1, stride_axis=0)[:, kb:]


def _bias_tiles(rel_bias, kb, tq):
    nh = rel_bias.shape[1]
    n_d = tq // kb + 1
    width = BIAS_CENTER + kb + tq + kb
    assert kb % LANE == 0 and tq % kb == 0 and kb >= MAX_DIST and BIAS_CENTER >= tq
    return pl.pallas_call(
        _bias_tiles_kernel,
        out_shape=[jax.ShapeDtypeStruct((nh, LANE), F32), jax.ShapeDtypeStruct((n_d, nh, kb, tq), F32)],
        scratch_shapes=[pltpu.VMEM((nh, width), F32)],
        name="bias_tiles",
    )(rel_bias.T)


def _sortable(x):
    x = jnp.where(x == 0.0, 0.0, x)
    bits = lax.bitcast_convert_type(x, I32)
    return jnp.where(bits < 0, bits ^ 0x7FFFFFFF, bits)


def _neg_inf_key():
    import numpy as np
    b = int(np.float32(NEG_INF).view(np.int32))
    return b ^ 0x7FFFFFFF


def _dsa_prompt_kernel(qt_ref, qit_ref, kwt_ref, kw_ref, k_ref, vt_ref, far_ref, near_ref, x_ref, wout_ref, o_ref,
                       key_ref, hi_ref, lo_ref, sel_ref, m_ref, l_ref, acc_ref, a_ref, *, top_k):
    qi = pl.program_id(1)
    tq = qt_ref.shape[2]
    n_keys = kw_ref.shape[1]
    kb_sz = LANE
    q0 = qi * tq
    nkb = jnp.minimum(n_keys, q0 + tq) // kb_sz
    negkey = _neg_inf_key()

    qlane = lax.broadcasted_iota(I32, (1, tq), 1)
    lim = ((q0 + qlane) // CHUNK + 1) * CHUNK
    krow = lax.broadcasted_iota(I32, (kb_sz, tq), 0)

    def kslice(kb):
        return pl.ds(pl.multiple_of(kb * kb_sz, kb_sz), kb_sz)

    sb = 2 * kb_sz
    srow = lax.broadcasted_iota(I32, (sb, tq), 0)

    def score_body(i, c):
        rows = pl.ds(pl.multiple_of(i * sb, sb), sb)
        kid = kw_ref[0, rows, :][:, :IDX_DIM].astype(BF16)
        sc = jnp.zeros((sb, tq), F32)
        for h in range(IDX_HEADS):
            s = jnp.dot(kid, qit_ref[0, h * IDX_DIM:(h + 1) * IDX_DIM, :], preferred_element_type=F32)
            sc = sc + jnp.maximum(s, 0.0) * kwt_ref[0, IDX_DIM + h:IDX_DIM + h + 1, :]
        sc = sc * ((IDX_DIM * IDX_HEADS) ** -0.5)
        key = jnp.where(i * sb + srow < lim, _sortable(sc), negkey)
        key_ref[rows, :] = key
        hi_ref[rows, :] = (key >> 16).astype(I16)
        lo_ref[rows, :] = ((key & 0xFFFF) - 0x8000).astype(I16)
        return c

    lax.fori_loop(0, nkb // 2, score_body, 0)

    def search16(ref):
        def bit_body(i, t_u):
            cand_u = t_u | jnp.left_shift(jnp.int32(1), 15 - i)
            cand = (cand_u - 0x8000).astype(I16)

            def body(j, a):
                ind = jnp.where(ref[pl.ds(pl.multiple_of(j * sb, sb), sb), :] >= cand,
                                jnp.ones((), I16), jnp.zeros((), I16))
                parts = [ind[16 * r:16 * (r + 1), :] for r in range(sb // 16)]
                while len(parts) > 1:
                    parts = [parts[r] + parts[r + 1] for r in range(0, len(parts), 2)]
                return a + parts[0]
            a = lax.fori_loop(0, nkb // 2, body, jnp.zeros((16, tq), I16))
            cnt = jnp.sum(a.astype(I32), axis=0, keepdims=True)
            return jnp.where(cnt >= top_k, cand_u, t_u)
        return lax.fori_loop(0, 16, bit_body, jnp.zeros((1, tq), I32))

    def count(pred_fn):
        def body(i, a):
            for u in range(2):
                kb = 2 * i + u
                ind = pred_fn(kb, key_ref[kslice(kb), :])
                a = a + jnp.sum(ind.reshape(kb_sz // 8, 8, tq), axis=0)
            return a
        a = lax.fori_loop(0, nkb // 2, body, jnp.zeros((8, tq), I32))
        return jnp.sum(a, axis=0, keepdims=True)

    t_hi = search16(hi_ref)
    t_hi16 = (t_hi - 0x8000).astype(I16)

    def lo_body(j, c):
        rows = pl.ds(pl.multiple_of(j * sb, sb), sb)
        hi = hi_ref[rows, :]
        lo_ref[rows, :] = jnp.where(hi == t_hi16, lo_ref[rows, :],
                                    jnp.where(hi > t_hi16, jnp.full((), 0x7FFF, I16), jnp.full((), -0x8000, I16)))
        return c

    lax.fori_loop(0, nkb // 2, lo_body, 0)
    t_s = (jnp.left_shift(t_hi, 16) | search16(lo_ref)) ^ INT_MIN

    def adm01(kb):
        return jnp.where(kb * kb_sz + krow < lim, 1.0, 0.0)

    def sel_body(kb, a):
        sel = jnp.where(key_ref[kslice(kb), :] >= t_s, adm01(kb), 0.0)
        sel_ref[kslice(kb), :] = sel
        return a + jnp.sum(sel.reshape(kb_sz // 8, 8, tq), axis=0)

    n_sel = jnp.sum(lax.fori_loop(0, nkb, sel_body, jnp.zeros((8, tq), F32)), axis=0, keepdims=True)

    @pl.when(jnp.max(n_sel) > top_k)
    def _():
        n_gt = count(lambda kb, key: jnp.where(key > t_s, 1, 0))
        need = (top_k - n_gt).astype(F32)
        r = lax.broadcasted_iota(I32, (kb_sz, kb_sz), 0)
        c = lax.broadcasted_iota(I32, (kb_sz, kb_sz), 1)
        ltri = jnp.where(c < r, 1.0, 0.0).astype(BF16)

        def tie_body(kb, carry):
            key = key_ref[kslice(kb), :]
            adm = adm01(kb)
            eq = jnp.where(key == t_s, adm, 0.0)
            rank = carry + jnp.dot(ltri, eq.astype(BF16), preferred_element_type=F32)
            keep = jnp.where(rank < need, eq, 0.0)
            sel_ref[kslice(kb), :] = jnp.where(key > t_s, adm, keep)
            return carry + jnp.sum(eq, axis=0, keepdims=True)

        lax.fori_loop(0, nkb, tie_body, jnp.zeros((1, tq), F32))

    m_ref[...] = jnp.full(m_ref.shape, NEG_INF, F32)
    l_ref[...] = jnp.zeros(l_ref.shape, F32)
    acc_ref[...] = jnp.zeros(acc_ref.shape, F32)
    c1 = (B_HD ** -0.5) * LOG2E

    def attend(k0, nk, bias2_fn):
        rows = pl.ds(pl.multiple_of(k0, LANE), nk)
        sel = sel_ref[rows, :] != 0.0
        ks = k_ref[0, rows, :]
        cols = []
        for h in range(B_HEADS):
            g = h // B_REP
            z = jnp.dot(ks[:, g * B_HD:(g + 1) * B_HD], qt_ref[0, h * B_HD:(h + 1) * B_HD, :],
                        preferred_element_type=F32)
            a = jnp.where(sel, z * c1 + bias2_fn(h), NEG_INF)
            a_ref[h, 0:nk, :] = a
            cols.append(jnp.max(a, axis=0, keepdims=True))
        m_old = m_ref[...]
        m_new = jnp.maximum(m_old, jnp.concatenate(cols, axis=0))
        alpha = jnp.exp2(m_old - m_new)
        m_ref[...] = m_new
        sums = []
        for h in range(B_HEADS):
            g = h // B_REP
            p = jnp.exp2(a_ref[h, 0:nk, :] - m_new[h:h + 1, :])
            sums.append(jnp.sum(p, axis=0, keepdims=True))
            hs = slice(h * B_HD, (h + 1) * B_HD)
            pv = jnp.dot(vt_ref[0, g * B_HD:(g + 1) * B_HD, rows], p.astype(BF16),
                         preferred_element_type=F32)
            acc_ref[hs, :] = alpha[h:h + 1, :] * acc_ref[hs, :] + pv
        l_ref[...] = alpha * l_ref[...] + jnp.concatenate(sums, axis=0)

    ab = near_ref.shape[2]
    n_far = jnp.maximum(q0 // ab - 1, 0)
    far_bias2 = far_ref[:, 0:1]

    def far_body(i, c):
        attend(i * ab, ab, lambda h: far_bias2[h:h + 1, :])
        return c

    lax.fori_loop(0, n_far, far_body, 0)

    def near_body(i, c):
        dd = i - q0 // ab + 1
        attend(i * ab, ab, lambda h: near_ref[dd, h])
        return c

    lax.fori_loop(n_far, nkb * kb_sz // ab, near_body, 0)

    _finish_heads(o_ref, x_ref, wout_ref, l_ref, acc_ref, B_HEADS)


def _dsa_attention_prompt(qt, qit, kwt, kw, k, vt, far, near, x, w_out):
    b, d, t = qt.shape
    tq = near.shape[3]
    top_k = min(TOPK_MAX, t // 4)
    ab = near.shape[2]
    assert t % tq == 0 and tq % CHUNK == 0 and tq % ab == 0 and ab % LANE == 0 and tq % (2 * LANE) == 0
    return pl.pallas_call(
        functools.partial(_dsa_prompt_kernel, top_k=top_k),
        grid=(b, t // tq),
        in_specs=[pl.BlockSpec((1, d, tq), lambda i, j: (i, 0, j)),
                  pl.BlockSpec((1, qit.shape[1], tq), lambda i, j: (i, 0, j)),
                  pl.BlockSpec((1, LANE, tq), lambda i, j: (i, 0, j)),
                  pl.BlockSpec((1, t, LANE), lambda i, j: (i, 0, 0)),
                  pl.BlockSpec((1, t, B_KV * B_HD), lambda i, j: (i, 0, 0)),
                  pl.BlockSpec((1, B_KV * B_HD, t), lambda i, j: (i, 0, 0)),
                  pl.BlockSpec(far.shape, lambda i, j: (0, 0)),
                  pl.BlockSpec(near.shape, lambda i, j: (0, 0, 0, 0)),
                  pl.BlockSpec((1, tq, d), lambda i, j: (i, j, 0)),
                  pl.BlockSpec((d, d), lambda i, j: (0, 0))],
        out_specs=pl.BlockSpec((1, tq, d), lambda i, j: (i, j, 0)),
        out_shape=jax.ShapeDtypeStruct((b, t, d), F32),
        scratch_shapes=[pltpu.VMEM((t, tq), I32), pltpu.VMEM((t, tq), I16), pltpu.VMEM((t, tq), I16),
                        pltpu.VMEM((t, tq), F32),
                        pltpu.VMEM((B_HEADS, tq), F32), pltpu.VMEM((B_HEADS, tq), F32),
                        pltpu.VMEM((d, tq), F32), pltpu.VMEM((B_HEADS, ab, tq), F32)],
        compiler_params=_cparams("parallel", "parallel"),
        name="dsa_attention",
    )(qt, qit, kwt, kw, k, vt, far, near, x, w_out)


def _dsa_cached_kernel(q_ref, qi_ref, wi_ref, rb_ref, kidx_ref, kp_ref, vp_ref, kn_ref, vn_ref, o_ref,
                       *, past, t_new, top_k):
    n_keys = past + t_new
    lp = kidx_ref.shape[1]
    negkey = _neg_inf_key()
    kpos = lax.broadcasted_iota(I32, (t_new, lp), 1)
    qpos = past + lax.broadcasted_iota(I32, (t_new, lp), 0)
    adm = kpos < (qpos // CHUNK + 1) * CHUNK

    s = _dot_nt(qi_ref[0], kidx_ref[0])
    w = jnp.maximum(s, 0.0) * wi_ref[0]
    sc = w[0:t_new]
    for h in range(1, IDX_HEADS):
        sc = sc + w[h * t_new:(h + 1) * t_new]
    sc = sc * ((IDX_DIM * IDX_HEADS) ** -0.5)
    key = jnp.where(adm, _sortable(sc), negkey)
    key = jnp.where(kpos < n_keys, key, INT_MIN)

    def bit_body(i, t_u):
        cand_u = t_u | jnp.left_shift(jnp.int32(1), 31 - i)
        cnt = jnp.sum(jnp.where(key >= (cand_u ^ INT_MIN), 1.0, 0.0), axis=1, keepdims=True)
        return jnp.where(cnt >= top_k, cand_u, t_u)

    t_s = lax.fori_loop(0, 32, bit_body, jnp.zeros((t_new, 1), I32)) ^ INT_MIN

    adm01 = jnp.where(adm, 1.0, 0.0)
    gt = jnp.where(key > t_s, adm01, 0.0)
    eq = jnp.where(key == t_s, adm01, 0.0)
    need = top_k - jnp.sum(jnp.where(key > t_s, 1.0, 0.0), axis=1, keepdims=True)
    r = lax.broadcasted_iota(I32, (LANE, LANE), 0)
    c = lax.broadcasted_iota(I32, (LANE, LANE), 1)
    utri = jnp.where(r < c, 1.0, 0.0).astype(BF16)
    carry = jnp.zeros((t_new, 1), F32)
    keeps = []
    for blk in range(lp // LANE):
        e = eq[:, blk * LANE:(blk + 1) * LANE]
        rank = carry + jnp.dot(e.astype(BF16), utri, preferred_element_type=F32)
        keeps.append(jnp.where(rank < need, e, 0.0))
        carry = carry + jnp.sum(e, axis=1, keepdims=True)
    sel = gt + jnp.concatenate(keeps, axis=1)

    near = max(past - MAX_DIST, 0) // LANE * LANE
    rel = (kpos - qpos)[:, near:]
    nb = N_BUCKETS // 2
    max_exact = nb // 2
    n = jnp.abs(rel)
    nf = jnp.maximum(n, 1).astype(F32)
    large = max_exact + (jnp.log(nf / max_exact) / math.log(MAX_DIST / max_exact)
                         * (nb - max_exact)).astype(I32)
    bucket = (rel > 0).astype(I32) * nb + jnp.where(n < max_exact, n, jnp.minimum(large, nb - 1))

    rows = B_REP * t_new
    sel_g = jnp.concatenate([sel] * B_REP, axis=0) != 0.0
    bucket_g = jnp.concatenate([bucket] * B_REP, axis=0)
    for g in range(B_KV):
        grp = lambda ref, n: ref[0, pl.ds(g, n, stride=B_KV), :]
        qg = q_ref[0, g * rows:(g + 1) * rows, :]
        rb = rb_ref[g * rows:(g + 1) * rows, :]
        bias_near = jnp.zeros((rows, lp - near), F32)
        for j in range(N_BUCKETS):
            bias_near = jnp.where(bucket_g == j, rb[:, j:j + 1], bias_near)
        bias = jnp.concatenate([jnp.broadcast_to(rb[:, nb - 1:nb], (rows, near)), bias_near], axis=1)
        zp = _dot_nt(qg, grp(kp_ref, past)) * (B_HD ** -0.5)
        zn = _dot_nt(qg, grp(kn_ref, t_new)) * (B_HD ** -0.5)
        ap = jnp.where(sel_g[:, :past], zp + bias[:, :past], NEG_INF)
        an = jnp.where(sel_g[:, past:n_keys], zn + bias[:, past:n_keys], NEG_INF)
        m = jnp.maximum(jnp.max(ap, axis=1, keepdims=True), jnp.max(an, axis=1, keepdims=True))
        pp, pn = jnp.exp(ap - m), jnp.exp(an - m)
        l = jnp.sum(pp, axis=1, keepdims=True) + jnp.sum(pn, axis=1, keepdims=True)
        o_ref[0, g * rows:(g + 1) * rows, :] = (_dot(pp, grp(vp_ref, past)) + _dot(pn, grp(vn_ref, t_new))) / l


def _dsa_attention_cached(q_rows, qi_rows, wi_col, rb_rows, kidx_all, k_past, v_past, k_new, v_new):
    b, rows, hd = q_rows.shape
    past, t_new = k_past.shape[1], k_new.shape[1]
    lp = kidx_all.shape[1]
    top_k = min(TOPK_MAX, (past + t_new) // 4)
    assert past % LANE == 0
    flat = lambda a: a.reshape(b, a.shape[1] * B_KV, B_HD)
    kv_spec = lambda n: pl.BlockSpec((1, n * B_KV, B_HD), lambda i: (i, 0, 0))
    return pl.pallas_call(
        functools.partial(_dsa_cached_kernel, past=past, t_new=t_new, top_k=top_k),
        grid=(b,),
        in_specs=[pl.BlockSpec((1, rows, hd), lambda i: (i, 0, 0)),
                  pl.BlockSpec((1,) + qi_rows.shape[1:], lambda i: (i, 0, 0)),
                  pl.BlockSpec((1,) + wi_col.shape[1:], lambda i: (i, 0, 0)),
                  pl.BlockSpec(rb_rows.shape, lambda i: (0, 0)),
                  pl.BlockSpec((1, lp, IDX_DIM), lambda i: (i, 0, 0)),
                  kv_spec(past), kv_spec(past), kv_spec(t_new), kv_spec(t_new)],
        out_specs=pl.BlockSpec((1, rows, hd), lambda i: (i, 0, 0)),
        out_shape=jax.ShapeDtypeStruct((b, rows, hd), F32),
        compiler_params=_cparams("parallel"),
        name="dsa_attention_cached",
    )(q_rows, qi_rows, wi_col, rb_rows, kidx_all, flat(k_past), flat(v_past), flat(k_new), flat(v_new))


def _cumsum_kernel(x_ref, o_ref):
    x = x_ref[0]
    n = x.shape[-1]
    lane = lax.broadcasted_iota(I32, x.shape, 1)
    s = 1
    while s < n:
        x = x + jnp.where(lane >= s, pltpu.roll(x, s, 1), 0.0)
        s *= 2
    o_ref[0] = x


def _cumsum_lanes(x):
    b, h, n = x.shape
    return pl.pallas_call(
        _cumsum_kernel,
        grid=(b,),
        in_specs=[pl.BlockSpec((1, h, n), lambda i: (i, 0, 0))],
        out_specs=pl.BlockSpec((1, h, n), lambda i: (i, 0, 0)),
        out_shape=jax.ShapeDtypeStruct((b, h, n), F32),
        compiler_params=_cparams("parallel"),
        name="logf_cumsum",
    )(x)


def _fox_init(m_ref, l_ref, acc_ref):
    m_ref[...] = jnp.full(m_ref.shape, NEG_INF, F32)
    l_ref[...] = jnp.zeros(l_ref.shape, F32)
    acc_ref[...] = jnp.zeros(acc_ref.shape, F32)


def _fox_tile(z_fn, pv_fn, cq, ck, mask, m_ref, l_ref, acc_ref, a_ref):
    c1 = (D_HD ** -0.5) * LOG2E
    cq2, ck2 = cq * LOG2E, ck * LOG2E
    cols = []
    for h in range(D_HEADS):
        a = z_fn(h) * c1 - ck2[:, h:h + 1]
        if mask is not None:
            a = jnp.where(mask, a, NEG_INF)
        a_ref[h] = a
        cols.append(jnp.max(a, axis=0, keepdims=True))
    m_old = m_ref[...]
    m_new = jnp.maximum(m_old, jnp.concatenate(cols, axis=0) + cq2)
    alpha = jnp.exp2(m_old - m_new)
    shift = m_new - cq2
    m_ref[...] = m_new
    sums = []
    for h in range(D_HEADS):
        p = jnp.exp2(a_ref[h] - shift[h:h + 1, :])
        sums.append(jnp.sum(p, axis=0, keepdims=True))
        hs = slice(h * D_HD, (h + 1) * D_HD)
        acc_ref[hs, :] = alpha[h:h + 1, :] * acc_ref[hs, :] + pv_fn(h, p.astype(BF16))
    l_ref[...] = alpha * l_ref[...] + jnp.concatenate(sums, axis=0)


def _hs(h):
    return slice(h * D_HD, (h + 1) * D_HD)


def _fox_prompt_kernel(qt_ref, k_ref, vt_ref, cq_ref, ck_ref, x_ref, wout_ref, o_ref,
                       m_ref, l_ref, acc_ref, a_ref):
    qi, step = pl.program_id(1), pl.program_id(2)
    tq, tk = qt_ref.shape[2], a_ref.shape[1]
    q0 = qi * tq
    ki = step - (pl.num_programs(2) - 1 - (q0 + tq - 1) // tk)
    k0 = ki * tk
    keys = pl.ds(pl.multiple_of(jnp.maximum(k0, 0), tk), tk)

    @pl.when(step == 0)
    def _():
        _fox_init(m_ref, l_ref, acc_ref)

    def run(masked):
        mask = None
        if masked:
            mask = (k0 + lax.broadcasted_iota(I32, (tk, tq), 0)) <= (q0 + lax.broadcasted_iota(I32, (tk, tq), 1))
        _fox_tile(lambda h: jnp.dot(k_ref[0, keys, _hs(h)], qt_ref[0, _hs(h), :], preferred_element_type=F32),
                  lambda h, p: jnp.dot(vt_ref[0, _hs(h), keys], p, preferred_element_type=F32),
                  cq_ref[0], ck_ref[0, keys, :], mask, m_ref, l_ref, acc_ref, a_ref)

    fully_visible = k0 + tk - 1 <= q0
    pl.when(jnp.logical_and(ki >= 0, fully_visible))(lambda: run(False))
    pl.when(jnp.logical_and(ki >= 0, jnp.logical_not(fully_visible)))(lambda: run(True))

    @pl.when(step == pl.num_programs(2) - 1)
    def _():
        _finish_heads(o_ref, x_ref, wout_ref, l_ref, acc_ref, D_HEADS)


def _fox_attention_prompt(qt, k, vt, cum_t, cum, x, w_out, tq, tk):
    b, d, t = qt.shape
    nq, nk = t // tq, t // tk
    return pl.pallas_call(
        _fox_prompt_kernel,
        grid=(b, nq, nk),
        in_specs=[pl.BlockSpec((1, d, tq), lambda i, j, kk: (i, 0, j)),
                  pl.BlockSpec((1, t, d), lambda i, j, kk: (i, 0, 0)),
                  pl.BlockSpec((1, d, t), lambda i, j, kk: (i, 0, 0)),
                  pl.BlockSpec((1, D_HEADS, tq), lambda i, j, kk: (i, 0, j)),
                  pl.BlockSpec((1, t, D_HEADS), lambda i, j, kk: (i, 0, 0)),
                  pl.BlockSpec((1, tq, d), lambda i, j, kk: (i, j, 0)),
                  pl.BlockSpec((d, d), lambda i, j, kk: (0, 0))],
        out_specs=pl.BlockSpec((1, tq, d), lambda i, j, kk: (i, j, 0)),
        out_shape=jax.ShapeDtypeStruct((b, t, d), F32),
        scratch_shapes=[pltpu.VMEM((D_HEADS, tq), F32), pltpu.VMEM((D_HEADS, tq), F32),
                        pltpu.VMEM((d, tq), F32), pltpu.VMEM((D_HEADS, tk, tq), F32)],
        compiler_params=_cparams("parallel", "parallel", "arbitrary"),
        name="fox_attention",
    )(qt, k, vt, cum_t, cum, x, w_out)


def _fox_cached_kernel(q_ref, kp_ref, vp_ref, kn_ref, vn_ref, cq_ref, ckp_ref, ckn_ref, o_ref,
                       m_ref, l_ref, acc_ref, *, t_new):
    ki = pl.program_id(1)
    n_past = pl.num_programs(1) - 1
    rows = q_ref.shape[1]
    c1 = (D_HD ** -0.5) * LOG2E

    @pl.when(ki == 0)
    def _():
        m_ref[...] = jnp.full(m_ref.shape, NEG_INF, F32)
        l_ref[...] = jnp.zeros(l_ref.shape, F32)
        acc_ref[...] = jnp.zeros(acc_ref.shape, F32)

    def tile(k2d, v2d, ck_row, causal):
        cols = k2d.shape[0]
        a = _dot_nt(q_ref[0], k2d) * c1 - ck_row * LOG2E
        r = lax.broadcasted_iota(I32, (rows, cols), 0)
        c = lax.broadcasted_iota(I32, (rows, cols), 1)
        ok = (c % D_HEADS) == (r // t_new)
        if causal:
            ok = jnp.logical_and(ok, (c // D_HEADS) <= (r % t_new))
        a = jnp.where(ok, a, NEG_INF)
        cq2 = cq_ref[0] * LOG2E
        m_old = m_ref[...]
        m_new = jnp.maximum(m_old, jnp.max(a, axis=1, keepdims=True) + cq2)
        alpha = jnp.exp2(m_old - m_new)
        p = jnp.exp2(a - (m_new - cq2))
        l_ref[...] = alpha * l_ref[...] + jnp.sum(p, axis=1, keepdims=True)
        acc_ref[...] = alpha * acc_ref[...] + _dot(p, v2d)
        m_ref[...] = m_new

    @pl.when(ki < n_past)
    def _():
        tk = kp_ref.shape[1]
        tile(kp_ref[0].reshape(tk * D_HEADS, D_HD), vp_ref[0].reshape(tk * D_HEADS, D_HD), ckp_ref[0], False)

    @pl.when(ki == n_past)
    def _():
        tile(kn_ref[0].reshape(t_new * D_HEADS, D_HD), vn_ref[0].reshape(t_new * D_HEADS, D_HD), ckn_ref[0], True)
        o_ref[0] = acc_ref[...] / l_ref[...]


def _fox_attention_cached(q_rows, k_past, v_past, k_new, v_new, cq_col, ck_past, ck_new, tk):
    b, rows, hd = q_rows.shape
    past, t_new = k_past.shape[1], k_new.shape[1]
    n_past = past // tk
    pidx = lambda i, kk: (i, jnp.minimum(kk, n_past - 1), 0, 0)
    return pl.pallas_call(
        functools.partial(_fox_cached_kernel, t_new=t_new),
        grid=(b, n_past + 1),
        in_specs=[pl.BlockSpec((1, rows, hd), lambda i, kk: (i, 0, 0)),
                  pl.BlockSpec((1, tk, D_HEADS, D_HD), pidx),
                  pl.BlockSpec((1, tk, D_HEADS, D_HD), pidx),
                  pl.BlockSpec((1, t_new, D_HEADS, D_HD), lambda i, kk: (i, 0, 0, 0)),
                  pl.BlockSpec((1, t_new, D_HEADS, D_HD), lambda i, kk: (i, 0, 0, 0)),
                  pl.BlockSpec((1, rows, 1), lambda i, kk: (i, 0, 0)),
                  pl.BlockSpec((1, 1, tk * D_HEADS), lambda i, kk: (i, 0, jnp.minimum(kk, n_past - 1))),
                  pl.BlockSpec((1, 1, t_new * D_HEADS), lambda i, kk: (i, 0, 0))],
        out_specs=pl.BlockSpec((1, rows, hd), lambda i, kk: (i, 0, 0)),
        out_shape=jax.ShapeDtypeStruct((b, rows, hd), F32),
        scratch_shapes=[pltpu.VMEM((rows, 1), F32), pltpu.VMEM((rows, 1), F32), pltpu.VMEM((rows, hd), F32)],
        compiler_params=_cparams("parallel", "arbitrary"),
        name="fox_attention_cached",
    )(q_rows, k_past, v_past, k_new, v_new, cq_col, ck_past, ck_new)


def _pad_rows(a, rows):
    if a.shape[1] == rows:
        return a
    return jnp.pad(a, ((0, 0), (0, rows - a.shape[1])) + ((0, 0),) * (a.ndim - 2))


def _round_up(n, m):
    return -(-n // m) * m


def _dsa_mixer(x, g, k_past, v_past, ki_past, w, w_out, rel_bias):
    b, t, d = x.shape
    past = k_past.shape[1]
    n_keys = past + t
    if past == 0:
        qt, k4, kb, v4, vt, qit, kw, kwt, ki = _proj(
            x, g, w, ((0, "t", BF16, None), (1, B_HD, F32, None), (1, "rows", BF16, None),
                      (2, B_HD, F32, None), (2, "t", BF16, None), (3, "t", BF16, None),
                      (4, "rows", F32, None), (4, "t", F32, None), (4, ("first", IDX_DIM), F32, None)),
            jnp.zeros((1, LANE), F32))
        far, near = _bias_tiles(rel_bias, 2 * LANE, 2 * LANE)
        y = _dsa_attention_prompt(qt, qit, kwt, kw, kb, vt, far, near, x, w_out)
        return (y, k4, v4, ki)
    q, k4, v4, qidx, kw = _proj(
        x, g, w, ((0, "rows", BF16, None), (1, B_HD, F32, None), (2, B_HD, F32, None),
                  (3, "rows", BF16, None), (4, "rows", F32, None)),
        jnp.zeros((1, LANE), F32))
    ki = kw[:, :, :IDX_DIM]
    to_rows = lambda a, nh: jnp.swapaxes(a.reshape(b, t, nh, -1), 1, 2).reshape(b, nh * t, -1)
    kidx_all = _pad_rows(jnp.concatenate([ki_past, ki], axis=1), _round_up(n_keys, LANE))
    o = _dsa_attention_cached(to_rows(q, B_HEADS), to_rows(qidx, IDX_HEADS),
                              to_rows(kw[:, :, IDX_DIM:IDX_DIM + IDX_HEADS], IDX_HEADS),
                              jnp.repeat(rel_bias.T, t, axis=0), kidx_all, k_past, v_past, k4, v4)
    o = jnp.swapaxes(o.reshape(b, B_HEADS, t, B_HD), 1, 2).reshape(b * t, d)
    y = _out_proj(x.reshape(b * t, d), o, w_out).reshape(b, t, d)
    return (y, k4, v4, ki)


def _fox_mixer(x, g, k_past, v_past, lf_past, w, b_f, w_out):
    b, t, d = x.shape
    past = k_past.shape[1]
    heads4 = ((1, D_HD, F32, None), (2, D_HD, F32, None), (3, ("first", D_HEADS), F32, "log_sigmoid"))
    if past == 0:
        tq = tk = min(4 * LANE, t)
        k4, v4, logf, qt, kb, vt = _proj(
            x, g, w, heads4 + ((0, "t", BF16, None), (1, "rows", BF16, None), (2, "t", BF16, None)), b_f)
        cum_t = _cumsum_lanes(jnp.swapaxes(logf, 1, 2))
        y = _fox_attention_prompt(qt, kb, vt, cum_t, jnp.swapaxes(cum_t, 1, 2), x, w_out, tq, tk)
        return (y, k4, v4, logf)
    else:
        tk = math.gcd(past, 4 * LANE)
        k4, v4, logf, q = _proj(x, g, w, heads4 + ((0, "rows", BF16, None),), b_f)
        lf_all = _pad_rows(jnp.concatenate([lf_past, logf], axis=1), _round_up(past + t, LANE))
        cum = jnp.swapaxes(_cumsum_lanes(jnp.swapaxes(lf_all, 1, 2)), 1, 2)[:, :past + t]
        ck = cum.reshape(b, 1, (past + t) * D_HEADS)
        to_rows = lambda a: jnp.swapaxes(a.reshape(b, t, D_HEADS, -1), 1, 2).reshape(b, D_HEADS * t, -1)
        o = _fox_attention_cached(to_rows(q), k_past, v_past, k4, v4, to_rows(cum[:, past:]),
                                  ck[:, :, :past * D_HEADS], ck[:, :, past * D_HEADS:], tk)
        o = jnp.swapaxes(o.reshape(b, D_HEADS, t, D_HD), 1, 2).reshape(b, t, d)
    y = _out_proj(x.reshape(b * t, d), o.reshape(b * t, d), w_out).reshape(b, t, d)
    return (y, k4, v4, logf)


def _run_group(x, pos0, a_st, b_k, b_v, b_ki, c_st, d_k, d_v, d_lf, mem_k, mem_v, prm):
    b, t, d = x.shape
    depth = prm["norm_mix"].shape[0]
    new = {n: [] for n in ("a", "bk", "bv", "bki", "c", "dk", "dv", "dlf")}
    for i in range(depth):
        kind, j = i % 4, i // 4
        g = prm["norm_mix"][i]
        if kind == 0:
            x, st = _conv_mixer(x, g, prm["a_w_in"][j], prm["a_conv"][j], a_st[j], prm["a_w_out"][j])
            new["a"].append(st)
        elif kind == 1:
            x, kk, vv, ki = _dsa_mixer(x, g, b_k[j], b_v[j], b_ki[j], prm["b_w"][j], prm["b_w_out"][j],
                                       prm["rel_bias"])
            new["bk"].append(kk); new["bv"].append(vv); new["bki"].append(ki)
        elif kind == 2:
            x, st = _pool_mixer(x, g, c_st[j], prm["c_w_group"][j], prm["c_scale"][j], pos0)
            new["c"].append(st)
        else:
            x, kk, vv, lf = _fox_mixer(x, g, d_k[j], d_v[j], d_lf[j], prm["d_w"][j], prm["d_b_f"][j],
                                       prm["d_w_out"][j])
            new["dk"].append(kk); new["dv"].append(vv); new["dlf"].append(lf)
        x = _xattn(x, prm["norm_xattn"], prm["xa_wq"], mem_k, mem_v, prm["xa_wo"], i)
        last = i == depth - 1
        x = _ffn(x.reshape(b * t, d), prm["norm_ffn"], prm["ffn_w1"], prm["ffn_w2"],
                 prm["final_norm"], i, last).reshape(b, t, d)
    return (x,) + tuple(jnp.stack(new[n]) for n in ("a", "bk", "bv", "bki", "c", "dk", "dv", "dlf"))


def kernel(x_prompt, x_sample, state_a_conv, cache_b_k, cache_b_v, cache_b_kidx, state_c_pool,
           cache_d_k, cache_d_v, cache_d_logf, cache_mem_k, cache_mem_v, mem_prompt,
           norm_mix, norm_xattn, norm_mem, norm_ffn, final_norm,
           a_w_in, a_conv, a_w_out, b_w_in, b_w_out, rel_bias, c_w_group, c_scale,
           d_w_in, d_b_f, d_w_out, xa_wq, xa_wkv, xa_wo, ffn_w1, ffn_w2):
    bp, t, d = x_prompt.shape
    depth = norm_mix.shape[0]
    n_b, n_d = b_w_in.shape[0], d_w_in.shape[0]
    bf = lambda w: w.astype(BF16)

    def split_cols(w, widths):
        out, c = [], 0
        for wd in widths:
            piece = w[:, c:c + wd]
            c += wd
            if wd % LANE:
                piece = jnp.pad(piece, ((0, 0), (0, _round_up(wd, LANE) - wd)))
            out.append(bf(piece))
        assert c == w.shape[1]
        return out

    b_q, b_kvw = B_HEADS * B_HD, B_KV * B_HD
    b_w = [split_cols(b_w_in[j], (b_q, b_kvw, b_kvw, IDX_HEADS * IDX_DIM, IDX_DIM + IDX_HEADS))
           for j in range(n_b)]
    d_w = [split_cols(d_w_in[j], (d, d, d, D_HEADS)) for j in range(n_d)]
    d_bf = [jnp.pad(d_b_f[j], (0, LANE - D_HEADS)).reshape(1, LANE) for j in range(n_d)]

    prm = {"norm_mix": norm_mix, "norm_xattn": norm_xattn, "norm_ffn": norm_ffn, "final_norm": final_norm,
           "a_w_in": bf(a_w_in), "a_conv": a_conv, "a_w_out": bf(a_w_out),
           "b_w": b_w, "b_w_out": bf(b_w_out), "rel_bias": rel_bias,
           "c_w_group": bf(c_w_group), "c_scale": c_scale,
           "d_w": d_w, "d_b_f": d_bf, "d_w_out": bf(d_w_out),
           "xa_wq": bf(xa_wq), "xa_wo": bf(xa_wo), "ffn_w1": bf(ffn_w1), "ffn_w2": bf(ffn_w2)}

    n_mem = mem_prompt.shape[1]
    mk, mv, mk_rows, mv_rows = _memory_kv(mem_prompt, norm_mem, bf(xa_wkv))

    n_a, n_c = a_w_in.shape[0], c_w_group.shape[0]
    z = lambda *s: jnp.zeros(s, F32)
    gp = _run_group(x_prompt, 0,
                    z(n_a, bp, CONV_W - 1, d),
                    z(n_b, bp, 0, B_KV, B_HD), z(n_b, bp, 0, B_KV, B_HD), z(n_b, bp, 0, IDX_DIM),
                    z(n_c, bp, POOL_STATE, d),
                    z(n_d, bp, 0, D_HEADS, D_HD), z(n_d, bp, 0, D_HEADS, D_HD), z(n_d, bp, 0, D_HEADS),
                    mk_rows, mv_rows, prm)

    bs = x_sample.shape[0]
    past_len = cache_b_k.shape[2]
    gs = _run_group(x_sample, past_len, state_a_conv, cache_b_k, cache_b_v, cache_b_kidx, state_c_pool,
                    cache_d_k, cache_d_v, cache_d_logf, cache_mem_k, cache_mem_v, prm)

    (y_p, a_p, bk_p, bv_p, bki_p, c_p, dk_p, dv_p, dlf_p) = gp
    (y_s, a_s, bk_s, bv_s, bki_s, c_s, dk_s, dv_s, dlf_s) = gs
    return (y_p, y_s, a_p, a_s, bk_p, bv_p, bki_p, bk_s, bv_s, bki_s, c_p, c_s,
            dk_p, dv_p, dlf_p, dk_s, dv_s, dlf_s, mk, mv)
```

```python
import functools
import math

import jax
import jax.numpy as jnp
from jax import lax
from jax.experimental import pallas as pl
from jax.experimental.pallas import tpu as pltpu

F32 = jnp.float32
BF16 = jnp.bfloat16
I32 = jnp.int32
I16 = jnp.int16

EPS = 1e-6
NEG_INF = -1e30
LOG2E = math.log2(math.e)
CHUNK = 64
LANE = 128
VMEM_LIMIT = 48 * 1024 * 1024

CONV_W = 3
POOL_WINDOWS = (2, 4, 8, 16)
POOL_STATE = max(POOL_WINDOWS) - 1
B_HEADS, B_KV, B_HD = 8, 2, 128
B_REP = B_HEADS // B_KV
IDX_HEADS, IDX_DIM = 8, 64
TOPK_MAX = 256
N_BUCKETS, MAX_DIST = 32, 128
D_HEADS, D_HD = 8, 128
MEM_HEADS = 4
INT_MIN = -2147483648
BIAS_CENTER = 2 * LANE


def _cparams(*sem):
    return pltpu.CompilerParams(dimension_semantics=sem, vmem_limit_bytes=VMEM_LIMIT)


def _dot(a, b):
    return jnp.dot(a.astype(BF16), b.astype(BF16), preferred_element_type=F32)


def _dot_nt(a, b):
    return lax.dot_general(a.astype(BF16), b.astype(BF16), (((1,), (1,)), ((), ())),
                           preferred_element_type=F32)


def _dot_tn(a, b):
    return lax.dot_general(a.astype(BF16), b.astype(BF16), (((0,), (0,)), ((), ())),
                           preferred_element_type=F32)


def _rms(x, g):
    return x * lax.rsqrt(jnp.mean(x * x, axis=-1, keepdims=True) + EPS) * g


def _finish_heads(o_ref, x_ref, w_ref, l_ref, acc_ref, n_heads):
    hd = acc_ref.shape[0] // n_heads
    inv_l = 1.0 / l_ref[...]
    heads_t = jnp.concatenate([acc_ref[h * hd:(h + 1) * hd, :] * inv_l[h:h + 1, :] for h in range(n_heads)],
                              axis=0)
    o_ref[0] = x_ref[0] + _dot_tn(heads_t, w_ref[...])


def _row_tile(n, cap):
    t = min(n, cap)
    assert n % t == 0
    return t


def _memkv_kernel(mem_ref, g_ref, w_ref, k_ref, v_ref, kb_ref, vb_ref):
    m = mem_ref[0]
    mn = m * lax.rsqrt(jnp.mean(m * m, axis=-1, keepdims=True) + EPS)
    h = (mn * g_ref[0]).astype(BF16)
    d = m.shape[-1]
    hd = d // MEM_HEADS
    k = jnp.dot(h, w_ref[0, :, :d], preferred_element_type=F32)
    v = jnp.dot(h, w_ref[0, :, d:], preferred_element_type=F32)
    kb_ref[0, 0] = k.astype(BF16)
    vb_ref[0, 0] = v.astype(BF16)
    k_ref[0, 0] = pltpu.einshape("m(hd)->mhd", k, d=hd)
    v_ref[0, 0] = pltpu.einshape("m(hd)->mhd", v, d=hd)


def _memory_kv(mem, g_mem, w_kv):
    depth, d = g_mem.shape
    b, nm, _ = mem.shape
    hd = d // MEM_HEADS
    out = jax.ShapeDtypeStruct((depth, b, nm, MEM_HEADS, hd), F32)
    out_b = jax.ShapeDtypeStruct((depth, b, nm, d), BF16)
    heads_spec = pl.BlockSpec((1, 1, nm, MEM_HEADS, hd), lambda l, i: (l, i, 0, 0, 0))
    rows_spec = pl.BlockSpec((1, 1, nm, d), lambda l, i: (l, i, 0, 0))
    return pl.pallas_call(
        _memkv_kernel,
        grid=(depth, b),
        in_specs=[pl.BlockSpec((1, nm, d), lambda l, i: (i, 0, 0)),
                  pl.BlockSpec((1, 1, d), lambda l, i: (l, 0, 0)),
                  pl.BlockSpec((1, d, 2 * d), lambda l, i: (l, 0, 0))],
        out_specs=[heads_spec, heads_spec, rows_spec, rows_spec],
        out_shape=[out, out, out_b, out_b],
        compiler_params=_cparams("parallel", "parallel"),
        name="memory_kv",
    )(mem, g_mem.reshape(depth, 1, d), w_kv)


def _ffn_kernel(x_ref, g_ref, w1_ref, w2_ref, gf_ref, o_ref, h_ref, acc_ref, *, final_norm):
    j = pl.program_id(1)

    @pl.when(j == 0)
    def _():
        h_ref[...] = _rms(x_ref[...], g_ref[...]).astype(BF16)
        acc_ref[...] = jnp.zeros_like(acc_ref)

    u = jnp.maximum(jnp.dot(h_ref[...], w1_ref[...], preferred_element_type=F32), 0.0)
    acc_ref[...] += jnp.dot((u * u).astype(BF16), w2_ref[...], preferred_element_type=F32)

    @pl.when(j == pl.num_programs(1) - 1)
    def _():
        y = x_ref[...] + acc_ref[...]
        o_ref[...] = _rms(y, gf_ref[...]) if final_norm else y


def _ffn(x, g, w1, w2, gf, layer, final_norm):
    n, d = x.shape
    f = w1.shape[2]
    tm = _row_tile(n, 1024)
    tf = 1024
    return pl.pallas_call(
        functools.partial(_ffn_kernel, final_norm=final_norm),
        grid=(n // tm, f // tf),
        in_specs=[pl.BlockSpec((tm, d), lambda i, j: (i, 0)),
                  pl.BlockSpec((None, 1, d), lambda i, j: (layer, 0, 0)),
                  pl.BlockSpec((None, d, tf), lambda i, j: (layer, 0, j)),
                  pl.BlockSpec((None, tf, d), lambda i, j: (layer, j, 0)),
                  pl.BlockSpec((1, d), lambda i, j: (0, 0))],
        out_specs=pl.BlockSpec((tm, d), lambda i, j: (i, 0)),
        out_shape=jax.ShapeDtypeStruct((n, d), F32),
        scratch_shapes=[pltpu.VMEM((tm, d), BF16), pltpu.VMEM((tm, d), F32)],
        compiler_params=_cparams("parallel", "arbitrary"),
        name="ffn",
    )(x, g.reshape(-1, 1, d), w1, w2, gf.reshape(1, d))


def _xattn_kernel(x_ref, g_ref, wq_ref, mk_ref, mv_ref, wo_ref, o_ref):
    x = x_ref[0]
    d = x.shape[-1]
    hd = d // MEM_HEADS
    h = _rms(x, g_ref[...]).astype(BF16)
    q = jnp.dot(h, wq_ref[...], preferred_element_type=F32)
    outs = []
    if len(mk_ref.shape) == 3:
        mk = pltpu.einshape("mhd->m(hd)", mk_ref[...]).astype(BF16)
        mv = pltpu.einshape("mhd->m(hd)", mv_ref[...]).astype(BF16)
    else:
        mk, mv = mk_ref[...], mv_ref[...]
    for hh in range(MEM_HEADS):
        sl = slice(hh * hd, (hh + 1) * hd)
        kh, vh = mk[:, sl], mv[:, sl]
        s = _dot_nt(q[:, sl], kh) * (hd ** -0.5)
        m = jnp.max(s, axis=-1, keepdims=True)
        p = jnp.exp(s - m)
        l = jnp.sum(p, axis=-1, keepdims=True)
        outs.append(_dot(p, vh) / l)
    o = jnp.concatenate(outs, axis=-1)
    o_ref[0] = x + _dot(o, wo_ref[...])


def _xattn(x, g, wq, mk, mv, wo, layer):
    b, t, d = x.shape
    tm = _row_tile(t, 1024)
    kv_spec = pl.BlockSpec((None, None) + mk.shape[2:], lambda i, j: (layer, i) + (0,) * (mk.ndim - 2))
    return pl.pallas_call(
        _xattn_kernel,
        grid=(b, t // tm),
        in_specs=[pl.BlockSpec((1, tm, d), lambda i, j: (i, j, 0)),
                  pl.BlockSpec((None, 1, d), lambda i, j: (layer, 0, 0)),
                  pl.BlockSpec((None, d, d), lambda i, j: (layer, 0, 0)),
                  kv_spec, kv_spec,
                  pl.BlockSpec((None, d, d), lambda i, j: (layer, 0, 0))],
        out_specs=pl.BlockSpec((1, tm, d), lambda i, j: (i, j, 0)),
        out_shape=jax.ShapeDtypeStruct((b, t, d), F32),
        compiler_params=_cparams("parallel", "parallel"),
        name="xattn",
    )(x, g.reshape(-1, 1, d), wq, mk, mv, wo)


def _conv_kernel(x_ref, g_ref, win_ref, wc_ref, st_ref, wout_ref, o_ref, nst_ref, z_ref):
    t = pl.program_id(1)
    x = x_ref[0]
    tm, d = x.shape
    pad = 8

    @pl.when(t == 0)
    def _():
        z_ref[pad - 2:pad, :] = st_ref[0]

    h = _rms(x, g_ref[...]).astype(BF16)
    bg = jnp.dot(h, win_ref[:, 0:d], preferred_element_type=F32)
    cg = jnp.dot(h, win_ref[:, d:2 * d], preferred_element_type=F32)
    u = jnp.dot(h, win_ref[:, 2 * d:3 * d], preferred_element_type=F32)
    z = cg * u
    z_ref[pad:pad + tm, :] = z
    conv = (z_ref[pad - 2:pad - 2 + tm, :] * wc_ref[0:1, :]
            + z_ref[pad - 1:pad - 1 + tm, :] * wc_ref[1:2, :]
            + z * wc_ref[2:3, :])
    o_ref[0] = x + _dot(bg * conv, wout_ref[...])
    last = z_ref[pad + tm - 2:pad + tm, :]
    z_ref[pad - 2:pad, :] = last

    @pl.when(t == pl.num_programs(1) - 1)
    def _():
        nst_ref[0] = last


def _conv_mixer(x, g, w_in, w_conv, state, w_out):
    b, t, d = x.shape
    tm = _row_tile(t, 512)
    return pl.pallas_call(
        _conv_kernel,
        grid=(b, t // tm),
        in_specs=[pl.BlockSpec((1, tm, d), lambda i, j: (i, j, 0)),
                  pl.BlockSpec((1, d), lambda i, j: (0, 0)),
                  pl.BlockSpec((d, 3 * d), lambda i, j: (0, 0)),
                  pl.BlockSpec((CONV_W, d), lambda i, j: (0, 0)),
                  pl.BlockSpec((1, CONV_W - 1, d), lambda i, j: (i, 0, 0)),
                  pl.BlockSpec((d, d), lambda i, j: (0, 0))],
        out_specs=[pl.BlockSpec((1, tm, d), lambda i, j: (i, j, 0)),
                   pl.BlockSpec((1, CONV_W - 1, d), lambda i, j: (i, 0, 0))],
        out_shape=[jax.ShapeDtypeStruct((b, t, d), F32),
                   jax.ShapeDtypeStruct((b, CONV_W - 1, d), F32)],
        scratch_shapes=[pltpu.VMEM((tm + 8, d), F32)],
        compiler_params=_cparams("parallel", "arbitrary"),
        name="conv_mixer",
    )(x, g.reshape(1, d), w_in, w_conv, state, w_out)


def _pool_kernel(x_ref, g_ref, st_ref, wg_ref, sc_ref, o_ref, nst_ref, h_ref, *s_refs, pos0):
    t = pl.program_id(1)
    x = x_ref[0]
    tm, d = x.shape
    n_lv = len(POOL_WINDOWS)
    gw = d // n_lv
    base = 2 * (POOL_STATE + 1)
    lead = base - POOL_STATE
    end = base + tm

    @pl.when(t == 0)
    def _():
        h_ref[0:lead, :] = jnp.zeros((lead, d), F32)
        h_ref[lead:base, :] = st_ref[0]

    h = _rms(x, g_ref[...])
    h_ref[base:end, :] = h
    pos = pos0 + t * tm + lax.broadcasted_iota(I32, (tm, gw), 0)
    ys = []
    prev, c_prev = h_ref, 0
    for lv in range(1, n_lv + 1):
        w, shift, start = POOL_WINDOWS[lv - 1], 2 ** (lv - 1), 8 * lv
        c0 = (lv - 1) * gw
        cols = slice(c0 - c_prev, d - c_prev)
        cur = prev[start:end, cols] + prev[start - shift:end - shift, cols]
        if lv < n_lv:
            s_refs[lv - 1][start:end, :] = cur[:, gw:]
        win = cur[base - start:, :gw]
        count = jnp.minimum(w, pos + 1).astype(F32)
        dlt = win / count - h[:, c0:c0 + gw]
        ys.append(_dot(dlt, wg_ref[lv - 1]))
        if lv < n_lv:
            prev, c_prev = s_refs[lv - 1], c0 + gw
    y = jnp.concatenate(ys, axis=-1) * sc_ref[...]
    o_ref[0] = x + y
    last = h_ref[end - POOL_STATE:end, :]
    h_ref[lead:base, :] = last

    @pl.when(t == pl.num_programs(1) - 1)
    def _():
        nst_ref[0] = last


def _pool_mixer(x, g, state, w_group, scale, pos0):
    b, t, d = x.shape
    ng, gw, _ = w_group.shape
    tm = _row_tile(t, 512)
    assert POOL_WINDOWS == tuple(2 ** (lv + 1) for lv in range(ng)) and tm >= POOL_STATE
    rows = tm + 2 * (POOL_STATE + 1)
    return pl.pallas_call(
        functools.partial(_pool_kernel, pos0=pos0),
        grid=(b, t // tm),
        in_specs=[pl.BlockSpec((1, tm, d), lambda i, j: (i, j, 0)),
                  pl.BlockSpec((1, d), lambda i, j: (0, 0)),
                  pl.BlockSpec((1, POOL_STATE, d), lambda i, j: (i, 0, 0)),
                  pl.BlockSpec((ng, gw, gw), lambda i, j: (0, 0, 0)),
                  pl.BlockSpec((1, d), lambda i, j: (0, 0))],
        out_specs=[pl.BlockSpec((1, tm, d), lambda i, j: (i, j, 0)),
                   pl.BlockSpec((1, POOL_STATE, d), lambda i, j: (i, 0, 0))],
        out_shape=[jax.ShapeDtypeStruct((b, t, d), F32),
                   jax.ShapeDtypeStruct((b, POOL_STATE, d), F32)],
        scratch_shapes=[pltpu.VMEM((rows, d - lv * gw), F32) for lv in range(ng)],
        compiler_params=_cparams("parallel", "arbitrary"),
        name="pool_mixer",
    )(x, g.reshape(1, d), state, w_group, scale.reshape(1, d))


def _proj_kernel(*refs, n_w, outs):
    x_ref, g_ref = refs[0], refs[1]
    w_refs = refs[2:2 + n_w]
    e_ref = refs[2 + n_w]
    o_refs = refs[3 + n_w:]
    h = _rms(x_ref[0], g_ref[...]).astype(BF16)
    ys = {}
    for (wi, mode, _, ep), o_ref in zip(outs, o_refs):
        if wi not in ys:
            ys[wi] = jnp.dot(h, w_refs[wi][...], preferred_element_type=F32)
        y = ys[wi]
        if ep == "log_sigmoid":
            u = -(y + e_ref[...])
            y = -(jnp.maximum(u, 0.0) + jnp.log1p(jnp.exp(-jnp.abs(u))))
        if mode == "rows":
            o_ref[0] = y.astype(o_ref.dtype)
        elif isinstance(mode, tuple) and mode[0] == "first":
            o_ref[0] = y[:, :mode[1]].astype(o_ref.dtype)
        elif isinstance(mode, tuple):
            o_ref[0] = jnp.transpose(y)[:mode[1], :].astype(o_ref.dtype)
        elif mode == "t":
            o_ref[0] = jnp.transpose(y).astype(o_ref.dtype)
        else:
            o_ref[0] = pltpu.einshape("m(hd)->mhd", y.astype(o_ref.dtype), d=mode)


def _proj(x, g, ws, outs, extra):
    b, t, d = x.shape
    tm = _row_tile(t, 512)
    in_specs = [pl.BlockSpec((1, tm, d), lambda i, j: (i, j, 0)), pl.BlockSpec((1, d), lambda i, j: (0, 0))]
    in_specs += [pl.BlockSpec(w.shape, lambda i, j: (0, 0)) for w in ws]
    in_specs += [pl.BlockSpec(extra.shape, lambda i, j: (0, 0))]
    out_specs, out_shape = [], []
    for wi, mode, dt, _ in outs:
        n = ws[wi].shape[1]
        if isinstance(mode, tuple):
            mode, n = ("rows" if mode[0] == "first" else "t"), mode[1]
        if mode == "rows":
            out_specs.append(pl.BlockSpec((1, tm, n), lambda i, j: (i, j, 0)))
            out_shape.append(jax.ShapeDtypeStruct((b, t, n), dt))
        elif mode == "t":
            out_specs.append(pl.BlockSpec((1, n, tm), lambda i, j: (i, 0, j)))
            out_shape.append(jax.ShapeDtypeStruct((b, n, t), dt))
        else:
            out_specs.append(pl.BlockSpec((1, tm, n // mode, mode), lambda i, j: (i, j, 0, 0)))
            out_shape.append(jax.ShapeDtypeStruct((b, t, n // mode, mode), dt))
    return pl.pallas_call(
        functools.partial(_proj_kernel, n_w=len(ws), outs=tuple(outs)),
        grid=(b, t // tm),
        in_specs=in_specs,
        out_specs=out_specs,
        out_shape=out_shape,
        compiler_params=_cparams("parallel", "parallel"),
        name="norm_proj",
    )(x, g.reshape(1, d), *ws, extra)


def _outproj_kernel(x_ref, a_ref, w_ref, o_ref):
    o_ref[...] = x_ref[...] + _dot(a_ref[...], w_ref[...])


def _out_proj(x, a, w):
    n, d = x.shape
    tm = _row_tile(n, 512)
    return pl.pallas_call(
        _outproj_kernel,
        grid=(n // tm,),
        in_specs=[pl.BlockSpec((tm, d), lambda i: (i, 0)),
                  pl.BlockSpec((tm, d), lambda i: (i, 0)),
                  pl.BlockSpec((d, d), lambda i: (0, 0))],
        out_specs=pl.BlockSpec((tm, d), lambda i: (i, 0)),
        out_shape=jax.ShapeDtypeStruct((n, d), F32),
        compiler_params=_cparams("parallel"),
        name="out_proj",
    )(x, a, w)


def _bias_table_kernel(rbt_ref, o_ref):
    width = o_ref.shape[-1]
    rel = BIAS_CENTER - lax.broadcasted_iota(I32, (1, width), 1)
    nb = N_BUCKETS // 2
    max_exact = nb // 2
    ret = (rel > 0).astype(I32) * nb
    n = jnp.abs(rel)
    nf = jnp.maximum(n, 1).astype(F32)
    large = max_exact + (jnp.log(nf / max_exact) / math.log(MAX_DIST / max_exact)
                         * (nb - max_exact)).astype(I32)
    large = jnp.minimum(large, nb - 1)
    bucket = ret + jnp.where(n < max_exact, n, large)
    acc = jnp.zeros(o_ref.shape, F32)
    for j in range(N_BUCKETS):
        acc = jnp.where(bucket == j, rbt_ref[:, j:j + 1], acc)
    o_ref[...] = acc * LOG2E


def _bias_tiles_kernel(rbt_ref, far_ref, near_ref, tab_ref):
    n_d, nh, kb, tq = near_ref.shape
    _bias_table_kernel(rbt_ref, tab_ref)
    far_ref[...] = jnp.broadcast_to(tab_ref[:, BIAS_CENTER + MAX_DIST:BIAS_CENTER + MAX_DIST + 1], far_ref.shape)
    for dd in range(n_d):
        s0 = BIAS_CENTER - (dd - 1) * kb - kb
        for h in range(nh):
            rows = jnp.broadcast_to(tab_ref[h:h + 1, s0:s0 + tq + kb], (kb, tq + kb))
            near_ref[dd, h] = pltpu.roll(rows, 0, 1, stride=1, stride_axis=0)[:, kb:]


def _bias_tiles(rel_bias, kb, tq):
    nh = rel_bias.shape[1]
    n_d = tq // kb + 1
    width = BIAS_CENTER + kb + tq + kb
    assert kb % LANE == 0 and tq % kb == 0 and kb >= MAX_DIST and BIAS_CENTER >= tq
    return pl.pallas_call(
        _bias_tiles_kernel,
        out_shape=[jax.ShapeDtypeStruct((nh, LANE), F32), jax.ShapeDtypeStruct((n_d, nh, kb, tq), F32)],
        scratch_shapes=[pltpu.VMEM((nh, width), F32)],
        name="bias_tiles",
    )(rel_bias.T)


def _sortable(x):
    x = jnp.where(x == 0.0, 0.0, x)
    bits = lax.bitcast_convert_type(x, I32)
    return jnp.where(bits < 0, bits ^ 0x7FFFFFFF, bits)


def _neg_inf_key():
    import numpy as np
    b = int(np.float32(NEG_INF).view(np.int32))
    return b ^ 0x7FFFFFFF


def _dsa_prompt_kernel(qt_ref, qit_ref, kwt_ref, kw_ref, k_ref, vt_ref, far_ref, near_ref, x_ref, wout_ref, o_ref,
                       key_ref, hi_ref, lo_ref, sel_ref, m_ref, l_ref, acc_ref, a_ref, *, top_k):
    qi = pl.program_id(1)
    tq = qt_ref.shape[2]
    n_keys = kw_ref.shape[1]
    kb_sz = LANE
    q0 = qi * tq
    nkb = jnp.minimum(n_keys, q0 + tq) // kb_sz
    negkey = _neg_inf_key()

    qlane = lax.broadcasted_iota(I32, (1, tq), 1)
    lim = ((q0 + qlane) // CHUNK + 1) * CHUNK
    krow = lax.broadcasted_iota(I32, (kb_sz, tq), 0)

    def kslice(kb):
        return pl.ds(pl.multiple_of(kb * kb_sz, kb_sz), kb_sz)

    sb = 2 * kb_sz
    srow = lax.broadcasted_iota(I32, (sb, tq), 0)

    def score_body(i, c):
        rows = pl.ds(pl.multiple_of(i * sb, sb), sb)
        kid = kw_ref[0, rows, :][:, :IDX_DIM].astype(BF16)
        sc = jnp.zeros((sb, tq), F32)
        for h in range(IDX_HEADS):
            s = jnp.dot(kid, qit_ref[0, h * IDX_DIM:(h + 1) * IDX_DIM, :], preferred_element_type=F32)
            sc = sc + jnp.maximum(s, 0.0) * kwt_ref[0, IDX_DIM + h:IDX_DIM + h + 1, :]
        sc = sc * ((IDX_DIM * IDX_HEADS) ** -0.5)
        key = jnp.where(i * sb + srow < lim, _sortable(sc), negkey)
        key_ref[rows, :] = key
        hi_ref[rows, :] = (key >> 16).astype(I16)
        lo_ref[rows, :] = ((key & 0xFFFF) - 0x8000).astype(I16)
        return c

    lax.fori_loop(0, nkb // 2, score_body, 0)

    def search16(ref):
        def bit_body(i, t_u):
            cand_u = t_u | jnp.left_shift(jnp.int32(1), 15 - i)
            cand = (cand_u - 0x8000).astype(I16)

            def body(j, a):
                ind = jnp.where(ref[pl.ds(pl.multiple_of(j * sb, sb), sb), :] >= cand,
                                jnp.ones((), I16), jnp.zeros((), I16))
                parts = [ind[16 * r:16 * (r + 1), :] for r in range(sb // 16)]
                while len(parts) > 1:
                    parts = [parts[r] + parts[r + 1] for r in range(0, len(parts), 2)]
                return a + parts[0]
            a = lax.fori_loop(0, nkb // 2, body, jnp.zeros((16, tq), I16))
            cnt = jnp.sum(a.astype(I32), axis=0, keepdims=True)
            return jnp.where(cnt >= top_k, cand_u, t_u)
        return lax.fori_loop(0, 16, bit_body, jnp.zeros((1, tq), I32))

    def count(pred_fn):
        def body(i, a):
            for u in range(2):
                kb = 2 * i + u
                ind = pred_fn(kb, key_ref[kslice(kb), :])
                a = a + jnp.sum(ind.reshape(kb_sz // 8, 8, tq), axis=0)
            return a
        a = lax.fori_loop(0, nkb // 2, body, jnp.zeros((8, tq), I32))
        return jnp.sum(a, axis=0, keepdims=True)

    t_hi = search16(hi_ref)
    t_hi16 = (t_hi - 0x8000).astype(I16)

    def lo_body(j, c):
        rows = pl.ds(pl.multiple_of(j * sb, sb), sb)
        hi = hi_ref[rows, :]
        lo_ref[rows, :] = jnp.where(hi == t_hi16, lo_ref[rows, :],
                                    jnp.where(hi > t_hi16, jnp.full((), 0x7FFF, I16), jnp.full((), -0x8000, I16)))
        return c

    lax.fori_loop(0, nkb // 2, lo_body, 0)
    t_s = (jnp.left_shift(t_hi, 16) | search16(lo_ref)) ^ INT_MIN

    def adm01(kb):
        return jnp.where(kb * kb_sz + krow < lim, 1.0, 0.0)

    def sel_body(kb, a):
        sel = jnp.where(key_ref[kslice(kb), :] >= t_s, adm01(kb), 0.0)
        sel_ref[kslice(kb), :] = sel
        return a + jnp.sum(sel.reshape(kb_sz // 8, 8, tq), axis=0)

    n_sel = jnp.sum(lax.fori_loop(0, nkb, sel_body, jnp.zeros((8, tq), F32)), axis=0, keepdims=True)

    @pl.when(jnp.max(n_sel) > top_k)
    def _():
        n_gt = count(lambda kb, key: jnp.where(key > t_s, 1, 0))
        need = (top_k - n_gt).astype(F32)
        r = lax.broadcasted_iota(I32, (kb_sz, kb_sz), 0)
        c = lax.broadcasted_iota(I32, (kb_sz, kb_sz), 1)
        ltri = jnp.where(c < r, 1.0, 0.0).astype(BF16)

        def tie_body(kb, carry):
            key = key_ref[kslice(kb), :]
            adm = adm01(kb)
            eq = jnp.where(key == t_s, adm, 0.0)
            rank = carry + jnp.dot(ltri, eq.astype(BF16), preferred_element_type=F32)
            keep = jnp.where(rank < need, eq, 0.0)
            sel_ref[kslice(kb), :] = jnp.where(key > t_s, adm, keep)
            return carry + jnp.sum(eq, axis=0, keepdims=True)

        lax.fori_loop(0, nkb, tie_body, jnp.zeros((1, tq), F32))

    m_ref[...] = jnp.full(m_ref.shape, NEG_INF, F32)
    l_ref[...] = jnp.zeros(l_ref.shape, F32)
    acc_ref[...] = jnp.zeros(acc_ref.shape, F32)
    c1 = (B_HD ** -0.5) * LOG2E

    def attend(k0, nk, bias2_fn):
        rows = pl.ds(pl.multiple_of(k0, LANE), nk)
        sel = sel_ref[rows, :] != 0.0
        ks = k_ref[0, rows, :]
        cols = []
        for h in range(B_HEADS):
            g = h // B_REP
            z = jnp.dot(ks[:, g * B_HD:(g + 1) * B_HD], qt_ref[0, h * B_HD:(h + 1) * B_HD, :],
                        preferred_element_type=F32)
            a = jnp.where(sel, z * c1 + bias2_fn(h), NEG_INF)
            a_ref[h, 0:nk, :] = a
            cols.append(jnp.max(a, axis=0, keepdims=True))
        m_old = m_ref[...]
        m_new = jnp.maximum(m_old, jnp.concatenate(cols, axis=0))
        alpha = jnp.exp2(m_old - m_new)
        m_ref[...] = m_new
        sums = []
        for h in range(B_HEADS):
            g = h // B_REP
            p = jnp.exp2(a_ref[h, 0:nk, :] - m_new[h:h + 1, :])
            sums.append(jnp.sum(p, axis=0, keepdims=True))
            hs = slice(h * B_HD, (h + 1) * B_HD)
            pv = jnp.dot(vt_ref[0, g * B_HD:(g + 1) * B_HD, rows], p.astype(BF16),
                         preferred_element_type=F32)
            acc_ref[hs, :] = alpha[h:h + 1, :] * acc_ref[hs, :] + pv
        l_ref[...] = alpha * l_ref[...] + jnp.concatenate(sums, axis=0)

    ab = near_ref.shape[2]
    n_far = jnp.maximum(q0 // ab - 1, 0)
    far_bias2 = far_ref[:, 0:1]

    def far_body(i, c):
        attend(i * ab, ab, lambda h: far_bias2[h:h + 1, :])
        return c

    lax.fori_loop(0, n_far, far_body, 0)

    def near_body(i, c):
        dd = i - q0 // ab + 1
        attend(i * ab, ab, lambda h: near_ref[dd, h])
        return c

    lax.fori_loop(n_far, nkb * kb_sz // ab, near_body, 0)

    _finish_heads(o_ref, x_ref, wout_ref, l_ref, acc_ref, B_HEADS)


def _dsa_attention_prompt(qt, qit, kwt, kw, k, vt, far, near, x, w_out):
    b, d, t = qt.shape
    tq = near.shape[3]
    top_k = min(TOPK_MAX, t // 4)
    ab = near.shape[2]
    assert t % tq == 0 and tq % CHUNK == 0 and tq % ab == 0 and ab % LANE == 0 and tq % (2 * LANE) == 0
    return pl.pallas_call(
        functools.partial(_dsa_prompt_kernel, top_k=top_k),
        grid=(b, t // tq),
        in_specs=[pl.BlockSpec((1, d, tq), lambda i, j: (i, 0, j)),
                  pl.BlockSpec((1, qit.shape[1], tq), lambda i, j: (i, 0, j)),
                  pl.BlockSpec((1, LANE, tq), lambda i, j: (i, 0, j)),
                  pl.BlockSpec((1, t, LANE), lambda i, j: (i, 0, 0)),
                  pl.BlockSpec((1, t, B_KV * B_HD), lambda i, j: (i, 0, 0)),
                  pl.BlockSpec((1, B_KV * B_HD, t), lambda i, j: (i, 0, 0)),
                  pl.BlockSpec(far.shape, lambda i, j: (0, 0)),
                  pl.BlockSpec(near.shape, lambda i, j: (0, 0, 0, 0)),
                  pl.BlockSpec((1, tq, d), lambda i, j: (i, j, 0)),
                  pl.BlockSpec((d, d), lambda i, j: (0, 0))],
        out_specs=pl.BlockSpec((1, tq, d), lambda i, j: (i, j, 0)),
        out_shape=jax.ShapeDtypeStruct((b, t, d), F32),
        scratch_shapes=[pltpu.VMEM((t, tq), I32), pltpu.VMEM((t, tq), I16), pltpu.VMEM((t, tq), I16),
                        pltpu.VMEM((t, tq), F32),
                        pltpu.VMEM((B_HEADS, tq), F32), pltpu.VMEM((B_HEADS, tq), F32),
                        pltpu.VMEM((d, tq), F32), pltpu.VMEM((B_HEADS, ab, tq), F32)],
        compiler_params=_cparams("parallel", "parallel"),
        name="dsa_attention",
    )(qt, qit, kwt, kw, k, vt, far, near, x, w_out)


def _dsa_cached_kernel(q_ref, qi_ref, wi_ref, rb_ref, kidx_ref, kp_ref, vp_ref, kn_ref, vn_ref, o_ref,
                       *, past, t_new, top_k):
    n_keys = past + t_new
    lp = kidx_ref.shape[1]
    negkey = _neg_inf_key()
    kpos = lax.broadcasted_iota(I32, (t_new, lp), 1)
    qpos = past + lax.broadcasted_iota(I32, (t_new, lp), 0)
    adm = kpos < (qpos // CHUNK + 1) * CHUNK

    s = _dot_nt(qi_ref[0], kidx_ref[0])
    w = jnp.maximum(s, 0.0) * wi_ref[0]
    sc = w[0:t_new]
    for h in range(1, IDX_HEADS):
        sc = sc + w[h * t_new:(h + 1) * t_new]
    sc = sc * ((IDX_DIM * IDX_HEADS) ** -0.5)
    key = jnp.where(adm, _sortable(sc), negkey)
    key = jnp.where(kpos < n_keys, key, INT_MIN)

    def bit_body(i, t_u):
        cand_u = t_u | jnp.left_shift(jnp.int32(1), 31 - i)
        cnt = jnp.sum(jnp.where(key >= (cand_u ^ INT_MIN), 1.0, 0.0), axis=1, keepdims=True)
        return jnp.where(cnt >= top_k, cand_u, t_u)

    t_s = lax.fori_loop(0, 32, bit_body, jnp.zeros((t_new, 1), I32)) ^ INT_MIN

    adm01 = jnp.where(adm, 1.0, 0.0)
    gt = jnp.where(key > t_s, adm01, 0.0)
    eq = jnp.where(key == t_s, adm01, 0.0)
    need = top_k - jnp.sum(jnp.where(key > t_s, 1.0, 0.0), axis=1, keepdims=True)
    r = lax.broadcasted_iota(I32, (LANE, LANE), 0)
    c = lax.broadcasted_iota(I32, (LANE, LANE), 1)
    utri = jnp.where(r < c, 1.0, 0.0).astype(BF16)
    carry = jnp.zeros((t_new, 1), F32)
    keeps = []
    for blk in range(lp // LANE):
        e = eq[:, blk * LANE:(blk + 1) * LANE]
        rank = carry + jnp.dot(e.astype(BF16), utri, preferred_element_type=F32)
        keeps.append(jnp.where(rank < need, e, 0.0))
        carry = carry + jnp.sum(e, axis=1, keepdims=True)
    sel = gt + jnp.concatenate(keeps, axis=1)

    near = max(past - MAX_DIST, 0) // LANE * LANE
    rel = (kpos - qpos)[:, near:]
    nb = N_BUCKETS // 2
    max_exact = nb // 2
    n = jnp.abs(rel)
    nf = jnp.maximum(n, 1).astype(F32)
    large = max_exact + (jnp.log(nf / max_exact) / math.log(MAX_DIST / max_exact)
                         * (nb - max_exact)).astype(I32)
    bucket = (rel > 0).astype(I32) * nb + jnp.where(n < max_exact, n, jnp.minimum(large, nb - 1))

    rows = B_REP * t_new
    sel_g = jnp.concatenate([sel] * B_REP, axis=0) != 0.0
    bucket_g = jnp.concatenate([bucket] * B_REP, axis=0)
    for g in range(B_KV):
        grp = lambda ref, n: ref[0, pl.ds(g, n, stride=B_KV), :]
        qg = q_ref[0, g * rows:(g + 1) * rows, :]
        rb = rb_ref[g * rows:(g + 1) * rows, :]
        bias_near = jnp.zeros((rows, lp - near), F32)
        for j in range(N_BUCKETS):
            bias_near = jnp.where(bucket_g == j, rb[:, j:j + 1], bias_near)
        bias = jnp.concatenate([jnp.broadcast_to(rb[:, nb - 1:nb], (rows, near)), bias_near], axis=1)
        zp = _dot_nt(qg, grp(kp_ref, past)) * (B_HD ** -0.5)
        zn = _dot_nt(qg, grp(kn_ref, t_new)) * (B_HD ** -0.5)
        ap = jnp.where(sel_g[:, :past], zp + bias[:, :past], NEG_INF)
        an = jnp.where(sel_g[:, past:n_keys], zn + bias[:, past:n_keys], NEG_INF)
        m = jnp.maximum(jnp.max(ap, axis=1, keepdims=True), jnp.max(an, axis=1, keepdims=True))
        pp, pn = jnp.exp(ap - m), jnp.exp(an - m)
        l = jnp.sum(pp, axis=1, keepdims=True) + jnp.sum(pn, axis=1, keepdims=True)
        o_ref[0, g * rows:(g + 1) * rows, :] = (_dot(pp, grp(vp_ref, past)) + _dot(pn, grp(vn_ref, t_new))) / l


def _dsa_attention_cached(q_rows, qi_rows, wi_col, rb_rows, kidx_all, k_past, v_past, k_new, v_new):
    b, rows, hd = q_rows.shape
    past, t_new = k_past.shape[1], k_new.shape[1]
    lp = kidx_all.shape[1]
    top_k = min(TOPK_MAX, (past + t_new) // 4)
    assert past % LANE == 0
    flat = lambda a: a.reshape(b, a.shape[1] * B_KV, B_HD)
    kv_spec = lambda n: pl.BlockSpec((1, n * B_KV, B_HD), lambda i: (i, 0, 0))
    return pl.pallas_call(
        functools.partial(_dsa_cached_kernel, past=past, t_new=t_new, top_k=top_k),
        grid=(b,),
        in_specs=[pl.BlockSpec((1, rows, hd), lambda i: (i, 0, 0)),
                  pl.BlockSpec((1,) + qi_rows.shape[1:], lambda i: (i, 0, 0)),
                  pl.BlockSpec((1,) + wi_col.shape[1:], lambda i: (i, 0, 0)),
                  pl.BlockSpec(rb_rows.shape, lambda i: (0, 0)),
                  pl.BlockSpec((1, lp, IDX_DIM), lambda i: (i, 0, 0)),
                  kv_spec(past), kv_spec(past), kv_spec(t_new), kv_spec(t_new)],
        out_specs=pl.BlockSpec((1, rows, hd), lambda i: (i, 0, 0)),
        out_shape=jax.ShapeDtypeStruct((b, rows, hd), F32),
        compiler_params=_cparams("parallel"),
        name="dsa_attention_cached",
    )(q_rows, qi_rows, wi_col, rb_rows, kidx_all, flat(k_past), flat(v_past), flat(k_new), flat(v_new))


def _cumsum_kernel(x_ref, o_ref):
    x = x_ref[0]
    n = x.shape[-1]
    lane = lax.broadcasted_iota(I32, x.shape, 1)
    s = 1
    while s < n:
        x = x + jnp.where(lane >= s, pltpu.roll(x, s, 1), 0.0)
        s *= 2
    o_ref[0] = x


def _cumsum_lanes(x):
    b, h, n = x.shape
    return pl.pallas_call(
        _cumsum_kernel,
        grid=(b,),
        in_specs=[pl.BlockSpec((1, h, n), lambda i: (i, 0, 0))],
        out_specs=pl.BlockSpec((1, h, n), lambda i: (i, 0, 0)),
        out_shape=jax.ShapeDtypeStruct((b, h, n), F32),
        compiler_params=_cparams("parallel"),
        name="logf_cumsum",
    )(x)


def _fox_init(m_ref, l_ref, acc_ref):
    m_ref[...] = jnp.full(m_ref.shape, NEG_INF, F32)
    l_ref[...] = jnp.zeros(l_ref.shape, F32)
    acc_ref[...] = jnp.zeros(acc_ref.shape, F32)


def _fox_tile(z_fn, pv_fn, cq, ck, mask, m_ref, l_ref, acc_ref, a_ref):
    c1 = (D_HD ** -0.5) * LOG2E
    cq2, ck2 = cq * LOG2E, ck * LOG2E
    cols = []
    for h in range(D_HEADS):
        a = z_fn(h) * c1 - ck2[:, h:h + 1]
        if mask is not None:
            a = jnp.where(mask, a, NEG_INF)
        a_ref[h] = a
        cols.append(jnp.max(a, axis=0, keepdims=True))
    m_old = m_ref[...]
    m_new = jnp.maximum(m_old, jnp.concatenate(cols, axis=0) + cq2)
    alpha = jnp.exp2(m_old - m_new)
    shift = m_new - cq2
    m_ref[...] = m_new
    sums = []
    for h in range(D_HEADS):
        p = jnp.exp2(a_ref[h] - shift[h:h + 1, :])
        sums.append(jnp.sum(p, axis=0, keepdims=True))
        hs = slice(h * D_HD, (h + 1) * D_HD)
        acc_ref[hs, :] = alpha[h:h + 1, :] * acc_ref[hs, :] + pv_fn(h, p.astype(BF16))
    l_ref[...] = alpha * l_ref[...] + jnp.concatenate(sums, axis=0)


def _hs(h):
    return slice(h * D_HD, (h + 1) * D_HD)


def _fox_prompt_kernel(qt_ref, k_ref, vt_ref, cq_ref, ck_ref, x_ref, wout_ref, o_ref,
                       m_ref, l_ref, acc_ref, a_ref):
    qi, step = pl.program_id(1), pl.program_id(2)
    tq, tk = qt_ref.shape[2], a_ref.shape[1]
    q0 = qi * tq
    ki = step - (pl.num_programs(2) - 1 - (q0 + tq - 1) // tk)
    k0 = ki * tk
    keys = pl.ds(pl.multiple_of(jnp.maximum(k0, 0), tk), tk)

    @pl.when(step == 0)
    def _():
        _fox_init(m_ref, l_ref, acc_ref)

    def run(masked):
        mask = None
        if masked:
            mask = (k0 + lax.broadcasted_iota(I32, (tk, tq), 0)) <= (q0 + lax.broadcasted_iota(I32, (tk, tq), 1))
        _fox_tile(lambda h: jnp.dot(k_ref[0, keys, _hs(h)], qt_ref[0, _hs(h), :], preferred_element_type=F32),
                  lambda h, p: jnp.dot(vt_ref[0, _hs(h), keys], p, preferred_element_type=F32),
                  cq_ref[0], ck_ref[0, keys, :], mask, m_ref, l_ref, acc_ref, a_ref)

    fully_visible = k0 + tk - 1 <= q0
    pl.when(jnp.logical_and(ki >= 0, fully_visible))(lambda: run(False))
    pl.when(jnp.logical_and(ki >= 0, jnp.logical_not(fully_visible)))(lambda: run(True))

    @pl.when(step == pl.num_programs(2) - 1)
    def _():
        _finish_heads(o_ref, x_ref, wout_ref, l_ref, acc_ref, D_HEADS)


def _fox_attention_prompt(qt, k, vt, cum_t, cum, x, w_out, tq, tk):
    b, d, t = qt.shape
    nq, nk = t // tq, t // tk
    return pl.pallas_call(
        _fox_prompt_kernel,
        grid=(b, nq, nk),
        in_specs=[pl.BlockSpec((1, d, tq), lambda i, j, kk: (i, 0, j)),
                  pl.BlockSpec((1, t, d), lambda i, j, kk: (i, 0, 0)),
                  pl.BlockSpec((1, d, t), lambda i, j, kk: (i, 0, 0)),
                  pl.BlockSpec((1, D_HEADS, tq), lambda i, j, kk: (i, 0, j)),
                  pl.BlockSpec((1, t, D_HEADS), lambda i, j, kk: (i, 0, 0)),
                  pl.BlockSpec((1, tq, d), lambda i, j, kk: (i, j, 0)),
                  pl.BlockSpec((d, d), lambda i, j, kk: (0, 0))],
        out_specs=pl.BlockSpec((1, tq, d), lambda i, j, kk: (i, j, 0)),
        out_shape=jax.ShapeDtypeStruct((b, t, d), F32),
        scratch_shapes=[pltpu.VMEM((D_HEADS, tq), F32), pltpu.VMEM((D_HEADS, tq), F32),
                        pltpu.VMEM((d, tq), F32), pltpu.VMEM((D_HEADS, tk, tq), F32)],
        compiler_params=_cparams("parallel", "parallel", "arbitrary"),
        name="fox_attention",
    )(qt, k, vt, cum_t, cum, x, w_out)


def _fox_cached_kernel(q_ref, kp_ref, vp_ref, kn_ref, vn_ref, cq_ref, ckp_ref, ckn_ref, o_ref,
                       m_ref, l_ref, acc_ref, *, t_new):
    ki = pl.program_id(1)
    n_past = pl.num_programs(1) - 1
    rows = q_ref.shape[1]
    c1 = (D_HD ** -0.5) * LOG2E

    @pl.when(ki == 0)
    def _():
        m_ref[...] = jnp.full(m_ref.shape, NEG_INF, F32)
        l_ref[...] = jnp.zeros(l_ref.shape, F32)
        acc_ref[...] = jnp.zeros(acc_ref.shape, F32)

    def tile(k2d, v2d, ck_row, causal):
        cols = k2d.shape[0]
        a = _dot_nt(q_ref[0], k2d) * c1 - ck_row * LOG2E
        r = lax.broadcasted_iota(I32, (rows, cols), 0)
        c = lax.broadcasted_iota(I32, (rows, cols), 1)
        ok = (c % D_HEADS) == (r // t_new)
        if causal:
            ok = jnp.logical_and(ok, (c // D_HEADS) <= (r % t_new))
        a = jnp.where(ok, a, NEG_INF)
        cq2 = cq_ref[0] * LOG2E
        m_old = m_ref[...]
        m_new = jnp.maximum(m_old, jnp.max(a, axis=1, keepdims=True) + cq2)
        alpha = jnp.exp2(m_old - m_new)
        p = jnp.exp2(a - (m_new - cq2))
        l_ref[...] = alpha * l_ref[...] + jnp.sum(p, axis=1, keepdims=True)
        acc_ref[...] = alpha * acc_ref[...] + _dot(p, v2d)
        m_ref[...] = m_new

    @pl.when(ki < n_past)
    def _():
        tk = kp_ref.shape[1]
        tile(kp_ref[0].reshape(tk * D_HEADS, D_HD), vp_ref[0].reshape(tk * D_HEADS, D_HD), ckp_ref[0], False)

    @pl.when(ki == n_past)
    def _():
        tile(kn_ref[0].reshape(t_new * D_HEADS, D_HD), vn_ref[0].reshape(t_new * D_HEADS, D_HD), ckn_ref[0], True)
        o_ref[0] = acc_ref[...] / l_ref[...]


def _fox_attention_cached(q_rows, k_past, v_past, k_new, v_new, cq_col, ck_past, ck_new, tk):
    b, rows, hd = q_rows.shape
    past, t_new = k_past.shape[1], k_new.shape[1]
    n_past = past // tk
    pidx = lambda i, kk: (i, jnp.minimum(kk, n_past - 1), 0, 0)
    return pl.pallas_call(
        functools.partial(_fox_cached_kernel, t_new=t_new),
        grid=(b, n_past + 1),
        in_specs=[pl.BlockSpec((1, rows, hd), lambda i, kk: (i, 0, 0)),
                  pl.BlockSpec((1, tk, D_HEADS, D_HD), pidx),
                  pl.BlockSpec((1, tk, D_HEADS, D_HD), pidx),
                  pl.BlockSpec((1, t_new, D_HEADS, D_HD), lambda i, kk: (i, 0, 0, 0)),
                  pl.BlockSpec((1, t_new, D_HEADS, D_HD), lambda i, kk: (i, 0, 0, 0)),
                  pl.BlockSpec((1, rows, 1), lambda i, kk: (i, 0, 0)),
                  pl.BlockSpec((1, 1, tk * D_HEADS), lambda i, kk: (i, 0, jnp.minimum(kk, n_past - 1))),
                  pl.BlockSpec((1, 1, t_new * D_HEADS), lambda i, kk: (i, 0, 0))],
        out_specs=pl.BlockSpec((1, rows, hd), lambda i, kk: (i, 0, 0)),
        out_shape=jax.ShapeDtypeStruct((b, rows, hd), F32),
        scratch_shapes=[pltpu.VMEM((rows, 1), F32), pltpu.VMEM((rows, 1), F32), pltpu.VMEM((rows, hd), F32)],
        compiler_params=_cparams("parallel", "arbitrary"),
        name="fox_attention_cached",
    )(q_rows, k_past, v_past, k_new, v_new, cq_col, ck_past, ck_new)


def _pad_rows(a, rows):
    if a.shape[1] == rows:
        return a
    return jnp.pad(a, ((0, 0), (0, rows - a.shape[1])) + ((0, 0),) * (a.ndim - 2))


def _round_up(n, m):
    return -(-n // m) * m


def _dsa_mixer(x, g, k_past, v_past, ki_past, w, w_out, rel_bias):
    b, t, d = x.shape
    past = k_past.shape[1]
    n_keys = past + t
    if past == 0:
        qt, k4, kb, v4, vt, qit, kw, kwt, ki_t = _proj(
            x, g, w, ((0, "t", BF16, None), (1, B_HD, F32, None), (1, "rows", BF16, None),
                      (2, B_HD, F32, None), (2, "t", BF16, None), (3, "t", BF16, None),
                      (4, "rows", F32, None), (4, "t", F32, None), (4, ("first_t", IDX_DIM), F32, None)),
            jnp.zeros((1, LANE), F32))
        far, near = _bias_tiles(rel_bias, 2 * LANE, 2 * LANE)
        y = _dsa_attention_prompt(qt, qit, kwt, kw, kb, vt, far, near, x, w_out)
        return (y, k4, v4, jnp.swapaxes(ki_t, 1, 2))
    q, k4, v4, qidx, kw = _proj(
        x, g, w, ((0, "rows", BF16, None), (1, B_HD, F32, None), (2, B_HD, F32, None),
                  (3, "rows", BF16, None), (4, "rows", F32, None)),
        jnp.zeros((1, LANE), F32))
    ki = kw[:, :, :IDX_DIM]
    to_rows = lambda a, nh: jnp.swapaxes(a.reshape(b, t, nh, -1), 1, 2).reshape(b, nh * t, -1)
    kidx_all = _pad_rows(jnp.concatenate([ki_past, ki], axis=1), _round_up(n_keys, LANE))
    o = _dsa_attention_cached(to_rows(q, B_HEADS), to_rows(qidx, IDX_HEADS),
                              to_rows(kw[:, :, IDX_DIM:IDX_DIM + IDX_HEADS], IDX_HEADS),
                              jnp.repeat(rel_bias.T, t, axis=0), kidx_all, k_past, v_past, k4, v4)
    o = jnp.swapaxes(o.reshape(b, B_HEADS, t, B_HD), 1, 2).reshape(b * t, d)
    y = _out_proj(x.reshape(b * t, d), o, w_out).reshape(b, t, d)
    return (y, k4, v4, ki)


def _fox_mixer(x, g, k_past, v_past, lf_past, w, b_f, w_out):
    b, t, d = x.shape
    past = k_past.shape[1]
    heads4 = ((1, D_HD, F32, None), (2, D_HD, F32, None))
    if past == 0:
        tq = tk = min(4 * LANE, t)
        k4, v4, logf_t, qt, kb, vt = _proj(
            x, g, w, heads4 + ((3, ("first_t", D_HEADS), F32, "log_sigmoid"), (0, "t", BF16, None),
                               (1, "rows", BF16, None), (2, "t", BF16, None)), b_f)
        cum_t = _cumsum_lanes(logf_t)
        y = _fox_attention_prompt(qt, kb, vt, cum_t, jnp.swapaxes(cum_t, 1, 2), x, w_out, tq, tk)
        return (y, k4, v4, jnp.swapaxes(logf_t, 1, 2))
    else:
        tk = math.gcd(past, 4 * LANE)
        k4, v4, logf, q = _proj(
            x, g, w, heads4 + ((3, ("first", D_HEADS), F32, "log_sigmoid"), (0, "rows", BF16, None)), b_f)
        lf_all = _pad_rows(jnp.concatenate([lf_past, logf], axis=1), _round_up(past + t, LANE))
        cum = jnp.swapaxes(_cumsum_lanes(jnp.swapaxes(lf_all, 1, 2)), 1, 2)[:, :past + t]
        ck = cum.reshape(b, 1, (past + t) * D_HEADS)
        to_rows = lambda a: jnp.swapaxes(a.reshape(b, t, D_HEADS, -1), 1, 2).reshape(b, D_HEADS * t, -1)
        o = _fox_attention_cached(to_rows(q), k_past, v_past, k4, v4, to_rows(cum[:, past:]),
                                  ck[:, :, :past * D_HEADS], ck[:, :, past * D_HEADS:], tk)
        o = jnp.swapaxes(o.reshape(b, D_HEADS, t, D_HD), 1, 2).reshape(b, t, d)
    y = _out_proj(x.reshape(b * t, d), o.reshape(b * t, d), w_out).reshape(b, t, d)
    return (y, k4, v4, logf)


def _run_group(x, pos0, a_st, b_k, b_v, b_ki, c_st, d_k, d_v, d_lf, mem_k, mem_v, prm):
    b, t, d = x.shape
    depth = prm["norm_mix"].shape[0]
    new = {n: [] for n in ("a", "bk", "bv", "bki", "c", "dk", "dv", "dlf")}
    for i in range(depth):
        kind, j = i % 4, i // 4
        g = prm["norm_mix"][i]
        if kind == 0:
            x, st = _conv_mixer(x, g, prm["a_w_in"][j], prm["a_conv"][j], a_st[j], prm["a_w_out"][j])
            new["a"].append(st)
        elif kind == 1:
            x, kk, vv, ki = _dsa_mixer(x, g, b_k[j], b_v[j], b_ki[j], prm["b_w"][j], prm["b_w_out"][j],
                                       prm["rel_bias"])
            new["bk"].append(kk); new["bv"].append(vv); new["bki"].append(ki)
        elif kind == 2:
            x, st = _pool_mixer(x, g, c_st[j], prm["c_w_group"][j], prm["c_scale"][j], pos0)
            new["c"].append(st)
        else:
            x, kk, vv, lf = _fox_mixer(x, g, d_k[j], d_v[j], d_lf[j], prm["d_w"][j], prm["d_b_f"][j],
                                       prm["d_w_out"][j])
            new["dk"].append(kk); new["dv"].append(vv); new["dlf"].append(lf)
        x = _xattn(x, prm["norm_xattn"], prm["xa_wq"], mem_k, mem_v, prm["xa_wo"], i)
        last = i == depth - 1
        x = _ffn(x.reshape(b * t, d), prm["norm_ffn"], prm["ffn_w1"], prm["ffn_w2"],
                 prm["final_norm"], i, last).reshape(b, t, d)
    return (x,) + tuple(jnp.stack(new[n]) for n in ("a", "bk", "bv", "bki", "c", "dk", "dv", "dlf"))


def kernel(x_prompt, x_sample, state_a_conv, cache_b_k, cache_b_v, cache_b_kidx, state_c_pool,
           cache_d_k, cache_d_v, cache_d_logf, cache_mem_k, cache_mem_v, mem_prompt,
           norm_mix, norm_xattn, norm_mem, norm_ffn, final_norm,
           a_w_in, a_conv, a_w_out, b_w_in, b_w_out, rel_bias, c_w_group, c_scale,
           d_w_in, d_b_f, d_w_out, xa_wq, xa_wkv, xa_wo, ffn_w1, ffn_w2):
    bp, t, d = x_prompt.shape
    depth = norm_mix.shape[0]
    n_b, n_d = b_w_in.shape[0], d_w_in.shape[0]
    bf = lambda w: w.astype(BF16)

    def split_cols(w, widths):
        out, c = [], 0
        for wd in widths:
            piece = w[:, c:c + wd]
            c += wd
            if wd % LANE:
                piece = jnp.pad(piece, ((0, 0), (0, _round_up(wd, LANE) - wd)))
            out.append(bf(piece))
        assert c == w.shape[1]
        return out

    b_q, b_kvw = B_HEADS * B_HD, B_KV * B_HD
    b_w = [split_cols(b_w_in[j], (b_q, b_kvw, b_kvw, IDX_HEADS * IDX_DIM, IDX_DIM + IDX_HEADS))
           for j in range(n_b)]
    d_w = [split_cols(d_w_in[j], (d, d, d, D_HEADS)) for j in range(n_d)]
    d_bf = [jnp.pad(d_b_f[j], (0, LANE - D_HEADS)).reshape(1, LANE) for j in range(n_d)]

    prm = {"norm_mix": norm_mix, "norm_xattn": norm_xattn, "norm_ffn": norm_ffn, "final_norm": final_norm,
           "a_w_in": bf(a_w_in), "a_conv": a_conv, "a_w_out": bf(a_w_out),
           "b_w": b_w, "b_w_out": bf(b_w_out), "rel_bias": rel_bias,
           "c_w_group": bf(c_w_group), "c_scale": c_scale,
           "d_w": d_w, "d_b_f": d_bf, "d_w_out": bf(d_w_out),
           "xa_wq": bf(xa_wq), "xa_wo": bf(xa_wo), "ffn_w1": bf(ffn_w1), "ffn_w2": bf(ffn_w2)}

    n_mem = mem_prompt.shape[1]
    mk, mv, mk_rows, mv_rows = _memory_kv(mem_prompt, norm_mem, bf(xa_wkv))

    n_a, n_c = a_w_in.shape[0], c_w_group.shape[0]
    z = lambda *s: jnp.zeros(s, F32)
    gp = _run_group(x_prompt, 0,
                    z(n_a, bp, CONV_W - 1, d),
                    z(n_b, bp, 0, B_KV, B_HD), z(n_b, bp, 0, B_KV, B_HD), z(n_b, bp, 0, IDX_DIM),
                    z(n_c, bp, POOL_STATE, d),
                    z(n_d, bp, 0, D_HEADS, D_HD), z(n_d, bp, 0, D_HEADS, D_HD), z(n_d, bp, 0, D_HEADS),
                    mk_rows, mv_rows, prm)

    bs = x_sample.shape[0]
    past_len = cache_b_k.shape[2]
    gs = _run_group(x_sample, past_len, state_a_conv, cache_b_k, cache_b_v, cache_b_kidx, state_c_pool,
                    cache_d_k, cache_d_v, cache_d_logf, cache_mem_k, cache_mem_v, prm)

    (y_p, a_p, bk_p, bv_p, bki_p, c_p, dk_p, dv_p, dlf_p) = gp
    (y_s, a_s, bk_s, bv_s, bki_s, c_s, dk_s, dv_s, dlf_s) = gs
    return (y_p, y_s, a_p, a_s, bk_p, bv_p, bki_p, bk_s, bv_s, bki_s, c_p, c_s,
            dk_p, dv_p, dlf_p, dk_s, dv_s, dlf_s, mk, mv)
```

```python
import functools
import math

import jax
import jax.numpy as jnp
from jax import lax
from jax.experimental import pallas as pl
from jax.experimental.pallas import tpu as pltpu

F32 = jnp.float32
BF16 = jnp.bfloat16
I32 = jnp.int32
I16 = jnp.int16

EPS = 1e-6
NEG_INF = -1e30
LOG2E = math.log2(math.e)
CHUNK = 64
LANE = 128
VMEM_LIMIT = 48 * 1024 * 1024

CONV_W = 3
POOL_WINDOWS = (2, 4, 8, 16)
POOL_STATE = max(POOL_WINDOWS) - 1
B_HEADS, B_KV, B_HD = 8, 2, 128
B_REP = B_HEADS // B_KV
IDX_HEADS, IDX_DIM = 8, 64
TOPK_MAX = 256
N_BUCKETS, MAX_DIST = 32, 128
D_HEADS, D_HD = 8, 128
MEM_HEADS = 4
INT_MIN = -2147483648
BIAS_CENTER = 2 * LANE


def _cparams(*sem):
    return pltpu.CompilerParams(dimension_semantics=sem, vmem_limit_bytes=VMEM_LIMIT)


def _dot(a, b):
    return jnp.dot(a.astype(BF16), b.astype(BF16), preferred_element_type=F32)


def _dot_nt(a, b):
    return lax.dot_general(a.astype(BF16), b.astype(BF16), (((1,), (1,)), ((), ())),
                           preferred_element_type=F32)


def _dot_tn(a, b):
    return lax.dot_general(a.astype(BF16), b.astype(BF16), (((0,), (0,)), ((), ())),
                           preferred_element_type=F32)


def _rms(x, g):
    return x * lax.rsqrt(jnp.mean(x * x, axis=-1, keepdims=True) + EPS) * g


def _finish_heads(o_ref, x_ref, w_ref, l_ref, acc_ref, n_heads):
    hd = acc_ref.shape[0] // n_heads
    inv_l = 1.0 / l_ref[...]
    heads_t = jnp.concatenate([acc_ref[h * hd:(h + 1) * hd, :] * inv_l[h:h + 1, :] for h in range(n_heads)],
                              axis=0)
    o_ref[0] = x_ref[0] + _dot_tn(heads_t, w_ref[...])


def _row_tile(n, cap):
    t = min(n, cap)
    assert n % t == 0
    return t


def _memkv_kernel(mem_ref, g_ref, w_ref, k_ref, v_ref, kb_ref, vb_ref):
    bb, nm, d = mem_ref.shape
    m = mem_ref[...].reshape(bb * nm, d)
    mn = m * lax.rsqrt(jnp.mean(m * m, axis=-1, keepdims=True) + EPS)
    h = (mn * g_ref[0]).astype(BF16)
    hd = d // MEM_HEADS
    k = jnp.dot(h, w_ref[0, :, :d], preferred_element_type=F32)
    v = jnp.dot(h, w_ref[0, :, d:], preferred_element_type=F32)
    for i in range(bb):
        rows = slice(i * nm, (i + 1) * nm)
        kb_ref[0, i] = k[rows].astype(BF16)
        vb_ref[0, i] = v[rows].astype(BF16)
        k_ref[0, i] = pltpu.einshape("m(hd)->mhd", k[rows], d=hd)
        v_ref[0, i] = pltpu.einshape("m(hd)->mhd", v[rows], d=hd)


def _memory_kv(mem, g_mem, w_kv):
    depth, d = g_mem.shape
    b, nm, _ = mem.shape
    hd = d // MEM_HEADS
    out = jax.ShapeDtypeStruct((depth, b, nm, MEM_HEADS, hd), F32)
    out_b = jax.ShapeDtypeStruct((depth, b, nm, d), BF16)
    bb = math.gcd(b, 2)
    heads_spec = pl.BlockSpec((1, bb, nm, MEM_HEADS, hd), lambda l, i: (l, i, 0, 0, 0))
    rows_spec = pl.BlockSpec((1, bb, nm, d), lambda l, i: (l, i, 0, 0))
    return pl.pallas_call(
        _memkv_kernel,
        grid=(depth, b // bb),
        in_specs=[pl.BlockSpec((bb, nm, d), lambda l, i: (i, 0, 0)),
                  pl.BlockSpec((1, 1, d), lambda l, i: (l, 0, 0)),
                  pl.BlockSpec((1, d, 2 * d), lambda l, i: (l, 0, 0))],
        out_specs=[heads_spec, heads_spec, rows_spec, rows_spec],
        out_shape=[out, out, out_b, out_b],
        compiler_params=_cparams("parallel", "parallel"),
        name="memory_kv",
    )(mem, g_mem.reshape(depth, 1, d), w_kv)


def _ffn_kernel(x_ref, g_ref, w1_ref, w2_ref, gf_ref, o_ref, h_ref, acc_ref, *, final_norm):
    j = pl.program_id(1)

    @pl.when(j == 0)
    def _():
        h_ref[...] = _rms(x_ref[...], g_ref[...]).astype(BF16)
        acc_ref[...] = jnp.zeros_like(acc_ref)

    u = jnp.maximum(jnp.dot(h_ref[...], w1_ref[...], preferred_element_type=F32), 0.0)
    acc_ref[...] += jnp.dot((u * u).astype(BF16), w2_ref[...], preferred_element_type=F32)

    @pl.when(j == pl.num_programs(1) - 1)
    def _():
        y = x_ref[...] + acc_ref[...]
        o_ref[...] = _rms(y, gf_ref[...]) if final_norm else y


def _ffn(x, g, w1, w2, gf, layer, final_norm):
    n, d = x.shape
    f = w1.shape[2]
    tm = _row_tile(n, 1024)
    tf = 1024
    return pl.pallas_call(
        functools.partial(_ffn_kernel, final_norm=final_norm),
        grid=(n // tm, f // tf),
        in_specs=[pl.BlockSpec((tm, d), lambda i, j: (i, 0)),
                  pl.BlockSpec((None, 1, d), lambda i, j: (layer, 0, 0)),
                  pl.BlockSpec((None, d, tf), lambda i, j: (layer, 0, j)),
                  pl.BlockSpec((None, tf, d), lambda i, j: (layer, j, 0)),
                  pl.BlockSpec((1, d), lambda i, j: (0, 0))],
        out_specs=pl.BlockSpec((tm, d), lambda i, j: (i, 0)),
        out_shape=jax.ShapeDtypeStruct((n, d), F32),
        scratch_shapes=[pltpu.VMEM((tm, d), BF16), pltpu.VMEM((tm, d), F32)],
        compiler_params=_cparams("parallel", "arbitrary"),
        name="ffn",
    )(x, g.reshape(-1, 1, d), w1, w2, gf.reshape(1, d))


def _xattn_kernel(x_ref, g_ref, wq_ref, mk_ref, mv_ref, wo_ref, o_ref):
    x = x_ref[0]
    d = x.shape[-1]
    hd = d // MEM_HEADS
    h = _rms(x, g_ref[...]).astype(BF16)
    q = jnp.dot(h, wq_ref[...], preferred_element_type=F32)
    outs = []
    if len(mk_ref.shape) == 3:
        mk = pltpu.einshape("mhd->m(hd)", mk_ref[...]).astype(BF16)
        mv = pltpu.einshape("mhd->m(hd)", mv_ref[...]).astype(BF16)
    else:
        mk, mv = mk_ref[...], mv_ref[...]
    for hh in range(MEM_HEADS):
        sl = slice(hh * hd, (hh + 1) * hd)
        kh, vh = mk[:, sl], mv[:, sl]
        s = _dot_nt(q[:, sl], kh) * (hd ** -0.5)
        m = jnp.max(s, axis=-1, keepdims=True)
        p = jnp.exp(s - m)
        l = jnp.sum(p, axis=-1, keepdims=True)
        outs.append(_dot(p, vh) / l)
    o = jnp.concatenate(outs, axis=-1)
    o_ref[0] = x + _dot(o, wo_ref[...])


def _xattn(x, g, wq, mk, mv, wo, layer):
    b, t, d = x.shape
    tm = _row_tile(t, 1024)
    kv_spec = pl.BlockSpec((None, None) + mk.shape[2:], lambda i, j: (layer, i) + (0,) * (mk.ndim - 2))
    return pl.pallas_call(
        _xattn_kernel,
        grid=(b, t // tm),
        in_specs=[pl.BlockSpec((1, tm, d), lambda i, j: (i, j, 0)),
                  pl.BlockSpec((None, 1, d), lambda i, j: (layer, 0, 0)),
                  pl.BlockSpec((None, d, d), lambda i, j: (layer, 0, 0)),
                  kv_spec, kv_spec,
                  pl.BlockSpec((None, d, d), lambda i, j: (layer, 0, 0))],
        out_specs=pl.BlockSpec((1, tm, d), lambda i, j: (i, j, 0)),
        out_shape=jax.ShapeDtypeStruct((b, t, d), F32),
        compiler_params=_cparams("parallel", "parallel"),
        name="xattn",
    )(x, g.reshape(-1, 1, d), wq, mk, mv, wo)


def _conv_kernel(x_ref, g_ref, win_ref, wc_ref, st_ref, wout_ref, o_ref, nst_ref, z_ref):
    t = pl.program_id(1)
    x = x_ref[0]
    tm, d = x.shape
    pad = 8

    @pl.when(t == 0)
    def _():
        z_ref[pad - 2:pad, :] = st_ref[0]

    h = _rms(x, g_ref[...]).astype(BF16)
    bg = jnp.dot(h, win_ref[:, 0:d], preferred_element_type=F32)
    cg = jnp.dot(h, win_ref[:, d:2 * d], preferred_element_type=F32)
    u = jnp.dot(h, win_ref[:, 2 * d:3 * d], preferred_element_type=F32)
    z = cg * u
    z_ref[pad:pad + tm, :] = z
    conv = (z_ref[pad - 2:pad - 2 + tm, :] * wc_ref[0:1, :]
            + z_ref[pad - 1:pad - 1 + tm, :] * wc_ref[1:2, :]
            + z * wc_ref[2:3, :])
    o_ref[0] = x + _dot(bg * conv, wout_ref[...])
    last = z_ref[pad + tm - 2:pad + tm, :]
    z_ref[pad - 2:pad, :] = last

    @pl.when(t == pl.num_programs(1) - 1)
    def _():
        nst_ref[0] = last


def _conv_mixer(x, g, w_in, w_conv, state, w_out):
    b, t, d = x.shape
    tm = _row_tile(t, 1024)
    once = pl.Buffered(1)
    return pl.pallas_call(
        _conv_kernel,
        grid=(b, t // tm),
        in_specs=[pl.BlockSpec((1, tm, d), lambda i, j: (i, j, 0)),
                  pl.BlockSpec((1, d), lambda i, j: (0, 0)),
                  pl.BlockSpec((d, 3 * d), lambda i, j: (0, 0), pipeline_mode=once),
                  pl.BlockSpec((CONV_W, d), lambda i, j: (0, 0)),
                  pl.BlockSpec((1, CONV_W - 1, d), lambda i, j: (i, 0, 0)),
                  pl.BlockSpec((d, d), lambda i, j: (0, 0), pipeline_mode=once)],
        out_specs=[pl.BlockSpec((1, tm, d), lambda i, j: (i, j, 0)),
                   pl.BlockSpec((1, CONV_W - 1, d), lambda i, j: (i, 0, 0))],
        out_shape=[jax.ShapeDtypeStruct((b, t, d), F32),
                   jax.ShapeDtypeStruct((b, CONV_W - 1, d), F32)],
        scratch_shapes=[pltpu.VMEM((tm + 8, d), F32)],
        compiler_params=_cparams("parallel", "arbitrary"),
        name="conv_mixer",
    )(x, g.reshape(1, d), w_in, w_conv, state, w_out)


def _pool_kernel(x_ref, g_ref, st_ref, wg_ref, sc_ref, o_ref, nst_ref, h_ref, *s_refs, pos0):
    t = pl.program_id(1)
    x = x_ref[0]
    tm, d = x.shape
    n_lv = len(POOL_WINDOWS)
    gw = d // n_lv
    base = 2 * (POOL_STATE + 1)
    lead = base - POOL_STATE
    end = base + tm

    @pl.when(t == 0)
    def _():
        h_ref[0:lead, :] = jnp.zeros((lead, d), F32)
        h_ref[lead:base, :] = st_ref[0]

    h = _rms(x, g_ref[...])
    h_ref[base:end, :] = h
    pos = pos0 + t * tm + lax.broadcasted_iota(I32, (tm, gw), 0)
    ys = []
    prev, c_prev = h_ref, 0
    for lv in range(1, n_lv + 1):
        w, shift, start = POOL_WINDOWS[lv - 1], 2 ** (lv - 1), 8 * lv
        c0 = (lv - 1) * gw
        cols = slice(c0 - c_prev, d - c_prev)
        cur = prev[start:end, cols] + prev[start - shift:end - shift, cols]
        if lv < n_lv:
            s_refs[lv - 1][start:end, :] = cur[:, gw:]
        win = cur[base - start:, :gw]
        count = jnp.minimum(w, pos + 1).astype(F32)
        dlt = win / count - h[:, c0:c0 + gw]
        ys.append(_dot(dlt, wg_ref[lv - 1]))
        if lv < n_lv:
            prev, c_prev = s_refs[lv - 1], c0 + gw
    y = jnp.concatenate(ys, axis=-1) * sc_ref[...]
    o_ref[0] = x + y
    last = h_ref[end - POOL_STATE:end, :]
    h_ref[lead:base, :] = last

    @pl.when(t == pl.num_programs(1) - 1)
    def _():
        nst_ref[0] = last


def _pool_mixer(x, g, state, w_group, scale, pos0):
    b, t, d = x.shape
    ng, gw, _ = w_group.shape
    tm = _row_tile(t, 512)
    assert POOL_WINDOWS == tuple(2 ** (lv + 1) for lv in range(ng)) and tm >= POOL_STATE
    rows = tm + 2 * (POOL_STATE + 1)
    return pl.pallas_call(
        functools.partial(_pool_kernel, pos0=pos0),
        grid=(b, t // tm),
        in_specs=[pl.BlockSpec((1, tm, d), lambda i, j: (i, j, 0)),
                  pl.BlockSpec((1, d), lambda i, j: (0, 0)),
                  pl.BlockSpec((1, POOL_STATE, d), lambda i, j: (i, 0, 0)),
                  pl.BlockSpec((ng, gw, gw), lambda i, j: (0, 0, 0)),
                  pl.BlockSpec((1, d), lambda i, j: (0, 0))],
        out_specs=[pl.BlockSpec((1, tm, d), lambda i, j: (i, j, 0)),
                   pl.BlockSpec((1, POOL_STATE, d), lambda i, j: (i, 0, 0))],
        out_shape=[jax.ShapeDtypeStruct((b, t, d), F32),
                   jax.ShapeDtypeStruct((b, POOL_STATE, d), F32)],
        scratch_shapes=[pltpu.VMEM((rows, d - lv * gw), F32) for lv in range(ng)],
        compiler_params=_cparams("parallel", "arbitrary"),
        name="pool_mixer",
    )(x, g.reshape(1, d), state, w_group, scale.reshape(1, d))


def _proj_kernel(*refs, n_w, outs):
    x_ref, g_ref = refs[0], refs[1]
    w_refs = refs[2:2 + n_w]
    e_ref = refs[2 + n_w]
    o_refs = refs[3 + n_w:]
    h = _rms(x_ref[0], g_ref[...]).astype(BF16)
    ys = {}
    for (wi, mode, _, ep), o_ref in zip(outs, o_refs):
        if wi not in ys:
            ys[wi] = jnp.dot(h, w_refs[wi][...], preferred_element_type=F32)
        y = ys[wi]
        if ep == "log_sigmoid":
            u = -(y + e_ref[...])
            y = -(jnp.maximum(u, 0.0) + jnp.log1p(jnp.exp(-jnp.abs(u))))
        if mode == "rows":
            o_ref[0] = y.astype(o_ref.dtype)
        elif isinstance(mode, tuple) and mode[0] == "first":
            o_ref[0] = y[:, :mode[1]].astype(o_ref.dtype)
        elif isinstance(mode, tuple):
            o_ref[0] = jnp.transpose(y)[:mode[1], :].astype(o_ref.dtype)
        elif mode == "t":
            o_ref[0] = jnp.transpose(y).astype(o_ref.dtype)
        else:
            o_ref[0] = pltpu.einshape("m(hd)->mhd", y.astype(o_ref.dtype), d=mode)


def _proj(x, g, ws, outs, extra):
    b, t, d = x.shape
    tm = _row_tile(t, 512)
    in_specs = [pl.BlockSpec((1, tm, d), lambda i, j: (i, j, 0)), pl.BlockSpec((1, d), lambda i, j: (0, 0))]
    in_specs += [pl.BlockSpec(w.shape, lambda i, j: (0, 0)) for w in ws]
    in_specs += [pl.BlockSpec(extra.shape, lambda i, j: (0, 0))]
    out_specs, out_shape = [], []
    for wi, mode, dt, _ in outs:
        n = ws[wi].shape[1]
        if isinstance(mode, tuple):
            mode, n = ("rows" if mode[0] == "first" else "t"), mode[1]
        if mode == "rows":
            out_specs.append(pl.BlockSpec((1, tm, n), lambda i, j: (i, j, 0)))
            out_shape.append(jax.ShapeDtypeStruct((b, t, n), dt))
        elif mode == "t":
            out_specs.append(pl.BlockSpec((1, n, tm), lambda i, j: (i, 0, j)))
            out_shape.append(jax.ShapeDtypeStruct((b, n, t), dt))
        else:
            out_specs.append(pl.BlockSpec((1, tm, n // mode, mode), lambda i, j: (i, j, 0, 0)))
            out_shape.append(jax.ShapeDtypeStruct((b, t, n // mode, mode), dt))
    return pl.pallas_call(
        functools.partial(_proj_kernel, n_w=len(ws), outs=tuple(outs)),
        grid=(b, t // tm),
        in_specs=in_specs,
        out_specs=out_specs,
        out_shape=out_shape,
        compiler_params=_cparams("parallel", "parallel"),
        name="norm_proj",
    )(x, g.reshape(1, d), *ws, extra)


def _outproj_kernel(x_ref, a_ref, w_ref, o_ref):
    o_ref[...] = x_ref[...] + _dot(a_ref[...], w_ref[...])


def _out_proj(x, a, w):
    n, d = x.shape
    tm = _row_tile(n, 512)
    return pl.pallas_call(
        _outproj_kernel,
        grid=(n // tm,),
        in_specs=[pl.BlockSpec((tm, d), lambda i: (i, 0)),
                  pl.BlockSpec((tm, d), lambda i: (i, 0)),
                  pl.BlockSpec((d, d), lambda i: (0, 0))],
        out_specs=pl.BlockSpec((tm, d), lambda i: (i, 0)),
        out_shape=jax.ShapeDtypeStruct((n, d), F32),
        compiler_params=_cparams("parallel"),
        name="out_proj",
    )(x, a, w)


def _bias_table_kernel(rbt_ref, o_ref):
    width = o_ref.shape[-1]
    rel = BIAS_CENTER - lax.broadcasted_iota(I32, (1, width), 1)
    nb = N_BUCKETS // 2
    max_exact = nb // 2
    ret = (rel > 0).astype(I32) * nb
    n = jnp.abs(rel)
    nf = jnp.maximum(n, 1).astype(F32)
    large = max_exact + (jnp.log(nf / max_exact) / math.log(MAX_DIST / max_exact)
                         * (nb - max_exact)).astype(I32)
    large = jnp.minimum(large, nb - 1)
    bucket = ret + jnp.where(n < max_exact, n, large)
    acc = jnp.zeros(o_ref.shape, F32)
    for j in range(N_BUCKETS):
        acc = jnp.where(bucket == j, rbt_ref[:, j:j + 1], acc)
    o_ref[...] = acc * LOG2E


def _bias_tiles_kernel(rbt_ref, far_ref, near_ref, tab_ref):
    n_d, nh, kb, tq = near_ref.shape
    _bias_table_kernel(rbt_ref, tab_ref)
    far_ref[...] = jnp.broadcast_to(tab_ref[:, BIAS_CENTER + MAX_DIST:BIAS_CENTER + MAX_DIST + 1], far_ref.shape)
    for dd in range(n_d):
        s0 = BIAS_CENTER - (dd - 1) * kb - kb
        for h in range(nh):
            rows = jnp.broadcast_to(tab_ref[h:h + 1, s0:s0 + tq + kb], (kb, tq + kb))
            near_ref[dd, h] = pltpu.roll(rows, 0, 1, stride=1, stride_axis=0)[:, kb:]


def _bias_tiles(rel_bias, kb, tq):
    nh = rel_bias.shape[1]
    n_d = tq // kb + 1
    width = BIAS_CENTER + kb + tq + kb
    assert kb % LANE == 0 and tq % kb == 0 and kb >= MAX_DIST and BIAS_CENTER >= tq
    return pl.pallas_call(
        _bias_tiles_kernel,
        out_shape=[jax.ShapeDtypeStruct((nh, LANE), F32), jax.ShapeDtypeStruct((n_d, nh, kb, tq), F32)],
        scratch_shapes=[pltpu.VMEM((nh, width), F32)],
        name="bias_tiles",
    )(rel_bias.T)


def _sortable(x):
    x = jnp.where(x == 0.0, 0.0, x)
    bits = lax.bitcast_convert_type(x, I32)
    return jnp.where(bits < 0, bits ^ 0x7FFFFFFF, bits)


def _neg_inf_key():
    import numpy as np
    b = int(np.float32(NEG_INF).view(np.int32))
    return b ^ 0x7FFFFFFF


def _dsa_prompt_kernel(qt_ref, qit_ref, kwt_ref, kw_ref, k_ref, vt_ref, far_ref, near_ref, x_ref, wout_ref, o_ref,
                       key_ref, hi_ref, lo_ref, sel_ref, m_ref, l_ref, acc_ref, a_ref, *, top_k):
    qi = pl.program_id(1)
    tq = qt_ref.shape[2]
    n_keys = kw_ref.shape[1]
    kb_sz = LANE
    q0 = qi * tq
    nkb = jnp.minimum(n_keys, q0 + tq) // kb_sz
    negkey = _neg_inf_key()

    qlane = lax.broadcasted_iota(I32, (1, tq), 1)
    lim = ((q0 + qlane) // CHUNK + 1) * CHUNK
    krow = lax.broadcasted_iota(I32, (kb_sz, tq), 0)

    def kslice(kb):
        return pl.ds(pl.multiple_of(kb * kb_sz, kb_sz), kb_sz)

    sb = 2 * kb_sz
    srow = lax.broadcasted_iota(I32, (sb, tq), 0)

    def score_body(i, c):
        rows = pl.ds(pl.multiple_of(i * sb, sb), sb)
        kid = kw_ref[0, rows, :][:, :IDX_DIM].astype(BF16)
        sc = jnp.zeros((sb, tq), F32)
        for h in range(IDX_HEADS):
            s = jnp.dot(kid, qit_ref[0, h * IDX_DIM:(h + 1) * IDX_DIM, :], preferred_element_type=F32)
            sc = sc + jnp.maximum(s, 0.0) * kwt_ref[0, IDX_DIM + h:IDX_DIM + h + 1, :]
        sc = sc * ((IDX_DIM * IDX_HEADS) ** -0.5)
        key = jnp.where(i * sb + srow < lim, _sortable(sc), negkey)
        key_ref[rows, :] = key
        hi_ref[rows, :] = (key >> 16).astype(I16)
        lo_ref[rows, :] = ((key & 0xFFFF) - 0x8000).astype(I16)
        return c

    lax.fori_loop(0, nkb // 2, score_body, 0)

    def search16(ref):
        def bit_body(i, t_u):
            cand_u = t_u | jnp.left_shift(jnp.int32(1), 15 - i)
            cand = (cand_u - 0x8000).astype(I16)

            def body(j, a):
                ind = jnp.where(ref[pl.ds(pl.multiple_of(j * sb, sb), sb), :] >= cand,
                                jnp.ones((), I16), jnp.zeros((), I16))
                parts = [ind[16 * r:16 * (r + 1), :] for r in range(sb // 16)]
                while len(parts) > 1:
                    parts = [parts[r] + parts[r + 1] for r in range(0, len(parts), 2)]
                return a + parts[0]
            a = lax.fori_loop(0, nkb // 2, body, jnp.zeros((16, tq), I16))
            cnt = jnp.sum(a.astype(I32), axis=0, keepdims=True)
            return jnp.where(cnt >= top_k, cand_u, t_u)
        return lax.fori_loop(0, 16, bit_body, jnp.zeros((1, tq), I32))

    def count(pred_fn):
        def body(i, a):
            for u in range(2):
                kb = 2 * i + u
                ind = pred_fn(kb, key_ref[kslice(kb), :])
                a = a + jnp.sum(ind.reshape(kb_sz // 8, 8, tq), axis=0)
            return a
        a = lax.fori_loop(0, nkb // 2, body, jnp.zeros((8, tq), I32))
        return jnp.sum(a, axis=0, keepdims=True)

    t_hi = search16(hi_ref)
    t_hi16 = (t_hi - 0x8000).astype(I16)

    def lo_body(j, c):
        rows = pl.ds(pl.multiple_of(j * sb, sb), sb)
        hi = hi_ref[rows, :]
        lo_ref[rows, :] = jnp.where(hi == t_hi16, lo_ref[rows, :],
                                    jnp.where(hi > t_hi16, jnp.full((), 0x7FFF, I16), jnp.full((), -0x8000, I16)))
        return c

    lax.fori_loop(0, nkb // 2, lo_body, 0)
    t_s = (jnp.left_shift(t_hi, 16) | search16(lo_ref)) ^ INT_MIN

    def adm01(kb):
        return jnp.where(kb * kb_sz + krow < lim, 1.0, 0.0)

    def sel_body(kb, a):
        sel = jnp.where(key_ref[kslice(kb), :] >= t_s, adm01(kb), 0.0)
        sel_ref[kslice(kb), :] = sel
        return a + jnp.sum(sel.reshape(kb_sz // 8, 8, tq), axis=0)

    n_sel = jnp.sum(lax.fori_loop(0, nkb, sel_body, jnp.zeros((8, tq), F32)), axis=0, keepdims=True)

    @pl.when(jnp.max(n_sel) > top_k)
    def _():
        n_gt = count(lambda kb, key: jnp.where(key > t_s, 1, 0))
        need = (top_k - n_gt).astype(F32)
        r = lax.broadcasted_iota(I32, (kb_sz, kb_sz), 0)
        c = lax.broadcasted_iota(I32, (kb_sz, kb_sz), 1)
        ltri = jnp.where(c < r, 1.0, 0.0).astype(BF16)

        def tie_body(kb, carry):
            key = key_ref[kslice(kb), :]
            adm = adm01(kb)
            eq = jnp.where(key == t_s, adm, 0.0)
            rank = carry + jnp.dot(ltri, eq.astype(BF16), preferred_element_type=F32)
            keep = jnp.where(rank < need, eq, 0.0)
            sel_ref[kslice(kb), :] = jnp.where(key > t_s, adm, keep)
            return carry + jnp.sum(eq, axis=0, keepdims=True)

        lax.fori_loop(0, nkb, tie_body, jnp.zeros((1, tq), F32))

    m_ref[...] = jnp.full(m_ref.shape, NEG_INF, F32)
    l_ref[...] = jnp.zeros(l_ref.shape, F32)
    acc_ref[...] = jnp.zeros(acc_ref.shape, F32)
    c1 = (B_HD ** -0.5) * LOG2E

    def attend(k0, nk, bias2_fn):
        rows = pl.ds(pl.multiple_of(k0, LANE), nk)
        sel = sel_ref[rows, :] != 0.0
        ks = k_ref[0, rows, :]
        cols = []
        for h in range(B_HEADS):
            g = h // B_REP
            z = jnp.dot(ks[:, g * B_HD:(g + 1) * B_HD], qt_ref[0, h * B_HD:(h + 1) * B_HD, :],
                        preferred_element_type=F32)
            a = jnp.where(sel, z * c1 + bias2_fn(h), NEG_INF)
            a_ref[h, 0:nk, :] = a
            cols.append(jnp.max(a, axis=0, keepdims=True))
        m_old = m_ref[...]
        m_new = jnp.maximum(m_old, jnp.concatenate(cols, axis=0))
        alpha = jnp.exp2(m_old - m_new)
        m_ref[...] = m_new
        sums = []
        for h in range(B_HEADS):
            g = h // B_REP
            p = jnp.exp2(a_ref[h, 0:nk, :] - m_new[h:h + 1, :])
            sums.append(jnp.sum(p, axis=0, keepdims=True))
            hs = slice(h * B_HD, (h + 1) * B_HD)
            pv = jnp.dot(vt_ref[0, g * B_HD:(g + 1) * B_HD, rows], p.astype(BF16),
                         preferred_element_type=F32)
            acc_ref[hs, :] = alpha[h:h + 1, :] * acc_ref[hs, :] + pv
        l_ref[...] = alpha * l_ref[...] + jnp.concatenate(sums, axis=0)

    ab = near_ref.shape[2]
    n_far = jnp.maximum(q0 // ab - 1, 0)
    far_bias2 = far_ref[:, 0:1]

    def far_body(i, c):
        attend(i * ab, ab, lambda h: far_bias2[h:h + 1, :])
        return c

    lax.fori_loop(0, n_far, far_body, 0)

    def near_body(i, c):
        dd = i - q0 // ab + 1
        attend(i * ab, ab, lambda h: near_ref[dd, h])
        return c

    lax.fori_loop(n_far, nkb * kb_sz // ab, near_body, 0)

    _finish_heads(o_ref, x_ref, wout_ref, l_ref, acc_ref, B_HEADS)


def _dsa_attention_prompt(qt, qit, kwt, kw, k, vt, far, near, x, w_out):
    b, d, t = qt.shape
    tq = near.shape[3]
    top_k = min(TOPK_MAX, t // 4)
    ab = near.shape[2]
    assert t % tq == 0 and tq % CHUNK == 0 and tq % ab == 0 and ab % LANE == 0 and tq % (2 * LANE) == 0
    return pl.pallas_call(
        functools.partial(_dsa_prompt_kernel, top_k=top_k),
        grid=(b, t // tq),
        in_specs=[pl.BlockSpec((1, d, tq), lambda i, j: (i, 0, j)),
                  pl.BlockSpec((1, qit.shape[1], tq), lambda i, j: (i, 0, j)),
                  pl.BlockSpec((1, LANE, tq), lambda i, j: (i, 0, j)),
                  pl.BlockSpec((1, t, LANE), lambda i, j: (i, 0, 0)),
                  pl.BlockSpec((1, t, B_KV * B_HD), lambda i, j: (i, 0, 0)),
                  pl.BlockSpec((1, B_KV * B_HD, t), lambda i, j: (i, 0, 0)),
                  pl.BlockSpec(far.shape, lambda i, j: (0, 0)),
                  pl.BlockSpec(near.shape, lambda i, j: (0, 0, 0, 0)),
                  pl.BlockSpec((1, tq, d), lambda i, j: (i, j, 0)),
                  pl.BlockSpec((d, d), lambda i, j: (0, 0))],
        out_specs=pl.BlockSpec((1, tq, d), lambda i, j: (i, j, 0)),
        out_shape=jax.ShapeDtypeStruct((b, t, d), F32),
        scratch_shapes=[pltpu.VMEM((t, tq), I32), pltpu.VMEM((t, tq), I16), pltpu.VMEM((t, tq), I16),
                        pltpu.VMEM((t, tq), F32),
                        pltpu.VMEM((B_HEADS, tq), F32), pltpu.VMEM((B_HEADS, tq), F32),
                        pltpu.VMEM((d, tq), F32), pltpu.VMEM((B_HEADS, ab, tq), F32)],
        compiler_params=_cparams("parallel", "parallel"),
        name="dsa_attention",
    )(qt, qit, kwt, kw, k, vt, far, near, x, w_out)


def _dsa_cached_kernel(q_ref, qi_ref, wi_ref, rb_ref, kidx_ref, kp_ref, vp_ref, kn_ref, vn_ref, o_ref,
                       *, past, t_new, top_k):
    n_keys = past + t_new
    lp = kidx_ref.shape[1]
    negkey = _neg_inf_key()
    kpos = lax.broadcasted_iota(I32, (t_new, lp), 1)
    qpos = past + lax.broadcasted_iota(I32, (t_new, lp), 0)
    adm = kpos < (qpos // CHUNK + 1) * CHUNK

    s = _dot_nt(qi_ref[0], kidx_ref[0])
    w = jnp.maximum(s, 0.0) * wi_ref[0]
    sc = w[0:t_new]
    for h in range(1, IDX_HEADS):
        sc = sc + w[h * t_new:(h + 1) * t_new]
    sc = sc * ((IDX_DIM * IDX_HEADS) ** -0.5)
    key = jnp.where(adm, _sortable(sc), negkey)
    key = jnp.where(kpos < n_keys, key, INT_MIN)

    def bit_body(i, t_u):
        cand_u = t_u | jnp.left_shift(jnp.int32(1), 31 - i)
        cnt = jnp.sum(jnp.where(key >= (cand_u ^ INT_MIN), 1.0, 0.0), axis=1, keepdims=True)
        return jnp.where(cnt >= top_k, cand_u, t_u)

    t_s = lax.fori_loop(0, 32, bit_body, jnp.zeros((t_new, 1), I32)) ^ INT_MIN

    adm01 = jnp.where(adm, 1.0, 0.0)
    gt = jnp.where(key > t_s, adm01, 0.0)
    eq = jnp.where(key == t_s, adm01, 0.0)
    need = top_k - jnp.sum(jnp.where(key > t_s, 1.0, 0.0), axis=1, keepdims=True)
    r = lax.broadcasted_iota(I32, (LANE, LANE), 0)
    c = lax.broadcasted_iota(I32, (LANE, LANE), 1)
    utri = jnp.where(r < c, 1.0, 0.0).astype(BF16)
    carry = jnp.zeros((t_new, 1), F32)
    keeps = []
    for blk in range(lp // LANE):
        e = eq[:, blk * LANE:(blk + 1) * LANE]
        rank = carry + jnp.dot(e.astype(BF16), utri, preferred_element_type=F32)
        keeps.append(jnp.where(rank < need, e, 0.0))
        carry = carry + jnp.sum(e, axis=1, keepdims=True)
    sel = gt + jnp.concatenate(keeps, axis=1)

    near = max(past - MAX_DIST, 0) // LANE * LANE
    rel = (kpos - qpos)[:, near:]
    nb = N_BUCKETS // 2
    max_exact = nb // 2
    n = jnp.abs(rel)
    nf = jnp.maximum(n, 1).astype(F32)
    large = max_exact + (jnp.log(nf / max_exact) / math.log(MAX_DIST / max_exact)
                         * (nb - max_exact)).astype(I32)
    bucket = (rel > 0).astype(I32) * nb + jnp.where(n < max_exact, n, jnp.minimum(large, nb - 1))

    rows = B_REP * t_new
    sel_g = jnp.concatenate([sel] * B_REP, axis=0) != 0.0
    bucket_g = jnp.concatenate([bucket] * B_REP, axis=0)
    for g in range(B_KV):
        grp = lambda ref, n: ref[0, pl.ds(g, n, stride=B_KV), :]
        qg = q_ref[0, g * rows:(g + 1) * rows, :]
        rb = rb_ref[g * rows:(g + 1) * rows, :]
        bias_near = jnp.zeros((rows, lp - near), F32)
        for j in range(N_BUCKETS):
            bias_near = jnp.where(bucket_g == j, rb[:, j:j + 1], bias_near)
        bias = jnp.concatenate([jnp.broadcast_to(rb[:, nb - 1:nb], (rows, near)), bias_near], axis=1)
        zp = _dot_nt(qg, grp(kp_ref, past)) * (B_HD ** -0.5)
        zn = _dot_nt(qg, grp(kn_ref, t_new)) * (B_HD ** -0.5)
        ap = jnp.where(sel_g[:, :past], zp + bias[:, :past], NEG_INF)
        an = jnp.where(sel_g[:, past:n_keys], zn + bias[:, past:n_keys], NEG_INF)
        m = jnp.maximum(jnp.max(ap, axis=1, keepdims=True), jnp.max(an, axis=1, keepdims=True))
        pp, pn = jnp.exp(ap - m), jnp.exp(an - m)
        l = jnp.sum(pp, axis=1, keepdims=True) + jnp.sum(pn, axis=1, keepdims=True)
        o_ref[0, g * rows:(g + 1) * rows, :] = (_dot(pp, grp(vp_ref, past)) + _dot(pn, grp(vn_ref, t_new))) / l


def _dsa_attention_cached(q_rows, qi_rows, wi_col, rb_rows, kidx_all, k_past, v_past, k_new, v_new):
    b, rows, hd = q_rows.shape
    past, t_new = k_past.shape[1], k_new.shape[1]
    lp = kidx_all.shape[1]
    top_k = min(TOPK_MAX, (past + t_new) // 4)
    assert past % LANE == 0
    flat = lambda a: a.reshape(b, a.shape[1] * B_KV, B_HD)
    kv_spec = lambda n: pl.BlockSpec((1, n * B_KV, B_HD), lambda i: (i, 0, 0))
    return pl.pallas_call(
        functools.partial(_dsa_cached_kernel, past=past, t_new=t_new, top_k=top_k),
        grid=(b,),
        in_specs=[pl.BlockSpec((1, rows, hd), lambda i: (i, 0, 0)),
                  pl.BlockSpec((1,) + qi_rows.shape[1:], lambda i: (i, 0, 0)),
                  pl.BlockSpec((1,) + wi_col.shape[1:], lambda i: (i, 0, 0)),
                  pl.BlockSpec(rb_rows.shape, lambda i: (0, 0)),
                  pl.BlockSpec((1, lp, IDX_DIM), lambda i: (i, 0, 0)),
                  kv_spec(past), kv_spec(past), kv_spec(t_new), kv_spec(t_new)],
        out_specs=pl.BlockSpec((1, rows, hd), lambda i: (i, 0, 0)),
        out_shape=jax.ShapeDtypeStruct((b, rows, hd), F32),
        compiler_params=_cparams("parallel"),
        name="dsa_attention_cached",
    )(q_rows, qi_rows, wi_col, rb_rows, kidx_all, flat(k_past), flat(v_past), flat(k_new), flat(v_new))


def _cumsum_kernel(x_ref, o_ref):
    x = x_ref[0]
    n = x.shape[-1]
    lane = lax.broadcasted_iota(I32, x.shape, 1)
    s = 1
    while s < n:
        x = x + jnp.where(lane >= s, pltpu.roll(x, s, 1), 0.0)
        s *= 2
    o_ref[0] = x


def _cumsum_lanes(x):
    b, h, n = x.shape
    return pl.pallas_call(
        _cumsum_kernel,
        grid=(b,),
        in_specs=[pl.BlockSpec((1, h, n), lambda i: (i, 0, 0))],
        out_specs=pl.BlockSpec((1, h, n), lambda i: (i, 0, 0)),
        out_shape=jax.ShapeDtypeStruct((b, h, n), F32),
        compiler_params=_cparams("parallel"),
        name="logf_cumsum",
    )(x)


def _fox_init(m_ref, l_ref, acc_ref):
    m_ref[...] = jnp.full(m_ref.shape, NEG_INF, F32)
    l_ref[...] = jnp.zeros(l_ref.shape, F32)
    acc_ref[...] = jnp.zeros(acc_ref.shape, F32)


def _fox_tile(z_fn, pv_fn, cq, ck, mask, m_ref, l_ref, acc_ref, a_ref):
    c1 = (D_HD ** -0.5) * LOG2E
    cq2, ck2 = cq * LOG2E, ck * LOG2E
    cols = []
    for h in range(D_HEADS):
        a = z_fn(h) * c1 - ck2[:, h:h + 1]
        if mask is not None:
            a = jnp.where(mask, a, NEG_INF)
        a_ref[h] = a
        cols.append(jnp.max(a, axis=0, keepdims=True))
    m_old = m_ref[...]
    m_new = jnp.maximum(m_old, jnp.concatenate(cols, axis=0) + cq2)
    alpha = jnp.exp2(m_old - m_new)
    shift = m_new - cq2
    m_ref[...] = m_new
    sums = []
    for h in range(D_HEADS):
        p = jnp.exp2(a_ref[h] - shift[h:h + 1, :])
        sums.append(jnp.sum(p, axis=0, keepdims=True))
        hs = slice(h * D_HD, (h + 1) * D_HD)
        acc_ref[hs, :] = alpha[h:h + 1, :] * acc_ref[hs, :] + pv_fn(h, p.astype(BF16))
    l_ref[...] = alpha * l_ref[...] + jnp.concatenate(sums, axis=0)


def _hs(h):
    return slice(h * D_HD, (h + 1) * D_HD)


def _fox_prompt_kernel(qt_ref, k_ref, vt_ref, cq_ref, ck_ref, x_ref, wout_ref, o_ref,
                       m_ref, l_ref, acc_ref, a_ref):
    qi, step = pl.program_id(1), pl.program_id(2)
    tq, tk = qt_ref.shape[2], a_ref.shape[1]
    q0 = qi * tq
    ki = step - (pl.num_programs(2) - 1 - (q0 + tq - 1) // tk)
    k0 = ki * tk
    keys = pl.ds(pl.multiple_of(jnp.maximum(k0, 0), tk), tk)

    @pl.when(step == 0)
    def _():
        _fox_init(m_ref, l_ref, acc_ref)

    def run(masked):
        mask = None
        if masked:
            mask = (k0 + lax.broadcasted_iota(I32, (tk, tq), 0)) <= (q0 + lax.broadcasted_iota(I32, (tk, tq), 1))
        _fox_tile(lambda h: jnp.dot(k_ref[0, keys, _hs(h)], qt_ref[0, _hs(h), :], preferred_element_type=F32),
                  lambda h, p: jnp.dot(vt_ref[0, _hs(h), keys], p, preferred_element_type=F32),
                  cq_ref[0], ck_ref[0, keys, :], mask, m_ref, l_ref, acc_ref, a_ref)

    fully_visible = k0 + tk - 1 <= q0
    pl.when(jnp.logical_and(ki >= 0, fully_visible))(lambda: run(False))
    pl.when(jnp.logical_and(ki >= 0, jnp.logical_not(fully_visible)))(lambda: run(True))

    @pl.when(step == pl.num_programs(2) - 1)
    def _():
        _finish_heads(o_ref, x_ref, wout_ref, l_ref, acc_ref, D_HEADS)


def _fox_attention_prompt(qt, k, vt, cum_t, cum, x, w_out, tq, tk):
    b, d, t = qt.shape
    nq, nk = t // tq, t // tk
    return pl.pallas_call(
        _fox_prompt_kernel,
        grid=(b, nq, nk),
        in_specs=[pl.BlockSpec((1, d, tq), lambda i, j, kk: (i, 0, j)),
                  pl.BlockSpec((1, t, d), lambda i, j, kk: (i, 0, 0)),
                  pl.BlockSpec((1, d, t), lambda i, j, kk: (i, 0, 0)),
                  pl.BlockSpec((1, D_HEADS, tq), lambda i, j, kk: (i, 0, j)),
                  pl.BlockSpec((1, t, D_HEADS), lambda i, j, kk: (i, 0, 0)),
                  pl.BlockSpec((1, tq, d), lambda i, j, kk: (i, j, 0)),
                  pl.BlockSpec((d, d), lambda i, j, kk: (0, 0))],
        out_specs=pl.BlockSpec((1, tq, d), lambda i, j, kk: (i, j, 0)),
        out_shape=jax.ShapeDtypeStruct((b, t, d), F32),
        scratch_shapes=[pltpu.VMEM((D_HEADS, tq), F32), pltpu.VMEM((D_HEADS, tq), F32),
                        pltpu.VMEM((d, tq), F32), pltpu.VMEM((D_HEADS, tk, tq), F32)],
        compiler_params=_cparams("parallel", "parallel", "arbitrary"),
        name="fox_attention",
    )(qt, k, vt, cum_t, cum, x, w_out)


def _fox_cached_kernel(q_ref, kp_ref, vp_ref, kn_ref, vn_ref, cq_ref, ckp_ref, ckn_ref, o_ref,
                       m_ref, l_ref, acc_ref, *, t_new):
    ki = pl.program_id(1)
    n_past = pl.num_programs(1) - 1
    rows = q_ref.shape[1]
    c1 = (D_HD ** -0.5) * LOG2E

    @pl.when(ki == 0)
    def _():
        m_ref[...] = jnp.full(m_ref.shape, NEG_INF, F32)
        l_ref[...] = jnp.zeros(l_ref.shape, F32)
        acc_ref[...] = jnp.zeros(acc_ref.shape, F32)

    def tile(k2d, v2d, ck_row, causal):
        cols = k2d.shape[0]
        a = _dot_nt(q_ref[0], k2d) * c1 - ck_row * LOG2E
        r = lax.broadcasted_iota(I32, (rows, cols), 0)
        c = lax.broadcasted_iota(I32, (rows, cols), 1)
        ok = (c % D_HEADS) == (r // t_new)
        if causal:
            ok = jnp.logical_and(ok, (c // D_HEADS) <= (r % t_new))
        a = jnp.where(ok, a, NEG_INF)
        cq2 = cq_ref[0] * LOG2E
        m_old = m_ref[...]
        m_new = jnp.maximum(m_old, jnp.max(a, axis=1, keepdims=True) + cq2)
        alpha = jnp.exp2(m_old - m_new)
        p = jnp.exp2(a - (m_new - cq2))
        l_ref[...] = alpha * l_ref[...] + jnp.sum(p, axis=1, keepdims=True)
        acc_ref[...] = alpha * acc_ref[...] + _dot(p, v2d)
        m_ref[...] = m_new

    @pl.when(ki < n_past)
    def _():
        tk = kp_ref.shape[1]
        tile(kp_ref[0].reshape(tk * D_HEADS, D_HD), vp_ref[0].reshape(tk * D_HEADS, D_HD), ckp_ref[0], False)

    @pl.when(ki == n_past)
    def _():
        tile(kn_ref[0].reshape(t_new * D_HEADS, D_HD), vn_ref[0].reshape(t_new * D_HEADS, D_HD), ckn_ref[0], True)
        o_ref[0] = acc_ref[...] / l_ref[...]


def _fox_attention_cached(q_rows, k_past, v_past, k_new, v_new, cq_col, ck_past, ck_new, tk):
    b, rows, hd = q_rows.shape
    past, t_new = k_past.shape[1], k_new.shape[1]
    n_past = past // tk
    pidx = lambda i, kk: (i, jnp.minimum(kk, n_past - 1), 0, 0)
    return pl.pallas_call(
        functools.partial(_fox_cached_kernel, t_new=t_new),
        grid=(b, n_past + 1),
        in_specs=[pl.BlockSpec((1, rows, hd), lambda i, kk: (i, 0, 0)),
                  pl.BlockSpec((1, tk, D_HEADS, D_HD), pidx),
                  pl.BlockSpec((1, tk, D_HEADS, D_HD), pidx),
                  pl.BlockSpec((1, t_new, D_HEADS, D_HD), lambda i, kk: (i, 0, 0, 0)),
                  pl.BlockSpec((1, t_new, D_HEADS, D_HD), lambda i, kk: (i, 0, 0, 0)),
                  pl.BlockSpec((1, rows, 1), lambda i, kk: (i, 0, 0)),
                  pl.BlockSpec((1, 1, tk * D_HEADS), lambda i, kk: (i, 0, jnp.minimum(kk, n_past - 1))),
                  pl.BlockSpec((1, 1, t_new * D_HEADS), lambda i, kk: (i, 0, 0))],
        out_specs=pl.BlockSpec((1, rows, hd), lambda i, kk: (i, 0, 0)),
        out_shape=jax.ShapeDtypeStruct((b, rows, hd), F32),
        scratch_shapes=[pltpu.VMEM((rows, 1), F32), pltpu.VMEM((rows, 1), F32), pltpu.VMEM((rows, hd), F32)],
        compiler_params=_cparams("parallel", "arbitrary"),
        name="fox_attention_cached",
    )(q_rows, k_past, v_past, k_new, v_new, cq_col, ck_past, ck_new)


def _pad_rows(a, rows):
    if a.shape[1] == rows:
        return a
    return jnp.pad(a, ((0, 0), (0, rows - a.shape[1])) + ((0, 0),) * (a.ndim - 2))


def _round_up(n, m):
    return -(-n // m) * m


def _dsa_mixer(x, g, k_past, v_past, ki_past, w, w_out, rel_bias):
    b, t, d = x.shape
    past = k_past.shape[1]
    n_keys = past + t
    if past == 0:
        qt, k4, kb, v4, vt, qit, kw, kwt, ki_t = _proj(
            x, g, w, ((0, "t", BF16, None), (1, B_HD, F32, None), (1, "rows", BF16, None),
                      (2, B_HD, F32, None), (2, "t", BF16, None), (3, "t", BF16, None),
                      (4, "rows", F32, None), (4, "t", F32, None), (4, ("first_t", IDX_DIM), F32, None)),
            jnp.zeros((1, LANE), F32))
        far, near = _bias_tiles(rel_bias, 2 * LANE, 2 * LANE)
        y = _dsa_attention_prompt(qt, qit, kwt, kw, kb, vt, far, near, x, w_out)
        return (y, k4, v4, jnp.swapaxes(ki_t, 1, 2))
    q, k4, v4, qidx, kw = _proj(
        x, g, w, ((0, "rows", BF16, None), (1, B_HD, F32, None), (2, B_HD, F32, None),
                  (3, "rows", BF16, None), (4, "rows", F32, None)),
        jnp.zeros((1, LANE), F32))
    ki = kw[:, :, :IDX_DIM]
    to_rows = lambda a, nh: jnp.swapaxes(a.reshape(b, t, nh, -1), 1, 2).reshape(b, nh * t, -1)
    kidx_all = _pad_rows(jnp.concatenate([ki_past, ki], axis=1), _round_up(n_keys, LANE))
    o = _dsa_attention_cached(to_rows(q, B_HEADS), to_rows(qidx, IDX_HEADS),
                              to_rows(kw[:, :, IDX_DIM:IDX_DIM + IDX_HEADS], IDX_HEADS),
                              jnp.repeat(rel_bias.T, t, axis=0), kidx_all, k_past, v_past, k4, v4)
    o = jnp.swapaxes(o.reshape(b, B_HEADS, t, B_HD), 1, 2).reshape(b * t, d)
    y = _out_proj(x.reshape(b * t, d), o, w_out).reshape(b, t, d)
    return (y, k4, v4, ki)


def _fox_mixer(x, g, k_past, v_past, lf_past, w, b_f, w_out):
    b, t, d = x.shape
    past = k_past.shape[1]
    heads4 = ((1, D_HD, F32, None), (2, D_HD, F32, None))
    if past == 0:
        tq = tk = min(4 * LANE, t)
        k4, v4, logf_t, qt, kb, vt = _proj(
            x, g, w, heads4 + ((3, ("first_t", D_HEADS), F32, "log_sigmoid"), (0, "t", BF16, None),
                               (1, "rows", BF16, None), (2, "t", BF16, None)), b_f)
        cum_t = _cumsum_lanes(logf_t)
        y = _fox_attention_prompt(qt, kb, vt, cum_t, jnp.swapaxes(cum_t, 1, 2), x, w_out, tq, tk)
        return (y, k4, v4, jnp.swapaxes(logf_t, 1, 2))
    else:
        tk = math.gcd(past, 4 * LANE)
        k4, v4, logf, q = _proj(
            x, g, w, heads4 + ((3, ("first", D_HEADS), F32, "log_sigmoid"), (0, "rows", BF16, None)), b_f)
        lf_all = _pad_rows(jnp.concatenate([lf_past, logf], axis=1), _round_up(past + t, LANE))
        cum = jnp.swapaxes(_cumsum_lanes(jnp.swapaxes(lf_all, 1, 2)), 1, 2)[:, :past + t]
        ck = cum.reshape(b, 1, (past + t) * D_HEADS)
        to_rows = lambda a: jnp.swapaxes(a.reshape(b, t, D_HEADS, -1), 1, 2).reshape(b, D_HEADS * t, -1)
        o = _fox_attention_cached(to_rows(q), k_past, v_past, k4, v4, to_rows(cum[:, past:]),
                                  ck[:, :, :past * D_HEADS], ck[:, :, past * D_HEADS:], tk)
        o = jnp.swapaxes(o.reshape(b, D_HEADS, t, D_HD), 1, 2).reshape(b, t, d)
    y = _out_proj(x.reshape(b * t, d), o.reshape(b * t, d), w_out).reshape(b, t, d)
    return (y, k4, v4, logf)


def _run_group(x, pos0, a_st, b_k, b_v, b_ki, c_st, d_k, d_v, d_lf, mem_k, mem_v, prm):
    b, t, d = x.shape
    depth = prm["norm_mix"].shape[0]
    new = {n: [] for n in ("a", "bk", "bv", "bki", "c", "dk", "dv", "dlf")}
    for i in range(depth):
        kind, j = i % 4, i // 4
        g = prm["norm_mix"][i]
        if kind == 0:
            x, st = _conv_mixer(x, g, prm["a_w_in"][j], prm["a_conv"][j], a_st[j], prm["a_w_out"][j])
            new["a"].append(st)
        elif kind == 1:
            x, kk, vv, ki = _dsa_mixer(x, g, b_k[j], b_v[j], b_ki[j], prm["b_w"][j], prm["b_w_out"][j],
                                       prm["rel_bias"])
            new["bk"].append(kk); new["bv"].append(vv); new["bki"].append(ki)
        elif kind == 2:
            x, st = _pool_mixer(x, g, c_st[j], prm["c_w_group"][j], prm["c_scale"][j], pos0)
            new["c"].append(st)
        else:
            x, kk, vv, lf = _fox_mixer(x, g, d_k[j], d_v[j], d_lf[j], prm["d_w"][j], prm["d_b_f"][j],
                                       prm["d_w_out"][j])
            new["dk"].append(kk); new["dv"].append(vv); new["dlf"].append(lf)
        x = _xattn(x, prm["norm_xattn"], prm["xa_wq"], mem_k, mem_v, prm["xa_wo"], i)
        last = i == depth - 1
        x = _ffn(x.reshape(b * t, d), prm["norm_ffn"], prm["ffn_w1"], prm["ffn_w2"],
                 prm["final_norm"], i, last).reshape(b, t, d)
    return (x,) + tuple(jnp.stack(new[n]) for n in ("a", "bk", "bv", "bki", "c", "dk", "dv", "dlf"))


def kernel(x_prompt, x_sample, state_a_conv, cache_b_k, cache_b_v, cache_b_kidx, state_c_pool,
           cache_d_k, cache_d_v, cache_d_logf, cache_mem_k, cache_mem_v, mem_prompt,
           norm_mix, norm_xattn, norm_mem, norm_ffn, final_norm,
           a_w_in, a_conv, a_w_out, b_w_in, b_w_out, rel_bias, c_w_group, c_scale,
           d_w_in, d_b_f, d_w_out, xa_wq, xa_wkv, xa_wo, ffn_w1, ffn_w2):
    bp, t, d = x_prompt.shape
    depth = norm_mix.shape[0]
    n_b, n_d = b_w_in.shape[0], d_w_in.shape[0]
    bf = lambda w: w.astype(BF16)

    def split_cols(w, widths):
        out, c = [], 0
        for wd in widths:
            piece = w[:, c:c + wd]
            c += wd
            if wd % LANE:
                piece = jnp.pad(piece, ((0, 0), (0, _round_up(wd, LANE) - wd)))
            out.append(bf(piece))
        assert c == w.shape[1]
        return out

    b_q, b_kvw = B_HEADS * B_HD, B_KV * B_HD
    b_w = [split_cols(b_w_in[j], (b_q, b_kvw, b_kvw, IDX_HEADS * IDX_DIM, IDX_DIM + IDX_HEADS))
           for j in range(n_b)]
    d_w = [split_cols(d_w_in[j], (d, d, d, D_HEADS)) for j in range(n_d)]
    d_bf = [jnp.pad(d_b_f[j], (0, LANE - D_HEADS)).reshape(1, LANE) for j in range(n_d)]

    prm = {"norm_mix": norm_mix, "norm_xattn": norm_xattn, "norm_ffn": norm_ffn, "final_norm": final_norm,
           "a_w_in": bf(a_w_in), "a_conv": a_conv, "a_w_out": bf(a_w_out),
           "b_w": b_w, "b_w_out": bf(b_w_out), "rel_bias": rel_bias,
           "c_w_group": bf(c_w_group), "c_scale": c_scale,
           "d_w": d_w, "d_b_f": d_bf, "d_w_out": bf(d_w_out),
           "xa_wq": bf(xa_wq), "xa_wo": bf(xa_wo), "ffn_w1": bf(ffn_w1), "ffn_w2": bf(ffn_w2)}

    n_mem = mem_prompt.shape[1]
    mk, mv, mk_rows, mv_rows = _memory_kv(mem_prompt, norm_mem, bf(xa_wkv))

    n_a, n_c = a_w_in.shape[0], c_w_group.shape[0]
    z = lambda *s: jnp.zeros(s, F32)
    gp = _run_group(x_prompt, 0,
                    z(n_a, bp, CONV_W - 1, d),
                    z(n_b, bp, 0, B_KV, B_HD), z(n_b, bp, 0, B_KV, B_HD), z(n_b, bp, 0, IDX_DIM),
                    z(n_c, bp, POOL_STATE, d),
                    z(n_d, bp, 0, D_HEADS, D_HD), z(n_d, bp, 0, D_HEADS, D_HD), z(n_d, bp, 0, D_HEADS),
                    mk_rows, mv_rows, prm)

    bs = x_sample.shape[0]
    past_len = cache_b_k.shape[2]
    gs = _run_group(x_sample, past_len, state_a_conv, cache_b_k, cache_b_v, cache_b_kidx, state_c_pool,
                    cache_d_k, cache_d_v, cache_d_logf, cache_mem_k, cache_mem_v, prm)

    (y_p, a_p, bk_p, bv_p, bki_p, c_p, dk_p, dv_p, dlf_p) = gp
    (y_s, a_s, bk_s, bv_s, bki_s, c_s, dk_s, dv_s, dlf_s) = gs
    return (y_p, y_s, a_p, a_s, bk_p, bv_p, bki_p, bk_s, bv_s, bki_s, c_p, c_s,
            dk_p, dv_p, dlf_p, dk_s, dv_s, dlf_s, mk, mv)
```

```python
import functools
import math

import jax
import jax.numpy as jnp
from jax import lax
from jax.experimental import pallas as pl
from jax.experimental.pallas import tpu as pltpu

F32 = jnp.float32
BF16 = jnp.bfloat16
I32 = jnp.int32
I16 = jnp.int16

EPS = 1e-6
NEG_INF = -1e30
LOG2E = math.log2(math.e)
CHUNK = 64
LANE = 128
VMEM_LIMIT = 48 * 1024 * 1024

CONV_W = 3
POOL_WINDOWS = (2, 4, 8, 16)
POOL_STATE = max(POOL_WINDOWS) - 1
B_HEADS, B_KV, B_HD = 8, 2, 128
B_REP = B_HEADS // B_KV
IDX_HEADS, IDX_DIM = 8, 64
TOPK_MAX = 256
N_BUCKETS, MAX_DIST = 32, 128
D_HEADS, D_HD = 8, 128
MEM_HEADS = 4
INT_MIN = -2147483648
BIAS_CENTER = 2 * LANE


def _cparams(*sem):
    return pltpu.CompilerParams(dimension_semantics=sem, vmem_limit_bytes=VMEM_LIMIT)


def _dot(a, b):
    return jnp.dot(a.astype(BF16), b.astype(BF16), preferred_element_type=F32)


def _dot_nt(a, b):
    return lax.dot_general(a.astype(BF16), b.astype(BF16), (((1,), (1,)), ((), ())),
                           preferred_element_type=F32)


def _dot_tn(a, b):
    return lax.dot_general(a.astype(BF16), b.astype(BF16), (((0,), (0,)), ((), ())),
                           preferred_element_type=F32)


def _rms(x, g):
    return x * lax.rsqrt(jnp.mean(x * x, axis=-1, keepdims=True) + EPS) * g


def _finish_heads(o_ref, x_ref, w_ref, l_ref, acc_ref, n_heads):
    hd = acc_ref.shape[0] // n_heads
    inv_l = 1.0 / l_ref[...]
    heads_t = jnp.concatenate([acc_ref[h * hd:(h + 1) * hd, :] * inv_l[h:h + 1, :] for h in range(n_heads)],
                              axis=0)
    o_ref[0] = x_ref[0] + _dot_tn(heads_t, w_ref[...])


def _row_tile(n, cap):
    t = min(n, cap)
    assert n % t == 0
    return t


def _memkv_kernel(mem_ref, g_ref, w_ref, k_ref, v_ref, kb_ref, vb_ref):
    bb, nm, d = mem_ref.shape
    m = mem_ref[...].reshape(bb * nm, d)
    mn = m * lax.rsqrt(jnp.mean(m * m, axis=-1, keepdims=True) + EPS)
    h = (mn * g_ref[0]).astype(BF16)
    hd = d // MEM_HEADS
    k = jnp.dot(h, w_ref[0, :, :d].astype(BF16), preferred_element_type=F32)
    v = jnp.dot(h, w_ref[0, :, d:].astype(BF16), preferred_element_type=F32)
    for i in range(bb):
        rows = slice(i * nm, (i + 1) * nm)
        kb_ref[0, i] = k[rows].astype(BF16)
        vb_ref[0, i] = v[rows].astype(BF16)
        k_ref[0, i] = pltpu.einshape("m(hd)->mhd", k[rows], d=hd)
        v_ref[0, i] = pltpu.einshape("m(hd)->mhd", v[rows], d=hd)


def _memory_kv(mem, g_mem, w_kv):
    depth, d = g_mem.shape
    b, nm, _ = mem.shape
    hd = d // MEM_HEADS
    out = jax.ShapeDtypeStruct((depth, b, nm, MEM_HEADS, hd), F32)
    out_b = jax.ShapeDtypeStruct((depth, b, nm, d), BF16)
    bb = math.gcd(b, 2)
    heads_spec = pl.BlockSpec((1, bb, nm, MEM_HEADS, hd), lambda l, i: (l, i, 0, 0, 0))
    rows_spec = pl.BlockSpec((1, bb, nm, d), lambda l, i: (l, i, 0, 0))
    return pl.pallas_call(
        _memkv_kernel,
        grid=(depth, b // bb),
        in_specs=[pl.BlockSpec((bb, nm, d), lambda l, i: (i, 0, 0)),
                  pl.BlockSpec((1, 1, d), lambda l, i: (l, 0, 0)),
                  pl.BlockSpec((1, d, 2 * d), lambda l, i: (l, 0, 0))],
        out_specs=[heads_spec, heads_spec, rows_spec, rows_spec],
        out_shape=[out, out, out_b, out_b],
        compiler_params=_cparams("parallel", "parallel"),
        name="memory_kv",
    )(mem, g_mem.reshape(depth, 1, d), w_kv)


def _ffn_kernel(x_ref, g_ref, w1_ref, w2_ref, gf_ref, o_ref, h_ref, acc_ref, *, final_norm):
    j = pl.program_id(1)

    @pl.when(j == 0)
    def _():
        h_ref[...] = _rms(x_ref[...], g_ref[...]).astype(BF16)
        acc_ref[...] = jnp.zeros_like(acc_ref)

    u = jnp.maximum(jnp.dot(h_ref[...], w1_ref[...].astype(BF16), preferred_element_type=F32), 0.0)
    acc_ref[...] += jnp.dot((u * u).astype(BF16), w2_ref[...].astype(BF16), preferred_element_type=F32)

    @pl.when(j == pl.num_programs(1) - 1)
    def _():
        y = x_ref[...] + acc_ref[...]
        o_ref[...] = _rms(y, gf_ref[...]) if final_norm else y


def _ffn(x, g, w1, w2, gf, layer, final_norm):
    n, d = x.shape
    f = w1.shape[2]
    tm = _row_tile(n, 1024)
    tf = 1024
    return pl.pallas_call(
        functools.partial(_ffn_kernel, final_norm=final_norm),
        grid=(n // tm, f // tf),
        in_specs=[pl.BlockSpec((tm, d), lambda i, j: (i, 0)),
                  pl.BlockSpec((None, 1, d), lambda i, j: (layer, 0, 0)),
                  pl.BlockSpec((None, d, tf), lambda i, j: (layer, 0, j)),
                  pl.BlockSpec((None, tf, d), lambda i, j: (layer, j, 0)),
                  pl.BlockSpec((1, d), lambda i, j: (0, 0))],
        out_specs=pl.BlockSpec((tm, d), lambda i, j: (i, 0)),
        out_shape=jax.ShapeDtypeStruct((n, d), F32),
        scratch_shapes=[pltpu.VMEM((tm, d), BF16), pltpu.VMEM((tm, d), F32)],
        compiler_params=_cparams("parallel", "arbitrary"),
        name="ffn",
    )(x, g.reshape(-1, 1, d), w1, w2, gf.reshape(1, d))


def _xattn_kernel(x_ref, g_ref, wq_ref, mk_ref, mv_ref, wo_ref, o_ref):
    x = x_ref[0]
    d = x.shape[-1]
    hd = d // MEM_HEADS
    h = _rms(x, g_ref[...]).astype(BF16)
    q = jnp.dot(h, wq_ref[...].astype(BF16), preferred_element_type=F32)
    outs = []
    if len(mk_ref.shape) == 3:
        mk = pltpu.einshape("mhd->m(hd)", mk_ref[...]).astype(BF16)
        mv = pltpu.einshape("mhd->m(hd)", mv_ref[...]).astype(BF16)
    else:
        mk, mv = mk_ref[...], mv_ref[...]
    for hh in range(MEM_HEADS):
        sl = slice(hh * hd, (hh + 1) * hd)
        kh, vh = mk[:, sl], mv[:, sl]
        s = _dot_nt(q[:, sl], kh) * (hd ** -0.5)
        m = jnp.max(s, axis=-1, keepdims=True)
        p = jnp.exp(s - m)
        l = jnp.sum(p, axis=-1, keepdims=True)
        outs.append(_dot(p, vh) / l)
    o = jnp.concatenate(outs, axis=-1)
    o_ref[0] = x + _dot(o, wo_ref[...])


def _xattn(x, g, wq, mk, mv, wo, layer):
    b, t, d = x.shape
    tm = _row_tile(t, 1024)
    kv_spec = pl.BlockSpec((None, None) + mk.shape[2:], lambda i, j: (layer, i) + (0,) * (mk.ndim - 2))
    return pl.pallas_call(
        _xattn_kernel,
        grid=(b, t // tm),
        in_specs=[pl.BlockSpec((1, tm, d), lambda i, j: (i, j, 0)),
                  pl.BlockSpec((None, 1, d), lambda i, j: (layer, 0, 0)),
                  pl.BlockSpec((None, d, d), lambda i, j: (layer, 0, 0)),
                  kv_spec, kv_spec,
                  pl.BlockSpec((None, d, d), lambda i, j: (layer, 0, 0))],
        out_specs=pl.BlockSpec((1, tm, d), lambda i, j: (i, j, 0)),
        out_shape=jax.ShapeDtypeStruct((b, t, d), F32),
        compiler_params=_cparams("parallel", "parallel"),
        name="xattn",
    )(x, g.reshape(-1, 1, d), wq, mk, mv, wo)


def _conv_kernel(x_ref, g_ref, win_ref, wc_ref, st_ref, wout_ref, o_ref, nst_ref, z_ref):
    t = pl.program_id(1)
    x = x_ref[0]
    tm, d = x.shape
    pad = 8

    @pl.when(t == 0)
    def _():
        z_ref[pad - 2:pad, :] = st_ref[0]

    h = _rms(x, g_ref[...]).astype(BF16)
    bg = jnp.dot(h, win_ref[:, 0:d], preferred_element_type=F32)
    cg = jnp.dot(h, win_ref[:, d:2 * d], preferred_element_type=F32)
    u = jnp.dot(h, win_ref[:, 2 * d:3 * d], preferred_element_type=F32)
    z = cg * u
    z_ref[pad:pad + tm, :] = z
    conv = (z_ref[pad - 2:pad - 2 + tm, :] * wc_ref[0:1, :]
            + z_ref[pad - 1:pad - 1 + tm, :] * wc_ref[1:2, :]
            + z * wc_ref[2:3, :])
    o_ref[0] = x + _dot(bg * conv, wout_ref[...])
    last = z_ref[pad + tm - 2:pad + tm, :]
    z_ref[pad - 2:pad, :] = last

    @pl.when(t == pl.num_programs(1) - 1)
    def _():
        nst_ref[0] = last


def _conv_mixer(x, g, w_in, w_conv, state, w_out):
    b, t, d = x.shape
    tm = _row_tile(t, 1024)
    once = pl.Buffered(1)
    return pl.pallas_call(
        _conv_kernel,
        grid=(b, t // tm),
        in_specs=[pl.BlockSpec((1, tm, d), lambda i, j: (i, j, 0)),
                  pl.BlockSpec((1, d), lambda i, j: (0, 0)),
                  pl.BlockSpec((d, 3 * d), lambda i, j: (0, 0), pipeline_mode=once),
                  pl.BlockSpec((CONV_W, d), lambda i, j: (0, 0)),
                  pl.BlockSpec((1, CONV_W - 1, d), lambda i, j: (i, 0, 0)),
                  pl.BlockSpec((d, d), lambda i, j: (0, 0), pipeline_mode=once)],
        out_specs=[pl.BlockSpec((1, tm, d), lambda i, j: (i, j, 0)),
                   pl.BlockSpec((1, CONV_W - 1, d), lambda i, j: (i, 0, 0))],
        out_shape=[jax.ShapeDtypeStruct((b, t, d), F32),
                   jax.ShapeDtypeStruct((b, CONV_W - 1, d), F32)],
        scratch_shapes=[pltpu.VMEM((tm + 8, d), F32)],
        compiler_params=_cparams("parallel", "arbitrary"),
        name="conv_mixer",
    )(x, g.reshape(1, d), w_in, w_conv, state, w_out)


def _pool_kernel(x_ref, g_ref, st_ref, wg_ref, sc_ref, o_ref, nst_ref, h_ref, *s_refs, pos0):
    t = pl.program_id(1)
    x = x_ref[0]
    tm, d = x.shape
    n_lv = len(POOL_WINDOWS)
    gw = d // n_lv
    base = 2 * (POOL_STATE + 1)
    lead = base - POOL_STATE
    end = base + tm

    @pl.when(t == 0)
    def _():
        h_ref[0:lead, :] = jnp.zeros((lead, d), F32)
        h_ref[lead:base, :] = st_ref[0]

    h = _rms(x, g_ref[...])
    h_ref[base:end, :] = h
    pos = pos0 + t * tm + lax.broadcasted_iota(I32, (tm, gw), 0)
    ys = []
    prev, c_prev = h_ref, 0
    for lv in range(1, n_lv + 1):
        w, shift, start = POOL_WINDOWS[lv - 1], 2 ** (lv - 1), 8 * lv
        c0 = (lv - 1) * gw
        cols = slice(c0 - c_prev, d - c_prev)
        cur = prev[start:end, cols] + prev[start - shift:end - shift, cols]
        if lv < n_lv:
            s_refs[lv - 1][start:end, :] = cur[:, gw:]
        win = cur[base - start:, :gw]
        count = jnp.minimum(w, pos + 1).astype(F32)
        dlt = win / count - h[:, c0:c0 + gw]
        ys.append(_dot(dlt, wg_ref[lv - 1]))
        if lv < n_lv:
            prev, c_prev = s_refs[lv - 1], c0 + gw
    y = jnp.concatenate(ys, axis=-1) * sc_ref[...]
    o_ref[0] = x + y
    last = h_ref[end - POOL_STATE:end, :]
    h_ref[lead:base, :] = last

    @pl.when(t == pl.num_programs(1) - 1)
    def _():
        nst_ref[0] = last


def _pool_mixer(x, g, state, w_group, scale, pos0):
    b, t, d = x.shape
    ng, gw, _ = w_group.shape
    tm = _row_tile(t, 512)
    assert POOL_WINDOWS == tuple(2 ** (lv + 1) for lv in range(ng)) and tm >= POOL_STATE
    rows = tm + 2 * (POOL_STATE + 1)
    return pl.pallas_call(
        functools.partial(_pool_kernel, pos0=pos0),
        grid=(b, t // tm),
        in_specs=[pl.BlockSpec((1, tm, d), lambda i, j: (i, j, 0)),
                  pl.BlockSpec((1, d), lambda i, j: (0, 0)),
                  pl.BlockSpec((1, POOL_STATE, d), lambda i, j: (i, 0, 0)),
                  pl.BlockSpec((ng, gw, gw), lambda i, j: (0, 0, 0)),
                  pl.BlockSpec((1, d), lambda i, j: (0, 0))],
        out_specs=[pl.BlockSpec((1, tm, d), lambda i, j: (i, j, 0)),
                   pl.BlockSpec((1, POOL_STATE, d), lambda i, j: (i, 0, 0))],
        out_shape=[jax.ShapeDtypeStruct((b, t, d), F32),
                   jax.ShapeDtypeStruct((b, POOL_STATE, d), F32)],
        scratch_shapes=[pltpu.VMEM((rows, d - lv * gw), F32) for lv in range(ng)],
        compiler_params=_cparams("parallel", "arbitrary"),
        name="pool_mixer",
    )(x, g.reshape(1, d), state, w_group, scale.reshape(1, d))


def _proj_kernel(*refs, n_w, outs):
    x_ref, g_ref = refs[0], refs[1]
    w_refs = refs[2:2 + n_w]
    e_ref = refs[2 + n_w]
    o_refs = refs[3 + n_w:]
    h = _rms(x_ref[0], g_ref[...]).astype(BF16)
    ys = {}
    for (wi, mode, _, ep), o_ref in zip(outs, o_refs):
        if wi not in ys:
            ys[wi] = jnp.dot(h, w_refs[wi][...], preferred_element_type=F32)
        y = ys[wi]
        if ep == "log_sigmoid":
            u = -(y + e_ref[...])
            y = -(jnp.maximum(u, 0.0) + jnp.log1p(jnp.exp(-jnp.abs(u))))
        if mode == "rows":
            o_ref[0] = y.astype(o_ref.dtype)
        elif isinstance(mode, tuple) and mode[0] == "first":
            o_ref[0] = y[:, :mode[1]].astype(o_ref.dtype)
        elif isinstance(mode, tuple):
            o_ref[0] = jnp.transpose(y)[:mode[1], :].astype(o_ref.dtype)
        elif mode == "t":
            o_ref[0] = jnp.transpose(y).astype(o_ref.dtype)
        else:
            o_ref[0] = pltpu.einshape("m(hd)->mhd", y.astype(o_ref.dtype), d=mode)


def _proj(x, g, ws, outs, extra):
    b, t, d = x.shape
    tm = _row_tile(t, 512)
    in_specs = [pl.BlockSpec((1, tm, d), lambda i, j: (i, j, 0)), pl.BlockSpec((1, d), lambda i, j: (0, 0))]
    in_specs += [pl.BlockSpec(w.shape, lambda i, j: (0, 0)) for w in ws]
    in_specs += [pl.BlockSpec(extra.shape, lambda i, j: (0, 0))]
    out_specs, out_shape = [], []
    for wi, mode, dt, _ in outs:
        n = ws[wi].shape[1]
        if isinstance(mode, tuple):
            mode, n = ("rows" if mode[0] == "first" else "t"), mode[1]
        if mode == "rows":
            out_specs.append(pl.BlockSpec((1, tm, n), lambda i, j: (i, j, 0)))
            out_shape.append(jax.ShapeDtypeStruct((b, t, n), dt))
        elif mode == "t":
            out_specs.append(pl.BlockSpec((1, n, tm), lambda i, j: (i, 0, j)))
            out_shape.append(jax.ShapeDtypeStruct((b, n, t), dt))
        else:
            out_specs.append(pl.BlockSpec((1, tm, n // mode, mode), lambda i, j: (i, j, 0, 0)))
            out_shape.append(jax.ShapeDtypeStruct((b, t, n // mode, mode), dt))
    return pl.pallas_call(
        functools.partial(_proj_kernel, n_w=len(ws), outs=tuple(outs)),
        grid=(b, t // tm),
        in_specs=in_specs,
        out_specs=out_specs,
        out_shape=out_shape,
        compiler_params=_cparams("parallel", "parallel"),
        name="norm_proj",
    )(x, g.reshape(1, d), *ws, extra)


def _outproj_kernel(x_ref, a_ref, w_ref, o_ref):
    o_ref[...] = x_ref[...] + _dot(a_ref[...], w_ref[...])


def _out_proj(x, a, w):
    n, d = x.shape
    tm = _row_tile(n, 512)
    return pl.pallas_call(
        _outproj_kernel,
        grid=(n // tm,),
        in_specs=[pl.BlockSpec((tm, d), lambda i: (i, 0)),
                  pl.BlockSpec((tm, d), lambda i: (i, 0)),
                  pl.BlockSpec((d, d), lambda i: (0, 0))],
        out_specs=pl.BlockSpec((tm, d), lambda i: (i, 0)),
        out_shape=jax.ShapeDtypeStruct((n, d), F32),
        compiler_params=_cparams("parallel"),
        name="out_proj",
    )(x, a, w)


def _bias_table_kernel(rbt_ref, o_ref):
    width = o_ref.shape[-1]
    rel = BIAS_CENTER - lax.broadcasted_iota(I32, (1, width), 1)
    nb = N_BUCKETS // 2
    max_exact = nb // 2
    ret = (rel > 0).astype(I32) * nb
    n = jnp.abs(rel)
    nf = jnp.maximum(n, 1).astype(F32)
    large = max_exact + (jnp.log(nf / max_exact) / math.log(MAX_DIST / max_exact)
                         * (nb - max_exact)).astype(I32)
    large = jnp.minimum(large, nb - 1)
    bucket = ret + jnp.where(n < max_exact, n, large)
    acc = jnp.zeros(o_ref.shape, F32)
    for j in range(N_BUCKETS):
        acc = jnp.where(bucket == j, rbt_ref[:, j:j + 1], acc)
    o_ref[...] = acc * LOG2E


def _bias_tiles_kernel(rbt_ref, far_ref, near_ref, tab_ref):
    n_d, nh, kb, tq = near_ref.shape
    _bias_table_kernel(rbt_ref, tab_ref)
    far_ref[...] = jnp.broadcast_to(tab_ref[:, BIAS_CENTER + MAX_DIST:BIAS_CENTER + MAX_DIST + 1], far_ref.shape)
    for dd in range(n_d):
        s0 = BIAS_CENTER - (dd - 1) * kb - kb
        for h in range(nh):
            rows = jnp.broadcast_to(tab_ref[h:h + 1, s0:s0 + tq + kb], (kb, tq + kb))
            near_ref[dd, h] = pltpu.roll(rows, 0, 1, stride=1, stride_axis=0)[:, kb:]


def _bias_tiles(rel_bias, kb, tq):
    nh = rel_bias.shape[1]
    n_d = tq // kb + 1
    width = BIAS_CENTER + kb + tq + kb
    assert kb % LANE == 0 and tq % kb == 0 and kb >= MAX_DIST and BIAS_CENTER >= tq
    return pl.pallas_call(
        _bias_tiles_kernel,
        out_shape=[jax.ShapeDtypeStruct((nh, LANE), F32), jax.ShapeDtypeStruct((n_d, nh, kb, tq), F32)],
        scratch_shapes=[pltpu.VMEM((nh, width), F32)],
        name="bias_tiles",
    )(rel_bias.T)


def _sortable(x):
    x = jnp.where(x == 0.0, 0.0, x)
    bits = lax.bitcast_convert_type(x, I32)
    return jnp.where(bits < 0, bits ^ 0x7FFFFFFF, bits)


def _neg_inf_key():
    import numpy as np
    b = int(np.float32(NEG_INF).view(np.int32))
    return b ^ 0x7FFFFFFF


def _dsa_prompt_kernel(qt_ref, qit_ref, kwt_ref, kw_ref, k_ref, vt_ref, far_ref, near_ref, x_ref, wout_ref, o_ref,
                       key_ref, hi_ref, lo_ref, sel_ref, m_ref, l_ref, acc_ref, a_ref, *, top_k):
    qi = pl.program_id(1)
    tq = qt_ref.shape[2]
    n_keys = kw_ref.shape[1]
    kb_sz = LANE
    q0 = qi * tq
    nkb = jnp.minimum(n_keys, q0 + tq) // kb_sz
    negkey = _neg_inf_key()

    qlane = lax.broadcasted_iota(I32, (1, tq), 1)
    lim = ((q0 + qlane) // CHUNK + 1) * CHUNK
    krow = lax.broadcasted_iota(I32, (kb_sz, tq), 0)

    def kslice(kb):
        return pl.ds(pl.multiple_of(kb * kb_sz, kb_sz), kb_sz)

    sb = 2 * kb_sz
    srow = lax.broadcasted_iota(I32, (sb, tq), 0)

    def score_body(i, c):
        rows = pl.ds(pl.multiple_of(i * sb, sb), sb)
        kid = kw_ref[0, rows, :][:, :IDX_DIM].astype(BF16)
        sc = jnp.zeros((sb, tq), F32)
        for h in range(IDX_HEADS):
            s = jnp.dot(kid, qit_ref[0, h * IDX_DIM:(h + 1) * IDX_DIM, :], preferred_element_type=F32)
            sc = sc + jnp.maximum(s, 0.0) * kwt_ref[0, IDX_DIM + h:IDX_DIM + h + 1, :]
        sc = sc * ((IDX_DIM * IDX_HEADS) ** -0.5)
        key = jnp.where(i * sb + srow < lim, _sortable(sc), negkey)
        key_ref[rows, :] = key
        hi_ref[rows, :] = (key >> 16).astype(I16)
        lo_ref[rows, :] = ((key & 0xFFFF) - 0x8000).astype(I16)
        return c

    lax.fori_loop(0, nkb // 2, score_body, 0)

    def search16(ref):
        def bit_body(i, t_u):
            cand_u = t_u | jnp.left_shift(jnp.int32(1), 15 - i)
            cand = (cand_u - 0x8000).astype(I16)

            def body(j, a):
                ind = jnp.where(ref[pl.ds(pl.multiple_of(j * sb, sb), sb), :] >= cand,
                                jnp.ones((), I16), jnp.zeros((), I16))
                parts = [ind[16 * r:16 * (r + 1), :] for r in range(sb // 16)]
                while len(parts) > 1:
                    parts = [parts[r] + parts[r + 1] for r in range(0, len(parts), 2)]
                return a + parts[0]
            a = lax.fori_loop(0, nkb // 2, body, jnp.zeros((16, tq), I16))
            cnt = jnp.sum(a.astype(I32), axis=0, keepdims=True)
            return jnp.where(cnt >= top_k, cand_u, t_u)
        return lax.fori_loop(0, 16, bit_body, jnp.zeros((1, tq), I32))

    def count(pred_fn):
        def body(i, a):
            for u in range(2):
                kb = 2 * i + u
                ind = pred_fn(kb, key_ref[kslice(kb), :])
                a = a + jnp.sum(ind.reshape(kb_sz // 8, 8, tq), axis=0)
            return a
        a = lax.fori_loop(0, nkb // 2, body, jnp.zeros((8, tq), I32))
        return jnp.sum(a, axis=0, keepdims=True)

    t_hi = search16(hi_ref)
    t_hi16 = (t_hi - 0x8000).astype(I16)

    def lo_body(j, c):
        rows = pl.ds(pl.multiple_of(j * sb, sb), sb)
        hi = hi_ref[rows, :]
        lo_ref[rows, :] = jnp.where(hi == t_hi16, lo_ref[rows, :],
                                    jnp.where(hi > t_hi16, jnp.full((), 0x7FFF, I16), jnp.full((), -0x8000, I16)))
        return c

    lax.fori_loop(0, nkb // 2, lo_body, 0)
    t_s = (jnp.left_shift(t_hi, 16) | search16(lo_ref)) ^ INT_MIN

    def adm01(kb):
        return jnp.where(kb * kb_sz + krow < lim, 1.0, 0.0)

    def sel_body(kb, a):
        sel = jnp.where(key_ref[kslice(kb), :] >= t_s, adm01(kb), 0.0)
        sel_ref[kslice(kb), :] = sel
        return a + jnp.sum(sel.reshape(kb_sz // 8, 8, tq), axis=0)

    n_sel = jnp.sum(lax.fori_loop(0, nkb, sel_body, jnp.zeros((8, tq), F32)), axis=0, keepdims=True)

    @pl.when(jnp.max(n_sel) > top_k)
    def _():
        n_gt = count(lambda kb, key: jnp.where(key > t_s, 1, 0))
        need = (top_k - n_gt).astype(F32)
        r = lax.broadcasted_iota(I32, (kb_sz, kb_sz), 0)
        c = lax.broadcasted_iota(I32, (kb_sz, kb_sz), 1)
        ltri = jnp.where(c < r, 1.0, 0.0).astype(BF16)

        def tie_body(kb, carry):
            key = key_ref[kslice(kb), :]
            adm = adm01(kb)
            eq = jnp.where(key == t_s, adm, 0.0)
            rank = carry + jnp.dot(ltri, eq.astype(BF16), preferred_element_type=F32)
            keep = jnp.where(rank < need, eq, 0.0)
            sel_ref[kslice(kb), :] = jnp.where(key > t_s, adm, keep)
            return carry + jnp.sum(eq, axis=0, keepdims=True)

        lax.fori_loop(0, nkb, tie_body, jnp.zeros((1, tq), F32))

    m_ref[...] = jnp.full(m_ref.shape, NEG_INF, F32)
    l_ref[...] = jnp.zeros(l_ref.shape, F32)
    acc_ref[...] = jnp.zeros(acc_ref.shape, F32)
    c1 = (B_HD ** -0.5) * LOG2E

    def attend(k0, nk, bias2_fn):
        rows = pl.ds(pl.multiple_of(k0, LANE), nk)
        sel = sel_ref[rows, :] != 0.0
        ks = k_ref[0, rows, :]
        cols = []
        for h in range(B_HEADS):
            g = h // B_REP
            z = jnp.dot(ks[:, g * B_HD:(g + 1) * B_HD], qt_ref[0, h * B_HD:(h + 1) * B_HD, :],
                        preferred_element_type=F32)
            a = jnp.where(sel, z * c1 + bias2_fn(h), NEG_INF)
            a_ref[h, 0:nk, :] = a
            cols.append(jnp.max(a, axis=0, keepdims=True))
        m_old = m_ref[...]
        m_new = jnp.maximum(m_old, jnp.concatenate(cols, axis=0))
        alpha = jnp.exp2(m_old - m_new)
        m_ref[...] = m_new
        sums = []
        for h in range(B_HEADS):
            g = h // B_REP
            p = jnp.exp2(a_ref[h, 0:nk, :] - m_new[h:h + 1, :])
            sums.append(jnp.sum(p, axis=0, keepdims=True))
            hs = slice(h * B_HD, (h + 1) * B_HD)
            pv = jnp.dot(vt_ref[0, g * B_HD:(g + 1) * B_HD, rows], p.astype(BF16),
                         preferred_element_type=F32)
            acc_ref[hs, :] = alpha[h:h + 1, :] * acc_ref[hs, :] + pv
        l_ref[...] = alpha * l_ref[...] + jnp.concatenate(sums, axis=0)

    ab = near_ref.shape[2]
    n_far = jnp.maximum(q0 // ab - 1, 0)
    far_bias2 = far_ref[:, 0:1]

    def far_body(i, c):
        attend(i * ab, ab, lambda h: far_bias2[h:h + 1, :])
        return c

    lax.fori_loop(0, n_far, far_body, 0)

    def near_body(i, c):
        dd = i - q0 // ab + 1
        attend(i * ab, ab, lambda h: near_ref[dd, h])
        return c

    lax.fori_loop(n_far, nkb * kb_sz // ab, near_body, 0)

    _finish_heads(o_ref, x_ref, wout_ref, l_ref, acc_ref, B_HEADS)


def _dsa_attention_prompt(qt, qit, kwt, kw, k, vt, far, near, x, w_out):
    b, d, t = qt.shape
    tq = near.shape[3]
    top_k = min(TOPK_MAX, t // 4)
    ab = near.shape[2]
    assert t % tq == 0 and tq % CHUNK == 0 and tq % ab == 0 and ab % LANE == 0 and tq % (2 * LANE) == 0
    return pl.pallas_call(
        functools.partial(_dsa_prompt_kernel, top_k=top_k),
        grid=(b, t // tq),
        in_specs=[pl.BlockSpec((1, d, tq), lambda i, j: (i, 0, j)),
                  pl.BlockSpec((1, qit.shape[1], tq), lambda i, j: (i, 0, j)),
                  pl.BlockSpec((1, LANE, tq), lambda i, j: (i, 0, j)),
                  pl.BlockSpec((1, t, LANE), lambda i, j: (i, 0, 0)),
                  pl.BlockSpec((1, t, B_KV * B_HD), lambda i, j: (i, 0, 0)),
                  pl.BlockSpec((1, B_KV * B_HD, t), lambda i, j: (i, 0, 0)),
                  pl.BlockSpec(far.shape, lambda i, j: (0, 0)),
                  pl.BlockSpec(near.shape, lambda i, j: (0, 0, 0, 0)),
                  pl.BlockSpec((1, tq, d), lambda i, j: (i, j, 0)),
                  pl.BlockSpec((d, d), lambda i, j: (0, 0))],
        out_specs=pl.BlockSpec((1, tq, d), lambda i, j: (i, j, 0)),
        out_shape=jax.ShapeDtypeStruct((b, t, d), F32),
        scratch_shapes=[pltpu.VMEM((t, tq), I32), pltpu.VMEM((t, tq), I16), pltpu.VMEM((t, tq), I16),
                        pltpu.VMEM((t, tq), F32),
                        pltpu.VMEM((B_HEADS, tq), F32), pltpu.VMEM((B_HEADS, tq), F32),
                        pltpu.VMEM((d, tq), F32), pltpu.VMEM((B_HEADS, ab, tq), F32)],
        compiler_params=_cparams("parallel", "parallel"),
        name="dsa_attention",
    )(qt, qit, kwt, kw, k, vt, far, near, x, w_out)


def _dsa_cached_kernel(q_ref, qi_ref, wi_ref, rb_ref, kidx_ref, kp_ref, vp_ref, kn_ref, vn_ref, o_ref,
                       *, past, t_new, top_k):
    n_keys = past + t_new
    lp = kidx_ref.shape[1]
    negkey = _neg_inf_key()
    kpos = lax.broadcasted_iota(I32, (t_new, lp), 1)
    qpos = past + lax.broadcasted_iota(I32, (t_new, lp), 0)
    adm = kpos < (qpos // CHUNK + 1) * CHUNK

    s = _dot_nt(qi_ref[0], kidx_ref[0])
    w = jnp.maximum(s, 0.0) * wi_ref[0]
    sc = w[0:t_new]
    for h in range(1, IDX_HEADS):
        sc = sc + w[h * t_new:(h + 1) * t_new]
    sc = sc * ((IDX_DIM * IDX_HEADS) ** -0.5)
    key = jnp.where(adm, _sortable(sc), negkey)
    key = jnp.where(kpos < n_keys, key, INT_MIN)

    def bit_body(i, t_u):
        cand_u = t_u | jnp.left_shift(jnp.int32(1), 31 - i)
        cnt = jnp.sum(jnp.where(key >= (cand_u ^ INT_MIN), 1.0, 0.0), axis=1, keepdims=True)
        return jnp.where(cnt >= top_k, cand_u, t_u)

    t_s = lax.fori_loop(0, 32, bit_body, jnp.zeros((t_new, 1), I32)) ^ INT_MIN

    adm01 = jnp.where(adm, 1.0, 0.0)
    gt = jnp.where(key > t_s, adm01, 0.0)
    eq = jnp.where(key == t_s, adm01, 0.0)
    need = top_k - jnp.sum(jnp.where(key > t_s, 1.0, 0.0), axis=1, keepdims=True)
    r = lax.broadcasted_iota(I32, (LANE, LANE), 0)
    c = lax.broadcasted_iota(I32, (LANE, LANE), 1)
    utri = jnp.where(r < c, 1.0, 0.0).astype(BF16)
    carry = jnp.zeros((t_new, 1), F32)
    keeps = []
    for blk in range(lp // LANE):
        e = eq[:, blk * LANE:(blk + 1) * LANE]
        rank = carry + jnp.dot(e.astype(BF16), utri, preferred_element_type=F32)
        keeps.append(jnp.where(rank < need, e, 0.0))
        carry = carry + jnp.sum(e, axis=1, keepdims=True)
    sel = gt + jnp.concatenate(keeps, axis=1)

    near = max(past - MAX_DIST, 0) // LANE * LANE
    rel = (kpos - qpos)[:, near:]
    nb = N_BUCKETS // 2
    max_exact = nb // 2
    n = jnp.abs(rel)
    nf = jnp.maximum(n, 1).astype(F32)
    large = max_exact + (jnp.log(nf / max_exact) / math.log(MAX_DIST / max_exact)
                         * (nb - max_exact)).astype(I32)
    bucket = (rel > 0).astype(I32) * nb + jnp.where(n < max_exact, n, jnp.minimum(large, nb - 1))

    rows = B_REP * t_new
    sel_g = jnp.concatenate([sel] * B_REP, axis=0) != 0.0
    bucket_g = jnp.concatenate([bucket] * B_REP, axis=0)
    for g in range(B_KV):
        grp = lambda ref, n: ref[0, pl.ds(g, n, stride=B_KV), :]
        qg = q_ref[0, g * rows:(g + 1) * rows, :]
        rb = rb_ref[g * rows:(g + 1) * rows, :]
        bias_near = jnp.zeros((rows, lp - near), F32)
        for j in range(N_BUCKETS):
            bias_near = jnp.where(bucket_g == j, rb[:, j:j + 1], bias_near)
        bias = jnp.concatenate([jnp.broadcast_to(rb[:, nb - 1:nb], (rows, near)), bias_near], axis=1)
        zp = _dot_nt(qg, grp(kp_ref, past)) * (B_HD ** -0.5)
        zn = _dot_nt(qg, grp(kn_ref, t_new)) * (B_HD ** -0.5)
        ap = jnp.where(sel_g[:, :past], zp + bias[:, :past], NEG_INF)
        an = jnp.where(sel_g[:, past:n_keys], zn + bias[:, past:n_keys], NEG_INF)
        m = jnp.maximum(jnp.max(ap, axis=1, keepdims=True), jnp.max(an, axis=1, keepdims=True))
        pp, pn = jnp.exp(ap - m), jnp.exp(an - m)
        l = jnp.sum(pp, axis=1, keepdims=True) + jnp.sum(pn, axis=1, keepdims=True)
        o_ref[0, g * rows:(g + 1) * rows, :] = (_dot(pp, grp(vp_ref, past)) + _dot(pn, grp(vn_ref, t_new))) / l


def _dsa_attention_cached(q_rows, qi_rows, wi_col, rb_rows, kidx_all, k_past, v_past, k_new, v_new):
    b, rows, hd = q_rows.shape
    past, t_new = k_past.shape[1], k_new.shape[1]
    lp = kidx_all.shape[1]
    top_k = min(TOPK_MAX, (past + t_new) // 4)
    assert past % LANE == 0
    flat = lambda a: a.reshape(b, a.shape[1] * B_KV, B_HD)
    kv_spec = lambda n: pl.BlockSpec((1, n * B_KV, B_HD), lambda i: (i, 0, 0))
    return pl.pallas_call(
        functools.partial(_dsa_cached_kernel, past=past, t_new=t_new, top_k=top_k),
        grid=(b,),
        in_specs=[pl.BlockSpec((1, rows, hd), lambda i: (i, 0, 0)),
                  pl.BlockSpec((1,) + qi_rows.shape[1:], lambda i: (i, 0, 0)),
                  pl.BlockSpec((1,) + wi_col.shape[1:], lambda i: (i, 0, 0)),
                  pl.BlockSpec(rb_rows.shape, lambda i: (0, 0)),
                  pl.BlockSpec((1, lp, IDX_DIM), lambda i: (i, 0, 0)),
                  kv_spec(past), kv_spec(past), kv_spec(t_new), kv_spec(t_new)],
        out_specs=pl.BlockSpec((1, rows, hd), lambda i: (i, 0, 0)),
        out_shape=jax.ShapeDtypeStruct((b, rows, hd), F32),
        compiler_params=_cparams("parallel"),
        name="dsa_attention_cached",
    )(q_rows, qi_rows, wi_col, rb_rows, kidx_all, flat(k_past), flat(v_past), flat(k_new), flat(v_new))


def _cumsum_kernel(x_ref, o_ref):
    x = x_ref[0]
    n = x.shape[-1]
    lane = lax.broadcasted_iota(I32, x.shape, 1)
    s = 1
    while s < n:
        x = x + jnp.where(lane >= s, pltpu.roll(x, s, 1), 0.0)
        s *= 2
    o_ref[0] = x


def _cumsum_lanes(x):
    b, h, n = x.shape
    return pl.pallas_call(
        _cumsum_kernel,
        grid=(b,),
        in_specs=[pl.BlockSpec((1, h, n), lambda i: (i, 0, 0))],
        out_specs=pl.BlockSpec((1, h, n), lambda i: (i, 0, 0)),
        out_shape=jax.ShapeDtypeStruct((b, h, n), F32),
        compiler_params=_cparams("parallel"),
        name="logf_cumsum",
    )(x)


def _fox_init(m_ref, l_ref, acc_ref):
    m_ref[...] = jnp.full(m_ref.shape, NEG_INF, F32)
    l_ref[...] = jnp.zeros(l_ref.shape, F32)
    acc_ref[...] = jnp.zeros(acc_ref.shape, F32)


def _fox_tile(z_fn, pv_fn, cq, ck, mask, m_ref, l_ref, acc_ref, a_ref):
    c1 = (D_HD ** -0.5) * LOG2E
    cq2, ck2 = cq * LOG2E, ck * LOG2E
    cols = []
    for h in range(D_HEADS):
        a = z_fn(h) * c1 - ck2[:, h:h + 1]
        if mask is not None:
            a = jnp.where(mask, a, NEG_INF)
        a_ref[h] = a
        cols.append(jnp.max(a, axis=0, keepdims=True))
    m_old = m_ref[...]
    m_new = jnp.maximum(m_old, jnp.concatenate(cols, axis=0) + cq2)
    alpha = jnp.exp2(m_old - m_new)
    shift = m_new - cq2
    m_ref[...] = m_new
    sums = []
    for h in range(D_HEADS):
        p = jnp.exp2(a_ref[h] - shift[h:h + 1, :])
        sums.append(jnp.sum(p, axis=0, keepdims=True))
        hs = slice(h * D_HD, (h + 1) * D_HD)
        acc_ref[hs, :] = alpha[h:h + 1, :] * acc_ref[hs, :] + pv_fn(h, p.astype(BF16))
    l_ref[...] = alpha * l_ref[...] + jnp.concatenate(sums, axis=0)


def _hs(h):
    return slice(h * D_HD, (h + 1) * D_HD)


def _fox_prompt_kernel(qt_ref, k_ref, vt_ref, cq_ref, ck_ref, x_ref, wout_ref, o_ref,
                       m_ref, l_ref, acc_ref, a_ref):
    qi, step = pl.program_id(1), pl.program_id(2)
    tq, tk = qt_ref.shape[2], a_ref.shape[1]
    q0 = qi * tq
    ki = step - (pl.num_programs(2) - 1 - (q0 + tq - 1) // tk)
    k0 = ki * tk
    keys = pl.ds(pl.multiple_of(jnp.maximum(k0, 0), tk), tk)

    @pl.when(step == 0)
    def _():
        _fox_init(m_ref, l_ref, acc_ref)

    def run(masked):
        mask = None
        if masked:
            mask = (k0 + lax.broadcasted_iota(I32, (tk, tq), 0)) <= (q0 + lax.broadcasted_iota(I32, (tk, tq), 1))
        _fox_tile(lambda h: jnp.dot(k_ref[0, keys, _hs(h)], qt_ref[0, _hs(h), :], preferred_element_type=F32),
                  lambda h, p: jnp.dot(vt_ref[0, _hs(h), keys], p, preferred_element_type=F32),
                  cq_ref[0], ck_ref[0, keys, :], mask, m_ref, l_ref, acc_ref, a_ref)

    fully_visible = k0 + tk - 1 <= q0
    pl.when(jnp.logical_and(ki >= 0, fully_visible))(lambda: run(False))
    pl.when(jnp.logical_and(ki >= 0, jnp.logical_not(fully_visible)))(lambda: run(True))

    @pl.when(step == pl.num_programs(2) - 1)
    def _():
        _finish_heads(o_ref, x_ref, wout_ref, l_ref, acc_ref, D_HEADS)


def _fox_attention_prompt(qt, k, vt, cum_t, cum, x, w_out, tq, tk):
    b, d, t = qt.shape
    nq, nk = t // tq, t // tk
    return pl.pallas_call(
        _fox_prompt_kernel,
        grid=(b, nq, nk),
        in_specs=[pl.BlockSpec((1, d, tq), lambda i, j, kk: (i, 0, j)),
                  pl.BlockSpec((1, t, d), lambda i, j, kk: (i, 0, 0)),
                  pl.BlockSpec((1, d, t), lambda i, j, kk: (i, 0, 0)),
                  pl.BlockSpec((1, D_HEADS, tq), lambda i, j, kk: (i, 0, j)),
                  pl.BlockSpec((1, t, D_HEADS), lambda i, j, kk: (i, 0, 0)),
                  pl.BlockSpec((1, tq, d), lambda i, j, kk: (i, j, 0)),
                  pl.BlockSpec((d, d), lambda i, j, kk: (0, 0))],
        out_specs=pl.BlockSpec((1, tq, d), lambda i, j, kk: (i, j, 0)),
        out_shape=jax.ShapeDtypeStruct((b, t, d), F32),
        scratch_shapes=[pltpu.VMEM((D_HEADS, tq), F32), pltpu.VMEM((D_HEADS, tq), F32),
                        pltpu.VMEM((d, tq), F32), pltpu.VMEM((D_HEADS, tk, tq), F32)],
        compiler_params=_cparams("parallel", "parallel", "arbitrary"),
        name="fox_attention",
    )(qt, k, vt, cum_t, cum, x, w_out)


def _fox_cached_kernel(q_ref, kp_ref, vp_ref, kn_ref, vn_ref, cq_ref, ckp_ref, ckn_ref, o_ref,
                       m_ref, l_ref, acc_ref, *, t_new):
    ki = pl.program_id(1)
    n_past = pl.num_programs(1) - 1
    rows = q_ref.shape[1]
    c1 = (D_HD ** -0.5) * LOG2E

    @pl.when(ki == 0)
    def _():
        m_ref[...] = jnp.full(m_ref.shape, NEG_INF, F32)
        l_ref[...] = jnp.zeros(l_ref.shape, F32)
        acc_ref[...] = jnp.zeros(acc_ref.shape, F32)

    def tile(k2d, v2d, ck_row, causal):
        cols = k2d.shape[0]
        a = _dot_nt(q_ref[0], k2d) * c1 - ck_row * LOG2E
        r = lax.broadcasted_iota(I32, (rows, cols), 0)
        c = lax.broadcasted_iota(I32, (rows, cols), 1)
        ok = (c % D_HEADS) == (r // t_new)
        if causal:
            ok = jnp.logical_and(ok, (c // D_HEADS) <= (r % t_new))
        a = jnp.where(ok, a, NEG_INF)
        cq2 = cq_ref[0] * LOG2E
        m_old = m_ref[...]
        m_new = jnp.maximum(m_old, jnp.max(a, axis=1, keepdims=True) + cq2)
        alpha = jnp.exp2(m_old - m_new)
        p = jnp.exp2(a - (m_new - cq2))
        l_ref[...] = alpha * l_ref[...] + jnp.sum(p, axis=1, keepdims=True)
        acc_ref[...] = alpha * acc_ref[...] + _dot(p, v2d)
        m_ref[...] = m_new

    @pl.when(ki < n_past)
    def _():
        tk = kp_ref.shape[1]
        tile(kp_ref[0].reshape(tk * D_HEADS, D_HD), vp_ref[0].reshape(tk * D_HEADS, D_HD), ckp_ref[0], False)

    @pl.when(ki == n_past)
    def _():
        tile(kn_ref[0].reshape(t_new * D_HEADS, D_HD), vn_ref[0].reshape(t_new * D_HEADS, D_HD), ckn_ref[0], True)
        o_ref[0] = acc_ref[...] / l_ref[...]


def _fox_attention_cached(q_rows, k_past, v_past, k_new, v_new, cq_col, ck_past, ck_new, tk):
    b, rows, hd = q_rows.shape
    past, t_new = k_past.shape[1], k_new.shape[1]
    n_past = past // tk
    pidx = lambda i, kk: (i, jnp.minimum(kk, n_past - 1), 0, 0)
    return pl.pallas_call(
        functools.partial(_fox_cached_kernel, t_new=t_new),
        grid=(b, n_past + 1),
        in_specs=[pl.BlockSpec((1, rows, hd), lambda i, kk: (i, 0, 0)),
                  pl.BlockSpec((1, tk, D_HEADS, D_HD), pidx),
                  pl.BlockSpec((1, tk, D_HEADS, D_HD), pidx),
                  pl.BlockSpec((1, t_new, D_HEADS, D_HD), lambda i, kk: (i, 0, 0, 0)),
                  pl.BlockSpec((1, t_new, D_HEADS, D_HD), lambda i, kk: (i, 0, 0, 0)),
                  pl.BlockSpec((1, rows, 1), lambda i, kk: (i, 0, 0)),
                  pl.BlockSpec((1, 1, tk * D_HEADS), lambda i, kk: (i, 0, jnp.minimum(kk, n_past - 1))),
                  pl.BlockSpec((1, 1, t_new * D_HEADS), lambda i, kk: (i, 0, 0))],
        out_specs=pl.BlockSpec((1, rows, hd), lambda i, kk: (i, 0, 0)),
        out_shape=jax.ShapeDtypeStruct((b, rows, hd), F32),
        scratch_shapes=[pltpu.VMEM((rows, 1), F32), pltpu.VMEM((rows, 1), F32), pltpu.VMEM((rows, hd), F32)],
        compiler_params=_cparams("parallel", "arbitrary"),
        name="fox_attention_cached",
    )(q_rows, k_past, v_past, k_new, v_new, cq_col, ck_past, ck_new)


def _pad_rows(a, rows):
    if a.shape[1] == rows:
        return a
    return jnp.pad(a, ((0, 0), (0, rows - a.shape[1])) + ((0, 0),) * (a.ndim - 2))


def _round_up(n, m):
    return -(-n // m) * m


def _dsa_mixer(x, g, k_past, v_past, ki_past, w, w_out, rel_bias):
    b, t, d = x.shape
    past = k_past.shape[1]
    n_keys = past + t
    if past == 0:
        qt, k4, kb, v4, vt, qit, kw, kwt, ki_t = _proj(
            x, g, w, ((0, "t", BF16, None), (1, B_HD, F32, None), (1, "rows", BF16, None),
                      (2, B_HD, F32, None), (2, "t", BF16, None), (3, "t", BF16, None),
                      (4, "rows", F32, None), (4, "t", F32, None), (4, ("first_t", IDX_DIM), F32, None)),
            jnp.zeros((1, LANE), F32))
        far, near = _bias_tiles(rel_bias, 2 * LANE, 2 * LANE)
        y = _dsa_attention_prompt(qt, qit, kwt, kw, kb, vt, far, near, x, w_out)
        return (y, k4, v4, jnp.swapaxes(ki_t, 1, 2))
    q, k4, v4, qidx, kw = _proj(
        x, g, w, ((0, "rows", BF16, None), (1, B_HD, F32, None), (2, B_HD, F32, None),
                  (3, "rows", BF16, None), (4, "rows", F32, None)),
        jnp.zeros((1, LANE), F32))
    ki = kw[:, :, :IDX_DIM]
    to_rows = lambda a, nh: jnp.swapaxes(a.reshape(b, t, nh, -1), 1, 2).reshape(b, nh * t, -1)
    kidx_all = _pad_rows(jnp.concatenate([ki_past, ki], axis=1), _round_up(n_keys, LANE))
    o = _dsa_attention_cached(to_rows(q, B_HEADS), to_rows(qidx, IDX_HEADS),
                              to_rows(kw[:, :, IDX_DIM:IDX_DIM + IDX_HEADS], IDX_HEADS),
                              jnp.repeat(rel_bias.T, t, axis=0), kidx_all, k_past, v_past, k4, v4)
    o = jnp.swapaxes(o.reshape(b, B_HEADS, t, B_HD), 1, 2).reshape(b * t, d)
    y = _out_proj(x.reshape(b * t, d), o, w_out).reshape(b, t, d)
    return (y, k4, v4, ki)


def _fox_mixer(x, g, k_past, v_past, lf_past, w, b_f, w_out):
    b, t, d = x.shape
    past = k_past.shape[1]
    heads4 = ((1, D_HD, F32, None), (2, D_HD, F32, None))
    if past == 0:
        tq = tk = min(4 * LANE, t)
        k4, v4, logf_t, qt, kb, vt = _proj(
            x, g, w, heads4 + ((3, ("first_t", D_HEADS), F32, "log_sigmoid"), (0, "t", BF16, None),
                               (1, "rows", BF16, None), (2, "t", BF16, None)), b_f)
        cum_t = _cumsum_lanes(logf_t)
        y = _fox_attention_prompt(qt, kb, vt, cum_t, jnp.swapaxes(cum_t, 1, 2), x, w_out, tq, tk)
        return (y, k4, v4, jnp.swapaxes(logf_t, 1, 2))
    else:
        tk = math.gcd(past, 4 * LANE)
        k4, v4, logf, q = _proj(
            x, g, w, heads4 + ((3, ("first", D_HEADS), F32, "log_sigmoid"), (0, "rows", BF16, None)), b_f)
        lf_all = _pad_rows(jnp.concatenate([lf_past, logf], axis=1), _round_up(past + t, LANE))
        cum = jnp.swapaxes(_cumsum_lanes(jnp.swapaxes(lf_all, 1, 2)), 1, 2)[:, :past + t]
        ck = cum.reshape(b, 1, (past + t) * D_HEADS)
        to_rows = lambda a: jnp.swapaxes(a.reshape(b, t, D_HEADS, -1), 1, 2).reshape(b, D_HEADS * t, -1)
        o = _fox_attention_cached(to_rows(q), k_past, v_past, k4, v4, to_rows(cum[:, past:]),
                                  ck[:, :, :past * D_HEADS], ck[:, :, past * D_HEADS:], tk)
        o = jnp.swapaxes(o.reshape(b, D_HEADS, t, D_HD), 1, 2).reshape(b, t, d)
    y = _out_proj(x.reshape(b * t, d), o.reshape(b * t, d), w_out).reshape(b, t, d)
    return (y, k4, v4, logf)


def _run_group(x, pos0, a_st, b_k, b_v, b_ki, c_st, d_k, d_v, d_lf, mem_k, mem_v, prm):
    b, t, d = x.shape
    depth = prm["norm_mix"].shape[0]
    new = {n: [] for n in ("a", "bk", "bv", "bki", "c", "dk", "dv", "dlf")}
    for i in range(depth):
        kind, j = i % 4, i // 4
        g = prm["norm_mix"][i]
        if kind == 0:
            x, st = _conv_mixer(x, g, prm["a_w_in"][j], prm["a_conv"][j], a_st[j], prm["a_w_out"][j])
            new["a"].append(st)
        elif kind == 1:
            x, kk, vv, ki = _dsa_mixer(x, g, b_k[j], b_v[j], b_ki[j], prm["b_w"][j], prm["b_w_out"][j],
                                       prm["rel_bias"])
            new["bk"].append(kk); new["bv"].append(vv); new["bki"].append(ki)
        elif kind == 2:
            x, st = _pool_mixer(x, g, c_st[j], prm["c_w_group"][j], prm["c_scale"][j], pos0)
            new["c"].append(st)
        else:
            x, kk, vv, lf = _fox_mixer(x, g, d_k[j], d_v[j], d_lf[j], prm["d_w"][j], prm["d_b_f"][j],
                                       prm["d_w_out"][j])
            new["dk"].append(kk); new["dv"].append(vv); new["dlf"].append(lf)
        x = _xattn(x, prm["norm_xattn"], prm["xa_wq"], mem_k, mem_v, prm["xa_wo"], i)
        last = i == depth - 1
        x = _ffn(x.reshape(b * t, d), prm["norm_ffn"], prm["ffn_w1"], prm["ffn_w2"],
                 prm["final_norm"], i, last).reshape(b, t, d)
    return (x,) + tuple(jnp.stack(new[n]) for n in ("a", "bk", "bv", "bki", "c", "dk", "dv", "dlf"))


def kernel(x_prompt, x_sample, state_a_conv, cache_b_k, cache_b_v, cache_b_kidx, state_c_pool,
           cache_d_k, cache_d_v, cache_d_logf, cache_mem_k, cache_mem_v, mem_prompt,
           norm_mix, norm_xattn, norm_mem, norm_ffn, final_norm,
           a_w_in, a_conv, a_w_out, b_w_in, b_w_out, rel_bias, c_w_group, c_scale,
           d_w_in, d_b_f, d_w_out, xa_wq, xa_wkv, xa_wo, ffn_w1, ffn_w2):
    bp, t, d = x_prompt.shape
    depth = norm_mix.shape[0]
    n_b, n_d = b_w_in.shape[0], d_w_in.shape[0]
    bf = lambda w: w.astype(BF16)

    def split_cols(w, widths):
        out, c = [], 0
        for wd in widths:
            piece = w[:, c:c + wd]
            c += wd
            if wd % LANE:
                piece = jnp.pad(piece, ((0, 0), (0, _round_up(wd, LANE) - wd)))
            out.append(bf(piece))
        assert c == w.shape[1]
        return out

    b_q, b_kvw = B_HEADS * B_HD, B_KV * B_HD
    b_w = [split_cols(b_w_in[j], (b_q, b_kvw, b_kvw, IDX_HEADS * IDX_DIM, IDX_DIM + IDX_HEADS))
           for j in range(n_b)]
    d_w = [split_cols(d_w_in[j], (d, d, d, D_HEADS)) for j in range(n_d)]
    d_bf = [jnp.pad(d_b_f[j], (0, LANE - D_HEADS)).reshape(1, LANE) for j in range(n_d)]

    prm = {"norm_mix": norm_mix, "norm_xattn": norm_xattn, "norm_ffn": norm_ffn, "final_norm": final_norm,
           "a_w_in": bf(a_w_in), "a_conv": a_conv, "a_w_out": bf(a_w_out),
           "b_w": b_w, "b_w_out": bf(b_w_out), "rel_bias": rel_bias,
           "c_w_group": bf(c_w_group), "c_scale": c_scale,
           "d_w": d_w, "d_b_f": d_bf, "d_w_out": bf(d_w_out),
           "xa_wq": xa_wq, "xa_wo": xa_wo, "ffn_w1": ffn_w1, "ffn_w2": ffn_w2}

    n_mem = mem_prompt.shape[1]
    mk, mv, mk_rows, mv_rows = _memory_kv(mem_prompt, norm_mem, xa_wkv)

    n_a, n_c = a_w_in.shape[0], c_w_group.shape[0]
    z = lambda *s: jnp.zeros(s, F32)
    gp = _run_group(x_prompt, 0,
                    z(n_a, bp, CONV_W - 1, d),
                    z(n_b, bp, 0, B_KV, B_HD), z(n_b, bp, 0, B_KV, B_HD), z(n_b, bp, 0, IDX_DIM),
                    z(n_c, bp, POOL_STATE, d),
                    z(n_d, bp, 0, D_HEADS, D_HD), z(n_d, bp, 0, D_HEADS, D_HD), z(n_d, bp, 0, D_HEADS),
                    mk_rows, mv_rows, prm)

    bs = x_sample.shape[0]
    past_len = cache_b_k.shape[2]
    gs = _run_group(x_sample, past_len, state_a_conv, cache_b_k, cache_b_v, cache_b_kidx, state_c_pool,
                    cache_d_k, cache_d_v, cache_d_logf, cache_mem_k, cache_mem_v, prm)

    (y_p, a_p, bk_p, bv_p, bki_p, c_p, dk_p, dv_p, dlf_p) = gp
    (y_s, a_s, bk_s, bv_s, bki_s, c_s, dk_s, dv_s, dlf_s) = gs
    return (y_p, y_s, a_p, a_s, bk_p, bv_p, bki_p, bk_s, bv_s, bki_s, c_p, c_s,
            dk_p, dv_p, dlf_p, dk_s, dv_s, dlf_s, mk, mv)
```

```python
import functools
import math

import jax
import jax.numpy as jnp
import numpy as np
from jax import lax
from jax.experimental import pallas as pl
from jax.experimental.pallas import tpu as pltpu

F32 = jnp.float32
BF16 = jnp.bfloat16
I32 = jnp.int32
I16 = jnp.int16

EPS = 1e-6
NEG_INF = -1e30
LOG2E = math.log2(math.e)
CHUNK = 64
LANE = 128
VMEM_LIMIT = 48 * 1024 * 1024

ROW_TILE = 512
WIDE_ROW_TILE = 1024
FF_TILE = 1024
FOX_TILE = 4 * LANE
DSA_TILE = 2 * LANE

CONV_W = 3
POOL_WINDOWS = (2, 4, 8, 16)
POOL_STATE = max(POOL_WINDOWS) - 1
B_HEADS, B_KV, B_HD = 8, 2, 128
B_REP = B_HEADS // B_KV
IDX_HEADS, IDX_DIM = 8, 64
TOPK_MAX = 256
N_BUCKETS, MAX_DIST = 32, 128
D_HEADS, D_HD = 8, 128
MEM_HEADS = 4
INT_MIN = -2147483648
BIAS_CENTER = 2 * LANE


def _cparams(*sem):
    return pltpu.CompilerParams(dimension_semantics=sem, vmem_limit_bytes=VMEM_LIMIT)


def _dot(a, b):
    return jnp.dot(a.astype(BF16), b.astype(BF16), preferred_element_type=F32)


def _dot_nt(a, b):
    return lax.dot_general(a.astype(BF16), b.astype(BF16), (((1,), (1,)), ((), ())),
                           preferred_element_type=F32)


def _dot_tn(a, b):
    return lax.dot_general(a.astype(BF16), b.astype(BF16), (((0,), (0,)), ((), ())),
                           preferred_element_type=F32)


def _rms(x, g):
    return x * lax.rsqrt(jnp.mean(x * x, axis=-1, keepdims=True) + EPS) * g


def _finish_heads(o_ref, x_ref, w_ref, l_ref, acc_ref, n_heads):
    hd = acc_ref.shape[0] // n_heads
    inv_l = 1.0 / l_ref[...]
    heads_t = jnp.concatenate([acc_ref[h * hd:(h + 1) * hd, :] * inv_l[h:h + 1, :] for h in range(n_heads)],
                              axis=0)
    o_ref[0] = x_ref[0] + _dot_tn(heads_t, w_ref[...])


def _row_tile(n, cap):
    t = min(n, cap)
    assert n % t == 0
    return t


def _memkv_kernel(mem_ref, g_ref, w_ref, k_ref, v_ref, kb_ref, vb_ref):
    bb, nm, d = mem_ref.shape
    m = mem_ref[...].reshape(bb * nm, d)
    mn = m * lax.rsqrt(jnp.mean(m * m, axis=-1, keepdims=True) + EPS)
    h = (mn * g_ref[0]).astype(BF16)
    hd = d // MEM_HEADS
    k = jnp.dot(h, w_ref[0, :, :d].astype(BF16), preferred_element_type=F32)
    v = jnp.dot(h, w_ref[0, :, d:].astype(BF16), preferred_element_type=F32)
    for i in range(bb):
        rows = slice(i * nm, (i + 1) * nm)
        kb_ref[0, i] = k[rows].astype(BF16)
        vb_ref[0, i] = v[rows].astype(BF16)
        k_ref[0, i] = pltpu.einshape("m(hd)->mhd", k[rows], d=hd)
        v_ref[0, i] = pltpu.einshape("m(hd)->mhd", v[rows], d=hd)


def _memory_kv(mem, g_mem, w_kv):
    depth, d = g_mem.shape
    b, nm, _ = mem.shape
    hd = d // MEM_HEADS
    out = jax.ShapeDtypeStruct((depth, b, nm, MEM_HEADS, hd), F32)
    out_b = jax.ShapeDtypeStruct((depth, b, nm, d), BF16)
    bb = math.gcd(b, 2)
    heads_spec = pl.BlockSpec((1, bb, nm, MEM_HEADS, hd), lambda l, i: (l, i, 0, 0, 0))
    rows_spec = pl.BlockSpec((1, bb, nm, d), lambda l, i: (l, i, 0, 0))
    return pl.pallas_call(
        _memkv_kernel,
        grid=(depth, b // bb),
        in_specs=[pl.BlockSpec((bb, nm, d), lambda l, i: (i, 0, 0)),
                  pl.BlockSpec((1, 1, d), lambda l, i: (l, 0, 0)),
                  pl.BlockSpec((1, d, 2 * d), lambda l, i: (l, 0, 0))],
        out_specs=[heads_spec, heads_spec, rows_spec, rows_spec],
        out_shape=[out, out, out_b, out_b],
        compiler_params=_cparams("parallel", "parallel"),
        name="memory_kv",
    )(mem, g_mem.reshape(depth, 1, d), w_kv)


def _ffn_kernel(x_ref, g_ref, w1_ref, w2_ref, gf_ref, o_ref, h_ref, acc_ref, *, final_norm):
    j = pl.program_id(1)

    @pl.when(j == 0)
    def _():
        h_ref[...] = _rms(x_ref[...], g_ref[...]).astype(BF16)
        acc_ref[...] = jnp.zeros_like(acc_ref)

    u = jnp.maximum(jnp.dot(h_ref[...], w1_ref[...].astype(BF16), preferred_element_type=F32), 0.0)
    acc_ref[...] += jnp.dot((u * u).astype(BF16), w2_ref[...].astype(BF16), preferred_element_type=F32)

    @pl.when(j == pl.num_programs(1) - 1)
    def _():
        y = x_ref[...] + acc_ref[...]
        o_ref[...] = _rms(y, gf_ref[...]) if final_norm else y


def _ffn(x, g, w1, w2, gf, layer, final_norm):
    n, d = x.shape
    f = w1.shape[2]
    tm = _row_tile(n, WIDE_ROW_TILE)
    tf = FF_TILE
    return pl.pallas_call(
        functools.partial(_ffn_kernel, final_norm=final_norm),
        grid=(n // tm, f // tf),
        in_specs=[pl.BlockSpec((tm, d), lambda i, j: (i, 0)),
                  pl.BlockSpec((None, 1, d), lambda i, j: (layer, 0, 0)),
                  pl.BlockSpec((None, d, tf), lambda i, j: (layer, 0, j)),
                  pl.BlockSpec((None, tf, d), lambda i, j: (layer, j, 0)),
                  pl.BlockSpec((1, d), lambda i, j: (0, 0))],
        out_specs=pl.BlockSpec((tm, d), lambda i, j: (i, 0)),
        out_shape=jax.ShapeDtypeStruct((n, d), F32),
        scratch_shapes=[pltpu.VMEM((tm, d), BF16), pltpu.VMEM((tm, d), F32)],
        compiler_params=_cparams("parallel", "arbitrary"),
        name="ffn",
    )(x, g.reshape(-1, 1, d), w1, w2, gf.reshape(1, d))


def _xattn_kernel(x_ref, g_ref, wq_ref, mk_ref, mv_ref, wo_ref, o_ref):
    x = x_ref[0]
    d = x.shape[-1]
    hd = d // MEM_HEADS
    h = _rms(x, g_ref[...]).astype(BF16)
    q = jnp.dot(h, wq_ref[...].astype(BF16), preferred_element_type=F32)
    outs = []
    if len(mk_ref.shape) == 3:
        mk = pltpu.einshape("mhd->m(hd)", mk_ref[...]).astype(BF16)
        mv = pltpu.einshape("mhd->m(hd)", mv_ref[...]).astype(BF16)
    else:
        mk, mv = mk_ref[...], mv_ref[...]
    for hh in range(MEM_HEADS):
        sl = slice(hh * hd, (hh + 1) * hd)
        kh, vh = mk[:, sl], mv[:, sl]
        s = _dot_nt(q[:, sl], kh) * (hd ** -0.5)
        m = jnp.max(s, axis=-1, keepdims=True)
        p = jnp.exp(s - m)
        l = jnp.sum(p, axis=-1, keepdims=True)
        outs.append(_dot(p, vh) / l)
    o = jnp.concatenate(outs, axis=-1)
    o_ref[0] = x + _dot(o, wo_ref[...])


def _xattn(x, g, wq, mk, mv, wo, layer):
    b, t, d = x.shape
    tm = _row_tile(t, WIDE_ROW_TILE)
    kv_spec = pl.BlockSpec((None, None) + mk.shape[2:], lambda i, j: (layer, i) + (0,) * (mk.ndim - 2))
    return pl.pallas_call(
        _xattn_kernel,
        grid=(b, t // tm),
        in_specs=[pl.BlockSpec((1, tm, d), lambda i, j: (i, j, 0)),
                  pl.BlockSpec((None, 1, d), lambda i, j: (layer, 0, 0)),
                  pl.BlockSpec((None, d, d), lambda i, j: (layer, 0, 0)),
                  kv_spec, kv_spec,
                  pl.BlockSpec((None, d, d), lambda i, j: (layer, 0, 0))],
        out_specs=pl.BlockSpec((1, tm, d), lambda i, j: (i, j, 0)),
        out_shape=jax.ShapeDtypeStruct((b, t, d), F32),
        compiler_params=_cparams("parallel", "parallel"),
        name="xattn",
    )(x, g.reshape(-1, 1, d), wq, mk, mv, wo)


def _conv_kernel(x_ref, g_ref, win_ref, wc_ref, st_ref, wout_ref, o_ref, nst_ref, z_ref):
    t = pl.program_id(1)
    x = x_ref[0]
    tm, d = x.shape
    pad = 8

    @pl.when(t == 0)
    def _():
        z_ref[pad - 2:pad, :] = st_ref[0]

    h = _rms(x, g_ref[...]).astype(BF16)
    bg = jnp.dot(h, win_ref[:, 0:d], preferred_element_type=F32)
    cg = jnp.dot(h, win_ref[:, d:2 * d], preferred_element_type=F32)
    u = jnp.dot(h, win_ref[:, 2 * d:3 * d], preferred_element_type=F32)
    z = cg * u
    z_ref[pad:pad + tm, :] = z
    conv = (z_ref[pad - 2:pad - 2 + tm, :] * wc_ref[0:1, :]
            + z_ref[pad - 1:pad - 1 + tm, :] * wc_ref[1:2, :]
            + z * wc_ref[2:3, :])
    o_ref[0] = x + _dot(bg * conv, wout_ref[...])
    last = z_ref[pad + tm - 2:pad + tm, :]
    z_ref[pad - 2:pad, :] = last

    @pl.when(t == pl.num_programs(1) - 1)
    def _():
        nst_ref[0] = last


def _conv_mixer(x, g, w_in, w_conv, state, w_out):
    b, t, d = x.shape
    tm = _row_tile(t, WIDE_ROW_TILE)
    once = pl.Buffered(1)
    return pl.pallas_call(
        _conv_kernel,
        grid=(b, t // tm),
        in_specs=[pl.BlockSpec((1, tm, d), lambda i, j: (i, j, 0)),
                  pl.BlockSpec((1, d), lambda i, j: (0, 0)),
                  pl.BlockSpec((d, 3 * d), lambda i, j: (0, 0), pipeline_mode=once),
                  pl.BlockSpec((CONV_W, d), lambda i, j: (0, 0)),
                  pl.BlockSpec((1, CONV_W - 1, d), lambda i, j: (i, 0, 0)),
                  pl.BlockSpec((d, d), lambda i, j: (0, 0), pipeline_mode=once)],
        out_specs=[pl.BlockSpec((1, tm, d), lambda i, j: (i, j, 0)),
                   pl.BlockSpec((1, CONV_W - 1, d), lambda i, j: (i, 0, 0))],
        out_shape=[jax.ShapeDtypeStruct((b, t, d), F32),
                   jax.ShapeDtypeStruct((b, CONV_W - 1, d), F32)],
        scratch_shapes=[pltpu.VMEM((tm + 8, d), F32)],
        compiler_params=_cparams("parallel", "arbitrary"),
        name="conv_mixer",
    )(x, g.reshape(1, d), w_in, w_conv, state, w_out)


def _pool_kernel(x_ref, g_ref, st_ref, wg_ref, sc_ref, o_ref, nst_ref, h_ref, *s_refs, pos0):
    t = pl.program_id(1)
    x = x_ref[0]
    tm, d = x.shape
    n_lv = len(POOL_WINDOWS)
    gw = d // n_lv
    base = 2 * (POOL_STATE + 1)
    lead = base - POOL_STATE
    end = base + tm

    @pl.when(t == 0)
    def _():
        h_ref[0:lead, :] = jnp.zeros((lead, d), F32)
        h_ref[lead:base, :] = st_ref[0]

    h = _rms(x, g_ref[...])
    h_ref[base:end, :] = h
    pos = pos0 + t * tm + lax.broadcasted_iota(I32, (tm, gw), 0)
    ys = []
    prev, c_prev = h_ref, 0
    for lv in range(1, n_lv + 1):
        w, shift, start = POOL_WINDOWS[lv - 1], 2 ** (lv - 1), 8 * lv
        c0 = (lv - 1) * gw
        cols = slice(c0 - c_prev, d - c_prev)
        cur = prev[start:end, cols] + prev[start - shift:end - shift, cols]
        if lv < n_lv:
            s_refs[lv - 1][start:end, :] = cur[:, gw:]
        win = cur[base - start:, :gw]
        count = jnp.minimum(w, pos + 1).astype(F32)
        dlt = win / count - h[:, c0:c0 + gw]
        ys.append(_dot(dlt, wg_ref[lv - 1]))
        if lv < n_lv:
            prev, c_prev = s_refs[lv - 1], c0 + gw
    y = jnp.concatenate(ys, axis=-1) * sc_ref[...]
    o_ref[0] = x + y
    last = h_ref[end - POOL_STATE:end, :]
    h_ref[lead:base, :] = last

    @pl.when(t == pl.num_programs(1) - 1)
    def _():
        nst_ref[0] = last


def _pool_mixer(x, g, state, w_group, scale, pos0):
    b, t, d = x.shape
    ng, gw, _ = w_group.shape
    tm = _row_tile(t, ROW_TILE)
    assert POOL_WINDOWS == tuple(2 ** (lv + 1) for lv in range(ng)) and tm >= POOL_STATE
    rows = tm + 2 * (POOL_STATE + 1)
    return pl.pallas_call(
        functools.partial(_pool_kernel, pos0=pos0),
        grid=(b, t // tm),
        in_specs=[pl.BlockSpec((1, tm, d), lambda i, j: (i, j, 0)),
                  pl.BlockSpec((1, d), lambda i, j: (0, 0)),
                  pl.BlockSpec((1, POOL_STATE, d), lambda i, j: (i, 0, 0)),
                  pl.BlockSpec((ng, gw, gw), lambda i, j: (0, 0, 0)),
                  pl.BlockSpec((1, d), lambda i, j: (0, 0))],
        out_specs=[pl.BlockSpec((1, tm, d), lambda i, j: (i, j, 0)),
                   pl.BlockSpec((1, POOL_STATE, d), lambda i, j: (i, 0, 0))],
        out_shape=[jax.ShapeDtypeStruct((b, t, d), F32),
                   jax.ShapeDtypeStruct((b, POOL_STATE, d), F32)],
        scratch_shapes=[pltpu.VMEM((rows, d - lv * gw), F32) for lv in range(ng)],
        compiler_params=_cparams("parallel", "arbitrary"),
        name="pool_mixer",
    )(x, g.reshape(1, d), state, w_group, scale.reshape(1, d))


def _proj_kernel(*refs, n_w, outs):
    x_ref, g_ref = refs[0], refs[1]
    w_refs = refs[2:2 + n_w]
    e_ref = refs[2 + n_w]
    o_refs = refs[3 + n_w:]
    h = _rms(x_ref[0], g_ref[...]).astype(BF16)
    ys = {}
    for (wi, mode, _, ep), o_ref in zip(outs, o_refs):
        if wi not in ys:
            ys[wi] = jnp.dot(h, w_refs[wi][...], preferred_element_type=F32)
        y = ys[wi]
        if ep == "log_sigmoid":
            u = -(y + e_ref[...])
            y = -(jnp.maximum(u, 0.0) + jnp.log1p(jnp.exp(-jnp.abs(u))))
        if mode == "rows":
            o_ref[0] = y.astype(o_ref.dtype)
        elif isinstance(mode, tuple) and mode[0] == "first":
            o_ref[0] = y[:, :mode[1]].astype(o_ref.dtype)
        elif isinstance(mode, tuple):
            o_ref[0] = jnp.transpose(y)[:mode[1], :].astype(o_ref.dtype)
        elif mode == "t":
            o_ref[0] = jnp.transpose(y).astype(o_ref.dtype)
        else:
            o_ref[0] = pltpu.einshape("m(hd)->mhd", y.astype(o_ref.dtype), d=mode)


def _proj(x, g, ws, outs, extra):
    b, t, d = x.shape
    tm = _row_tile(t, ROW_TILE)
    in_specs = [pl.BlockSpec((1, tm, d), lambda i, j: (i, j, 0)), pl.BlockSpec((1, d), lambda i, j: (0, 0))]
    in_specs += [pl.BlockSpec(w.shape, lambda i, j: (0, 0)) for w in ws]
    in_specs += [pl.BlockSpec(extra.shape, lambda i, j: (0, 0))]
    out_specs, out_shape = [], []
    for wi, mode, dt, _ in outs:
        n = ws[wi].shape[1]
        if isinstance(mode, tuple):
            mode, n = ("rows" if mode[0] == "first" else "t"), mode[1]
        if mode == "rows":
            out_specs.append(pl.BlockSpec((1, tm, n), lambda i, j: (i, j, 0)))
            out_shape.append(jax.ShapeDtypeStruct((b, t, n), dt))
        elif mode == "t":
            out_specs.append(pl.BlockSpec((1, n, tm), lambda i, j: (i, 0, j)))
            out_shape.append(jax.ShapeDtypeStruct((b, n, t), dt))
        else:
            out_specs.append(pl.BlockSpec((1, tm, n // mode, mode), lambda i, j: (i, j, 0, 0)))
            out_shape.append(jax.ShapeDtypeStruct((b, t, n // mode, mode), dt))
    return pl.pallas_call(
        functools.partial(_proj_kernel, n_w=len(ws), outs=tuple(outs)),
        grid=(b, t // tm),
        in_specs=in_specs,
        out_specs=out_specs,
        out_shape=out_shape,
        compiler_params=_cparams("parallel", "parallel"),
        name="norm_proj",
    )(x, g.reshape(1, d), *ws, extra)


def _outproj_kernel(x_ref, a_ref, w_ref, o_ref):
    o_ref[...] = x_ref[...] + _dot(a_ref[...], w_ref[...])


def _out_proj(x, a, w):
    n, d = x.shape
    tm = _row_tile(n, ROW_TILE)
    return pl.pallas_call(
        _outproj_kernel,
        grid=(n // tm,),
        in_specs=[pl.BlockSpec((tm, d), lambda i: (i, 0)),
                  pl.BlockSpec((tm, d), lambda i: (i, 0)),
                  pl.BlockSpec((d, d), lambda i: (0, 0))],
        out_specs=pl.BlockSpec((tm, d), lambda i: (i, 0)),
        out_shape=jax.ShapeDtypeStruct((n, d), F32),
        compiler_params=_cparams("parallel"),
        name="out_proj",
    )(x, a, w)


def _bias_table_kernel(rbt_ref, o_ref):
    width = o_ref.shape[-1]
    rel = BIAS_CENTER - lax.broadcasted_iota(I32, (1, width), 1)
    nb = N_BUCKETS // 2
    max_exact = nb // 2
    ret = (rel > 0).astype(I32) * nb
    n = jnp.abs(rel)
    nf = jnp.maximum(n, 1).astype(F32)
    large = max_exact + (jnp.log(nf / max_exact) / math.log(MAX_DIST / max_exact)
                         * (nb - max_exact)).astype(I32)
    large = jnp.minimum(large, nb - 1)
    bucket = ret + jnp.where(n < max_exact, n, large)
    acc = jnp.zeros(o_ref.shape, F32)
    for j in range(N_BUCKETS):
        acc = jnp.where(bucket == j, rbt_ref[:, j:j + 1], acc)
    o_ref[...] = acc * LOG2E


def _bias_tiles_kernel(rbt_ref, far_ref, near_ref, tab_ref):
    n_d, nh, kb, tq = near_ref.shape
    _bias_table_kernel(rbt_ref, tab_ref)
    far_ref[...] = jnp.broadcast_to(tab_ref[:, BIAS_CENTER + MAX_DIST:BIAS_CENTER + MAX_DIST + 1], far_ref.shape)
    for dd in range(n_d):
        s0 = BIAS_CENTER - (dd - 1) * kb - kb
        for h in range(nh):
            rows = jnp.broadcast_to(tab_ref[h:h + 1, s0:s0 + tq + kb], (kb, tq + kb))
            near_ref[dd, h] = pltpu.roll(rows, 0, 1, stride=1, stride_axis=0)[:, kb:]


def _bias_tiles(rel_bias, kb, tq):
    nh = rel_bias.shape[1]
    n_d = tq // kb + 1
    width = BIAS_CENTER + kb + tq + kb
    assert kb % LANE == 0 and tq % kb == 0 and kb >= MAX_DIST and BIAS_CENTER >= tq
    return pl.pallas_call(
        _bias_tiles_kernel,
        out_shape=[jax.ShapeDtypeStruct((nh, LANE), F32), jax.ShapeDtypeStruct((n_d, nh, kb, tq), F32)],
        scratch_shapes=[pltpu.VMEM((nh, width), F32)],
        name="bias_tiles",
    )(rel_bias.T)


def _sortable(x):
    x = jnp.where(x == 0.0, 0.0, x)
    bits = lax.bitcast_convert_type(x, I32)
    return jnp.where(bits < 0, bits ^ 0x7FFFFFFF, bits)


def _neg_inf_key():
    return int(np.float32(NEG_INF).view(np.int32)) ^ 0x7FFFFFFF


def _dsa_prompt_kernel(qt_ref, qit_ref, kwt_ref, kw_ref, k_ref, vt_ref, far_ref, near_ref, x_ref, wout_ref, o_ref,
                       key_ref, hi_ref, lo_ref, sel_ref, m_ref, l_ref, acc_ref, a_ref, *, top_k):
    qi = pl.program_id(1)
    tq = qt_ref.shape[2]
    n_keys = kw_ref.shape[1]
    kb_sz = LANE
    q0 = qi * tq
    nkb = jnp.minimum(n_keys, q0 + tq) // kb_sz
    negkey = _neg_inf_key()

    qlane = lax.broadcasted_iota(I32, (1, tq), 1)
    lim = ((q0 + qlane) // CHUNK + 1) * CHUNK
    krow = lax.broadcasted_iota(I32, (kb_sz, tq), 0)

    def kslice(kb):
        return pl.ds(pl.multiple_of(kb * kb_sz, kb_sz), kb_sz)

    sb = 2 * kb_sz
    srow = lax.broadcasted_iota(I32, (sb, tq), 0)

    def score_body(i, c):
        rows = pl.ds(pl.multiple_of(i * sb, sb), sb)
        kid = kw_ref[0, rows, :][:, :IDX_DIM].astype(BF16)
        sc = jnp.zeros((sb, tq), F32)
        for h in range(IDX_HEADS):
            s = jnp.dot(kid, qit_ref[0, h * IDX_DIM:(h + 1) * IDX_DIM, :], preferred_element_type=F32)
            sc = sc + jnp.maximum(s, 0.0) * kwt_ref[0, IDX_DIM + h:IDX_DIM + h + 1, :]
        sc = sc * ((IDX_DIM * IDX_HEADS) ** -0.5)
        key = jnp.where(i * sb + srow < lim, _sortable(sc), negkey)
        key_ref[rows, :] = key
        hi_ref[rows, :] = (key >> 16).astype(I16)
        lo_ref[rows, :] = ((key & 0xFFFF) - 0x8000).astype(I16)
        return c

    lax.fori_loop(0, nkb // 2, score_body, 0)

    def search16(ref):
        def bit_body(i, t_u):
            cand_u = t_u | jnp.left_shift(jnp.int32(1), 15 - i)
            cand = (cand_u - 0x8000).astype(I16)

            def body(j, a):
                ind = jnp.where(ref[pl.ds(pl.multiple_of(j * sb, sb), sb), :] >= cand,
                                jnp.ones((), I16), jnp.zeros((), I16))
                parts = [ind[16 * r:16 * (r + 1), :] for r in range(sb // 16)]
                while len(parts) > 1:
                    parts = [parts[r] + parts[r + 1] for r in range(0, len(parts), 2)]
                return a + parts[0]
            a = lax.fori_loop(0, nkb // 2, body, jnp.zeros((16, tq), I16))
            cnt = jnp.sum(a.astype(I32), axis=0, keepdims=True)
            return jnp.where(cnt >= top_k, cand_u, t_u)
        return lax.fori_loop(0, 16, bit_body, jnp.zeros((1, tq), I32))

    def count(pred_fn):
        def body(i, a):
            for u in range(2):
                kb = 2 * i + u
                ind = pred_fn(kb, key_ref[kslice(kb), :])
                a = a + jnp.sum(ind.reshape(kb_sz // 8, 8, tq), axis=0)
            return a
        a = lax.fori_loop(0, nkb // 2, body, jnp.zeros((8, tq), I32))
        return jnp.sum(a, axis=0, keepdims=True)

    t_hi = search16(hi_ref)
    t_hi16 = (t_hi - 0x8000).astype(I16)

    def lo_body(j, c):
        rows = pl.ds(pl.multiple_of(j * sb, sb), sb)
        hi = hi_ref[rows, :]
        lo_ref[rows, :] = jnp.where(hi == t_hi16, lo_ref[rows, :],
                                    jnp.where(hi > t_hi16, jnp.full((), 0x7FFF, I16), jnp.full((), -0x8000, I16)))
        return c

    lax.fori_loop(0, nkb // 2, lo_body, 0)
    t_s = (jnp.left_shift(t_hi, 16) | search16(lo_ref)) ^ INT_MIN

    def adm01(kb):
        return jnp.where(kb * kb_sz + krow < lim, 1.0, 0.0)

    def sel_body(kb, a):
        sel = jnp.where(key_ref[kslice(kb), :] >= t_s, adm01(kb), 0.0)
        sel_ref[kslice(kb), :] = sel
        return a + jnp.sum(sel.reshape(kb_sz // 8, 8, tq), axis=0)

    n_sel = jnp.sum(lax.fori_loop(0, nkb, sel_body, jnp.zeros((8, tq), F32)), axis=0, keepdims=True)

    @pl.when(jnp.max(n_sel) > top_k)
    def _():
        n_gt = count(lambda kb, key: jnp.where(key > t_s, 1, 0))
        need = (top_k - n_gt).astype(F32)
        r = lax.broadcasted_iota(I32, (kb_sz, kb_sz), 0)
        c = lax.broadcasted_iota(I32, (kb_sz, kb_sz), 1)
        ltri = jnp.where(c < r, 1.0, 0.0).astype(BF16)

        def tie_body(kb, carry):
            key = key_ref[kslice(kb), :]
            adm = adm01(kb)
            eq = jnp.where(key == t_s, adm, 0.0)
            rank = carry + jnp.dot(ltri, eq.astype(BF16), preferred_element_type=F32)
            keep = jnp.where(rank < need, eq, 0.0)
            sel_ref[kslice(kb), :] = jnp.where(key > t_s, adm, keep)
            return carry + jnp.sum(eq, axis=0, keepdims=True)

        lax.fori_loop(0, nkb, tie_body, jnp.zeros((1, tq), F32))

    m_ref[...] = jnp.full(m_ref.shape, NEG_INF, F32)
    l_ref[...] = jnp.zeros(l_ref.shape, F32)
    acc_ref[...] = jnp.zeros(acc_ref.shape, F32)
    c1 = (B_HD ** -0.5) * LOG2E

    def attend(k0, nk, bias2_fn):
        rows = pl.ds(pl.multiple_of(k0, LANE), nk)
        sel = sel_ref[rows, :] != 0.0
        ks = k_ref[0, rows, :]
        cols = []
        for h in range(B_HEADS):
            g = h // B_REP
            z = jnp.dot(ks[:, g * B_HD:(g + 1) * B_HD], qt_ref[0, h * B_HD:(h + 1) * B_HD, :],
                        preferred_element_type=F32)
            a = jnp.where(sel, z * c1 + bias2_fn(h), NEG_INF)
            a_ref[h, 0:nk, :] = a
            cols.append(jnp.max(a, axis=0, keepdims=True))
        m_old = m_ref[...]
        m_new = jnp.maximum(m_old, jnp.concatenate(cols, axis=0))
        alpha = jnp.exp2(m_old - m_new)
        m_ref[...] = m_new
        sums = []
        for h in range(B_HEADS):
            g = h // B_REP
            p = jnp.exp2(a_ref[h, 0:nk, :] - m_new[h:h + 1, :])
            sums.append(jnp.sum(p, axis=0, keepdims=True))
            hs = slice(h * B_HD, (h + 1) * B_HD)
            pv = jnp.dot(vt_ref[0, g * B_HD:(g + 1) * B_HD, rows], p.astype(BF16),
                         preferred_element_type=F32)
            acc_ref[hs, :] = alpha[h:h + 1, :] * acc_ref[hs, :] + pv
        l_ref[...] = alpha * l_ref[...] + jnp.concatenate(sums, axis=0)

    ab = near_ref.shape[2]
    n_far = jnp.maximum(q0 // ab - 1, 0)
    far_bias2 = far_ref[:, 0:1]

    def far_body(i, c):
        attend(i * ab, ab, lambda h: far_bias2[h:h + 1, :])
        return c

    lax.fori_loop(0, n_far, far_body, 0)

    def near_body(i, c):
        dd = i - q0 // ab + 1
        attend(i * ab, ab, lambda h: near_ref[dd, h])
        return c

    lax.fori_loop(n_far, nkb * kb_sz // ab, near_body, 0)

    _finish_heads(o_ref, x_ref, wout_ref, l_ref, acc_ref, B_HEADS)


def _dsa_attention_prompt(qt, qit, kwt, kw, k, vt, far, near, x, w_out):
    b, d, t = qt.shape
    tq = near.shape[3]
    top_k = min(TOPK_MAX, t // 4)
    ab = near.shape[2]
    assert t % tq == 0 and tq % CHUNK == 0 and tq % ab == 0 and ab % LANE == 0 and tq % (2 * LANE) == 0
    return pl.pallas_call(
        functools.partial(_dsa_prompt_kernel, top_k=top_k),
        grid=(b, t // tq),
        in_specs=[pl.BlockSpec((1, d, tq), lambda i, j: (i, 0, j)),
                  pl.BlockSpec((1, qit.shape[1], tq), lambda i, j: (i, 0, j)),
                  pl.BlockSpec((1, LANE, tq), lambda i, j: (i, 0, j)),
                  pl.BlockSpec((1, t, LANE), lambda i, j: (i, 0, 0)),
                  pl.BlockSpec((1, t, B_KV * B_HD), lambda i, j: (i, 0, 0)),
                  pl.BlockSpec((1, B_KV * B_HD, t), lambda i, j: (i, 0, 0)),
                  pl.BlockSpec(far.shape, lambda i, j: (0, 0)),
                  pl.BlockSpec(near.shape, lambda i, j: (0, 0, 0, 0)),
                  pl.BlockSpec((1, tq, d), lambda i, j: (i, j, 0)),
                  pl.BlockSpec((d, d), lambda i, j: (0, 0))],
        out_specs=pl.BlockSpec((1, tq, d), lambda i, j: (i, j, 0)),
        out_shape=jax.ShapeDtypeStruct((b, t, d), F32),
        scratch_shapes=[pltpu.VMEM((t, tq), I32), pltpu.VMEM((t, tq), I16), pltpu.VMEM((t, tq), I16),
                        pltpu.VMEM((t, tq), F32),
                        pltpu.VMEM((B_HEADS, tq), F32), pltpu.VMEM((B_HEADS, tq), F32),
                        pltpu.VMEM((d, tq), F32), pltpu.VMEM((B_HEADS, ab, tq), F32)],
        compiler_params=_cparams("parallel", "parallel"),
        name="dsa_attention",
    )(qt, qit, kwt, kw, k, vt, far, near, x, w_out)


def _dsa_cached_kernel(q_ref, qi_ref, wi_ref, rb_ref, kidx_ref, kp_ref, vp_ref, kn_ref, vn_ref, o_ref,
                       *, past, t_new, top_k):
    n_keys = past + t_new
    lp = kidx_ref.shape[1]
    negkey = _neg_inf_key()
    kpos = lax.broadcasted_iota(I32, (t_new, lp), 1)
    qpos = past + lax.broadcasted_iota(I32, (t_new, lp), 0)
    adm = kpos < (qpos // CHUNK + 1) * CHUNK

    s = _dot_nt(qi_ref[0], kidx_ref[0])
    w = jnp.maximum(s, 0.0) * wi_ref[0]
    sc = w[0:t_new]
    for h in range(1, IDX_HEADS):
        sc = sc + w[h * t_new:(h + 1) * t_new]
    sc = sc * ((IDX_DIM * IDX_HEADS) ** -0.5)
    key = jnp.where(adm, _sortable(sc), negkey)
    key = jnp.where(kpos < n_keys, key, INT_MIN)

    def bit_body(i, t_u):
        cand_u = t_u | jnp.left_shift(jnp.int32(1), 31 - i)
        cnt = jnp.sum(jnp.where(key >= (cand_u ^ INT_MIN), 1.0, 0.0), axis=1, keepdims=True)
        return jnp.where(cnt >= top_k, cand_u, t_u)

    t_s = lax.fori_loop(0, 32, bit_body, jnp.zeros((t_new, 1), I32)) ^ INT_MIN

    adm01 = jnp.where(adm, 1.0, 0.0)
    gt = jnp.where(key > t_s, adm01, 0.0)
    eq = jnp.where(key == t_s, adm01, 0.0)
    need = top_k - jnp.sum(jnp.where(key > t_s, 1.0, 0.0), axis=1, keepdims=True)
    r = lax.broadcasted_iota(I32, (LANE, LANE), 0)
    c = lax.broadcasted_iota(I32, (LANE, LANE), 1)
    utri = jnp.where(r < c, 1.0, 0.0).astype(BF16)
    carry = jnp.zeros((t_new, 1), F32)
    keeps = []
    for blk in range(lp // LANE):
        e = eq[:, blk * LANE:(blk + 1) * LANE]
        rank = carry + jnp.dot(e.astype(BF16), utri, preferred_element_type=F32)
        keeps.append(jnp.where(rank < need, e, 0.0))
        carry = carry + jnp.sum(e, axis=1, keepdims=True)
    sel = gt + jnp.concatenate(keeps, axis=1)

    near = max(past - MAX_DIST, 0) // LANE * LANE
    rel = (kpos - qpos)[:, near:]
    nb = N_BUCKETS // 2
    max_exact = nb // 2
    n = jnp.abs(rel)
    nf = jnp.maximum(n, 1).astype(F32)
    large = max_exact + (jnp.log(nf / max_exact) / math.log(MAX_DIST / max_exact)
                         * (nb - max_exact)).astype(I32)
    bucket = (rel > 0).astype(I32) * nb + jnp.where(n < max_exact, n, jnp.minimum(large, nb - 1))

    rows = B_REP * t_new
    sel_g = jnp.concatenate([sel] * B_REP, axis=0) != 0.0
    bucket_g = jnp.concatenate([bucket] * B_REP, axis=0)
    for g in range(B_KV):
        grp = lambda ref, n: ref[0, pl.ds(g, n, stride=B_KV), :]
        qg = q_ref[0, g * rows:(g + 1) * rows, :]
        rb = rb_ref[g * rows:(g + 1) * rows, :]
        bias_near = jnp.zeros((rows, lp - near), F32)
        for j in range(N_BUCKETS):
            bias_near = jnp.where(bucket_g == j, rb[:, j:j + 1], bias_near)
        bias = jnp.concatenate([jnp.broadcast_to(rb[:, nb - 1:nb], (rows, near)), bias_near], axis=1)
        zp = _dot_nt(qg, grp(kp_ref, past)) * (B_HD ** -0.5)
        zn = _dot_nt(qg, grp(kn_ref, t_new)) * (B_HD ** -0.5)
        ap = jnp.where(sel_g[:, :past], zp + bias[:, :past], NEG_INF)
        an = jnp.where(sel_g[:, past:n_keys], zn + bias[:, past:n_keys], NEG_INF)
        m = jnp.maximum(jnp.max(ap, axis=1, keepdims=True), jnp.max(an, axis=1, keepdims=True))
        pp, pn = jnp.exp(ap - m), jnp.exp(an - m)
        l = jnp.sum(pp, axis=1, keepdims=True) + jnp.sum(pn, axis=1, keepdims=True)
        o_ref[0, g * rows:(g + 1) * rows, :] = (_dot(pp, grp(vp_ref, past)) + _dot(pn, grp(vn_ref, t_new))) / l


def _dsa_attention_cached(q_rows, qi_rows, wi_col, rb_rows, kidx_all, k_past, v_past, k_new, v_new):
    b, rows, hd = q_rows.shape
    past, t_new = k_past.shape[1], k_new.shape[1]
    lp = kidx_all.shape[1]
    top_k = min(TOPK_MAX, (past + t_new) // 4)
    assert past % LANE == 0
    flat = lambda a: a.reshape(b, a.shape[1] * B_KV, B_HD)
    kv_spec = lambda n: pl.BlockSpec((1, n * B_KV, B_HD), lambda i: (i, 0, 0))
    return pl.pallas_call(
        functools.partial(_dsa_cached_kernel, past=past, t_new=t_new, top_k=top_k),
        grid=(b,),
        in_specs=[pl.BlockSpec((1, rows, hd), lambda i: (i, 0, 0)),
                  pl.BlockSpec((1,) + qi_rows.shape[1:], lambda i: (i, 0, 0)),
                  pl.BlockSpec((1,) + wi_col.shape[1:], lambda i: (i, 0, 0)),
                  pl.BlockSpec(rb_rows.shape, lambda i: (0, 0)),
                  pl.BlockSpec((1, lp, IDX_DIM), lambda i: (i, 0, 0)),
                  kv_spec(past), kv_spec(past), kv_spec(t_new), kv_spec(t_new)],
        out_specs=pl.BlockSpec((1, rows, hd), lambda i: (i, 0, 0)),
        out_shape=jax.ShapeDtypeStruct((b, rows, hd), F32),
        compiler_params=_cparams("parallel"),
        name="dsa_attention_cached",
    )(q_rows, qi_rows, wi_col, rb_rows, kidx_all, flat(k_past), flat(v_past), flat(k_new), flat(v_new))


def _cumsum_kernel(x_ref, o_ref):
    x = x_ref[0]
    n = x.shape[-1]
    lane = lax.broadcasted_iota(I32, x.shape, 1)
    s = 1
    while s < n:
        x = x + jnp.where(lane >= s, pltpu.roll(x, s, 1), 0.0)
        s *= 2
    o_ref[0] = x


def _cumsum_lanes(x):
    b, h, n = x.shape
    return pl.pallas_call(
        _cumsum_kernel,
        grid=(b,),
        in_specs=[pl.BlockSpec((1, h, n), lambda i: (i, 0, 0))],
        out_specs=pl.BlockSpec((1, h, n), lambda i: (i, 0, 0)),
        out_shape=jax.ShapeDtypeStruct((b, h, n), F32),
        compiler_params=_cparams("parallel"),
        name="logf_cumsum",
    )(x)


def _fox_init(m_ref, l_ref, acc_ref):
    m_ref[...] = jnp.full(m_ref.shape, NEG_INF, F32)
    l_ref[...] = jnp.zeros(l_ref.shape, F32)
    acc_ref[...] = jnp.zeros(acc_ref.shape, F32)


def _fox_tile(z_fn, pv_fn, cq, ck, mask, m_ref, l_ref, acc_ref, a_ref):
    c1 = (D_HD ** -0.5) * LOG2E
    cq2, ck2 = cq * LOG2E, ck * LOG2E
    cols = []
    for h in range(D_HEADS):
        a = z_fn(h) * c1 - ck2[:, h:h + 1]
        if mask is not None:
            a = jnp.where(mask, a, NEG_INF)
        a_ref[h] = a
        cols.append(jnp.max(a, axis=0, keepdims=True))
    m_old = m_ref[...]
    m_new = jnp.maximum(m_old, jnp.concatenate(cols, axis=0) + cq2)
    alpha = jnp.exp2(m_old - m_new)
    shift = m_new - cq2
    m_ref[...] = m_new
    sums = []
    for h in range(D_HEADS):
        p = jnp.exp2(a_ref[h] - shift[h:h + 1, :])
        sums.append(jnp.sum(p, axis=0, keepdims=True))
        hs = slice(h * D_HD, (h + 1) * D_HD)
        acc_ref[hs, :] = alpha[h:h + 1, :] * acc_ref[hs, :] + pv_fn(h, p.astype(BF16))
    l_ref[...] = alpha * l_ref[...] + jnp.concatenate(sums, axis=0)


def _hs(h):
    return slice(h * D_HD, (h + 1) * D_HD)


def _fox_prompt_kernel(qt_ref, k_ref, vt_ref, cq_ref, ck_ref, x_ref, wout_ref, o_ref,
                       m_ref, l_ref, acc_ref, a_ref):
    qi, step = pl.program_id(1), pl.program_id(2)
    tq, tk = qt_ref.shape[2], a_ref.shape[1]
    q0 = qi * tq
    ki = step - (pl.num_programs(2) - 1 - (q0 + tq - 1) // tk)
    k0 = ki * tk
    keys = pl.ds(pl.multiple_of(jnp.maximum(k0, 0), tk), tk)

    @pl.when(step == 0)
    def _():
        _fox_init(m_ref, l_ref, acc_ref)

    def run(masked):
        mask = None
        if masked:
            mask = (k0 + lax.broadcasted_iota(I32, (tk, tq), 0)) <= (q0 + lax.broadcasted_iota(I32, (tk, tq), 1))
        _fox_tile(lambda h: jnp.dot(k_ref[0, keys, _hs(h)], qt_ref[0, _hs(h), :], preferred_element_type=F32),
                  lambda h, p: jnp.dot(vt_ref[0, _hs(h), keys], p, preferred_element_type=F32),
                  cq_ref[0], ck_ref[0, keys, :], mask, m_ref, l_ref, acc_ref, a_ref)

    fully_visible = k0 + tk - 1 <= q0
    pl.when(jnp.logical_and(ki >= 0, fully_visible))(lambda: run(False))
    pl.when(jnp.logical_and(ki >= 0, jnp.logical_not(fully_visible)))(lambda: run(True))

    @pl.when(step == pl.num_programs(2) - 1)
    def _():
        _finish_heads(o_ref, x_ref, wout_ref, l_ref, acc_ref, D_HEADS)


def _fox_attention_prompt(qt, k, vt, cum_t, cum, x, w_out, tq, tk):
    b, d, t = qt.shape
    nq, nk = t // tq, t // tk
    return pl.pallas_call(
        _fox_prompt_kernel,
        grid=(b, nq, nk),
        in_specs=[pl.BlockSpec((1, d, tq), lambda i, j, kk: (i, 0, j)),
                  pl.BlockSpec((1, t, d), lambda i, j, kk: (i, 0, 0)),
                  pl.BlockSpec((1, d, t), lambda i, j, kk: (i, 0, 0)),
                  pl.BlockSpec((1, D_HEADS, tq), lambda i, j, kk: (i, 0, j)),
                  pl.BlockSpec((1, t, D_HEADS), lambda i, j, kk: (i, 0, 0)),
                  pl.BlockSpec((1, tq, d), lambda i, j, kk: (i, j, 0)),
                  pl.BlockSpec((d, d), lambda i, j, kk: (0, 0))],
        out_specs=pl.BlockSpec((1, tq, d), lambda i, j, kk: (i, j, 0)),
        out_shape=jax.ShapeDtypeStruct((b, t, d), F32),
        scratch_shapes=[pltpu.VMEM((D_HEADS, tq), F32), pltpu.VMEM((D_HEADS, tq), F32),
                        pltpu.VMEM((d, tq), F32), pltpu.VMEM((D_HEADS, tk, tq), F32)],
        compiler_params=_cparams("parallel", "parallel", "arbitrary"),
        name="fox_attention",
    )(qt, k, vt, cum_t, cum, x, w_out)


def _fox_cached_kernel(q_ref, kp_ref, vp_ref, kn_ref, vn_ref, cq_ref, ckp_ref, ckn_ref, o_ref,
                       m_ref, l_ref, acc_ref, *, t_new):
    ki = pl.program_id(1)
    n_past = pl.num_programs(1) - 1
    rows = q_ref.shape[1]
    c1 = (D_HD ** -0.5) * LOG2E

    @pl.when(ki == 0)
    def _():
        m_ref[...] = jnp.full(m_ref.shape, NEG_INF, F32)
        l_ref[...] = jnp.zeros(l_ref.shape, F32)
        acc_ref[...] = jnp.zeros(acc_ref.shape, F32)

    def tile(k2d, v2d, ck_row, causal):
        cols = k2d.shape[0]
        a = _dot_nt(q_ref[0], k2d) * c1 - ck_row * LOG2E
        r = lax.broadcasted_iota(I32, (rows, cols), 0)
        c = lax.broadcasted_iota(I32, (rows, cols), 1)
        ok = (c % D_HEADS) == (r // t_new)
        if causal:
            ok = jnp.logical_and(ok, (c // D_HEADS) <= (r % t_new))
        a = jnp.where(ok, a, NEG_INF)
        cq2 = cq_ref[0] * LOG2E
        m_old = m_ref[...]
        m_new = jnp.maximum(m_old, jnp.max(a, axis=1, keepdims=True) + cq2)
        alpha = jnp.exp2(m_old - m_new)
        p = jnp.exp2(a - (m_new - cq2))
        l_ref[...] = alpha * l_ref[...] + jnp.sum(p, axis=1, keepdims=True)
        acc_ref[...] = alpha * acc_ref[...] + _dot(p, v2d)
        m_ref[...] = m_new

    @pl.when(ki < n_past)
    def _():
        tk = kp_ref.shape[1]
        tile(kp_ref[0].reshape(tk * D_HEADS, D_HD), vp_ref[0].reshape(tk * D_HEADS, D_HD), ckp_ref[0], False)

    @pl.when(ki == n_past)
    def _():
        tile(kn_ref[0].reshape(t_new * D_HEADS, D_HD), vn_ref[0].reshape(t_new * D_HEADS, D_HD), ckn_ref[0], True)
        o_ref[0] = acc_ref[...] / l_ref[...]


def _fox_attention_cached(q_rows, k_past, v_past, k_new, v_new, cq_col, ck_past, ck_new, tk):
    b, rows, hd = q_rows.shape
    past, t_new = k_past.shape[1], k_new.shape[1]
    n_past = past // tk
    pidx = lambda i, kk: (i, jnp.minimum(kk, n_past - 1), 0, 0)
    return pl.pallas_call(
        functools.partial(_fox_cached_kernel, t_new=t_new),
        grid=(b, n_past + 1),
        in_specs=[pl.BlockSpec((1, rows, hd), lambda i, kk: (i, 0, 0)),
                  pl.BlockSpec((1, tk, D_HEADS, D_HD), pidx),
                  pl.BlockSpec((1, tk, D_HEADS, D_HD), pidx),
                  pl.BlockSpec((1, t_new, D_HEADS, D_HD), lambda i, kk: (i, 0, 0, 0)),
                  pl.BlockSpec((1, t_new, D_HEADS, D_HD), lambda i, kk: (i, 0, 0, 0)),
                  pl.BlockSpec((1, rows, 1), lambda i, kk: (i, 0, 0)),
                  pl.BlockSpec((1, 1, tk * D_HEADS), lambda i, kk: (i, 0, jnp.minimum(kk, n_past - 1))),
                  pl.BlockSpec((1, 1, t_new * D_HEADS), lambda i, kk: (i, 0, 0))],
        out_specs=pl.BlockSpec((1, rows, hd), lambda i, kk: (i, 0, 0)),
        out_shape=jax.ShapeDtypeStruct((b, rows, hd), F32),
        scratch_shapes=[pltpu.VMEM((rows, 1), F32), pltpu.VMEM((rows, 1), F32), pltpu.VMEM((rows, hd), F32)],
        compiler_params=_cparams("parallel", "arbitrary"),
        name="fox_attention_cached",
    )(q_rows, k_past, v_past, k_new, v_new, cq_col, ck_past, ck_new)


def _pad_rows(a, rows):
    if a.shape[1] == rows:
        return a
    return jnp.pad(a, ((0, 0), (0, rows - a.shape[1])) + ((0, 0),) * (a.ndim - 2))


def _round_up(n, m):
    return -(-n // m) * m


def _dsa_mixer(x, g, k_past, v_past, ki_past, w, w_out, rel_bias):
    b, t, d = x.shape
    past = k_past.shape[1]
    n_keys = past + t
    if past == 0:
        qt, k4, kb, v4, vt, qit, kw, kwt, ki_t = _proj(
            x, g, w, ((0, "t", BF16, None), (1, B_HD, F32, None), (1, "rows", BF16, None),
                      (2, B_HD, F32, None), (2, "t", BF16, None), (3, "t", BF16, None),
                      (4, "rows", F32, None), (4, "t", F32, None), (4, ("first_t", IDX_DIM), F32, None)),
            jnp.zeros((1, LANE), F32))
        far, near = _bias_tiles(rel_bias, DSA_TILE, DSA_TILE)
        y = _dsa_attention_prompt(qt, qit, kwt, kw, kb, vt, far, near, x, w_out)
        return (y, k4, v4, jnp.swapaxes(ki_t, 1, 2))
    q, k4, v4, qidx, kw = _proj(
        x, g, w, ((0, "rows", BF16, None), (1, B_HD, F32, None), (2, B_HD, F32, None),
                  (3, "rows", BF16, None), (4, "rows", F32, None)),
        jnp.zeros((1, LANE), F32))
    ki = kw[:, :, :IDX_DIM]
    to_rows = lambda a, nh: jnp.swapaxes(a.reshape(b, t, nh, -1), 1, 2).reshape(b, nh * t, -1)
    kidx_all = _pad_rows(jnp.concatenate([ki_past, ki], axis=1), _round_up(n_keys, LANE))
    o = _dsa_attention_cached(to_rows(q, B_HEADS), to_rows(qidx, IDX_HEADS),
                              to_rows(kw[:, :, IDX_DIM:IDX_DIM + IDX_HEADS], IDX_HEADS),
                              jnp.repeat(rel_bias.T, t, axis=0), kidx_all, k_past, v_past, k4, v4)
    o = jnp.swapaxes(o.reshape(b, B_HEADS, t, B_HD), 1, 2).reshape(b * t, d)
    y = _out_proj(x.reshape(b * t, d), o, w_out).reshape(b, t, d)
    return (y, k4, v4, ki)


def _fox_mixer(x, g, k_past, v_past, lf_past, w, b_f, w_out):
    b, t, d = x.shape
    past = k_past.shape[1]
    heads4 = ((1, D_HD, F32, None), (2, D_HD, F32, None))
    if past == 0:
        tq = tk = min(FOX_TILE, t)
        k4, v4, logf_t, qt, kb, vt = _proj(
            x, g, w, heads4 + ((3, ("first_t", D_HEADS), F32, "log_sigmoid"), (0, "t", BF16, None),
                               (1, "rows", BF16, None), (2, "t", BF16, None)), b_f)
        cum_t = _cumsum_lanes(logf_t)
        y = _fox_attention_prompt(qt, kb, vt, cum_t, jnp.swapaxes(cum_t, 1, 2), x, w_out, tq, tk)
        return (y, k4, v4, jnp.swapaxes(logf_t, 1, 2))
    else:
        tk = math.gcd(past, FOX_TILE)
        k4, v4, logf, q = _proj(
            x, g, w, heads4 + ((3, ("first", D_HEADS), F32, "log_sigmoid"), (0, "rows", BF16, None)), b_f)
        lf_all = _pad_rows(jnp.concatenate([lf_past, logf], axis=1), _round_up(past + t, LANE))
        cum = jnp.swapaxes(_cumsum_lanes(jnp.swapaxes(lf_all, 1, 2)), 1, 2)[:, :past + t]
        ck = cum.reshape(b, 1, (past + t) * D_HEADS)
        to_rows = lambda a: jnp.swapaxes(a.reshape(b, t, D_HEADS, -1), 1, 2).reshape(b, D_HEADS * t, -1)
        o = _fox_attention_cached(to_rows(q), k_past, v_past, k4, v4, to_rows(cum[:, past:]),
                                  ck[:, :, :past * D_HEADS], ck[:, :, past * D_HEADS:], tk)
        o = jnp.swapaxes(o.reshape(b, D_HEADS, t, D_HD), 1, 2).reshape(b, t, d)
    y = _out_proj(x.reshape(b * t, d), o.reshape(b * t, d), w_out).reshape(b, t, d)
    return (y, k4, v4, logf)


def _run_group(x, pos0, a_st, b_k, b_v, b_ki, c_st, d_k, d_v, d_lf, mem_k, mem_v, prm):
    b, t, d = x.shape
    depth = prm["norm_mix"].shape[0]
    new = {n: [] for n in ("a", "bk", "bv", "bki", "c", "dk", "dv", "dlf")}
    for i in range(depth):
        kind, j = i % 4, i // 4
        g = prm["norm_mix"][i]
        if kind == 0:
            x, st = _conv_mixer(x, g, prm["a_w_in"][j], prm["a_conv"][j], a_st[j], prm["a_w_out"][j])
            new["a"].append(st)
        elif kind == 1:
            x, kk, vv, ki = _dsa_mixer(x, g, b_k[j], b_v[j], b_ki[j], prm["b_w"][j], prm["b_w_out"][j],
                                       prm["rel_bias"])
            new["bk"].append(kk); new["bv"].append(vv); new["bki"].append(ki)
        elif kind == 2:
            x, st = _pool_mixer(x, g, c_st[j], prm["c_w_group"][j], prm["c_scale"][j], pos0)
            new["c"].append(st)
        else:
            x, kk, vv, lf = _fox_mixer(x, g, d_k[j], d_v[j], d_lf[j], prm["d_w"][j], prm["d_b_f"][j],
                                       prm["d_w_out"][j])
            new["dk"].append(kk); new["dv"].append(vv); new["dlf"].append(lf)
        x = _xattn(x, prm["norm_xattn"], prm["xa_wq"], mem_k, mem_v, prm["xa_wo"], i)
        last = i == depth - 1
        x = _ffn(x.reshape(b * t, d), prm["norm_ffn"], prm["ffn_w1"], prm["ffn_w2"],
                 prm["final_norm"], i, last).reshape(b, t, d)
    return (x,) + tuple(jnp.stack(new[n]) for n in ("a", "bk", "bv", "bki", "c", "dk", "dv", "dlf"))


def kernel(x_prompt, x_sample, state_a_conv, cache_b_k, cache_b_v, cache_b_kidx, state_c_pool,
           cache_d_k, cache_d_v, cache_d_logf, cache_mem_k, cache_mem_v, mem_prompt,
           norm_mix, norm_xattn, norm_mem, norm_ffn, final_norm,
           a_w_in, a_conv, a_w_out, b_w_in, b_w_out, rel_bias, c_w_group, c_scale,
           d_w_in, d_b_f, d_w_out, xa_wq, xa_wkv, xa_wo, ffn_w1, ffn_w2):
    bp, _, d = x_prompt.shape
    n_b, n_d = b_w_in.shape[0], d_w_in.shape[0]
    bf = lambda w: w.astype(BF16)

    def split_cols(w, widths):
        out, c = [], 0
        for wd in widths:
            piece = w[:, c:c + wd]
            c += wd
            if wd % LANE:
                piece = jnp.pad(piece, ((0, 0), (0, _round_up(wd, LANE) - wd)))
            out.append(bf(piece))
        assert c == w.shape[1]
        return out

    b_q, b_kvw = B_HEADS * B_HD, B_KV * B_HD
    b_w = [split_cols(b_w_in[j], (b_q, b_kvw, b_kvw, IDX_HEADS * IDX_DIM, IDX_DIM + IDX_HEADS))
           for j in range(n_b)]
    d_w = [split_cols(d_w_in[j], (d, d, d, D_HEADS)) for j in range(n_d)]
    d_bf = [jnp.pad(d_b_f[j], (0, LANE - D_HEADS)).reshape(1, LANE) for j in range(n_d)]

    prm = {"norm_mix": norm_mix, "norm_xattn": norm_xattn, "norm_ffn": norm_ffn, "final_norm": final_norm,
           "a_w_in": bf(a_w_in), "a_conv": a_conv, "a_w_out": bf(a_w_out),
           "b_w": b_w, "b_w_out": bf(b_w_out), "rel_bias": rel_bias,
           "c_w_group": bf(c_w_group), "c_scale": c_scale,
           "d_w": d_w, "d_b_f": d_bf, "d_w_out": bf(d_w_out),
           "xa_wq": xa_wq, "xa_wo": xa_wo, "ffn_w1": ffn_w1, "ffn_w2": ffn_w2}

    mk, mv, mk_rows, mv_rows = _memory_kv(mem_prompt, norm_mem, xa_wkv)

    n_a, n_c = a_w_in.shape[0], c_w_group.shape[0]
    z = lambda *s: jnp.zeros(s, F32)
    gp = _run_group(x_prompt, 0,
                    z(n_a, bp, CONV_W - 1, d),
                    z(n_b, bp, 0, B_KV, B_HD), z(n_b, bp, 0, B_KV, B_HD), z(n_b, bp, 0, IDX_DIM),
                    z(n_c, bp, POOL_STATE, d),
                    z(n_d, bp, 0, D_HEADS, D_HD), z(n_d, bp, 0, D_HEADS, D_HD), z(n_d, bp, 0, D_HEADS),
                    mk_rows, mv_rows, prm)

    past_len = cache_b_k.shape[2]
    gs = _run_group(x_sample, past_len, state_a_conv, cache_b_k, cache_b_v, cache_b_kidx, state_c_pool,
                    cache_d_k, cache_d_v, cache_d_logf, cache_mem_k, cache_mem_v, prm)

    (y_p, a_p, bk_p, bv_p, bki_p, c_p, dk_p, dv_p, dlf_p) = gp
    (y_s, a_s, bk_s, bv_s, bki_s, c_s, dk_s, dv_s, dlf_s) = gs
    return (y_p, y_s, a_p, a_s, bk_p, bv_p, bki_p, bk_s, bv_s, bki_s, c_p, c_s,
            dk_p, dv_p, dlf_p, dk_s, dv_s, dlf_s, mk, mv)
```

```python
import functools
import math

import jax
import jax.numpy as jnp
import numpy as np
from jax import lax
from jax.experimental import pallas as pl
from jax.experimental.pallas import tpu as pltpu

F32 = jnp.float32
BF16 = jnp.bfloat16
I32 = jnp.int32
I16 = jnp.int16

EPS = 1e-6
NEG_INF = -1e30
LOG2E = math.log2(math.e)
CHUNK = 64
LANE = 128
VMEM_LIMIT = 48 * 1024 * 1024

ROW_TILE = 512
WIDE_ROW_TILE = 1024
FF_TILE = 1024
FOX_TILE = 4 * LANE
DSA_TILE = 2 * LANE

CONV_W = 3
POOL_WINDOWS = (2, 4, 8, 16)
POOL_STATE = max(POOL_WINDOWS) - 1
B_HEADS, B_KV, B_HD = 8, 2, 128
B_REP = B_HEADS // B_KV
IDX_HEADS, IDX_DIM = 8, 64
TOPK_MAX = 256
N_BUCKETS, MAX_DIST = 32, 128
D_HEADS, D_HD = 8, 128
MEM_HEADS = 4
INT_MIN = -2147483648
BIAS_CENTER = 2 * LANE


def _cparams(*sem):
    return pltpu.CompilerParams(dimension_semantics=sem, vmem_limit_bytes=VMEM_LIMIT)


def _dot(a, b):
    return jnp.dot(a.astype(BF16), b.astype(BF16), preferred_element_type=F32)


def _dot_nt(a, b):
    return lax.dot_general(a.astype(BF16), b.astype(BF16), (((1,), (1,)), ((), ())),
                           preferred_element_type=F32)


def _dot_tn(a, b):
    return lax.dot_general(a.astype(BF16), b.astype(BF16), (((0,), (0,)), ((), ())),
                           preferred_element_type=F32)


def _rms(x, g):
    return x * lax.rsqrt(jnp.mean(x * x, axis=-1, keepdims=True) + EPS) * g


def _finish_heads(o_ref, x_ref, w_ref, l_ref, acc_ref, n_heads):
    hd = acc_ref.shape[0] // n_heads
    inv_l = 1.0 / l_ref[...]
    heads_t = jnp.concatenate([acc_ref[h * hd:(h + 1) * hd, :] * inv_l[h:h + 1, :] for h in range(n_heads)],
                              axis=0)
    o_ref[0] = x_ref[0] + _dot_tn(heads_t, w_ref[...])


def _row_tile(n, cap):
    t = min(n, cap)
    assert n % t == 0
    return t


def _memkv_kernel(mem_ref, g_ref, w_ref, k_ref, v_ref, kb_ref, vb_ref):
    bb, nm, d = mem_ref.shape
    m = mem_ref[...].reshape(bb * nm, d)
    mn = m * lax.rsqrt(jnp.mean(m * m, axis=-1, keepdims=True) + EPS)
    h = (mn * g_ref[0]).astype(BF16)
    hd = d // MEM_HEADS
    k = jnp.dot(h, w_ref[0, :, :d].astype(BF16), preferred_element_type=F32)
    v = jnp.dot(h, w_ref[0, :, d:].astype(BF16), preferred_element_type=F32)
    for i in range(bb):
        rows = slice(i * nm, (i + 1) * nm)
        kb_ref[0, i] = k[rows].astype(BF16)
        vb_ref[0, i] = v[rows].astype(BF16)
        k_ref[0, i] = pltpu.einshape("m(hd)->mhd", k[rows], d=hd)
        v_ref[0, i] = pltpu.einshape("m(hd)->mhd", v[rows], d=hd)


def _memory_kv(mem, g_mem, w_kv):
    depth, d = g_mem.shape
    b, nm, _ = mem.shape
    hd = d // MEM_HEADS
    out = jax.ShapeDtypeStruct((depth, b, nm, MEM_HEADS, hd), F32)
    out_b = jax.ShapeDtypeStruct((depth, b, nm, d), BF16)
    bb = math.gcd(b, 2)
    heads_spec = pl.BlockSpec((1, bb, nm, MEM_HEADS, hd), lambda l, i: (l, i, 0, 0, 0))
    rows_spec = pl.BlockSpec((1, bb, nm, d), lambda l, i: (l, i, 0, 0))
    return pl.pallas_call(
        _memkv_kernel,
        grid=(depth, b // bb),
        in_specs=[pl.BlockSpec((bb, nm, d), lambda l, i: (i, 0, 0)),
                  pl.BlockSpec((1, 1, d), lambda l, i: (l, 0, 0)),
                  pl.BlockSpec((1, d, 2 * d), lambda l, i: (l, 0, 0))],
        out_specs=[heads_spec, heads_spec, rows_spec, rows_spec],
        out_shape=[out, out, out_b, out_b],
        compiler_params=_cparams("parallel", "parallel"),
        name="memory_kv",
    )(mem, g_mem.reshape(depth, 1, d), w_kv)


def _ffn_kernel(x_ref, g_ref, w1_ref, w2_ref, gf_ref, o_ref, h_ref, acc_ref, *, final_norm):
    j = pl.program_id(1)

    @pl.when(j == 0)
    def _():
        h_ref[...] = _rms(x_ref[...], g_ref[...]).astype(BF16)
        acc_ref[...] = jnp.zeros_like(acc_ref)

    u = jnp.maximum(jnp.dot(h_ref[...], w1_ref[...].astype(BF16), preferred_element_type=F32), 0.0)
    acc_ref[...] += jnp.dot((u * u).astype(BF16), w2_ref[...].astype(BF16), preferred_element_type=F32)

    @pl.when(j == pl.num_programs(1) - 1)
    def _():
        y = x_ref[...] + acc_ref[...]
        o_ref[...] = _rms(y, gf_ref[...]) if final_norm else y


def _ffn(x, g, w1, w2, gf, layer, final_norm):
    n, d = x.shape
    f = w1.shape[2]
    tm = _row_tile(n, WIDE_ROW_TILE)
    tf = FF_TILE
    return pl.pallas_call(
        functools.partial(_ffn_kernel, final_norm=final_norm),
        grid=(n // tm, f // tf),
        in_specs=[pl.BlockSpec((tm, d), lambda i, j: (i, 0)),
                  pl.BlockSpec((None, 1, d), lambda i, j: (layer, 0, 0)),
                  pl.BlockSpec((None, d, tf), lambda i, j: (layer, 0, j)),
                  pl.BlockSpec((None, tf, d), lambda i, j: (layer, j, 0)),
                  pl.BlockSpec((1, d), lambda i, j: (0, 0))],
        out_specs=pl.BlockSpec((tm, d), lambda i, j: (i, 0)),
        out_shape=jax.ShapeDtypeStruct((n, d), F32),
        scratch_shapes=[pltpu.VMEM((tm, d), BF16), pltpu.VMEM((tm, d), F32)],
        compiler_params=_cparams("parallel", "arbitrary"),
        name="ffn",
    )(x, g.reshape(-1, 1, d), w1, w2, gf.reshape(1, d))


def _xattn_kernel(x_ref, g_ref, wq_ref, mk_ref, mv_ref, wo_ref, o_ref):
    x = x_ref[0]
    d = x.shape[-1]
    hd = d // MEM_HEADS
    h = _rms(x, g_ref[...]).astype(BF16)
    q = jnp.dot(h, wq_ref[...].astype(BF16), preferred_element_type=F32)
    outs = []
    if len(mk_ref.shape) == 3:
        mk = pltpu.einshape("mhd->m(hd)", mk_ref[...]).astype(BF16)
        mv = pltpu.einshape("mhd->m(hd)", mv_ref[...]).astype(BF16)
    else:
        mk, mv = mk_ref[...], mv_ref[...]
    for hh in range(MEM_HEADS):
        sl = slice(hh * hd, (hh + 1) * hd)
        kh, vh = mk[:, sl], mv[:, sl]
        s = _dot_nt(q[:, sl], kh) * (hd ** -0.5)
        m = jnp.max(s, axis=-1, keepdims=True)
        p = jnp.exp(s - m)
        l = jnp.sum(p, axis=-1, keepdims=True)
        outs.append(_dot(p, vh) / l)
    o = jnp.concatenate(outs, axis=-1)
    o_ref[0] = x + _dot(o, wo_ref[...])


def _xattn(x, g, wq, mk, mv, wo, layer):
    b, t, d = x.shape
    tm = _row_tile(t, WIDE_ROW_TILE)
    kv_spec = pl.BlockSpec((None, None) + mk.shape[2:], lambda i, j: (layer, i) + (0,) * (mk.ndim - 2))
    return pl.pallas_call(
        _xattn_kernel,
        grid=(b, t // tm),
        in_specs=[pl.BlockSpec((1, tm, d), lambda i, j: (i, j, 0)),
                  pl.BlockSpec((None, 1, d), lambda i, j: (layer, 0, 0)),
                  pl.BlockSpec((None, d, d), lambda i, j: (layer, 0, 0)),
                  kv_spec, kv_spec,
                  pl.BlockSpec((None, d, d), lambda i, j: (layer, 0, 0))],
        out_specs=pl.BlockSpec((1, tm, d), lambda i, j: (i, j, 0)),
        out_shape=jax.ShapeDtypeStruct((b, t, d), F32),
        compiler_params=_cparams("parallel", "parallel"),
        name="xattn",
    )(x, g.reshape(-1, 1, d), wq, mk, mv, wo)


def _conv_kernel(x_ref, g_ref, win_ref, wc_ref, st_ref, wout_ref, o_ref, nst_ref, z_ref):
    t = pl.program_id(1)
    x = x_ref[0]
    tm, d = x.shape
    pad = 8

    @pl.when(t == 0)
    def _():
        z_ref[pad - 2:pad, :] = st_ref[0]

    h = _rms(x, g_ref[...]).astype(BF16)
    bg = jnp.dot(h, win_ref[:, 0:d], preferred_element_type=F32)
    cg = jnp.dot(h, win_ref[:, d:2 * d], preferred_element_type=F32)
    u = jnp.dot(h, win_ref[:, 2 * d:3 * d], preferred_element_type=F32)
    z = cg * u
    z_ref[pad:pad + tm, :] = z
    conv = (z_ref[pad - 2:pad - 2 + tm, :] * wc_ref[0:1, :]
            + z_ref[pad - 1:pad - 1 + tm, :] * wc_ref[1:2, :]
            + z * wc_ref[2:3, :])
    o_ref[0] = x + _dot(bg * conv, wout_ref[...])
    last = z_ref[pad + tm - 2:pad + tm, :]
    z_ref[pad - 2:pad, :] = last

    @pl.when(t == pl.num_programs(1) - 1)
    def _():
        nst_ref[0] = last


def _conv_mixer(x, g, w_in, w_conv, state, w_out):
    b, t, d = x.shape
    tm = _row_tile(t, WIDE_ROW_TILE)
    once = pl.Buffered(1)
    return pl.pallas_call(
        _conv_kernel,
        grid=(b, t // tm),
        in_specs=[pl.BlockSpec((1, tm, d), lambda i, j: (i, j, 0)),
                  pl.BlockSpec((1, d), lambda i, j: (0, 0)),
                  pl.BlockSpec((d, 3 * d), lambda i, j: (0, 0), pipeline_mode=once),
                  pl.BlockSpec((CONV_W, d), lambda i, j: (0, 0)),
                  pl.BlockSpec((1, CONV_W - 1, d), lambda i, j: (i, 0, 0)),
                  pl.BlockSpec((d, d), lambda i, j: (0, 0), pipeline_mode=once)],
        out_specs=[pl.BlockSpec((1, tm, d), lambda i, j: (i, j, 0)),
                   pl.BlockSpec((1, CONV_W - 1, d), lambda i, j: (i, 0, 0))],
        out_shape=[jax.ShapeDtypeStruct((b, t, d), F32),
                   jax.ShapeDtypeStruct((b, CONV_W - 1, d), F32)],
        scratch_shapes=[pltpu.VMEM((tm + 8, d), F32)],
        compiler_params=_cparams("parallel", "arbitrary"),
        name="conv_mixer",
    )(x, g.reshape(1, d), w_in, w_conv, state, w_out)


def _pool_kernel(x_ref, g_ref, st_ref, wg_ref, sc_ref, o_ref, nst_ref, h_ref, *s_refs, pos0):
    t = pl.program_id(1)
    x = x_ref[0]
    tm, d = x.shape
    n_lv = len(POOL_WINDOWS)
    gw = d // n_lv
    base = 2 * (POOL_STATE + 1)
    lead = base - POOL_STATE
    end = base + tm

    @pl.when(t == 0)
    def _():
        h_ref[0:lead, :] = jnp.zeros((lead, d), F32)
        h_ref[lead:base, :] = st_ref[0]

    h = _rms(x, g_ref[...])
    h_ref[base:end, :] = h
    pos = pos0 + t * tm + lax.broadcasted_iota(I32, (tm, gw), 0)
    ys = []
    prev, c_prev = h_ref, 0
    for lv in range(1, n_lv + 1):
        w, shift, start = POOL_WINDOWS[lv - 1], 2 ** (lv - 1), 8 * lv
        c0 = (lv - 1) * gw
        cols = slice(c0 - c_prev, d - c_prev)
        cur = prev[start:end, cols] + prev[start - shift:end - shift, cols]
        if lv < n_lv:
            s_refs[lv - 1][start:end, :] = cur[:, gw:]
        win = cur[base - start:, :gw]
        count = jnp.minimum(w, pos + 1).astype(F32)
        dlt = win / count - h[:, c0:c0 + gw]
        ys.append(_dot(dlt, wg_ref[lv - 1]))
        if lv < n_lv:
            prev, c_prev = s_refs[lv - 1], c0 + gw
    y = jnp.concatenate(ys, axis=-1) * sc_ref[...]
    o_ref[0] = x + y
    last = h_ref[end - POOL_STATE:end, :]
    h_ref[lead:base, :] = last

    @pl.when(t == pl.num_programs(1) - 1)
    def _():
        nst_ref[0] = last


def _pool_mixer(x, g, state, w_group, scale, pos0):
    b, t, d = x.shape
    ng, gw, _ = w_group.shape
    tm = _row_tile(t, ROW_TILE)
    assert POOL_WINDOWS == tuple(2 ** (lv + 1) for lv in range(ng)) and tm >= POOL_STATE
    rows = tm + 2 * (POOL_STATE + 1)
    return pl.pallas_call(
        functools.partial(_pool_kernel, pos0=pos0),
        grid=(b, t // tm),
        in_specs=[pl.BlockSpec((1, tm, d), lambda i, j: (i, j, 0)),
                  pl.BlockSpec((1, d), lambda i, j: (0, 0)),
                  pl.BlockSpec((1, POOL_STATE, d), lambda i, j: (i, 0, 0)),
                  pl.BlockSpec((ng, gw, gw), lambda i, j: (0, 0, 0)),
                  pl.BlockSpec((1, d), lambda i, j: (0, 0))],
        out_specs=[pl.BlockSpec((1, tm, d), lambda i, j: (i, j, 0)),
                   pl.BlockSpec((1, POOL_STATE, d), lambda i, j: (i, 0, 0))],
        out_shape=[jax.ShapeDtypeStruct((b, t, d), F32),
                   jax.ShapeDtypeStruct((b, POOL_STATE, d), F32)],
        scratch_shapes=[pltpu.VMEM((rows, d - lv * gw), F32) for lv in range(ng)],
        compiler_params=_cparams("parallel", "arbitrary"),
        name="pool_mixer",
    )(x, g.reshape(1, d), state, w_group, scale.reshape(1, d))


def _proj_kernel(*refs, n_w, outs):
    x_ref, g_ref = refs[0], refs[1]
    w_refs = refs[2:2 + n_w]
    e_ref = refs[2 + n_w]
    o_refs = refs[3 + n_w:]
    h = _rms(x_ref[0], g_ref[...]).astype(BF16)
    ys = {}
    for (wi, mode, _, ep), o_ref in zip(outs, o_refs):
        if wi not in ys:
            ys[wi] = jnp.dot(h, w_refs[wi][...], preferred_element_type=F32)
        y = ys[wi]
        if ep == "log_sigmoid":
            u = -(y + e_ref[...])
            y = -(jnp.maximum(u, 0.0) + jnp.log1p(jnp.exp(-jnp.abs(u))))
        if mode == "rows":
            o_ref[0] = y.astype(o_ref.dtype)
        elif isinstance(mode, tuple) and mode[0] == "first":
            o_ref[0] = y[:, :mode[1]].astype(o_ref.dtype)
        elif isinstance(mode, tuple):
            o_ref[0] = jnp.transpose(y)[:mode[1], :].astype(o_ref.dtype)
        elif mode == "t":
            o_ref[0] = jnp.transpose(y).astype(o_ref.dtype)
        else:
            o_ref[0] = pltpu.einshape("m(hd)->mhd", y.astype(o_ref.dtype), d=mode)


def _proj(x, g, ws, outs, extra):
    b, t, d = x.shape
    tm = _row_tile(t, ROW_TILE)
    in_specs = [pl.BlockSpec((1, tm, d), lambda i, j: (i, j, 0)), pl.BlockSpec((1, d), lambda i, j: (0, 0))]
    in_specs += [pl.BlockSpec(w.shape, lambda i, j: (0, 0)) for w in ws]
    in_specs += [pl.BlockSpec(extra.shape, lambda i, j: (0, 0))]
    out_specs, out_shape = [], []
    for wi, mode, dt, _ in outs:
        n = ws[wi].shape[1]
        if isinstance(mode, tuple):
            mode, n = ("rows" if mode[0] == "first" else "t"), mode[1]
        if mode == "rows":
            out_specs.append(pl.BlockSpec((1, tm, n), lambda i, j: (i, j, 0)))
            out_shape.append(jax.ShapeDtypeStruct((b, t, n), dt))
        elif mode == "t":
            out_specs.append(pl.BlockSpec((1, n, tm), lambda i, j: (i, 0, j)))
            out_shape.append(jax.ShapeDtypeStruct((b, n, t), dt))
        else:
            out_specs.append(pl.BlockSpec((1, tm, n // mode, mode), lambda i, j: (i, j, 0, 0)))
            out_shape.append(jax.ShapeDtypeStruct((b, t, n // mode, mode), dt))
    return pl.pallas_call(
        functools.partial(_proj_kernel, n_w=len(ws), outs=tuple(outs)),
        grid=(b, t // tm),
        in_specs=in_specs,
        out_specs=out_specs,
        out_shape=out_shape,
        compiler_params=_cparams("parallel", "parallel"),
        name="norm_proj",
    )(x, g.reshape(1, d), *ws, extra)


def _outproj_kernel(x_ref, a_ref, w_ref, o_ref):
    o_ref[...] = x_ref[...] + _dot(a_ref[...], w_ref[...])


def _out_proj(x, a, w):
    n, d = x.shape
    tm = _row_tile(n, ROW_TILE)
    return pl.pallas_call(
        _outproj_kernel,
        grid=(n // tm,),
        in_specs=[pl.BlockSpec((tm, d), lambda i: (i, 0)),
                  pl.BlockSpec((tm, d), lambda i: (i, 0)),
                  pl.BlockSpec((d, d), lambda i: (0, 0))],
        out_specs=pl.BlockSpec((tm, d), lambda i: (i, 0)),
        out_shape=jax.ShapeDtypeStruct((n, d), F32),
        compiler_params=_cparams("parallel"),
        name="out_proj",
    )(x, a, w)


def _bias_table_kernel(rbt_ref, o_ref):
    width = o_ref.shape[-1]
    rel = BIAS_CENTER - lax.broadcasted_iota(I32, (1, width), 1)
    nb = N_BUCKETS // 2
    max_exact = nb // 2
    ret = (rel > 0).astype(I32) * nb
    n = jnp.abs(rel)
    nf = jnp.maximum(n, 1).astype(F32)
    large = max_exact + (jnp.log(nf / max_exact) / math.log(MAX_DIST / max_exact)
                         * (nb - max_exact)).astype(I32)
    large = jnp.minimum(large, nb - 1)
    bucket = ret + jnp.where(n < max_exact, n, large)
    acc = jnp.zeros(o_ref.shape, F32)
    for j in range(N_BUCKETS):
        acc = jnp.where(bucket == j, rbt_ref[:, j:j + 1], acc)
    o_ref[...] = acc * LOG2E


def _bias_tiles_kernel(rbt_ref, far_ref, near_ref, tab_ref):
    n_d, nh, kb, tq = near_ref.shape
    _bias_table_kernel(rbt_ref, tab_ref)
    far_ref[...] = jnp.broadcast_to(tab_ref[:, BIAS_CENTER + MAX_DIST:BIAS_CENTER + MAX_DIST + 1], far_ref.shape)
    for dd in range(n_d):
        s0 = BIAS_CENTER - (dd - 1) * kb - kb
        for h in range(nh):
            rows = jnp.broadcast_to(tab_ref[h:h + 1, s0:s0 + tq + kb], (kb, tq + kb))
            near_ref[dd, h] = pltpu.roll(rows, 0, 1, stride=1, stride_axis=0)[:, kb:]


def _bias_tiles(rel_bias, kb, tq):
    nh = rel_bias.shape[1]
    n_d = tq // kb + 1
    width = BIAS_CENTER + kb + tq + kb
    assert kb % LANE == 0 and tq % kb == 0 and kb >= MAX_DIST and BIAS_CENTER >= tq
    return pl.pallas_call(
        _bias_tiles_kernel,
        out_shape=[jax.ShapeDtypeStruct((nh, LANE), F32), jax.ShapeDtypeStruct((n_d, nh, kb, tq), F32)],
        scratch_shapes=[pltpu.VMEM((nh, width), F32)],
        name="bias_tiles",
    )(rel_bias.T)


def _sortable(x):
    x = jnp.where(x == 0.0, 0.0, x)
    bits = lax.bitcast_convert_type(x, I32)
    return jnp.where(bits < 0, bits ^ 0x7FFFFFFF, bits)


def _neg_inf_key():
    return int(np.float32(NEG_INF).view(np.int32)) ^ 0x7FFFFFFF


def _dsa_prompt_kernel(qt_ref, qit_ref, kwt_ref, kw_ref, k_ref, vt_ref, far_ref, near_ref, x_ref, wout_ref, o_ref,
                       key_ref, hi_ref, lo_ref, sel_ref, m_ref, l_ref, acc_ref, a_ref, *, top_k):
    qi = pl.program_id(1)
    tq = qt_ref.shape[2]
    n_keys = kw_ref.shape[1]
    kb_sz = LANE
    q0 = qi * tq
    nkb = jnp.minimum(n_keys, q0 + tq) // kb_sz
    negkey = _neg_inf_key()

    qlane = lax.broadcasted_iota(I32, (1, tq), 1)
    lim = ((q0 + qlane) // CHUNK + 1) * CHUNK
    krow = lax.broadcasted_iota(I32, (kb_sz, tq), 0)

    def kslice(kb):
        return pl.ds(pl.multiple_of(kb * kb_sz, kb_sz), kb_sz)

    sb = 2 * kb_sz
    srow = lax.broadcasted_iota(I32, (sb, tq), 0)

    def score_body(i, c):
        rows = pl.ds(pl.multiple_of(i * sb, sb), sb)
        kid = kw_ref[0, rows, :][:, :IDX_DIM].astype(BF16)
        sc = jnp.zeros((sb, tq), F32)
        for h in range(IDX_HEADS):
            s = jnp.dot(kid, qit_ref[0, h * IDX_DIM:(h + 1) * IDX_DIM, :], preferred_element_type=F32)
            sc = sc + jnp.maximum(s, 0.0) * kwt_ref[0, IDX_DIM + h:IDX_DIM + h + 1, :]
        sc = sc * ((IDX_DIM * IDX_HEADS) ** -0.5)
        key = jnp.where(i * sb + srow < lim, _sortable(sc), negkey)
        key_ref[rows, :] = key
        hi_ref[rows, :] = (key >> 16).astype(I16)
        lo_ref[rows, :] = ((key & 0xFFFF) - 0x8000).astype(I16)
        return c

    lax.fori_loop(0, nkb // 2, score_body, 0)

    def search16(ref):
        def bit_body(i, t_u):
            cand_u = t_u | jnp.left_shift(jnp.int32(1), 15 - i)
            cand = (cand_u - 0x8000).astype(I16)

            def body(j, a):
                ind = jnp.where(ref[pl.ds(pl.multiple_of(j * sb, sb), sb), :] >= cand,
                                jnp.ones((), I16), jnp.zeros((), I16))
                parts = [ind[16 * r:16 * (r + 1), :] for r in range(sb // 16)]
                while len(parts) > 1:
                    parts = [parts[r] + parts[r + 1] for r in range(0, len(parts), 2)]
                return a + parts[0]
            a = lax.fori_loop(0, nkb // 2, body, jnp.zeros((16, tq), I16))
            cnt = jnp.sum(a.astype(I32), axis=0, keepdims=True)
            return jnp.where(cnt >= top_k, cand_u, t_u)
        return lax.fori_loop(0, 16, bit_body, jnp.zeros((1, tq), I32))

    def count(pred_fn):
        def body(i, a):
            for u in range(2):
                kb = 2 * i + u
                ind = pred_fn(kb, key_ref[kslice(kb), :])
                a = a + jnp.sum(ind.reshape(kb_sz // 8, 8, tq), axis=0)
            return a
        a = lax.fori_loop(0, nkb // 2, body, jnp.zeros((8, tq), I32))
        return jnp.sum(a, axis=0, keepdims=True)

    t_hi = search16(hi_ref)
    t_hi16 = (t_hi - 0x8000).astype(I16)

    def lo_body(j, c):
        rows = pl.ds(pl.multiple_of(j * sb, sb), sb)
        hi = hi_ref[rows, :]
        lo_ref[rows, :] = jnp.where(hi == t_hi16, lo_ref[rows, :],
                                    jnp.where(hi > t_hi16, jnp.full((), 0x7FFF, I16), jnp.full((), -0x8000, I16)))
        return c

    lax.fori_loop(0, nkb // 2, lo_body, 0)
    t_s = (jnp.left_shift(t_hi, 16) | search16(lo_ref)) ^ INT_MIN

    def adm01(kb):
        return jnp.where(kb * kb_sz + krow < lim, 1.0, 0.0)

    def sel_body(kb, a):
        sel = jnp.where(key_ref[kslice(kb), :] >= t_s, adm01(kb), 0.0)
        sel_ref[kslice(kb), :] = sel
        return a + jnp.sum(sel.reshape(kb_sz // 8, 8, tq), axis=0)

    n_sel = jnp.sum(lax.fori_loop(0, nkb, sel_body, jnp.zeros((8, tq), F32)), axis=0, keepdims=True)

    @pl.when(jnp.max(n_sel) > top_k)
    def _():
        n_gt = count(lambda kb, key: jnp.where(key > t_s, 1, 0))
        need = (top_k - n_gt).astype(F32)
        r = lax.broadcasted_iota(I32, (kb_sz, kb_sz), 0)
        c = lax.broadcasted_iota(I32, (kb_sz, kb_sz), 1)
        ltri = jnp.where(c < r, 1.0, 0.0).astype(BF16)

        def tie_body(kb, carry):
            key = key_ref[kslice(kb), :]
            adm = adm01(kb)
            eq = jnp.where(key == t_s, adm, 0.0)
            rank = carry + jnp.dot(ltri, eq.astype(BF16), preferred_element_type=F32)
            keep = jnp.where(rank < need, eq, 0.0)
            sel_ref[kslice(kb), :] = jnp.where(key > t_s, adm, keep)
            return carry + jnp.sum(eq, axis=0, keepdims=True)

        lax.fori_loop(0, nkb, tie_body, jnp.zeros((1, tq), F32))

    m_ref[...] = jnp.full(m_ref.shape, NEG_INF, F32)
    l_ref[...] = jnp.zeros(l_ref.shape, F32)
    acc_ref[...] = jnp.zeros(acc_ref.shape, F32)
    c1 = (B_HD ** -0.5) * LOG2E

    def attend(k0, nk, bias2_fn):
        rows = pl.ds(pl.multiple_of(k0, LANE), nk)
        sel = sel_ref[rows, :] != 0.0
        ks = k_ref[0, rows, :]
        cols = []
        for h in range(B_HEADS):
            g = h // B_REP
            z = jnp.dot(ks[:, g * B_HD:(g + 1) * B_HD], qt_ref[0, h * B_HD:(h + 1) * B_HD, :],
                        preferred_element_type=F32)
            a = jnp.where(sel, z * c1 + bias2_fn(h), NEG_INF)
            a_ref[h, 0:nk, :] = a
            cols.append(jnp.max(a, axis=0, keepdims=True))
        m_old = m_ref[...]
        m_new = jnp.maximum(m_old, jnp.concatenate(cols, axis=0))
        alpha = jnp.exp2(m_old - m_new)
        m_ref[...] = m_new
        sums = []
        for h in range(B_HEADS):
            g = h // B_REP
            p = jnp.exp2(a_ref[h, 0:nk, :] - m_new[h:h + 1, :])
            sums.append(jnp.sum(p, axis=0, keepdims=True))
            hs = slice(h * B_HD, (h + 1) * B_HD)
            pv = jnp.dot(vt_ref[0, g * B_HD:(g + 1) * B_HD, rows], p.astype(BF16),
                         preferred_element_type=F32)
            acc_ref[hs, :] = alpha[h:h + 1, :] * acc_ref[hs, :] + pv
        l_ref[...] = alpha * l_ref[...] + jnp.concatenate(sums, axis=0)

    ab = near_ref.shape[2]
    n_far = jnp.maximum(q0 // ab - 1, 0)
    far_bias2 = far_ref[:, 0:1]

    def far_body(i, c):
        attend(i * ab, ab, lambda h: far_bias2[h:h + 1, :])
        return c

    lax.fori_loop(0, n_far, far_body, 0)

    def near_body(i, c):
        dd = i - q0 // ab + 1
        attend(i * ab, ab, lambda h: near_ref[dd, h])
        return c

    lax.fori_loop(n_far, nkb * kb_sz // ab, near_body, 0)

    _finish_heads(o_ref, x_ref, wout_ref, l_ref, acc_ref, B_HEADS)


def _dsa_attention_prompt(qt, qit, kwt, kw, k, vt, far, near, x, w_out):
    b, d, t = qt.shape
    tq = near.shape[3]
    top_k = min(TOPK_MAX, t // 4)
    ab = near.shape[2]
    assert t % tq == 0 and tq % CHUNK == 0 and tq % ab == 0 and ab % LANE == 0 and tq % (2 * LANE) == 0
    return pl.pallas_call(
        functools.partial(_dsa_prompt_kernel, top_k=top_k),
        grid=(b, t // tq),
        in_specs=[pl.BlockSpec((1, d, tq), lambda i, j: (i, 0, j)),
                  pl.BlockSpec((1, qit.shape[1], tq), lambda i, j: (i, 0, j)),
                  pl.BlockSpec((1, LANE, tq), lambda i, j: (i, 0, j)),
                  pl.BlockSpec((1, t, LANE), lambda i, j: (i, 0, 0)),
                  pl.BlockSpec((1, t, B_KV * B_HD), lambda i, j: (i, 0, 0)),
                  pl.BlockSpec((1, B_KV * B_HD, t), lambda i, j: (i, 0, 0)),
                  pl.BlockSpec(far.shape, lambda i, j: (0, 0)),
                  pl.BlockSpec(near.shape, lambda i, j: (0, 0, 0, 0)),
                  pl.BlockSpec((1, tq, d), lambda i, j: (i, j, 0)),
                  pl.BlockSpec((d, d), lambda i, j: (0, 0))],
        out_specs=pl.BlockSpec((1, tq, d), lambda i, j: (i, j, 0)),
        out_shape=jax.ShapeDtypeStruct((b, t, d), F32),
        scratch_shapes=[pltpu.VMEM((t, tq), I32), pltpu.VMEM((t, tq), I16), pltpu.VMEM((t, tq), I16),
                        pltpu.VMEM((t, tq), F32),
                        pltpu.VMEM((B_HEADS, tq), F32), pltpu.VMEM((B_HEADS, tq), F32),
                        pltpu.VMEM((d, tq), F32), pltpu.VMEM((B_HEADS, ab, tq), F32)],
        compiler_params=_cparams("parallel", "parallel"),
        name="dsa_attention",
    )(qt, qit, kwt, kw, k, vt, far, near, x, w_out)


def _dsa_cached_kernel(q_ref, qi_ref, wi_ref, rb_ref, kidx_ref, kp_ref, vp_ref, kn_ref, vn_ref, o_ref,
                       *, past, t_new, top_k):
    n_keys = past + t_new
    lp = kidx_ref.shape[1]
    negkey = _neg_inf_key()
    kpos = lax.broadcasted_iota(I32, (t_new, lp), 1)
    qpos = past + lax.broadcasted_iota(I32, (t_new, lp), 0)
    adm = kpos < (qpos // CHUNK + 1) * CHUNK

    s = _dot_nt(qi_ref[0], kidx_ref[0])
    w = jnp.maximum(s, 0.0) * wi_ref[0]
    sc = w[0:t_new]
    for h in range(1, IDX_HEADS):
        sc = sc + w[h * t_new:(h + 1) * t_new]
    sc = sc * ((IDX_DIM * IDX_HEADS) ** -0.5)
    key = jnp.where(adm, _sortable(sc), negkey)
    key = jnp.where(kpos < n_keys, key, INT_MIN)

    def bit_body(i, t_u):
        cand_u = t_u | jnp.left_shift(jnp.int32(1), 31 - i)
        cnt = jnp.sum(jnp.where(key >= (cand_u ^ INT_MIN), 1.0, 0.0), axis=1, keepdims=True)
        return jnp.where(cnt >= top_k, cand_u, t_u)

    t_s = lax.fori_loop(0, 32, bit_body, jnp.zeros((t_new, 1), I32)) ^ INT_MIN

    adm01 = jnp.where(adm, 1.0, 0.0)
    gt = jnp.where(key > t_s, adm01, 0.0)
    eq = jnp.where(key == t_s, adm01, 0.0)
    need = top_k - jnp.sum(jnp.where(key > t_s, 1.0, 0.0), axis=1, keepdims=True)
    r = lax.broadcasted_iota(I32, (LANE, LANE), 0)
    c = lax.broadcasted_iota(I32, (LANE, LANE), 1)
    utri = jnp.where(r < c, 1.0, 0.0).astype(BF16)
    carry = jnp.zeros((t_new, 1), F32)
    keeps = []
    for blk in range(lp // LANE):
        e = eq[:, blk * LANE:(blk + 1) * LANE]
        rank = carry + jnp.dot(e.astype(BF16), utri, preferred_element_type=F32)
        keeps.append(jnp.where(rank < need, e, 0.0))
        carry = carry + jnp.sum(e, axis=1, keepdims=True)
    sel = gt + jnp.concatenate(keeps, axis=1)

    near = max(past - MAX_DIST, 0) // LANE * LANE
    rel = (kpos - qpos)[:, near:]
    nb = N_BUCKETS // 2
    max_exact = nb // 2
    n = jnp.abs(rel)
    nf = jnp.maximum(n, 1).astype(F32)
    large = max_exact + (jnp.log(nf / max_exact) / math.log(MAX_DIST / max_exact)
                         * (nb - max_exact)).astype(I32)
    bucket = (rel > 0).astype(I32) * nb + jnp.where(n < max_exact, n, jnp.minimum(large, nb - 1))

    rows = B_REP * t_new
    sel_g = jnp.concatenate([sel] * B_REP, axis=0) != 0.0
    bucket_g = jnp.concatenate([bucket] * B_REP, axis=0)
    for g in range(B_KV):
        grp = lambda ref, n: ref[0, pl.ds(g, n, stride=B_KV), :]
        qg = q_ref[0, g * rows:(g + 1) * rows, :]
        rb = rb_ref[g * rows:(g + 1) * rows, :]
        bias_near = jnp.zeros((rows, lp - near), F32)
        for j in range(N_BUCKETS):
            bias_near = jnp.where(bucket_g == j, rb[:, j:j + 1], bias_near)
        bias = jnp.concatenate([jnp.broadcast_to(rb[:, nb - 1:nb], (rows, near)), bias_near], axis=1)
        zp = _dot_nt(qg, grp(kp_ref, past)) * (B_HD ** -0.5)
        zn = _dot_nt(qg, grp(kn_ref, t_new)) * (B_HD ** -0.5)
        ap = jnp.where(sel_g[:, :past], zp + bias[:, :past], NEG_INF)
        an = jnp.where(sel_g[:, past:n_keys], zn + bias[:, past:n_keys], NEG_INF)
        m = jnp.maximum(jnp.max(ap, axis=1, keepdims=True), jnp.max(an, axis=1, keepdims=True))
        pp, pn = jnp.exp(ap - m), jnp.exp(an - m)
        l = jnp.sum(pp, axis=1, keepdims=True) + jnp.sum(pn, axis=1, keepdims=True)
        o_ref[0, g * rows:(g + 1) * rows, :] = (_dot(pp, grp(vp_ref, past)) + _dot(pn, grp(vn_ref, t_new))) / l


def _dsa_attention_cached(q_rows, qi_rows, wi_col, rb_rows, kidx_all, k_past, v_past, k_new, v_new):
    b, rows, hd = q_rows.shape
    past, t_new = k_past.shape[1], k_new.shape[1]
    lp = kidx_all.shape[1]
    top_k = min(TOPK_MAX, (past + t_new) // 4)
    assert past % LANE == 0
    flat = lambda a: a.reshape(b, a.shape[1] * B_KV, B_HD)
    kv_spec = lambda n: pl.BlockSpec((1, n * B_KV, B_HD), lambda i: (i, 0, 0))
    return pl.pallas_call(
        functools.partial(_dsa_cached_kernel, past=past, t_new=t_new, top_k=top_k),
        grid=(b,),
        in_specs=[pl.BlockSpec((1, rows, hd), lambda i: (i, 0, 0)),
                  pl.BlockSpec((1,) + qi_rows.shape[1:], lambda i: (i, 0, 0)),
                  pl.BlockSpec((1,) + wi_col.shape[1:], lambda i: (i, 0, 0)),
                  pl.BlockSpec(rb_rows.shape, lambda i: (0, 0)),
                  pl.BlockSpec((1, lp, IDX_DIM), lambda i: (i, 0, 0)),
                  kv_spec(past), kv_spec(past), kv_spec(t_new), kv_spec(t_new)],
        out_specs=pl.BlockSpec((1, rows, hd), lambda i: (i, 0, 0)),
        out_shape=jax.ShapeDtypeStruct((b, rows, hd), F32),
        compiler_params=_cparams("parallel"),
        name="dsa_attention_cached",
    )(q_rows, qi_rows, wi_col, rb_rows, kidx_all, flat(k_past), flat(v_past), flat(k_new), flat(v_new))


def _cumsum_kernel(x_ref, o_ref):
    x = x_ref[0]
    n = x.shape[-1]
    lane = lax.broadcasted_iota(I32, x.shape, 1)
    s = 1
    while s < n:
        x = x + jnp.where(lane >= s, pltpu.roll(x, s, 1), 0.0)
        s *= 2
    o_ref[0] = x


def _cumsum_lanes(x):
    b, h, n = x.shape
    return pl.pallas_call(
        _cumsum_kernel,
        grid=(b,),
        in_specs=[pl.BlockSpec((1, h, n), lambda i: (i, 0, 0))],
        out_specs=pl.BlockSpec((1, h, n), lambda i: (i, 0, 0)),
        out_shape=jax.ShapeDtypeStruct((b, h, n), F32),
        compiler_params=_cparams("parallel"),
        name="logf_cumsum",
    )(x)


def _fox_init(m_ref, l_ref, acc_ref):
    m_ref[...] = jnp.full(m_ref.shape, NEG_INF, F32)
    l_ref[...] = jnp.zeros(l_ref.shape, F32)
    acc_ref[...] = jnp.zeros(acc_ref.shape, F32)


def _fox_tile(z_fn, pv_fn, cq, ck, mask, m_ref, l_ref, acc_ref, a_ref):
    c1 = (D_HD ** -0.5) * LOG2E
    cq2, ck2 = cq * LOG2E, ck * LOG2E
    cols = []
    for h in range(D_HEADS):
        a = z_fn(h) * c1 - ck2[:, h:h + 1]
        if mask is not None:
            a = jnp.where(mask, a, NEG_INF)
        a_ref[h] = a
        cols.append(jnp.max(a, axis=0, keepdims=True))
    m_old = m_ref[...]
    m_new = jnp.maximum(m_old, jnp.concatenate(cols, axis=0) + cq2)
    alpha = jnp.exp2(m_old - m_new)
    shift = m_new - cq2
    m_ref[...] = m_new
    sums = []
    for h in range(D_HEADS):
        p = jnp.exp2(a_ref[h] - shift[h:h + 1, :])
        sums.append(jnp.sum(p, axis=0, keepdims=True))
        hs = slice(h * D_HD, (h + 1) * D_HD)
        acc_ref[hs, :] = alpha[h:h + 1, :] * acc_ref[hs, :] + pv_fn(h, p.astype(BF16))
    l_ref[...] = alpha * l_ref[...] + jnp.concatenate(sums, axis=0)


def _hs(h):
    return slice(h * D_HD, (h + 1) * D_HD)


def _fox_prompt_kernel(qt_ref, k_ref, vt_ref, cq_ref, ck_ref, x_ref, wout_ref, o_ref,
                       m_ref, l_ref, acc_ref, a_ref):
    qi, step = pl.program_id(1), pl.program_id(2)
    tq, tk = qt_ref.shape[2], a_ref.shape[1]
    q0 = qi * tq
    ki = step - (pl.num_programs(2) - 1 - (q0 + tq - 1) // tk)
    k0 = ki * tk
    keys = pl.ds(pl.multiple_of(jnp.maximum(k0, 0), tk), tk)

    @pl.when(step == 0)
    def _():
        _fox_init(m_ref, l_ref, acc_ref)

    def run(masked):
        mask = None
        if masked:
            mask = (k0 + lax.broadcasted_iota(I32, (tk, tq), 0)) <= (q0 + lax.broadcasted_iota(I32, (tk, tq), 1))
        _fox_tile(lambda h: jnp.dot(k_ref[0, keys, _hs(h)], qt_ref[0, _hs(h), :], preferred_element_type=F32),
                  lambda h, p: jnp.dot(vt_ref[0, _hs(h), keys], p, preferred_element_type=F32),
                  cq_ref[0], ck_ref[0, keys, :], mask, m_ref, l_ref, acc_ref, a_ref)

    fully_visible = k0 + tk - 1 <= q0
    pl.when(jnp.logical_and(ki >= 0, fully_visible))(lambda: run(False))
    pl.when(jnp.logical_and(ki >= 0, jnp.logical_not(fully_visible)))(lambda: run(True))

    @pl.when(step == pl.num_programs(2) - 1)
    def _():
        _finish_heads(o_ref, x_ref, wout_ref, l_ref, acc_ref, D_HEADS)


def _fox_attention_prompt(qt, k, vt, cum_t, cum, x, w_out, tq, tk):
    b, d, t = qt.shape
    nq, nk = t // tq, t // tk
    return pl.pallas_call(
        _fox_prompt_kernel,
        grid=(b, nq, nk),
        in_specs=[pl.BlockSpec((1, d, tq), lambda i, j, kk: (i, 0, j)),
                  pl.BlockSpec((1, t, d), lambda i, j, kk: (i, 0, 0)),
                  pl.BlockSpec((1, d, t), lambda i, j, kk: (i, 0, 0)),
                  pl.BlockSpec((1, D_HEADS, tq), lambda i, j, kk: (i, 0, j)),
                  pl.BlockSpec((1, t, D_HEADS), lambda i, j, kk: (i, 0, 0)),
                  pl.BlockSpec((1, tq, d), lambda i, j, kk: (i, j, 0)),
                  pl.BlockSpec((d, d), lambda i, j, kk: (0, 0))],
        out_specs=pl.BlockSpec((1, tq, d), lambda i, j, kk: (i, j, 0)),
        out_shape=jax.ShapeDtypeStruct((b, t, d), F32),
        scratch_shapes=[pltpu.VMEM((D_HEADS, tq), F32), pltpu.VMEM((D_HEADS, tq), F32),
                        pltpu.VMEM((d, tq), F32), pltpu.VMEM((D_HEADS, tk, tq), F32)],
        compiler_params=_cparams("parallel", "parallel", "arbitrary"),
        name="fox_attention",
    )(qt, k, vt, cum_t, cum, x, w_out)


def _fox_cached_kernel(q_ref, kp_ref, vp_ref, kn_ref, vn_ref, cq_ref, ckp_ref, ckn_ref, o_ref,
                       m_ref, l_ref, acc_ref, *, t_new):
    ki = pl.program_id(1)
    n_past = pl.num_programs(1) - 1
    rows = q_ref.shape[1]
    c1 = (D_HD ** -0.5) * LOG2E

    @pl.when(ki == 0)
    def _():
        m_ref[...] = jnp.full(m_ref.shape, NEG_INF, F32)
        l_ref[...] = jnp.zeros(l_ref.shape, F32)
        acc_ref[...] = jnp.zeros(acc_ref.shape, F32)

    def tile(k2d, v2d, ck_row, causal):
        cols = k2d.shape[0]
        a = _dot_nt(q_ref[0], k2d) * c1 - ck_row * LOG2E
        r = lax.broadcasted_iota(I32, (rows, cols), 0)
        c = lax.broadcasted_iota(I32, (rows, cols), 1)
        ok = (c % D_HEADS) == (r // t_new)
        if causal:
            ok = jnp.logical_and(ok, (c // D_HEADS) <= (r % t_new))
        a = jnp.where(ok, a, NEG_INF)
        cq2 = cq_ref[0] * LOG2E
        m_old = m_ref[...]
        m_new = jnp.maximum(m_old, jnp.max(a, axis=1, keepdims=True) + cq2)
        alpha = jnp.exp2(m_old - m_new)
        p = jnp.exp2(a - (m_new - cq2))
        l_ref[...] = alpha * l_ref[...] + jnp.sum(p, axis=1, keepdims=True)
        acc_ref[...] = alpha * acc_ref[...] + _dot(p, v2d)
        m_ref[...] = m_new

    @pl.when(ki < n_past)
    def _():
        tk = kp_ref.shape[1]
        tile(kp_ref[0].reshape(tk * D_HEADS, D_HD), vp_ref[0].reshape(tk * D_HEADS, D_HD), ckp_ref[0], False)

    @pl.when(ki == n_past)
    def _():
        tile(kn_ref[0].reshape(t_new * D_HEADS, D_HD), vn_ref[0].reshape(t_new * D_HEADS, D_HD), ckn_ref[0], True)
        o_ref[0] = acc_ref[...] / l_ref[...]


def _fox_attention_cached(q_rows, k_past, v_past, k_new, v_new, cq_col, ck_past, ck_new, tk):
    b, rows, hd = q_rows.shape
    past, t_new = k_past.shape[1], k_new.shape[1]
    n_past = past // tk
    pidx = lambda i, kk: (i, jnp.minimum(kk, n_past - 1), 0, 0)
    return pl.pallas_call(
        functools.partial(_fox_cached_kernel, t_new=t_new),
        grid=(b, n_past + 1),
        in_specs=[pl.BlockSpec((1, rows, hd), lambda i, kk: (i, 0, 0)),
                  pl.BlockSpec((1, tk, D_HEADS, D_HD), pidx),
                  pl.BlockSpec((1, tk, D_HEADS, D_HD), pidx),
                  pl.BlockSpec((1, t_new, D_HEADS, D_HD), lambda i, kk: (i, 0, 0, 0)),
                  pl.BlockSpec((1, t_new, D_HEADS, D_HD), lambda i, kk: (i, 0, 0, 0)),
                  pl.BlockSpec((1, rows, 1), lambda i, kk: (i, 0, 0)),
                  pl.BlockSpec((1, 1, tk * D_HEADS), lambda i, kk: (i, 0, jnp.minimum(kk, n_past - 1))),
                  pl.BlockSpec((1, 1, t_new * D_HEADS), lambda i, kk: (i, 0, 0))],
        out_specs=pl.BlockSpec((1, rows, hd), lambda i, kk: (i, 0, 0)),
        out_shape=jax.ShapeDtypeStruct((b, rows, hd), F32),
        scratch_shapes=[pltpu.VMEM((rows, 1), F32), pltpu.VMEM((rows, 1), F32), pltpu.VMEM((rows, hd), F32)],
        compiler_params=_cparams("parallel", "arbitrary"),
        name="fox_attention_cached",
    )(q_rows, k_past, v_past, k_new, v_new, cq_col, ck_past, ck_new)


def _pad_rows(a, rows):
    if a.shape[1] == rows:
        return a
    return jnp.pad(a, ((0, 0), (0, rows - a.shape[1])) + ((0, 0),) * (a.ndim - 2))


def _round_up(n, m):
    return -(-n // m) * m


def _dsa_mixer(x, g, k_past, v_past, ki_past, w, w_out, rel_bias):
    b, t, d = x.shape
    past = k_past.shape[1]
    n_keys = past + t
    if past == 0:
        qt, k4, kb, v4, vt, qit, kw, kwt, ki_t = _proj(
            x, g, w, ((0, "t", BF16, None), (1, B_HD, F32, None), (1, "rows", BF16, None),
                      (2, B_HD, F32, None), (2, "t", BF16, None), (3, "t", BF16, None),
                      (4, "rows", F32, None), (4, "t", F32, None), (4, ("first_t", IDX_DIM), F32, None)),
            jnp.zeros((1, LANE), F32))
        far, near = _bias_tiles(rel_bias, DSA_TILE, DSA_TILE)
        y = _dsa_attention_prompt(qt, qit, kwt, kw, kb, vt, far, near, x, w_out)
        return (y, k4, v4, jnp.swapaxes(ki_t, 1, 2))
    q, k4, v4, qidx, kw = _proj(
        x, g, w, ((0, "rows", BF16, None), (1, B_HD, F32, None), (2, B_HD, F32, None),
                  (3, "rows", BF16, None), (4, "rows", F32, None)),
        jnp.zeros((1, LANE), F32))
    ki = kw[:, :, :IDX_DIM]
    to_rows = lambda a, nh: jnp.swapaxes(a.reshape(b, t, nh, -1), 1, 2).reshape(b, nh * t, -1)
    kidx_all = _pad_rows(jnp.concatenate([ki_past, ki], axis=1), _round_up(n_keys, LANE))
    o = _dsa_attention_cached(to_rows(q, B_HEADS), to_rows(qidx, IDX_HEADS),
                              to_rows(kw[:, :, IDX_DIM:IDX_DIM + IDX_HEADS], IDX_HEADS),
                              jnp.repeat(rel_bias.T, t, axis=0), kidx_all, k_past, v_past, k4, v4)
    o = jnp.swapaxes(o.reshape(b, B_HEADS, t, B_HD), 1, 2).reshape(b * t, d)
    y = _out_proj(x.reshape(b * t, d), o, w_out).reshape(b, t, d)
    return (y, k4, v4, ki)


def _fox_mixer(x, g, k_past, v_past, lf_past, w, b_f, w_out):
    b, t, d = x.shape
    past = k_past.shape[1]
    heads4 = ((1, D_HD, F32, None), (2, D_HD, F32, None))
    if past == 0:
        tq, tk = min(FOX_TILE, t), min(FOX_TILE // 2, t)
        k4, v4, logf_t, qt, kb, vt = _proj(
            x, g, w, heads4 + ((3, ("first_t", D_HEADS), F32, "log_sigmoid"), (0, "t", BF16, None),
                               (1, "rows", BF16, None), (2, "t", BF16, None)), b_f)
        cum_t = _cumsum_lanes(logf_t)
        y = _fox_attention_prompt(qt, kb, vt, cum_t, jnp.swapaxes(cum_t, 1, 2), x, w_out, tq, tk)
        return (y, k4, v4, jnp.swapaxes(logf_t, 1, 2))
    else:
        tk = math.gcd(past, FOX_TILE)
        k4, v4, logf, q = _proj(
            x, g, w, heads4 + ((3, ("first", D_HEADS), F32, "log_sigmoid"), (0, "rows", BF16, None)), b_f)
        lf_all = _pad_rows(jnp.concatenate([lf_past, logf], axis=1), _round_up(past + t, LANE))
        cum = jnp.swapaxes(_cumsum_lanes(jnp.swapaxes(lf_all, 1, 2)), 1, 2)[:, :past + t]
        ck = cum.reshape(b, 1, (past + t) * D_HEADS)
        to_rows = lambda a: jnp.swapaxes(a.reshape(b, t, D_HEADS, -1), 1, 2).reshape(b, D_HEADS * t, -1)
        o = _fox_attention_cached(to_rows(q), k_past, v_past, k4, v4, to_rows(cum[:, past:]),
                                  ck[:, :, :past * D_HEADS], ck[:, :, past * D_HEADS:], tk)
        o = jnp.swapaxes(o.reshape(b, D_HEADS, t, D_HD), 1, 2).reshape(b, t, d)
    y = _out_proj(x.reshape(b * t, d), o.reshape(b * t, d), w_out).reshape(b, t, d)
    return (y, k4, v4, logf)


def _run_group(x, pos0, a_st, b_k, b_v, b_ki, c_st, d_k, d_v, d_lf, mem_k, mem_v, prm):
    b, t, d = x.shape
    depth = prm["norm_mix"].shape[0]
    new = {n: [] for n in ("a", "bk", "bv", "bki", "c", "dk", "dv", "dlf")}
    for i in range(depth):
        kind, j = i % 4, i // 4
        g = prm["norm_mix"][i]
        if kind == 0:
            x, st = _conv_mixer(x, g, prm["a_w_in"][j], prm["a_conv"][j], a_st[j], prm["a_w_out"][j])
            new["a"].append(st)
        elif kind == 1:
            x, kk, vv, ki = _dsa_mixer(x, g, b_k[j], b_v[j], b_ki[j], prm["b_w"][j], prm["b_w_out"][j],
                                       prm["rel_bias"])
            new["bk"].append(kk); new["bv"].append(vv); new["bki"].append(ki)
        elif kind == 2:
            x, st = _pool_mixer(x, g, c_st[j], prm["c_w_group"][j], prm["c_scale"][j], pos0)
            new["c"].append(st)
        else:
            x, kk, vv, lf = _fox_mixer(x, g, d_k[j], d_v[j], d_lf[j], prm["d_w"][j], prm["d_b_f"][j],
                                       prm["d_w_out"][j])
            new["dk"].append(kk); new["dv"].append(vv); new["dlf"].append(lf)
        x = _xattn(x, prm["norm_xattn"], prm["xa_wq"], mem_k, mem_v, prm["xa_wo"], i)
        last = i == depth - 1
        x = _ffn(x.reshape(b * t, d), prm["norm_ffn"], prm["ffn_w1"], prm["ffn_w2"],
                 prm["final_norm"], i, last).reshape(b, t, d)
    return (x,) + tuple(jnp.stack(new[n]) for n in ("a", "bk", "bv", "bki", "c", "dk", "dv", "dlf"))


def kernel(x_prompt, x_sample, state_a_conv, cache_b_k, cache_b_v, cache_b_kidx, state_c_pool,
           cache_d_k, cache_d_v, cache_d_logf, cache_mem_k, cache_mem_v, mem_prompt,
           norm_mix, norm_xattn, norm_mem, norm_ffn, final_norm,
           a_w_in, a_conv, a_w_out, b_w_in, b_w_out, rel_bias, c_w_group, c_scale,
           d_w_in, d_b_f, d_w_out, xa_wq, xa_wkv, xa_wo, ffn_w1, ffn_w2):
    bp, _, d = x_prompt.shape
    n_b, n_d = b_w_in.shape[0], d_w_in.shape[0]
    bf = lambda w: w.astype(BF16)

    def split_cols(w, widths):
        out, c = [], 0
        for wd in widths:
            piece = w[:, c:c + wd]
            c += wd
            if wd % LANE:
                piece = jnp.pad(piece, ((0, 0), (0, _round_up(wd, LANE) - wd)))
            out.append(bf(piece))
        assert c == w.shape[1]
        return out

    b_q, b_kvw = B_HEADS * B_HD, B_KV * B_HD
    b_w = [split_cols(b_w_in[j], (b_q, b_kvw, b_kvw, IDX_HEADS * IDX_DIM, IDX_DIM + IDX_HEADS))
           for j in range(n_b)]
    d_w = [split_cols(d_w_in[j], (d, d, d, D_HEADS)) for j in range(n_d)]
    d_bf = [jnp.pad(d_b_f[j], (0, LANE - D_HEADS)).reshape(1, LANE) for j in range(n_d)]

    prm = {"norm_mix": norm_mix, "norm_xattn": norm_xattn, "norm_ffn": norm_ffn, "final_norm": final_norm,
           "a_w_in": bf(a_w_in), "a_conv": a_conv, "a_w_out": bf(a_w_out),
           "b_w": b_w, "b_w_out": bf(b_w_out), "rel_bias": rel_bias,
           "c_w_group": bf(c_w_group), "c_scale": c_scale,
           "d_w": d_w, "d_b_f": d_bf, "d_w_out": bf(d_w_out),
           "xa_wq": xa_wq, "xa_wo": xa_wo, "ffn_w1": ffn_w1, "ffn_w2": ffn_w2}

    mk, mv, mk_rows, mv_rows = _memory_kv(mem_prompt, norm_mem, xa_wkv)

    n_a, n_c = a_w_in.shape[0], c_w_group.shape[0]
    z = lambda *s: jnp.zeros(s, F32)
    gp = _run_group(x_prompt, 0,
                    z(n_a, bp, CONV_W - 1, d),
                    z(n_b, bp, 0, B_KV, B_HD), z(n_b, bp, 0, B_KV, B_HD), z(n_b, bp, 0, IDX_DIM),
                    z(n_c, bp, POOL_STATE, d),
                    z(n_d, bp, 0, D_HEADS, D_HD), z(n_d, bp, 0, D_HEADS, D_HD), z(n_d, bp, 0, D_HEADS),
                    mk_rows, mv_rows, prm)

    past_len = cache_b_k.shape[2]
    gs = _run_group(x_sample, past_len, state_a_conv, cache_b_k, cache_b_v, cache_b_kidx, state_c_pool,
                    cache_d_k, cache_d_v, cache_d_logf, cache_mem_k, cache_mem_v, prm)

    (y_p, a_p, bk_p, bv_p, bki_p, c_p, dk_p, dv_p, dlf_p) = gp
    (y_s, a_s, bk_s, bv_s, bki_s, c_s, dk_s, dv_s, dlf_s) = gs
    return (y_p, y_s, a_p, a_s, bk_p, bv_p, bki_p, bk_s, bv_s, bki_s, c_p, c_s,
            dk_p, dv_p, dlf_p, dk_s, dv_s, dlf_s, mk, mv)
```

```python
import functools
import math

import jax
import jax.numpy as jnp
import numpy as np
from jax import lax
from jax.experimental import pallas as pl
from jax.experimental.pallas import tpu as pltpu

F32 = jnp.float32
BF16 = jnp.bfloat16
I32 = jnp.int32
I16 = jnp.int16

EPS = 1e-6
NEG_INF = -1e30
LOG2E = math.log2(math.e)
CHUNK = 64
LANE = 128
VMEM_LIMIT = 48 * 1024 * 1024

ROW_TILE = 512
WIDE_ROW_TILE = 1024
FF_TILE = 1024
FOX_TILE = 4 * LANE
DSA_TILE = 2 * LANE

CONV_W = 3
POOL_WINDOWS = (2, 4, 8, 16)
POOL_STATE = max(POOL_WINDOWS) - 1
B_HEADS, B_KV, B_HD = 8, 2, 128
B_REP = B_HEADS // B_KV
IDX_HEADS, IDX_DIM = 8, 64
TOPK_MAX = 256
N_BUCKETS, MAX_DIST = 32, 128
D_HEADS, D_HD = 8, 128
MEM_HEADS = 4
B_WIDTHS = (B_HEADS * B_HD, B_KV * B_HD, B_KV * B_HD, IDX_HEADS * IDX_DIM, IDX_DIM + IDX_HEADS)
D_WIDTHS = (D_HEADS * D_HD,) * 3 + (D_HEADS,)
INT_MIN = -2147483648
BIAS_CENTER = 2 * LANE


def _cparams(*sem):
    return pltpu.CompilerParams(dimension_semantics=sem, vmem_limit_bytes=VMEM_LIMIT)


def _dot(a, b):
    return jnp.dot(a.astype(BF16), b.astype(BF16), preferred_element_type=F32)


def _dot_nt(a, b):
    return lax.dot_general(a.astype(BF16), b.astype(BF16), (((1,), (1,)), ((), ())),
                           preferred_element_type=F32)


def _dot_tn(a, b):
    return lax.dot_general(a.astype(BF16), b.astype(BF16), (((0,), (0,)), ((), ())),
                           preferred_element_type=F32)


def _rms(x, g):
    return x * lax.rsqrt(jnp.mean(x * x, axis=-1, keepdims=True) + EPS) * g


def _finish_heads(o_ref, x_ref, w_ref, l_ref, acc_ref, n_heads):
    hd = acc_ref.shape[0] // n_heads
    inv_l = 1.0 / l_ref[...]
    heads_t = jnp.concatenate([acc_ref[h * hd:(h + 1) * hd, :] * inv_l[h:h + 1, :] for h in range(n_heads)],
                              axis=0)
    o_ref[0] = x_ref[0] + _dot_tn(heads_t, w_ref[...])


def _row_tile(n, cap):
    t = min(n, cap)
    assert n % t == 0
    return t


def _memkv_kernel(mem_ref, g_ref, w_ref, k_ref, v_ref, kb_ref, vb_ref):
    bb, nm, d = mem_ref.shape
    m = mem_ref[...].reshape(bb * nm, d)
    mn = m * lax.rsqrt(jnp.mean(m * m, axis=-1, keepdims=True) + EPS)
    h = (mn * g_ref[0]).astype(BF16)
    hd = d // MEM_HEADS
    k = jnp.dot(h, w_ref[0, :, :d].astype(BF16), preferred_element_type=F32)
    v = jnp.dot(h, w_ref[0, :, d:].astype(BF16), preferred_element_type=F32)
    for i in range(bb):
        rows = slice(i * nm, (i + 1) * nm)
        kb_ref[0, i] = k[rows].astype(BF16)
        vb_ref[0, i] = v[rows].astype(BF16)
        k_ref[0, i] = pltpu.einshape("m(hd)->mhd", k[rows], d=hd)
        v_ref[0, i] = pltpu.einshape("m(hd)->mhd", v[rows], d=hd)


def _memory_kv(mem, g_mem, w_kv):
    depth, d = g_mem.shape
    b, nm, _ = mem.shape
    hd = d // MEM_HEADS
    out = jax.ShapeDtypeStruct((depth, b, nm, MEM_HEADS, hd), F32)
    out_b = jax.ShapeDtypeStruct((depth, b, nm, d), BF16)
    bb = math.gcd(b, 2)
    heads_spec = pl.BlockSpec((1, bb, nm, MEM_HEADS, hd), lambda l, i: (l, i, 0, 0, 0))
    rows_spec = pl.BlockSpec((1, bb, nm, d), lambda l, i: (l, i, 0, 0))
    return pl.pallas_call(
        _memkv_kernel,
        grid=(depth, b // bb),
        in_specs=[pl.BlockSpec((bb, nm, d), lambda l, i: (i, 0, 0)),
                  pl.BlockSpec((1, 1, d), lambda l, i: (l, 0, 0)),
                  pl.BlockSpec((1, d, 2 * d), lambda l, i: (l, 0, 0))],
        out_specs=[heads_spec, heads_spec, rows_spec, rows_spec],
        out_shape=[out, out, out_b, out_b],
        compiler_params=_cparams("parallel", "parallel"),
        name="memory_kv",
    )(mem, g_mem.reshape(depth, 1, d), w_kv)


def _ffn_kernel(x_ref, g_ref, w1_ref, w2_ref, gf_ref, o_ref, h_ref, acc_ref, *, final_norm):
    j = pl.program_id(1)

    @pl.when(j == 0)
    def _():
        h_ref[...] = _rms(x_ref[...], g_ref[...]).astype(BF16)
        acc_ref[...] = jnp.zeros_like(acc_ref)

    u = jnp.maximum(jnp.dot(h_ref[...], w1_ref[...].astype(BF16), preferred_element_type=F32), 0.0)
    acc_ref[...] += jnp.dot((u * u).astype(BF16), w2_ref[...].astype(BF16), preferred_element_type=F32)

    @pl.when(j == pl.num_programs(1) - 1)
    def _():
        y = x_ref[...] + acc_ref[...]
        o_ref[...] = _rms(y, gf_ref[...]) if final_norm else y


def _ffn(x, g, w1, w2, gf, layer, final_norm):
    n, d = x.shape
    f = w1.shape[2]
    tm = _row_tile(n, WIDE_ROW_TILE)
    tf = FF_TILE
    return pl.pallas_call(
        functools.partial(_ffn_kernel, final_norm=final_norm),
        grid=(n // tm, f // tf),
        in_specs=[pl.BlockSpec((tm, d), lambda i, j: (i, 0)),
                  pl.BlockSpec((None, 1, d), lambda i, j: (layer, 0, 0)),
                  pl.BlockSpec((None, d, tf), lambda i, j: (layer, 0, j)),
                  pl.BlockSpec((None, tf, d), lambda i, j: (layer, j, 0)),
                  pl.BlockSpec((1, d), lambda i, j: (0, 0))],
        out_specs=pl.BlockSpec((tm, d), lambda i, j: (i, 0)),
        out_shape=jax.ShapeDtypeStruct((n, d), F32),
        scratch_shapes=[pltpu.VMEM((tm, d), BF16), pltpu.VMEM((tm, d), F32)],
        compiler_params=_cparams("parallel", "arbitrary"),
        name="ffn",
    )(x, g.reshape(-1, 1, d), w1, w2, gf.reshape(1, d))


def _xattn_kernel(x_ref, g_ref, wq_ref, mk_ref, mv_ref, wo_ref, o_ref):
    x = x_ref[0]
    d = x.shape[-1]
    hd = d // MEM_HEADS
    h = _rms(x, g_ref[...]).astype(BF16)
    q = jnp.dot(h, wq_ref[...].astype(BF16), preferred_element_type=F32)
    outs = []
    if len(mk_ref.shape) == 3:
        mk = pltpu.einshape("mhd->m(hd)", mk_ref[...]).astype(BF16)
        mv = pltpu.einshape("mhd->m(hd)", mv_ref[...]).astype(BF16)
    else:
        mk, mv = mk_ref[...], mv_ref[...]
    for hh in range(MEM_HEADS):
        sl = slice(hh * hd, (hh + 1) * hd)
        kh, vh = mk[:, sl], mv[:, sl]
        s = _dot_nt(q[:, sl], kh) * (hd ** -0.5)
        m = jnp.max(s, axis=-1, keepdims=True)
        p = jnp.exp(s - m)
        l = jnp.sum(p, axis=-1, keepdims=True)
        outs.append(_dot(p, vh) / l)
    o = jnp.concatenate(outs, axis=-1)
    o_ref[0] = x + _dot(o, wo_ref[...])


def _xattn(x, g, wq, mk, mv, wo, layer):
    b, t, d = x.shape
    tm = _row_tile(t, WIDE_ROW_TILE)
    kv_spec = pl.BlockSpec((None, None) + mk.shape[2:], lambda i, j: (layer, i) + (0,) * (mk.ndim - 2))
    return pl.pallas_call(
        _xattn_kernel,
        grid=(b, t // tm),
        in_specs=[pl.BlockSpec((1, tm, d), lambda i, j: (i, j, 0)),
                  pl.BlockSpec((None, 1, d), lambda i, j: (layer, 0, 0)),
                  pl.BlockSpec((None, d, d), lambda i, j: (layer, 0, 0)),
                  kv_spec, kv_spec,
                  pl.BlockSpec((None, d, d), lambda i, j: (layer, 0, 0))],
        out_specs=pl.BlockSpec((1, tm, d), lambda i, j: (i, j, 0)),
        out_shape=jax.ShapeDtypeStruct((b, t, d), F32),
        compiler_params=_cparams("parallel", "parallel"),
        name="xattn",
    )(x, g.reshape(-1, 1, d), wq, mk, mv, wo)


def _conv_kernel(x_ref, g_ref, win_ref, wc_ref, st_ref, wout_ref, o_ref, nst_ref, z_ref):
    t = pl.program_id(1)
    x = x_ref[0]
    tm, d = x.shape
    pad = 8

    @pl.when(t == 0)
    def _():
        z_ref[pad - 2:pad, :] = st_ref[0]

    h = _rms(x, g_ref[...]).astype(BF16)
    bg = jnp.dot(h, win_ref[:, 0:d], preferred_element_type=F32)
    cg = jnp.dot(h, win_ref[:, d:2 * d], preferred_element_type=F32)
    u = jnp.dot(h, win_ref[:, 2 * d:3 * d], preferred_element_type=F32)
    z = cg * u
    z_ref[pad:pad + tm, :] = z
    conv = (z_ref[pad - 2:pad - 2 + tm, :] * wc_ref[0:1, :]
            + z_ref[pad - 1:pad - 1 + tm, :] * wc_ref[1:2, :]
            + z * wc_ref[2:3, :])
    o_ref[0] = x + _dot(bg * conv, wout_ref[...])
    last = z_ref[pad + tm - 2:pad + tm, :]
    z_ref[pad - 2:pad, :] = last

    @pl.when(t == pl.num_programs(1) - 1)
    def _():
        nst_ref[0] = last


def _conv_mixer(x, g, w_in, w_conv, state, w_out):
    b, t, d = x.shape
    tm = _row_tile(t, WIDE_ROW_TILE)
    once = pl.Buffered(1)
    return pl.pallas_call(
        _conv_kernel,
        grid=(b, t // tm),
        in_specs=[pl.BlockSpec((1, tm, d), lambda i, j: (i, j, 0)),
                  pl.BlockSpec((1, d), lambda i, j: (0, 0)),
                  pl.BlockSpec((d, 3 * d), lambda i, j: (0, 0), pipeline_mode=once),
                  pl.BlockSpec((CONV_W, d), lambda i, j: (0, 0)),
                  pl.BlockSpec((1, CONV_W - 1, d), lambda i, j: (i, 0, 0)),
                  pl.BlockSpec((d, d), lambda i, j: (0, 0), pipeline_mode=once)],
        out_specs=[pl.BlockSpec((1, tm, d), lambda i, j: (i, j, 0)),
                   pl.BlockSpec((1, CONV_W - 1, d), lambda i, j: (i, 0, 0))],
        out_shape=[jax.ShapeDtypeStruct((b, t, d), F32),
                   jax.ShapeDtypeStruct((b, CONV_W - 1, d), F32)],
        scratch_shapes=[pltpu.VMEM((tm + 8, d), F32)],
        compiler_params=_cparams("parallel", "arbitrary"),
        name="conv_mixer",
    )(x, g.reshape(1, d), w_in, w_conv, state, w_out)


def _pool_kernel(x_ref, g_ref, st_ref, wg_ref, sc_ref, o_ref, nst_ref, h_ref, *s_refs, pos0):
    t = pl.program_id(1)
    x = x_ref[0]
    tm, d = x.shape
    n_lv = len(POOL_WINDOWS)
    gw = d // n_lv
    base = 2 * (POOL_STATE + 1)
    lead = base - POOL_STATE
    end = base + tm

    @pl.when(t == 0)
    def _():
        h_ref[0:lead, :] = jnp.zeros((lead, d), F32)
        h_ref[lead:base, :] = st_ref[0]

    h = _rms(x, g_ref[...])
    h_ref[base:end, :] = h
    pos = pos0 + t * tm + lax.broadcasted_iota(I32, (tm, gw), 0)
    ys = []
    prev, c_prev = h_ref, 0
    for lv in range(1, n_lv + 1):
        w, shift, start = POOL_WINDOWS[lv - 1], 2 ** (lv - 1), 8 * lv
        c0 = (lv - 1) * gw
        cols = slice(c0 - c_prev, d - c_prev)
        cur = prev[start:end, cols] + prev[start - shift:end - shift, cols]
        if lv < n_lv:
            s_refs[lv - 1][start:end, :] = cur[:, gw:]
        win = cur[base - start:, :gw]
        count = jnp.minimum(w, pos + 1).astype(F32)
        dlt = win / count - h[:, c0:c0 + gw]
        ys.append(_dot(dlt, wg_ref[lv - 1]))
        if lv < n_lv:
            prev, c_prev = s_refs[lv - 1], c0 + gw
    y = jnp.concatenate(ys, axis=-1) * sc_ref[...]
    o_ref[0] = x + y
    last = h_ref[end - POOL_STATE:end, :]
    h_ref[lead:base, :] = last

    @pl.when(t == pl.num_programs(1) - 1)
    def _():
        nst_ref[0] = last


def _pool_mixer(x, g, state, w_group, scale, pos0):
    b, t, d = x.shape
    ng, gw, _ = w_group.shape
    tm = _row_tile(t, ROW_TILE)
    assert POOL_WINDOWS == tuple(2 ** (lv + 1) for lv in range(ng)) and tm >= POOL_STATE
    rows = tm + 2 * (POOL_STATE + 1)
    return pl.pallas_call(
        functools.partial(_pool_kernel, pos0=pos0),
        grid=(b, t // tm),
        in_specs=[pl.BlockSpec((1, tm, d), lambda i, j: (i, j, 0)),
                  pl.BlockSpec((1, d), lambda i, j: (0, 0)),
                  pl.BlockSpec((1, POOL_STATE, d), lambda i, j: (i, 0, 0)),
                  pl.BlockSpec((ng, gw, gw), lambda i, j: (0, 0, 0)),
                  pl.BlockSpec((1, d), lambda i, j: (0, 0))],
        out_specs=[pl.BlockSpec((1, tm, d), lambda i, j: (i, j, 0)),
                   pl.BlockSpec((1, POOL_STATE, d), lambda i, j: (i, 0, 0))],
        out_shape=[jax.ShapeDtypeStruct((b, t, d), F32),
                   jax.ShapeDtypeStruct((b, POOL_STATE, d), F32)],
        scratch_shapes=[pltpu.VMEM((rows, d - lv * gw), F32) for lv in range(ng)],
        compiler_params=_cparams("parallel", "arbitrary"),
        name="pool_mixer",
    )(x, g.reshape(1, d), state, w_group, scale.reshape(1, d))


def _proj_kernel(x_ref, g_ref, w_ref, e_ref, *o_refs, widths, outs):
    h = _rms(x_ref[0], g_ref[...]).astype(BF16)
    starts = [sum(widths[:i]) for i in range(len(widths))]
    ys = {}
    for (wi, mode, _, ep), o_ref in zip(outs, o_refs):
        if wi not in ys:
            w = w_ref[:, starts[wi]:starts[wi] + widths[wi]].astype(BF16)
            if widths[wi] % LANE:
                w = jnp.concatenate([w, jnp.zeros((w.shape[0], -widths[wi] % LANE), BF16)], axis=1)
            ys[wi] = jnp.dot(h, w, preferred_element_type=F32)
        y = ys[wi]
        if ep == "log_sigmoid":
            u = -(y + e_ref[...])
            y = -(jnp.maximum(u, 0.0) + jnp.log1p(jnp.exp(-jnp.abs(u))))
        if mode == "rows":
            o_ref[0] = y.astype(o_ref.dtype)
        elif isinstance(mode, tuple) and mode[0] == "first":
            o_ref[0] = y[:, :mode[1]].astype(o_ref.dtype)
        elif isinstance(mode, tuple):
            o_ref[0] = jnp.transpose(y)[:mode[1], :].astype(o_ref.dtype)
        elif mode == "t":
            o_ref[0] = jnp.transpose(y).astype(o_ref.dtype)
        else:
            o_ref[0] = pltpu.einshape("m(hd)->mhd", y.astype(o_ref.dtype), d=mode)


def _proj(x, g, w, widths, outs, extra):
    b, t, d = x.shape
    assert sum(widths) == w.shape[1]
    tm = _row_tile(t, ROW_TILE)
    in_specs = [pl.BlockSpec((1, tm, d), lambda i, j: (i, j, 0)),
                pl.BlockSpec((1, d), lambda i, j: (0, 0)),
                pl.BlockSpec(w.shape, lambda i, j: (0, 0), pipeline_mode=pl.Buffered(1)),
                pl.BlockSpec(extra.shape, lambda i, j: (0, 0))]
    out_specs, out_shape = [], []
    for wi, mode, dt, _ in outs:
        n = _round_up(widths[wi], LANE)
        if isinstance(mode, tuple):
            mode, n = ("rows" if mode[0] == "first" else "t"), mode[1]
        if mode == "rows":
            out_specs.append(pl.BlockSpec((1, tm, n), lambda i, j: (i, j, 0)))
            out_shape.append(jax.ShapeDtypeStruct((b, t, n), dt))
        elif mode == "t":
            out_specs.append(pl.BlockSpec((1, n, tm), lambda i, j: (i, 0, j)))
            out_shape.append(jax.ShapeDtypeStruct((b, n, t), dt))
        else:
            out_specs.append(pl.BlockSpec((1, tm, n // mode, mode), lambda i, j: (i, j, 0, 0)))
            out_shape.append(jax.ShapeDtypeStruct((b, t, n // mode, mode), dt))
    return pl.pallas_call(
        functools.partial(_proj_kernel, widths=tuple(widths), outs=tuple(outs)),
        grid=(b, t // tm),
        in_specs=in_specs,
        out_specs=out_specs,
        out_shape=out_shape,
        compiler_params=_cparams("parallel", "parallel"),
        name="norm_proj",
    )(x, g.reshape(1, d), w, extra)


def _outproj_kernel(x_ref, a_ref, w_ref, o_ref):
    o_ref[...] = x_ref[...] + _dot(a_ref[...], w_ref[...])


def _out_proj(x, a, w):
    n, d = x.shape
    tm = _row_tile(n, ROW_TILE)
    return pl.pallas_call(
        _outproj_kernel,
        grid=(n // tm,),
        in_specs=[pl.BlockSpec((tm, d), lambda i: (i, 0)),
                  pl.BlockSpec((tm, d), lambda i: (i, 0)),
                  pl.BlockSpec((d, d), lambda i: (0, 0))],
        out_specs=pl.BlockSpec((tm, d), lambda i: (i, 0)),
        out_shape=jax.ShapeDtypeStruct((n, d), F32),
        compiler_params=_cparams("parallel"),
        name="out_proj",
    )(x, a, w)


def _bias_table_kernel(rbt_ref, o_ref):
    width = o_ref.shape[-1]
    rel = BIAS_CENTER - lax.broadcasted_iota(I32, (1, width), 1)
    nb = N_BUCKETS // 2
    max_exact = nb // 2
    ret = (rel > 0).astype(I32) * nb
    n = jnp.abs(rel)
    nf = jnp.maximum(n, 1).astype(F32)
    large = max_exact + (jnp.log(nf / max_exact) / math.log(MAX_DIST / max_exact)
                         * (nb - max_exact)).astype(I32)
    large = jnp.minimum(large, nb - 1)
    bucket = ret + jnp.where(n < max_exact, n, large)
    acc = jnp.zeros(o_ref.shape, F32)
    for j in range(N_BUCKETS):
        acc = jnp.where(bucket == j, rbt_ref[:, j:j + 1], acc)
    o_ref[...] = acc * LOG2E


def _bias_tiles_kernel(rbt_ref, far_ref, near_ref, tab_ref):
    n_d, nh, kb, tq = near_ref.shape
    _bias_table_kernel(rbt_ref, tab_ref)
    far_ref[...] = jnp.broadcast_to(tab_ref[:, BIAS_CENTER + MAX_DIST:BIAS_CENTER + MAX_DIST + 1], far_ref.shape)
    for dd in range(n_d):
        s0 = BIAS_CENTER - (dd - 1) * kb - kb
        for h in range(nh):
            rows = jnp.broadcast_to(tab_ref[h:h + 1, s0:s0 + tq + kb], (kb, tq + kb))
            near_ref[dd, h] = pltpu.roll(rows, 0, 1, stride=1, stride_axis=0)[:, kb:]


def _bias_tiles(rel_bias, kb, tq):
    nh = rel_bias.shape[1]
    n_d = tq // kb + 1
    width = BIAS_CENTER + kb + tq + kb
    assert kb % LANE == 0 and tq % kb == 0 and kb >= MAX_DIST and BIAS_CENTER >= tq
    return pl.pallas_call(
        _bias_tiles_kernel,
        out_shape=[jax.ShapeDtypeStruct((nh, LANE), F32), jax.ShapeDtypeStruct((n_d, nh, kb, tq), F32)],
        scratch_shapes=[pltpu.VMEM((nh, width), F32)],
        name="bias_tiles",
    )(rel_bias.T)


def _sortable(x):
    x = jnp.where(x == 0.0, 0.0, x)
    bits = lax.bitcast_convert_type(x, I32)
    return jnp.where(bits < 0, bits ^ 0x7FFFFFFF, bits)


def _neg_inf_key():
    return int(np.float32(NEG_INF).view(np.int32)) ^ 0x7FFFFFFF


def _dsa_prompt_kernel(qt_ref, qit_ref, kwt_ref, kw_ref, k_ref, vt_ref, far_ref, near_ref, x_ref, wout_ref, o_ref,
                       key_ref, hi_ref, lo_ref, sel_ref, m_ref, l_ref, acc_ref, a_ref, *, top_k):
    qi = pl.program_id(1)
    tq = qt_ref.shape[2]
    n_keys = kw_ref.shape[1]
    kb_sz = LANE
    q0 = qi * tq
    nkb = jnp.minimum(n_keys, q0 + tq) // kb_sz
    negkey = _neg_inf_key()

    qlane = lax.broadcasted_iota(I32, (1, tq), 1)
    lim = ((q0 + qlane) // CHUNK + 1) * CHUNK
    krow = lax.broadcasted_iota(I32, (kb_sz, tq), 0)

    def kslice(kb):
        return pl.ds(pl.multiple_of(kb * kb_sz, kb_sz), kb_sz)

    sb = 2 * kb_sz
    srow = lax.broadcasted_iota(I32, (sb, tq), 0)

    def score_body(i, c):
        rows = pl.ds(pl.multiple_of(i * sb, sb), sb)
        kid = kw_ref[0, rows, :][:, :IDX_DIM].astype(BF16)
        sc = jnp.zeros((sb, tq), F32)
        for h in range(IDX_HEADS):
            s = jnp.dot(kid, qit_ref[0, h * IDX_DIM:(h + 1) * IDX_DIM, :], preferred_element_type=F32)
            sc = sc + jnp.maximum(s, 0.0) * kwt_ref[0, IDX_DIM + h:IDX_DIM + h + 1, :]
        sc = sc * ((IDX_DIM * IDX_HEADS) ** -0.5)
        key = jnp.where(i * sb + srow < lim, _sortable(sc), negkey)
        key_ref[rows, :] = key
        hi_ref[rows, :] = (key >> 16).astype(I16)
        lo_ref[rows, :] = ((key & 0xFFFF) - 0x8000).astype(I16)
        return c

    lax.fori_loop(0, nkb // 2, score_body, 0)

    def search16(ref):
        def bit_body(i, t_u):
            cand_u = t_u | jnp.left_shift(jnp.int32(1), 15 - i)
            cand = (cand_u - 0x8000).astype(I16)

            def body(j, a):
                ind = jnp.where(ref[pl.ds(pl.multiple_of(j * sb, sb), sb), :] >= cand,
                                jnp.ones((), I16), jnp.zeros((), I16))
                parts = [ind[16 * r:16 * (r + 1), :] for r in range(sb // 16)]
                while len(parts) > 1:
                    parts = [parts[r] + parts[r + 1] for r in range(0, len(parts), 2)]
                return a + parts[0]
            a = lax.fori_loop(0, nkb // 2, body, jnp.zeros((16, tq), I16))
            cnt = jnp.sum(a.astype(I32), axis=0, keepdims=True)
            return jnp.where(cnt >= top_k, cand_u, t_u)
        return lax.fori_loop(0, 16, bit_body, jnp.zeros((1, tq), I32))

    def count(pred_fn):
        def body(i, a):
            for u in range(2):
                kb = 2 * i + u
                ind = pred_fn(kb, key_ref[kslice(kb), :])
                a = a + jnp.sum(ind.reshape(kb_sz // 8, 8, tq), axis=0)
            return a
        a = lax.fori_loop(0, nkb // 2, body, jnp.zeros((8, tq), I32))
        return jnp.sum(a, axis=0, keepdims=True)

    t_hi = search16(hi_ref)
    t_hi16 = (t_hi - 0x8000).astype(I16)

    def lo_body(j, c):
        rows = pl.ds(pl.multiple_of(j * sb, sb), sb)
        hi = hi_ref[rows, :]
        lo_ref[rows, :] = jnp.where(hi == t_hi16, lo_ref[rows, :],
                                    jnp.where(hi > t_hi16, jnp.full((), 0x7FFF, I16), jnp.full((), -0x8000, I16)))
        return c

    lax.fori_loop(0, nkb // 2, lo_body, 0)
    t_s = (jnp.left_shift(t_hi, 16) | search16(lo_ref)) ^ INT_MIN

    def adm01(kb):
        return jnp.where(kb * kb_sz + krow < lim, 1.0, 0.0)

    def sel_body(kb, a):
        sel = jnp.where(key_ref[kslice(kb), :] >= t_s, adm01(kb), 0.0)
        sel_ref[kslice(kb), :] = sel
        return a + jnp.sum(sel.reshape(kb_sz // 8, 8, tq), axis=0)

    n_sel = jnp.sum(lax.fori_loop(0, nkb, sel_body, jnp.zeros((8, tq), F32)), axis=0, keepdims=True)

    @pl.when(jnp.max(n_sel) > top_k)
    def _():
        n_gt = count(lambda kb, key: jnp.where(key > t_s, 1, 0))
        need = (top_k - n_gt).astype(F32)
        r = lax.broadcasted_iota(I32, (kb_sz, kb_sz), 0)
        c = lax.broadcasted_iota(I32, (kb_sz, kb_sz), 1)
        ltri = jnp.where(c < r, 1.0, 0.0).astype(BF16)

        def tie_body(kb, carry):
            key = key_ref[kslice(kb), :]
            adm = adm01(kb)
            eq = jnp.where(key == t_s, adm, 0.0)
            rank = carry + jnp.dot(ltri, eq.astype(BF16), preferred_element_type=F32)
            keep = jnp.where(rank < need, eq, 0.0)
            sel_ref[kslice(kb), :] = jnp.where(key > t_s, adm, keep)
            return carry + jnp.sum(eq, axis=0, keepdims=True)

        lax.fori_loop(0, nkb, tie_body, jnp.zeros((1, tq), F32))

    m_ref[...] = jnp.full(m_ref.shape, NEG_INF, F32)
    l_ref[...] = jnp.zeros(l_ref.shape, F32)
    acc_ref[...] = jnp.zeros(acc_ref.shape, F32)
    c1 = (B_HD ** -0.5) * LOG2E

    def attend(k0, nk, bias2_fn):
        rows = pl.ds(pl.multiple_of(k0, LANE), nk)
        sel = sel_ref[rows, :] != 0.0
        ks = k_ref[0, rows, :]
        cols = []
        for h in range(B_HEADS):
            g = h // B_REP
            z = jnp.dot(ks[:, g * B_HD:(g + 1) * B_HD], qt_ref[0, h * B_HD:(h + 1) * B_HD, :],
                        preferred_element_type=F32)
            a = jnp.where(sel, z * c1 + bias2_fn(h), NEG_INF)
            a_ref[h, 0:nk, :] = a
            cols.append(jnp.max(a, axis=0, keepdims=True))
        m_old = m_ref[...]
        m_new = jnp.maximum(m_old, jnp.concatenate(cols, axis=0))
        alpha = jnp.exp2(m_old - m_new)
        m_ref[...] = m_new
        sums = []
        for h in range(B_HEADS):
            g = h // B_REP
            p = jnp.exp2(a_ref[h, 0:nk, :] - m_new[h:h + 1, :])
            sums.append(jnp.sum(p, axis=0, keepdims=True))
            hs = slice(h * B_HD, (h + 1) * B_HD)
            pv = jnp.dot(vt_ref[0, g * B_HD:(g + 1) * B_HD, rows], p.astype(BF16),
                         preferred_element_type=F32)
            acc_ref[hs, :] = alpha[h:h + 1, :] * acc_ref[hs, :] + pv
        l_ref[...] = alpha * l_ref[...] + jnp.concatenate(sums, axis=0)

    ab = near_ref.shape[2]
    n_far = jnp.maximum(q0 // ab - 1, 0)
    far_bias2 = far_ref[:, 0:1]

    def far_body(i, c):
        attend(i * ab, ab, lambda h: far_bias2[h:h + 1, :])
        return c

    lax.fori_loop(0, n_far, far_body, 0)

    def near_body(i, c):
        dd = i - q0 // ab + 1
        attend(i * ab, ab, lambda h: near_ref[dd, h])
        return c

    lax.fori_loop(n_far, nkb * kb_sz // ab, near_body, 0)

    _finish_heads(o_ref, x_ref, wout_ref, l_ref, acc_ref, B_HEADS)


def _dsa_attention_prompt(qt, qit, kwt, kw, k, vt, far, near, x, w_out):
    b, d, t = qt.shape
    tq = near.shape[3]
    top_k = min(TOPK_MAX, t // 4)
    ab = near.shape[2]
    assert t % tq == 0 and tq % CHUNK == 0 and tq % ab == 0 and ab % LANE == 0 and tq % (2 * LANE) == 0
    return pl.pallas_call(
        functools.partial(_dsa_prompt_kernel, top_k=top_k),
        grid=(b, t // tq),
        in_specs=[pl.BlockSpec((1, d, tq), lambda i, j: (i, 0, j)),
                  pl.BlockSpec((1, qit.shape[1], tq), lambda i, j: (i, 0, j)),
                  pl.BlockSpec((1, LANE, tq), lambda i, j: (i, 0, j)),
                  pl.BlockSpec((1, t, LANE), lambda i, j: (i, 0, 0)),
                  pl.BlockSpec((1, t, B_KV * B_HD), lambda i, j: (i, 0, 0)),
                  pl.BlockSpec((1, B_KV * B_HD, t), lambda i, j: (i, 0, 0)),
                  pl.BlockSpec(far.shape, lambda i, j: (0, 0)),
                  pl.BlockSpec(near.shape, lambda i, j: (0, 0, 0, 0)),
                  pl.BlockSpec((1, tq, d), lambda i, j: (i, j, 0)),
                  pl.BlockSpec((d, d), lambda i, j: (0, 0))],
        out_specs=pl.BlockSpec((1, tq, d), lambda i, j: (i, j, 0)),
        out_shape=jax.ShapeDtypeStruct((b, t, d), F32),
        scratch_shapes=[pltpu.VMEM((t, tq), I32), pltpu.VMEM((t, tq), I16), pltpu.VMEM((t, tq), I16),
                        pltpu.VMEM((t, tq), F32),
                        pltpu.VMEM((B_HEADS, tq), F32), pltpu.VMEM((B_HEADS, tq), F32),
                        pltpu.VMEM((d, tq), F32), pltpu.VMEM((B_HEADS, ab, tq), F32)],
        compiler_params=_cparams("parallel", "parallel"),
        name="dsa_attention",
    )(qt, qit, kwt, kw, k, vt, far, near, x, w_out)


def _dsa_cached_kernel(q_ref, qi_ref, wi_ref, rb_ref, kidx_ref, kp_ref, vp_ref, kn_ref, vn_ref, o_ref,
                       *, past, t_new, top_k):
    n_keys = past + t_new
    lp = kidx_ref.shape[1]
    negkey = _neg_inf_key()
    kpos = lax.broadcasted_iota(I32, (t_new, lp), 1)
    qpos = past + lax.broadcasted_iota(I32, (t_new, lp), 0)
    adm = kpos < (qpos // CHUNK + 1) * CHUNK

    s = _dot_nt(qi_ref[0], kidx_ref[0])
    w = jnp.maximum(s, 0.0) * wi_ref[0]
    sc = w[0:t_new]
    for h in range(1, IDX_HEADS):
        sc = sc + w[h * t_new:(h + 1) * t_new]
    sc = sc * ((IDX_DIM * IDX_HEADS) ** -0.5)
    key = jnp.where(adm, _sortable(sc), negkey)
    key = jnp.where(kpos < n_keys, key, INT_MIN)

    def bit_body(i, t_u):
        cand_u = t_u | jnp.left_shift(jnp.int32(1), 31 - i)
        cnt = jnp.sum(jnp.where(key >= (cand_u ^ INT_MIN), 1.0, 0.0), axis=1, keepdims=True)
        return jnp.where(cnt >= top_k, cand_u, t_u)

    t_s = lax.fori_loop(0, 32, bit_body, jnp.zeros((t_new, 1), I32)) ^ INT_MIN

    adm01 = jnp.where(adm, 1.0, 0.0)
    gt = jnp.where(key > t_s, adm01, 0.0)
    eq = jnp.where(key == t_s, adm01, 0.0)
    need = top_k - jnp.sum(jnp.where(key > t_s, 1.0, 0.0), axis=1, keepdims=True)
    r = lax.broadcasted_iota(I32, (LANE, LANE), 0)
    c = lax.broadcasted_iota(I32, (LANE, LANE), 1)
    utri = jnp.where(r < c, 1.0, 0.0).astype(BF16)
    carry = jnp.zeros((t_new, 1), F32)
    keeps = []
    for blk in range(lp // LANE):
        e = eq[:, blk * LANE:(blk + 1) * LANE]
        rank = carry + jnp.dot(e.astype(BF16), utri, preferred_element_type=F32)
        keeps.append(jnp.where(rank < need, e, 0.0))
        carry = carry + jnp.sum(e, axis=1, keepdims=True)
    sel = gt + jnp.concatenate(keeps, axis=1)

    near = max(past - MAX_DIST, 0) // LANE * LANE
    rel = (kpos - qpos)[:, near:]
    nb = N_BUCKETS // 2
    max_exact = nb // 2
    n = jnp.abs(rel)
    nf = jnp.maximum(n, 1).astype(F32)
    large = max_exact + (jnp.log(nf / max_exact) / math.log(MAX_DIST / max_exact)
                         * (nb - max_exact)).astype(I32)
    bucket = (rel > 0).astype(I32) * nb + jnp.where(n < max_exact, n, jnp.minimum(large, nb - 1))

    rows = B_REP * t_new
    sel_g = jnp.concatenate([sel] * B_REP, axis=0) != 0.0
    bucket_g = jnp.concatenate([bucket] * B_REP, axis=0)
    for g in range(B_KV):
        grp = lambda ref, n: ref[0, pl.ds(g, n, stride=B_KV), :]
        qg = q_ref[0, g * rows:(g + 1) * rows, :]
        rb = rb_ref[g * rows:(g + 1) * rows, :]
        bias_near = jnp.zeros((rows, lp - near), F32)
        for j in range(N_BUCKETS):
            bias_near = jnp.where(bucket_g == j, rb[:, j:j + 1], bias_near)
        bias = jnp.concatenate([jnp.broadcast_to(rb[:, nb - 1:nb], (rows, near)), bias_near], axis=1)
        zp = _dot_nt(qg, grp(kp_ref, past)) * (B_HD ** -0.5)
        zn = _dot_nt(qg, grp(kn_ref, t_new)) * (B_HD ** -0.5)
        ap = jnp.where(sel_g[:, :past], zp + bias[:, :past], NEG_INF)
        an = jnp.where(sel_g[:, past:n_keys], zn + bias[:, past:n_keys], NEG_INF)
        m = jnp.maximum(jnp.max(ap, axis=1, keepdims=True), jnp.max(an, axis=1, keepdims=True))
        pp, pn = jnp.exp(ap - m), jnp.exp(an - m)
        l = jnp.sum(pp, axis=1, keepdims=True) + jnp.sum(pn, axis=1, keepdims=True)
        o_ref[0, g * rows:(g + 1) * rows, :] = (_dot(pp, grp(vp_ref, past)) + _dot(pn, grp(vn_ref, t_new))) / l


def _dsa_attention_cached(q_rows, qi_rows, wi_col, rb_rows, kidx_all, k_past, v_past, k_new, v_new):
    b, rows, hd = q_rows.shape
    past, t_new = k_past.shape[1], k_new.shape[1]
    lp = kidx_all.shape[1]
    top_k = min(TOPK_MAX, (past + t_new) // 4)
    assert past % LANE == 0
    flat = lambda a: a.reshape(b, a.shape[1] * B_KV, B_HD)
    kv_spec = lambda n: pl.BlockSpec((1, n * B_KV, B_HD), lambda i: (i, 0, 0))
    return pl.pallas_call(
        functools.partial(_dsa_cached_kernel, past=past, t_new=t_new, top_k=top_k),
        grid=(b,),
        in_specs=[pl.BlockSpec((1, rows, hd), lambda i: (i, 0, 0)),
                  pl.BlockSpec((1,) + qi_rows.shape[1:], lambda i: (i, 0, 0)),
                  pl.BlockSpec((1,) + wi_col.shape[1:], lambda i: (i, 0, 0)),
                  pl.BlockSpec(rb_rows.shape, lambda i: (0, 0)),
                  pl.BlockSpec((1, lp, IDX_DIM), lambda i: (i, 0, 0)),
                  kv_spec(past), kv_spec(past), kv_spec(t_new), kv_spec(t_new)],
        out_specs=pl.BlockSpec((1, rows, hd), lambda i: (i, 0, 0)),
        out_shape=jax.ShapeDtypeStruct((b, rows, hd), F32),
        compiler_params=_cparams("parallel"),
        name="dsa_attention_cached",
    )(q_rows, qi_rows, wi_col, rb_rows, kidx_all, flat(k_past), flat(v_past), flat(k_new), flat(v_new))


def _cumsum_kernel(x_ref, o_ref):
    x = x_ref[0]
    n = x.shape[-1]
    lane = lax.broadcasted_iota(I32, x.shape, 1)
    s = 1
    while s < n:
        x = x + jnp.where(lane >= s, pltpu.roll(x, s, 1), 0.0)
        s *= 2
    o_ref[0] = x


def _cumsum_lanes(x):
    b, h, n = x.shape
    return pl.pallas_call(
        _cumsum_kernel,
        grid=(b,),
        in_specs=[pl.BlockSpec((1, h, n), lambda i: (i, 0, 0))],
        out_specs=pl.BlockSpec((1, h, n), lambda i: (i, 0, 0)),
        out_shape=jax.ShapeDtypeStruct((b, h, n), F32),
        compiler_params=_cparams("parallel"),
        name="logf_cumsum",
    )(x)


def _fox_init(m_ref, l_ref, acc_ref):
    m_ref[...] = jnp.full(m_ref.shape, NEG_INF, F32)
    l_ref[...] = jnp.zeros(l_ref.shape, F32)
    acc_ref[...] = jnp.zeros(acc_ref.shape, F32)


def _fox_tile(z_fn, pv_fn, cq, ck, mask, m_ref, l_ref, acc_ref, a_ref):
    c1 = (D_HD ** -0.5) * LOG2E
    cq2, ck2 = cq * LOG2E, ck * LOG2E
    cols = []
    for h in range(D_HEADS):
        a = z_fn(h) * c1 - ck2[:, h:h + 1]
        if mask is not None:
            a = jnp.where(mask, a, NEG_INF)
        a_ref[h] = a
        cols.append(jnp.max(a, axis=0, keepdims=True))
    m_old = m_ref[...]
    m_new = jnp.maximum(m_old, jnp.concatenate(cols, axis=0) + cq2)
    alpha = jnp.exp2(m_old - m_new)
    shift = m_new - cq2
    m_ref[...] = m_new
    sums = []
    for h in range(D_HEADS):
        p = jnp.exp2(a_ref[h] - shift[h:h + 1, :])
        sums.append(jnp.sum(p, axis=0, keepdims=True))
        hs = slice(h * D_HD, (h + 1) * D_HD)
        acc_ref[hs, :] = alpha[h:h + 1, :] * acc_ref[hs, :] + pv_fn(h, p.astype(BF16))
    l_ref[...] = alpha * l_ref[...] + jnp.concatenate(sums, axis=0)


def _hs(h):
    return slice(h * D_HD, (h + 1) * D_HD)


def _fox_prompt_kernel(qt_ref, k_ref, vt_ref, cq_ref, ck_ref, x_ref, wout_ref, o_ref,
                       m_ref, l_ref, acc_ref, a_ref):
    qi, step = pl.program_id(1), pl.program_id(2)
    tq, tk = qt_ref.shape[2], a_ref.shape[1]
    q0 = qi * tq
    ki = step - (pl.num_programs(2) - 1 - (q0 + tq - 1) // tk)
    k0 = ki * tk
    keys = pl.ds(pl.multiple_of(jnp.maximum(k0, 0), tk), tk)

    @pl.when(step == 0)
    def _():
        _fox_init(m_ref, l_ref, acc_ref)

    def run(masked):
        mask = None
        if masked:
            mask = (k0 + lax.broadcasted_iota(I32, (tk, tq), 0)) <= (q0 + lax.broadcasted_iota(I32, (tk, tq), 1))
        _fox_tile(lambda h: jnp.dot(k_ref[0, keys, _hs(h)], qt_ref[0, _hs(h), :], preferred_element_type=F32),
                  lambda h, p: jnp.dot(vt_ref[0, _hs(h), keys], p, preferred_element_type=F32),
                  cq_ref[0], ck_ref[0, keys, :], mask, m_ref, l_ref, acc_ref, a_ref)

    fully_visible = k0 + tk - 1 <= q0
    pl.when(jnp.logical_and(ki >= 0, fully_visible))(lambda: run(False))
    pl.when(jnp.logical_and(ki >= 0, jnp.logical_not(fully_visible)))(lambda: run(True))

    @pl.when(step == pl.num_programs(2) - 1)
    def _():
        _finish_heads(o_ref, x_ref, wout_ref, l_ref, acc_ref, D_HEADS)


def _fox_attention_prompt(qt, k, vt, cum_t, cum, x, w_out, tq, tk):
    b, d, t = qt.shape
    nq, nk = t // tq, t // tk
    return pl.pallas_call(
        _fox_prompt_kernel,
        grid=(b, nq, nk),
        in_specs=[pl.BlockSpec((1, d, tq), lambda i, j, kk: (i, 0, j)),
                  pl.BlockSpec((1, t, d), lambda i, j, kk: (i, 0, 0)),
                  pl.BlockSpec((1, d, t), lambda i, j, kk: (i, 0, 0)),
                  pl.BlockSpec((1, D_HEADS, tq), lambda i, j, kk: (i, 0, j)),
                  pl.BlockSpec((1, t, D_HEADS), lambda i, j, kk: (i, 0, 0)),
                  pl.BlockSpec((1, tq, d), lambda i, j, kk: (i, j, 0)),
                  pl.BlockSpec((d, d), lambda i, j, kk: (0, 0))],
        out_specs=pl.BlockSpec((1, tq, d), lambda i, j, kk: (i, j, 0)),
        out_shape=jax.ShapeDtypeStruct((b, t, d), F32),
        scratch_shapes=[pltpu.VMEM((D_HEADS, tq), F32), pltpu.VMEM((D_HEADS, tq), F32),
                        pltpu.VMEM((d, tq), F32), pltpu.VMEM((D_HEADS, tk, tq), F32)],
        compiler_params=_cparams("parallel", "parallel", "arbitrary"),
        name="fox_attention",
    )(qt, k, vt, cum_t, cum, x, w_out)


def _fox_cached_kernel(q_ref, kp_ref, vp_ref, kn_ref, vn_ref, cq_ref, ckp_ref, ckn_ref, o_ref,
                       m_ref, l_ref, acc_ref, *, t_new):
    ki = pl.program_id(1)
    n_past = pl.num_programs(1) - 1
    rows = q_ref.shape[1]
    c1 = (D_HD ** -0.5) * LOG2E

    @pl.when(ki == 0)
    def _():
        m_ref[...] = jnp.full(m_ref.shape, NEG_INF, F32)
        l_ref[...] = jnp.zeros(l_ref.shape, F32)
        acc_ref[...] = jnp.zeros(acc_ref.shape, F32)

    def tile(k2d, v2d, ck_row, causal):
        cols = k2d.shape[0]
        a = _dot_nt(q_ref[0], k2d) * c1 - ck_row * LOG2E
        r = lax.broadcasted_iota(I32, (rows, cols), 0)
        c = lax.broadcasted_iota(I32, (rows, cols), 1)
        ok = (c % D_HEADS) == (r // t_new)
        if causal:
            ok = jnp.logical_and(ok, (c // D_HEADS) <= (r % t_new))
        a = jnp.where(ok, a, NEG_INF)
        cq2 = cq_ref[0] * LOG2E
        m_old = m_ref[...]
        m_new = jnp.maximum(m_old, jnp.max(a, axis=1, keepdims=True) + cq2)
        alpha = jnp.exp2(m_old - m_new)
        p = jnp.exp2(a - (m_new - cq2))
        l_ref[...] = alpha * l_ref[...] + jnp.sum(p, axis=1, keepdims=True)
        acc_ref[...] = alpha * acc_ref[...] + _dot(p, v2d)
        m_ref[...] = m_new

    @pl.when(ki < n_past)
    def _():
        tk = kp_ref.shape[1]
        tile(kp_ref[0].reshape(tk * D_HEADS, D_HD), vp_ref[0].reshape(tk * D_HEADS, D_HD), ckp_ref[0], False)

    @pl.when(ki == n_past)
    def _():
        tile(kn_ref[0].reshape(t_new * D_HEADS, D_HD), vn_ref[0].reshape(t_new * D_HEADS, D_HD), ckn_ref[0], True)
        o_ref[0] = acc_ref[...] / l_ref[...]


def _fox_attention_cached(q_rows, k_past, v_past, k_new, v_new, cq_col, ck_past, ck_new, tk):
    b, rows, hd = q_rows.shape
    past, t_new = k_past.shape[1], k_new.shape[1]
    n_past = past // tk
    pidx = lambda i, kk: (i, jnp.minimum(kk, n_past - 1), 0, 0)
    return pl.pallas_call(
        functools.partial(_fox_cached_kernel, t_new=t_new),
        grid=(b, n_past + 1),
        in_specs=[pl.BlockSpec((1, rows, hd), lambda i, kk: (i, 0, 0)),
                  pl.BlockSpec((1, tk, D_HEADS, D_HD), pidx),
                  pl.BlockSpec((1, tk, D_HEADS, D_HD), pidx),
                  pl.BlockSpec((1, t_new, D_HEADS, D_HD), lambda i, kk: (i, 0, 0, 0)),
                  pl.BlockSpec((1, t_new, D_HEADS, D_HD), lambda i, kk: (i, 0, 0, 0)),
                  pl.BlockSpec((1, rows, 1), lambda i, kk: (i, 0, 0)),
                  pl.BlockSpec((1, 1, tk * D_HEADS), lambda i, kk: (i, 0, jnp.minimum(kk, n_past - 1))),
                  pl.BlockSpec((1, 1, t_new * D_HEADS), lambda i, kk: (i, 0, 0))],
        out_specs=pl.BlockSpec((1, rows, hd), lambda i, kk: (i, 0, 0)),
        out_shape=jax.ShapeDtypeStruct((b, rows, hd), F32),
        scratch_shapes=[pltpu.VMEM((rows, 1), F32), pltpu.VMEM((rows, 1), F32), pltpu.VMEM((rows, hd), F32)],
        compiler_params=_cparams("parallel", "arbitrary"),
        name="fox_attention_cached",
    )(q_rows, k_past, v_past, k_new, v_new, cq_col, ck_past, ck_new)


def _pad_rows(a, rows):
    if a.shape[1] == rows:
        return a
    return jnp.pad(a, ((0, 0), (0, rows - a.shape[1])) + ((0, 0),) * (a.ndim - 2))


def _round_up(n, m):
    return -(-n // m) * m


def _dsa_mixer(x, g, k_past, v_past, ki_past, w, w_out, rel_bias):
    b, t, d = x.shape
    past = k_past.shape[1]
    n_keys = past + t
    if past == 0:
        qt, k4, kb, v4, vt, qit, kw, kwt, ki_t = _proj(
            x, g, w, B_WIDTHS, ((0, "t", BF16, None), (1, B_HD, F32, None), (1, "rows", BF16, None),
                      (2, B_HD, F32, None), (2, "t", BF16, None), (3, "t", BF16, None),
                      (4, "rows", F32, None), (4, "t", F32, None), (4, ("first_t", IDX_DIM), F32, None)),
            jnp.zeros((1, LANE), F32))
        far, near = _bias_tiles(rel_bias, DSA_TILE, DSA_TILE)
        y = _dsa_attention_prompt(qt, qit, kwt, kw, kb, vt, far, near, x, w_out)
        return (y, k4, v4, jnp.swapaxes(ki_t, 1, 2))
    q, k4, v4, qidx, kw = _proj(
        x, g, w, B_WIDTHS, ((0, "rows", BF16, None), (1, B_HD, F32, None), (2, B_HD, F32, None),
                  (3, "rows", BF16, None), (4, "rows", F32, None)),
        jnp.zeros((1, LANE), F32))
    ki = kw[:, :, :IDX_DIM]
    to_rows = lambda a, nh: jnp.swapaxes(a.reshape(b, t, nh, -1), 1, 2).reshape(b, nh * t, -1)
    kidx_all = _pad_rows(jnp.concatenate([ki_past, ki], axis=1), _round_up(n_keys, LANE))
    o = _dsa_attention_cached(to_rows(q, B_HEADS), to_rows(qidx, IDX_HEADS),
                              to_rows(kw[:, :, IDX_DIM:IDX_DIM + IDX_HEADS], IDX_HEADS),
                              jnp.repeat(rel_bias.T, t, axis=0), kidx_all, k_past, v_past, k4, v4)
    o = jnp.swapaxes(o.reshape(b, B_HEADS, t, B_HD), 1, 2).reshape(b * t, d)
    y = _out_proj(x.reshape(b * t, d), o, w_out).reshape(b, t, d)
    return (y, k4, v4, ki)


def _fox_mixer(x, g, k_past, v_past, lf_past, w, b_f, w_out):
    b, t, d = x.shape
    past = k_past.shape[1]
    heads4 = ((1, D_HD, F32, None), (2, D_HD, F32, None))
    if past == 0:
        tq = tk = min(FOX_TILE, t)
        k4, v4, logf_t, qt, kb, vt = _proj(
            x, g, w, D_WIDTHS, heads4 + ((3, ("first_t", D_HEADS), F32, "log_sigmoid"), (0, "t", BF16, None),
                               (1, "rows", BF16, None), (2, "t", BF16, None)), b_f)
        cum_t = _cumsum_lanes(logf_t)
        y = _fox_attention_prompt(qt, kb, vt, cum_t, jnp.swapaxes(cum_t, 1, 2), x, w_out, tq, tk)
        return (y, k4, v4, jnp.swapaxes(logf_t, 1, 2))
    else:
        tk = math.gcd(past, FOX_TILE)
        k4, v4, logf, q = _proj(
            x, g, w, D_WIDTHS, heads4 + ((3, ("first", D_HEADS), F32, "log_sigmoid"), (0, "rows", BF16, None)), b_f)
        lf_all = _pad_rows(jnp.concatenate([lf_past, logf], axis=1), _round_up(past + t, LANE))
        cum = jnp.swapaxes(_cumsum_lanes(jnp.swapaxes(lf_all, 1, 2)), 1, 2)[:, :past + t]
        ck = cum.reshape(b, 1, (past + t) * D_HEADS)
        to_rows = lambda a: jnp.swapaxes(a.reshape(b, t, D_HEADS, -1), 1, 2).reshape(b, D_HEADS * t, -1)
        o = _fox_attention_cached(to_rows(q), k_past, v_past, k4, v4, to_rows(cum[:, past:]),
                                  ck[:, :, :past * D_HEADS], ck[:, :, past * D_HEADS:], tk)
        o = jnp.swapaxes(o.reshape(b, D_HEADS, t, D_HD), 1, 2).reshape(b, t, d)
    y = _out_proj(x.reshape(b * t, d), o.reshape(b * t, d), w_out).reshape(b, t, d)
    return (y, k4, v4, logf)


def _run_group(x, pos0, a_st, b_k, b_v, b_ki, c_st, d_k, d_v, d_lf, mem_k, mem_v, prm):
    b, t, d = x.shape
    depth = prm["norm_mix"].shape[0]
    new = {n: [] for n in ("a", "bk", "bv", "bki", "c", "dk", "dv", "dlf")}
    for i in range(depth):
        kind, j = i % 4, i // 4
        g = prm["norm_mix"][i]
        if kind == 0:
            x, st = _conv_mixer(x, g, prm["a_w_in"][j], prm["a_conv"][j], a_st[j], prm["a_w_out"][j])
            new["a"].append(st)
        elif kind == 1:
            x, kk, vv, ki = _dsa_mixer(x, g, b_k[j], b_v[j], b_ki[j], prm["b_w"][j], prm["b_w_out"][j],
                                       prm["rel_bias"])
            new["bk"].append(kk); new["bv"].append(vv); new["bki"].append(ki)
        elif kind == 2:
            x, st = _pool_mixer(x, g, c_st[j], prm["c_w_group"][j], prm["c_scale"][j], pos0)
            new["c"].append(st)
        else:
            x, kk, vv, lf = _fox_mixer(x, g, d_k[j], d_v[j], d_lf[j], prm["d_w"][j], prm["d_b_f"][j],
                                       prm["d_w_out"][j])
            new["dk"].append(kk); new["dv"].append(vv); new["dlf"].append(lf)
        x = _xattn(x, prm["norm_xattn"], prm["xa_wq"], mem_k, mem_v, prm["xa_wo"], i)
        last = i == depth - 1
        x = _ffn(x.reshape(b * t, d), prm["norm_ffn"], prm["ffn_w1"], prm["ffn_w2"],
                 prm["final_norm"], i, last).reshape(b, t, d)
    return (x,) + tuple(jnp.stack(new[n]) for n in ("a", "bk", "bv", "bki", "c", "dk", "dv", "dlf"))


def kernel(x_prompt, x_sample, state_a_conv, cache_b_k, cache_b_v, cache_b_kidx, state_c_pool,
           cache_d_k, cache_d_v, cache_d_logf, cache_mem_k, cache_mem_v, mem_prompt,
           norm_mix, norm_xattn, norm_mem, norm_ffn, final_norm,
           a_w_in, a_conv, a_w_out, b_w_in, b_w_out, rel_bias, c_w_group, c_scale,
           d_w_in, d_b_f, d_w_out, xa_wq, xa_wkv, xa_wo, ffn_w1, ffn_w2):
    bp, _, d = x_prompt.shape
    n_b, n_d = b_w_in.shape[0], d_w_in.shape[0]
    bf = lambda w: w.astype(BF16)
    d_bf = [jnp.pad(d_b_f[j], (0, LANE - D_HEADS)).reshape(1, LANE) for j in range(n_d)]

    prm = {"norm_mix": norm_mix, "norm_xattn": norm_xattn, "norm_ffn": norm_ffn, "final_norm": final_norm,
           "a_w_in": bf(a_w_in), "a_conv": a_conv, "a_w_out": bf(a_w_out),
           "b_w": b_w_in, "b_w_out": bf(b_w_out), "rel_bias": rel_bias,
           "c_w_group": bf(c_w_group), "c_scale": c_scale,
           "d_w": d_w_in, "d_b_f": d_bf, "d_w_out": bf(d_w_out),
           "xa_wq": xa_wq, "xa_wo": xa_wo, "ffn_w1": ffn_w1, "ffn_w2": ffn_w2}

    mk, mv, mk_rows, mv_rows = _memory_kv(mem_prompt, norm_mem, xa_wkv)

    n_a, n_c = a_w_in.shape[0], c_w_group.shape[0]
    z = lambda *s: jnp.zeros(s, F32)
    gp = _run_group(x_prompt, 0,
                    z(n_a, bp, CONV_W - 1, d),
                    z(n_b, bp, 0, B_KV, B_HD), z(n_b, bp, 0, B_KV, B_HD), z(n_b, bp, 0, IDX_DIM),
                    z(n_c, bp, POOL_STATE, d),
                    z(n_d, bp, 0, D_HEADS, D_HD), z(n_d, bp, 0, D_HEADS, D_HD), z(n_d, bp, 0, D_HEADS),
                    mk_rows, mv_rows, prm)

    past_len = cache_b_k.shape[2]
    gs = _run_group(x_sample, past_len, state_a_conv, cache_b_k, cache_b_v, cache_b_kidx, state_c_pool,
                    cache_d_k, cache_d_v, cache_d_logf, cache_mem_k, cache_mem_v, prm)

    (y_p, a_p, bk_p, bv_p, bki_p, c_p, dk_p, dv_p, dlf_p) = gp
    (y_s, a_s, bk_s, bv_s, bki_s, c_s, dk_s, dv_s, dlf_s) = gs
    return (y_p, y_s, a_p, a_s, bk_p, bv_p, bki_p, bk_s, bv_s, bki_s, c_p, c_s,
            dk_p, dv_p, dlf_p, dk_s, dv_s, dlf_s, mk, mv)
```

```python
import functools
import math

import jax
import jax.numpy as jnp
import numpy as np
from jax import lax
from jax.experimental import pallas as pl
from jax.experimental.pallas import tpu as pltpu

F32 = jnp.float32
BF16 = jnp.bfloat16
I32 = jnp.int32
I16 = jnp.int16

EPS = 1e-6
NEG_INF = -1e30
LOG2E = math.log2(math.e)
CHUNK = 64
LANE = 128
VMEM_LIMIT = 48 * 1024 * 1024

ROW_TILE = 512
WIDE_ROW_TILE = 1024
FF_TILE = 1024
FOX_TILE = 4 * LANE
DSA_TILE = 2 * LANE

CONV_W = 3
POOL_WINDOWS = (2, 4, 8, 16)
POOL_STATE = max(POOL_WINDOWS) - 1
B_HEADS, B_KV, B_HD = 8, 2, 128
B_REP = B_HEADS // B_KV
IDX_HEADS, IDX_DIM = 8, 64
TOPK_MAX = 256
N_BUCKETS, MAX_DIST = 32, 128
D_HEADS, D_HD = 8, 128
MEM_HEADS = 4
B_WIDTHS = (B_HEADS * B_HD, B_KV * B_HD, B_KV * B_HD, IDX_HEADS * IDX_DIM, IDX_DIM + IDX_HEADS)
D_WIDTHS = (D_HEADS * D_HD,) * 3 + (D_HEADS,)
INT_MIN = -2147483648
BIAS_CENTER = 2 * LANE


def _cparams(*sem):
    return pltpu.CompilerParams(dimension_semantics=sem, vmem_limit_bytes=VMEM_LIMIT)


def _dot(a, b):
    return jnp.dot(a.astype(BF16), b.astype(BF16), preferred_element_type=F32)


def _dot_nt(a, b):
    return lax.dot_general(a.astype(BF16), b.astype(BF16), (((1,), (1,)), ((), ())),
                           preferred_element_type=F32)


def _dot_tn(a, b):
    return lax.dot_general(a.astype(BF16), b.astype(BF16), (((0,), (0,)), ((), ())),
                           preferred_element_type=F32)


def _rms(x, g):
    return x * lax.rsqrt(jnp.mean(x * x, axis=-1, keepdims=True) + EPS) * g


def _finish_heads(o_ref, x_ref, w_ref, l_ref, acc_ref, n_heads):
    hd = acc_ref.shape[0] // n_heads
    inv_l = 1.0 / l_ref[...]
    heads_t = jnp.concatenate([acc_ref[h * hd:(h + 1) * hd, :] * inv_l[h:h + 1, :] for h in range(n_heads)],
                              axis=0)
    o_ref[0] = x_ref[0] + _dot_tn(heads_t, w_ref[...])


def _row_tile(n, cap):
    t = min(n, cap)
    assert n % t == 0
    return t


def _memkv_kernel(mem_ref, g_ref, w_ref, k_ref, v_ref, kb_ref, vb_ref):
    bb, nm, d = mem_ref.shape
    m = mem_ref[...].reshape(bb * nm, d)
    mn = m * lax.rsqrt(jnp.mean(m * m, axis=-1, keepdims=True) + EPS)
    h = (mn * g_ref[0]).astype(BF16)
    hd = d // MEM_HEADS
    k = jnp.dot(h, w_ref[0, :, :d].astype(BF16), preferred_element_type=F32)
    v = jnp.dot(h, w_ref[0, :, d:].astype(BF16), preferred_element_type=F32)
    for i in range(bb):
        rows = slice(i * nm, (i + 1) * nm)
        kb_ref[0, i] = k[rows].astype(BF16)
        vb_ref[0, i] = v[rows].astype(BF16)
        k_ref[0, i] = pltpu.einshape("m(hd)->mhd", k[rows], d=hd)
        v_ref[0, i] = pltpu.einshape("m(hd)->mhd", v[rows], d=hd)


def _memory_kv(mem, g_mem, w_kv):
    depth, d = g_mem.shape
    b, nm, _ = mem.shape
    hd = d // MEM_HEADS
    out = jax.ShapeDtypeStruct((depth, b, nm, MEM_HEADS, hd), F32)
    out_b = jax.ShapeDtypeStruct((depth, b, nm, d), BF16)
    bb = math.gcd(b, 2)
    heads_spec = pl.BlockSpec((1, bb, nm, MEM_HEADS, hd), lambda l, i: (l, i, 0, 0, 0))
    rows_spec = pl.BlockSpec((1, bb, nm, d), lambda l, i: (l, i, 0, 0))
    return pl.pallas_call(
        _memkv_kernel,
        grid=(depth, b // bb),
        in_specs=[pl.BlockSpec((bb, nm, d), lambda l, i: (i, 0, 0)),
                  pl.BlockSpec((1, 1, d), lambda l, i: (l, 0, 0)),
                  pl.BlockSpec((1, d, 2 * d), lambda l, i: (l, 0, 0))],
        out_specs=[heads_spec, heads_spec, rows_spec, rows_spec],
        out_shape=[out, out, out_b, out_b],
        compiler_params=_cparams("parallel", "parallel"),
        name="memory_kv",
    )(mem, g_mem.reshape(depth, 1, d), w_kv)


def _ffn_kernel(x_ref, g_ref, w1_ref, w2_ref, gf_ref, o_ref, h_ref, acc_ref, *, final_norm):
    j = pl.program_id(1)

    @pl.when(j == 0)
    def _():
        h_ref[...] = _rms(x_ref[...], g_ref[...]).astype(BF16)
        acc_ref[...] = jnp.zeros_like(acc_ref)

    u = jnp.maximum(jnp.dot(h_ref[...], w1_ref[...].astype(BF16), preferred_element_type=F32), 0.0)
    acc_ref[...] += jnp.dot((u * u).astype(BF16), w2_ref[...].astype(BF16), preferred_element_type=F32)

    @pl.when(j == pl.num_programs(1) - 1)
    def _():
        y = x_ref[...] + acc_ref[...]
        o_ref[...] = _rms(y, gf_ref[...]) if final_norm else y


def _ffn(x, g, w1, w2, gf, layer, final_norm):
    n, d = x.shape
    f = w1.shape[2]
    tm = _row_tile(n, WIDE_ROW_TILE)
    tf = FF_TILE
    return pl.pallas_call(
        functools.partial(_ffn_kernel, final_norm=final_norm),
        grid=(n // tm, f // tf),
        in_specs=[pl.BlockSpec((tm, d), lambda i, j: (i, 0)),
                  pl.BlockSpec((None, 1, d), lambda i, j: (layer, 0, 0)),
                  pl.BlockSpec((None, d, tf), lambda i, j: (layer, 0, j)),
                  pl.BlockSpec((None, tf, d), lambda i, j: (layer, j, 0)),
                  pl.BlockSpec((1, d), lambda i, j: (0, 0))],
        out_specs=pl.BlockSpec((tm, d), lambda i, j: (i, 0)),
        out_shape=jax.ShapeDtypeStruct((n, d), F32),
        scratch_shapes=[pltpu.VMEM((tm, d), BF16), pltpu.VMEM((tm, d), F32)],
        compiler_params=_cparams("parallel", "arbitrary"),
        name="ffn",
    )(x, g.reshape(-1, 1, d), w1, w2, gf.reshape(1, d))


def _xattn_kernel(x_ref, g_ref, wq_ref, mk_ref, mv_ref, wo_ref, o_ref):
    x = x_ref[0]
    d = x.shape[-1]
    hd = d // MEM_HEADS
    h = _rms(x, g_ref[...]).astype(BF16)
    q = jnp.dot(h, wq_ref[...].astype(BF16), preferred_element_type=F32)
    outs = []
    if len(mk_ref.shape) == 3:
        mk = pltpu.einshape("mhd->m(hd)", mk_ref[...]).astype(BF16)
        mv = pltpu.einshape("mhd->m(hd)", mv_ref[...]).astype(BF16)
    else:
        mk, mv = mk_ref[...], mv_ref[...]
    for hh in range(MEM_HEADS):
        sl = slice(hh * hd, (hh + 1) * hd)
        kh, vh = mk[:, sl], mv[:, sl]
        s = _dot_nt(q[:, sl], kh) * (hd ** -0.5)
        m = jnp.max(s, axis=-1, keepdims=True)
        p = jnp.exp(s - m)
        l = jnp.sum(p, axis=-1, keepdims=True)
        outs.append(_dot(p, vh) / l)
    o = jnp.concatenate(outs, axis=-1)
    o_ref[0] = x + _dot(o, wo_ref[...])


def _xattn(x, g, wq, mk, mv, wo, layer):
    b, t, d = x.shape
    tm = _row_tile(t, WIDE_ROW_TILE)
    kv_spec = pl.BlockSpec((None, None) + mk.shape[2:], lambda i, j: (layer, i) + (0,) * (mk.ndim - 2))
    return pl.pallas_call(
        _xattn_kernel,
        grid=(b, t // tm),
        in_specs=[pl.BlockSpec((1, tm, d), lambda i, j: (i, j, 0)),
                  pl.BlockSpec((None, 1, d), lambda i, j: (layer, 0, 0)),
                  pl.BlockSpec((None, d, d), lambda i, j: (layer, 0, 0)),
                  kv_spec, kv_spec,
                  pl.BlockSpec((None, d, d), lambda i, j: (layer, 0, 0))],
        out_specs=pl.BlockSpec((1, tm, d), lambda i, j: (i, j, 0)),
        out_shape=jax.ShapeDtypeStruct((b, t, d), F32),
        compiler_params=_cparams("parallel", "parallel"),
        name="xattn",
    )(x, g.reshape(-1, 1, d), wq, mk, mv, wo)


def _conv_kernel(x_ref, g_ref, win_ref, wc_ref, st_ref, wout_ref, o_ref, nst_ref, z_ref):
    t = pl.program_id(1)
    x = x_ref[0]
    tm, d = x.shape
    pad = 8

    @pl.when(t == 0)
    def _():
        z_ref[pad - 2:pad, :] = st_ref[0]

    h = _rms(x, g_ref[...]).astype(BF16)
    bg = jnp.dot(h, win_ref[:, 0:d], preferred_element_type=F32)
    cg = jnp.dot(h, win_ref[:, d:2 * d], preferred_element_type=F32)
    u = jnp.dot(h, win_ref[:, 2 * d:3 * d], preferred_element_type=F32)
    z = cg * u
    z_ref[pad:pad + tm, :] = z
    conv = (z_ref[pad - 2:pad - 2 + tm, :] * wc_ref[0:1, :]
            + z_ref[pad - 1:pad - 1 + tm, :] * wc_ref[1:2, :]
            + z * wc_ref[2:3, :])
    o_ref[0] = x + _dot(bg * conv, wout_ref[...])
    last = z_ref[pad + tm - 2:pad + tm, :]
    z_ref[pad - 2:pad, :] = last

    @pl.when(t == pl.num_programs(1) - 1)
    def _():
        nst_ref[0] = last


def _conv_mixer(x, g, w_in, w_conv, state, w_out):
    b, t, d = x.shape
    tm = _row_tile(t, WIDE_ROW_TILE)
    once = pl.Buffered(1)
    return pl.pallas_call(
        _conv_kernel,
        grid=(b, t // tm),
        in_specs=[pl.BlockSpec((1, tm, d), lambda i, j: (i, j, 0)),
                  pl.BlockSpec((1, d), lambda i, j: (0, 0)),
                  pl.BlockSpec((d, 3 * d), lambda i, j: (0, 0), pipeline_mode=once),
                  pl.BlockSpec((CONV_W, d), lambda i, j: (0, 0)),
                  pl.BlockSpec((1, CONV_W - 1, d), lambda i, j: (i, 0, 0)),
                  pl.BlockSpec((d, d), lambda i, j: (0, 0), pipeline_mode=once)],
        out_specs=[pl.BlockSpec((1, tm, d), lambda i, j: (i, j, 0)),
                   pl.BlockSpec((1, CONV_W - 1, d), lambda i, j: (i, 0, 0))],
        out_shape=[jax.ShapeDtypeStruct((b, t, d), F32),
                   jax.ShapeDtypeStruct((b, CONV_W - 1, d), F32)],
        scratch_shapes=[pltpu.VMEM((tm + 8, d), F32)],
        compiler_params=_cparams("parallel", "arbitrary"),
        name="conv_mixer",
    )(x, g.reshape(1, d), w_in, w_conv, state, w_out)


def _pool_kernel(x_ref, g_ref, st_ref, wg_ref, sc_ref, o_ref, nst_ref, h_ref, *s_refs, pos0):
    t = pl.program_id(1)
    x = x_ref[0]
    tm, d = x.shape
    n_lv = len(POOL_WINDOWS)
    gw = d // n_lv
    base = 2 * (POOL_STATE + 1)
    lead = base - POOL_STATE
    end = base + tm

    @pl.when(t == 0)
    def _():
        h_ref[0:lead, :] = jnp.zeros((lead, d), F32)
        h_ref[lead:base, :] = st_ref[0]

    h = _rms(x, g_ref[...])
    h_ref[base:end, :] = h
    pos = pos0 + t * tm + lax.broadcasted_iota(I32, (tm, gw), 0)
    ys = []
    prev, c_prev = h_ref, 0
    for lv in range(1, n_lv + 1):
        w, shift, start = POOL_WINDOWS[lv - 1], 2 ** (lv - 1), 8 * lv
        c0 = (lv - 1) * gw
        cols = slice(c0 - c_prev, d - c_prev)
        cur = prev[start:end, cols] + prev[start - shift:end - shift, cols]
        if lv < n_lv:
            s_refs[lv - 1][start:end, :] = cur[:, gw:]
        win = cur[base - start:, :gw]
        count = jnp.minimum(w, pos + 1).astype(F32)
        dlt = win / count - h[:, c0:c0 + gw]
        ys.append(_dot(dlt, wg_ref[lv - 1]))
        if lv < n_lv:
            prev, c_prev = s_refs[lv - 1], c0 + gw
    y = jnp.concatenate(ys, axis=-1) * sc_ref[...]
    o_ref[0] = x + y
    last = h_ref[end - POOL_STATE:end, :]
    h_ref[lead:base, :] = last

    @pl.when(t == pl.num_programs(1) - 1)
    def _():
        nst_ref[0] = last


def _pool_mixer(x, g, state, w_group, scale, pos0):
    b, t, d = x.shape
    ng, gw, _ = w_group.shape
    tm = _row_tile(t, ROW_TILE)
    assert POOL_WINDOWS == tuple(2 ** (lv + 1) for lv in range(ng)) and tm >= POOL_STATE
    rows = tm + 2 * (POOL_STATE + 1)
    return pl.pallas_call(
        functools.partial(_pool_kernel, pos0=pos0),
        grid=(b, t // tm),
        in_specs=[pl.BlockSpec((1, tm, d), lambda i, j: (i, j, 0)),
                  pl.BlockSpec((1, d), lambda i, j: (0, 0)),
                  pl.BlockSpec((1, POOL_STATE, d), lambda i, j: (i, 0, 0)),
                  pl.BlockSpec((ng, gw, gw), lambda i, j: (0, 0, 0)),
                  pl.BlockSpec((1, d), lambda i, j: (0, 0))],
        out_specs=[pl.BlockSpec((1, tm, d), lambda i, j: (i, j, 0)),
                   pl.BlockSpec((1, POOL_STATE, d), lambda i, j: (i, 0, 0))],
        out_shape=[jax.ShapeDtypeStruct((b, t, d), F32),
                   jax.ShapeDtypeStruct((b, POOL_STATE, d), F32)],
        scratch_shapes=[pltpu.VMEM((rows, d - lv * gw), F32) for lv in range(ng)],
        compiler_params=_cparams("parallel", "arbitrary"),
        name="pool_mixer",
    )(x, g.reshape(1, d), state, w_group, scale.reshape(1, d))


def _proj_kernel(x_ref, g_ref, w_ref, e_ref, *o_refs, widths, outs):
    h = _rms(x_ref[0], g_ref[...]).astype(BF16)
    starts = [sum(widths[:i]) for i in range(len(widths))]
    ys = {}
    for (wi, mode, _, ep), o_ref in zip(outs, o_refs):
        if wi not in ys:
            w = w_ref[:, starts[wi]:starts[wi] + widths[wi]].astype(BF16)
            if widths[wi] % LANE:
                w = jnp.concatenate([w, jnp.zeros((w.shape[0], -widths[wi] % LANE), BF16)], axis=1)
            ys[wi] = jnp.dot(h, w, preferred_element_type=F32)
        y = ys[wi]
        if ep == "log_sigmoid":
            u = -(y + e_ref[...])
            y = -(jnp.maximum(u, 0.0) + jnp.log1p(jnp.exp(-jnp.abs(u))))
        if mode == "rows":
            o_ref[0] = y.astype(o_ref.dtype)
        elif isinstance(mode, tuple) and mode[0] == "first":
            o_ref[0] = y[:, :mode[1]].astype(o_ref.dtype)
        elif isinstance(mode, tuple):
            o_ref[0] = jnp.transpose(y)[:mode[1], :].astype(o_ref.dtype)
        elif mode == "t":
            o_ref[0] = jnp.transpose(y).astype(o_ref.dtype)
        else:
            o_ref[0] = pltpu.einshape("m(hd)->mhd", y.astype(o_ref.dtype), d=mode)


def _proj(x, g, w, widths, outs, extra):
    b, t, d = x.shape
    assert sum(widths) == w.shape[1]
    tm = _row_tile(t, ROW_TILE)
    in_specs = [pl.BlockSpec((1, tm, d), lambda i, j: (i, j, 0)),
                pl.BlockSpec((1, d), lambda i, j: (0, 0)),
                pl.BlockSpec(w.shape, lambda i, j: (0, 0), pipeline_mode=pl.Buffered(1)),
                pl.BlockSpec(extra.shape, lambda i, j: (0, 0))]
    out_specs, out_shape = [], []
    for wi, mode, dt, _ in outs:
        n = _round_up(widths[wi], LANE)
        if isinstance(mode, tuple):
            mode, n = ("rows" if mode[0] == "first" else "t"), mode[1]
        if mode == "rows":
            out_specs.append(pl.BlockSpec((1, tm, n), lambda i, j: (i, j, 0)))
            out_shape.append(jax.ShapeDtypeStruct((b, t, n), dt))
        elif mode == "t":
            out_specs.append(pl.BlockSpec((1, n, tm), lambda i, j: (i, 0, j)))
            out_shape.append(jax.ShapeDtypeStruct((b, n, t), dt))
        else:
            out_specs.append(pl.BlockSpec((1, tm, n // mode, mode), lambda i, j: (i, j, 0, 0)))
            out_shape.append(jax.ShapeDtypeStruct((b, t, n // mode, mode), dt))
    return pl.pallas_call(
        functools.partial(_proj_kernel, widths=tuple(widths), outs=tuple(outs)),
        grid=(b, t // tm),
        in_specs=in_specs,
        out_specs=out_specs,
        out_shape=out_shape,
        compiler_params=_cparams("parallel", "parallel"),
        name="norm_proj",
    )(x, g.reshape(1, d), w, extra)


def _outproj_kernel(x_ref, a_ref, w_ref, o_ref):
    o_ref[...] = x_ref[...] + _dot(a_ref[...], w_ref[...])


def _out_proj(x, a, w):
    n, d = x.shape
    tm = _row_tile(n, ROW_TILE)
    return pl.pallas_call(
        _outproj_kernel,
        grid=(n // tm,),
        in_specs=[pl.BlockSpec((tm, d), lambda i: (i, 0)),
                  pl.BlockSpec((tm, d), lambda i: (i, 0)),
                  pl.BlockSpec((d, d), lambda i: (0, 0))],
        out_specs=pl.BlockSpec((tm, d), lambda i: (i, 0)),
        out_shape=jax.ShapeDtypeStruct((n, d), F32),
        compiler_params=_cparams("parallel"),
        name="out_proj",
    )(x, a, w)


def _bias_table_kernel(rbt_ref, o_ref):
    width = o_ref.shape[-1]
    rel = BIAS_CENTER - lax.broadcasted_iota(I32, (1, width), 1)
    nb = N_BUCKETS // 2
    max_exact = nb // 2
    ret = (rel > 0).astype(I32) * nb
    n = jnp.abs(rel)
    nf = jnp.maximum(n, 1).astype(F32)
    large = max_exact + (jnp.log(nf / max_exact) / math.log(MAX_DIST / max_exact)
                         * (nb - max_exact)).astype(I32)
    large = jnp.minimum(large, nb - 1)
    bucket = ret + jnp.where(n < max_exact, n, large)
    acc = jnp.zeros(o_ref.shape, F32)
    for j in range(N_BUCKETS):
        acc = jnp.where(bucket == j, rbt_ref[:, j:j + 1], acc)
    o_ref[...] = acc * LOG2E


def _bias_tiles_kernel(rbt_ref, far_ref, near_ref, tab_ref):
    n_d, nh, kb, tq = near_ref.shape
    _bias_table_kernel(rbt_ref, tab_ref)
    far_ref[...] = jnp.broadcast_to(tab_ref[:, BIAS_CENTER + MAX_DIST:BIAS_CENTER + MAX_DIST + 1], far_ref.shape)
    for dd in range(n_d):
        s0 = BIAS_CENTER - (dd - 1) * kb - kb
        for h in range(nh):
            rows = jnp.broadcast_to(tab_ref[h:h + 1, s0:s0 + tq + kb], (kb, tq + kb))
            near_ref[dd, h] = pltpu.roll(rows, 0, 1, stride=1, stride_axis=0)[:, kb:]


def _bias_tiles(rel_bias, kb, tq):
    nh = rel_bias.shape[1]
    n_d = tq // kb + 1
    width = BIAS_CENTER + kb + tq + kb
    assert kb % LANE == 0 and tq % kb == 0 and kb >= MAX_DIST and BIAS_CENTER >= tq
    return pl.pallas_call(
        _bias_tiles_kernel,
        out_shape=[jax.ShapeDtypeStruct((nh, LANE), F32), jax.ShapeDtypeStruct((n_d, nh, kb, tq), F32)],
        scratch_shapes=[pltpu.VMEM((nh, width), F32)],
        name="bias_tiles",
    )(rel_bias.T)


def _sortable(x):
    x = jnp.where(x == 0.0, 0.0, x)
    bits = lax.bitcast_convert_type(x, I32)
    return jnp.where(bits < 0, bits ^ 0x7FFFFFFF, bits)


def _neg_inf_key():
    return int(np.float32(NEG_INF).view(np.int32)) ^ 0x7FFFFFFF


def _dsa_prompt_kernel(qt_ref, qit_ref, kwt_ref, kw_ref, k_ref, vt_ref, far_ref, near_ref, x_ref, wout_ref, o_ref,
                       key_ref, hi_ref, lo_ref, sel_ref, m_ref, l_ref, acc_ref, a_ref, *, top_k):
    qi = pl.program_id(1)
    tq = qt_ref.shape[2]
    n_keys = kw_ref.shape[1]
    kb_sz = LANE
    q0 = qi * tq
    nkb = jnp.minimum(n_keys, q0 + tq) // kb_sz
    negkey = _neg_inf_key()

    qlane = lax.broadcasted_iota(I32, (1, tq), 1)
    lim = ((q0 + qlane) // CHUNK + 1) * CHUNK
    krow = lax.broadcasted_iota(I32, (kb_sz, tq), 0)

    def kslice(kb):
        return pl.ds(pl.multiple_of(kb * kb_sz, kb_sz), kb_sz)

    sb = 2 * kb_sz
    srow = lax.broadcasted_iota(I32, (sb, tq), 0)

    def score_body(i, c):
        rows = pl.ds(pl.multiple_of(i * sb, sb), sb)
        kid = kw_ref[0, rows, :][:, :IDX_DIM].astype(BF16)
        sc = jnp.zeros((sb, tq), F32)
        for h in range(IDX_HEADS):
            s = jnp.dot(kid, qit_ref[0, h * IDX_DIM:(h + 1) * IDX_DIM, :], preferred_element_type=F32)
            sc = sc + jnp.maximum(s, 0.0) * kwt_ref[0, IDX_DIM + h:IDX_DIM + h + 1, :]
        sc = sc * ((IDX_DIM * IDX_HEADS) ** -0.5)
        key = jnp.where(i * sb + srow < lim, _sortable(sc), negkey)
        key_ref[rows, :] = key
        hi_ref[rows, :] = (key >> 16).astype(I16)
        lo_ref[rows, :] = ((key & 0xFFFF) - 0x8000).astype(I16)
        return c

    lax.fori_loop(0, nkb // 2, score_body, 0)

    def search16(ref):
        def bit_body(i, t_u):
            cand_u = t_u | jnp.left_shift(jnp.int32(1), 15 - i)
            cand = (cand_u - 0x8000).astype(I16)

            def body(j, a):
                ind = jnp.where(ref[pl.ds(pl.multiple_of(j * sb, sb), sb), :] >= cand,
                                jnp.ones((), I16), jnp.zeros((), I16))
                parts = [ind[16 * r:16 * (r + 1), :] for r in range(sb // 16)]
                while len(parts) > 1:
                    parts = [parts[r] + parts[r + 1] for r in range(0, len(parts), 2)]
                return a + parts[0]
            a = lax.fori_loop(0, nkb // 2, body, jnp.zeros((16, tq), I16))
            cnt = jnp.sum(a.astype(I32), axis=0, keepdims=True)
            return jnp.where(cnt >= top_k, cand_u, t_u)
        return lax.fori_loop(0, 16, bit_body, jnp.zeros((1, tq), I32))

    def count(pred_fn):
        def body(i, a):
            for u in range(2):
                kb = 2 * i + u
                ind = pred_fn(kb, key_ref[kslice(kb), :])
                a = a + jnp.sum(ind.reshape(kb_sz // 8, 8, tq), axis=0)
            return a
        a = lax.fori_loop(0, nkb // 2, body, jnp.zeros((8, tq), I32))
        return jnp.sum(a, axis=0, keepdims=True)

    t_hi = search16(hi_ref)
    t_hi16 = (t_hi - 0x8000).astype(I16)

    def lo_body(j, c):
        rows = pl.ds(pl.multiple_of(j * sb, sb), sb)
        hi = hi_ref[rows, :]
        lo_ref[rows, :] = jnp.where(hi == t_hi16, lo_ref[rows, :],
                                    jnp.where(hi > t_hi16, jnp.full((), 0x7FFF, I16), jnp.full((), -0x8000, I16)))
        return c

    lax.fori_loop(0, nkb // 2, lo_body, 0)
    t_s = (jnp.left_shift(t_hi, 16) | search16(lo_ref)) ^ INT_MIN

    def adm01(kb):
        return jnp.where(kb * kb_sz + krow < lim, 1.0, 0.0)

    def cap(sel):
        return jnp.where(sel != 0.0, jnp.inf, NEG_INF)

    def sel_body(kb, a):
        sel = jnp.where(key_ref[kslice(kb), :] >= t_s, adm01(kb), 0.0)
        sel_ref[kslice(kb), :] = cap(sel)
        return a + jnp.sum(sel.reshape(kb_sz // 8, 8, tq), axis=0)

    n_sel = jnp.sum(lax.fori_loop(0, nkb, sel_body, jnp.zeros((8, tq), F32)), axis=0, keepdims=True)

    @pl.when(jnp.max(n_sel) > top_k)
    def _():
        n_gt = count(lambda kb, key: jnp.where(key > t_s, 1, 0))
        need = (top_k - n_gt).astype(F32)
        r = lax.broadcasted_iota(I32, (kb_sz, kb_sz), 0)
        c = lax.broadcasted_iota(I32, (kb_sz, kb_sz), 1)
        ltri = jnp.where(c < r, 1.0, 0.0).astype(BF16)

        def tie_body(kb, carry):
            key = key_ref[kslice(kb), :]
            adm = adm01(kb)
            eq = jnp.where(key == t_s, adm, 0.0)
            rank = carry + jnp.dot(ltri, eq.astype(BF16), preferred_element_type=F32)
            keep = jnp.where(rank < need, eq, 0.0)
            sel_ref[kslice(kb), :] = cap(jnp.where(key > t_s, adm, keep))
            return carry + jnp.sum(eq, axis=0, keepdims=True)

        lax.fori_loop(0, nkb, tie_body, jnp.zeros((1, tq), F32))

    m_ref[...] = jnp.full(m_ref.shape, NEG_INF, F32)
    l_ref[...] = jnp.zeros(l_ref.shape, F32)
    acc_ref[...] = jnp.zeros(acc_ref.shape, F32)
    c1 = (B_HD ** -0.5) * LOG2E

    def attend(k0, nk, bias2_fn):
        rows = pl.ds(pl.multiple_of(k0, LANE), nk)
        sel_cap = sel_ref[rows, :]
        ks = k_ref[0, rows, :]
        cols = []
        for h in range(B_HEADS):
            g = h // B_REP
            z = jnp.dot(ks[:, g * B_HD:(g + 1) * B_HD], qt_ref[0, h * B_HD:(h + 1) * B_HD, :],
                        preferred_element_type=F32)
            a = jnp.minimum(z * c1 + bias2_fn(h), sel_cap)
            a_ref[h, 0:nk, :] = a
            cols.append(jnp.max(a, axis=0, keepdims=True))
        m_old = m_ref[...]
        m_new = jnp.maximum(m_old, jnp.concatenate(cols, axis=0))
        alpha = jnp.exp2(m_old - m_new)
        m_ref[...] = m_new
        sums = []
        for h in range(B_HEADS):
            g = h // B_REP
            p = jnp.exp2(a_ref[h, 0:nk, :] - m_new[h:h + 1, :])
            sums.append(jnp.sum(p, axis=0, keepdims=True))
            hs = slice(h * B_HD, (h + 1) * B_HD)
            pv = jnp.dot(vt_ref[0, g * B_HD:(g + 1) * B_HD, rows], p.astype(BF16),
                         preferred_element_type=F32)
            acc_ref[hs, :] = alpha[h:h + 1, :] * acc_ref[hs, :] + pv
        l_ref[...] = alpha * l_ref[...] + jnp.concatenate(sums, axis=0)

    ab = near_ref.shape[2]
    n_far = jnp.maximum(q0 // ab - 1, 0)
    far_bias2 = far_ref[:, 0:1]

    def far_body(i, c):
        attend(i * ab, ab, lambda h: far_bias2[h:h + 1, :])
        return c

    lax.fori_loop(0, n_far, far_body, 0)

    def near_body(i, c):
        dd = i - q0 // ab + 1
        attend(i * ab, ab, lambda h: near_ref[dd, h])
        return c

    lax.fori_loop(n_far, nkb * kb_sz // ab, near_body, 0)

    _finish_heads(o_ref, x_ref, wout_ref, l_ref, acc_ref, B_HEADS)


def _dsa_attention_prompt(qt, qit, kwt, kw, k, vt, far, near, x, w_out):
    b, d, t = qt.shape
    tq = near.shape[3]
    top_k = min(TOPK_MAX, t // 4)
    ab = near.shape[2]
    assert t % tq == 0 and tq % CHUNK == 0 and tq % ab == 0 and ab % LANE == 0 and tq % (2 * LANE) == 0
    return pl.pallas_call(
        functools.partial(_dsa_prompt_kernel, top_k=top_k),
        grid=(b, t // tq),
        in_specs=[pl.BlockSpec((1, d, tq), lambda i, j: (i, 0, j)),
                  pl.BlockSpec((1, qit.shape[1], tq), lambda i, j: (i, 0, j)),
                  pl.BlockSpec((1, LANE, tq), lambda i, j: (i, 0, j)),
                  pl.BlockSpec((1, t, LANE), lambda i, j: (i, 0, 0)),
                  pl.BlockSpec((1, t, B_KV * B_HD), lambda i, j: (i, 0, 0)),
                  pl.BlockSpec((1, B_KV * B_HD, t), lambda i, j: (i, 0, 0)),
                  pl.BlockSpec(far.shape, lambda i, j: (0, 0)),
                  pl.BlockSpec(near.shape, lambda i, j: (0, 0, 0, 0)),
                  pl.BlockSpec((1, tq, d), lambda i, j: (i, j, 0)),
                  pl.BlockSpec((d, d), lambda i, j: (0, 0))],
        out_specs=pl.BlockSpec((1, tq, d), lambda i, j: (i, j, 0)),
        out_shape=jax.ShapeDtypeStruct((b, t, d), F32),
        scratch_shapes=[pltpu.VMEM((t, tq), I32), pltpu.VMEM((t, tq), I16), pltpu.VMEM((t, tq), I16),
                        pltpu.VMEM((t, tq), F32),
                        pltpu.VMEM((B_HEADS, tq), F32), pltpu.VMEM((B_HEADS, tq), F32),
                        pltpu.VMEM((d, tq), F32), pltpu.VMEM((B_HEADS, ab, tq), F32)],
        compiler_params=_cparams("parallel", "parallel"),
        name="dsa_attention",
    )(qt, qit, kwt, kw, k, vt, far, near, x, w_out)


def _dsa_cached_kernel(q_ref, qi_ref, wi_ref, rb_ref, kidx_ref, kp_ref, vp_ref, kn_ref, vn_ref, o_ref,
                       *, past, t_new, top_k):
    n_keys = past + t_new
    lp = kidx_ref.shape[1]
    negkey = _neg_inf_key()
    kpos = lax.broadcasted_iota(I32, (t_new, lp), 1)
    qpos = past + lax.broadcasted_iota(I32, (t_new, lp), 0)
    adm = kpos < (qpos // CHUNK + 1) * CHUNK

    s = _dot_nt(qi_ref[0], kidx_ref[0])
    w = jnp.maximum(s, 0.0) * wi_ref[0]
    sc = w[0:t_new]
    for h in range(1, IDX_HEADS):
        sc = sc + w[h * t_new:(h + 1) * t_new]
    sc = sc * ((IDX_DIM * IDX_HEADS) ** -0.5)
    key = jnp.where(adm, _sortable(sc), negkey)
    key = jnp.where(kpos < n_keys, key, INT_MIN)

    def bit_body(i, t_u):
        cand_u = t_u | jnp.left_shift(jnp.int32(1), 31 - i)
        cnt = jnp.sum(jnp.where(key >= (cand_u ^ INT_MIN), 1.0, 0.0), axis=1, keepdims=True)
        return jnp.where(cnt >= top_k, cand_u, t_u)

    t_s = lax.fori_loop(0, 32, bit_body, jnp.zeros((t_new, 1), I32)) ^ INT_MIN

    adm01 = jnp.where(adm, 1.0, 0.0)
    gt = jnp.where(key > t_s, adm01, 0.0)
    eq = jnp.where(key == t_s, adm01, 0.0)
    need = top_k - jnp.sum(jnp.where(key > t_s, 1.0, 0.0), axis=1, keepdims=True)
    r = lax.broadcasted_iota(I32, (LANE, LANE), 0)
    c = lax.broadcasted_iota(I32, (LANE, LANE), 1)
    utri = jnp.where(r < c, 1.0, 0.0).astype(BF16)
    carry = jnp.zeros((t_new, 1), F32)
    keeps = []
    for blk in range(lp // LANE):
        e = eq[:, blk * LANE:(blk + 1) * LANE]
        rank = carry + jnp.dot(e.astype(BF16), utri, preferred_element_type=F32)
        keeps.append(jnp.where(rank < need, e, 0.0))
        carry = carry + jnp.sum(e, axis=1, keepdims=True)
    sel = gt + jnp.concatenate(keeps, axis=1)

    near = max(past - MAX_DIST, 0) // LANE * LANE
    rel = (kpos - qpos)[:, near:]
    nb = N_BUCKETS // 2
    max_exact = nb // 2
    n = jnp.abs(rel)
    nf = jnp.maximum(n, 1).astype(F32)
    large = max_exact + (jnp.log(nf / max_exact) / math.log(MAX_DIST / max_exact)
                         * (nb - max_exact)).astype(I32)
    bucket = (rel > 0).astype(I32) * nb + jnp.where(n < max_exact, n, jnp.minimum(large, nb - 1))

    rows = B_REP * t_new
    sel_g = jnp.concatenate([sel] * B_REP, axis=0) != 0.0
    bucket_g = jnp.concatenate([bucket] * B_REP, axis=0)
    for g in range(B_KV):
        grp = lambda ref, n: ref[0, pl.ds(g, n, stride=B_KV), :]
        qg = q_ref[0, g * rows:(g + 1) * rows, :]
        rb = rb_ref[g * rows:(g + 1) * rows, :]
        bias_near = jnp.zeros((rows, lp - near), F32)
        for j in range(N_BUCKETS):
            bias_near = jnp.where(bucket_g == j, rb[:, j:j + 1], bias_near)
        bias = jnp.concatenate([jnp.broadcast_to(rb[:, nb - 1:nb], (rows, near)), bias_near], axis=1)
        zp = _dot_nt(qg, grp(kp_ref, past)) * (B_HD ** -0.5)
        zn = _dot_nt(qg, grp(kn_ref, t_new)) * (B_HD ** -0.5)
        ap = jnp.where(sel_g[:, :past], zp + bias[:, :past], NEG_INF)
        an = jnp.where(sel_g[:, past:n_keys], zn + bias[:, past:n_keys], NEG_INF)
        m = jnp.maximum(jnp.max(ap, axis=1, keepdims=True), jnp.max(an, axis=1, keepdims=True))
        pp, pn = jnp.exp(ap - m), jnp.exp(an - m)
        l = jnp.sum(pp, axis=1, keepdims=True) + jnp.sum(pn, axis=1, keepdims=True)
        o_ref[0, g * rows:(g + 1) * rows, :] = (_dot(pp, grp(vp_ref, past)) + _dot(pn, grp(vn_ref, t_new))) / l


def _dsa_attention_cached(q_rows, qi_rows, wi_col, rb_rows, kidx_all, k_past, v_past, k_new, v_new):
    b, rows, hd = q_rows.shape
    past, t_new = k_past.shape[1], k_new.shape[1]
    lp = kidx_all.shape[1]
    top_k = min(TOPK_MAX, (past + t_new) // 4)
    assert past % LANE == 0
    flat = lambda a: a.reshape(b, a.shape[1] * B_KV, B_HD)
    kv_spec = lambda n: pl.BlockSpec((1, n * B_KV, B_HD), lambda i: (i, 0, 0))
    return pl.pallas_call(
        functools.partial(_dsa_cached_kernel, past=past, t_new=t_new, top_k=top_k),
        grid=(b,),
        in_specs=[pl.BlockSpec((1, rows, hd), lambda i: (i, 0, 0)),
                  pl.BlockSpec((1,) + qi_rows.shape[1:], lambda i: (i, 0, 0)),
                  pl.BlockSpec((1,) + wi_col.shape[1:], lambda i: (i, 0, 0)),
                  pl.BlockSpec(rb_rows.shape, lambda i: (0, 0)),
                  pl.BlockSpec((1, lp, IDX_DIM), lambda i: (i, 0, 0)),
                  kv_spec(past), kv_spec(past), kv_spec(t_new), kv_spec(t_new)],
        out_specs=pl.BlockSpec((1, rows, hd), lambda i: (i, 0, 0)),
        out_shape=jax.ShapeDtypeStruct((b, rows, hd), F32),
        compiler_params=_cparams("parallel"),
        name="dsa_attention_cached",
    )(q_rows, qi_rows, wi_col, rb_rows, kidx_all, flat(k_past), flat(v_past), flat(k_new), flat(v_new))


def _cumsum_kernel(x_ref, o_ref):
    x = x_ref[0]
    n = x.shape[-1]
    lane = lax.broadcasted_iota(I32, x.shape, 1)
    s = 1
    while s < n:
        x = x + jnp.where(lane >= s, pltpu.roll(x, s, 1), 0.0)
        s *= 2
    o_ref[0] = x


def _cumsum_lanes(x):
    b, h, n = x.shape
    return pl.pallas_call(
        _cumsum_kernel,
        grid=(b,),
        in_specs=[pl.BlockSpec((1, h, n), lambda i: (i, 0, 0))],
        out_specs=pl.BlockSpec((1, h, n), lambda i: (i, 0, 0)),
        out_shape=jax.ShapeDtypeStruct((b, h, n), F32),
        compiler_params=_cparams("parallel"),
        name="logf_cumsum",
    )(x)


def _fox_init(m_ref, l_ref, acc_ref):
    m_ref[...] = jnp.full(m_ref.shape, NEG_INF, F32)
    l_ref[...] = jnp.zeros(l_ref.shape, F32)
    acc_ref[...] = jnp.zeros(acc_ref.shape, F32)


def _fox_tile(z_fn, pv_fn, cq, ck, mask, m_ref, l_ref, acc_ref, a_ref):
    c1 = (D_HD ** -0.5) * LOG2E
    cq2, ck2 = cq * LOG2E, ck * LOG2E
    cols = []
    for h in range(D_HEADS):
        a = z_fn(h) * c1 - ck2[:, h:h + 1]
        if mask is not None:
            a = jnp.where(mask, a, NEG_INF)
        a_ref[h] = a
        cols.append(jnp.max(a, axis=0, keepdims=True))
    m_old = m_ref[...]
    m_new = jnp.maximum(m_old, jnp.concatenate(cols, axis=0) + cq2)
    alpha = jnp.exp2(m_old - m_new)
    shift = m_new - cq2
    m_ref[...] = m_new
    sums = []
    for h in range(D_HEADS):
        p = jnp.exp2(a_ref[h] - shift[h:h + 1, :])
        sums.append(jnp.sum(p, axis=0, keepdims=True))
        hs = slice(h * D_HD, (h + 1) * D_HD)
        acc_ref[hs, :] = alpha[h:h + 1, :] * acc_ref[hs, :] + pv_fn(h, p.astype(BF16))
    l_ref[...] = alpha * l_ref[...] + jnp.concatenate(sums, axis=0)


def _hs(h):
    return slice(h * D_HD, (h + 1) * D_HD)


def _fox_prompt_kernel(qt_ref, k_ref, vt_ref, cq_ref, ck_ref, x_ref, wout_ref, o_ref,
                       m_ref, l_ref, acc_ref, a_ref):
    qi, step = pl.program_id(1), pl.program_id(2)
    tq, tk = qt_ref.shape[2], a_ref.shape[1]
    q0 = qi * tq
    ki = step - (pl.num_programs(2) - 1 - (q0 + tq - 1) // tk)
    k0 = ki * tk
    keys = pl.ds(pl.multiple_of(jnp.maximum(k0, 0), tk), tk)

    @pl.when(step == 0)
    def _():
        _fox_init(m_ref, l_ref, acc_ref)

    def run(masked):
        mask = None
        if masked:
            mask = (k0 + lax.broadcasted_iota(I32, (tk, tq), 0)) <= (q0 + lax.broadcasted_iota(I32, (tk, tq), 1))
        _fox_tile(lambda h: jnp.dot(k_ref[0, keys, _hs(h)], qt_ref[0, _hs(h), :], preferred_element_type=F32),
                  lambda h, p: jnp.dot(vt_ref[0, _hs(h), keys], p, preferred_element_type=F32),
                  cq_ref[0], ck_ref[0, keys, :], mask, m_ref, l_ref, acc_ref, a_ref)

    fully_visible = k0 + tk - 1 <= q0
    pl.when(jnp.logical_and(ki >= 0, fully_visible))(lambda: run(False))
    pl.when(jnp.logical_and(ki >= 0, jnp.logical_not(fully_visible)))(lambda: run(True))

    @pl.when(step == pl.num_programs(2) - 1)
    def _():
        _finish_heads(o_ref, x_ref, wout_ref, l_ref, acc_ref, D_HEADS)


def _fox_attention_prompt(qt, k, vt, cum_t, cum, x, w_out, tq, tk):
    b, d, t = qt.shape
    nq, nk = t // tq, t // tk
    return pl.pallas_call(
        _fox_prompt_kernel,
        grid=(b, nq, nk),
        in_specs=[pl.BlockSpec((1, d, tq), lambda i, j, kk: (i, 0, j)),
                  pl.BlockSpec((1, t, d), lambda i, j, kk: (i, 0, 0)),
                  pl.BlockSpec((1, d, t), lambda i, j, kk: (i, 0, 0)),
                  pl.BlockSpec((1, D_HEADS, tq), lambda i, j, kk: (i, 0, j)),
                  pl.BlockSpec((1, t, D_HEADS), lambda i, j, kk: (i, 0, 0)),
                  pl.BlockSpec((1, tq, d), lambda i, j, kk: (i, j, 0)),
                  pl.BlockSpec((d, d), lambda i, j, kk: (0, 0))],
        out_specs=pl.BlockSpec((1, tq, d), lambda i, j, kk: (i, j, 0)),
        out_shape=jax.ShapeDtypeStruct((b, t, d), F32),
        scratch_shapes=[pltpu.VMEM((D_HEADS, tq), F32), pltpu.VMEM((D_HEADS, tq), F32),
                        pltpu.VMEM((d, tq), F32), pltpu.VMEM((D_HEADS, tk, tq), F32)],
        compiler_params=_cparams("parallel", "parallel", "arbitrary"),
        name="fox_attention",
    )(qt, k, vt, cum_t, cum, x, w_out)


def _fox_cached_kernel(q_ref, kp_ref, vp_ref, kn_ref, vn_ref, cq_ref, ckp_ref, ckn_ref, o_ref,
                       m_ref, l_ref, acc_ref, *, t_new):
    ki = pl.program_id(1)
    n_past = pl.num_programs(1) - 1
    rows = q_ref.shape[1]
    c1 = (D_HD ** -0.5) * LOG2E

    @pl.when(ki == 0)
    def _():
        m_ref[...] = jnp.full(m_ref.shape, NEG_INF, F32)
        l_ref[...] = jnp.zeros(l_ref.shape, F32)
        acc_ref[...] = jnp.zeros(acc_ref.shape, F32)

    def tile(k2d, v2d, ck_row, causal):
        cols = k2d.shape[0]
        a = _dot_nt(q_ref[0], k2d) * c1 - ck_row * LOG2E
        r = lax.broadcasted_iota(I32, (rows, cols), 0)
        c = lax.broadcasted_iota(I32, (rows, cols), 1)
        ok = (c % D_HEADS) == (r // t_new)
        if causal:
            ok = jnp.logical_and(ok, (c // D_HEADS) <= (r % t_new))
        a = jnp.where(ok, a, NEG_INF)
        cq2 = cq_ref[0] * LOG2E
        m_old = m_ref[...]
        m_new = jnp.maximum(m_old, jnp.max(a, axis=1, keepdims=True) + cq2)
        alpha = jnp.exp2(m_old - m_new)
        p = jnp.exp2(a - (m_new - cq2))
        l_ref[...] = alpha * l_ref[...] + jnp.sum(p, axis=1, keepdims=True)
        acc_ref[...] = alpha * acc_ref[...] + _dot(p, v2d)
        m_ref[...] = m_new

    @pl.when(ki < n_past)
    def _():
        tk = kp_ref.shape[1]
        tile(kp_ref[0].reshape(tk * D_HEADS, D_HD), vp_ref[0].reshape(tk * D_HEADS, D_HD), ckp_ref[0], False)

    @pl.when(ki == n_past)
    def _():
        tile(kn_ref[0].reshape(t_new * D_HEADS, D_HD), vn_ref[0].reshape(t_new * D_HEADS, D_HD), ckn_ref[0], True)
        o_ref[0] = acc_ref[...] / l_ref[...]


def _fox_attention_cached(q_rows, k_past, v_past, k_new, v_new, cq_col, ck_past, ck_new, tk):
    b, rows, hd = q_rows.shape
    past, t_new = k_past.shape[1], k_new.shape[1]
    n_past = past // tk
    pidx = lambda i, kk: (i, jnp.minimum(kk, n_past - 1), 0, 0)
    return pl.pallas_call(
        functools.partial(_fox_cached_kernel, t_new=t_new),
        grid=(b, n_past + 1),
        in_specs=[pl.BlockSpec((1, rows, hd), lambda i, kk: (i, 0, 0)),
                  pl.BlockSpec((1, tk, D_HEADS, D_HD), pidx),
                  pl.BlockSpec((1, tk, D_HEADS, D_HD), pidx),
                  pl.BlockSpec((1, t_new, D_HEADS, D_HD), lambda i, kk: (i, 0, 0, 0)),
                  pl.BlockSpec((1, t_new, D_HEADS, D_HD), lambda i, kk: (i, 0, 0, 0)),
                  pl.BlockSpec((1, rows, 1), lambda i, kk: (i, 0, 0)),
                  pl.BlockSpec((1, 1, tk * D_HEADS), lambda i, kk: (i, 0, jnp.minimum(kk, n_past - 1))),
                  pl.BlockSpec((1, 1, t_new * D_HEADS), lambda i, kk: (i, 0, 0))],
        out_specs=pl.BlockSpec((1, rows, hd), lambda i, kk: (i, 0, 0)),
        out_shape=jax.ShapeDtypeStruct((b, rows, hd), F32),
        scratch_shapes=[pltpu.VMEM((rows, 1), F32), pltpu.VMEM((rows, 1), F32), pltpu.VMEM((rows, hd), F32)],
        compiler_params=_cparams("parallel", "arbitrary"),
        name="fox_attention_cached",
    )(q_rows, k_past, v_past, k_new, v_new, cq_col, ck_past, ck_new)


def _pad_rows(a, rows):
    if a.shape[1] == rows:
        return a
    return jnp.pad(a, ((0, 0), (0, rows - a.shape[1])) + ((0, 0),) * (a.ndim - 2))


def _round_up(n, m):
    return -(-n // m) * m


def _dsa_mixer(x, g, k_past, v_past, ki_past, w, w_out, rel_bias):
    b, t, d = x.shape
    past = k_past.shape[1]
    n_keys = past + t
    if past == 0:
        qt, k4, kb, v4, vt, qit, kw, kwt, ki_t = _proj(
            x, g, w, B_WIDTHS, ((0, "t", BF16, None), (1, B_HD, F32, None), (1, "rows", BF16, None),
                      (2, B_HD, F32, None), (2, "t", BF16, None), (3, "t", BF16, None),
                      (4, "rows", F32, None), (4, "t", F32, None), (4, ("first_t", IDX_DIM), F32, None)),
            jnp.zeros((1, LANE), F32))
        far, near = _bias_tiles(rel_bias, DSA_TILE, DSA_TILE)
        y = _dsa_attention_prompt(qt, qit, kwt, kw, kb, vt, far, near, x, w_out)
        return (y, k4, v4, jnp.swapaxes(ki_t, 1, 2))
    q, k4, v4, qidx, kw = _proj(
        x, g, w, B_WIDTHS, ((0, "rows", BF16, None), (1, B_HD, F32, None), (2, B_HD, F32, None),
                  (3, "rows", BF16, None), (4, "rows", F32, None)),
        jnp.zeros((1, LANE), F32))
    ki = kw[:, :, :IDX_DIM]
    to_rows = lambda a, nh: jnp.swapaxes(a.reshape(b, t, nh, -1), 1, 2).reshape(b, nh * t, -1)
    kidx_all = _pad_rows(jnp.concatenate([ki_past, ki], axis=1), _round_up(n_keys, LANE))
    o = _dsa_attention_cached(to_rows(q, B_HEADS), to_rows(qidx, IDX_HEADS),
                              to_rows(kw[:, :, IDX_DIM:IDX_DIM + IDX_HEADS], IDX_HEADS),
                              jnp.repeat(rel_bias.T, t, axis=0), kidx_all, k_past, v_past, k4, v4)
    o = jnp.swapaxes(o.reshape(b, B_HEADS, t, B_HD), 1, 2).reshape(b * t, d)
    y = _out_proj(x.reshape(b * t, d), o, w_out).reshape(b, t, d)
    return (y, k4, v4, ki)


def _fox_mixer(x, g, k_past, v_past, lf_past, w, b_f, w_out):
    b, t, d = x.shape
    past = k_past.shape[1]
    heads4 = ((1, D_HD, F32, None), (2, D_HD, F32, None))
    if past == 0:
        tq = tk = min(FOX_TILE, t)
        k4, v4, logf_t, qt, kb, vt = _proj(
            x, g, w, D_WIDTHS, heads4 + ((3, ("first_t", D_HEADS), F32, "log_sigmoid"), (0, "t", BF16, None),
                               (1, "rows", BF16, None), (2, "t", BF16, None)), b_f)
        cum_t = _cumsum_lanes(logf_t)
        y = _fox_attention_prompt(qt, kb, vt, cum_t, jnp.swapaxes(cum_t, 1, 2), x, w_out, tq, tk)
        return (y, k4, v4, jnp.swapaxes(logf_t, 1, 2))
    else:
        tk = math.gcd(past, FOX_TILE)
        k4, v4, logf, q = _proj(
            x, g, w, D_WIDTHS, heads4 + ((3, ("first", D_HEADS), F32, "log_sigmoid"), (0, "rows", BF16, None)), b_f)
        lf_all = _pad_rows(jnp.concatenate([lf_past, logf], axis=1), _round_up(past + t, LANE))
        cum = jnp.swapaxes(_cumsum_lanes(jnp.swapaxes(lf_all, 1, 2)), 1, 2)[:, :past + t]
        ck = cum.reshape(b, 1, (past + t) * D_HEADS)
        to_rows = lambda a: jnp.swapaxes(a.reshape(b, t, D_HEADS, -1), 1, 2).reshape(b, D_HEADS * t, -1)
        o = _fox_attention_cached(to_rows(q), k_past, v_past, k4, v4, to_rows(cum[:, past:]),
                                  ck[:, :, :past * D_HEADS], ck[:, :, past * D_HEADS:], tk)
        o = jnp.swapaxes(o.reshape(b, D_HEADS, t, D_HD), 1, 2).reshape(b, t, d)
    y = _out_proj(x.reshape(b * t, d), o.reshape(b * t, d), w_out).reshape(b, t, d)
    return (y, k4, v4, logf)


def _run_group(x, pos0, a_st, b_k, b_v, b_ki, c_st, d_k, d_v, d_lf, mem_k, mem_v, prm):
    b, t, d = x.shape
    depth = prm["norm_mix"].shape[0]
    new = {n: [] for n in ("a", "bk", "bv", "bki", "c", "dk", "dv", "dlf")}
    for i in range(depth):
        kind, j = i % 4, i // 4
        g = prm["norm_mix"][i]
        if kind == 0:
            x, st = _conv_mixer(x, g, prm["a_w_in"][j], prm["a_conv"][j], a_st[j], prm["a_w_out"][j])
            new["a"].append(st)
        elif kind == 1:
            x, kk, vv, ki = _dsa_mixer(x, g, b_k[j], b_v[j], b_ki[j], prm["b_w"][j], prm["b_w_out"][j],
                                       prm["rel_bias"])
            new["bk"].append(kk); new["bv"].append(vv); new["bki"].append(ki)
        elif kind == 2:
            x, st = _pool_mixer(x, g, c_st[j], prm["c_w_group"][j], prm["c_scale"][j], pos0)
            new["c"].append(st)
        else:
            x, kk, vv, lf = _fox_mixer(x, g, d_k[j], d_v[j], d_lf[j], prm["d_w"][j], prm["d_b_f"][j],
                                       prm["d_w_out"][j])
            new["dk"].append(kk); new["dv"].append(vv); new["dlf"].append(lf)
        x = _xattn(x, prm["norm_xattn"], prm["xa_wq"], mem_k, mem_v, prm["xa_wo"], i)
        last = i == depth - 1
        x = _ffn(x.reshape(b * t, d), prm["norm_ffn"], prm["ffn_w1"], prm["ffn_w2"],
                 prm["final_norm"], i, last).reshape(b, t, d)
    return (x,) + tuple(jnp.stack(new[n]) for n in ("a", "bk", "bv", "bki", "c", "dk", "dv", "dlf"))


def kernel(x_prompt, x_sample, state_a_conv, cache_b_k, cache_b_v, cache_b_kidx, state_c_pool,
           cache_d_k, cache_d_v, cache_d_logf, cache_mem_k, cache_mem_v, mem_prompt,
           norm_mix, norm_xattn, norm_mem, norm_ffn, final_norm,
           a_w_in, a_conv, a_w_out, b_w_in, b_w_out, rel_bias, c_w_group, c_scale,
           d_w_in, d_b_f, d_w_out, xa_wq, xa_wkv, xa_wo, ffn_w1, ffn_w2):
    bp, _, d = x_prompt.shape
    n_b, n_d = b_w_in.shape[0], d_w_in.shape[0]
    bf = lambda w: w.astype(BF16)
    d_bf = [jnp.pad(d_b_f[j], (0, LANE - D_HEADS)).reshape(1, LANE) for j in range(n_d)]

    prm = {"norm_mix": norm_mix, "norm_xattn": norm_xattn, "norm_ffn": norm_ffn, "final_norm": final_norm,
           "a_w_in": bf(a_w_in), "a_conv": a_conv, "a_w_out": bf(a_w_out),
           "b_w": b_w_in, "b_w_out": bf(b_w_out), "rel_bias": rel_bias,
           "c_w_group": bf(c_w_group), "c_scale": c_scale,
           "d_w": d_w_in, "d_b_f": d_bf, "d_w_out": bf(d_w_out),
           "xa_wq": xa_wq, "xa_wo": xa_wo, "ffn_w1": ffn_w1, "ffn_w2": ffn_w2}

    mk, mv, mk_rows, mv_rows = _memory_kv(mem_prompt, norm_mem, xa_wkv)

    n_a, n_c = a_w_in.shape[0], c_w_group.shape[0]
    z = lambda *s: jnp.zeros(s, F32)
    gp = _run_group(x_prompt, 0,
                    z(n_a, bp, CONV_W - 1, d),
                    z(n_b, bp, 0, B_KV, B_HD), z(n_b, bp, 0, B_KV, B_HD), z(n_b, bp, 0, IDX_DIM),
                    z(n_c, bp, POOL_STATE, d),
                    z(n_d, bp, 0, D_HEADS, D_HD), z(n_d, bp, 0, D_HEADS, D_HD), z(n_d, bp, 0, D_HEADS),
                    mk_rows, mv_rows, prm)

    past_len = cache_b_k.shape[2]
    gs = _run_group(x_sample, past_len, state_a_conv, cache_b_k, cache_b_v, cache_b_kidx, state_c_pool,
                    cache_d_k, cache_d_v, cache_d_logf, cache_mem_k, cache_mem_v, prm)

    (y_p, a_p, bk_p, bv_p, bki_p, c_p, dk_p, dv_p, dlf_p) = gp
    (y_s, a_s, bk_s, bv_s, bki_s, c_s, dk_s, dv_s, dlf_s) = gs
    return (y_p, y_s, a_p, a_s, bk_p, bv_p, bki_p, bk_s, bv_s, bki_s, c_p, c_s,
            dk_p, dv_p, dlf_p, dk_s, dv_s, dlf_s, mk, mv)
```

```python
import functools
import math

import jax
import jax.numpy as jnp
import numpy as np
from jax import lax
from jax.experimental import pallas as pl
from jax.experimental.pallas import tpu as pltpu

F32 = jnp.float32
BF16 = jnp.bfloat16
I32 = jnp.int32
I16 = jnp.int16

EPS = 1e-6
NEG_INF = -1e30
LOG2E = math.log2(math.e)
CHUNK = 64
LANE = 128
VMEM_LIMIT = 48 * 1024 * 1024

ROW_TILE = 512
WIDE_ROW_TILE = 1024
FF_TILE = 1024
FOX_TILE = 4 * LANE
DSA_TILE = 2 * LANE

CONV_W = 3
POOL_WINDOWS = (2, 4, 8, 16)
POOL_STATE = max(POOL_WINDOWS) - 1
B_HEADS, B_KV, B_HD = 8, 2, 128
B_REP = B_HEADS // B_KV
IDX_HEADS, IDX_DIM = 8, 64
TOPK_MAX = 256
N_BUCKETS, MAX_DIST = 32, 128
D_HEADS, D_HD = 8, 128
MEM_HEADS = 4
B_WIDTHS = (B_HEADS * B_HD, B_KV * B_HD, B_KV * B_HD, IDX_HEADS * IDX_DIM, IDX_DIM + IDX_HEADS)
D_WIDTHS = (D_HEADS * D_HD,) * 3 + (D_HEADS,)
INT_MIN = -2147483648
BIAS_CENTER = 2 * LANE


def _cparams(*sem):
    return pltpu.CompilerParams(dimension_semantics=sem, vmem_limit_bytes=VMEM_LIMIT)


def _dot(a, b):
    return jnp.dot(a.astype(BF16), b.astype(BF16), preferred_element_type=F32)


def _dot_nt(a, b):
    return lax.dot_general(a.astype(BF16), b.astype(BF16), (((1,), (1,)), ((), ())),
                           preferred_element_type=F32)


def _dot_tn(a, b):
    return lax.dot_general(a.astype(BF16), b.astype(BF16), (((0,), (0,)), ((), ())),
                           preferred_element_type=F32)


def _rms(x, g):
    return x * lax.rsqrt(jnp.mean(x * x, axis=-1, keepdims=True) + EPS) * g


def _finish_heads(o_ref, x_ref, w_ref, l_ref, acc_ref, n_heads):
    hd = acc_ref.shape[0] // n_heads
    inv_l = 1.0 / l_ref[...]
    heads_t = jnp.concatenate([acc_ref[h * hd:(h + 1) * hd, :] * inv_l[h:h + 1, :] for h in range(n_heads)],
                              axis=0)
    o_ref[0] = x_ref[0] + _dot_tn(heads_t, w_ref[...])


def _row_tile(n, cap):
    t = min(n, cap)
    assert n % t == 0
    return t


def _memkv_kernel(mem_ref, g_ref, w_ref, k_ref, v_ref, kb_ref, vb_ref):
    bb, nm, d = mem_ref.shape
    m = mem_ref[...].reshape(bb * nm, d)
    mn = m * lax.rsqrt(jnp.mean(m * m, axis=-1, keepdims=True) + EPS)
    h = (mn * g_ref[0]).astype(BF16)
    hd = d // MEM_HEADS
    k = jnp.dot(h, w_ref[0, :, :d].astype(BF16), preferred_element_type=F32)
    v = jnp.dot(h, w_ref[0, :, d:].astype(BF16), preferred_element_type=F32)
    for i in range(bb):
        rows = slice(i * nm, (i + 1) * nm)
        kb_ref[0, i] = k[rows].astype(BF16)
        vb_ref[0, i] = v[rows].astype(BF16)
        k_ref[0, i] = pltpu.einshape("m(hd)->mhd", k[rows], d=hd)
        v_ref[0, i] = pltpu.einshape("m(hd)->mhd", v[rows], d=hd)


def _memory_kv(mem, g_mem, w_kv):
    depth, d = g_mem.shape
    b, nm, _ = mem.shape
    hd = d // MEM_HEADS
    out = jax.ShapeDtypeStruct((depth, b, nm, MEM_HEADS, hd), F32)
    out_b = jax.ShapeDtypeStruct((depth, b, nm, d), BF16)
    bb = math.gcd(b, 2)
    heads_spec = pl.BlockSpec((1, bb, nm, MEM_HEADS, hd), lambda l, i: (l, i, 0, 0, 0))
    rows_spec = pl.BlockSpec((1, bb, nm, d), lambda l, i: (l, i, 0, 0))
    return pl.pallas_call(
        _memkv_kernel,
        grid=(depth, b // bb),
        in_specs=[pl.BlockSpec((bb, nm, d), lambda l, i: (i, 0, 0)),
                  pl.BlockSpec((1, 1, d), lambda l, i: (l, 0, 0)),
                  pl.BlockSpec((1, d, 2 * d), lambda l, i: (l, 0, 0))],
        out_specs=[heads_spec, heads_spec, rows_spec, rows_spec],
        out_shape=[out, out, out_b, out_b],
        compiler_params=_cparams("parallel", "parallel"),
        name="memory_kv",
    )(mem, g_mem.reshape(depth, 1, d), w_kv)


def _ffn_kernel(x_ref, g_ref, w1_ref, w2_ref, gf_ref, o_ref, h_ref, acc_ref, *, final_norm):
    j = pl.program_id(1)

    @pl.when(j == 0)
    def _():
        h_ref[...] = _rms(x_ref[...], g_ref[...]).astype(BF16)
        acc_ref[...] = jnp.zeros_like(acc_ref)

    u = jnp.maximum(jnp.dot(h_ref[...], w1_ref[...].astype(BF16), preferred_element_type=F32), 0.0)
    acc_ref[...] += jnp.dot((u * u).astype(BF16), w2_ref[...].astype(BF16), preferred_element_type=F32)

    @pl.when(j == pl.num_programs(1) - 1)
    def _():
        y = x_ref[...] + acc_ref[...]
        o_ref[...] = _rms(y, gf_ref[...]) if final_norm else y


def _ffn(x, g, w1, w2, gf, layer, final_norm):
    n, d = x.shape
    f = w1.shape[2]
    tm = _row_tile(n, WIDE_ROW_TILE)
    tf = FF_TILE
    return pl.pallas_call(
        functools.partial(_ffn_kernel, final_norm=final_norm),
        grid=(n // tm, f // tf),
        in_specs=[pl.BlockSpec((tm, d), lambda i, j: (i, 0)),
                  pl.BlockSpec((None, 1, d), lambda i, j: (layer, 0, 0)),
                  pl.BlockSpec((None, d, tf), lambda i, j: (layer, 0, j)),
                  pl.BlockSpec((None, tf, d), lambda i, j: (layer, j, 0)),
                  pl.BlockSpec((1, d), lambda i, j: (0, 0))],
        out_specs=pl.BlockSpec((tm, d), lambda i, j: (i, 0)),
        out_shape=jax.ShapeDtypeStruct((n, d), F32),
        scratch_shapes=[pltpu.VMEM((tm, d), BF16), pltpu.VMEM((tm, d), F32)],
        compiler_params=_cparams("parallel", "arbitrary"),
        name="ffn",
    )(x, g.reshape(-1, 1, d), w1, w2, gf.reshape(1, d))


def _xattn_kernel(x_ref, g_ref, wq_ref, mk_ref, mv_ref, wo_ref, o_ref):
    x = x_ref[0]
    d = x.shape[-1]
    hd = d // MEM_HEADS
    h = _rms(x, g_ref[...]).astype(BF16)
    q = jnp.dot(h, wq_ref[...].astype(BF16), preferred_element_type=F32)
    outs = []
    if len(mk_ref.shape) == 3:
        mk = pltpu.einshape("mhd->m(hd)", mk_ref[...]).astype(BF16)
        mv = pltpu.einshape("mhd->m(hd)", mv_ref[...]).astype(BF16)
    else:
        mk, mv = mk_ref[...], mv_ref[...]
    for hh in range(MEM_HEADS):
        sl = slice(hh * hd, (hh + 1) * hd)
        kh, vh = mk[:, sl], mv[:, sl]
        s = _dot_nt(q[:, sl], kh) * (hd ** -0.5)
        m = jnp.max(s, axis=-1, keepdims=True)
        p = jnp.exp(s - m)
        l = jnp.sum(p, axis=-1, keepdims=True)
        outs.append(_dot(p, vh) / l)
    o = jnp.concatenate(outs, axis=-1)
    o_ref[0] = x + _dot(o, wo_ref[...])


def _xattn(x, g, wq, mk, mv, wo, layer):
    b, t, d = x.shape
    tm = _row_tile(t, WIDE_ROW_TILE)
    kv_spec = pl.BlockSpec((None, None) + mk.shape[2:], lambda i, j: (layer, i) + (0,) * (mk.ndim - 2))
    return pl.pallas_call(
        _xattn_kernel,
        grid=(b, t // tm),
        in_specs=[pl.BlockSpec((1, tm, d), lambda i, j: (i, j, 0)),
                  pl.BlockSpec((None, 1, d), lambda i, j: (layer, 0, 0)),
                  pl.BlockSpec((None, d, d), lambda i, j: (layer, 0, 0)),
                  kv_spec, kv_spec,
                  pl.BlockSpec((None, d, d), lambda i, j: (layer, 0, 0))],
        out_specs=pl.BlockSpec((1, tm, d), lambda i, j: (i, j, 0)),
        out_shape=jax.ShapeDtypeStruct((b, t, d), F32),
        compiler_params=_cparams("parallel", "parallel"),
        name="xattn",
    )(x, g.reshape(-1, 1, d), wq, mk, mv, wo)


def _conv_kernel(x_ref, g_ref, win_ref, wc_ref, st_ref, wout_ref, o_ref, nst_ref, z_ref):
    t = pl.program_id(1)
    x = x_ref[0]
    tm, d = x.shape
    pad = 8

    @pl.when(t == 0)
    def _():
        z_ref[pad - 2:pad, :] = st_ref[0]

    h = _rms(x, g_ref[...]).astype(BF16)
    bg = jnp.dot(h, win_ref[:, 0:d], preferred_element_type=F32)
    cg = jnp.dot(h, win_ref[:, d:2 * d], preferred_element_type=F32)
    u = jnp.dot(h, win_ref[:, 2 * d:3 * d], preferred_element_type=F32)
    z = cg * u
    z_ref[pad:pad + tm, :] = z
    conv = (z_ref[pad - 2:pad - 2 + tm, :] * wc_ref[0:1, :]
            + z_ref[pad - 1:pad - 1 + tm, :] * wc_ref[1:2, :]
            + z * wc_ref[2:3, :])
    o_ref[0] = x + _dot(bg * conv, wout_ref[...])
    last = z_ref[pad + tm - 2:pad + tm, :]
    z_ref[pad - 2:pad, :] = last

    @pl.when(t == pl.num_programs(1) - 1)
    def _():
        nst_ref[0] = last


def _conv_mixer(x, g, w_in, w_conv, state, w_out):
    b, t, d = x.shape
    tm = _row_tile(t, WIDE_ROW_TILE)
    once = pl.Buffered(1)
    return pl.pallas_call(
        _conv_kernel,
        grid=(b, t // tm),
        in_specs=[pl.BlockSpec((1, tm, d), lambda i, j: (i, j, 0)),
                  pl.BlockSpec((1, d), lambda i, j: (0, 0)),
                  pl.BlockSpec((d, 3 * d), lambda i, j: (0, 0), pipeline_mode=once),
                  pl.BlockSpec((CONV_W, d), lambda i, j: (0, 0)),
                  pl.BlockSpec((1, CONV_W - 1, d), lambda i, j: (i, 0, 0)),
                  pl.BlockSpec((d, d), lambda i, j: (0, 0), pipeline_mode=once)],
        out_specs=[pl.BlockSpec((1, tm, d), lambda i, j: (i, j, 0)),
                   pl.BlockSpec((1, CONV_W - 1, d), lambda i, j: (i, 0, 0))],
        out_shape=[jax.ShapeDtypeStruct((b, t, d), F32),
                   jax.ShapeDtypeStruct((b, CONV_W - 1, d), F32)],
        scratch_shapes=[pltpu.VMEM((tm + 8, d), F32)],
        compiler_params=_cparams("parallel", "arbitrary"),
        name="conv_mixer",
    )(x, g.reshape(1, d), w_in, w_conv, state, w_out)


def _pool_kernel(x_ref, g_ref, st_ref, wg_ref, sc_ref, o_ref, nst_ref, h_ref, *s_refs, pos0):
    t = pl.program_id(1)
    x = x_ref[0]
    tm, d = x.shape
    n_lv = len(POOL_WINDOWS)
    gw = d // n_lv
    base = 2 * (POOL_STATE + 1)
    lead = base - POOL_STATE
    end = base + tm

    @pl.when(t == 0)
    def _():
        h_ref[0:lead, :] = jnp.zeros((lead, d), F32)
        h_ref[lead:base, :] = st_ref[0]

    h = _rms(x, g_ref[...])
    h_ref[base:end, :] = h
    pos = pos0 + t * tm + lax.broadcasted_iota(I32, (tm, gw), 0)
    ys = []
    prev, c_prev = h_ref, 0
    for lv in range(1, n_lv + 1):
        w, shift, start = POOL_WINDOWS[lv - 1], 2 ** (lv - 1), 8 * lv
        c0 = (lv - 1) * gw
        cols = slice(c0 - c_prev, d - c_prev)
        cur = prev[start:end, cols] + prev[start - shift:end - shift, cols]
        if lv < n_lv:
            s_refs[lv - 1][start:end, :] = cur[:, gw:]
        win = cur[base - start:, :gw]
        count = jnp.minimum(w, pos + 1).astype(F32)
        dlt = win / count - h[:, c0:c0 + gw]
        ys.append(_dot(dlt, wg_ref[lv - 1]))
        if lv < n_lv:
            prev, c_prev = s_refs[lv - 1], c0 + gw
    y = jnp.concatenate(ys, axis=-1) * sc_ref[...]
    o_ref[0] = x + y
    last = h_ref[end - POOL_STATE:end, :]
    h_ref[lead:base, :] = last

    @pl.when(t == pl.num_programs(1) - 1)
    def _():
        nst_ref[0] = last


def _pool_mixer(x, g, state, w_group, scale, pos0):
    b, t, d = x.shape
    ng, gw, _ = w_group.shape
    tm = _row_tile(t, ROW_TILE)
    assert POOL_WINDOWS == tuple(2 ** (lv + 1) for lv in range(ng)) and tm >= POOL_STATE
    rows = tm + 2 * (POOL_STATE + 1)
    return pl.pallas_call(
        functools.partial(_pool_kernel, pos0=pos0),
        grid=(b, t // tm),
        in_specs=[pl.BlockSpec((1, tm, d), lambda i, j: (i, j, 0)),
                  pl.BlockSpec((1, d), lambda i, j: (0, 0)),
                  pl.BlockSpec((1, POOL_STATE, d), lambda i, j: (i, 0, 0)),
                  pl.BlockSpec((ng, gw, gw), lambda i, j: (0, 0, 0)),
                  pl.BlockSpec((1, d), lambda i, j: (0, 0))],
        out_specs=[pl.BlockSpec((1, tm, d), lambda i, j: (i, j, 0)),
                   pl.BlockSpec((1, POOL_STATE, d), lambda i, j: (i, 0, 0))],
        out_shape=[jax.ShapeDtypeStruct((b, t, d), F32),
                   jax.ShapeDtypeStruct((b, POOL_STATE, d), F32)],
        scratch_shapes=[pltpu.VMEM((rows, d - lv * gw), F32) for lv in range(ng)],
        compiler_params=_cparams("parallel", "arbitrary"),
        name="pool_mixer",
    )(x, g.reshape(1, d), state, w_group, scale.reshape(1, d))


def _proj_kernel(x_ref, g_ref, w_ref, e_ref, *o_refs, widths, outs):
    h = _rms(x_ref[0], g_ref[...]).astype(BF16)
    starts = [sum(widths[:i]) for i in range(len(widths))]
    ys = {}
    for (wi, mode, _, ep), o_ref in zip(outs, o_refs):
        if wi not in ys:
            w = w_ref[:, starts[wi]:starts[wi] + widths[wi]].astype(BF16)
            if widths[wi] % LANE:
                w = jnp.concatenate([w, jnp.zeros((w.shape[0], -widths[wi] % LANE), BF16)], axis=1)
            ys[wi] = jnp.dot(h, w, preferred_element_type=F32)
        y = ys[wi]
        if ep == "log_sigmoid":
            u = -(y + e_ref[...])
            y = -(jnp.maximum(u, 0.0) + jnp.log1p(jnp.exp(-jnp.abs(u))))
        if mode == "rows":
            o_ref[0] = y.astype(o_ref.dtype)
        elif isinstance(mode, tuple) and mode[0] == "first":
            o_ref[0] = y[:, :mode[1]].astype(o_ref.dtype)
        elif isinstance(mode, tuple):
            o_ref[0] = jnp.transpose(y)[:mode[1], :].astype(o_ref.dtype)
        elif mode == "t":
            o_ref[0] = jnp.transpose(y).astype(o_ref.dtype)
        else:
            o_ref[0] = pltpu.einshape("m(hd)->mhd", y.astype(o_ref.dtype), d=mode)


def _proj(x, g, w, widths, outs, extra):
    b, t, d = x.shape
    assert sum(widths) == w.shape[1]
    tm = _row_tile(t, ROW_TILE)
    in_specs = [pl.BlockSpec((1, tm, d), lambda i, j: (i, j, 0)),
                pl.BlockSpec((1, d), lambda i, j: (0, 0)),
                pl.BlockSpec(w.shape, lambda i, j: (0, 0), pipeline_mode=pl.Buffered(1)),
                pl.BlockSpec(extra.shape, lambda i, j: (0, 0))]
    out_specs, out_shape = [], []
    for wi, mode, dt, _ in outs:
        n = _round_up(widths[wi], LANE)
        if isinstance(mode, tuple):
            mode, n = ("rows" if mode[0] == "first" else "t"), mode[1]
        if mode == "rows":
            out_specs.append(pl.BlockSpec((1, tm, n), lambda i, j: (i, j, 0)))
            out_shape.append(jax.ShapeDtypeStruct((b, t, n), dt))
        elif mode == "t":
            out_specs.append(pl.BlockSpec((1, n, tm), lambda i, j: (i, 0, j)))
            out_shape.append(jax.ShapeDtypeStruct((b, n, t), dt))
        else:
            out_specs.append(pl.BlockSpec((1, tm, n // mode, mode), lambda i, j: (i, j, 0, 0)))
            out_shape.append(jax.ShapeDtypeStruct((b, t, n // mode, mode), dt))
    return pl.pallas_call(
        functools.partial(_proj_kernel, widths=tuple(widths), outs=tuple(outs)),
        grid=(b, t // tm),
        in_specs=in_specs,
        out_specs=out_specs,
        out_shape=out_shape,
        compiler_params=_cparams("parallel", "parallel"),
        name="norm_proj",
    )(x, g.reshape(1, d), w, extra)


def _outproj_kernel(x_ref, a_ref, w_ref, o_ref):
    o_ref[...] = x_ref[...] + _dot(a_ref[...], w_ref[...])


def _out_proj(x, a, w):
    n, d = x.shape
    tm = _row_tile(n, ROW_TILE)
    return pl.pallas_call(
        _outproj_kernel,
        grid=(n // tm,),
        in_specs=[pl.BlockSpec((tm, d), lambda i: (i, 0)),
                  pl.BlockSpec((tm, d), lambda i: (i, 0)),
                  pl.BlockSpec((d, d), lambda i: (0, 0))],
        out_specs=pl.BlockSpec((tm, d), lambda i: (i, 0)),
        out_shape=jax.ShapeDtypeStruct((n, d), F32),
        compiler_params=_cparams("parallel"),
        name="out_proj",
    )(x, a, w)


def _bias_table_kernel(rbt_ref, o_ref):
    width = o_ref.shape[-1]
    rel = BIAS_CENTER - lax.broadcasted_iota(I32, (1, width), 1)
    nb = N_BUCKETS // 2
    max_exact = nb // 2
    ret = (rel > 0).astype(I32) * nb
    n = jnp.abs(rel)
    nf = jnp.maximum(n, 1).astype(F32)
    large = max_exact + (jnp.log(nf / max_exact) / math.log(MAX_DIST / max_exact)
                         * (nb - max_exact)).astype(I32)
    large = jnp.minimum(large, nb - 1)
    bucket = ret + jnp.where(n < max_exact, n, large)
    acc = jnp.zeros(o_ref.shape, F32)
    for j in range(N_BUCKETS):
        acc = jnp.where(bucket == j, rbt_ref[:, j:j + 1], acc)
    o_ref[...] = acc * LOG2E


def _bias_tiles_kernel(rbt_ref, far_ref, near_ref, tab_ref):
    n_d, nh, kb, tq = near_ref.shape
    _bias_table_kernel(rbt_ref, tab_ref)
    far_ref[...] = jnp.broadcast_to(tab_ref[:, BIAS_CENTER + MAX_DIST:BIAS_CENTER + MAX_DIST + 1], far_ref.shape)
    for dd in range(n_d):
        s0 = BIAS_CENTER - (dd - 1) * kb - kb
        for h in range(nh):
            rows = jnp.broadcast_to(tab_ref[h:h + 1, s0:s0 + tq + kb], (kb, tq + kb))
            near_ref[dd, h] = pltpu.roll(rows, 0, 1, stride=1, stride_axis=0)[:, kb:]


def _bias_tiles(rel_bias, kb, tq):
    nh = rel_bias.shape[1]
    n_d = tq // kb + 1
    width = BIAS_CENTER + kb + tq + kb
    assert kb % LANE == 0 and tq % kb == 0 and kb >= MAX_DIST and BIAS_CENTER >= tq
    return pl.pallas_call(
        _bias_tiles_kernel,
        out_shape=[jax.ShapeDtypeStruct((nh, LANE), F32), jax.ShapeDtypeStruct((n_d, nh, kb, tq), F32)],
        scratch_shapes=[pltpu.VMEM((nh, width), F32)],
        name="bias_tiles",
    )(rel_bias.T)


def _sortable(x):
    x = jnp.where(x == 0.0, 0.0, x)
    bits = lax.bitcast_convert_type(x, I32)
    return jnp.where(bits < 0, bits ^ 0x7FFFFFFF, bits)


def _neg_inf_key():
    return int(np.float32(NEG_INF).view(np.int32)) ^ 0x7FFFFFFF


def _dsa_prompt_kernel(qt_ref, qit_ref, kwt_ref, kw_ref, k_ref, vt_ref, far_ref, near_ref, x_ref, wout_ref, o_ref,
                       key_ref, hi_ref, lo_ref, sel_ref, m_ref, l_ref, acc_ref, a_ref, *, top_k):
    qi = pl.program_id(1)
    tq = qt_ref.shape[2]
    n_keys = kw_ref.shape[1]
    kb_sz = LANE
    q0 = qi * tq
    nkb = jnp.minimum(n_keys, q0 + tq) // kb_sz
    negkey = _neg_inf_key()

    qlane = lax.broadcasted_iota(I32, (1, tq), 1)
    lim = ((q0 + qlane) // CHUNK + 1) * CHUNK
    krow = lax.broadcasted_iota(I32, (kb_sz, tq), 0)

    def kslice(kb):
        return pl.ds(pl.multiple_of(kb * kb_sz, kb_sz), kb_sz)

    sb = 2 * kb_sz
    srow = lax.broadcasted_iota(I32, (sb, tq), 0)

    def score_body(i, c):
        rows = pl.ds(pl.multiple_of(i * sb, sb), sb)
        kid = kw_ref[0, rows, :][:, :IDX_DIM].astype(BF16)
        sc = jnp.zeros((sb, tq), F32)
        for h in range(IDX_HEADS):
            s = jnp.dot(kid, qit_ref[0, h * IDX_DIM:(h + 1) * IDX_DIM, :], preferred_element_type=F32)
            sc = sc + jnp.maximum(s, 0.0) * kwt_ref[0, IDX_DIM + h:IDX_DIM + h + 1, :]
        sc = sc * ((IDX_DIM * IDX_HEADS) ** -0.5)
        key = jnp.where(i * sb + srow < lim, _sortable(sc), negkey)
        key_ref[rows, :] = key
        hi_ref[rows, :] = (key >> 16).astype(I16)
        lo_ref[rows, :] = ((key & 0xFFFF) - 0x8000).astype(I16)
        return c

    lax.fori_loop(0, nkb // 2, score_body, 0)

    def search16(ref):
        def bit_body(i, t_u):
            cand_u = t_u | jnp.left_shift(jnp.int32(1), 15 - i)
            cand = (cand_u - 0x8000).astype(I16)

            def body(j, a):
                ind = jnp.where(ref[pl.ds(pl.multiple_of(j * sb, sb), sb), :] >= cand,
                                jnp.ones((), I16), jnp.zeros((), I16))
                parts = [ind[16 * r:16 * (r + 1), :] for r in range(sb // 16)]
                while len(parts) > 1:
                    parts = [parts[r] + parts[r + 1] for r in range(0, len(parts), 2)]
                return a + parts[0]
            a = lax.fori_loop(0, nkb // 2, body, jnp.zeros((16, tq), I16))
            cnt = jnp.sum(a.astype(I32), axis=0, keepdims=True)
            return jnp.where(cnt >= top_k, cand_u, t_u)
        return lax.fori_loop(0, 16, bit_body, jnp.zeros((1, tq), I32))

    def count(pred_fn):
        def body(i, a):
            for u in range(2):
                kb = 2 * i + u
                ind = pred_fn(kb, key_ref[kslice(kb), :])
                a = a + jnp.sum(ind.reshape(kb_sz // 8, 8, tq), axis=0)
            return a
        a = lax.fori_loop(0, nkb // 2, body, jnp.zeros((8, tq), I32))
        return jnp.sum(a, axis=0, keepdims=True)

    t_hi = search16(hi_ref)
    t_hi16 = (t_hi - 0x8000).astype(I16)

    def lo_body(j, c):
        rows = pl.ds(pl.multiple_of(j * sb, sb), sb)
        hi = hi_ref[rows, :]
        lo_ref[rows, :] = jnp.where(hi == t_hi16, lo_ref[rows, :],
                                    jnp.where(hi > t_hi16, jnp.full((), 0x7FFF, I16), jnp.full((), -0x8000, I16)))
        return c

    lax.fori_loop(0, nkb // 2, lo_body, 0)
    t_s = (jnp.left_shift(t_hi, 16) | search16(lo_ref)) ^ INT_MIN

    def adm01(kb):
        return jnp.where(kb * kb_sz + krow < lim, 1.0, 0.0)

    def cap(sel):
        return jnp.where(sel != 0.0, jnp.inf, NEG_INF)

    def sel_body(kb, a):
        sel = jnp.where(key_ref[kslice(kb), :] >= t_s, adm01(kb), 0.0)
        sel_ref[kslice(kb), :] = cap(sel)
        return a + jnp.sum(sel.reshape(kb_sz // 8, 8, tq), axis=0)

    n_sel = jnp.sum(lax.fori_loop(0, nkb, sel_body, jnp.zeros((8, tq), F32)), axis=0, keepdims=True)

    @pl.when(jnp.max(n_sel) > top_k)
    def _():
        n_gt = count(lambda kb, key: jnp.where(key > t_s, 1, 0))
        need = (top_k - n_gt).astype(F32)
        r = lax.broadcasted_iota(I32, (kb_sz, kb_sz), 0)
        c = lax.broadcasted_iota(I32, (kb_sz, kb_sz), 1)
        ltri = jnp.where(c < r, 1.0, 0.0).astype(BF16)

        def tie_body(kb, carry):
            key = key_ref[kslice(kb), :]
            adm = adm01(kb)
            eq = jnp.where(key == t_s, adm, 0.0)
            rank = carry + jnp.dot(ltri, eq.astype(BF16), preferred_element_type=F32)
            keep = jnp.where(rank < need, eq, 0.0)
            sel_ref[kslice(kb), :] = cap(jnp.where(key > t_s, adm, keep))
            return carry + jnp.sum(eq, axis=0, keepdims=True)

        lax.fori_loop(0, nkb, tie_body, jnp.zeros((1, tq), F32))

    m_ref[...] = jnp.full(m_ref.shape, NEG_INF, F32)
    l_ref[...] = jnp.zeros(l_ref.shape, F32)
    acc_ref[...] = jnp.zeros(acc_ref.shape, F32)
    c1 = (B_HD ** -0.5) * LOG2E

    def attend(k0, nk, bias2_fn):
        rows = pl.ds(pl.multiple_of(k0, LANE), nk)
        sel_cap = sel_ref[rows, :]
        ks = k_ref[0, rows, :]
        cols = []
        for h in range(B_HEADS):
            g = h // B_REP
            z = jnp.dot(ks[:, g * B_HD:(g + 1) * B_HD], qt_ref[0, h * B_HD:(h + 1) * B_HD, :],
                        preferred_element_type=F32)
            a = jnp.minimum(z * c1 + bias2_fn(h), sel_cap)
            a_ref[h, 0:nk, :] = a
            cols.append(jnp.max(a, axis=0, keepdims=True))
        m_old = m_ref[...]
        m_new = jnp.maximum(m_old, jnp.concatenate(cols, axis=0))
        alpha = jnp.exp2(m_old - m_new)
        m_ref[...] = m_new
        sums = []
        for h in range(B_HEADS):
            g = h // B_REP
            p = jnp.exp2(a_ref[h, 0:nk, :] - m_new[h:h + 1, :])
            sums.append(jnp.sum(p, axis=0, keepdims=True))
            hs = slice(h * B_HD, (h + 1) * B_HD)
            pv = jnp.dot(vt_ref[0, g * B_HD:(g + 1) * B_HD, rows], p.astype(BF16),
                         preferred_element_type=F32)
            acc_ref[hs, :] = alpha[h:h + 1, :] * acc_ref[hs, :] + pv
        l_ref[...] = alpha * l_ref[...] + jnp.concatenate(sums, axis=0)

    ab = near_ref.shape[2]
    n_far = jnp.maximum(q0 // ab - 1, 0)
    far_bias2 = far_ref[:, 0:1]

    def far_body(i, c):
        attend(i * ab, ab, lambda h: far_bias2[h:h + 1, :])
        return c

    lax.fori_loop(0, n_far, far_body, 0)

    def near_body(i, c):
        dd = i - q0 // ab + 1
        attend(i * ab, ab, lambda h: near_ref[dd, h])
        return c

    lax.fori_loop(n_far, nkb * kb_sz // ab, near_body, 0)

    _finish_heads(o_ref, x_ref, wout_ref, l_ref, acc_ref, B_HEADS)


def _dsa_attention_prompt(qt, qit, kwt, kw, k, vt, far, near, x, w_out):
    b, d, t = qt.shape
    tq = near.shape[3]
    top_k = min(TOPK_MAX, t // 4)
    ab = near.shape[2]
    assert t % tq == 0 and tq % CHUNK == 0 and tq % ab == 0 and ab % LANE == 0 and tq % (2 * LANE) == 0
    return pl.pallas_call(
        functools.partial(_dsa_prompt_kernel, top_k=top_k),
        grid=(b, t // tq),
        in_specs=[pl.BlockSpec((1, d, tq), lambda i, j: (i, 0, j)),
                  pl.BlockSpec((1, qit.shape[1], tq), lambda i, j: (i, 0, j)),
                  pl.BlockSpec((1, LANE, tq), lambda i, j: (i, 0, j)),
                  pl.BlockSpec((1, t, LANE), lambda i, j: (i, 0, 0)),
                  pl.BlockSpec((1, t, B_KV * B_HD), lambda i, j: (i, 0, 0)),
                  pl.BlockSpec((1, B_KV * B_HD, t), lambda i, j: (i, 0, 0)),
                  pl.BlockSpec(far.shape, lambda i, j: (0, 0)),
                  pl.BlockSpec(near.shape, lambda i, j: (0, 0, 0, 0)),
                  pl.BlockSpec((1, tq, d), lambda i, j: (i, j, 0)),
                  pl.BlockSpec((d, d), lambda i, j: (0, 0))],
        out_specs=pl.BlockSpec((1, tq, d), lambda i, j: (i, j, 0)),
        out_shape=jax.ShapeDtypeStruct((b, t, d), F32),
        scratch_shapes=[pltpu.VMEM((t, tq), I32), pltpu.VMEM((t, tq), I16), pltpu.VMEM((t, tq), I16),
                        pltpu.VMEM((t, tq), F32),
                        pltpu.VMEM((B_HEADS, tq), F32), pltpu.VMEM((B_HEADS, tq), F32),
                        pltpu.VMEM((d, tq), F32), pltpu.VMEM((B_HEADS, ab, tq), F32)],
        compiler_params=_cparams("parallel", "parallel"),
        name="dsa_attention",
    )(qt, qit, kwt, kw, k, vt, far, near, x, w_out)


def _dsa_cached_kernel(q_ref, qi_ref, wi_ref, rb_ref, kidx_ref, kp_ref, vp_ref, kn_ref, vn_ref, o_ref,
                       *, past, t_new, top_k):
    n_keys = past + t_new
    lp = kidx_ref.shape[1]
    negkey = _neg_inf_key()
    kpos = lax.broadcasted_iota(I32, (t_new, lp), 1)
    qpos = past + lax.broadcasted_iota(I32, (t_new, lp), 0)
    adm = kpos < (qpos // CHUNK + 1) * CHUNK

    s = _dot_nt(qi_ref[0], kidx_ref[0])
    w = jnp.maximum(s, 0.0) * wi_ref[0]
    sc = w[0:t_new]
    for h in range(1, IDX_HEADS):
        sc = sc + w[h * t_new:(h + 1) * t_new]
    sc = sc * ((IDX_DIM * IDX_HEADS) ** -0.5)
    key = jnp.where(adm, _sortable(sc), negkey)
    key = jnp.where(kpos < n_keys, key, INT_MIN)

    def bit_body(i, t_u):
        cand_u = t_u | jnp.left_shift(jnp.int32(1), 31 - i)
        cnt = jnp.sum(jnp.where(key >= (cand_u ^ INT_MIN), 1.0, 0.0), axis=1, keepdims=True)
        return jnp.where(cnt >= top_k, cand_u, t_u)

    t_s = lax.fori_loop(0, 32, bit_body, jnp.zeros((t_new, 1), I32)) ^ INT_MIN

    adm01 = jnp.where(adm, 1.0, 0.0)
    gt = jnp.where(key > t_s, adm01, 0.0)
    eq = jnp.where(key == t_s, adm01, 0.0)
    need = top_k - jnp.sum(jnp.where(key > t_s, 1.0, 0.0), axis=1, keepdims=True)
    r = lax.broadcasted_iota(I32, (LANE, LANE), 0)
    c = lax.broadcasted_iota(I32, (LANE, LANE), 1)
    utri = jnp.where(r < c, 1.0, 0.0).astype(BF16)
    carry = jnp.zeros((t_new, 1), F32)
    keeps = []
    for blk in range(lp // LANE):
        e = eq[:, blk * LANE:(blk + 1) * LANE]
        rank = carry + jnp.dot(e.astype(BF16), utri, preferred_element_type=F32)
        keeps.append(jnp.where(rank < need, e, 0.0))
        carry = carry + jnp.sum(e, axis=1, keepdims=True)
    sel = gt + jnp.concatenate(keeps, axis=1)

    near = max(past - MAX_DIST, 0) // LANE * LANE
    rel = (kpos - qpos)[:, near:]
    nb = N_BUCKETS // 2
    max_exact = nb // 2
    n = jnp.abs(rel)
    nf = jnp.maximum(n, 1).astype(F32)
    large = max_exact + (jnp.log(nf / max_exact) / math.log(MAX_DIST / max_exact)
                         * (nb - max_exact)).astype(I32)
    bucket = (rel > 0).astype(I32) * nb + jnp.where(n < max_exact, n, jnp.minimum(large, nb - 1))

    rows = B_REP * t_new
    sel_g = jnp.concatenate([sel] * B_REP, axis=0) != 0.0
    bucket_g = jnp.concatenate([bucket] * B_REP, axis=0)
    for g in range(B_KV):
        grp = lambda ref, n: ref[0, pl.ds(g, n, stride=B_KV), :]
        qg = q_ref[0, g * rows:(g + 1) * rows, :]
        rb = rb_ref[g * rows:(g + 1) * rows, :]
        bias_near = jnp.zeros((rows, lp - near), F32)
        for j in range(N_BUCKETS):
            bias_near = jnp.where(bucket_g == j, rb[:, j:j + 1], bias_near)
        bias = jnp.concatenate([jnp.broadcast_to(rb[:, nb - 1:nb], (rows, near)), bias_near], axis=1)
        zp = _dot_nt(qg, grp(kp_ref, past)) * (B_HD ** -0.5)
        zn = _dot_nt(qg, grp(kn_ref, t_new)) * (B_HD ** -0.5)
        ap = jnp.where(sel_g[:, :past], zp + bias[:, :past], NEG_INF)
        an = jnp.where(sel_g[:, past:n_keys], zn + bias[:, past:n_keys], NEG_INF)
        m = jnp.maximum(jnp.max(ap, axis=1, keepdims=True), jnp.max(an, axis=1, keepdims=True))
        pp, pn = jnp.exp(ap - m), jnp.exp(an - m)
        l = jnp.sum(pp, axis=1, keepdims=True) + jnp.sum(pn, axis=1, keepdims=True)
        o_ref[0, g * rows:(g + 1) * rows, :] = (_dot(pp, grp(vp_ref, past)) + _dot(pn, grp(vn_ref, t_new))) / l


def _dsa_attention_cached(q_rows, qi_rows, wi_col, rb_rows, kidx_all, k_past, v_past, k_new, v_new):
    b, rows, hd = q_rows.shape
    past, t_new = k_past.shape[1], k_new.shape[1]
    lp = kidx_all.shape[1]
    top_k = min(TOPK_MAX, (past + t_new) // 4)
    assert past % LANE == 0
    flat = lambda a: a.reshape(b, a.shape[1] * B_KV, B_HD)
    kv_spec = lambda n: pl.BlockSpec((1, n * B_KV, B_HD), lambda i: (i, 0, 0))
    return pl.pallas_call(
        functools.partial(_dsa_cached_kernel, past=past, t_new=t_new, top_k=top_k),
        grid=(b,),
        in_specs=[pl.BlockSpec((1, rows, hd), lambda i: (i, 0, 0)),
                  pl.BlockSpec((1,) + qi_rows.shape[1:], lambda i: (i, 0, 0)),
                  pl.BlockSpec((1,) + wi_col.shape[1:], lambda i: (i, 0, 0)),
                  pl.BlockSpec(rb_rows.shape, lambda i: (0, 0)),
                  pl.BlockSpec((1, lp, IDX_DIM), lambda i: (i, 0, 0)),
                  kv_spec(past), kv_spec(past), kv_spec(t_new), kv_spec(t_new)],
        out_specs=pl.BlockSpec((1, rows, hd), lambda i: (i, 0, 0)),
        out_shape=jax.ShapeDtypeStruct((b, rows, hd), F32),
        compiler_params=_cparams("parallel"),
        name="dsa_attention_cached",
    )(q_rows, qi_rows, wi_col, rb_rows, kidx_all, flat(k_past), flat(v_past), flat(k_new), flat(v_new))


def _cumsum_kernel(x_ref, o_ref):
    x = x_ref[0]
    n = x.shape[-1]
    lane = lax.broadcasted_iota(I32, x.shape, 1)
    s = 1
    while s < n:
        x = x + jnp.where(lane >= s, pltpu.roll(x, s, 1), 0.0)
        s *= 2
    o_ref[0] = x


def _cumsum_lanes(x):
    b, h, n = x.shape
    return pl.pallas_call(
        _cumsum_kernel,
        grid=(b,),
        in_specs=[pl.BlockSpec((1, h, n), lambda i: (i, 0, 0))],
        out_specs=pl.BlockSpec((1, h, n), lambda i: (i, 0, 0)),
        out_shape=jax.ShapeDtypeStruct((b, h, n), F32),
        compiler_params=_cparams("parallel"),
        name="logf_cumsum",
    )(x)


def _fox_init(m_ref, l_ref, acc_ref):
    m_ref[...] = jnp.full(m_ref.shape, NEG_INF, F32)
    l_ref[...] = jnp.zeros(l_ref.shape, F32)
    acc_ref[...] = jnp.zeros(acc_ref.shape, F32)


def _fox_tile(z_fn, pv_fn, cq, ck, cap, m_ref, l_ref, acc_ref, a_ref):
    c1 = (D_HD ** -0.5) * LOG2E
    cq2, ck2 = cq * LOG2E, ck * LOG2E
    cols = []
    for h in range(D_HEADS):
        a = z_fn(h) * c1 - ck2[:, h:h + 1]
        if cap is not None:
            a = jnp.minimum(a, cap)
        a_ref[h] = a
        cols.append(jnp.max(a, axis=0, keepdims=True))
    m_old = m_ref[...]
    m_new = jnp.maximum(m_old, jnp.concatenate(cols, axis=0) + cq2)
    alpha = jnp.exp2(m_old - m_new)
    shift = m_new - cq2
    m_ref[...] = m_new
    sums = []
    for h in range(D_HEADS):
        p = jnp.exp2(a_ref[h] - shift[h:h + 1, :])
        sums.append(jnp.sum(p, axis=0, keepdims=True))
        hs = slice(h * D_HD, (h + 1) * D_HD)
        acc_ref[hs, :] = alpha[h:h + 1, :] * acc_ref[hs, :] + pv_fn(h, p.astype(BF16))
    l_ref[...] = alpha * l_ref[...] + jnp.concatenate(sums, axis=0)


def _hs(h):
    return slice(h * D_HD, (h + 1) * D_HD)


def _fox_prompt_kernel(qt_ref, k_ref, vt_ref, cq_ref, ck_ref, x_ref, wout_ref, o_ref,
                       m_ref, l_ref, acc_ref, a_ref):
    qi, step = pl.program_id(1), pl.program_id(2)
    tq, tk = qt_ref.shape[2], a_ref.shape[1]
    q0 = qi * tq
    ki = step - (pl.num_programs(2) - 1 - (q0 + tq - 1) // tk)
    k0 = ki * tk
    keys = pl.ds(pl.multiple_of(jnp.maximum(k0, 0), tk), tk)

    @pl.when(step == 0)
    def _():
        _fox_init(m_ref, l_ref, acc_ref)

    def run(masked):
        cap = None
        if masked:
            causal = (k0 + lax.broadcasted_iota(I32, (tk, tq), 0)) <= (q0 + lax.broadcasted_iota(I32, (tk, tq), 1))
            cap = jnp.where(causal, jnp.inf, NEG_INF)
        _fox_tile(lambda h: jnp.dot(k_ref[0, keys, _hs(h)], qt_ref[0, _hs(h), :], preferred_element_type=F32),
                  lambda h, p: jnp.dot(vt_ref[0, _hs(h), keys], p, preferred_element_type=F32),
                  cq_ref[0], ck_ref[0, keys, :], cap, m_ref, l_ref, acc_ref, a_ref)

    fully_visible = k0 + tk - 1 <= q0
    pl.when(jnp.logical_and(ki >= 0, fully_visible))(lambda: run(False))
    pl.when(jnp.logical_and(ki >= 0, jnp.logical_not(fully_visible)))(lambda: run(True))

    @pl.when(step == pl.num_programs(2) - 1)
    def _():
        _finish_heads(o_ref, x_ref, wout_ref, l_ref, acc_ref, D_HEADS)


def _fox_attention_prompt(qt, k, vt, cum_t, cum, x, w_out, tq, tk):
    b, d, t = qt.shape
    nq, nk = t // tq, t // tk
    return pl.pallas_call(
        _fox_prompt_kernel,
        grid=(b, nq, nk),
        in_specs=[pl.BlockSpec((1, d, tq), lambda i, j, kk: (i, 0, j)),
                  pl.BlockSpec((1, t, d), lambda i, j, kk: (i, 0, 0)),
                  pl.BlockSpec((1, d, t), lambda i, j, kk: (i, 0, 0)),
                  pl.BlockSpec((1, D_HEADS, tq), lambda i, j, kk: (i, 0, j)),
                  pl.BlockSpec((1, t, D_HEADS), lambda i, j, kk: (i, 0, 0)),
                  pl.BlockSpec((1, tq, d), lambda i, j, kk: (i, j, 0)),
                  pl.BlockSpec((d, d), lambda i, j, kk: (0, 0))],
        out_specs=pl.BlockSpec((1, tq, d), lambda i, j, kk: (i, j, 0)),
        out_shape=jax.ShapeDtypeStruct((b, t, d), F32),
        scratch_shapes=[pltpu.VMEM((D_HEADS, tq), F32), pltpu.VMEM((D_HEADS, tq), F32),
                        pltpu.VMEM((d, tq), F32), pltpu.VMEM((D_HEADS, tk, tq), F32)],
        compiler_params=_cparams("parallel", "parallel", "arbitrary"),
        name="fox_attention",
    )(qt, k, vt, cum_t, cum, x, w_out)


def _fox_cached_kernel(q_ref, kp_ref, vp_ref, kn_ref, vn_ref, cq_ref, ckp_ref, ckn_ref, o_ref,
                       m_ref, l_ref, acc_ref, *, t_new):
    ki = pl.program_id(1)
    n_past = pl.num_programs(1) - 1
    rows = q_ref.shape[1]
    c1 = (D_HD ** -0.5) * LOG2E

    @pl.when(ki == 0)
    def _():
        m_ref[...] = jnp.full(m_ref.shape, NEG_INF, F32)
        l_ref[...] = jnp.zeros(l_ref.shape, F32)
        acc_ref[...] = jnp.zeros(acc_ref.shape, F32)

    def tile(k2d, v2d, ck_row, causal):
        cols = k2d.shape[0]
        a = _dot_nt(q_ref[0], k2d) * c1 - ck_row * LOG2E
        r = lax.broadcasted_iota(I32, (rows, cols), 0)
        c = lax.broadcasted_iota(I32, (rows, cols), 1)
        ok = (c % D_HEADS) == (r // t_new)
        if causal:
            ok = jnp.logical_and(ok, (c // D_HEADS) <= (r % t_new))
        a = jnp.where(ok, a, NEG_INF)
        cq2 = cq_ref[0] * LOG2E
        m_old = m_ref[...]
        m_new = jnp.maximum(m_old, jnp.max(a, axis=1, keepdims=True) + cq2)
        alpha = jnp.exp2(m_old - m_new)
        p = jnp.exp2(a - (m_new - cq2))
        l_ref[...] = alpha * l_ref[...] + jnp.sum(p, axis=1, keepdims=True)
        acc_ref[...] = alpha * acc_ref[...] + _dot(p, v2d)
        m_ref[...] = m_new

    @pl.when(ki < n_past)
    def _():
        tk = kp_ref.shape[1]
        tile(kp_ref[0].reshape(tk * D_HEADS, D_HD), vp_ref[0].reshape(tk * D_HEADS, D_HD), ckp_ref[0], False)

    @pl.when(ki == n_past)
    def _():
        tile(kn_ref[0].reshape(t_new * D_HEADS, D_HD), vn_ref[0].reshape(t_new * D_HEADS, D_HD), ckn_ref[0], True)
        o_ref[0] = acc_ref[...] / l_ref[...]


def _fox_attention_cached(q_rows, k_past, v_past, k_new, v_new, cq_col, ck_past, ck_new, tk):
    b, rows, hd = q_rows.shape
    past, t_new = k_past.shape[1], k_new.shape[1]
    n_past = past // tk
    pidx = lambda i, kk: (i, jnp.minimum(kk, n_past - 1), 0, 0)
    return pl.pallas_call(
        functools.partial(_fox_cached_kernel, t_new=t_new),
        grid=(b, n_past + 1),
        in_specs=[pl.BlockSpec((1, rows, hd), lambda i, kk: (i, 0, 0)),
                  pl.BlockSpec((1, tk, D_HEADS, D_HD), pidx),
                  pl.BlockSpec((1, tk, D_HEADS, D_HD), pidx),
                  pl.BlockSpec((1, t_new, D_HEADS, D_HD), lambda i, kk: (i, 0, 0, 0)),
                  pl.BlockSpec((1, t_new, D_HEADS, D_HD), lambda i, kk: (i, 0, 0, 0)),
                  pl.BlockSpec((1, rows, 1), lambda i, kk: (i, 0, 0)),
                  pl.BlockSpec((1, 1, tk * D_HEADS), lambda i, kk: (i, 0, jnp.minimum(kk, n_past - 1))),
                  pl.BlockSpec((1, 1, t_new * D_HEADS), lambda i, kk: (i, 0, 0))],
        out_specs=pl.BlockSpec((1, rows, hd), lambda i, kk: (i, 0, 0)),
        out_shape=jax.ShapeDtypeStruct((b, rows, hd), F32),
        scratch_shapes=[pltpu.VMEM((rows, 1), F32), pltpu.VMEM((rows, 1), F32), pltpu.VMEM((rows, hd), F32)],
        compiler_params=_cparams("parallel", "arbitrary"),
        name="fox_attention_cached",
    )(q_rows, k_past, v_past, k_new, v_new, cq_col, ck_past, ck_new)


def _pad_rows(a, rows):
    if a.shape[1] == rows:
        return a
    return jnp.pad(a, ((0, 0), (0, rows - a.shape[1])) + ((0, 0),) * (a.ndim - 2))


def _round_up(n, m):
    return -(-n // m) * m


def _dsa_mixer(x, g, k_past, v_past, ki_past, w, w_out, rel_bias):
    b, t, d = x.shape
    past = k_past.shape[1]
    n_keys = past + t
    if past == 0:
        qt, k4, kb, v4, vt, qit, kw, kwt, ki_t = _proj(
            x, g, w, B_WIDTHS, ((0, "t", BF16, None), (1, B_HD, F32, None), (1, "rows", BF16, None),
                      (2, B_HD, F32, None), (2, "t", BF16, None), (3, "t", BF16, None),
                      (4, "rows", F32, None), (4, "t", F32, None), (4, ("first_t", IDX_DIM), F32, None)),
            jnp.zeros((1, LANE), F32))
        far, near = _bias_tiles(rel_bias, DSA_TILE, DSA_TILE)
        y = _dsa_attention_prompt(qt, qit, kwt, kw, kb, vt, far, near, x, w_out)
        return (y, k4, v4, jnp.swapaxes(ki_t, 1, 2))
    q, k4, v4, qidx, kw = _proj(
        x, g, w, B_WIDTHS, ((0, "rows", BF16, None), (1, B_HD, F32, None), (2, B_HD, F32, None),
                  (3, "rows", BF16, None), (4, "rows", F32, None)),
        jnp.zeros((1, LANE), F32))
    ki = kw[:, :, :IDX_DIM]
    to_rows = lambda a, nh: jnp.swapaxes(a.reshape(b, t, nh, -1), 1, 2).reshape(b, nh * t, -1)
    kidx_all = _pad_rows(jnp.concatenate([ki_past, ki], axis=1), _round_up(n_keys, LANE))
    o = _dsa_attention_cached(to_rows(q, B_HEADS), to_rows(qidx, IDX_HEADS),
                              to_rows(kw[:, :, IDX_DIM:IDX_DIM + IDX_HEADS], IDX_HEADS),
                              jnp.repeat(rel_bias.T, t, axis=0), kidx_all, k_past, v_past, k4, v4)
    o = jnp.swapaxes(o.reshape(b, B_HEADS, t, B_HD), 1, 2).reshape(b * t, d)
    y = _out_proj(x.reshape(b * t, d), o, w_out).reshape(b, t, d)
    return (y, k4, v4, ki)


def _fox_mixer(x, g, k_past, v_past, lf_past, w, b_f, w_out):
    b, t, d = x.shape
    past = k_past.shape[1]
    heads4 = ((1, D_HD, F32, None), (2, D_HD, F32, None))
    if past == 0:
        tq = tk = min(FOX_TILE, t)
        k4, v4, logf_t, qt, kb, vt = _proj(
            x, g, w, D_WIDTHS, heads4 + ((3, ("first_t", D_HEADS), F32, "log_sigmoid"), (0, "t", BF16, None),
                               (1, "rows", BF16, None), (2, "t", BF16, None)), b_f)
        cum_t = _cumsum_lanes(logf_t)
        y = _fox_attention_prompt(qt, kb, vt, cum_t, jnp.swapaxes(cum_t, 1, 2), x, w_out, tq, tk)
        return (y, k4, v4, jnp.swapaxes(logf_t, 1, 2))
    else:
        tk = math.gcd(past, FOX_TILE)
        k4, v4, logf, q = _proj(
            x, g, w, D_WIDTHS, heads4 + ((3, ("first", D_HEADS), F32, "log_sigmoid"), (0, "rows", BF16, None)), b_f)
        lf_all = _pad_rows(jnp.concatenate([lf_past, logf], axis=1), _round_up(past + t, LANE))
        cum = jnp.swapaxes(_cumsum_lanes(jnp.swapaxes(lf_all, 1, 2)), 1, 2)[:, :past + t]
        ck = cum.reshape(b, 1, (past + t) * D_HEADS)
        to_rows = lambda a: jnp.swapaxes(a.reshape(b, t, D_HEADS, -1), 1, 2).reshape(b, D_HEADS * t, -1)
        o = _fox_attention_cached(to_rows(q), k_past, v_past, k4, v4, to_rows(cum[:, past:]),
                                  ck[:, :, :past * D_HEADS], ck[:, :, past * D_HEADS:], tk)
        o = jnp.swapaxes(o.reshape(b, D_HEADS, t, D_HD), 1, 2).reshape(b, t, d)
    y = _out_proj(x.reshape(b * t, d), o.reshape(b * t, d), w_out).reshape(b, t, d)
    return (y, k4, v4, logf)


def _run_group(x, pos0, a_st, b_k, b_v, b_ki, c_st, d_k, d_v, d_lf, mem_k, mem_v, prm):
    b, t, d = x.shape
    depth = prm["norm_mix"].shape[0]
    new = {n: [] for n in ("a", "bk", "bv", "bki", "c", "dk", "dv", "dlf")}
    for i in range(depth):
        kind, j = i % 4, i // 4
        g = prm["norm_mix"][i]
        if kind == 0:
            x, st = _conv_mixer(x, g, prm["a_w_in"][j], prm["a_conv"][j], a_st[j], prm["a_w_out"][j])
            new["a"].append(st)
        elif kind == 1:
            x, kk, vv, ki = _dsa_mixer(x, g, b_k[j], b_v[j], b_ki[j], prm["b_w"][j], prm["b_w_out"][j],
                                       prm["rel_bias"])
            new["bk"].append(kk); new["bv"].append(vv); new["bki"].append(ki)
        elif kind == 2:
            x, st = _pool_mixer(x, g, c_st[j], prm["c_w_group"][j], prm["c_scale"][j], pos0)
            new["c"].append(st)
        else:
            x, kk, vv, lf = _fox_mixer(x, g, d_k[j], d_v[j], d_lf[j], prm["d_w"][j], prm["d_b_f"][j],
                                       prm["d_w_out"][j])
            new["dk"].append(kk); new["dv"].append(vv); new["dlf"].append(lf)
        x = _xattn(x, prm["norm_xattn"], prm["xa_wq"], mem_k, mem_v, prm["xa_wo"], i)
        last = i == depth - 1
        x = _ffn(x.reshape(b * t, d), prm["norm_ffn"], prm["ffn_w1"], prm["ffn_w2"],
                 prm["final_norm"], i, last).reshape(b, t, d)
    return (x,) + tuple(jnp.stack(new[n]) for n in ("a", "bk", "bv", "bki", "c", "dk", "dv", "dlf"))


def kernel(x_prompt, x_sample, state_a_conv, cache_b_k, cache_b_v, cache_b_kidx, state_c_pool,
           cache_d_k, cache_d_v, cache_d_logf, cache_mem_k, cache_mem_v, mem_prompt,
           norm_mix, norm_xattn, norm_mem, norm_ffn, final_norm,
           a_w_in, a_conv, a_w_out, b_w_in, b_w_out, rel_bias, c_w_group, c_scale,
           d_w_in, d_b_f, d_w_out, xa_wq, xa_wkv, xa_wo, ffn_w1, ffn_w2):
    bp, _, d = x_prompt.shape
    n_b, n_d = b_w_in.shape[0], d_w_in.shape[0]
    bf = lambda w: w.astype(BF16)
    d_bf = [jnp.pad(d_b_f[j], (0, LANE - D_HEADS)).reshape(1, LANE) for j in range(n_d)]

    prm = {"norm_mix": norm_mix, "norm_xattn": norm_xattn, "norm_ffn": norm_ffn, "final_norm": final_norm,
           "a_w_in": bf(a_w_in), "a_conv": a_conv, "a_w_out": bf(a_w_out),
           "b_w": b_w_in, "b_w_out": bf(b_w_out), "rel_bias": rel_bias,
           "c_w_group": bf(c_w_group), "c_scale": c_scale,
           "d_w": d_w_in, "d_b_f": d_bf, "d_w_out": bf(d_w_out),
           "xa_wq": xa_wq, "xa_wo": xa_wo, "ffn_w1": ffn_w1, "ffn_w2": ffn_w2}

    mk, mv, mk_rows, mv_rows = _memory_kv(mem_prompt, norm_mem, xa_wkv)

    n_a, n_c = a_w_in.shape[0], c_w_group.shape[0]
    z = lambda *s: jnp.zeros(s, F32)
    gp = _run_group(x_prompt, 0,
                    z(n_a, bp, CONV_W - 1, d),
                    z(n_b, bp, 0, B_KV, B_HD), z(n_b, bp, 0, B_KV, B_HD), z(n_b, bp, 0, IDX_DIM),
                    z(n_c, bp, POOL_STATE, d),
                    z(n_d, bp, 0, D_HEADS, D_HD), z(n_d, bp, 0, D_HEADS, D_HD), z(n_d, bp, 0, D_HEADS),
                    mk_rows, mv_rows, prm)

    past_len = cache_b_k.shape[2]
    gs = _run_group(x_sample, past_len, state_a_conv, cache_b_k, cache_b_v, cache_b_kidx, state_c_pool,
                    cache_d_k, cache_d_v, cache_d_logf, cache_mem_k, cache_mem_v, prm)

    (y_p, a_p, bk_p, bv_p, bki_p, c_p, dk_p, dv_p, dlf_p) = gp
    (y_s, a_s, bk_s, bv_s, bki_s, c_s, dk_s, dv_s, dlf_s) = gs
    return (y_p, y_s, a_p, a_s, bk_p, bv_p, bki_p, bk_s, bv_s, bki_s, c_p, c_s,
            dk_p, dv_p, dlf_p, dk_s, dv_s, dlf_s, mk, mv)
```

```python
import functools
import math

import jax
import jax.numpy as jnp
import numpy as np
from jax import lax
from jax.experimental import pallas as pl
from jax.experimental.pallas import tpu as pltpu

F32 = jnp.float32
BF16 = jnp.bfloat16
I32 = jnp.int32
I16 = jnp.int16

EPS = 1e-6
NEG_INF = -1e30
LOG2E = math.log2(math.e)
CHUNK = 64
LANE = 128
VMEM_LIMIT = 48 * 1024 * 1024

ROW_TILE = 512
WIDE_ROW_TILE = 1024
FF_TILE = 1024
FOX_TILE = 4 * LANE
DSA_TILE = 2 * LANE

CONV_W = 3
POOL_WINDOWS = (2, 4, 8, 16)
POOL_STATE = max(POOL_WINDOWS) - 1
B_HEADS, B_KV, B_HD = 8, 2, 128
B_REP = B_HEADS // B_KV
IDX_HEADS, IDX_DIM = 8, 64
TOPK_MAX = 256
N_BUCKETS, MAX_DIST = 32, 128
D_HEADS, D_HD = 8, 128
MEM_HEADS = 4
B_WIDTHS = (B_HEADS * B_HD, B_KV * B_HD, B_KV * B_HD, IDX_HEADS * IDX_DIM, IDX_DIM + IDX_HEADS)
D_WIDTHS = (D_HEADS * D_HD,) * 3 + (D_HEADS,)
INT_MIN = -2147483648
BIAS_CENTER = 2 * LANE


def _cparams(*sem):
    return pltpu.CompilerParams(dimension_semantics=sem, vmem_limit_bytes=VMEM_LIMIT)


def _dot(a, b):
    return jnp.dot(a.astype(BF16), b.astype(BF16), preferred_element_type=F32)


def _dot_nt(a, b):
    return lax.dot_general(a.astype(BF16), b.astype(BF16), (((1,), (1,)), ((), ())),
                           preferred_element_type=F32)


def _dot_tn(a, b):
    return lax.dot_general(a.astype(BF16), b.astype(BF16), (((0,), (0,)), ((), ())),
                           preferred_element_type=F32)


def _rms(x, g):
    return x * lax.rsqrt(jnp.mean(x * x, axis=-1, keepdims=True) + EPS) * g


def _finish_heads(o_ref, x_ref, w_ref, l_ref, acc_ref, n_heads):
    hd = acc_ref.shape[0] // n_heads
    inv_l = 1.0 / l_ref[...]
    heads_t = jnp.concatenate([acc_ref[h * hd:(h + 1) * hd, :] * inv_l[h:h + 1, :] for h in range(n_heads)],
                              axis=0)
    o_ref[0] = x_ref[0] + _dot_tn(heads_t, w_ref[...])


def _row_tile(n, cap):
    t = min(n, cap)
    assert n % t == 0
    return t


def _memkv_kernel(mem_ref, g_ref, w_ref, k_ref, v_ref, kb_ref, vb_ref):
    bb, nm, d = mem_ref.shape
    m = mem_ref[...].reshape(bb * nm, d)
    mn = m * lax.rsqrt(jnp.mean(m * m, axis=-1, keepdims=True) + EPS)
    h = (mn * g_ref[0]).astype(BF16)
    hd = d // MEM_HEADS
    k = jnp.dot(h, w_ref[0, :, :d].astype(BF16), preferred_element_type=F32)
    v = jnp.dot(h, w_ref[0, :, d:].astype(BF16), preferred_element_type=F32)
    for i in range(bb):
        rows = slice(i * nm, (i + 1) * nm)
        kb_ref[0, i] = k[rows].astype(BF16)
        vb_ref[0, i] = v[rows].astype(BF16)
        k_ref[0, i] = pltpu.einshape("m(hd)->mhd", k[rows], d=hd)
        v_ref[0, i] = pltpu.einshape("m(hd)->mhd", v[rows], d=hd)


def _memory_kv(mem, g_mem, w_kv):
    depth, d = g_mem.shape
    b, nm, _ = mem.shape
    hd = d // MEM_HEADS
    out = jax.ShapeDtypeStruct((depth, b, nm, MEM_HEADS, hd), F32)
    out_b = jax.ShapeDtypeStruct((depth, b, nm, d), BF16)
    bb = math.gcd(b, 2)
    heads_spec = pl.BlockSpec((1, bb, nm, MEM_HEADS, hd), lambda l, i: (l, i, 0, 0, 0))
    rows_spec = pl.BlockSpec((1, bb, nm, d), lambda l, i: (l, i, 0, 0))
    return pl.pallas_call(
        _memkv_kernel,
        grid=(depth, b // bb),
        in_specs=[pl.BlockSpec((bb, nm, d), lambda l, i: (i, 0, 0)),
                  pl.BlockSpec((1, 1, d), lambda l, i: (l, 0, 0)),
                  pl.BlockSpec((1, d, 2 * d), lambda l, i: (l, 0, 0))],
        out_specs=[heads_spec, heads_spec, rows_spec, rows_spec],
        out_shape=[out, out, out_b, out_b],
        compiler_params=_cparams("parallel", "parallel"),
        name="memory_kv",
    )(mem, g_mem.reshape(depth, 1, d), w_kv)


def _ffn_kernel(x_ref, g_ref, w1_ref, w2_ref, gf_ref, o_ref, h_ref, acc_ref, *, final_norm):
    j = pl.program_id(1)

    @pl.when(j == 0)
    def _():
        h_ref[...] = _rms(x_ref[...], g_ref[...]).astype(BF16)
        acc_ref[...] = jnp.zeros_like(acc_ref)

    u = jnp.maximum(jnp.dot(h_ref[...], w1_ref[...].astype(BF16), preferred_element_type=F32), 0.0)
    acc_ref[...] += jnp.dot((u * u).astype(BF16), w2_ref[...].astype(BF16), preferred_element_type=F32)

    @pl.when(j == pl.num_programs(1) - 1)
    def _():
        y = x_ref[...] + acc_ref[...]
        o_ref[...] = _rms(y, gf_ref[...]) if final_norm else y


def _ffn(x, g, w1, w2, gf, layer, final_norm):
    n, d = x.shape
    f = w1.shape[2]
    tm = _row_tile(n, WIDE_ROW_TILE)
    tf = FF_TILE
    return pl.pallas_call(
        functools.partial(_ffn_kernel, final_norm=final_norm),
        grid=(n // tm, f // tf),
        in_specs=[pl.BlockSpec((tm, d), lambda i, j: (i, 0)),
                  pl.BlockSpec((None, 1, d), lambda i, j: (layer, 0, 0)),
                  pl.BlockSpec((None, d, tf), lambda i, j: (layer, 0, j)),
                  pl.BlockSpec((None, tf, d), lambda i, j: (layer, j, 0)),
                  pl.BlockSpec((1, d), lambda i, j: (0, 0))],
        out_specs=pl.BlockSpec((tm, d), lambda i, j: (i, 0)),
        out_shape=jax.ShapeDtypeStruct((n, d), F32),
        scratch_shapes=[pltpu.VMEM((tm, d), BF16), pltpu.VMEM((tm, d), F32)],
        compiler_params=_cparams("parallel", "arbitrary"),
        name="ffn",
    )(x, g.reshape(-1, 1, d), w1, w2, gf.reshape(1, d))


def _xattn_kernel(x_ref, g_ref, wq_ref, mk_ref, mv_ref, wo_ref, o_ref):
    x = x_ref[0]
    d = x.shape[-1]
    hd = d // MEM_HEADS
    h = _rms(x, g_ref[...]).astype(BF16)
    q = jnp.dot(h, wq_ref[...].astype(BF16), preferred_element_type=F32)
    outs = []
    if len(mk_ref.shape) == 3:
        mk = pltpu.einshape("mhd->m(hd)", mk_ref[...]).astype(BF16)
        mv = pltpu.einshape("mhd->m(hd)", mv_ref[...]).astype(BF16)
    else:
        mk, mv = mk_ref[...], mv_ref[...]
    for hh in range(MEM_HEADS):
        sl = slice(hh * hd, (hh + 1) * hd)
        kh, vh = mk[:, sl], mv[:, sl]
        s = _dot_nt(q[:, sl], kh) * (hd ** -0.5)
        m = jnp.max(s, axis=-1, keepdims=True)
        p = jnp.exp(s - m)
        l = jnp.sum(p, axis=-1, keepdims=True)
        outs.append(_dot(p, vh) / l)
    o = jnp.concatenate(outs, axis=-1)
    o_ref[0] = x + _dot(o, wo_ref[...])


def _xattn(x, g, wq, mk, mv, wo, layer):
    b, t, d = x.shape
    tm = _row_tile(t, WIDE_ROW_TILE)
    kv_spec = pl.BlockSpec((None, None) + mk.shape[2:], lambda i, j: (layer, i) + (0,) * (mk.ndim - 2))
    return pl.pallas_call(
        _xattn_kernel,
        grid=(b, t // tm),
        in_specs=[pl.BlockSpec((1, tm, d), lambda i, j: (i, j, 0)),
                  pl.BlockSpec((None, 1, d), lambda i, j: (layer, 0, 0)),
                  pl.BlockSpec((None, d, d), lambda i, j: (layer, 0, 0)),
                  kv_spec, kv_spec,
                  pl.BlockSpec((None, d, d), lambda i, j: (layer, 0, 0))],
        out_specs=pl.BlockSpec((1, tm, d), lambda i, j: (i, j, 0)),
        out_shape=jax.ShapeDtypeStruct((b, t, d), F32),
        compiler_params=_cparams("parallel", "parallel"),
        name="xattn",
    )(x, g.reshape(-1, 1, d), wq, mk, mv, wo)


def _conv_kernel(x_ref, g_ref, win_ref, wc_ref, st_ref, wout_ref, o_ref, nst_ref, z_ref):
    t = pl.program_id(1)
    x = x_ref[0]
    tm, d = x.shape
    pad = 8

    @pl.when(t == 0)
    def _():
        z_ref[pad - 2:pad, :] = st_ref[0]

    h = _rms(x, g_ref[...]).astype(BF16)
    bg = jnp.dot(h, win_ref[:, 0:d], preferred_element_type=F32)
    cg = jnp.dot(h, win_ref[:, d:2 * d], preferred_element_type=F32)
    u = jnp.dot(h, win_ref[:, 2 * d:3 * d], preferred_element_type=F32)
    z = cg * u
    z_ref[pad:pad + tm, :] = z
    conv = (z_ref[pad - 2:pad - 2 + tm, :] * wc_ref[0:1, :]
            + z_ref[pad - 1:pad - 1 + tm, :] * wc_ref[1:2, :]
            + z * wc_ref[2:3, :])
    o_ref[0] = x + _dot(bg * conv, wout_ref[...])
    last = z_ref[pad + tm - 2:pad + tm, :]
    z_ref[pad - 2:pad, :] = last

    @pl.when(t == pl.num_programs(1) - 1)
    def _():
        nst_ref[0] = last


def _conv_mixer(x, g, w_in, w_conv, state, w_out):
    b, t, d = x.shape
    tm = _row_tile(t, WIDE_ROW_TILE)
    once = pl.Buffered(1)
    return pl.pallas_call(
        _conv_kernel,
        grid=(b, t // tm),
        in_specs=[pl.BlockSpec((1, tm, d), lambda i, j: (i, j, 0)),
                  pl.BlockSpec((1, d), lambda i, j: (0, 0)),
                  pl.BlockSpec((d, 3 * d), lambda i, j: (0, 0), pipeline_mode=once),
                  pl.BlockSpec((CONV_W, d), lambda i, j: (0, 0)),
                  pl.BlockSpec((1, CONV_W - 1, d), lambda i, j: (i, 0, 0)),
                  pl.BlockSpec((d, d), lambda i, j: (0, 0), pipeline_mode=once)],
        out_specs=[pl.BlockSpec((1, tm, d), lambda i, j: (i, j, 0)),
                   pl.BlockSpec((1, CONV_W - 1, d), lambda i, j: (i, 0, 0))],
        out_shape=[jax.ShapeDtypeStruct((b, t, d), F32),
                   jax.ShapeDtypeStruct((b, CONV_W - 1, d), F32)],
        scratch_shapes=[pltpu.VMEM((tm + 8, d), F32)],
        compiler_params=_cparams("parallel", "arbitrary"),
        name="conv_mixer",
    )(x, g.reshape(1, d), w_in, w_conv, state, w_out)


def _pool_kernel(x_ref, g_ref, st_ref, wg_ref, sc_ref, o_ref, nst_ref, h_ref, *s_refs, pos0):
    t = pl.program_id(1)
    x = x_ref[0]
    tm, d = x.shape
    n_lv = len(POOL_WINDOWS)
    gw = d // n_lv
    base = 2 * (POOL_STATE + 1)
    lead = base - POOL_STATE
    end = base + tm

    @pl.when(t == 0)
    def _():
        h_ref[0:lead, :] = jnp.zeros((lead, d), F32)
        h_ref[lead:base, :] = st_ref[0]

    h = _rms(x, g_ref[...])
    h_ref[base:end, :] = h
    pos = pos0 + t * tm + lax.broadcasted_iota(I32, (tm, gw), 0)
    ys = []
    prev, c_prev = h_ref, 0
    for lv in range(1, n_lv + 1):
        w, shift, start = POOL_WINDOWS[lv - 1], 2 ** (lv - 1), 8 * lv
        c0 = (lv - 1) * gw
        cols = slice(c0 - c_prev, d - c_prev)
        cur = prev[start:end, cols] + prev[start - shift:end - shift, cols]
        if lv < n_lv:
            s_refs[lv - 1][start:end, :] = cur[:, gw:]
        win = cur[base - start:, :gw]
        count = jnp.minimum(w, pos + 1).astype(F32)
        dlt = win / count - h[:, c0:c0 + gw]
        ys.append(_dot(dlt, wg_ref[lv - 1]))
        if lv < n_lv:
            prev, c_prev = s_refs[lv - 1], c0 + gw
    y = jnp.concatenate(ys, axis=-1) * sc_ref[...]
    o_ref[0] = x + y
    last = h_ref[end - POOL_STATE:end, :]
    h_ref[lead:base, :] = last

    @pl.when(t == pl.num_programs(1) - 1)
    def _():
        nst_ref[0] = last


def _pool_mixer(x, g, state, w_group, scale, pos0):
    b, t, d = x.shape
    ng, gw, _ = w_group.shape
    tm = _row_tile(t, ROW_TILE)
    assert POOL_WINDOWS == tuple(2 ** (lv + 1) for lv in range(ng)) and tm >= POOL_STATE
    rows = tm + 2 * (POOL_STATE + 1)
    return pl.pallas_call(
        functools.partial(_pool_kernel, pos0=pos0),
        grid=(b, t // tm),
        in_specs=[pl.BlockSpec((1, tm, d), lambda i, j: (i, j, 0)),
                  pl.BlockSpec((1, d), lambda i, j: (0, 0)),
                  pl.BlockSpec((1, POOL_STATE, d), lambda i, j: (i, 0, 0)),
                  pl.BlockSpec((ng, gw, gw), lambda i, j: (0, 0, 0)),
                  pl.BlockSpec((1, d), lambda i, j: (0, 0))],
        out_specs=[pl.BlockSpec((1, tm, d), lambda i, j: (i, j, 0)),
                   pl.BlockSpec((1, POOL_STATE, d), lambda i, j: (i, 0, 0))],
        out_shape=[jax.ShapeDtypeStruct((b, t, d), F32),
                   jax.ShapeDtypeStruct((b, POOL_STATE, d), F32)],
        scratch_shapes=[pltpu.VMEM((rows, d - lv * gw), F32) for lv in range(ng)],
        compiler_params=_cparams("parallel", "arbitrary"),
        name="pool_mixer",
    )(x, g.reshape(1, d), state, w_group, scale.reshape(1, d))


def _proj_kernel(x_ref, g_ref, w_ref, e_ref, *o_refs, widths, outs):
    h = _rms(x_ref[0], g_ref[...]).astype(BF16)
    starts = [sum(widths[:i]) for i in range(len(widths))]
    ys = {}
    for (wi, mode, _, ep), o_ref in zip(outs, o_refs):
        if wi not in ys:
            w = w_ref[:, starts[wi]:starts[wi] + widths[wi]].astype(BF16)
            if widths[wi] % LANE:
                w = jnp.concatenate([w, jnp.zeros((w.shape[0], -widths[wi] % LANE), BF16)], axis=1)
            ys[wi] = jnp.dot(h, w, preferred_element_type=F32)
        y = ys[wi]
        if ep == "log_sigmoid":
            u = -(y + e_ref[...])
            y = -(jnp.maximum(u, 0.0) + jnp.log1p(jnp.exp(-jnp.abs(u))))
        if mode == "rows":
            o_ref[0] = y.astype(o_ref.dtype)
        elif isinstance(mode, tuple) and mode[0] == "first":
            o_ref[0] = y[:, :mode[1]].astype(o_ref.dtype)
        elif isinstance(mode, tuple):
            o_ref[0] = jnp.transpose(y)[:mode[1], :].astype(o_ref.dtype)
        elif mode == "t":
            o_ref[0] = jnp.transpose(y).astype(o_ref.dtype)
        else:
            o_ref[0] = pltpu.einshape("m(hd)->mhd", y.astype(o_ref.dtype), d=mode)


def _proj(x, g, w, widths, outs, extra):
    b, t, d = x.shape
    assert sum(widths) == w.shape[1]
    tm = _row_tile(t, ROW_TILE)
    in_specs = [pl.BlockSpec((1, tm, d), lambda i, j: (i, j, 0)),
                pl.BlockSpec((1, d), lambda i, j: (0, 0)),
                pl.BlockSpec(w.shape, lambda i, j: (0, 0), pipeline_mode=pl.Buffered(1)),
                pl.BlockSpec(extra.shape, lambda i, j: (0, 0))]
    out_specs, out_shape = [], []
    for wi, mode, dt, _ in outs:
        n = _round_up(widths[wi], LANE)
        if isinstance(mode, tuple):
            mode, n = ("rows" if mode[0] == "first" else "t"), mode[1]
        if mode == "rows":
            out_specs.append(pl.BlockSpec((1, tm, n), lambda i, j: (i, j, 0)))
            out_shape.append(jax.ShapeDtypeStruct((b, t, n), dt))
        elif mode == "t":
            out_specs.append(pl.BlockSpec((1, n, tm), lambda i, j: (i, 0, j)))
            out_shape.append(jax.ShapeDtypeStruct((b, n, t), dt))
        else:
            out_specs.append(pl.BlockSpec((1, tm, n // mode, mode), lambda i, j: (i, j, 0, 0)))
            out_shape.append(jax.ShapeDtypeStruct((b, t, n // mode, mode), dt))
    return pl.pallas_call(
        functools.partial(_proj_kernel, widths=tuple(widths), outs=tuple(outs)),
        grid=(b, t // tm),
        in_specs=in_specs,
        out_specs=out_specs,
        out_shape=out_shape,
        compiler_params=_cparams("parallel", "parallel"),
        name="norm_proj",
    )(x, g.reshape(1, d), w, extra)


def _outproj_kernel(x_ref, a_ref, w_ref, o_ref):
    o_ref[...] = x_ref[...] + _dot(a_ref[...], w_ref[...])


def _out_proj(x, a, w):
    n, d = x.shape
    tm = _row_tile(n, ROW_TILE)
    return pl.pallas_call(
        _outproj_kernel,
        grid=(n // tm,),
        in_specs=[pl.BlockSpec((tm, d), lambda i: (i, 0)),
                  pl.BlockSpec((tm, d), lambda i: (i, 0)),
                  pl.BlockSpec((d, d), lambda i: (0, 0))],
        out_specs=pl.BlockSpec((tm, d), lambda i: (i, 0)),
        out_shape=jax.ShapeDtypeStruct((n, d), F32),
        compiler_params=_cparams("parallel"),
        name="out_proj",
    )(x, a, w)


def _bias_table_kernel(rbt_ref, o_ref):
    width = o_ref.shape[-1]
    rel = BIAS_CENTER - lax.broadcasted_iota(I32, (1, width), 1)
    nb = N_BUCKETS // 2
    max_exact = nb // 2
    ret = (rel > 0).astype(I32) * nb
    n = jnp.abs(rel)
    nf = jnp.maximum(n, 1).astype(F32)
    large = max_exact + (jnp.log(nf / max_exact) / math.log(MAX_DIST / max_exact)
                         * (nb - max_exact)).astype(I32)
    large = jnp.minimum(large, nb - 1)
    bucket = ret + jnp.where(n < max_exact, n, large)
    acc = jnp.zeros(o_ref.shape, F32)
    for j in range(N_BUCKETS):
        acc = jnp.where(bucket == j, rbt_ref[:, j:j + 1], acc)
    o_ref[...] = acc * LOG2E


def _bias_tiles_kernel(rbt_ref, far_ref, near_ref, tab_ref):
    n_d, nh, kb, tq = near_ref.shape
    _bias_table_kernel(rbt_ref, tab_ref)
    far_ref[...] = jnp.broadcast_to(tab_ref[:, BIAS_CENTER + MAX_DIST:BIAS_CENTER + MAX_DIST + 1], far_ref.shape)
    for dd in range(n_d):
        s0 = BIAS_CENTER - (dd - 1) * kb - kb
        for h in range(nh):
            rows = jnp.broadcast_to(tab_ref[h:h + 1, s0:s0 + tq + kb], (kb, tq + kb))
            near_ref[dd, h] = pltpu.roll(rows, 0, 1, stride=1, stride_axis=0)[:, kb:]


def _bias_tiles(rel_bias, kb, tq):
    nh = rel_bias.shape[1]
    n_d = tq // kb + 1
    width = BIAS_CENTER + kb + tq + kb
    assert kb % LANE == 0 and tq % kb == 0 and kb >= MAX_DIST and BIAS_CENTER >= tq
    return pl.pallas_call(
        _bias_tiles_kernel,
        out_shape=[jax.ShapeDtypeStruct((nh, LANE), F32), jax.ShapeDtypeStruct((n_d, nh, kb, tq), F32)],
        scratch_shapes=[pltpu.VMEM((nh, width), F32)],
        name="bias_tiles",
    )(rel_bias.T)


def _sortable(x):
    x = jnp.where(x == 0.0, 0.0, x)
    bits = lax.bitcast_convert_type(x, I32)
    return jnp.where(bits < 0, bits ^ 0x7FFFFFFF, bits)


def _neg_inf_key():
    return int(np.float32(NEG_INF).view(np.int32)) ^ 0x7FFFFFFF


def _dsa_prompt_kernel(qt_ref, qit_ref, kwt_ref, kw_ref, k_ref, vt_ref, far_ref, near_ref, x_ref, wout_ref, o_ref,
                       key_ref, hi_ref, lo_ref, sel_ref, m_ref, l_ref, acc_ref, a_ref, *, top_k):
    qi = pl.program_id(1)
    tq = qt_ref.shape[2]
    n_keys = kw_ref.shape[1]
    kb_sz = LANE
    q0 = qi * tq
    nkb = jnp.minimum(n_keys, q0 + tq) // kb_sz
    negkey = _neg_inf_key()

    qlane = lax.broadcasted_iota(I32, (1, tq), 1)
    lim = ((q0 + qlane) // CHUNK + 1) * CHUNK
    krow = lax.broadcasted_iota(I32, (kb_sz, tq), 0)

    def kslice(kb):
        return pl.ds(pl.multiple_of(kb * kb_sz, kb_sz), kb_sz)

    sb = 2 * kb_sz
    srow = lax.broadcasted_iota(I32, (sb, tq), 0)

    def score_body(i, c):
        rows = pl.ds(pl.multiple_of(i * sb, sb), sb)
        kid = kw_ref[0, rows, :][:, :IDX_DIM].astype(BF16)
        sc = jnp.zeros((sb, tq), F32)
        for h in range(IDX_HEADS):
            s = jnp.dot(kid, qit_ref[0, h * IDX_DIM:(h + 1) * IDX_DIM, :], preferred_element_type=F32)
            sc = sc + jnp.maximum(s, 0.0) * kwt_ref[0, IDX_DIM + h:IDX_DIM + h + 1, :]
        sc = sc * ((IDX_DIM * IDX_HEADS) ** -0.5)
        key = jnp.where(i * sb + srow < lim, _sortable(sc), negkey)
        key_ref[rows, :] = key
        hi_ref[rows, :] = (key >> 16).astype(I16)
        lo_ref[rows, :] = ((key & 0xFFFF) - 0x8000).astype(I16)
        return c

    lax.fori_loop(0, nkb // 2, score_body, 0)

    def search16(ref):
        def bit_body(i, t_u):
            cand_u = t_u | jnp.left_shift(jnp.int32(1), 15 - i)
            cand = (cand_u - 0x8000).astype(I16)

            def body(j, a):
                ind = jnp.where(ref[pl.ds(pl.multiple_of(j * sb, sb), sb), :] >= cand,
                                jnp.ones((), I16), jnp.zeros((), I16))
                parts = [ind[16 * r:16 * (r + 1), :] for r in range(sb // 16)]
                while len(parts) > 1:
                    parts = [parts[r] + parts[r + 1] for r in range(0, len(parts), 2)]
                return a + parts[0]
            a = lax.fori_loop(0, nkb // 2, body, jnp.zeros((16, tq), I16))
            cnt = jnp.sum(a.astype(I32), axis=0, keepdims=True)
            return jnp.where(cnt >= top_k, cand_u, t_u)
        return lax.fori_loop(0, 16, bit_body, jnp.zeros((1, tq), I32))

    def count(pred_fn):
        def body(i, a):
            for u in range(2):
                kb = 2 * i + u
                ind = pred_fn(kb, key_ref[kslice(kb), :])
                a = a + jnp.sum(ind.reshape(kb_sz // 8, 8, tq), axis=0)
            return a
        a = lax.fori_loop(0, nkb // 2, body, jnp.zeros((8, tq), I32))
        return jnp.sum(a, axis=0, keepdims=True)

    t_hi = search16(hi_ref)
    t_hi16 = (t_hi - 0x8000).astype(I16)

    def lo_body(j, c):
        rows = pl.ds(pl.multiple_of(j * sb, sb), sb)
        hi = hi_ref[rows, :]
        lo_ref[rows, :] = jnp.where(hi == t_hi16, lo_ref[rows, :],
                                    jnp.where(hi > t_hi16, jnp.full((), 0x7FFF, I16), jnp.full((), -0x8000, I16)))
        return c

    lax.fori_loop(0, nkb // 2, lo_body, 0)
    t_s = (jnp.left_shift(t_hi, 16) | search16(lo_ref)) ^ INT_MIN

    def adm01(kb):
        return jnp.where(kb * kb_sz + krow < lim, 1.0, 0.0)

    def cap(sel):
        return jnp.where(sel != 0.0, jnp.inf, NEG_INF)

    def sel_body(kb, a):
        sel = jnp.where(key_ref[kslice(kb), :] >= t_s, adm01(kb), 0.0)
        sel_ref[kslice(kb), :] = cap(sel)
        return a + jnp.sum(sel.reshape(kb_sz // 8, 8, tq), axis=0)

    n_sel = jnp.sum(lax.fori_loop(0, nkb, sel_body, jnp.zeros((8, tq), F32)), axis=0, keepdims=True)

    @pl.when(jnp.max(n_sel) > top_k)
    def _():
        n_gt = count(lambda kb, key: jnp.where(key > t_s, 1, 0))
        need = (top_k - n_gt).astype(F32)
        r = lax.broadcasted_iota(I32, (kb_sz, kb_sz), 0)
        c = lax.broadcasted_iota(I32, (kb_sz, kb_sz), 1)
        ltri = jnp.where(c < r, 1.0, 0.0).astype(BF16)

        def tie_body(kb, carry):
            key = key_ref[kslice(kb), :]
            adm = adm01(kb)
            eq = jnp.where(key == t_s, adm, 0.0)
            rank = carry + jnp.dot(ltri, eq.astype(BF16), preferred_element_type=F32)
            keep = jnp.where(rank < need, eq, 0.0)
            sel_ref[kslice(kb), :] = cap(jnp.where(key > t_s, adm, keep))
            return carry + jnp.sum(eq, axis=0, keepdims=True)

        lax.fori_loop(0, nkb, tie_body, jnp.zeros((1, tq), F32))

    m_ref[...] = jnp.full(m_ref.shape, NEG_INF, F32)
    l_ref[...] = jnp.zeros(l_ref.shape, F32)
    acc_ref[...] = jnp.zeros(acc_ref.shape, F32)
    c1 = (B_HD ** -0.5) * LOG2E

    def attend(k0, nk, bias2_fn):
        rows = pl.ds(pl.multiple_of(k0, LANE), nk)
        sel_cap = sel_ref[rows, :]
        ks = k_ref[0, rows, :]
        cols = []
        for h in range(B_HEADS):
            g = h // B_REP
            z = jnp.dot(ks[:, g * B_HD:(g + 1) * B_HD], qt_ref[0, h * B_HD:(h + 1) * B_HD, :],
                        preferred_element_type=F32)
            a = jnp.minimum(z * c1 + bias2_fn(h), sel_cap)
            a_ref[h, 0:nk, :] = a
            cols.append(jnp.max(a, axis=0, keepdims=True))
        m_old = m_ref[...]
        m_new = jnp.maximum(m_old, jnp.concatenate(cols, axis=0))
        alpha = jnp.exp2(m_old - m_new)
        m_ref[...] = m_new
        sums = []
        ones = jnp.ones((8, nk), BF16)
        for h in range(B_HEADS):
            g = h // B_REP
            p = jnp.exp2(a_ref[h, 0:nk, :] - m_new[h:h + 1, :]).astype(BF16)
            sums.append(jnp.dot(ones, p, preferred_element_type=F32)[0:1, :])
            hs = slice(h * B_HD, (h + 1) * B_HD)
            pv = jnp.dot(vt_ref[0, g * B_HD:(g + 1) * B_HD, rows], p, preferred_element_type=F32)
            acc_ref[hs, :] = alpha[h:h + 1, :] * acc_ref[hs, :] + pv
        l_ref[...] = alpha * l_ref[...] + jnp.concatenate(sums, axis=0)

    ab = near_ref.shape[2]
    n_far = jnp.maximum(q0 // ab - 1, 0)
    far_bias2 = far_ref[:, 0:1]

    def far_body(i, c):
        attend(i * ab, ab, lambda h: far_bias2[h:h + 1, :])
        return c

    lax.fori_loop(0, n_far, far_body, 0)

    def near_body(i, c):
        dd = i - q0 // ab + 1
        attend(i * ab, ab, lambda h: near_ref[dd, h])
        return c

    lax.fori_loop(n_far, nkb * kb_sz // ab, near_body, 0)

    _finish_heads(o_ref, x_ref, wout_ref, l_ref, acc_ref, B_HEADS)


def _dsa_attention_prompt(qt, qit, kwt, kw, k, vt, far, near, x, w_out):
    b, d, t = qt.shape
    tq = near.shape[3]
    top_k = min(TOPK_MAX, t // 4)
    ab = near.shape[2]
    assert t % tq == 0 and tq % CHUNK == 0 and tq % ab == 0 and ab % LANE == 0 and tq % (2 * LANE) == 0
    return pl.pallas_call(
        functools.partial(_dsa_prompt_kernel, top_k=top_k),
        grid=(b, t // tq),
        in_specs=[pl.BlockSpec((1, d, tq), lambda i, j: (i, 0, j)),
                  pl.BlockSpec((1, qit.shape[1], tq), lambda i, j: (i, 0, j)),
                  pl.BlockSpec((1, LANE, tq), lambda i, j: (i, 0, j)),
                  pl.BlockSpec((1, t, LANE), lambda i, j: (i, 0, 0)),
                  pl.BlockSpec((1, t, B_KV * B_HD), lambda i, j: (i, 0, 0)),
                  pl.BlockSpec((1, B_KV * B_HD, t), lambda i, j: (i, 0, 0)),
                  pl.BlockSpec(far.shape, lambda i, j: (0, 0)),
                  pl.BlockSpec(near.shape, lambda i, j: (0, 0, 0, 0)),
                  pl.BlockSpec((1, tq, d), lambda i, j: (i, j, 0)),
                  pl.BlockSpec((d, d), lambda i, j: (0, 0))],
        out_specs=pl.BlockSpec((1, tq, d), lambda i, j: (i, j, 0)),
        out_shape=jax.ShapeDtypeStruct((b, t, d), F32),
        scratch_shapes=[pltpu.VMEM((t, tq), I32), pltpu.VMEM((t, tq), I16), pltpu.VMEM((t, tq), I16),
                        pltpu.VMEM((t, tq), F32),
                        pltpu.VMEM((B_HEADS, tq), F32), pltpu.VMEM((B_HEADS, tq), F32),
                        pltpu.VMEM((d, tq), F32), pltpu.VMEM((B_HEADS, ab, tq), F32)],
        compiler_params=_cparams("parallel", "parallel"),
        name="dsa_attention",
    )(qt, qit, kwt, kw, k, vt, far, near, x, w_out)


def _dsa_cached_kernel(q_ref, qi_ref, wi_ref, rb_ref, kidx_ref, kp_ref, vp_ref, kn_ref, vn_ref, o_ref,
                       *, past, t_new, top_k):
    n_keys = past + t_new
    lp = kidx_ref.shape[1]
    negkey = _neg_inf_key()
    kpos = lax.broadcasted_iota(I32, (t_new, lp), 1)
    qpos = past + lax.broadcasted_iota(I32, (t_new, lp), 0)
    adm = kpos < (qpos // CHUNK + 1) * CHUNK

    s = _dot_nt(qi_ref[0], kidx_ref[0])
    w = jnp.maximum(s, 0.0) * wi_ref[0]
    sc = w[0:t_new]
    for h in range(1, IDX_HEADS):
        sc = sc + w[h * t_new:(h + 1) * t_new]
    sc = sc * ((IDX_DIM * IDX_HEADS) ** -0.5)
    key = jnp.where(adm, _sortable(sc), negkey)
    key = jnp.where(kpos < n_keys, key, INT_MIN)

    def bit_body(i, t_u):
        cand_u = t_u | jnp.left_shift(jnp.int32(1), 31 - i)
        cnt = jnp.sum(jnp.where(key >= (cand_u ^ INT_MIN), 1.0, 0.0), axis=1, keepdims=True)
        return jnp.where(cnt >= top_k, cand_u, t_u)

    t_s = lax.fori_loop(0, 32, bit_body, jnp.zeros((t_new, 1), I32)) ^ INT_MIN

    adm01 = jnp.where(adm, 1.0, 0.0)
    gt = jnp.where(key > t_s, adm01, 0.0)
    eq = jnp.where(key == t_s, adm01, 0.0)
    need = top_k - jnp.sum(jnp.where(key > t_s, 1.0, 0.0), axis=1, keepdims=True)
    r = lax.broadcasted_iota(I32, (LANE, LANE), 0)
    c = lax.broadcasted_iota(I32, (LANE, LANE), 1)
    utri = jnp.where(r < c, 1.0, 0.0).astype(BF16)
    carry = jnp.zeros((t_new, 1), F32)
    keeps = []
    for blk in range(lp // LANE):
        e = eq[:, blk * LANE:(blk + 1) * LANE]
        rank = carry + jnp.dot(e.astype(BF16), utri, preferred_element_type=F32)
        keeps.append(jnp.where(rank < need, e, 0.0))
        carry = carry + jnp.sum(e, axis=1, keepdims=True)
    sel = gt + jnp.concatenate(keeps, axis=1)

    near = max(past - MAX_DIST, 0) // LANE * LANE
    rel = (kpos - qpos)[:, near:]
    nb = N_BUCKETS // 2
    max_exact = nb // 2
    n = jnp.abs(rel)
    nf = jnp.maximum(n, 1).astype(F32)
    large = max_exact + (jnp.log(nf / max_exact) / math.log(MAX_DIST / max_exact)
                         * (nb - max_exact)).astype(I32)
    bucket = (rel > 0).astype(I32) * nb + jnp.where(n < max_exact, n, jnp.minimum(large, nb - 1))

    rows = B_REP * t_new
    sel_g = jnp.concatenate([sel] * B_REP, axis=0) != 0.0
    bucket_g = jnp.concatenate([bucket] * B_REP, axis=0)
    for g in range(B_KV):
        grp = lambda ref, n: ref[0, pl.ds(g, n, stride=B_KV), :]
        qg = q_ref[0, g * rows:(g + 1) * rows, :]
        rb = rb_ref[g * rows:(g + 1) * rows, :]
        bias_near = jnp.zeros((rows, lp - near), F32)
        for j in range(N_BUCKETS):
            bias_near = jnp.where(bucket_g == j, rb[:, j:j + 1], bias_near)
        bias = jnp.concatenate([jnp.broadcast_to(rb[:, nb - 1:nb], (rows, near)), bias_near], axis=1)
        zp = _dot_nt(qg, grp(kp_ref, past)) * (B_HD ** -0.5)
        zn = _dot_nt(qg, grp(kn_ref, t_new)) * (B_HD ** -0.5)
        ap = jnp.where(sel_g[:, :past], zp + bias[:, :past], NEG_INF)
        an = jnp.where(sel_g[:, past:n_keys], zn + bias[:, past:n_keys], NEG_INF)
        m = jnp.maximum(jnp.max(ap, axis=1, keepdims=True), jnp.max(an, axis=1, keepdims=True))
        pp, pn = jnp.exp(ap - m), jnp.exp(an - m)
        l = jnp.sum(pp, axis=1, keepdims=True) + jnp.sum(pn, axis=1, keepdims=True)
        o_ref[0, g * rows:(g + 1) * rows, :] = (_dot(pp, grp(vp_ref, past)) + _dot(pn, grp(vn_ref, t_new))) / l


def _dsa_attention_cached(q_rows, qi_rows, wi_col, rb_rows, kidx_all, k_past, v_past, k_new, v_new):
    b, rows, hd = q_rows.shape
    past, t_new = k_past.shape[1], k_new.shape[1]
    lp = kidx_all.shape[1]
    top_k = min(TOPK_MAX, (past + t_new) // 4)
    assert past % LANE == 0
    flat = lambda a: a.reshape(b, a.shape[1] * B_KV, B_HD)
    kv_spec = lambda n: pl.BlockSpec((1, n * B_KV, B_HD), lambda i: (i, 0, 0))
    return pl.pallas_call(
        functools.partial(_dsa_cached_kernel, past=past, t_new=t_new, top_k=top_k),
        grid=(b,),
        in_specs=[pl.BlockSpec((1, rows, hd), lambda i: (i, 0, 0)),
                  pl.BlockSpec((1,) + qi_rows.shape[1:], lambda i: (i, 0, 0)),
                  pl.BlockSpec((1,) + wi_col.shape[1:], lambda i: (i, 0, 0)),
                  pl.BlockSpec(rb_rows.shape, lambda i: (0, 0)),
                  pl.BlockSpec((1, lp, IDX_DIM), lambda i: (i, 0, 0)),
                  kv_spec(past), kv_spec(past), kv_spec(t_new), kv_spec(t_new)],
        out_specs=pl.BlockSpec((1, rows, hd), lambda i: (i, 0, 0)),
        out_shape=jax.ShapeDtypeStruct((b, rows, hd), F32),
        compiler_params=_cparams("parallel"),
        name="dsa_attention_cached",
    )(q_rows, qi_rows, wi_col, rb_rows, kidx_all, flat(k_past), flat(v_past), flat(k_new), flat(v_new))


def _cumsum_kernel(x_ref, o_ref):
    x = x_ref[0]
    n = x.shape[-1]
    lane = lax.broadcasted_iota(I32, x.shape, 1)
    s = 1
    while s < n:
        x = x + jnp.where(lane >= s, pltpu.roll(x, s, 1), 0.0)
        s *= 2
    o_ref[0] = x


def _cumsum_lanes(x):
    b, h, n = x.shape
    return pl.pallas_call(
        _cumsum_kernel,
        grid=(b,),
        in_specs=[pl.BlockSpec((1, h, n), lambda i: (i, 0, 0))],
        out_specs=pl.BlockSpec((1, h, n), lambda i: (i, 0, 0)),
        out_shape=jax.ShapeDtypeStruct((b, h, n), F32),
        compiler_params=_cparams("parallel"),
        name="logf_cumsum",
    )(x)


def _fox_init(m_ref, l_ref, acc_ref):
    m_ref[...] = jnp.full(m_ref.shape, NEG_INF, F32)
    l_ref[...] = jnp.zeros(l_ref.shape, F32)
    acc_ref[...] = jnp.zeros(acc_ref.shape, F32)


def _fox_tile(z_fn, pv_fn, cq, ck, cap, m_ref, l_ref, acc_ref, a_ref):
    c1 = (D_HD ** -0.5) * LOG2E
    cq2, ck2 = cq * LOG2E, ck * LOG2E
    cols = []
    for h in range(D_HEADS):
        a = z_fn(h) * c1 - ck2[:, h:h + 1]
        if cap is not None:
            a = jnp.minimum(a, cap)
        a_ref[h] = a
        cols.append(jnp.max(a, axis=0, keepdims=True))
    m_old = m_ref[...]
    m_new = jnp.maximum(m_old, jnp.concatenate(cols, axis=0) + cq2)
    alpha = jnp.exp2(m_old - m_new)
    shift = m_new - cq2
    m_ref[...] = m_new
    sums = []
    ones = jnp.ones((8, a_ref.shape[1]), BF16)
    for h in range(D_HEADS):
        p = jnp.exp2(a_ref[h] - shift[h:h + 1, :]).astype(BF16)
        sums.append(jnp.dot(ones, p, preferred_element_type=F32)[0:1, :])
        hs = slice(h * D_HD, (h + 1) * D_HD)
        acc_ref[hs, :] = alpha[h:h + 1, :] * acc_ref[hs, :] + pv_fn(h, p)
    l_ref[...] = alpha * l_ref[...] + jnp.concatenate(sums, axis=0)


def _hs(h):
    return slice(h * D_HD, (h + 1) * D_HD)


def _fox_prompt_kernel(qt_ref, k_ref, vt_ref, cq_ref, ck_ref, x_ref, wout_ref, o_ref,
                       m_ref, l_ref, acc_ref, a_ref):
    qi, step = pl.program_id(1), pl.program_id(2)
    tq, tk = qt_ref.shape[2], a_ref.shape[1]
    q0 = qi * tq
    ki = step - (pl.num_programs(2) - 1 - (q0 + tq - 1) // tk)
    k0 = ki * tk
    keys = pl.ds(pl.multiple_of(jnp.maximum(k0, 0), tk), tk)

    @pl.when(step == 0)
    def _():
        _fox_init(m_ref, l_ref, acc_ref)

    def run(masked):
        cap = None
        if masked:
            causal = (k0 + lax.broadcasted_iota(I32, (tk, tq), 0)) <= (q0 + lax.broadcasted_iota(I32, (tk, tq), 1))
            cap = jnp.where(causal, jnp.inf, NEG_INF)
        _fox_tile(lambda h: jnp.dot(k_ref[0, keys, _hs(h)], qt_ref[0, _hs(h), :], preferred_element_type=F32),
                  lambda h, p: jnp.dot(vt_ref[0, _hs(h), keys], p, preferred_element_type=F32),
                  cq_ref[0], ck_ref[0, keys, :], cap, m_ref, l_ref, acc_ref, a_ref)

    fully_visible = k0 + tk - 1 <= q0
    pl.when(jnp.logical_and(ki >= 0, fully_visible))(lambda: run(False))
    pl.when(jnp.logical_and(ki >= 0, jnp.logical_not(fully_visible)))(lambda: run(True))

    @pl.when(step == pl.num_programs(2) - 1)
    def _():
        _finish_heads(o_ref, x_ref, wout_ref, l_ref, acc_ref, D_HEADS)


def _fox_attention_prompt(qt, k, vt, cum_t, cum, x, w_out, tq, tk):
    b, d, t = qt.shape
    nq, nk = t // tq, t // tk
    return pl.pallas_call(
        _fox_prompt_kernel,
        grid=(b, nq, nk),
        in_specs=[pl.BlockSpec((1, d, tq), lambda i, j, kk: (i, 0, j)),
                  pl.BlockSpec((1, t, d), lambda i, j, kk: (i, 0, 0)),
                  pl.BlockSpec((1, d, t), lambda i, j, kk: (i, 0, 0)),
                  pl.BlockSpec((1, D_HEADS, tq), lambda i, j, kk: (i, 0, j)),
                  pl.BlockSpec((1, t, D_HEADS), lambda i, j, kk: (i, 0, 0)),
                  pl.BlockSpec((1, tq, d), lambda i, j, kk: (i, j, 0)),
                  pl.BlockSpec((d, d), lambda i, j, kk: (0, 0))],
        out_specs=pl.BlockSpec((1, tq, d), lambda i, j, kk: (i, j, 0)),
        out_shape=jax.ShapeDtypeStruct((b, t, d), F32),
        scratch_shapes=[pltpu.VMEM((D_HEADS, tq), F32), pltpu.VMEM((D_HEADS, tq), F32),
                        pltpu.VMEM((d, tq), F32), pltpu.VMEM((D_HEADS, tk, tq), F32)],
        compiler_params=_cparams("parallel", "parallel", "arbitrary"),
        name="fox_attention",
    )(qt, k, vt, cum_t, cum, x, w_out)


def _fox_cached_kernel(q_ref, kp_ref, vp_ref, kn_ref, vn_ref, cq_ref, ckp_ref, ckn_ref, o_ref,
                       m_ref, l_ref, acc_ref, *, t_new):
    ki = pl.program_id(1)
    n_past = pl.num_programs(1) - 1
    rows = q_ref.shape[1]
    c1 = (D_HD ** -0.5) * LOG2E

    @pl.when(ki == 0)
    def _():
        m_ref[...] = jnp.full(m_ref.shape, NEG_INF, F32)
        l_ref[...] = jnp.zeros(l_ref.shape, F32)
        acc_ref[...] = jnp.zeros(acc_ref.shape, F32)

    def tile(k2d, v2d, ck_row, causal):
        cols = k2d.shape[0]
        a = _dot_nt(q_ref[0], k2d) * c1 - ck_row * LOG2E
        r = lax.broadcasted_iota(I32, (rows, cols), 0)
        c = lax.broadcasted_iota(I32, (rows, cols), 1)
        ok = (c % D_HEADS) == (r // t_new)
        if causal:
            ok = jnp.logical_and(ok, (c // D_HEADS) <= (r % t_new))
        a = jnp.where(ok, a, NEG_INF)
        cq2 = cq_ref[0] * LOG2E
        m_old = m_ref[...]
        m_new = jnp.maximum(m_old, jnp.max(a, axis=1, keepdims=True) + cq2)
        alpha = jnp.exp2(m_old - m_new)
        p = jnp.exp2(a - (m_new - cq2))
        l_ref[...] = alpha * l_ref[...] + jnp.sum(p, axis=1, keepdims=True)
        acc_ref[...] = alpha * acc_ref[...] + _dot(p, v2d)
        m_ref[...] = m_new

    @pl.when(ki < n_past)
    def _():
        tk = kp_ref.shape[1]
        tile(kp_ref[0].reshape(tk * D_HEADS, D_HD), vp_ref[0].reshape(tk * D_HEADS, D_HD), ckp_ref[0], False)

    @pl.when(ki == n_past)
    def _():
        tile(kn_ref[0].reshape(t_new * D_HEADS, D_HD), vn_ref[0].reshape(t_new * D_HEADS, D_HD), ckn_ref[0], True)
        o_ref[0] = acc_ref[...] / l_ref[...]


def _fox_attention_cached(q_rows, k_past, v_past, k_new, v_new, cq_col, ck_past, ck_new, tk):
    b, rows, hd = q_rows.shape
    past, t_new = k_past.shape[1], k_new.shape[1]
    n_past = past // tk
    pidx = lambda i, kk: (i, jnp.minimum(kk, n_past - 1), 0, 0)
    return pl.pallas_call(
        functools.partial(_fox_cached_kernel, t_new=t_new),
        grid=(b, n_past + 1),
        in_specs=[pl.BlockSpec((1, rows, hd), lambda i, kk: (i, 0, 0)),
                  pl.BlockSpec((1, tk, D_HEADS, D_HD), pidx),
                  pl.BlockSpec((1, tk, D_HEADS, D_HD), pidx),
                  pl.BlockSpec((1, t_new, D_HEADS, D_HD), lambda i, kk: (i, 0, 0, 0)),
                  pl.BlockSpec((1, t_new, D_HEADS, D_HD), lambda i, kk: (i, 0, 0, 0)),
                  pl.BlockSpec((1, rows, 1), lambda i, kk: (i, 0, 0)),
                  pl.BlockSpec((1, 1, tk * D_HEADS), lambda i, kk: (i, 0, jnp.minimum(kk, n_past - 1))),
                  pl.BlockSpec((1, 1, t_new * D_HEADS), lambda i, kk: (i, 0, 0))],
        out_specs=pl.BlockSpec((1, rows, hd), lambda i, kk: (i, 0, 0)),
        out_shape=jax.ShapeDtypeStruct((b, rows, hd), F32),
        scratch_shapes=[pltpu.VMEM((rows, 1), F32), pltpu.VMEM((rows, 1), F32), pltpu.VMEM((rows, hd), F32)],
        compiler_params=_cparams("parallel", "arbitrary"),
        name="fox_attention_cached",
    )(q_rows, k_past, v_past, k_new, v_new, cq_col, ck_past, ck_new)


def _pad_rows(a, rows):
    if a.shape[1] == rows:
        return a
    return jnp.pad(a, ((0, 0), (0, rows - a.shape[1])) + ((0, 0),) * (a.ndim - 2))


def _round_up(n, m):
    return -(-n // m) * m


def _dsa_mixer(x, g, k_past, v_past, ki_past, w, w_out, rel_bias):
    b, t, d = x.shape
    past = k_past.shape[1]
    n_keys = past + t
    if past == 0:
        qt, k4, kb, v4, vt, qit, kw, kwt, ki_t = _proj(
            x, g, w, B_WIDTHS, ((0, "t", BF16, None), (1, B_HD, F32, None), (1, "rows", BF16, None),
                      (2, B_HD, F32, None), (2, "t", BF16, None), (3, "t", BF16, None),
                      (4, "rows", F32, None), (4, "t", F32, None), (4, ("first_t", IDX_DIM), F32, None)),
            jnp.zeros((1, LANE), F32))
        far, near = _bias_tiles(rel_bias, DSA_TILE, DSA_TILE)
        y = _dsa_attention_prompt(qt, qit, kwt, kw, kb, vt, far, near, x, w_out)
        return (y, k4, v4, jnp.swapaxes(ki_t, 1, 2))
    q, k4, v4, qidx, kw = _proj(
        x, g, w, B_WIDTHS, ((0, "rows", BF16, None), (1, B_HD, F32, None), (2, B_HD, F32, None),
                  (3, "rows", BF16, None), (4, "rows", F32, None)),
        jnp.zeros((1, LANE), F32))
    ki = kw[:, :, :IDX_DIM]
    to_rows = lambda a, nh: jnp.swapaxes(a.reshape(b, t, nh, -1), 1, 2).reshape(b, nh * t, -1)
    kidx_all = _pad_rows(jnp.concatenate([ki_past, ki], axis=1), _round_up(n_keys, LANE))
    o = _dsa_attention_cached(to_rows(q, B_HEADS), to_rows(qidx, IDX_HEADS),
                              to_rows(kw[:, :, IDX_DIM:IDX_DIM + IDX_HEADS], IDX_HEADS),
                              jnp.repeat(rel_bias.T, t, axis=0), kidx_all, k_past, v_past, k4, v4)
    o = jnp.swapaxes(o.reshape(b, B_HEADS, t, B_HD), 1, 2).reshape(b * t, d)
    y = _out_proj(x.reshape(b * t, d), o, w_out).reshape(b, t, d)
    return (y, k4, v4, ki)


def _fox_mixer(x, g, k_past, v_past, lf_past, w, b_f, w_out):
    b, t, d = x.shape
    past = k_past.shape[1]
    heads4 = ((1, D_HD, F32, None), (2, D_HD, F32, None))
    if past == 0:
        tq = tk = min(FOX_TILE, t)
        k4, v4, logf_t, qt, kb, vt = _proj(
            x, g, w, D_WIDTHS, heads4 + ((3, ("first_t", D_HEADS), F32, "log_sigmoid"), (0, "t", BF16, None),
                               (1, "rows", BF16, None), (2, "t", BF16, None)), b_f)
        cum_t = _cumsum_lanes(logf_t)
        y = _fox_attention_prompt(qt, kb, vt, cum_t, jnp.swapaxes(cum_t, 1, 2), x, w_out, tq, tk)
        return (y, k4, v4, jnp.swapaxes(logf_t, 1, 2))
    else:
        tk = math.gcd(past, FOX_TILE)
        k4, v4, logf, q = _proj(
            x, g, w, D_WIDTHS, heads4 + ((3, ("first", D_HEADS), F32, "log_sigmoid"), (0, "rows", BF16, None)), b_f)
        lf_all = _pad_rows(jnp.concatenate([lf_past, logf], axis=1), _round_up(past + t, LANE))
        cum = jnp.swapaxes(_cumsum_lanes(jnp.swapaxes(lf_all, 1, 2)), 1, 2)[:, :past + t]
        ck = cum.reshape(b, 1, (past + t) * D_HEADS)
        to_rows = lambda a: jnp.swapaxes(a.reshape(b, t, D_HEADS, -1), 1, 2).reshape(b, D_HEADS * t, -1)
        o = _fox_attention_cached(to_rows(q), k_past, v_past, k4, v4, to_rows(cum[:, past:]),
                                  ck[:, :, :past * D_HEADS], ck[:, :, past * D_HEADS:], tk)
        o = jnp.swapaxes(o.reshape(b, D_HEADS, t, D_HD), 1, 2).reshape(b, t, d)
    y = _out_proj(x.reshape(b * t, d), o.reshape(b * t, d), w_out).reshape(b, t, d)
    return (y, k4, v4, logf)


def _run_group(x, pos0, a_st, b_k, b_v, b_ki, c_st, d_k, d_v, d_lf, mem_k, mem_v, prm):
    b, t, d = x.shape
    depth = prm["norm_mix"].shape[0]
    new = {n: [] for n in ("a", "bk", "bv", "bki", "c", "dk", "dv", "dlf")}
    for i in range(depth):
        kind, j = i % 4, i // 4
        g = prm["norm_mix"][i]
        if kind == 0:
            x, st = _conv_mixer(x, g, prm["a_w_in"][j], prm["a_conv"][j], a_st[j], prm["a_w_out"][j])
            new["a"].append(st)
        elif kind == 1:
            x, kk, vv, ki = _dsa_mixer(x, g, b_k[j], b_v[j], b_ki[j], prm["b_w"][j], prm["b_w_out"][j],
                                       prm["rel_bias"])
            new["bk"].append(kk); new["bv"].append(vv); new["bki"].append(ki)
        elif kind == 2:
            x, st = _pool_mixer(x, g, c_st[j], prm["c_w_group"][j], prm["c_scale"][j], pos0)
            new["c"].append(st)
        else:
            x, kk, vv, lf = _fox_mixer(x, g, d_k[j], d_v[j], d_lf[j], prm["d_w"][j], prm["d_b_f"][j],
                                       prm["d_w_out"][j])
            new["dk"].append(kk); new["dv"].append(vv); new["dlf"].append(lf)
        x = _xattn(x, prm["norm_xattn"], prm["xa_wq"], mem_k, mem_v, prm["xa_wo"], i)
        last = i == depth - 1
        x = _ffn(x.reshape(b * t, d), prm["norm_ffn"], prm["ffn_w1"], prm["ffn_w2"],
                 prm["final_norm"], i, last).reshape(b, t, d)
    return (x,) + tuple(jnp.stack(new[n]) for n in ("a", "bk", "bv", "bki", "c", "dk", "dv", "dlf"))


def kernel(x_prompt, x_sample, state_a_conv, cache_b_k, cache_b_v, cache_b_kidx, state_c_pool,
           cache_d_k, cache_d_v, cache_d_logf, cache_mem_k, cache_mem_v, mem_prompt,
           norm_mix, norm_xattn, norm_mem, norm_ffn, final_norm,
           a_w_in, a_conv, a_w_out, b_w_in, b_w_out, rel_bias, c_w_group, c_scale,
           d_w_in, d_b_f, d_w_out, xa_wq, xa_wkv, xa_wo, ffn_w1, ffn_w2):
    bp, _, d = x_prompt.shape
    n_b, n_d = b_w_in.shape[0], d_w_in.shape[0]
    bf = lambda w: w.astype(BF16)
    d_bf = [jnp.pad(d_b_f[j], (0, LANE - D_HEADS)).reshape(1, LANE) for j in range(n_d)]

    prm = {"norm_mix": norm_mix, "norm_xattn": norm_xattn, "norm_ffn": norm_ffn, "final_norm": final_norm,
           "a_w_in": bf(a_w_in), "a_conv": a_conv, "a_w_out": bf(a_w_out),
           "b_w": b_w_in, "b_w_out": bf(b_w_out), "rel_bias": rel_bias,
           "c_w_group": bf(c_w_group), "c_scale": c_scale,
           "d_w": d_w_in, "d_b_f": d_bf, "d_w_out": bf(d_w_out),
           "xa_wq": xa_wq, "xa_wo": xa_wo, "ffn_w1": ffn_w1, "ffn_w2": ffn_w2}

    mk, mv, mk_rows, mv_rows = _memory_kv(mem_prompt, norm_mem, xa_wkv)

    n_a, n_c = a_w_in.shape[0], c_w_group.shape[0]
    z = lambda *s: jnp.zeros(s, F32)
    gp = _run_group(x_prompt, 0,
                    z(n_a, bp, CONV_W - 1, d),
                    z(n_b, bp, 0, B_KV, B_HD), z(n_b, bp, 0, B_KV, B_HD), z(n_b, bp, 0, IDX_DIM),
                    z(n_c, bp, POOL_STATE, d),
                    z(n_d, bp, 0, D_HEADS, D_HD), z(n_d, bp, 0, D_HEADS, D_HD), z(n_d, bp, 0, D_HEADS),
                    mk_rows, mv_rows, prm)

    past_len = cache_b_k.shape[2]
    gs = _run_group(x_sample, past_len, state_a_conv, cache_b_k, cache_b_v, cache_b_kidx, state_c_pool,
                    cache_d_k, cache_d_v, cache_d_logf, cache_mem_k, cache_mem_v, prm)

    (y_p, a_p, bk_p, bv_p, bki_p, c_p, dk_p, dv_p, dlf_p) = gp
    (y_s, a_s, bk_s, bv_s, bki_s, c_s, dk_s, dv_s, dlf_s) = gs
    return (y_p, y_s, a_p, a_s, bk_p, bv_p, bki_p, bk_s, bv_s, bki_s, c_p, c_s,
            dk_p, dv_p, dlf_p, dk_s, dv_s, dlf_s, mk, mv)
```
